```python
import jax
import jax.numpy as jnp
from jax import lax
import numpy as np

D_MODEL = 1024
BATCH = 8
SEQ = 2048
DEPTH = 1
DEC_BATCH = 32
DEC_SEQ = 4
PAST_LEN = 8192
PAGE_SIZE = 128

M_HEADS = 4
M_DH = D_MODEL // 8
M_WIDTH = M_HEADS * M_DH
M_CHUNK = 64
A_HEADS = 8
A_KV = 2
A_HPG = A_HEADS // A_KV
A_DH = 64
A_WIDTH = A_HEADS * A_DH
CMP_STRIDE = 16
CMP_LEN = 2 * CMP_STRIDE
SEL_LEN = 64
N_SEL = 16
WINDOW = 512
Q_BLOCK = 128
SEL_Q_BLOCK = 64
ROPE_THETA = 500000.0
ROT_DIM = A_DH // 4
ATT_SCALE = A_DH ** -0.5
N_EXPERTS = 32
TOP_K = 4
D_EXPERT = D_MODEL
SWIGLU_LIMIT = 7.0
SWIGLU_ALPHA = 1.702
MOE_BLOCK = 128
EPS = 1e-6
IN_SIZES = (M_WIDTH, M_WIDTH, M_WIDTH, M_WIDTH, M_HEADS, M_HEADS, A_WIDTH, 6 * A_KV * A_DH, 3 * A_HEADS, D_MODEL, D_MODEL)
N_IN = 4 * M_WIDTH + 2 * M_HEADS + A_WIDTH + 6 * A_KV * A_DH + 3 * A_HEADS + 2 * D_MODEL
F_GATE_OFF = 4 * M_WIDTH + M_HEADS

kernel_name = 'hybrid_mlstm_nsa_moe_adaln_step'


def _rmsnorm(x, g):
    xf = x.astype(jnp.float32)
    y = xf * lax.rsqrt(jnp.mean(xf * xf, axis=-1, keepdims=True) + EPS)
    return (y * g.astype(jnp.float32)).astype(x.dtype)


def _partial_rope(x, pos):
    half = ROT_DIM // 2
    inv = ROPE_THETA ** (-jnp.arange(half, dtype=jnp.float32) * (2.0 / ROT_DIM))
    ang = pos.astype(jnp.float32)[:, None] * inv[None, :]
    cos = jnp.cos(ang)[None, :, None, :]
    sin = jnp.sin(ang)[None, :, None, :]
    xf = x.astype(jnp.float32)
    x1, x2 = xf[..., :half], xf[..., half:ROT_DIM]
    out = jnp.concatenate([x1 * cos - x2 * sin, x1 * sin + x2 * cos, xf[..., ROT_DIM:]], axis=-1)
    return out.astype(x.dtype)


def _masked_softmax(s, mask):
    s = jnp.where(mask, s, -jnp.inf)
    m = jnp.max(s, axis=-1, keepdims=True)
    m = jnp.where(jnp.isfinite(m), m, 0.0)
    e = jnp.exp(s - m)
    d = jnp.sum(e, axis=-1, keepdims=True)
    return e / jnp.where(d > 0, d, 1.0)


def _adaln(c, w, b):
    mod = jax.nn.silu(c) @ w + b
    return jnp.split(mod[:, None, :], 6, axis=-1)


def _mixer_inputs(h, pos, p):
    B, T, _ = h.shape
    z = h @ p['w_in'] + p['b_in']
    parts = []
    off = 0
    for size in IN_SIZES:
        parts.append(z[..., off:off + size])
        off += size
    mq, mk, mv, mo, mi, mf, aq, akv, ag, ga, gb = parts
    to_heads = lambda a: jnp.moveaxis(a.reshape(B, T, M_HEADS, M_DH), 2, 1)
    mq, mk, mv = to_heads(mq), to_heads(mk) * (M_DH ** -0.5), to_heads(mv)
    mi, mf = jnp.moveaxis(mi, 2, 1), jnp.moveaxis(mf, 2, 1)
    q = _rmsnorm(aq.reshape(B, T, A_HEADS, A_DH), p['q_norm_g'])
    q_rope = _partial_rope(q, pos)
    kv = akv.reshape(B, T, 6, A_KV, A_DH)
    k_sel = _partial_rope(_rmsnorm(kv[:, :, 2], p['k_norm_g'][1]), pos)
    k_win = _partial_rope(_rmsnorm(kv[:, :, 4], p['k_norm_g'][2]), pos)
    rows = jnp.stack([kv[:, :, 0], kv[:, :, 1], k_sel, kv[:, :, 3]], axis=2)
    win_rows = jnp.stack([k_win, kv[:, :, 5]], axis=2)
    br_gates = jax.nn.sigmoid(ag.reshape(B, T, A_HEADS, 3))
    return (mq, mk, mv, mi, mf, mo), (q, q_rope, rows, win_rows, br_gates), (ga, gb)


def _mlstm_chunkwise(q, k, v, i_pre, f_pre, C0, n0, m0, chunk):
    B, H, T, Dh = q.shape
    nc = T // chunk
    f32 = jnp.float32

    def chunks(a):
        return jnp.moveaxis(a.astype(f32).reshape(B, H, nc, chunk, *a.shape[3:]), 2, 0)

    causal = jnp.tril(jnp.ones((chunk, chunk), bool))

    def step(carry, inp):
        C, n, m = carry
        q_, k_, v_, i_, lf_ = inp
        b = jnp.cumsum(lf_, axis=-1)
        dmat = jnp.where(causal, b[..., :, None] - b[..., None, :] + i_[..., None, :], -jnp.inf)
        inter = b + m[..., None]
        m_row = jnp.maximum(jnp.max(dmat, axis=-1), inter)
        w = jnp.exp(dmat - m_row[..., None])
        w_inter = jnp.exp(inter - m_row)
        s = jnp.einsum('bhtd,bhsd->bhts', q_, k_) * w
        num = jnp.einsum('bhts,bhsd->bhtd', s, v_) + w_inter[..., None] * jnp.einsum('bhvk,bhtk->bhtv', C, q_)
        den = jnp.sum(s, axis=-1) + w_inter * jnp.einsum('bhk,bhtk->bht', n, q_)
        h = num / jnp.maximum(jnp.abs(den), jnp.exp(-m_row))[..., None]
        bL = b[..., -1]
        dec = bL[..., None] - b + i_
        m_new = jnp.maximum(bL + m, jnp.max(dec, axis=-1))
        ws = jnp.exp(dec - m_new[..., None])
        wc = jnp.exp(bL + m - m_new)
        C_new = wc[..., None, None] * C + jnp.einsum('bhs,bhsv,bhsk->bhvk', ws, v_, k_)
        n_new = wc[..., None] * n + jnp.einsum('bhs,bhsk->bhk', ws, k_)
        return (C_new, n_new, m_new), h

    xs = (chunks(q), chunks(k), chunks(v), chunks(i_pre), chunks(jax.nn.log_sigmoid(f_pre.astype(f32))))
    (C, n, m), h = lax.scan(step, (C0.astype(f32), n0.astype(f32), m0.astype(f32)), xs)
    h = jnp.moveaxis(h, 0, 2).reshape(B, H, T, Dh)
    return h, (C.astype(C0.dtype), n.astype(n0.dtype), m.astype(m0.dtype))


def _compress(k_rows, pe, w):
    B, Tk, G, Dh = k_rows.shape
    seg = k_rows.reshape(B, Tk // CMP_STRIDE, CMP_STRIDE, G, Dh)
    blocks = jnp.concatenate([seg[:, :-1], seg[:, 1:]], axis=2)
    return jnp.einsum('bnlgd,lde->bnge', blocks + pe[None, None, :, None, :], w)


def _cmp_attention(q, kc, vc, q_pos):
    B, Tq = q.shape[:2]
    nc = kc.shape[1]
    qg = q.reshape(B, Tq, A_KV, A_HPG, A_DH)
    s = jnp.einsum('btghd,bngd->btghn', qg, kc).astype(jnp.float32) * ATT_SCALE
    end = jnp.arange(nc) * CMP_STRIDE + (CMP_LEN - 1)
    mask = end[None, :] <= q_pos[:, None]
    p = _masked_softmax(s, mask[None, :, None, None, :])
    o = jnp.einsum('btghn,bngd->btghd', p, vc.astype(jnp.float32))
    return o.reshape(B, Tq, A_HEADS, A_DH), p


def _select_blocks(p_cmp, q_pos, nsb):
    B, Tq = p_cmp.shape[:2]
    per = SEL_LEN // CMP_STRIDE
    imp = jnp.sum(p_cmp, axis=3)
    nc = imp.shape[-1]
    imp = jnp.pad(imp, ((0, 0), (0, 0), (0, 0), (0, nsb * per - nc)))
    imp = jnp.sum(imp.reshape(B, Tq, A_KV, nsb, per), axis=-1)
    cur = q_pos // SEL_LEN
    past = jnp.arange(nsb)[None, :] < cur[:, None]
    imp = jnp.where(past[None, :, None, :], imp, -1.0)
    top_v, top_i = lax.top_k(imp, min(N_SEL - 1, nsb))
    cur_b = jnp.broadcast_to(cur[None, :, None, None], (B, Tq, A_KV, 1)).astype(top_i.dtype)
    idx = jnp.concatenate([cur_b, top_i], axis=-1)
    valid = jnp.concatenate([jnp.ones((B, Tq, A_KV, 1), bool), top_v >= 0.0], axis=-1)
    return idx, valid


def _sel_attention(qg, k_blk, v_blk, idx, valid, q_pos):
    B, Tq = qg.shape[:2]
    ns = idx.shape[-1]
    bi = jnp.arange(B)[:, None, None, None]
    gi = jnp.arange(A_KV)[None, None, :, None]
    kg = k_blk[bi, gi, idx]
    vg = v_blk[bi, gi, idx]
    s = jnp.einsum('btghd,btgnjd->btghnj', qg, kg).astype(jnp.float32) * ATT_SCALE
    kpos = idx[..., None] * SEL_LEN + jnp.arange(SEL_LEN)
    mask = valid[..., None] & (kpos <= q_pos[None, :, None, None, None])
    p = _masked_softmax(s.reshape(B, Tq, A_KV, A_HPG, ns * SEL_LEN), mask.reshape(B, Tq, A_KV, 1, ns * SEL_LEN))
    o = jnp.einsum('btghm,btgmd->btghd', p, vg.reshape(B, Tq, A_KV, ns * SEL_LEN, A_DH).astype(jnp.float32))
    return o.reshape(B, Tq, A_HEADS, A_DH)


def _window_prompt(q, win_rows):
    B, T = q.shape[:2]
    nq = T // Q_BLOCK
    span = WINDOW + Q_BLOCK
    kv = jnp.pad(win_rows, ((0, 0), (WINDOW, 0), (0, 0), (0, 0), (0, 0)))
    kidx = jnp.arange(nq)[:, None] * Q_BLOCK + jnp.arange(span)[None, :]
    kvb = kv[:, kidx]
    qb = q.reshape(B, nq, Q_BLOCK, A_KV, A_HPG, A_DH)
    s = jnp.einsum('bcqghd,bckgd->bcghqk', qb, kvb[:, :, :, 0]).astype(jnp.float32) * ATT_SCALE
    qpos = jnp.arange(nq)[:, None] * Q_BLOCK + jnp.arange(Q_BLOCK)[None, :]
    kpos = kidx - WINDOW
    diff = qpos[:, :, None] - kpos[:, None, :]
    mask = (diff >= 0) & (diff < WINDOW) & (kpos[:, None, :] >= 0)
    p = _masked_softmax(s, mask[None, :, None, None])
    o = jnp.einsum('bcghqk,bckgd->bcqghd', p, kvb[:, :, :, 1].astype(jnp.float32))
    return o.reshape(B, T, A_HEADS, A_DH)


def _window_sample(q, win_all, q_pos, k_pos):
    B, Tq = q.shape[:2]
    qg = q.reshape(B, Tq, A_KV, A_HPG, A_DH)
    s = jnp.einsum('btghd,bkgd->btghk', qg, win_all[:, :, 0]).astype(jnp.float32) * ATT_SCALE
    diff = q_pos[:, None] - k_pos[None, :]
    mask = (diff >= 0) & (diff < WINDOW)
    p = _masked_softmax(s, mask[None, :, None, None, :])
    o = jnp.einsum('btghk,bkgd->btghd', p, win_all[:, :, 1].astype(jnp.float32))
    return o.reshape(B, Tq, A_HEADS, A_DH)


def _nsa(q, q_rope, br_gates, rows, q_pos, o_win, p):
    B, Tq = q.shape[:2]
    Tk = rows.shape[1]
    kc = _rmsnorm(_compress(rows[:, :, 0], p['cmp_pe_k'], p['cmp_w_k']), p['k_norm_g'][0])
    vc = _compress(rows[:, :, 1], p['cmp_pe_v'], p['cmp_w_v'])
    o_cmp, p_cmp = _cmp_attention(q, kc, vc, q_pos)
    nsb = Tk // SEL_LEN
    idx, valid = _select_blocks(p_cmp, q_pos, nsb)
    k_blk = jnp.moveaxis(rows[:, :, 2].reshape(B, nsb, SEL_LEN, A_KV, A_DH), 3, 1)
    v_blk = jnp.moveaxis(rows[:, :, 3].reshape(B, nsb, SEL_LEN, A_KV, A_DH), 3, 1)
    qg = q_rope.reshape(B, Tq, A_KV, A_HPG, A_DH)
    blk = SEL_Q_BLOCK if Tq % SEL_Q_BLOCK == 0 else Tq
    nq = Tq // blk
    if nq == 1:
        o_sel = _sel_attention(qg, k_blk, v_blk, idx, valid, q_pos)
    else:
        split = lambda a: jnp.moveaxis(a.reshape(a.shape[0], nq, blk, *a.shape[2:]), 1, 0)
        o_sel = lax.map(lambda a: _sel_attention(a[0], k_blk, v_blk, a[1], a[2], a[3]),
                        (split(qg), split(idx), split(valid), q_pos.reshape(nq, blk)))
        o_sel = jnp.moveaxis(o_sel, 0, 1).reshape(B, Tq, A_HEADS, A_DH)
    g = br_gates.astype(jnp.float32)
    return g[..., 0:1] * o_cmp + g[..., 1:2] * o_sel + g[..., 2:3] * o_win


def _mixer_out(h_m, mo, o_nsa, ga, gb, p):
    B, T = mo.shape[:2]
    hm = jnp.moveaxis(h_m, 1, 2).reshape(B, T, M_WIDTH).astype(mo.dtype)
    ym = (jax.nn.sigmoid(mo) * hm) @ p['w_up_m']
    ya = o_nsa.reshape(B, T, A_WIDTH).astype(mo.dtype) @ p['w_up_a']
    return (jax.nn.sigmoid(ga) * ym + jax.nn.sigmoid(gb) * ya) @ p['w_out']


def _expert(xb, e, w_gu, b_gu, w_dn, b_dn):
    gu = xb @ w_gu[e] + b_gu[e]
    g = jnp.minimum(gu[..., :D_EXPERT], SWIGLU_LIMIT)
    u = jnp.clip(gu[..., D_EXPERT:], -SWIGLU_LIMIT, SWIGLU_LIMIT)
    act = g * jax.nn.sigmoid(SWIGLU_ALPHA * g) * (u + 1.0)
    return act @ w_dn[e] + b_dn[e]


def _moe(x, p):
    M, D = x.shape
    logits = (x @ p['w_router'] + p['b_router']).astype(jnp.float32)
    top_v, top_e = lax.top_k(logits, TOP_K)
    gate = jax.nn.softmax(top_v, axis=-1)
    MK = M * TOP_K
    flat_e = top_e.reshape(MK)
    order = jnp.argsort(flat_e)
    sorted_e = flat_e[order]
    counts = jnp.bincount(flat_e, length=N_EXPERTS)
    padded = (counts + MOE_BLOCK - 1) // MOE_BLOCK * MOE_BLOCK
    pad_end = jnp.cumsum(padded)
    pad_start = pad_end - padded
    start = jnp.cumsum(counts) - counts
    dest = pad_start[sorted_e] + jnp.arange(MK) - start[sorted_e]
    n_blocks = (MK + N_EXPERTS * (MOE_BLOCK - 1) + MOE_BLOCK - 1) // MOE_BLOCK
    src_tok = jnp.full((n_blocks * MOE_BLOCK,), M, jnp.int32).at[dest].set((order // TOP_K).astype(jnp.int32))
    block_e = jnp.minimum(jnp.searchsorted(pad_end, jnp.arange(n_blocks) * MOE_BLOCK, side='right'), N_EXPERTS - 1)
    xp = jnp.concatenate([x, jnp.zeros((1, D), x.dtype)], axis=0)
    xb = xp[src_tok].reshape(n_blocks, MOE_BLOCK, D)
    yb = lax.map(lambda a: _expert(a[0], a[1], p['w_gu'], p['b_gu'], p['w_dn'], p['b_dn']), (xb, block_e))
    y_sorted = yb.reshape(-1, D)[dest]
    y_slots = jnp.zeros((MK, D), yb.dtype).at[order].set(y_sorted)
    return jnp.einsum('mkd,mk->md', y_slots.reshape(M, TOP_K, D), gate.astype(y_slots.dtype))


def _ffn(x, sh, sc, p):
    B, T, D = x.shape
    h = _rmsnorm(x, p['g_ffn']) * (1.0 + sc) + sh
    return _moe(h.reshape(B * T, D), p).reshape(B, T, D)


def _prompt_layer(x, c, p):
    B, T, _ = x.shape
    sh1, sc1, gt1, sh2, sc2, gt2 = _adaln(c, p['w_ada'], p['b_ada'])
    h = _rmsnorm(x, p['g_mix']) * (1.0 + sc1) + sh1
    pos = jnp.arange(T, dtype=jnp.int32)
    (mq, mk, mv, mi, mf, mo), (q, q_rope, rows, win_rows, br_gates), (ga, gb) = _mixer_inputs(h, pos, p)
    C0 = jnp.zeros((B, M_HEADS, M_DH, M_DH), x.dtype)
    n0 = jnp.zeros((B, M_HEADS, M_DH), x.dtype)
    m0 = jnp.zeros((B, M_HEADS), x.dtype)
    h_m, (C, n, m) = _mlstm_chunkwise(mq, mk, mv, mi, mf, C0, n0, m0, min(M_CHUNK, T))
    o_win = _window_prompt(q_rope, win_rows)
    o_nsa = _nsa(q, q_rope, br_gates, rows, pos, o_win, p)
    x = x + gt1 * _mixer_out(h_m, mo, o_nsa, ga, gb, p)
    x = x + gt2 * _ffn(x, sh2, sc2, p)
    win_state = win_rows[:, T - min(WINDOW, T):]
    return x, rows, win_state, C, n, m


def _sample_layer(x, c, cache_kv, page_table, win_buf, C0, n0, m0, p):
    B, T, _ = x.shape
    sh1, sc1, gt1, sh2, sc2, gt2 = _adaln(c, p['w_ada'], p['b_ada'])
    h = _rmsnorm(x, p['g_mix']) * (1.0 + sc1) + sh1
    pos = PAST_LEN + jnp.arange(T, dtype=jnp.int32)
    (mq, mk, mv, mi, mf, mo), (q, q_rope, rows, win_rows, br_gates), (ga, gb) = _mixer_inputs(h, pos, p)
    h_m, (C, n, m) = _mlstm_chunkwise(mq, mk, mv, mi, mf, C0, n0, m0, T)
    n_pages = page_table.shape[1]
    past = cache_kv[page_table].reshape(B, n_pages * PAGE_SIZE, 4, A_KV, A_DH)
    pad = (-T) % SEL_LEN
    all_rows = jnp.concatenate([past, rows.astype(past.dtype), jnp.zeros((B, pad, 4, A_KV, A_DH), past.dtype)], axis=1)
    win_all = jnp.concatenate([win_buf, win_rows.astype(win_buf.dtype)], axis=1)
    wb = win_buf.shape[1]
    k_pos = PAST_LEN - wb + jnp.arange(wb + T, dtype=jnp.int32)
    o_win = _window_sample(q_rope, win_all, pos, k_pos)
    o_nsa = _nsa(q, q_rope, br_gates, all_rows, pos, o_win, p)
    x = x + gt1 * _mixer_out(h_m, mo, o_nsa, ga, gb, p)
    x = x + gt2 * _ffn(x, sh2, sc2, p)
    return x, rows, win_all[:, T:], C, n, m


def setup_inputs(seed: int = 0) -> dict:
    key = jax.random.key(seed)
    ks = jax.random.split(key, 32)
    f32 = jnp.float32
    nrm = lambda k, shape, scale: jax.random.normal(k, shape, f32) * scale
    n_pages = PAST_LEN // PAGE_SIZE
    n_used = DEC_BATCH * n_pages
    n_phys = n_used + n_used // 4
    win_buf = min(WINDOW, PAST_LEN)
    page_table = jax.random.permutation(ks[0], n_phys)[:n_used].reshape(DEC_BATCH, n_pages).astype(jnp.int32)
    b_in = nrm(ks[14], (DEPTH, N_IN), 0.02)
    b_in = b_in.at[:, F_GATE_OFF:F_GATE_OFF + M_HEADS].add(3.0)
    return {
        'x_prompt': nrm(ks[1], (BATCH, SEQ, D_MODEL), 1.0),
        'x_sample': nrm(ks[2], (DEC_BATCH, DEC_SEQ, D_MODEL), 1.0),
        'cache_nsa_kv': nrm(ks[3], (DEPTH, n_phys, PAGE_SIZE, 4, A_KV, A_DH), 1.0),
        'state_win_kv': nrm(ks[4], (DEPTH, DEC_BATCH, win_buf, 2, A_KV, A_DH), 1.0),
        'state_mlstm_C': nrm(ks[5], (DEPTH, DEC_BATCH, M_HEADS, M_DH, M_DH), 0.1),
        'state_mlstm_n': nrm(ks[6], (DEPTH, DEC_BATCH, M_HEADS, M_DH), 0.5),
        'state_mlstm_m': nrm(ks[7], (DEPTH, DEC_BATCH, M_HEADS), 0.5),
        'page_table': page_table,
        'c_prompt': nrm(ks[8], (BATCH, D_MODEL), 1.0),
        'c_sample': nrm(ks[9], (DEC_BATCH, D_MODEL), 1.0),
        'w_ada': nrm(ks[10], (DEPTH, D_MODEL, 6 * D_MODEL), 0.5 * D_MODEL ** -0.5),
        'b_ada': nrm(ks[11], (DEPTH, 6 * D_MODEL), 0.02),
        'g_mix': 1.0 + nrm(ks[12], (DEPTH, D_MODEL), 0.02),
        'g_ffn': 1.0 + nrm(ks[13], (DEPTH, D_MODEL), 0.02),
        'w_in': nrm(ks[15], (DEPTH, D_MODEL, N_IN), D_MODEL ** -0.5),
        'b_in': b_in,
        'q_norm_g': 1.0 + nrm(ks[16], (DEPTH, A_DH), 0.02),
        'k_norm_g': 1.0 + nrm(ks[17], (DEPTH, 3, A_DH), 0.02),
        'cmp_pe_k': nrm(ks[18], (DEPTH, CMP_LEN, A_DH), 0.02),
        'cmp_pe_v': nrm(ks[19], (DEPTH, CMP_LEN, A_DH), 0.02),
        'cmp_w_k': nrm(ks[20], (DEPTH, CMP_LEN, A_DH, A_DH), (CMP_LEN * A_DH) ** -0.5),
        'cmp_w_v': nrm(ks[21], (DEPTH, CMP_LEN, A_DH, A_DH), (CMP_LEN * A_DH) ** -0.5),
        'w_up_m': nrm(ks[22], (DEPTH, M_WIDTH, D_MODEL), M_WIDTH ** -0.5),
        'w_up_a': nrm(ks[23], (DEPTH, A_WIDTH, D_MODEL), A_WIDTH ** -0.5),
        'w_out': nrm(ks[24], (DEPTH, D_MODEL, D_MODEL), D_MODEL ** -0.5),
        'w_router': nrm(ks[25], (DEPTH, D_MODEL, N_EXPERTS), D_MODEL ** -0.5),
        'b_router': nrm(ks[26], (DEPTH, N_EXPERTS), 0.01),
        'w_gu': nrm(ks[27], (DEPTH, N_EXPERTS, D_MODEL, 2 * D_EXPERT), D_MODEL ** -0.5),
        'b_gu': nrm(ks[28], (DEPTH, N_EXPERTS, 2 * D_EXPERT), 0.01),
        'w_dn': nrm(ks[29], (DEPTH, N_EXPERTS, D_EXPERT, D_MODEL), D_EXPERT ** -0.5),
        'b_dn': nrm(ks[30], (DEPTH, N_EXPERTS, D_MODEL), 0.01),
    }


def reference(x_prompt, x_sample, cache_nsa_kv, state_win_kv, state_mlstm_C, state_mlstm_n, state_mlstm_m,
              page_table, c_prompt, c_sample, w_ada, b_ada, g_mix, g_ffn, w_in, b_in, q_norm_g, k_norm_g,
              cmp_pe_k, cmp_pe_v, cmp_w_k, cmp_w_v, w_up_m, w_up_a, w_out, w_router, b_router,
              w_gu, b_gu, w_dn, b_dn):
    y_p, y_s = x_prompt, x_sample
    kv_p, kv_s, win_p, win_s = [], [], [], []
    Cp, np_, mp, Cs, ns, ms = [], [], [], [], [], []
    for l in range(DEPTH):
        p = dict(w_ada=w_ada[l], b_ada=b_ada[l], g_mix=g_mix[l], g_ffn=g_ffn[l], w_in=w_in[l], b_in=b_in[l],
                 q_norm_g=q_norm_g[l], k_norm_g=k_norm_g[l], cmp_pe_k=cmp_pe_k[l], cmp_pe_v=cmp_pe_v[l],
                 cmp_w_k=cmp_w_k[l], cmp_w_v=cmp_w_v[l], w_up_m=w_up_m[l], w_up_a=w_up_a[l], w_out=w_out[l],
                 w_router=w_router[l], b_router=b_router[l], w_gu=w_gu[l], b_gu=b_gu[l], w_dn=w_dn[l], b_dn=b_dn[l])
        y_p, r_p, w_p, C_p, n_p, m_p = _prompt_layer(y_p, c_prompt, p)
        y_s, r_s, w_s, C_s, n_s, m_s = _sample_layer(y_s, c_sample, cache_nsa_kv[l], page_table, state_win_kv[l],
                                                     state_mlstm_C[l], state_mlstm_n[l], state_mlstm_m[l], p)
        kv_p.append(r_p); kv_s.append(r_s); win_p.append(w_p); win_s.append(w_s)
        Cp.append(C_p); np_.append(n_p); mp.append(m_p); Cs.append(C_s); ns.append(n_s); ms.append(m_s)
    kv_prompt, kv_sample = jnp.stack(kv_p), jnp.stack(kv_s)
    win_prompt, win_sample = jnp.stack(win_p), jnp.stack(win_s)
    C_prompt, n_prompt, m_prompt = jnp.stack(Cp), jnp.stack(np_), jnp.stack(mp)
    C_sample, n_sample, m_sample = jnp.stack(Cs), jnp.stack(ns), jnp.stack(ms)
    return (y_p, y_s, kv_prompt, kv_sample, win_prompt, win_sample,
            C_prompt, n_prompt, m_prompt, C_sample, n_sample, m_sample)
```

```python
import functools
import math

import jax
import jax.numpy as jnp
from jax import lax
from jax.experimental import pallas as pl
from jax.experimental.pallas import tpu as pltpu

F32 = jnp.float32
BF16 = jnp.bfloat16

D_MODEL = 1024
M_HEADS = 4
M_DH = 128
M_WIDTH = M_HEADS * M_DH
A_HEADS = 8
A_KV = 2
A_HPG = A_HEADS // A_KV
A_DH = 64
A_WIDTH = A_HEADS * A_DH
CMP_STRIDE = 16
CMP_LEN = 32
SEL_LEN = 64
N_SEL = 16
WINDOW = 512
PAGE_SIZE = 128
ROPE_THETA = 500000.0
ROT_DIM = A_DH // 4
ATT_SCALE = A_DH ** -0.5
N_EXPERTS = 32
TOP_K = 4
D_EXPERT = D_MODEL
SWIGLU_LIMIT = 7.0
SWIGLU_ALPHA = 1.702
EPS = 1e-6

OFF_MQ, OFF_MK, OFF_MV, OFF_MO = 0, M_WIDTH, 2 * M_WIDTH, 3 * M_WIDTH
OFF_MI = 4 * M_WIDTH
OFF_MF = OFF_MI + M_HEADS
OFF_AQ = OFF_MF + M_HEADS
OFF_AKV = OFF_AQ + A_WIDTH
OFF_AG = OFF_AKV + 6 * A_KV * A_DH
OFF_GA = OFF_AG + 3 * A_HEADS
OFF_GB = OFF_GA + D_MODEL
N_IN = OFF_GB + D_MODEL

LANES = 128
SUBLANES = 8
VMEM_LIMIT = 56 * 1024 * 1024

NEG_BIG = -1e30
SAMPLE_PAD_T = 8


def _cparams(sem):
    return pltpu.CompilerParams(dimension_semantics=sem, vmem_limit_bytes=VMEM_LIMIT)


def _bdot(a, b):
    return jnp.dot(a.astype(BF16), b.astype(BF16), preferred_element_type=F32)


def _bdot_t(a, b):
    return lax.dot_general(a.astype(BF16), b.astype(BF16), (((1,), (1,)), ((), ())),
                           preferred_element_type=F32)


def _split(a):
    hi = a.astype(BF16)
    lo = (a - hi.astype(F32)).astype(BF16)
    return hi, lo


def _dot3(a, b):
    ah, al = _split(a)
    bh, bl = _split(b)
    return (jnp.dot(ah, bh, preferred_element_type=F32) + jnp.dot(al, bh, preferred_element_type=F32)
            + jnp.dot(ah, bl, preferred_element_type=F32))


def _dot2_exact_rhs(a, b_bf16):
    ah, al = _split(a)
    return jnp.dot(ah, b_bf16, preferred_element_type=F32) + jnp.dot(al, b_bf16, preferred_element_type=F32)


def _sigmoid(x):
    return 1.0 / (1.0 + jnp.exp(-x))


def _rmsnorm_rows(x, g):
    return x * lax.rsqrt(jnp.mean(x * x, axis=-1, keepdims=True) + EPS) * g


def _adaln_kernel(c_ref, w_ref, b_ref, o_ref):
    c = c_ref[...]
    s = c * _sigmoid(c)
    o_ref[...] = _dot3(s, w_ref[...]) + b_ref[...]


def _adaln(c, w, b):
    mc, d = c.shape
    n = w.shape[1]
    tn = 1024
    return pl.pallas_call(
        _adaln_kernel,
        grid=(n // tn,),
        in_specs=[pl.BlockSpec((mc, d), lambda j: (0, 0)),
                  pl.BlockSpec((d, tn), lambda j: (0, j)),
                  pl.BlockSpec((1, tn), lambda j: (0, j))],
        out_specs=pl.BlockSpec((mc, tn), lambda j: (0, j)),
        out_shape=jax.ShapeDtypeStruct((mc, n), F32),
        compiler_params=_cparams(("parallel",)),
        name="adaln",
    )(c, w, b.reshape(1, n))


def _head_norm(z, bd, gain):
    ms = _dot2_exact_rhs(z * z, bd)
    return z * lax.rsqrt(ms + EPS) * gain


def _rope(z, cos, s_prev, s_next):
    w = z.shape[1]
    rep = w // LANES
    if rep > 1:
        cos = jnp.concatenate([cos] * rep, axis=1)
        s_prev = jnp.concatenate([s_prev] * rep, axis=1)
        s_next = jnp.concatenate([s_next] * rep, axis=1)
    z_prev = pltpu.roll(z, ROT_DIM // 2, 1)
    z_next = pltpu.roll(z, w - ROT_DIM // 2, 1)
    return z * cos + z_prev * s_prev + z_next * s_next


def _inproj_kernel(x_ref, mod_ref, gmix_ref, cos_ref, sp_ref, sn_ref,
                   wm_ref, bm_ref, wq_ref, bq_ref, wkv_ref, bkv_ref, ws_ref, bs_ref,
                   qg_ref, kg_ref, bd_ref,
                   mq_ref, mk_ref, mv_ref, q_ref, qr_ref, rows_ref, win_ref, small_ref):
    x = x_ref[...]
    sh1 = mod_ref[:, 0:D_MODEL]
    sc1 = mod_ref[:, D_MODEL:2 * D_MODEL]
    h = _rmsnorm_rows(x, gmix_ref[...]) * (1.0 + sc1) + sh1
    hb = h.astype(BF16)

    mq_ref[...] = jnp.dot(hb, wm_ref[:, 0:M_WIDTH], preferred_element_type=F32) + bm_ref[:, 0:M_WIDTH]
    mk = jnp.dot(hb, wm_ref[:, M_WIDTH:2 * M_WIDTH], preferred_element_type=F32) + bm_ref[:, M_WIDTH:2 * M_WIDTH]
    mk_ref[...] = mk * (M_DH ** -0.5)
    mv_ref[...] = (jnp.dot(hb, wm_ref[:, 2 * M_WIDTH:3 * M_WIDTH], preferred_element_type=F32)
                   + bm_ref[:, 2 * M_WIDTH:3 * M_WIDTH])

    cos, sp, sn = cos_ref[...], sp_ref[...], sn_ref[...]
    zq = jnp.dot(hb, wq_ref[...], preferred_element_type=F32) + bq_ref[...]
    qn = _head_norm(zq, bd_ref[...], qg_ref[...])
    q_ref[...] = qn
    qr_ref[...] = _rope(qn, cos, sp, sn)

    zkv = jnp.dot(hb, wkv_ref[...], preferred_element_type=F32) + bkv_ref[...]
    bd2 = bd_ref[0:LANES, 0:LANES]
    rows_ref[:, 0:2 * LANES] = zkv[:, 0:2 * LANES]
    ksel = _head_norm(zkv[:, 2 * LANES:3 * LANES], bd2, kg_ref[0:1, :])
    rows_ref[:, 2 * LANES:3 * LANES] = _rope(ksel, cos, sp, sn)
    rows_ref[:, 3 * LANES:4 * LANES] = zkv[:, 3 * LANES:4 * LANES]
    kwin = _head_norm(zkv[:, 4 * LANES:5 * LANES], bd2, kg_ref[1:2, :])
    win_ref[:, 0:LANES] = _rope(kwin, cos, sp, sn)
    win_ref[:, LANES:2 * LANES] = zkv[:, 5 * LANES:6 * LANES]

    small_ref[...] = _dot3(h, ws_ref[...]) + bs_ref[...]


def _inproj(x2, mod3, gmix, tabs, wts, tm, tiles_per_mod, pos_tiles):
    m = x2.shape[0]
    cos_t, sp_t, sn_t = tabs
    (wm, bm, wq, bq, wkv, bkv, ws, bs, qg, kg, bd) = wts
    r = mod3.shape[1]
    row = lambda i: (i, 0)
    const = lambda i: (0, 0)
    tab = lambda i: (i % pos_tiles, 0)
    in_specs = [
        pl.BlockSpec((tm, D_MODEL), row),
        pl.BlockSpec((None, r, 6 * D_MODEL), lambda i: (i // tiles_per_mod, 0, 0)),
        pl.BlockSpec((1, D_MODEL), const),
        pl.BlockSpec((tm, LANES), tab), pl.BlockSpec((tm, LANES), tab), pl.BlockSpec((tm, LANES), tab),
        pl.BlockSpec(wm.shape, const), pl.BlockSpec(bm.shape, const),
        pl.BlockSpec(wq.shape, const), pl.BlockSpec(bq.shape, const),
        pl.BlockSpec(wkv.shape, const), pl.BlockSpec(bkv.shape, const),
        pl.BlockSpec(ws.shape, const), pl.BlockSpec(bs.shape, const),
        pl.BlockSpec(qg.shape, const), pl.BlockSpec(kg.shape, const), pl.BlockSpec(bd.shape, const),
    ]
    widths = (M_WIDTH, M_WIDTH, M_WIDTH, A_WIDTH, A_WIDTH, 4 * LANES, 2 * LANES, LANES)
    return pl.pallas_call(
        _inproj_kernel,
        grid=(m // tm,),
        in_specs=in_specs,
        out_specs=[pl.BlockSpec((tm, w), row) for w in widths],
        out_shape=[jax.ShapeDtypeStruct((m, w), F32) for w in widths],
        compiler_params=_cparams(("parallel",)),
        name="inproj",
    )(x2, mod3, gmix, cos_t, sp_t, sn_t, wm, bm, wq, bq, wkv, bkv, ws, bs, qg, kg, bd)


def _log_sigmoid(x):
    return jnp.minimum(x, 0.0) - jnp.log(1.0 + jnp.exp(-jnp.abs(x)))


def _mlstm_kernel(*refs, L, t_valid, has_state):
    if has_state:
        q_ref, k_ref, v_ref, s_ref, c0_ref, n0_ref, m0_ref, h_ref, c_ref, n_ref, m_ref = refs
    else:
        q_ref, k_ref, v_ref, s_ref, h_ref, c_ref, n_ref, m_ref = refs
    c = pl.program_id(1)

    @pl.when(c == 0)
    def _():
        if has_state:
            c_ref[...] = c0_ref[...]
            n_ref[...] = n0_ref[...]
            m_ref[...] = m0_ref[...]
        else:
            c_ref[...] = jnp.zeros(c_ref.shape, F32)
            n_ref[...] = jnp.zeros(n_ref.shape, F32)
            m_ref[...] = jnp.zeros(m_ref.shape, F32)

    row = lax.broadcasted_iota(jnp.int32, (L, L), 0)
    col = lax.broadcasted_iota(jnp.int32, (L, L), 1)
    causal = col <= row
    eye = col == row
    tok_col = c * L + lax.broadcasted_iota(jnp.int32, (L, 1), 0)
    valid_col = tok_col < t_valid
    for hd in range(M_HEADS):
        lo, hi = hd * M_DH, (hd + 1) * M_DH
        q = q_ref[:, lo:hi]
        k = k_ref[:, lo:hi]
        v = v_ref[:, lo:hi]
        i_col = s_ref[:, hd:hd + 1]
        lf_col = _log_sigmoid(s_ref[:, M_HEADS + hd:M_HEADS + hd + 1])
        lf_col = jnp.where(valid_col, lf_col, 0.0)
        i_col = jnp.where(valid_col, i_col, -jnp.inf)
        i_row = jnp.sum(jnp.where(eye, i_col, 0.0), axis=0, keepdims=True)
        lf_row = jnp.sum(jnp.where(eye, lf_col, 0.0), axis=0, keepdims=True)
        b_col = jnp.sum(jnp.where(causal, lf_row, 0.0), axis=1, keepdims=True)
        b_row = jnp.sum(jnp.where(row <= col, lf_col, 0.0), axis=0, keepdims=True)
        m_prev = m_ref[:, hd:hd + 1]
        dmat = jnp.where(causal, b_col - b_row + i_row, -jnp.inf)
        inter = b_col + m_prev
        m_row = jnp.maximum(jnp.max(dmat, axis=1, keepdims=True), inter)
        w = jnp.exp(dmat - m_row)
        w_inter = jnp.exp(inter - m_row)
        s = _bdot_t(q, k) * w
        cm = c_ref[hd]
        nv = n_ref[hd]
        num = _bdot(s, v) + w_inter * _bdot_t(q, cm)
        den = jnp.sum(s, axis=1, keepdims=True) + w_inter * jnp.sum(q * nv, axis=1, keepdims=True)
        h_ref[:, lo:hi] = num / jnp.maximum(jnp.abs(den), jnp.exp(-m_row))
        b_last = b_col[L - 1:L, :]
        dec_col = b_last - b_col + i_col
        dec_row = b_last - b_row + i_row
        m_new = jnp.maximum(b_last + m_prev, jnp.max(dec_row, axis=1, keepdims=True))
        ws_col = jnp.exp(dec_col - m_new)
        wc = jnp.exp(b_last + m_prev - m_new)
        vw = (v * ws_col).astype(BF16)
        upd = lax.dot_general(vw, k.astype(BF16), (((0,), (0,)), ((), ())), preferred_element_type=F32)
        c_ref[hd] = wc * cm + upd
        n_ref[hd] = wc * nv + jnp.sum(k * ws_col, axis=0, keepdims=True)
        m_ref[:, hd:hd + 1] = m_new


def _mlstm(mq, mk, mv, small, nb, t_pad, t_valid, L, state=None):
    nc = t_pad // L
    has_state = state is not None
    blk = lambda b, c: (b * nc + c, 0)
    st4 = lambda b, c: (b, 0, 0, 0)
    st3 = lambda b, c: (b, 0, 0)
    in_specs = [pl.BlockSpec((L, M_WIDTH), blk)] * 3 + [pl.BlockSpec((L, LANES), blk)]
    args = [mq, mk, mv, small]
    if has_state:
        c0, n0, m0 = state
        in_specs += [pl.BlockSpec((None, M_HEADS, M_DH, M_DH), st4),
                     pl.BlockSpec((None, M_HEADS, 1, M_DH), st4),
                     pl.BlockSpec((None, 1, M_HEADS), st3)]
        args += [c0, n0.reshape(nb, M_HEADS, 1, M_DH), m0.reshape(nb, 1, M_HEADS)]
    out_specs = [pl.BlockSpec((L, M_WIDTH), blk),
                 pl.BlockSpec((None, M_HEADS, M_DH, M_DH), st4),
                 pl.BlockSpec((None, M_HEADS, 1, M_DH), st4),
                 pl.BlockSpec((None, 1, M_HEADS), st3)]
    out_shape = [jax.ShapeDtypeStruct((nb * t_pad, M_WIDTH), F32),
                 jax.ShapeDtypeStruct((nb, M_HEADS, M_DH, M_DH), F32),
                 jax.ShapeDtypeStruct((nb, M_HEADS, 1, M_DH), F32),
                 jax.ShapeDtypeStruct((nb, 1, M_HEADS), F32)]
    h, cs, ns, ms = pl.pallas_call(
        functools.partial(_mlstm_kernel, L=L, t_valid=t_valid, has_state=has_state),
        grid=(nb, nc),
        in_specs=in_specs,
        out_specs=out_specs,
        out_shape=out_shape,
        compiler_params=_cparams(("parallel", "arbitrary")),
        name="mlstm",
    )(*args)
    return h, cs, ns.reshape(nb, M_HEADS, M_DH), ms.reshape(nb, M_HEADS)


def _stack_heads(qt, g):
    t = qt.shape[0]
    z = jnp.zeros((t, A_DH), F32)
    parts = []
    for hh in range(A_HPG):
        hd = g * A_HPG + hh
        qh = qt[:, hd * A_DH:(hd + 1) * A_DH] * ATT_SCALE
        parts.append(jnp.concatenate([qh, z], axis=1) if g == 0 else jnp.concatenate([z, qh], axis=1))
    return jnp.concatenate(parts, axis=0).astype(BF16)


def _gate_cols(small, g, br):
    cols = []
    for hh in range(A_HPG):
        c0 = 2 * M_HEADS + (g * A_HPG + hh) * 3 + br
        cols.append(_sigmoid(small[:, c0:c0 + 1]))
    return jnp.concatenate(cols, axis=0)


def _compress(k_ref, v_ref, nseg, wbd_ref, pe_ref, kg0):
    acc_lo = jnp.zeros((nseg, 2 * LANES), F32)
    acc_hi = jnp.zeros((nseg, 2 * LANES), F32)
    for l in range(CMP_STRIDE):
        xl = jnp.concatenate([k_ref[pl.ds(l, nseg, stride=CMP_STRIDE), :],
                              v_ref[pl.ds(l, nseg, stride=CMP_STRIDE), :]], axis=1)
        acc_lo = acc_lo + _bdot(xl + pe_ref[l], wbd_ref[l])
        acc_hi = acc_hi + _bdot(xl + pe_ref[CMP_STRIDE + l], wbd_ref[CMP_STRIDE + l])
    kv = acc_lo + pltpu.roll(acc_hi, nseg - 1, 0)
    kc = kv[:, 0:LANES]
    vc = kv[:, LANES:2 * LANES]
    lane = lax.broadcasted_iota(jnp.int32, (nseg, LANES), 1)
    sq = kc * kc
    ms0 = jnp.sum(jnp.where(lane < A_DH, sq, 0.0), axis=1, keepdims=True) * (1.0 / A_DH)
    ms1 = jnp.sum(jnp.where(lane >= A_DH, sq, 0.0), axis=1, keepdims=True) * (1.0 / A_DH)
    ms = jnp.where(lane < A_DH, ms0, ms1)
    kc = kc * lax.rsqrt(ms + EPS) * kg0
    return kc, vc


def _cmp_branch(qn_g, kc_b, vc_b, tpos_rows, nseg, n_tok):
    s = _bdot_t(qn_g, kc_b)
    nidx = lax.broadcasted_iota(jnp.int32, (1, nseg), 1)
    vis = (nidx * CMP_STRIDE + (CMP_LEN - 1)) <= tpos_rows
    sm = jnp.where(vis, s, NEG_BIG)
    mx = jnp.max(sm, axis=1, keepdims=True)
    e = jnp.where(vis, jnp.exp(sm - mx), 0.0)
    d = jnp.sum(e, axis=1, keepdims=True)
    p = e / jnp.where(d > 0, d, 1.0)
    o = _bdot(p, vc_b)
    imp = p[0:n_tok]
    for hh in range(1, A_HPG):
        imp = imp + p[hh * n_tok:(hh + 1) * n_tok]
    return o, imp


def _masked_attn_direct(q_g, k_parts, v_parts, allowed_parts):
    ss = [jnp.where(al, _bdot_t(q_g, kk), NEG_BIG) for kk, al in zip(k_parts, allowed_parts)]
    mx = ss[0].max(axis=1, keepdims=True)
    for s in ss[1:]:
        mx = jnp.maximum(mx, s.max(axis=1, keepdims=True))
    num = None
    den = None
    for s, al, vv in zip(ss, allowed_parts, v_parts):
        e = jnp.where(al, jnp.exp(s - mx), 0.0)
        dd = jnp.sum(e, axis=1, keepdims=True)
        oo = _bdot(e, vv)
        num = oo if num is None else num + oo
        den = dd if den is None else den + dd
    return num / jnp.where(den > 0, den, 1.0)


def _assemble_heads(o_groups, n_tok):
    pieces = []
    for g in range(A_KV):
        for hh in range(A_HPG):
            pieces.append(o_groups[g][hh * n_tok:(hh + 1) * n_tok, g * A_DH:(g + 1) * A_DH])
    return jnp.concatenate(pieces, axis=1)


def _flash(q_g, k_ref, v_ref, kcol, vcol, lo, hi, kc_len, mask_fn, m_sc, l_sc, acc_sc):
    r = q_g.shape[0]
    m_sc[...] = jnp.full(m_sc.shape, NEG_BIG, F32)
    l_sc[...] = jnp.zeros(l_sc.shape, F32)
    acc_sc[...] = jnp.zeros(acc_sc.shape, F32)

    def body(c, carry):
        k0 = pl.multiple_of(c * kc_len, kc_len)
        kb = k_ref[pl.ds(k0, kc_len), kcol:kcol + LANES]
        vb = v_ref[pl.ds(k0, kc_len), vcol:vcol + LANES]
        s = _bdot_t(q_g, kb)
        al = jnp.concatenate([mask_fn(k0)] * A_HPG, axis=0) > 0.5
        sm = jnp.where(al, s, NEG_BIG)
        m_prev = m_sc[:, 0:1]
        m_new = jnp.maximum(m_prev, jnp.max(sm, axis=1, keepdims=True))
        alpha = jnp.exp(m_prev - m_new)
        p = jnp.where(al, jnp.exp(sm - m_new), 0.0)
        l_new = alpha * l_sc[:, 0:1] + jnp.sum(p, axis=1, keepdims=True)
        acc_sc[...] = alpha * acc_sc[...] + _bdot(p, vb)
        m_sc[...] = jnp.broadcast_to(m_new, (r, LANES))
        l_sc[...] = jnp.broadcast_to(l_new, (r, LANES))
        return carry

    lax.fori_loop(lo, hi, body, 0)
    l = l_sc[:, 0:1]
    return acc_sc[...] / jnp.where(l > 0, l, 1.0)


def _nsa_prompt_kernel(q_ref, qr_ref, small_ref, rows_ref, win_ref, wbd_ref, pe_ref, kg0_ref,
                       pool_ref, o_ref,
                       kraw_sc, vraw_sc, kc_sc, vc_sc, m_sc, l_sc, acc_sc, *, T, tq, kc_len):
    qi = pl.program_id(1)
    nseg = T // CMP_STRIDE
    nsb = T // SEL_LEN

    @pl.when(qi == 0)
    def _():
        kraw_sc[...] = rows_ref[:, 0:LANES]
        vraw_sc[...] = rows_ref[:, LANES:2 * LANES]
        kc, vc = _compress(kraw_sc, vraw_sc, nseg, wbd_ref, pe_ref, kg0_ref[...])
        kc_sc[...] = kc
        vc_sc[...] = vc

    t0 = qi * tq
    tpos_col = t0 + lax.broadcasted_iota(jnp.int32, (tq, 1), 0)
    tpos_rows = jnp.concatenate([tpos_col] * A_HPG, axis=0)
    tpos_lane = t0 + lax.broadcasted_iota(jnp.int32, (1, tq), 1)
    q = q_ref[...]
    qr = qr_ref[...]
    small = small_ref[...]
    kc_b = kc_sc[...].astype(BF16)
    vc_b = vc_sc[...].astype(BF16)
    bidx = lax.broadcasted_iota(jnp.int32, (nsb, tq), 0)
    cur = tpos_lane // SEL_LEN
    o_groups = []
    for g in range(A_KV):
        qn_g = _stack_heads(q, g)
        qr_g = _stack_heads(qr, g)
        o_cmp, imp = _cmp_branch(qn_g, kc_b, vc_b, tpos_rows, nseg, tq)
        imp_sel = _dot2_exact_rhs(imp, pool_ref[...])
        imp_t = jnp.transpose(imp_sel)[0:nsb, :]
        val = jnp.where(bidx < cur, imp_t, -1.0)
        rank = jnp.zeros((nsb, tq), F32)
        for bp in range(nsb):
            vb = val[bp:bp + 1, :]
            ahead = jnp.where(vb > val, 1.0, jnp.where((vb == val) & (bidx > bp), 1.0, 0.0))
            rank = rank + ahead
        sel_t = jnp.where(((rank < (N_SEL - 1)) & (bidx < cur)) | (bidx == cur), 1.0, 0.0)
        if nsb < LANES:
            sel_t = jnp.concatenate([sel_t, jnp.zeros((LANES - nsb, tq), F32)], axis=0)
        sel_b = jnp.transpose(sel_t).astype(BF16)

        def sel_mask(k0):
            kblk = (k0 + lax.broadcasted_iota(jnp.int32, (LANES, kc_len), 1)) // SEL_LEN
            expand = jnp.where(kblk == lax.broadcasted_iota(jnp.int32, (LANES, kc_len), 0), 1.0, 0.0)
            mk = jnp.dot(sel_b, expand.astype(BF16), preferred_element_type=F32)
            kpos = k0 + lax.broadcasted_iota(jnp.int32, (1, kc_len), 1)
            return jnp.where(kpos <= tpos_col, mk, 0.0)

        n_sel_chunks = (t0 + tq + kc_len - 1) // kc_len
        o_sel = _flash(qr_g, rows_ref, rows_ref, 2 * LANES, 3 * LANES, 0, n_sel_chunks, kc_len,
                       sel_mask, m_sc, l_sc, acc_sc)

        def win_mask(k0):
            kpos = k0 + lax.broadcasted_iota(jnp.int32, (1, tq), 1)
            diff = tpos_col - kpos
            return jnp.where((diff >= 0) & (diff < WINDOW), 1.0, 0.0)

        w_lo = jnp.maximum(qi - WINDOW // tq, 0)
        o_win = _flash(qr_g, win_ref, win_ref, 0, LANES, w_lo, qi + 1, tq,
                       win_mask, m_sc, l_sc, acc_sc)
        o_groups.append(_gate_cols(small, g, 0) * o_cmp + _gate_cols(small, g, 1) * o_sel
                        + _gate_cols(small, g, 2) * o_win)
    o_ref[...] = _assemble_heads(o_groups, tq)


def _nsa_prompt(q, qr, small, rows, win, wbd, pe, kg0, nb, T):
    tq = 128
    kc_len = 256 if T % 256 == 0 else 128
    nq = T // tq
    nseg = T // CMP_STRIDE
    nsb = T // SEL_LEN
    pool = (jnp.arange(nseg)[:, None] // (SEL_LEN // CMP_STRIDE) == jnp.arange(LANES)[None, :]).astype(BF16)
    tile = lambda b, i: (b * nq + i, 0)
    per_b = lambda b, i: (b, 0)
    c2 = lambda b, i: (0, 0)
    c3 = lambda b, i: (0, 0, 0)
    r4 = A_HPG * tq
    return pl.pallas_call(
        functools.partial(_nsa_prompt_kernel, T=T, tq=tq, kc_len=kc_len),
        grid=(nb, nq),
        in_specs=[pl.BlockSpec((tq, A_WIDTH), tile), pl.BlockSpec((tq, A_WIDTH), tile),
                  pl.BlockSpec((tq, LANES), tile),
                  pl.BlockSpec((T, 4 * LANES), per_b), pl.BlockSpec((T, 2 * LANES), per_b),
                  pl.BlockSpec(wbd.shape, c3), pl.BlockSpec(pe.shape, c3), pl.BlockSpec(kg0.shape, c2),
                  pl.BlockSpec(pool.shape, c2)],
        out_specs=pl.BlockSpec((tq, A_WIDTH), tile),
        out_shape=jax.ShapeDtypeStruct((nb * T, A_WIDTH), F32),
        scratch_shapes=[pltpu.VMEM((T, LANES), F32), pltpu.VMEM((T, LANES), F32),
                        pltpu.VMEM((nseg, LANES), F32), pltpu.VMEM((nseg, LANES), F32),
                        pltpu.VMEM((r4, LANES), F32), pltpu.VMEM((r4, LANES), F32),
                        pltpu.VMEM((r4, LANES), F32)],
        compiler_params=_cparams(("parallel", "arbitrary")),
        name="nsa_prompt",
    )(q, qr, small, rows, win, wbd, pe, kg0, pool)


def _nsa_sample_kernel(pt_ref, cache_ref, q_ref, qr_ref, small_ref, rows_ref, winnew_ref, winbuf_ref,
                       wbd_ref, pe_ref, kg0_ref, pool_ref, expand_ref,
                       o_ref, winout_ref,
                       cmp_buf, sel_buf, sems, *, n_pages, past_len, t_valid):
    b = pl.program_id(0)
    nb = pl.num_programs(0)
    tp = SAMPLE_PAD_T
    nseg = past_len // CMP_STRIDE
    nsb = past_len // SEL_LEN
    wbuf = winbuf_ref.shape[0]

    def page_copies(bb, p, phase):
        page = pt_ref[bb * n_pages + p]
        dst_rows = pl.ds(p * PAGE_SIZE, PAGE_SIZE)
        if phase == 0:
            return [pltpu.make_async_copy(cache_ref.at[page, :, pl.ds(j * LANES, LANES)],
                                          cmp_buf.at[j, dst_rows, :], sems.at[0]) for j in range(2)]
        return [pltpu.make_async_copy(cache_ref.at[page, :, pl.ds(2 * LANES, 2 * LANES)],
                                      sel_buf.at[dst_rows, :], sems.at[1])]

    def start_all(bb, phase):
        def body(p, c):
            for cp in page_copies(bb, p, phase):
                cp.start()
            return c
        lax.fori_loop(0, n_pages, body, 0)

    def wait_all(bb, phase):
        def body(p, c):
            for cp in page_copies(bb, p, phase):
                cp.wait()
            return c
        lax.fori_loop(0, n_pages, body, 0)

    @pl.when(b == 0)
    def _():
        start_all(b, 0)

    start_all(b, 1)
    wait_all(b, 0)

    kc, vc = _compress(cmp_buf.at[0], cmp_buf.at[1], nseg, wbd_ref, pe_ref, kg0_ref[...])
    kc_b = kc.astype(BF16)
    vc_b = vc.astype(BF16)
    q = q_ref[...]
    qr = qr_ref[...]
    small = small_ref[...]
    tpos_col = past_len + lax.broadcasted_iota(jnp.int32, (tp, 1), 0)
    tpos_rows = jnp.concatenate([tpos_col] * A_HPG, axis=0)
    bp_idx = lax.broadcasted_iota(jnp.int32, (nsb, nsb), 0)
    b_idx = lax.broadcasted_iota(jnp.int32, (nsb, nsb), 1)
    o_cmps = []
    sels = []
    for g in range(A_KV):
        qn_g = _stack_heads(q, g)
        o_cmp, imp = _cmp_branch(qn_g, kc_b, vc_b, tpos_rows, nseg, tp)
        o_cmps.append(o_cmp)
        imp_sel = _dot2_exact_rhs(imp, pool_ref[...])
        imp_pad = jnp.concatenate([imp_sel, jnp.zeros((nsb - tp, nsb), F32)], axis=0)
        imp_t = jnp.transpose(imp_pad)
        rows_sel = []
        for t in range(tp):
            if t < t_valid:
                row_t = imp_sel[t:t + 1, :]
                col_t = imp_t[:, t:t + 1]
                ahead = jnp.where(col_t > row_t, 1.0, jnp.where((col_t == row_t) & (bp_idx < b_idx), 1.0, 0.0))
                rank = jnp.sum(ahead, axis=0, keepdims=True)
                rows_sel.append(jnp.where(rank < (N_SEL - 1), 1.0, 0.0))
            else:
                rows_sel.append(jnp.zeros((1, nsb), F32))
        sels.append(jnp.concatenate(rows_sel, axis=0).astype(BF16))

    @pl.when(b + 1 < nb)
    def _():
        start_all(b + 1, 0)

    wait_all(b, 1)

    new_idx = lax.broadcasted_iota(jnp.int32, (tp, tp), 1)
    tok_idx = lax.broadcasted_iota(jnp.int32, (tp, tp), 0)
    new_ok = jnp.concatenate([jnp.where(new_idx <= tok_idx, 1.0, 0.0)] * A_HPG, axis=0) > 0.5
    wpos = past_len - wbuf + lax.broadcasted_iota(jnp.int32, (1, wbuf), 1)
    wdiff = tpos_col - wpos
    win_ok = jnp.concatenate([jnp.where((wdiff >= 0) & (wdiff < WINDOW), 1.0, 0.0)] * A_HPG, axis=0) > 0.5
    k_past = sel_buf[:, 0:LANES].astype(BF16)
    v_past = sel_buf[:, LANES:2 * LANES].astype(BF16)
    k_new = rows_ref[:, 2 * LANES:3 * LANES]
    v_new = rows_ref[:, 3 * LANES:4 * LANES]
    kw_past = winbuf_ref[:, 0:LANES]
    vw_past = winbuf_ref[:, LANES:2 * LANES]
    kw_new = winnew_ref[:, 0:LANES]
    vw_new = winnew_ref[:, LANES:2 * LANES]
    o_groups = []
    for g in range(A_KV):
        qr_g = _stack_heads(qr, g)
        mk = jnp.dot(sels[g], expand_ref[...], preferred_element_type=F32)
        past_ok = jnp.concatenate([mk] * A_HPG, axis=0) > 0.5
        o_sel = _masked_attn_direct(qr_g, [k_past, k_new], [v_past, v_new], [past_ok, new_ok])
        o_win = _masked_attn_direct(qr_g, [kw_past, kw_new], [vw_past, vw_new], [win_ok, new_ok])
        o_groups.append(_gate_cols(small, g, 0) * o_cmps[g] + _gate_cols(small, g, 1) * o_sel
                        + _gate_cols(small, g, 2) * o_win)
    o_ref[...] = _assemble_heads(o_groups, tp)

    wb = winbuf_ref[...]
    rolled = pltpu.roll(wb, wbuf - t_valid, 0)
    newr = pltpu.roll(winnew_ref[...], tp - t_valid, 0)
    sub = lax.broadcasted_iota(jnp.int32, (tp, 2 * LANES), 0)
    winout_ref[0:wbuf - tp, :] = rolled[0:wbuf - tp, :]
    winout_ref[wbuf - tp:wbuf, :] = jnp.where(sub < tp - t_valid, rolled[wbuf - tp:wbuf, :], newr)


def _nsa_sample(page_table, cache, q, qr, small, rows, winnew, winbuf, wbd, pe, kg0, t_valid):
    nb, n_pages = page_table.shape
    past_len = n_pages * PAGE_SIZE
    nseg = past_len // CMP_STRIDE
    nsb = past_len // SEL_LEN
    tp = SAMPLE_PAD_T
    wbuf = winbuf.shape[1]
    pool = (jnp.arange(nseg)[:, None] // (SEL_LEN // CMP_STRIDE) == jnp.arange(nsb)[None, :]).astype(BF16)
    expand = (jnp.arange(nsb)[:, None] == jnp.arange(past_len)[None, :] // SEL_LEN).astype(BF16)
    tile = lambda b, pt: (b, 0)
    c2 = lambda b, pt: (0, 0)
    c3 = lambda b, pt: (0, 0, 0)
    gs = pltpu.PrefetchScalarGridSpec(
        num_scalar_prefetch=1,
        grid=(nb,),
        in_specs=[pl.BlockSpec(memory_space=pl.ANY),
                  pl.BlockSpec((tp, A_WIDTH), tile), pl.BlockSpec((tp, A_WIDTH), tile),
                  pl.BlockSpec((tp, LANES), tile), pl.BlockSpec((tp, 4 * LANES), tile),
                  pl.BlockSpec((tp, 2 * LANES), tile),
                  pl.BlockSpec((None, wbuf, 2 * LANES), lambda b, pt: (b, 0, 0)),
                  pl.BlockSpec(wbd.shape, c3), pl.BlockSpec(pe.shape, c3), pl.BlockSpec(kg0.shape, c2),
                  pl.BlockSpec(pool.shape, c2), pl.BlockSpec(expand.shape, c2)],
        out_specs=[pl.BlockSpec((tp, A_WIDTH), tile),
                   pl.BlockSpec((None, wbuf, 2 * LANES), lambda b, pt: (b, 0, 0))],
        scratch_shapes=[pltpu.VMEM((2, past_len, LANES), F32), pltpu.VMEM((past_len, 2 * LANES), F32),
                        pltpu.SemaphoreType.DMA((2,))],
    )
    return pl.pallas_call(
        functools.partial(_nsa_sample_kernel, n_pages=n_pages, past_len=past_len, t_valid=t_valid),
        grid_spec=gs,
        out_shape=[jax.ShapeDtypeStruct((nb * tp, A_WIDTH), F32),
                   jax.ShapeDtypeStruct((nb, wbuf, 2 * LANES), F32)],
        compiler_params=_cparams(("arbitrary",)),
        name="nsa_sample",
    )(page_table.reshape(-1), cache, q, qr, small, rows, winnew, winbuf, wbd, pe, kg0, pool, expand)


def _mixout_kernel(x_ref, hm_ref, on_ref, mod_ref, gmix_ref, gffn_ref,
                   wog_ref, bog_ref, wum_ref, wua_ref, wout_ref, wr_ref, br_ref,
                   x1_ref, h2_ref, lg_ref):
    d = D_MODEL
    x = x_ref[...]
    sh1, sc1, gt1 = mod_ref[:, 0:d], mod_ref[:, d:2 * d], mod_ref[:, 2 * d:3 * d]
    sh2, sc2 = mod_ref[:, 3 * d:4 * d], mod_ref[:, 4 * d:5 * d]
    h = _rmsnorm_rows(x, gmix_ref[...]) * (1.0 + sc1) + sh1
    hb = h.astype(BF16)
    mo = jnp.dot(hb, wog_ref[:, 0:M_WIDTH], preferred_element_type=F32) + bog_ref[:, 0:M_WIDTH]
    ym = _bdot(_sigmoid(mo) * hm_ref[...], wum_ref[...])
    ya = _bdot(on_ref[...], wua_ref[...])
    ga = jnp.dot(hb, wog_ref[:, M_WIDTH:M_WIDTH + d], preferred_element_type=F32) + bog_ref[:, M_WIDTH:M_WIDTH + d]
    u = _sigmoid(ga) * ym
    gb = (jnp.dot(hb, wog_ref[:, M_WIDTH + d:M_WIDTH + 2 * d], preferred_element_type=F32)
          + bog_ref[:, M_WIDTH + d:M_WIDTH + 2 * d])
    u = u + _sigmoid(gb) * ya
    x1 = x + gt1 * _bdot(u, wout_ref[...])
    x1_ref[...] = x1
    h2 = _rmsnorm_rows(x1, gffn_ref[...]) * (1.0 + sc2) + sh2
    h2_ref[...] = h2.astype(BF16)
    lg_ref[...] = _dot3(h2, wr_ref[...]) + br_ref[...]


def _mixout(x2, hm, on, mod3, gmix, gffn, wts, tm, tiles_per_mod):
    m = x2.shape[0]
    (wog, bog, wum, wua, wout, wr, br) = wts
    r = mod3.shape[1]
    row = lambda i: (i, 0)
    const = lambda i: (0, 0)
    return pl.pallas_call(
        _mixout_kernel,
        grid=(m // tm,),
        in_specs=[pl.BlockSpec((tm, D_MODEL), row), pl.BlockSpec((tm, M_WIDTH), row),
                  pl.BlockSpec((tm, A_WIDTH), row),
                  pl.BlockSpec((None, r, 6 * D_MODEL), lambda i: (i // tiles_per_mod, 0, 0)),
                  pl.BlockSpec((1, D_MODEL), const), pl.BlockSpec((1, D_MODEL), const),
                  pl.BlockSpec(wog.shape, const), pl.BlockSpec(bog.shape, const),
                  pl.BlockSpec(wum.shape, const), pl.BlockSpec(wua.shape, const),
                  pl.BlockSpec(wout.shape, const), pl.BlockSpec(wr.shape, const),
                  pl.BlockSpec(br.shape, const)],
        out_specs=[pl.BlockSpec((tm, D_MODEL), row), pl.BlockSpec((tm, D_MODEL), row),
                   pl.BlockSpec((tm, LANES), row)],
        out_shape=[jax.ShapeDtypeStruct((m, D_MODEL), F32), jax.ShapeDtypeStruct((m, D_MODEL), BF16),
                   jax.ShapeDtypeStruct((m, LANES), F32)],
        compiler_params=_cparams(("parallel",)),
        name="mixout",
    )(x2, hm, on, mod3, gmix, gffn, wog, bog, wum, wua, wout, wr, br)


MOE_BM = 256
MOE_CH = 512


def _moe_kernel(be_ref, na_ref, xs_ref, wgu_ref, bgu_ref, wdn_ref, bdn_ref, y_ref, wgu_bf, wdn_bf):
    i = pl.program_id(0)
    e = be_ref[i]
    prev = be_ref[jnp.maximum(i - 1, 0)]

    @pl.when((i == 0) | (e != prev))
    def _():
        for j in range(2 * D_EXPERT // MOE_CH):
            wgu_bf[:, j * MOE_CH:(j + 1) * MOE_CH] = wgu_ref[:, j * MOE_CH:(j + 1) * MOE_CH].astype(BF16)
        for j in range(D_EXPERT // MOE_CH):
            wdn_bf[j * MOE_CH:(j + 1) * MOE_CH, :] = wdn_ref[j * MOE_CH:(j + 1) * MOE_CH, :].astype(BF16)

    @pl.when(i < na_ref[0])
    def _():
        xb = xs_ref[...]
        acc = jnp.zeros(y_ref.shape, F32) + bdn_ref[...]
        for j in range(D_EXPERT // MOE_CH):
            lo, hi = j * MOE_CH, (j + 1) * MOE_CH
            gj = jnp.dot(xb, wgu_bf[:, lo:hi], preferred_element_type=F32) + bgu_ref[:, lo:hi]
            uj = (jnp.dot(xb, wgu_bf[:, D_EXPERT + lo:D_EXPERT + hi], preferred_element_type=F32)
                  + bgu_ref[:, D_EXPERT + lo:D_EXPERT + hi])
            gj = jnp.minimum(gj, SWIGLU_LIMIT)
            uj = jnp.clip(uj, -SWIGLU_LIMIT, SWIGLU_LIMIT)
            act = gj * _sigmoid(SWIGLU_ALPHA * gj) * (uj + 1.0)
            acc = acc + jnp.dot(act.astype(BF16), wdn_bf[lo:hi, :], preferred_element_type=F32)
        y_ref[...] = acc

    @pl.when(i >= na_ref[0])
    def _():
        y_ref[...] = jnp.zeros(y_ref.shape, F32)


def _moe_experts(block_e, n_active, xs, w_gu, b_gu, w_dn, b_dn):
    p = xs.shape[0]
    nblk = p // MOE_BM
    gs = pltpu.PrefetchScalarGridSpec(
        num_scalar_prefetch=2,
        grid=(nblk,),
        in_specs=[pl.BlockSpec((MOE_BM, D_MODEL), lambda i, be, na: (i, 0)),
                  pl.BlockSpec((None, D_MODEL, 2 * D_EXPERT), lambda i, be, na: (be[i], 0, 0)),
                  pl.BlockSpec((None, 1, 2 * D_EXPERT), lambda i, be, na: (be[i], 0, 0)),
                  pl.BlockSpec((None, D_EXPERT, D_MODEL), lambda i, be, na: (be[i], 0, 0)),
                  pl.BlockSpec((None, 1, D_MODEL), lambda i, be, na: (be[i], 0, 0))],
        out_specs=pl.BlockSpec((MOE_BM, D_MODEL), lambda i, be, na: (i, 0)),
        scratch_shapes=[pltpu.VMEM((D_MODEL, 2 * D_EXPERT), BF16), pltpu.VMEM((D_EXPERT, D_MODEL), BF16)],
    )
    return pl.pallas_call(
        _moe_kernel,
        grid_spec=gs,
        out_shape=jax.ShapeDtypeStruct((p, D_MODEL), F32),
        compiler_params=_cparams(("arbitrary",)),
        name="moe_experts",
    )(block_e, n_active, xs, w_gu, b_gu.reshape(N_EXPERTS, 1, -1), w_dn, b_dn.reshape(N_EXPERTS, 1, -1))


def _moe(h2, logits, w_gu, b_gu, w_dn, b_dn):
    m = h2.shape[0]
    top_v, top_e = lax.top_k(logits[:, :N_EXPERTS], TOP_K)
    gate = jax.nn.softmax(top_v, axis=-1)
    mk = m * TOP_K
    flat_e = top_e.reshape(mk)
    order = jnp.argsort(flat_e)
    sorted_e = flat_e[order]
    counts = jnp.bincount(flat_e, length=N_EXPERTS)
    padded = (counts + MOE_BM - 1) // MOE_BM * MOE_BM
    pad_end = jnp.cumsum(padded)
    pad_start = pad_end - padded
    start = jnp.cumsum(counts) - counts
    dest = (pad_start[sorted_e] + jnp.arange(mk) - start[sorted_e]).astype(jnp.int32)
    n_blocks = (mk + N_EXPERTS * (MOE_BM - 1) + MOE_BM - 1) // MOE_BM
    src_tok = jnp.full((n_blocks * MOE_BM,), m, jnp.int32).at[dest].set((order // TOP_K).astype(jnp.int32))
    block_e = jnp.minimum(jnp.searchsorted(pad_end, jnp.arange(n_blocks) * MOE_BM, side='right'),
                          N_EXPERTS - 1).astype(jnp.int32)
    n_active = (pad_end[-1] // MOE_BM).astype(jnp.int32).reshape(1)
    hp = jnp.concatenate([h2, jnp.zeros((1, D_MODEL), h2.dtype)], axis=0)
    xs = hp[src_tok]
    ys = _moe_experts(block_e, n_active, xs, w_gu, b_gu, w_dn, b_dn)
    slot_dest = jnp.zeros((mk,), jnp.int32).at[order].set(dest)
    y_slots = ys[slot_dest].reshape(m, TOP_K, D_MODEL)
    return jnp.einsum('mkd,mk->md', y_slots, gate)


def _rope_tables(pos):
    half = ROT_DIM // 2
    inv = ROPE_THETA ** (-jnp.arange(half, dtype=F32) * (2.0 / ROT_DIM))
    ang = pos.astype(F32)[:, None] * inv[None, :]
    cos, sin = jnp.cos(ang), jnp.sin(ang)
    n = pos.shape[0]
    ones = jnp.ones((n, A_DH - ROT_DIM), F32)
    zeros_h = jnp.zeros((n, half), F32)
    zeros_r = jnp.zeros((n, A_DH - ROT_DIM), F32)
    cos64 = jnp.concatenate([cos, cos, ones], axis=1)
    sprev64 = jnp.concatenate([zeros_h, sin, zeros_r], axis=1)
    snext64 = jnp.concatenate([-sin, zeros_h, zeros_r], axis=1)
    two = lambda a: jnp.concatenate([a, a], axis=1)
    return two(cos64), two(sprev64), two(snext64)


def _prep_weights(w_in, b_in, q_norm_g, k_norm_g, cmp_pe_k, cmp_pe_v, cmp_w_k, cmp_w_v,
                  w_up_m, w_up_a, w_out, w_router, b_router):
    b2 = b_in.reshape(1, N_IN)
    wm = w_in[:, OFF_MQ:OFF_MO].astype(BF16)
    bm = b2[:, OFF_MQ:OFF_MO]
    wq = w_in[:, OFF_AQ:OFF_AKV].astype(BF16)
    bq = b2[:, OFF_AQ:OFF_AKV]
    wkv = w_in[:, OFF_AKV:OFF_AG].astype(BF16)
    bkv = b2[:, OFF_AKV:OFF_AG]
    n_small = 2 * M_HEADS + 3 * A_HEADS
    ws = jnp.concatenate([w_in[:, OFF_MI:OFF_AQ], w_in[:, OFF_AG:OFF_GA],
                          jnp.zeros((D_MODEL, LANES - n_small), F32)], axis=1)
    bs = jnp.concatenate([b2[:, OFF_MI:OFF_AQ], b2[:, OFF_AG:OFF_GA], jnp.zeros((1, LANES - n_small), F32)], axis=1)
    qg = jnp.tile(q_norm_g, A_HEADS).reshape(1, A_WIDTH)
    kg = jnp.stack([jnp.tile(k_norm_g[1], A_KV), jnp.tile(k_norm_g[2], A_KV)], axis=0)
    kg0 = jnp.tile(k_norm_g[0], A_KV).reshape(1, LANES)
    hid = jnp.arange(A_WIDTH) // A_DH
    bd = jnp.where(hid[:, None] == hid[None, :], 1.0 / A_DH, 0.0).astype(BF16)
    inproj_w = (wm, bm, wq, bq, wkv, bkv, ws, bs, qg, kg, bd)

    z = jnp.zeros((CMP_LEN, A_DH, A_DH), F32)
    r0 = jnp.concatenate([cmp_w_k, z, z, z], axis=2)
    r1 = jnp.concatenate([z, cmp_w_k, z, z], axis=2)
    r2 = jnp.concatenate([z, z, cmp_w_v, z], axis=2)
    r3 = jnp.concatenate([z, z, z, cmp_w_v], axis=2)
    wbd = jnp.concatenate([r0, r1, r2, r3], axis=1).astype(BF16)
    pe = jnp.concatenate([cmp_pe_k, cmp_pe_k, cmp_pe_v, cmp_pe_v], axis=1).reshape(CMP_LEN, 1, 2 * LANES)

    wog = jnp.concatenate([w_in[:, OFF_MO:OFF_MI], w_in[:, OFF_GA:N_IN]], axis=1).astype(BF16)
    bog = jnp.concatenate([b2[:, OFF_MO:OFF_MI], b2[:, OFF_GA:N_IN]], axis=1)
    wr = jnp.concatenate([w_router, jnp.zeros((D_MODEL, LANES - N_EXPERTS), F32)], axis=1)
    br = jnp.concatenate([b_router, jnp.zeros((LANES - N_EXPERTS,), F32)]).reshape(1, LANES)
    mixout_w = (wog, bog, w_up_m.astype(BF16), w_up_a.astype(BF16), w_out.astype(BF16), wr, br)
    return inproj_w, (wbd, pe, kg0), mixout_w


def _pick_tile(m, pref):
    t = pref
    while m % t:
        t //= 2
    return t


def kernel(x_prompt, x_sample, cache_nsa_kv, state_win_kv, state_mlstm_C, state_mlstm_n, state_mlstm_m, page_table, c_prompt, c_sample, w_ada, b_ada, g_mix, g_ffn, w_in, b_in, q_norm_g, k_norm_g, cmp_pe_k, cmp_pe_v, cmp_w_k, cmp_w_v, w_up_m, w_up_a, w_out, w_router, b_router, w_gu, b_gu, w_dn, b_dn):
    depth = w_in.shape[0]
    assert depth == 1
    B, T, D = x_prompt.shape
    DB, TS, _ = x_sample.shape
    n_pages = page_table.shape[1]
    past_len = n_pages * PAGE_SIZE
    wbuf = state_win_kv.shape[2]
    tp = SAMPLE_PAD_T
    assert TS <= tp and wbuf % tp == 0 and T % 128 == 0

    l = 0
    inproj_w, cmp_w, mixout_w = _prep_weights(
        w_in[l], b_in[l], q_norm_g[l], k_norm_g[l], cmp_pe_k[l], cmp_pe_v[l], cmp_w_k[l], cmp_w_v[l],
        w_up_m[l], w_up_a[l], w_out[l], w_router[l], b_router[l])
    wbd, pe, kg0 = cmp_w
    gmix = g_mix[l].reshape(1, D)
    gffn = g_ffn[l].reshape(1, D)

    nc = B + DB
    nc_pad = -(-nc // SUBLANES) * SUBLANES
    c_all = jnp.concatenate([c_prompt, c_sample, jnp.zeros((nc_pad - nc, D), F32)], axis=0)
    mod = _adaln(c_all, w_ada[l], b_ada[l])
    mod_p = mod[:B].reshape(B, 1, 6 * D)
    mod_s = jnp.repeat(mod[B:B + DB], tp, axis=0).reshape(1, DB * tp, 6 * D)

    mp = B * T
    tm = _pick_tile(T, 256)
    xp = x_prompt.reshape(mp, D)
    tabs_p = _rope_tables(jnp.arange(T, dtype=jnp.int32))
    mq, mk, mv, q, qr, rows, win, small = _inproj(xp, mod_p, gmix, tabs_p, inproj_w, tm, T // tm, T // tm)
    Lp = _pick_tile(T, 128)
    hm, C_p, n_p, m_p = _mlstm(mq, mk, mv, small, B, T, T, Lp)
    o_nsa = _nsa_prompt(q, qr, small, rows, win, wbd, pe, kg0, B, T)
    x1_p, h2_p, lg_p = _mixout(xp, hm, o_nsa, mod_p, gmix, gffn, mixout_w, tm, T // tm)

    ms = DB * tp
    xs_pad = jnp.concatenate([x_sample, jnp.zeros((DB, tp - TS, D), F32)], axis=1).reshape(ms, D)
    pos_s = past_len + jnp.tile(jnp.arange(tp, dtype=jnp.int32), DB)
    tabs_s = _rope_tables(pos_s)
    mq_s, mk_s, mv_s, q_s, qr_s, rows_s, win_s, small_s = _inproj(xs_pad, mod_s, gmix, tabs_s, inproj_w, ms, 1, 1)
    hm_s, C_s, n_s, m_s = _mlstm(mq_s, mk_s, mv_s, small_s, DB, tp, TS, tp,
                                 state=(state_mlstm_C[l], state_mlstm_n[l], state_mlstm_m[l]))
    cache2 = cache_nsa_kv[l].reshape(cache_nsa_kv.shape[1], PAGE_SIZE, 4 * LANES)
    winbuf = state_win_kv[l].reshape(DB, wbuf, 2 * LANES)
    o_nsa_s, win_out_s = _nsa_sample(page_table, cache2, q_s, qr_s, small_s, rows_s, win_s, winbuf,
                                     wbd, pe, kg0, TS)
    x1_s, h2_s, lg_s = _mixout(xs_pad, hm_s, o_nsa_s, mod_s, gmix, gffn, mixout_w, ms, 1)

    valid = lambda a: a.reshape(DB, tp, -1)[:, :TS].reshape(DB * TS, -1)
    h2_all = jnp.concatenate([h2_p, valid(h2_s)], axis=0)
    lg_all = jnp.concatenate([lg_p, valid(lg_s)], axis=0)
    moe = _moe(h2_all, lg_all, w_gu[l], b_gu[l], w_dn[l], b_dn[l])
    gt2_p = mod[:B, 5 * D:6 * D][:, None, :]
    gt2_s = mod[B:B + DB, 5 * D:6 * D][:, None, :]
    y_p = x1_p.reshape(B, T, D) + gt2_p * moe[:mp].reshape(B, T, D)
    y_s = valid(x1_s).reshape(DB, TS, D) + gt2_s * moe[mp:].reshape(DB, TS, D)

    kv_p = rows.reshape(1, B, T, 4, A_KV, A_DH)
    kv_s = valid(rows_s).reshape(1, DB, TS, 4, A_KV, A_DH)
    wp = min(WINDOW, T)
    win_p = win.reshape(B, T, 2, A_KV, A_DH)[:, T - wp:][None]
    win_s_out = win_out_s.reshape(1, DB, wbuf, 2, A_KV, A_DH)
    return (y_p, y_s, kv_p, kv_s, win_p, win_s_out,
            C_p[None], n_p[None], m_p[None], C_s[None], n_s[None], m_s[None])
```

```python
import functools
import math

import jax
import jax.numpy as jnp
from jax import lax
from jax.experimental import pallas as pl
from jax.experimental.pallas import tpu as pltpu

F32 = jnp.float32
BF16 = jnp.bfloat16

D_MODEL = 1024
M_HEADS = 4
M_DH = 128
M_WIDTH = M_HEADS * M_DH
A_HEADS = 8
A_KV = 2
A_HPG = A_HEADS // A_KV
A_DH = 64
A_WIDTH = A_HEADS * A_DH
CMP_STRIDE = 16
CMP_LEN = 32
SEL_LEN = 64
N_SEL = 16
WINDOW = 512
PAGE_SIZE = 128
ROPE_THETA = 500000.0
ROT_DIM = A_DH // 4
ATT_SCALE = A_DH ** -0.5
N_EXPERTS = 32
TOP_K = 4
D_EXPERT = D_MODEL
SWIGLU_LIMIT = 7.0
SWIGLU_ALPHA = 1.702
EPS = 1e-6

OFF_MQ, OFF_MK, OFF_MV, OFF_MO = 0, M_WIDTH, 2 * M_WIDTH, 3 * M_WIDTH
OFF_MI = 4 * M_WIDTH
OFF_MF = OFF_MI + M_HEADS
OFF_AQ = OFF_MF + M_HEADS
OFF_AKV = OFF_AQ + A_WIDTH
OFF_AG = OFF_AKV + 6 * A_KV * A_DH
OFF_GA = OFF_AG + 3 * A_HEADS
OFF_GB = OFF_GA + D_MODEL
N_IN = OFF_GB + D_MODEL

LANES = 128
SUBLANES = 8
VMEM_LIMIT = 56 * 1024 * 1024

NEG_BIG = -1e30
M_INIT = -1e29
LOG2E = 1.4426950408889634
SAMPLE_PAD_T = 8


def _cparams(sem):
    return pltpu.CompilerParams(dimension_semantics=sem, vmem_limit_bytes=VMEM_LIMIT)


def _bdot(a, b):
    return jnp.dot(a.astype(BF16), b.astype(BF16), preferred_element_type=F32)


def _bdot_t(a, b):
    return lax.dot_general(a.astype(BF16), b.astype(BF16), (((1,), (1,)), ((), ())),
                           preferred_element_type=F32)


def _split(a):
    hi = a.astype(BF16)
    lo = (a - hi.astype(F32)).astype(BF16)
    return hi, lo


def _dot3(a, b):
    ah, al = _split(a)
    bh, bl = _split(b)
    return (jnp.dot(ah, bh, preferred_element_type=F32) + jnp.dot(al, bh, preferred_element_type=F32)
            + jnp.dot(ah, bl, preferred_element_type=F32))


def _dot2_exact_rhs(a, b_bf16):
    ah, al = _split(a)
    return jnp.dot(ah, b_bf16, preferred_element_type=F32) + jnp.dot(al, b_bf16, preferred_element_type=F32)


def _sigmoid(x):
    return 1.0 / (1.0 + jnp.exp(-x))


def _rmsnorm_rows(x, g):
    return x * lax.rsqrt(jnp.mean(x * x, axis=-1, keepdims=True) + EPS) * g


def _adaln_kernel(c_ref, w_ref, b_ref, o_ref):
    c = c_ref[...]
    s = c * _sigmoid(c)
    o_ref[...] = _dot3(s, w_ref[...]) + b_ref[...]


def _adaln(c, w, b):
    mc, d = c.shape
    n = w.shape[1]
    tn = 1024
    return pl.pallas_call(
        _adaln_kernel,
        grid=(n // tn,),
        in_specs=[pl.BlockSpec((mc, d), lambda j: (0, 0)),
                  pl.BlockSpec((d, tn), lambda j: (0, j)),
                  pl.BlockSpec((1, tn), lambda j: (0, j))],
        out_specs=pl.BlockSpec((mc, tn), lambda j: (0, j)),
        out_shape=jax.ShapeDtypeStruct((mc, n), F32),
        compiler_params=_cparams(("parallel",)),
        name="adaln",
    )(c, w, b.reshape(1, n))


def _head_norm(z, bd, gain):
    ms = _dot2_exact_rhs(z * z, bd)
    return z * lax.rsqrt(ms + EPS) * gain


def _rope(z, cos, s_prev, s_next):
    w = z.shape[1]
    rep = w // LANES
    if rep > 1:
        cos = jnp.concatenate([cos] * rep, axis=1)
        s_prev = jnp.concatenate([s_prev] * rep, axis=1)
        s_next = jnp.concatenate([s_next] * rep, axis=1)
    z_prev = pltpu.roll(z, ROT_DIM // 2, 1)
    z_next = pltpu.roll(z, w - ROT_DIM // 2, 1)
    return z * cos + z_prev * s_prev + z_next * s_next


def _inproj_kernel(x_ref, mod_ref, gmix_ref, cos_ref, sp_ref, sn_ref,
                   wm_ref, bm_ref, wq_ref, bq_ref, wkv_ref, bkv_ref, ws_ref, bs_ref,
                   qg_ref, kg_ref, bd_ref,
                   mq_ref, mk_ref, mv_ref, q_ref, qr_ref, rows_ref, win_ref, small_ref):
    x = x_ref[...]
    sh1 = mod_ref[:, 0:D_MODEL]
    sc1 = mod_ref[:, D_MODEL:2 * D_MODEL]
    h = _rmsnorm_rows(x, gmix_ref[...]) * (1.0 + sc1) + sh1
    hb = h.astype(BF16)

    mq_ref[...] = jnp.dot(hb, wm_ref[:, 0:M_WIDTH], preferred_element_type=F32) + bm_ref[:, 0:M_WIDTH]
    mk = jnp.dot(hb, wm_ref[:, M_WIDTH:2 * M_WIDTH], preferred_element_type=F32) + bm_ref[:, M_WIDTH:2 * M_WIDTH]
    mk_ref[...] = mk * (M_DH ** -0.5)
    mv_ref[...] = (jnp.dot(hb, wm_ref[:, 2 * M_WIDTH:3 * M_WIDTH], preferred_element_type=F32)
                   + bm_ref[:, 2 * M_WIDTH:3 * M_WIDTH])

    cos, sp, sn = cos_ref[...], sp_ref[...], sn_ref[...]
    zq = jnp.dot(hb, wq_ref[...], preferred_element_type=F32) + bq_ref[...]
    qn = _head_norm(zq, bd_ref[...], qg_ref[...])
    q_ref[...] = qn
    qr_ref[...] = _rope(qn, cos, sp, sn)

    zkv = jnp.dot(hb, wkv_ref[...], preferred_element_type=F32) + bkv_ref[...]
    bd2 = bd_ref[0:LANES, 0:LANES]
    rows_ref[:, 0:2 * LANES] = zkv[:, 0:2 * LANES]
    ksel = _head_norm(zkv[:, 2 * LANES:3 * LANES], bd2, kg_ref[0:1, :])
    rows_ref[:, 2 * LANES:3 * LANES] = _rope(ksel, cos, sp, sn)
    rows_ref[:, 3 * LANES:4 * LANES] = zkv[:, 3 * LANES:4 * LANES]
    kwin = _head_norm(zkv[:, 4 * LANES:5 * LANES], bd2, kg_ref[1:2, :])
    win_ref[:, 0:LANES] = _rope(kwin, cos, sp, sn)
    win_ref[:, LANES:2 * LANES] = zkv[:, 5 * LANES:6 * LANES]

    small_ref[...] = _dot3(h, ws_ref[...]) + bs_ref[...]


def _inproj(x2, mod3, gmix, tabs, wts, tm, tiles_per_mod, pos_tiles):
    m = x2.shape[0]
    cos_t, sp_t, sn_t = tabs
    (wm, bm, wq, bq, wkv, bkv, ws, bs, qg, kg, bd) = wts
    r = mod3.shape[1]
    row = lambda i: (i, 0)
    const = lambda i: (0, 0)
    tab = lambda i: (i % pos_tiles, 0)
    in_specs = [
        pl.BlockSpec((tm, D_MODEL), row),
        pl.BlockSpec((None, r, 6 * D_MODEL), lambda i: (i // tiles_per_mod, 0, 0)),
        pl.BlockSpec((1, D_MODEL), const),
        pl.BlockSpec((tm, LANES), tab), pl.BlockSpec((tm, LANES), tab), pl.BlockSpec((tm, LANES), tab),
        pl.BlockSpec(wm.shape, const), pl.BlockSpec(bm.shape, const),
        pl.BlockSpec(wq.shape, const), pl.BlockSpec(bq.shape, const),
        pl.BlockSpec(wkv.shape, const), pl.BlockSpec(bkv.shape, const),
        pl.BlockSpec(ws.shape, const), pl.BlockSpec(bs.shape, const),
        pl.BlockSpec(qg.shape, const), pl.BlockSpec(kg.shape, const), pl.BlockSpec(bd.shape, const),
    ]
    widths = (M_WIDTH, M_WIDTH, M_WIDTH, A_WIDTH, A_WIDTH, 4 * LANES, 2 * LANES, LANES)
    return pl.pallas_call(
        _inproj_kernel,
        grid=(m // tm,),
        in_specs=in_specs,
        out_specs=[pl.BlockSpec((tm, w), row) for w in widths],
        out_shape=[jax.ShapeDtypeStruct((m, w), F32) for w in widths],
        compiler_params=_cparams(("parallel",)),
        name="inproj",
    )(x2, mod3, gmix, cos_t, sp_t, sn_t, wm, bm, wq, bq, wkv, bkv, ws, bs, qg, kg, bd)


def _log_sigmoid(x):
    return jnp.minimum(x, 0.0) - jnp.log(1.0 + jnp.exp(-jnp.abs(x)))


def _mlstm_kernel(*refs, L, t_valid, has_state):
    if has_state:
        q_ref, k_ref, v_ref, s_ref, c0_ref, n0_ref, m0_ref, h_ref, c_ref, n_ref, m_ref = refs
    else:
        q_ref, k_ref, v_ref, s_ref, h_ref, c_ref, n_ref, m_ref = refs
    c = pl.program_id(1)

    @pl.when(c == 0)
    def _():
        if has_state:
            c_ref[...] = c0_ref[...]
            n_ref[...] = n0_ref[...]
            m_ref[...] = m0_ref[...]
        else:
            c_ref[...] = jnp.zeros(c_ref.shape, F32)
            n_ref[...] = jnp.zeros(n_ref.shape, F32)
            m_ref[...] = jnp.zeros(m_ref.shape, F32)

    row = lax.broadcasted_iota(jnp.int32, (L, L), 0)
    col = lax.broadcasted_iota(jnp.int32, (L, L), 1)
    causal = col <= row
    eye = col == row
    tok_col = c * L + lax.broadcasted_iota(jnp.int32, (L, 1), 0)
    valid_col = tok_col < t_valid
    for hd in range(M_HEADS):
        lo, hi = hd * M_DH, (hd + 1) * M_DH
        q = q_ref[:, lo:hi]
        k = k_ref[:, lo:hi]
        v = v_ref[:, lo:hi]
        i_col = s_ref[:, hd:hd + 1]
        lf_col = _log_sigmoid(s_ref[:, M_HEADS + hd:M_HEADS + hd + 1])
        lf_col = jnp.where(valid_col, lf_col, 0.0)
        i_col = jnp.where(valid_col, i_col, -jnp.inf)
        i_row = jnp.sum(jnp.where(eye, i_col, 0.0), axis=0, keepdims=True)
        lf_row = jnp.sum(jnp.where(eye, lf_col, 0.0), axis=0, keepdims=True)
        b_col = jnp.sum(jnp.where(causal, lf_row, 0.0), axis=1, keepdims=True)
        b_row = jnp.sum(jnp.where(row <= col, lf_col, 0.0), axis=0, keepdims=True)
        m_prev = m_ref[:, hd:hd + 1]
        dmat = jnp.where(causal, b_col - b_row + i_row, -jnp.inf)
        inter = b_col + m_prev
        m_row = jnp.maximum(jnp.max(dmat, axis=1, keepdims=True), inter)
        w = jnp.exp(dmat - m_row)
        w_inter = jnp.exp(inter - m_row)
        s = _bdot_t(q, k) * w
        cm = c_ref[hd]
        nv = n_ref[hd]
        num = _bdot(s, v) + w_inter * _bdot_t(q, cm)
        den = jnp.sum(s, axis=1, keepdims=True) + w_inter * jnp.sum(q * nv, axis=1, keepdims=True)
        h_ref[:, lo:hi] = num / jnp.maximum(jnp.abs(den), jnp.exp(-m_row))
        b_last = b_col[L - 1:L, :]
        dec_col = b_last - b_col + i_col
        dec_row = b_last - b_row + i_row
        m_new = jnp.maximum(b_last + m_prev, jnp.max(dec_row, axis=1, keepdims=True))
        ws_col = jnp.exp(dec_col - m_new)
        wc = jnp.exp(b_last + m_prev - m_new)
        vw = (v * ws_col).astype(BF16)
        upd = lax.dot_general(vw, k.astype(BF16), (((0,), (0,)), ((), ())), preferred_element_type=F32)
        c_ref[hd] = wc * cm + upd
        n_ref[hd] = wc * nv + jnp.sum(k * ws_col, axis=0, keepdims=True)
        m_ref[:, hd:hd + 1] = m_new


def _mlstm(mq, mk, mv, small, nb, t_pad, t_valid, L, state=None):
    nc = t_pad // L
    has_state = state is not None
    blk = lambda b, c: (b * nc + c, 0)
    st4 = lambda b, c: (b, 0, 0, 0)
    st3 = lambda b, c: (b, 0, 0)
    in_specs = [pl.BlockSpec((L, M_WIDTH), blk)] * 3 + [pl.BlockSpec((L, LANES), blk)]
    args = [mq, mk, mv, small]
    if has_state:
        c0, n0, m0 = state
        in_specs += [pl.BlockSpec((None, M_HEADS, M_DH, M_DH), st4),
                     pl.BlockSpec((None, M_HEADS, 1, M_DH), st4),
                     pl.BlockSpec((None, 1, M_HEADS), st3)]
        args += [c0, n0.reshape(nb, M_HEADS, 1, M_DH), m0.reshape(nb, 1, M_HEADS)]
    out_specs = [pl.BlockSpec((L, M_WIDTH), blk),
                 pl.BlockSpec((None, M_HEADS, M_DH, M_DH), st4),
                 pl.BlockSpec((None, M_HEADS, 1, M_DH), st4),
                 pl.BlockSpec((None, 1, M_HEADS), st3)]
    out_shape = [jax.ShapeDtypeStruct((nb * t_pad, M_WIDTH), F32),
                 jax.ShapeDtypeStruct((nb, M_HEADS, M_DH, M_DH), F32),
                 jax.ShapeDtypeStruct((nb, M_HEADS, 1, M_DH), F32),
                 jax.ShapeDtypeStruct((nb, 1, M_HEADS), F32)]
    h, cs, ns, ms = pl.pallas_call(
        functools.partial(_mlstm_kernel, L=L, t_valid=t_valid, has_state=has_state),
        grid=(nb, nc),
        in_specs=in_specs,
        out_specs=out_specs,
        out_shape=out_shape,
        compiler_params=_cparams(("parallel", "arbitrary")),
        name="mlstm",
    )(*args)
    return h, cs, ns.reshape(nb, M_HEADS, M_DH), ms.reshape(nb, M_HEADS)


def _stack_heads(qt, g):
    t = qt.shape[0]
    z = jnp.zeros((t, A_DH), F32)
    parts = []
    for hh in range(A_HPG):
        hd = g * A_HPG + hh
        qh = qt[:, hd * A_DH:(hd + 1) * A_DH] * (ATT_SCALE * LOG2E)
        parts.append(jnp.concatenate([qh, z], axis=1) if g == 0 else jnp.concatenate([z, qh], axis=1))
    return jnp.concatenate(parts, axis=0).astype(BF16)


def _gate_cols(small, g, br):
    cols = []
    for hh in range(A_HPG):
        c0 = 2 * M_HEADS + (g * A_HPG + hh) * 3 + br
        cols.append(_sigmoid(small[:, c0:c0 + 1]))
    return jnp.concatenate(cols, axis=0)


def _compress(k_ref, v_ref, nseg, wbd_ref, pe_ref, kg0):
    acc_lo = jnp.zeros((nseg, 2 * LANES), F32)
    acc_hi = jnp.zeros((nseg, 2 * LANES), F32)
    for l in range(CMP_STRIDE):
        xl = jnp.concatenate([k_ref[pl.ds(l, nseg, stride=CMP_STRIDE), :],
                              v_ref[pl.ds(l, nseg, stride=CMP_STRIDE), :]], axis=1)
        acc_lo = acc_lo + _bdot(xl + pe_ref[l], wbd_ref[l])
        acc_hi = acc_hi + _bdot(xl + pe_ref[CMP_STRIDE + l], wbd_ref[CMP_STRIDE + l])
    kv = acc_lo + pltpu.roll(acc_hi, nseg - 1, 0)
    kc = kv[:, 0:LANES]
    vc = kv[:, LANES:2 * LANES]
    lane = lax.broadcasted_iota(jnp.int32, (nseg, LANES), 1)
    sq = kc * kc
    ms0 = jnp.sum(jnp.where(lane < A_DH, sq, 0.0), axis=1, keepdims=True) * (1.0 / A_DH)
    ms1 = jnp.sum(jnp.where(lane >= A_DH, sq, 0.0), axis=1, keepdims=True) * (1.0 / A_DH)
    ms = jnp.where(lane < A_DH, ms0, ms1)
    kc = kc * lax.rsqrt(ms + EPS) * kg0
    return kc, vc


def _cmp_branch(qn_g, kc_b, vc_b, tpos_rows, nseg, n_tok):
    s = _bdot_t(qn_g, kc_b)
    nidx = lax.broadcasted_iota(jnp.int32, (1, nseg), 1)
    vis = (nidx * CMP_STRIDE + (CMP_LEN - 1)) <= tpos_rows
    sm = jnp.where(vis, s, NEG_BIG)
    mx = jnp.max(sm, axis=1, keepdims=True)
    e = jnp.where(vis, jnp.exp2(sm - mx), 0.0)
    d = jnp.sum(e, axis=1, keepdims=True)
    p = e / jnp.where(d > 0, d, 1.0)
    o = _bdot(p, vc_b)
    imp = p[0:n_tok]
    for hh in range(1, A_HPG):
        imp = imp + p[hh * n_tok:(hh + 1) * n_tok]
    return o, imp


def _masked_attn_direct(q_g, k_parts, v_parts, allowed_parts):
    ss = [jnp.where(al, _bdot_t(q_g, kk), NEG_BIG) for kk, al in zip(k_parts, allowed_parts)]
    mx = ss[0].max(axis=1, keepdims=True)
    for s in ss[1:]:
        mx = jnp.maximum(mx, s.max(axis=1, keepdims=True))
    num = None
    den = None
    for s, al, vv in zip(ss, allowed_parts, v_parts):
        e = jnp.where(al, jnp.exp2(s - mx), 0.0)
        dd = jnp.sum(e, axis=1, keepdims=True)
        oo = _bdot(e, vv)
        num = oo if num is None else num + oo
        den = dd if den is None else den + dd
    return num / jnp.where(den > 0, den, 1.0)


def _assemble_heads(o_groups, n_tok):
    pieces = []
    for g in range(A_KV):
        for hh in range(A_HPG):
            pieces.append(o_groups[g][hh * n_tok:(hh + 1) * n_tok, g * A_DH:(g + 1) * A_DH])
    return jnp.concatenate(pieces, axis=1)


def _add_bias(s, bias):
    t, k = bias.shape
    return (s.reshape(A_HPG, t, k) + bias[None]).reshape(A_HPG * t, k)


def _nsa_prompt_kernel(q_ref, qr_ref, small_ref, rows_ref, win_ref, wbd_ref, pe_ref, kg0_ref,
                       pool_ref, o_ref,
                       kraw_sc, vraw_sc, kc_sc, vc_sc, m_sc, l_sc, acc_sc, *, T, tq, kc_len):
    qi = pl.program_id(1)
    nseg = T // CMP_STRIDE
    nsb = T // SEL_LEN

    @pl.when(qi == 0)
    def _():
        kraw_sc[...] = rows_ref[:, 0:LANES]
        vraw_sc[...] = rows_ref[:, LANES:2 * LANES]
        kc, vc = _compress(kraw_sc, vraw_sc, nseg, wbd_ref, pe_ref, kg0_ref[...])
        kc_sc[...] = kc
        vc_sc[...] = vc

    t0 = qi * tq
    tpos_col = t0 + lax.broadcasted_iota(jnp.int32, (tq, 1), 0)
    tpos_rows = jnp.concatenate([tpos_col] * A_HPG, axis=0)
    tpos_lane = t0 + lax.broadcasted_iota(jnp.int32, (1, tq), 1)
    q = q_ref[...]
    qr = qr_ref[...]
    small = small_ref[...]
    kc_b = kc_sc[...].astype(BF16)
    vc_b = vc_sc[...].astype(BF16)
    bidx = lax.broadcasted_iota(jnp.int32, (nsb, tq), 0)
    cur = tpos_lane // SEL_LEN
    r4 = A_HPG * tq
    qr_gs = [_stack_heads(qr, g) for g in range(A_KV)]
    o_cmps = []
    sel_bs = []
    for g in range(A_KV):
        qn_g = _stack_heads(q, g)
        o_cmp, imp = _cmp_branch(qn_g, kc_b, vc_b, tpos_rows, nseg, tq)
        o_cmps.append(o_cmp)
        imp_sel = _dot2_exact_rhs(imp, pool_ref[...])
        imp_t = jnp.transpose(imp_sel)[0:nsb, :]
        val = jnp.where(bidx < cur, imp_t, -1.0)
        rank = jnp.zeros((nsb, tq), F32)
        for bp in range(nsb):
            vb = val[bp:bp + 1, :]
            ahead = jnp.where(vb > val, 1.0, jnp.where((vb == val) & (bidx > bp), 1.0, 0.0))
            rank = rank + ahead
        sel_t = jnp.where(((rank < (N_SEL - 1)) & (bidx < cur)) | (bidx == cur), 1.0, 0.0)
        if nsb < LANES:
            sel_t = jnp.concatenate([sel_t, jnp.zeros((LANES - nsb, tq), F32)], axis=0)
        sel_bs.append(jnp.transpose(sel_t).astype(BF16))

    m_sc[...] = jnp.full(m_sc.shape, M_INIT, F32)
    l_sc[...] = jnp.zeros(l_sc.shape, F32)
    acc_sc[...] = jnp.zeros(acc_sc.shape, F32)

    def sel_body(c, carry):
        k0 = pl.multiple_of(c * kc_len, kc_len)
        kb = rows_ref[pl.ds(k0, kc_len), 2 * LANES:3 * LANES].astype(BF16)
        vb = rows_ref[pl.ds(k0, kc_len), 3 * LANES:4 * LANES].astype(BF16)
        kpos = k0 + lax.broadcasted_iota(jnp.int32, (1, kc_len), 1)
        causal = kpos <= tpos_col
        kblk = (k0 + lax.broadcasted_iota(jnp.int32, (LANES, kc_len), 1)) // SEL_LEN
        expand = jnp.where(kblk == lax.broadcasted_iota(jnp.int32, (LANES, kc_len), 0), 1.0, 0.0).astype(BF16)
        for g in range(A_KV):
            mk = jnp.dot(sel_bs[g], expand, preferred_element_type=F32)
            bias = jnp.where(causal, (mk - 1.0) * (-NEG_BIG), NEG_BIG)
            sm = _add_bias(_bdot_t(qr_gs[g], kb), bias)
            m_prev = m_sc[g, :, 0:1]
            m_new = jnp.maximum(m_prev, jnp.max(sm, axis=1, keepdims=True))
            alpha = jnp.exp2(m_prev - m_new)
            p = jnp.exp2(sm - m_new)
            l_new = alpha * l_sc[g, :, 0:1] + jnp.sum(p, axis=1, keepdims=True)
            acc_sc[g] = alpha * acc_sc[g] + _bdot(p, vb)
            m_sc[g] = jnp.broadcast_to(m_new, (r4, LANES))
            l_sc[g] = jnp.broadcast_to(l_new, (r4, LANES))
        return carry

    lax.fori_loop(0, (t0 + tq + kc_len - 1) // kc_len, sel_body, 0)

    wk = min(WINDOW + tq, T)
    w0 = pl.multiple_of(jnp.clip(t0 + tq - wk, 0, T - wk), tq)
    kw = win_ref[pl.ds(w0, wk), 0:LANES].astype(BF16)
    vw = win_ref[pl.ds(w0, wk), LANES:2 * LANES].astype(BF16)
    wdiff = tpos_col - (w0 + lax.broadcasted_iota(jnp.int32, (1, wk), 1))
    wbias = jnp.where((wdiff >= 0) & (wdiff < WINDOW), 0.0, NEG_BIG)

    o_groups = []
    for g in range(A_KV):
        l = l_sc[g, :, 0:1]
        o_sel = acc_sc[g] / jnp.where(l > 0, l, 1.0)
        sw = _add_bias(_bdot_t(qr_gs[g], kw), wbias)
        pw = jnp.exp2(sw - jnp.max(sw, axis=1, keepdims=True))
        o_win = _bdot(pw, vw) / jnp.sum(pw, axis=1, keepdims=True)
        o_groups.append(_gate_cols(small, g, 0) * o_cmps[g] + _gate_cols(small, g, 1) * o_sel
                        + _gate_cols(small, g, 2) * o_win)
    o_ref[...] = _assemble_heads(o_groups, tq)


def _nsa_prompt(q, qr, small, rows, win, wbd, pe, kg0, nb, T):
    tq = 128
    kc_len = _pick_tile(T, 512)
    nq = T // tq
    nseg = T // CMP_STRIDE
    nsb = T // SEL_LEN
    pool = (jnp.arange(nseg)[:, None] // (SEL_LEN // CMP_STRIDE) == jnp.arange(LANES)[None, :]).astype(BF16)
    tile = lambda b, i: (b * nq + i, 0)
    per_b = lambda b, i: (b, 0)
    c2 = lambda b, i: (0, 0)
    c3 = lambda b, i: (0, 0, 0)
    r4 = A_HPG * tq
    return pl.pallas_call(
        functools.partial(_nsa_prompt_kernel, T=T, tq=tq, kc_len=kc_len),
        grid=(nb, nq),
        in_specs=[pl.BlockSpec((tq, A_WIDTH), tile), pl.BlockSpec((tq, A_WIDTH), tile),
                  pl.BlockSpec((tq, LANES), tile),
                  pl.BlockSpec((T, 4 * LANES), per_b), pl.BlockSpec((T, 2 * LANES), per_b),
                  pl.BlockSpec(wbd.shape, c3), pl.BlockSpec(pe.shape, c3), pl.BlockSpec(kg0.shape, c2),
                  pl.BlockSpec(pool.shape, c2)],
        out_specs=pl.BlockSpec((tq, A_WIDTH), tile),
        out_shape=jax.ShapeDtypeStruct((nb * T, A_WIDTH), F32),
        scratch_shapes=[pltpu.VMEM((T, LANES), F32), pltpu.VMEM((T, LANES), F32),
                        pltpu.VMEM((nseg, LANES), F32), pltpu.VMEM((nseg, LANES), F32),
                        pltpu.VMEM((A_KV, r4, LANES), F32), pltpu.VMEM((A_KV, r4, LANES), F32),
                        pltpu.VMEM((A_KV, r4, LANES), F32)],
        compiler_params=_cparams(("parallel", "arbitrary")),
        name="nsa_prompt",
    )(q, qr, small, rows, win, wbd, pe, kg0, pool)


def _nsa_sample_kernel(pt_ref, cache_ref, q_ref, qr_ref, small_ref, rows_ref, winnew_ref, winbuf_ref,
                       wbd_ref, pe_ref, kg0_ref, pool_ref, expand_ref,
                       o_ref, winout_ref,
                       cmp_buf, sel_buf, sems, *, n_pages, past_len, t_valid):
    b = pl.program_id(0)
    nb = pl.num_programs(0)
    tp = SAMPLE_PAD_T
    nseg = past_len // CMP_STRIDE
    nsb = past_len // SEL_LEN
    wbuf = winbuf_ref.shape[0]

    def page_copies(bb, p, phase):
        page = pt_ref[bb * n_pages + p]
        dst_rows = pl.ds(p * PAGE_SIZE, PAGE_SIZE)
        if phase == 0:
            return [pltpu.make_async_copy(cache_ref.at[page, :, pl.ds(j * LANES, LANES)],
                                          cmp_buf.at[j, dst_rows, :], sems.at[0]) for j in range(2)]
        return [pltpu.make_async_copy(cache_ref.at[page, :, pl.ds(2 * LANES, 2 * LANES)],
                                      sel_buf.at[dst_rows, :], sems.at[1])]

    def start_all(bb, phase):
        def body(p, c):
            for cp in page_copies(bb, p, phase):
                cp.start()
            return c
        lax.fori_loop(0, n_pages, body, 0)

    def wait_all(bb, phase):
        def body(p, c):
            for cp in page_copies(bb, p, phase):
                cp.wait()
            return c
        lax.fori_loop(0, n_pages, body, 0)

    @pl.when(b == 0)
    def _():
        start_all(b, 0)

    start_all(b, 1)
    wait_all(b, 0)

    kc, vc = _compress(cmp_buf.at[0], cmp_buf.at[1], nseg, wbd_ref, pe_ref, kg0_ref[...])
    kc_b = kc.astype(BF16)
    vc_b = vc.astype(BF16)
    q = q_ref[...]
    qr = qr_ref[...]
    small = small_ref[...]
    tpos_col = past_len + lax.broadcasted_iota(jnp.int32, (tp, 1), 0)
    tpos_rows = jnp.concatenate([tpos_col] * A_HPG, axis=0)
    bp_idx = lax.broadcasted_iota(jnp.int32, (nsb, nsb), 0)
    b_idx = lax.broadcasted_iota(jnp.int32, (nsb, nsb), 1)
    o_cmps = []
    sels = []
    for g in range(A_KV):
        qn_g = _stack_heads(q, g)
        o_cmp, imp = _cmp_branch(qn_g, kc_b, vc_b, tpos_rows, nseg, tp)
        o_cmps.append(o_cmp)
        imp_sel = _dot2_exact_rhs(imp, pool_ref[...])
        imp_pad = jnp.concatenate([imp_sel, jnp.zeros((nsb - tp, nsb), F32)], axis=0)
        imp_t = jnp.transpose(imp_pad)
        rows_sel = []
        for t in range(tp):
            if t < t_valid:
                row_t = imp_sel[t:t + 1, :]
                col_t = imp_t[:, t:t + 1]
                ahead = jnp.where(col_t > row_t, 1.0, jnp.where((col_t == row_t) & (bp_idx < b_idx), 1.0, 0.0))
                rank = jnp.sum(ahead, axis=0, keepdims=True)
                rows_sel.append(jnp.where(rank < (N_SEL - 1), 1.0, 0.0))
            else:
                rows_sel.append(jnp.zeros((1, nsb), F32))
        sels.append(jnp.concatenate(rows_sel, axis=0).astype(BF16))

    @pl.when(b + 1 < nb)
    def _():
        start_all(b + 1, 0)

    wait_all(b, 1)

    new_idx = lax.broadcasted_iota(jnp.int32, (tp, tp), 1)
    tok_idx = lax.broadcasted_iota(jnp.int32, (tp, tp), 0)
    new_ok = jnp.concatenate([jnp.where(new_idx <= tok_idx, 1.0, 0.0)] * A_HPG, axis=0) > 0.5
    wpos = past_len - wbuf + lax.broadcasted_iota(jnp.int32, (1, wbuf), 1)
    wdiff = tpos_col - wpos
    win_ok = jnp.concatenate([jnp.where((wdiff >= 0) & (wdiff < WINDOW), 1.0, 0.0)] * A_HPG, axis=0) > 0.5
    k_past = sel_buf[:, 0:LANES].astype(BF16)
    v_past = sel_buf[:, LANES:2 * LANES].astype(BF16)
    k_new = rows_ref[:, 2 * LANES:3 * LANES]
    v_new = rows_ref[:, 3 * LANES:4 * LANES]
    kw_past = winbuf_ref[:, 0:LANES]
    vw_past = winbuf_ref[:, LANES:2 * LANES]
    kw_new = winnew_ref[:, 0:LANES]
    vw_new = winnew_ref[:, LANES:2 * LANES]
    o_groups = []
    for g in range(A_KV):
        qr_g = _stack_heads(qr, g)
        mk = jnp.dot(sels[g], expand_ref[...], preferred_element_type=F32)
        past_ok = jnp.concatenate([mk] * A_HPG, axis=0) > 0.5
        o_sel = _masked_attn_direct(qr_g, [k_past, k_new], [v_past, v_new], [past_ok, new_ok])
        o_win = _masked_attn_direct(qr_g, [kw_past, kw_new], [vw_past, vw_new], [win_ok, new_ok])
        o_groups.append(_gate_cols(small, g, 0) * o_cmps[g] + _gate_cols(small, g, 1) * o_sel
                        + _gate_cols(small, g, 2) * o_win)
    o_ref[...] = _assemble_heads(o_groups, tp)

    wb = winbuf_ref[...]
    rolled = pltpu.roll(wb, wbuf - t_valid, 0)
    newr = pltpu.roll(winnew_ref[...], tp - t_valid, 0)
    sub = lax.broadcasted_iota(jnp.int32, (tp, 2 * LANES), 0)
    winout_ref[0:wbuf - tp, :] = rolled[0:wbuf - tp, :]
    winout_ref[wbuf - tp:wbuf, :] = jnp.where(sub < tp - t_valid, rolled[wbuf - tp:wbuf, :], newr)


def _nsa_sample(page_table, cache, q, qr, small, rows, winnew, winbuf, wbd, pe, kg0, t_valid):
    nb, n_pages = page_table.shape
    past_len = n_pages * PAGE_SIZE
    nseg = past_len // CMP_STRIDE
    nsb = past_len // SEL_LEN
    tp = SAMPLE_PAD_T
    wbuf = winbuf.shape[1]
    pool = (jnp.arange(nseg)[:, None] // (SEL_LEN // CMP_STRIDE) == jnp.arange(nsb)[None, :]).astype(BF16)
    expand = (jnp.arange(nsb)[:, None] == jnp.arange(past_len)[None, :] // SEL_LEN).astype(BF16)
    tile = lambda b, pt: (b, 0)
    c2 = lambda b, pt: (0, 0)
    c3 = lambda b, pt: (0, 0, 0)
    gs = pltpu.PrefetchScalarGridSpec(
        num_scalar_prefetch=1,
        grid=(nb,),
        in_specs=[pl.BlockSpec(memory_space=pl.ANY),
                  pl.BlockSpec((tp, A_WIDTH), tile), pl.BlockSpec((tp, A_WIDTH), tile),
                  pl.BlockSpec((tp, LANES), tile), pl.BlockSpec((tp, 4 * LANES), tile),
                  pl.BlockSpec((tp, 2 * LANES), tile),
                  pl.BlockSpec((None, wbuf, 2 * LANES), lambda b, pt: (b, 0, 0)),
                  pl.BlockSpec(wbd.shape, c3), pl.BlockSpec(pe.shape, c3), pl.BlockSpec(kg0.shape, c2),
                  pl.BlockSpec(pool.shape, c2), pl.BlockSpec(expand.shape, c2)],
        out_specs=[pl.BlockSpec((tp, A_WIDTH), tile),
                   pl.BlockSpec((None, wbuf, 2 * LANES), lambda b, pt: (b, 0, 0))],
        scratch_shapes=[pltpu.VMEM((2, past_len, LANES), F32), pltpu.VMEM((past_len, 2 * LANES), F32),
                        pltpu.SemaphoreType.DMA((2,))],
    )
    return pl.pallas_call(
        functools.partial(_nsa_sample_kernel, n_pages=n_pages, past_len=past_len, t_valid=t_valid),
        grid_spec=gs,
        out_shape=[jax.ShapeDtypeStruct((nb * tp, A_WIDTH), F32),
                   jax.ShapeDtypeStruct((nb, wbuf, 2 * LANES), F32)],
        compiler_params=_cparams(("arbitrary",)),
        name="nsa_sample",
    )(page_table.reshape(-1), cache, q, qr, small, rows, winnew, winbuf, wbd, pe, kg0, pool, expand)


def _mixout_kernel(x_ref, hm_ref, on_ref, mod_ref, gmix_ref, gffn_ref,
                   wog_ref, bog_ref, wum_ref, wua_ref, wout_ref, wr_ref, br_ref,
                   x1_ref, h2_ref, lg_ref):
    d = D_MODEL
    x = x_ref[...]
    sh1, sc1, gt1 = mod_ref[:, 0:d], mod_ref[:, d:2 * d], mod_ref[:, 2 * d:3 * d]
    sh2, sc2 = mod_ref[:, 3 * d:4 * d], mod_ref[:, 4 * d:5 * d]
    h = _rmsnorm_rows(x, gmix_ref[...]) * (1.0 + sc1) + sh1
    hb = h.astype(BF16)
    mo = jnp.dot(hb, wog_ref[:, 0:M_WIDTH], preferred_element_type=F32) + bog_ref[:, 0:M_WIDTH]
    ym = _bdot(_sigmoid(mo) * hm_ref[...], wum_ref[...])
    ya = _bdot(on_ref[...], wua_ref[...])
    ga = jnp.dot(hb, wog_ref[:, M_WIDTH:M_WIDTH + d], preferred_element_type=F32) + bog_ref[:, M_WIDTH:M_WIDTH + d]
    u = _sigmoid(ga) * ym
    gb = (jnp.dot(hb, wog_ref[:, M_WIDTH + d:M_WIDTH + 2 * d], preferred_element_type=F32)
          + bog_ref[:, M_WIDTH + d:M_WIDTH + 2 * d])
    u = u + _sigmoid(gb) * ya
    x1 = x + gt1 * _bdot(u, wout_ref[...])
    x1_ref[...] = x1
    h2 = _rmsnorm_rows(x1, gffn_ref[...]) * (1.0 + sc2) + sh2
    h2_ref[...] = h2.astype(BF16)
    lg_ref[...] = _dot3(h2, wr_ref[...]) + br_ref[...]


def _mixout(x2, hm, on, mod3, gmix, gffn, wts, tm, tiles_per_mod):
    m = x2.shape[0]
    (wog, bog, wum, wua, wout, wr, br) = wts
    r = mod3.shape[1]
    row = lambda i: (i, 0)
    const = lambda i: (0, 0)
    return pl.pallas_call(
        _mixout_kernel,
        grid=(m // tm,),
        in_specs=[pl.BlockSpec((tm, D_MODEL), row), pl.BlockSpec((tm, M_WIDTH), row),
                  pl.BlockSpec((tm, A_WIDTH), row),
                  pl.BlockSpec((None, r, 6 * D_MODEL), lambda i: (i // tiles_per_mod, 0, 0)),
                  pl.BlockSpec((1, D_MODEL), const), pl.BlockSpec((1, D_MODEL), const),
                  pl.BlockSpec(wog.shape, const), pl.BlockSpec(bog.shape, const),
                  pl.BlockSpec(wum.shape, const), pl.BlockSpec(wua.shape, const),
                  pl.BlockSpec(wout.shape, const), pl.BlockSpec(wr.shape, const),
                  pl.BlockSpec(br.shape, const)],
        out_specs=[pl.BlockSpec((tm, D_MODEL), row), pl.BlockSpec((tm, D_MODEL), row),
                   pl.BlockSpec((tm, LANES), row)],
        out_shape=[jax.ShapeDtypeStruct((m, D_MODEL), F32), jax.ShapeDtypeStruct((m, D_MODEL), BF16),
                   jax.ShapeDtypeStruct((m, LANES), F32)],
        compiler_params=_cparams(("parallel",)),
        name="mixout",
    )(x2, hm, on, mod3, gmix, gffn, wog, bog, wum, wua, wout, wr, br)


MOE_BM = 256
MOE_CH = 512


def _moe_kernel(be_ref, na_ref, xs_ref, wgu_ref, bgu_ref, wdn_ref, bdn_ref, y_ref, wgu_bf, wdn_bf):
    i = pl.program_id(0)
    e = be_ref[i]
    prev = be_ref[jnp.maximum(i - 1, 0)]

    @pl.when((i == 0) | (e != prev))
    def _():
        for j in range(2 * D_EXPERT // MOE_CH):
            wgu_bf[:, j * MOE_CH:(j + 1) * MOE_CH] = wgu_ref[:, j * MOE_CH:(j + 1) * MOE_CH].astype(BF16)
        for j in range(D_EXPERT // MOE_CH):
            wdn_bf[j * MOE_CH:(j + 1) * MOE_CH, :] = wdn_ref[j * MOE_CH:(j + 1) * MOE_CH, :].astype(BF16)

    @pl.when(i < na_ref[0])
    def _():
        xb = xs_ref[...]
        acc = jnp.zeros(y_ref.shape, F32) + bdn_ref[...]
        for j in range(D_EXPERT // MOE_CH):
            lo, hi = j * MOE_CH, (j + 1) * MOE_CH
            gj = jnp.dot(xb, wgu_bf[:, lo:hi], preferred_element_type=F32) + bgu_ref[:, lo:hi]
            uj = (jnp.dot(xb, wgu_bf[:, D_EXPERT + lo:D_EXPERT + hi], preferred_element_type=F32)
                  + bgu_ref[:, D_EXPERT + lo:D_EXPERT + hi])
            gj = jnp.minimum(gj, SWIGLU_LIMIT)
            uj = jnp.clip(uj, -SWIGLU_LIMIT, SWIGLU_LIMIT)
            act = gj * _sigmoid(SWIGLU_ALPHA * gj) * (uj + 1.0)
            acc = acc + jnp.dot(act.astype(BF16), wdn_bf[lo:hi, :], preferred_element_type=F32)
        y_ref[...] = acc

    @pl.when(i >= na_ref[0])
    def _():
        y_ref[...] = jnp.zeros(y_ref.shape, F32)


def _moe_experts(block_e, n_active, xs, w_gu, b_gu, w_dn, b_dn):
    p = xs.shape[0]
    nblk = p // MOE_BM
    gs = pltpu.PrefetchScalarGridSpec(
        num_scalar_prefetch=2,
        grid=(nblk,),
        in_specs=[pl.BlockSpec((MOE_BM, D_MODEL), lambda i, be, na: (i, 0)),
                  pl.BlockSpec((None, D_MODEL, 2 * D_EXPERT), lambda i, be, na: (be[i], 0, 0)),
                  pl.BlockSpec((None, 1, 2 * D_EXPERT), lambda i, be, na: (be[i], 0, 0)),
                  pl.BlockSpec((None, D_EXPERT, D_MODEL), lambda i, be, na: (be[i], 0, 0)),
                  pl.BlockSpec((None, 1, D_MODEL), lambda i, be, na: (be[i], 0, 0))],
        out_specs=pl.BlockSpec((MOE_BM, D_MODEL), lambda i, be, na: (i, 0)),
        scratch_shapes=[pltpu.VMEM((D_MODEL, 2 * D_EXPERT), BF16), pltpu.VMEM((D_EXPERT, D_MODEL), BF16)],
    )
    return pl.pallas_call(
        _moe_kernel,
        grid_spec=gs,
        out_shape=jax.ShapeDtypeStruct((p, D_MODEL), F32),
        compiler_params=_cparams(("arbitrary",)),
        name="moe_experts",
    )(block_e, n_active, xs, w_gu, b_gu.reshape(N_EXPERTS, 1, -1), w_dn, b_dn.reshape(N_EXPERTS, 1, -1))


def _moe(h2, logits, w_gu, b_gu, w_dn, b_dn):
    m = h2.shape[0]
    top_v, top_e = lax.top_k(logits[:, :N_EXPERTS], TOP_K)
    gate = jax.nn.softmax(top_v, axis=-1)
    mk = m * TOP_K
    flat_e = top_e.reshape(mk)
    order = jnp.argsort(flat_e)
    sorted_e = flat_e[order]
    counts = jnp.bincount(flat_e, length=N_EXPERTS)
    padded = (counts + MOE_BM - 1) // MOE_BM * MOE_BM
    pad_end = jnp.cumsum(padded)
    pad_start = pad_end - padded
    start = jnp.cumsum(counts) - counts
    dest = (pad_start[sorted_e] + jnp.arange(mk) - start[sorted_e]).astype(jnp.int32)
    n_blocks = (mk + N_EXPERTS * (MOE_BM - 1) + MOE_BM - 1) // MOE_BM
    src_tok = jnp.full((n_blocks * MOE_BM,), m, jnp.int32).at[dest].set((order // TOP_K).astype(jnp.int32))
    block_e = jnp.minimum(jnp.searchsorted(pad_end, jnp.arange(n_blocks) * MOE_BM, side='right'),
                          N_EXPERTS - 1).astype(jnp.int32)
    n_active = (pad_end[-1] // MOE_BM).astype(jnp.int32).reshape(1)
    hp = jnp.concatenate([h2, jnp.zeros((1, D_MODEL), h2.dtype)], axis=0)
    xs = hp[src_tok]
    ys = _moe_experts(block_e, n_active, xs, w_gu, b_gu, w_dn, b_dn)
    slot_dest = jnp.zeros((mk,), jnp.int32).at[order].set(dest)
    y_slots = ys[slot_dest].reshape(m, TOP_K, D_MODEL)
    return jnp.einsum('mkd,mk->md', y_slots, gate)


def _rope_tables(pos):
    half = ROT_DIM // 2
    inv = ROPE_THETA ** (-jnp.arange(half, dtype=F32) * (2.0 / ROT_DIM))
    ang = pos.astype(F32)[:, None] * inv[None, :]
    cos, sin = jnp.cos(ang), jnp.sin(ang)
    n = pos.shape[0]
    ones = jnp.ones((n, A_DH - ROT_DIM), F32)
    zeros_h = jnp.zeros((n, half), F32)
    zeros_r = jnp.zeros((n, A_DH - ROT_DIM), F32)
    cos64 = jnp.concatenate([cos, cos, ones], axis=1)
    sprev64 = jnp.concatenate([zeros_h, sin, zeros_r], axis=1)
    snext64 = jnp.concatenate([-sin, zeros_h, zeros_r], axis=1)
    two = lambda a: jnp.concatenate([a, a], axis=1)
    return two(cos64), two(sprev64), two(snext64)


def _prep_weights(w_in, b_in, q_norm_g, k_norm_g, cmp_pe_k, cmp_pe_v, cmp_w_k, cmp_w_v,
                  w_up_m, w_up_a, w_out, w_router, b_router):
    b2 = b_in.reshape(1, N_IN)
    wm = w_in[:, OFF_MQ:OFF_MO].astype(BF16)
    bm = b2[:, OFF_MQ:OFF_MO]
    wq = w_in[:, OFF_AQ:OFF_AKV].astype(BF16)
    bq = b2[:, OFF_AQ:OFF_AKV]
    wkv = w_in[:, OFF_AKV:OFF_AG].astype(BF16)
    bkv = b2[:, OFF_AKV:OFF_AG]
    n_small = 2 * M_HEADS + 3 * A_HEADS
    ws = jnp.concatenate([w_in[:, OFF_MI:OFF_AQ], w_in[:, OFF_AG:OFF_GA],
                          jnp.zeros((D_MODEL, LANES - n_small), F32)], axis=1)
    bs = jnp.concatenate([b2[:, OFF_MI:OFF_AQ], b2[:, OFF_AG:OFF_GA], jnp.zeros((1, LANES - n_small), F32)], axis=1)
    qg = jnp.tile(q_norm_g, A_HEADS).reshape(1, A_WIDTH)
    kg = jnp.stack([jnp.tile(k_norm_g[1], A_KV), jnp.tile(k_norm_g[2], A_KV)], axis=0)
    kg0 = jnp.tile(k_norm_g[0], A_KV).reshape(1, LANES)
    hid = jnp.arange(A_WIDTH) // A_DH
    bd = jnp.where(hid[:, None] == hid[None, :], 1.0 / A_DH, 0.0).astype(BF16)
    inproj_w = (wm, bm, wq, bq, wkv, bkv, ws, bs, qg, kg, bd)

    z = jnp.zeros((CMP_LEN, A_DH, A_DH), F32)
    r0 = jnp.concatenate([cmp_w_k, z, z, z], axis=2)
    r1 = jnp.concatenate([z, cmp_w_k, z, z], axis=2)
    r2 = jnp.concatenate([z, z, cmp_w_v, z], axis=2)
    r3 = jnp.concatenate([z, z, z, cmp_w_v], axis=2)
    wbd = jnp.concatenate([r0, r1, r2, r3], axis=1).astype(BF16)
    pe = jnp.concatenate([cmp_pe_k, cmp_pe_k, cmp_pe_v, cmp_pe_v], axis=1).reshape(CMP_LEN, 1, 2 * LANES)

    wog = jnp.concatenate([w_in[:, OFF_MO:OFF_MI], w_in[:, OFF_GA:N_IN]], axis=1).astype(BF16)
    bog = jnp.concatenate([b2[:, OFF_MO:OFF_MI], b2[:, OFF_GA:N_IN]], axis=1)
    wr = jnp.concatenate([w_router, jnp.zeros((D_MODEL, LANES - N_EXPERTS), F32)], axis=1)
    br = jnp.concatenate([b_router, jnp.zeros((LANES - N_EXPERTS,), F32)]).reshape(1, LANES)
    mixout_w = (wog, bog, w_up_m.astype(BF16), w_up_a.astype(BF16), w_out.astype(BF16), wr, br)
    return inproj_w, (wbd, pe, kg0), mixout_w


def _pick_tile(m, pref):
    t = pref
    while m % t:
        t //= 2
    return t


def kernel(x_prompt, x_sample, cache_nsa_kv, state_win_kv, state_mlstm_C, state_mlstm_n, state_mlstm_m, page_table, c_prompt, c_sample, w_ada, b_ada, g_mix, g_ffn, w_in, b_in, q_norm_g, k_norm_g, cmp_pe_k, cmp_pe_v, cmp_w_k, cmp_w_v, w_up_m, w_up_a, w_out, w_router, b_router, w_gu, b_gu, w_dn, b_dn):
    depth = w_in.shape[0]
    assert depth == 1
    B, T, D = x_prompt.shape
    DB, TS, _ = x_sample.shape
    n_pages = page_table.shape[1]
    past_len = n_pages * PAGE_SIZE
    wbuf = state_win_kv.shape[2]
    tp = SAMPLE_PAD_T
    assert TS <= tp and wbuf % tp == 0 and T % 128 == 0

    l = 0
    inproj_w, cmp_w, mixout_w = _prep_weights(
        w_in[l], b_in[l], q_norm_g[l], k_norm_g[l], cmp_pe_k[l], cmp_pe_v[l], cmp_w_k[l], cmp_w_v[l],
        w_up_m[l], w_up_a[l], w_out[l], w_router[l], b_router[l])
    wbd, pe, kg0 = cmp_w
    gmix = g_mix[l].reshape(1, D)
    gffn = g_ffn[l].reshape(1, D)

    nc = B + DB
    nc_pad = -(-nc // SUBLANES) * SUBLANES
    c_all = jnp.concatenate([c_prompt, c_sample, jnp.zeros((nc_pad - nc, D), F32)], axis=0)
    mod = _adaln(c_all, w_ada[l], b_ada[l])
    mod_p = mod[:B].reshape(B, 1, 6 * D)
    mod_s = jnp.repeat(mod[B:B + DB], tp, axis=0).reshape(1, DB * tp, 6 * D)

    mp = B * T
    tm = _pick_tile(T, 256)
    xp = x_prompt.reshape(mp, D)
    tabs_p = _rope_tables(jnp.arange(T, dtype=jnp.int32))
    mq, mk, mv, q, qr, rows, win, small = _inproj(xp, mod_p, gmix, tabs_p, inproj_w, tm, T // tm, T // tm)
    Lp = _pick_tile(T, 128)
    hm, C_p, n_p, m_p = _mlstm(mq, mk, mv, small, B, T, T, Lp)
    o_nsa = _nsa_prompt(q, qr, small, rows, win, wbd, pe, kg0, B, T)
    x1_p, h2_p, lg_p = _mixout(xp, hm, o_nsa, mod_p, gmix, gffn, mixout_w, tm, T // tm)

    ms = DB * tp
    xs_pad = jnp.concatenate([x_sample, jnp.zeros((DB, tp - TS, D), F32)], axis=1).reshape(ms, D)
    pos_s = past_len + jnp.tile(jnp.arange(tp, dtype=jnp.int32), DB)
    tabs_s = _rope_tables(pos_s)
    mq_s, mk_s, mv_s, q_s, qr_s, rows_s, win_s, small_s = _inproj(xs_pad, mod_s, gmix, tabs_s, inproj_w, ms, 1, 1)
    hm_s, C_s, n_s, m_s = _mlstm(mq_s, mk_s, mv_s, small_s, DB, tp, TS, tp,
                                 state=(state_mlstm_C[l], state_mlstm_n[l], state_mlstm_m[l]))
    cache2 = cache_nsa_kv[l].reshape(cache_nsa_kv.shape[1], PAGE_SIZE, 4 * LANES)
    winbuf = state_win_kv[l].reshape(DB, wbuf, 2 * LANES)
    o_nsa_s, win_out_s = _nsa_sample(page_table, cache2, q_s, qr_s, small_s, rows_s, win_s, winbuf,
                                     wbd, pe, kg0, TS)
    x1_s, h2_s, lg_s = _mixout(xs_pad, hm_s, o_nsa_s, mod_s, gmix, gffn, mixout_w, ms, 1)

    valid = lambda a: a.reshape(DB, tp, -1)[:, :TS].reshape(DB * TS, -1)
    h2_all = jnp.concatenate([h2_p, valid(h2_s)], axis=0)
    lg_all = jnp.concatenate([lg_p, valid(lg_s)], axis=0)
    moe = _moe(h2_all, lg_all, w_gu[l], b_gu[l], w_dn[l], b_dn[l])
    gt2_p = mod[:B, 5 * D:6 * D][:, None, :]
    gt2_s = mod[B:B + DB, 5 * D:6 * D][:, None, :]
    y_p = x1_p.reshape(B, T, D) + gt2_p * moe[:mp].reshape(B, T, D)
    y_s = valid(x1_s).reshape(DB, TS, D) + gt2_s * moe[mp:].reshape(DB, TS, D)

    kv_p = rows.reshape(1, B, T, 4, A_KV, A_DH)
    kv_s = valid(rows_s).reshape(1, DB, TS, 4, A_KV, A_DH)
    wp = min(WINDOW, T)
    win_p = win.reshape(B, T, 2, A_KV, A_DH)[:, T - wp:][None]
    win_s_out = win_out_s.reshape(1, DB, wbuf, 2, A_KV, A_DH)
    return (y_p, y_s, kv_p, kv_s, win_p, win_s_out,
            C_p[None], n_p[None], m_p[None], C_s[None], n_s[None], m_s[None])
```

```python
import functools
import math

import jax
import jax.numpy as jnp
from jax import lax
from jax.experimental import pallas as pl
from jax.experimental.pallas import tpu as pltpu

F32 = jnp.float32
BF16 = jnp.bfloat16

D_MODEL = 1024
M_HEADS = 4
M_DH = 128
M_WIDTH = M_HEADS * M_DH
A_HEADS = 8
A_KV = 2
A_HPG = A_HEADS // A_KV
A_DH = 64
A_WIDTH = A_HEADS * A_DH
CMP_STRIDE = 16
CMP_LEN = 32
SEL_LEN = 64
N_SEL = 16
WINDOW = 512
PAGE_SIZE = 128
ROPE_THETA = 500000.0
ROT_DIM = A_DH // 4
ATT_SCALE = A_DH ** -0.5
N_EXPERTS = 32
TOP_K = 4
D_EXPERT = D_MODEL
SWIGLU_LIMIT = 7.0
SWIGLU_ALPHA = 1.702
EPS = 1e-6

OFF_MQ, OFF_MK, OFF_MV, OFF_MO = 0, M_WIDTH, 2 * M_WIDTH, 3 * M_WIDTH
OFF_MI = 4 * M_WIDTH
OFF_MF = OFF_MI + M_HEADS
OFF_AQ = OFF_MF + M_HEADS
OFF_AKV = OFF_AQ + A_WIDTH
OFF_AG = OFF_AKV + 6 * A_KV * A_DH
OFF_GA = OFF_AG + 3 * A_HEADS
OFF_GB = OFF_GA + D_MODEL
N_IN = OFF_GB + D_MODEL

LANES = 128
SUBLANES = 8
VMEM_LIMIT = 56 * 1024 * 1024

NEG_BIG = -1e30
M_INIT = -1e29
LOG2E = 1.4426950408889634
SAMPLE_PAD_T = 8


def _cparams(sem):
    return pltpu.CompilerParams(dimension_semantics=sem, vmem_limit_bytes=VMEM_LIMIT)


def _bdot(a, b):
    return jnp.dot(a.astype(BF16), b.astype(BF16), preferred_element_type=F32)


def _bdot_t(a, b):
    return lax.dot_general(a.astype(BF16), b.astype(BF16), (((1,), (1,)), ((), ())),
                           preferred_element_type=F32)


def _split(a):
    hi = a.astype(BF16)
    lo = (a - hi.astype(F32)).astype(BF16)
    return hi, lo


def _dot3(a, b):
    ah, al = _split(a)
    bh, bl = _split(b)
    return (jnp.dot(ah, bh, preferred_element_type=F32) + jnp.dot(al, bh, preferred_element_type=F32)
            + jnp.dot(ah, bl, preferred_element_type=F32))


def _dot2_exact_rhs(a, b_bf16):
    ah, al = _split(a)
    return jnp.dot(ah, b_bf16, preferred_element_type=F32) + jnp.dot(al, b_bf16, preferred_element_type=F32)


def _sigmoid(x):
    return 1.0 / (1.0 + jnp.exp(-x))


def _rmsnorm_rows(x, g):
    return x * lax.rsqrt(jnp.mean(x * x, axis=-1, keepdims=True) + EPS) * g


def _adaln_kernel(c_ref, w_ref, b_ref, o_ref):
    c = c_ref[...]
    s = c * _sigmoid(c)
    o_ref[...] = _dot3(s, w_ref[...]) + b_ref[...]


def _adaln(c, w, b):
    mc, d = c.shape
    n = w.shape[1]
    tn = 1024
    return pl.pallas_call(
        _adaln_kernel,
        grid=(n // tn,),
        in_specs=[pl.BlockSpec((mc, d), lambda j: (0, 0)),
                  pl.BlockSpec((d, tn), lambda j: (0, j)),
                  pl.BlockSpec((1, tn), lambda j: (0, j))],
        out_specs=pl.BlockSpec((mc, tn), lambda j: (0, j)),
        out_shape=jax.ShapeDtypeStruct((mc, n), F32),
        compiler_params=_cparams(("parallel",)),
        name="adaln",
    )(c, w, b.reshape(1, n))


def _head_norm(z, bd, gain):
    ms = _dot2_exact_rhs(z * z, bd)
    return z * lax.rsqrt(ms + EPS) * gain


def _rope(z, cos, s_prev, s_next):
    w = z.shape[1]
    rep = w // LANES
    if rep > 1:
        cos = jnp.concatenate([cos] * rep, axis=1)
        s_prev = jnp.concatenate([s_prev] * rep, axis=1)
        s_next = jnp.concatenate([s_next] * rep, axis=1)
    z_prev = pltpu.roll(z, ROT_DIM // 2, 1)
    z_next = pltpu.roll(z, w - ROT_DIM // 2, 1)
    return z * cos + z_prev * s_prev + z_next * s_next


def _inproj_kernel(x_ref, mod_ref, gmix_ref, cos_ref, sp_ref, sn_ref,
                   wm_ref, bm_ref, wq_ref, bq_ref, wkv_ref, bkv_ref, ws_ref, bs_ref,
                   qg_ref, kg_ref, bd_ref,
                   mq_ref, mk_ref, mv_ref, q_ref, qr_ref, rows_ref, win_ref, small_ref):
    x = x_ref[...]
    sh1 = mod_ref[:, 0:D_MODEL]
    sc1 = mod_ref[:, D_MODEL:2 * D_MODEL]
    h = _rmsnorm_rows(x, gmix_ref[...]) * (1.0 + sc1) + sh1
    hb = h.astype(BF16)

    mq_ref[...] = jnp.dot(hb, wm_ref[:, 0:M_WIDTH], preferred_element_type=F32) + bm_ref[:, 0:M_WIDTH]
    mk = jnp.dot(hb, wm_ref[:, M_WIDTH:2 * M_WIDTH], preferred_element_type=F32) + bm_ref[:, M_WIDTH:2 * M_WIDTH]
    mk_ref[...] = mk * (M_DH ** -0.5)
    mv_ref[...] = (jnp.dot(hb, wm_ref[:, 2 * M_WIDTH:3 * M_WIDTH], preferred_element_type=F32)
                   + bm_ref[:, 2 * M_WIDTH:3 * M_WIDTH])

    cos, sp, sn = cos_ref[...], sp_ref[...], sn_ref[...]
    zq = jnp.dot(hb, wq_ref[...], preferred_element_type=F32) + bq_ref[...]
    qn = _head_norm(zq, bd_ref[...], qg_ref[...])
    q_ref[...] = qn
    qr_ref[...] = _rope(qn, cos, sp, sn)

    zkv = jnp.dot(hb, wkv_ref[...], preferred_element_type=F32) + bkv_ref[...]
    bd2 = bd_ref[0:LANES, 0:LANES]
    rows_ref[:, 0:2 * LANES] = zkv[:, 0:2 * LANES]
    ksel = _head_norm(zkv[:, 2 * LANES:3 * LANES], bd2, kg_ref[0:1, :])
    rows_ref[:, 2 * LANES:3 * LANES] = _rope(ksel, cos, sp, sn)
    rows_ref[:, 3 * LANES:4 * LANES] = zkv[:, 3 * LANES:4 * LANES]
    kwin = _head_norm(zkv[:, 4 * LANES:5 * LANES], bd2, kg_ref[1:2, :])
    win_ref[:, 0:LANES] = _rope(kwin, cos, sp, sn)
    win_ref[:, LANES:2 * LANES] = zkv[:, 5 * LANES:6 * LANES]

    small_ref[...] = _dot3(h, ws_ref[...]) + bs_ref[...]


def _inproj(x2, mod3, gmix, tabs, wts, tm, tiles_per_mod, pos_tiles):
    m = x2.shape[0]
    cos_t, sp_t, sn_t = tabs
    (wm, bm, wq, bq, wkv, bkv, ws, bs, qg, kg, bd) = wts
    r = mod3.shape[1]
    row = lambda i: (i, 0)
    const = lambda i: (0, 0)
    tab = lambda i: (i % pos_tiles, 0)
    in_specs = [
        pl.BlockSpec((tm, D_MODEL), row),
        pl.BlockSpec((None, r, 6 * D_MODEL), lambda i: (i // tiles_per_mod, 0, 0)),
        pl.BlockSpec((1, D_MODEL), const),
        pl.BlockSpec((tm, LANES), tab), pl.BlockSpec((tm, LANES), tab), pl.BlockSpec((tm, LANES), tab),
        pl.BlockSpec(wm.shape, const), pl.BlockSpec(bm.shape, const),
        pl.BlockSpec(wq.shape, const), pl.BlockSpec(bq.shape, const),
        pl.BlockSpec(wkv.shape, const), pl.BlockSpec(bkv.shape, const),
        pl.BlockSpec(ws.shape, const), pl.BlockSpec(bs.shape, const),
        pl.BlockSpec(qg.shape, const), pl.BlockSpec(kg.shape, const), pl.BlockSpec(bd.shape, const),
    ]
    widths = (M_WIDTH, M_WIDTH, M_WIDTH, A_WIDTH, A_WIDTH, 4 * LANES, 2 * LANES, LANES)
    return pl.pallas_call(
        _inproj_kernel,
        grid=(m // tm,),
        in_specs=in_specs,
        out_specs=[pl.BlockSpec((tm, w), row) for w in widths],
        out_shape=[jax.ShapeDtypeStruct((m, w), F32) for w in widths],
        compiler_params=_cparams(("parallel",)),
        name="inproj",
    )(x2, mod3, gmix, cos_t, sp_t, sn_t, wm, bm, wq, bq, wkv, bkv, ws, bs, qg, kg, bd)


def _log_sigmoid(x):
    return jnp.minimum(x, 0.0) - jnp.log(1.0 + jnp.exp(-jnp.abs(x)))


def _mlstm_kernel(*refs, L, t_valid, has_state):
    if has_state:
        q_ref, k_ref, v_ref, s_ref, c0_ref, n0_ref, m0_ref, h_ref, c_ref, n_ref, m_ref = refs
    else:
        q_ref, k_ref, v_ref, s_ref, h_ref, c_ref, n_ref, m_ref = refs
    c = pl.program_id(1)

    @pl.when(c == 0)
    def _():
        if has_state:
            c_ref[...] = c0_ref[...]
            n_ref[...] = n0_ref[...]
            m_ref[...] = m0_ref[...]
        else:
            c_ref[...] = jnp.zeros(c_ref.shape, F32)
            n_ref[...] = jnp.zeros(n_ref.shape, F32)
            m_ref[...] = jnp.zeros(m_ref.shape, F32)

    row = lax.broadcasted_iota(jnp.int32, (L, L), 0)
    col = lax.broadcasted_iota(jnp.int32, (L, L), 1)
    causal = col <= row
    eye = col == row
    tok_col = c * L + lax.broadcasted_iota(jnp.int32, (L, 1), 0)
    valid_col = tok_col < t_valid
    for hd in range(M_HEADS):
        lo, hi = hd * M_DH, (hd + 1) * M_DH
        q = q_ref[:, lo:hi]
        k = k_ref[:, lo:hi]
        v = v_ref[:, lo:hi]
        i_col = s_ref[:, hd:hd + 1]
        lf_col = _log_sigmoid(s_ref[:, M_HEADS + hd:M_HEADS + hd + 1])
        lf_col = jnp.where(valid_col, lf_col, 0.0)
        i_col = jnp.where(valid_col, i_col, -jnp.inf)
        i_row = jnp.sum(jnp.where(eye, i_col, 0.0), axis=0, keepdims=True)
        lf_row = jnp.sum(jnp.where(eye, lf_col, 0.0), axis=0, keepdims=True)
        b_col = jnp.sum(jnp.where(causal, lf_row, 0.0), axis=1, keepdims=True)
        b_row = jnp.sum(jnp.where(row <= col, lf_col, 0.0), axis=0, keepdims=True)
        m_prev = m_ref[:, hd:hd + 1]
        dmat = jnp.where(causal, b_col - b_row + i_row, -jnp.inf)
        inter = b_col + m_prev
        m_row = jnp.maximum(jnp.max(dmat, axis=1, keepdims=True), inter)
        w = jnp.exp(dmat - m_row)
        w_inter = jnp.exp(inter - m_row)
        s = _bdot_t(q, k) * w
        cm = c_ref[hd]
        nv = n_ref[hd]
        num = _bdot(s, v) + w_inter * _bdot_t(q, cm)
        den = jnp.sum(s, axis=1, keepdims=True) + w_inter * jnp.sum(q * nv, axis=1, keepdims=True)
        h_ref[:, lo:hi] = num / jnp.maximum(jnp.abs(den), jnp.exp(-m_row))
        b_last = b_col[L - 1:L, :]
        dec_col = b_last - b_col + i_col
        dec_row = b_last - b_row + i_row
        m_new = jnp.maximum(b_last + m_prev, jnp.max(dec_row, axis=1, keepdims=True))
        ws_col = jnp.exp(dec_col - m_new)
        wc = jnp.exp(b_last + m_prev - m_new)
        vw = (v * ws_col).astype(BF16)
        upd = lax.dot_general(vw, k.astype(BF16), (((0,), (0,)), ((), ())), preferred_element_type=F32)
        c_ref[hd] = wc * cm + upd
        n_ref[hd] = wc * nv + jnp.sum(k * ws_col, axis=0, keepdims=True)
        m_ref[:, hd:hd + 1] = m_new


def _mlstm(mq, mk, mv, small, nb, t_pad, t_valid, L, state=None):
    nc = t_pad // L
    has_state = state is not None
    blk = lambda b, c: (b * nc + c, 0)
    st4 = lambda b, c: (b, 0, 0, 0)
    st3 = lambda b, c: (b, 0, 0)
    in_specs = [pl.BlockSpec((L, M_WIDTH), blk)] * 3 + [pl.BlockSpec((L, LANES), blk)]
    args = [mq, mk, mv, small]
    if has_state:
        c0, n0, m0 = state
        in_specs += [pl.BlockSpec((None, M_HEADS, M_DH, M_DH), st4),
                     pl.BlockSpec((None, M_HEADS, 1, M_DH), st4),
                     pl.BlockSpec((None, 1, M_HEADS), st3)]
        args += [c0, n0.reshape(nb, M_HEADS, 1, M_DH), m0.reshape(nb, 1, M_HEADS)]
    out_specs = [pl.BlockSpec((L, M_WIDTH), blk),
                 pl.BlockSpec((None, M_HEADS, M_DH, M_DH), st4),
                 pl.BlockSpec((None, M_HEADS, 1, M_DH), st4),
                 pl.BlockSpec((None, 1, M_HEADS), st3)]
    out_shape = [jax.ShapeDtypeStruct((nb * t_pad, M_WIDTH), F32),
                 jax.ShapeDtypeStruct((nb, M_HEADS, M_DH, M_DH), F32),
                 jax.ShapeDtypeStruct((nb, M_HEADS, 1, M_DH), F32),
                 jax.ShapeDtypeStruct((nb, 1, M_HEADS), F32)]
    h, cs, ns, ms = pl.pallas_call(
        functools.partial(_mlstm_kernel, L=L, t_valid=t_valid, has_state=has_state),
        grid=(nb, nc),
        in_specs=in_specs,
        out_specs=out_specs,
        out_shape=out_shape,
        compiler_params=_cparams(("parallel", "arbitrary")),
        name="mlstm",
    )(*args)
    return h, cs, ns.reshape(nb, M_HEADS, M_DH), ms.reshape(nb, M_HEADS)


def _stack_heads(qt, g):
    t = qt.shape[0]
    z = jnp.zeros((t, A_DH), F32)
    parts = []
    for hh in range(A_HPG):
        hd = g * A_HPG + hh
        qh = qt[:, hd * A_DH:(hd + 1) * A_DH] * (ATT_SCALE * LOG2E)
        parts.append(jnp.concatenate([qh, z], axis=1) if g == 0 else jnp.concatenate([z, qh], axis=1))
    return jnp.concatenate(parts, axis=0).astype(BF16)


def _gate_cols(small, g, br):
    cols = []
    for hh in range(A_HPG):
        c0 = 2 * M_HEADS + (g * A_HPG + hh) * 3 + br
        cols.append(_sigmoid(small[:, c0:c0 + 1]))
    return jnp.concatenate(cols, axis=0)


def _compress(k_ref, v_ref, nseg, wbd_ref, pe_ref, kg0):
    acc_lo = jnp.zeros((nseg, 2 * LANES), F32)
    acc_hi = jnp.zeros((nseg, 2 * LANES), F32)
    for l in range(CMP_STRIDE):
        xl = jnp.concatenate([k_ref[pl.ds(l, nseg, stride=CMP_STRIDE), :],
                              v_ref[pl.ds(l, nseg, stride=CMP_STRIDE), :]], axis=1)
        acc_lo = acc_lo + _bdot(xl + pe_ref[l], wbd_ref[l])
        acc_hi = acc_hi + _bdot(xl + pe_ref[CMP_STRIDE + l], wbd_ref[CMP_STRIDE + l])
    kv = acc_lo + pltpu.roll(acc_hi, nseg - 1, 0)
    kc = kv[:, 0:LANES]
    vc = kv[:, LANES:2 * LANES]
    lane = lax.broadcasted_iota(jnp.int32, (nseg, LANES), 1)
    sq = kc * kc
    ms0 = jnp.sum(jnp.where(lane < A_DH, sq, 0.0), axis=1, keepdims=True) * (1.0 / A_DH)
    ms1 = jnp.sum(jnp.where(lane >= A_DH, sq, 0.0), axis=1, keepdims=True) * (1.0 / A_DH)
    ms = jnp.where(lane < A_DH, ms0, ms1)
    kc = kc * lax.rsqrt(ms + EPS) * kg0
    return kc, vc


def _cmp_branch(qn_g, kc_b, vc_b, tpos_rows, nseg, n_tok):
    s = _bdot_t(qn_g, kc_b)
    nidx = lax.broadcasted_iota(jnp.int32, (1, nseg), 1)
    vis = (nidx * CMP_STRIDE + (CMP_LEN - 1)) <= tpos_rows
    sm = jnp.where(vis, s, NEG_BIG)
    mx = jnp.max(sm, axis=1, keepdims=True)
    e = jnp.where(vis, jnp.exp2(sm - mx), 0.0)
    d = jnp.sum(e, axis=1, keepdims=True)
    p = e / jnp.where(d > 0, d, 1.0)
    o = _bdot(p, vc_b)
    imp = p[0:n_tok]
    for hh in range(1, A_HPG):
        imp = imp + p[hh * n_tok:(hh + 1) * n_tok]
    return o, imp


def _masked_attn_direct(q_g, k_parts, v_parts, allowed_parts):
    ss = [jnp.where(al, _bdot_t(q_g, kk), NEG_BIG) for kk, al in zip(k_parts, allowed_parts)]
    mx = ss[0].max(axis=1, keepdims=True)
    for s in ss[1:]:
        mx = jnp.maximum(mx, s.max(axis=1, keepdims=True))
    num = None
    den = None
    for s, al, vv in zip(ss, allowed_parts, v_parts):
        e = jnp.where(al, jnp.exp2(s - mx), 0.0)
        dd = jnp.sum(e, axis=1, keepdims=True)
        oo = _bdot(e, vv)
        num = oo if num is None else num + oo
        den = dd if den is None else den + dd
    return num / jnp.where(den > 0, den, 1.0)


def _assemble_heads(o_groups, n_tok):
    pieces = []
    for g in range(A_KV):
        for hh in range(A_HPG):
            pieces.append(o_groups[g][hh * n_tok:(hh + 1) * n_tok, g * A_DH:(g + 1) * A_DH])
    return jnp.concatenate(pieces, axis=1)


def _lane_tile(a, width):
    rep = width // LANES
    return a if rep == 1 else jnp.concatenate([a] * rep, axis=1)


def _add_bias(s, bias):
    t, k = bias.shape
    return (s.reshape(A_HPG, t, k) + bias[None]).reshape(A_HPG * t, k)


def _nsa_prompt_kernel(q_ref, qr_ref, small_ref, rows_ref, win_ref, wbd_ref, pe_ref, kg0_ref,
                       pool_ref, o_ref,
                       kraw_sc, vraw_sc, kc_sc, vc_sc, m_sc, l_sc, acc_sc, *, T, tq, kc_len):
    qi = pl.program_id(1)
    nseg = T // CMP_STRIDE
    nsb = T // SEL_LEN

    @pl.when(qi == 0)
    def _():
        kraw_sc[...] = rows_ref[:, 0:LANES]
        vraw_sc[...] = rows_ref[:, LANES:2 * LANES]
        kc, vc = _compress(kraw_sc, vraw_sc, nseg, wbd_ref, pe_ref, kg0_ref[...])
        kc_sc[...] = kc
        vc_sc[...] = vc

    t0 = qi * tq
    tpos_col = t0 + lax.broadcasted_iota(jnp.int32, (tq, 1), 0)
    tpos_rows = jnp.concatenate([tpos_col] * A_HPG, axis=0)
    tpos_lane = t0 + lax.broadcasted_iota(jnp.int32, (1, tq), 1)
    q = q_ref[...]
    qr = qr_ref[...]
    small = small_ref[...]
    kc_b = kc_sc[...].astype(BF16)
    vc_b = vc_sc[...].astype(BF16)
    bidx = lax.broadcasted_iota(jnp.int32, (nsb, tq), 0)
    cur = tpos_lane // SEL_LEN
    r4 = A_HPG * tq
    qr_gs = [_stack_heads(qr, g) for g in range(A_KV)]
    o_cmps = []
    sel_bs = []
    for g in range(A_KV):
        qn_g = _stack_heads(q, g)
        o_cmp, imp = _cmp_branch(qn_g, kc_b, vc_b, tpos_rows, nseg, tq)
        o_cmps.append(o_cmp)
        imp_sel = _dot2_exact_rhs(imp, pool_ref[...])
        imp_t = jnp.transpose(imp_sel)[0:nsb, :]
        val = jnp.where(bidx < cur, imp_t, -1.0)
        rank = jnp.zeros((nsb, tq), F32)
        for bp in range(nsb):
            vb = val[bp:bp + 1, :]
            ahead = jnp.where(vb > val, 1.0, jnp.where((vb == val) & (bidx > bp), 1.0, 0.0))
            rank = rank + ahead
        sel_t = jnp.where(((rank < (N_SEL - 1)) & (bidx < cur)) | (bidx == cur), 1.0, 0.0)
        if nsb < LANES:
            sel_t = jnp.concatenate([sel_t, jnp.zeros((LANES - nsb, tq), F32)], axis=0)
        sel_bs.append(jnp.transpose(sel_t).astype(BF16))

    m_sc[...] = jnp.full(m_sc.shape, M_INIT, F32)
    l_sc[...] = jnp.zeros(l_sc.shape, F32)
    acc_sc[...] = jnp.zeros(acc_sc.shape, F32)

    def sel_body(c, carry):
        k0 = pl.multiple_of(c * kc_len, kc_len)
        kb = rows_ref[pl.ds(k0, kc_len), 2 * LANES:3 * LANES].astype(BF16)
        vb = rows_ref[pl.ds(k0, kc_len), 3 * LANES:4 * LANES].astype(BF16)
        kpos = k0 + lax.broadcasted_iota(jnp.int32, (1, kc_len), 1)
        causal = kpos <= tpos_col
        kblk = (k0 + lax.broadcasted_iota(jnp.int32, (LANES, kc_len), 1)) // SEL_LEN
        expand = jnp.where(kblk == lax.broadcasted_iota(jnp.int32, (LANES, kc_len), 0), 1.0, 0.0).astype(BF16)
        for g in range(A_KV):
            mk = jnp.dot(sel_bs[g], expand, preferred_element_type=F32)
            bias = jnp.where(causal, (mk - 1.0) * (-NEG_BIG), NEG_BIG)
            sm = _add_bias(_bdot_t(qr_gs[g], kb), bias)
            m_prev = m_sc[g]
            m_new = jnp.maximum(m_prev, jnp.max(sm, axis=1, keepdims=True))
            alpha = jnp.exp2(m_prev - m_new)
            p = jnp.exp2(sm - _lane_tile(m_new, kc_len))
            l_sc[g] = alpha * l_sc[g] + jnp.sum(p, axis=1, keepdims=True)
            acc_sc[g] = alpha * acc_sc[g] + _bdot(p, vb)
            m_sc[g] = m_new
        return carry

    lax.fori_loop(0, (t0 + tq + kc_len - 1) // kc_len, sel_body, 0)

    wk = min(WINDOW + tq, T)
    w0 = pl.multiple_of(jnp.clip(t0 + tq - wk, 0, T - wk), tq)
    kw = win_ref[pl.ds(w0, wk), 0:LANES].astype(BF16)
    vw = win_ref[pl.ds(w0, wk), LANES:2 * LANES].astype(BF16)
    wdiff = tpos_col - (w0 + lax.broadcasted_iota(jnp.int32, (1, wk), 1))
    wbias = jnp.where((wdiff >= 0) & (wdiff < WINDOW), 0.0, NEG_BIG)

    o_groups = []
    for g in range(A_KV):
        l = l_sc[g]
        o_sel = acc_sc[g] / jnp.where(l > 0, l, 1.0)
        sw = _add_bias(_bdot_t(qr_gs[g], kw), wbias)
        mw = jnp.broadcast_to(jnp.max(sw, axis=1, keepdims=True), (r4, LANES))
        pw = jnp.exp2(sw - _lane_tile(mw, wk))
        o_win = _bdot(pw, vw) / jnp.broadcast_to(jnp.sum(pw, axis=1, keepdims=True), (r4, LANES))
        o_groups.append(_gate_cols(small, g, 0) * o_cmps[g] + _gate_cols(small, g, 1) * o_sel
                        + _gate_cols(small, g, 2) * o_win)
    o_ref[...] = _assemble_heads(o_groups, tq)


def _nsa_prompt(q, qr, small, rows, win, wbd, pe, kg0, nb, T):
    tq = 128
    kc_len = _pick_tile(T, 512)
    nq = T // tq
    nseg = T // CMP_STRIDE
    nsb = T // SEL_LEN
    pool = (jnp.arange(nseg)[:, None] // (SEL_LEN // CMP_STRIDE) == jnp.arange(LANES)[None, :]).astype(BF16)
    tile = lambda b, i: (b * nq + i, 0)
    per_b = lambda b, i: (b, 0)
    c2 = lambda b, i: (0, 0)
    c3 = lambda b, i: (0, 0, 0)
    r4 = A_HPG * tq
    return pl.pallas_call(
        functools.partial(_nsa_prompt_kernel, T=T, tq=tq, kc_len=kc_len),
        grid=(nb, nq),
        in_specs=[pl.BlockSpec((tq, A_WIDTH), tile), pl.BlockSpec((tq, A_WIDTH), tile),
                  pl.BlockSpec((tq, LANES), tile),
                  pl.BlockSpec((T, 4 * LANES), per_b), pl.BlockSpec((T, 2 * LANES), per_b),
                  pl.BlockSpec(wbd.shape, c3), pl.BlockSpec(pe.shape, c3), pl.BlockSpec(kg0.shape, c2),
                  pl.BlockSpec(pool.shape, c2)],
        out_specs=pl.BlockSpec((tq, A_WIDTH), tile),
        out_shape=jax.ShapeDtypeStruct((nb * T, A_WIDTH), F32),
        scratch_shapes=[pltpu.VMEM((T, LANES), F32), pltpu.VMEM((T, LANES), F32),
                        pltpu.VMEM((nseg, LANES), F32), pltpu.VMEM((nseg, LANES), F32),
                        pltpu.VMEM((A_KV, r4, LANES), F32), pltpu.VMEM((A_KV, r4, LANES), F32),
                        pltpu.VMEM((A_KV, r4, LANES), F32)],
        compiler_params=_cparams(("parallel", "arbitrary")),
        name="nsa_prompt",
    )(q, qr, small, rows, win, wbd, pe, kg0, pool)


def _nsa_sample_kernel(pt_ref, cache_ref, q_ref, qr_ref, small_ref, rows_ref, winnew_ref, winbuf_ref,
                       wbd_ref, pe_ref, kg0_ref, pool_ref, expand_ref,
                       o_ref, winout_ref,
                       cmp_buf, sel_buf, sems, *, n_pages, past_len, t_valid):
    b = pl.program_id(0)
    nb = pl.num_programs(0)
    tp = SAMPLE_PAD_T
    nseg = past_len // CMP_STRIDE
    nsb = past_len // SEL_LEN
    wbuf = winbuf_ref.shape[0]

    def page_copies(bb, p, phase):
        page = pt_ref[bb * n_pages + p]
        dst_rows = pl.ds(p * PAGE_SIZE, PAGE_SIZE)
        if phase == 0:
            return [pltpu.make_async_copy(cache_ref.at[page, :, pl.ds(j * LANES, LANES)],
                                          cmp_buf.at[j, dst_rows, :], sems.at[0]) for j in range(2)]
        return [pltpu.make_async_copy(cache_ref.at[page, :, pl.ds(2 * LANES, 2 * LANES)],
                                      sel_buf.at[dst_rows, :], sems.at[1])]

    def start_all(bb, phase):
        def body(p, c):
            for cp in page_copies(bb, p, phase):
                cp.start()
            return c
        lax.fori_loop(0, n_pages, body, 0)

    def wait_all(bb, phase):
        def body(p, c):
            for cp in page_copies(bb, p, phase):
                cp.wait()
            return c
        lax.fori_loop(0, n_pages, body, 0)

    @pl.when(b == 0)
    def _():
        start_all(b, 0)

    start_all(b, 1)
    wait_all(b, 0)

    kc, vc = _compress(cmp_buf.at[0], cmp_buf.at[1], nseg, wbd_ref, pe_ref, kg0_ref[...])
    kc_b = kc.astype(BF16)
    vc_b = vc.astype(BF16)
    q = q_ref[...]
    qr = qr_ref[...]
    small = small_ref[...]
    tpos_col = past_len + lax.broadcasted_iota(jnp.int32, (tp, 1), 0)
    tpos_rows = jnp.concatenate([tpos_col] * A_HPG, axis=0)
    bp_idx = lax.broadcasted_iota(jnp.int32, (nsb, nsb), 0)
    b_idx = lax.broadcasted_iota(jnp.int32, (nsb, nsb), 1)
    o_cmps = []
    sels = []
    for g in range(A_KV):
        qn_g = _stack_heads(q, g)
        o_cmp, imp = _cmp_branch(qn_g, kc_b, vc_b, tpos_rows, nseg, tp)
        o_cmps.append(o_cmp)
        imp_sel = _dot2_exact_rhs(imp, pool_ref[...])
        imp_pad = jnp.concatenate([imp_sel, jnp.zeros((nsb - tp, nsb), F32)], axis=0)
        imp_t = jnp.transpose(imp_pad)
        rows_sel = []
        for t in range(tp):
            if t < t_valid:
                row_t = imp_sel[t:t + 1, :]
                col_t = imp_t[:, t:t + 1]
                ahead = jnp.where(col_t > row_t, 1.0, jnp.where((col_t == row_t) & (bp_idx < b_idx), 1.0, 0.0))
                rank = jnp.sum(ahead, axis=0, keepdims=True)
                rows_sel.append(jnp.where(rank < (N_SEL - 1), 1.0, 0.0))
            else:
                rows_sel.append(jnp.zeros((1, nsb), F32))
        sels.append(jnp.concatenate(rows_sel, axis=0).astype(BF16))

    @pl.when(b + 1 < nb)
    def _():
        start_all(b + 1, 0)

    wait_all(b, 1)

    new_idx = lax.broadcasted_iota(jnp.int32, (tp, tp), 1)
    tok_idx = lax.broadcasted_iota(jnp.int32, (tp, tp), 0)
    new_ok = jnp.concatenate([jnp.where(new_idx <= tok_idx, 1.0, 0.0)] * A_HPG, axis=0) > 0.5
    wpos = past_len - wbuf + lax.broadcasted_iota(jnp.int32, (1, wbuf), 1)
    wdiff = tpos_col - wpos
    win_ok = jnp.concatenate([jnp.where((wdiff >= 0) & (wdiff < WINDOW), 1.0, 0.0)] * A_HPG, axis=0) > 0.5
    k_past = sel_buf[:, 0:LANES].astype(BF16)
    v_past = sel_buf[:, LANES:2 * LANES].astype(BF16)
    k_new = rows_ref[:, 2 * LANES:3 * LANES]
    v_new = rows_ref[:, 3 * LANES:4 * LANES]
    kw_past = winbuf_ref[:, 0:LANES]
    vw_past = winbuf_ref[:, LANES:2 * LANES]
    kw_new = winnew_ref[:, 0:LANES]
    vw_new = winnew_ref[:, LANES:2 * LANES]
    o_groups = []
    for g in range(A_KV):
        qr_g = _stack_heads(qr, g)
        mk = jnp.dot(sels[g], expand_ref[...], preferred_element_type=F32)
        past_ok = jnp.concatenate([mk] * A_HPG, axis=0) > 0.5
        o_sel = _masked_attn_direct(qr_g, [k_past, k_new], [v_past, v_new], [past_ok, new_ok])
        o_win = _masked_attn_direct(qr_g, [kw_past, kw_new], [vw_past, vw_new], [win_ok, new_ok])
        o_groups.append(_gate_cols(small, g, 0) * o_cmps[g] + _gate_cols(small, g, 1) * o_sel
                        + _gate_cols(small, g, 2) * o_win)
    o_ref[...] = _assemble_heads(o_groups, tp)

    wb = winbuf_ref[...]
    rolled = pltpu.roll(wb, wbuf - t_valid, 0)
    newr = pltpu.roll(winnew_ref[...], tp - t_valid, 0)
    sub = lax.broadcasted_iota(jnp.int32, (tp, 2 * LANES), 0)
    winout_ref[0:wbuf - tp, :] = rolled[0:wbuf - tp, :]
    winout_ref[wbuf - tp:wbuf, :] = jnp.where(sub < tp - t_valid, rolled[wbuf - tp:wbuf, :], newr)


def _nsa_sample(page_table, cache, q, qr, small, rows, winnew, winbuf, wbd, pe, kg0, t_valid):
    nb, n_pages = page_table.shape
    past_len = n_pages * PAGE_SIZE
    nseg = past_len // CMP_STRIDE
    nsb = past_len // SEL_LEN
    tp = SAMPLE_PAD_T
    wbuf = winbuf.shape[1]
    pool = (jnp.arange(nseg)[:, None] // (SEL_LEN // CMP_STRIDE) == jnp.arange(nsb)[None, :]).astype(BF16)
    expand = (jnp.arange(nsb)[:, None] == jnp.arange(past_len)[None, :] // SEL_LEN).astype(BF16)
    tile = lambda b, pt: (b, 0)
    c2 = lambda b, pt: (0, 0)
    c3 = lambda b, pt: (0, 0, 0)
    gs = pltpu.PrefetchScalarGridSpec(
        num_scalar_prefetch=1,
        grid=(nb,),
        in_specs=[pl.BlockSpec(memory_space=pl.ANY),
                  pl.BlockSpec((tp, A_WIDTH), tile), pl.BlockSpec((tp, A_WIDTH), tile),
                  pl.BlockSpec((tp, LANES), tile), pl.BlockSpec((tp, 4 * LANES), tile),
                  pl.BlockSpec((tp, 2 * LANES), tile),
                  pl.BlockSpec((None, wbuf, 2 * LANES), lambda b, pt: (b, 0, 0)),
                  pl.BlockSpec(wbd.shape, c3), pl.BlockSpec(pe.shape, c3), pl.BlockSpec(kg0.shape, c2),
                  pl.BlockSpec(pool.shape, c2), pl.BlockSpec(expand.shape, c2)],
        out_specs=[pl.BlockSpec((tp, A_WIDTH), tile),
                   pl.BlockSpec((None, wbuf, 2 * LANES), lambda b, pt: (b, 0, 0))],
        scratch_shapes=[pltpu.VMEM((2, past_len, LANES), F32), pltpu.VMEM((past_len, 2 * LANES), F32),
                        pltpu.SemaphoreType.DMA((2,))],
    )
    return pl.pallas_call(
        functools.partial(_nsa_sample_kernel, n_pages=n_pages, past_len=past_len, t_valid=t_valid),
        grid_spec=gs,
        out_shape=[jax.ShapeDtypeStruct((nb * tp, A_WIDTH), F32),
                   jax.ShapeDtypeStruct((nb, wbuf, 2 * LANES), F32)],
        compiler_params=_cparams(("arbitrary",)),
        name="nsa_sample",
    )(page_table.reshape(-1), cache, q, qr, small, rows, winnew, winbuf, wbd, pe, kg0, pool, expand)


def _mixout_kernel(x_ref, hm_ref, on_ref, mod_ref, gmix_ref, gffn_ref,
                   wog_ref, bog_ref, wum_ref, wua_ref, wout_ref, wr_ref, br_ref,
                   x1_ref, h2_ref, lg_ref):
    d = D_MODEL
    x = x_ref[...]
    sh1, sc1, gt1 = mod_ref[:, 0:d], mod_ref[:, d:2 * d], mod_ref[:, 2 * d:3 * d]
    sh2, sc2 = mod_ref[:, 3 * d:4 * d], mod_ref[:, 4 * d:5 * d]
    h = _rmsnorm_rows(x, gmix_ref[...]) * (1.0 + sc1) + sh1
    hb = h.astype(BF16)
    mo = jnp.dot(hb, wog_ref[:, 0:M_WIDTH], preferred_element_type=F32) + bog_ref[:, 0:M_WIDTH]
    ym = _bdot(_sigmoid(mo) * hm_ref[...], wum_ref[...])
    ya = _bdot(on_ref[...], wua_ref[...])
    ga = jnp.dot(hb, wog_ref[:, M_WIDTH:M_WIDTH + d], preferred_element_type=F32) + bog_ref[:, M_WIDTH:M_WIDTH + d]
    u = _sigmoid(ga) * ym
    gb = (jnp.dot(hb, wog_ref[:, M_WIDTH + d:M_WIDTH + 2 * d], preferred_element_type=F32)
          + bog_ref[:, M_WIDTH + d:M_WIDTH + 2 * d])
    u = u + _sigmoid(gb) * ya
    x1 = x + gt1 * _bdot(u, wout_ref[...])
    x1_ref[...] = x1
    h2 = _rmsnorm_rows(x1, gffn_ref[...]) * (1.0 + sc2) + sh2
    h2_ref[...] = h2.astype(BF16)
    lg_ref[...] = _dot3(h2, wr_ref[...]) + br_ref[...]


def _mixout(x2, hm, on, mod3, gmix, gffn, wts, tm, tiles_per_mod):
    m = x2.shape[0]
    (wog, bog, wum, wua, wout, wr, br) = wts
    r = mod3.shape[1]
    row = lambda i: (i, 0)
    const = lambda i: (0, 0)
    return pl.pallas_call(
        _mixout_kernel,
        grid=(m // tm,),
        in_specs=[pl.BlockSpec((tm, D_MODEL), row), pl.BlockSpec((tm, M_WIDTH), row),
                  pl.BlockSpec((tm, A_WIDTH), row),
                  pl.BlockSpec((None, r, 6 * D_MODEL), lambda i: (i // tiles_per_mod, 0, 0)),
                  pl.BlockSpec((1, D_MODEL), const), pl.BlockSpec((1, D_MODEL), const),
                  pl.BlockSpec(wog.shape, const), pl.BlockSpec(bog.shape, const),
                  pl.BlockSpec(wum.shape, const), pl.BlockSpec(wua.shape, const),
                  pl.BlockSpec(wout.shape, const), pl.BlockSpec(wr.shape, const),
                  pl.BlockSpec(br.shape, const)],
        out_specs=[pl.BlockSpec((tm, D_MODEL), row), pl.BlockSpec((tm, D_MODEL), row),
                   pl.BlockSpec((tm, LANES), row)],
        out_shape=[jax.ShapeDtypeStruct((m, D_MODEL), F32), jax.ShapeDtypeStruct((m, D_MODEL), BF16),
                   jax.ShapeDtypeStruct((m, LANES), F32)],
        compiler_params=_cparams(("parallel",)),
        name="mixout",
    )(x2, hm, on, mod3, gmix, gffn, wog, bog, wum, wua, wout, wr, br)


MOE_BM = 256
MOE_CH = 512


def _moe_kernel(be_ref, na_ref, xs_ref, wgu_ref, bgu_ref, wdn_ref, bdn_ref, y_ref, wgu_bf, wdn_bf):
    i = pl.program_id(0)
    e = be_ref[i]
    prev = be_ref[jnp.maximum(i - 1, 0)]

    @pl.when((i == 0) | (e != prev))
    def _():
        for j in range(2 * D_EXPERT // MOE_CH):
            wgu_bf[:, j * MOE_CH:(j + 1) * MOE_CH] = wgu_ref[:, j * MOE_CH:(j + 1) * MOE_CH].astype(BF16)
        for j in range(D_EXPERT // MOE_CH):
            wdn_bf[j * MOE_CH:(j + 1) * MOE_CH, :] = wdn_ref[j * MOE_CH:(j + 1) * MOE_CH, :].astype(BF16)

    @pl.when(i < na_ref[0])
    def _():
        xb = xs_ref[...]
        acc = jnp.zeros(y_ref.shape, F32) + bdn_ref[...]
        for j in range(D_EXPERT // MOE_CH):
            lo, hi = j * MOE_CH, (j + 1) * MOE_CH
            gj = jnp.dot(xb, wgu_bf[:, lo:hi], preferred_element_type=F32) + bgu_ref[:, lo:hi]
            uj = (jnp.dot(xb, wgu_bf[:, D_EXPERT + lo:D_EXPERT + hi], preferred_element_type=F32)
                  + bgu_ref[:, D_EXPERT + lo:D_EXPERT + hi])
            gj = jnp.minimum(gj, SWIGLU_LIMIT)
            uj = jnp.clip(uj, -SWIGLU_LIMIT, SWIGLU_LIMIT)
            act = gj * _sigmoid(SWIGLU_ALPHA * gj) * (uj + 1.0)
            acc = acc + jnp.dot(act.astype(BF16), wdn_bf[lo:hi, :], preferred_element_type=F32)
        y_ref[...] = acc

    @pl.when(i >= na_ref[0])
    def _():
        y_ref[...] = jnp.zeros(y_ref.shape, F32)


def _moe_experts(block_e, n_active, xs, w_gu, b_gu, w_dn, b_dn):
    p = xs.shape[0]
    nblk = p // MOE_BM
    gs = pltpu.PrefetchScalarGridSpec(
        num_scalar_prefetch=2,
        grid=(nblk,),
        in_specs=[pl.BlockSpec((MOE_BM, D_MODEL), lambda i, be, na: (i, 0)),
                  pl.BlockSpec((None, D_MODEL, 2 * D_EXPERT), lambda i, be, na: (be[i], 0, 0)),
                  pl.BlockSpec((None, 1, 2 * D_EXPERT), lambda i, be, na: (be[i], 0, 0)),
                  pl.BlockSpec((None, D_EXPERT, D_MODEL), lambda i, be, na: (be[i], 0, 0)),
                  pl.BlockSpec((None, 1, D_MODEL), lambda i, be, na: (be[i], 0, 0))],
        out_specs=pl.BlockSpec((MOE_BM, D_MODEL), lambda i, be, na: (i, 0)),
        scratch_shapes=[pltpu.VMEM((D_MODEL, 2 * D_EXPERT), BF16), pltpu.VMEM((D_EXPERT, D_MODEL), BF16)],
    )
    return pl.pallas_call(
        _moe_kernel,
        grid_spec=gs,
        out_shape=jax.ShapeDtypeStruct((p, D_MODEL), F32),
        compiler_params=_cparams(("arbitrary",)),
        name="moe_experts",
    )(block_e, n_active, xs, w_gu, b_gu.reshape(N_EXPERTS, 1, -1), w_dn, b_dn.reshape(N_EXPERTS, 1, -1))


def _moe(h2, logits, w_gu, b_gu, w_dn, b_dn):
    m = h2.shape[0]
    top_v, top_e = lax.top_k(logits[:, :N_EXPERTS], TOP_K)
    gate = jax.nn.softmax(top_v, axis=-1)
    mk = m * TOP_K
    flat_e = top_e.reshape(mk)
    order = jnp.argsort(flat_e)
    sorted_e = flat_e[order]
    counts = jnp.bincount(flat_e, length=N_EXPERTS)
    padded = (counts + MOE_BM - 1) // MOE_BM * MOE_BM
    pad_end = jnp.cumsum(padded)
    pad_start = pad_end - padded
    start = jnp.cumsum(counts) - counts
    dest = (pad_start[sorted_e] + jnp.arange(mk) - start[sorted_e]).astype(jnp.int32)
    n_blocks = (mk + N_EXPERTS * (MOE_BM - 1) + MOE_BM - 1) // MOE_BM
    src_tok = jnp.full((n_blocks * MOE_BM,), m, jnp.int32).at[dest].set((order // TOP_K).astype(jnp.int32))
    block_e = jnp.minimum(jnp.searchsorted(pad_end, jnp.arange(n_blocks) * MOE_BM, side='right'),
                          N_EXPERTS - 1).astype(jnp.int32)
    n_active = (pad_end[-1] // MOE_BM).astype(jnp.int32).reshape(1)
    hp = jnp.concatenate([h2, jnp.zeros((1, D_MODEL), h2.dtype)], axis=0)
    xs = hp[src_tok]
    ys = _moe_experts(block_e, n_active, xs, w_gu, b_gu, w_dn, b_dn)
    slot_dest = jnp.zeros((mk,), jnp.int32).at[order].set(dest)
    y_slots = ys[slot_dest].reshape(m, TOP_K, D_MODEL)
    return jnp.einsum('mkd,mk->md', y_slots, gate)


def _rope_tables(pos):
    half = ROT_DIM // 2
    inv = ROPE_THETA ** (-jnp.arange(half, dtype=F32) * (2.0 / ROT_DIM))
    ang = pos.astype(F32)[:, None] * inv[None, :]
    cos, sin = jnp.cos(ang), jnp.sin(ang)
    n = pos.shape[0]
    ones = jnp.ones((n, A_DH - ROT_DIM), F32)
    zeros_h = jnp.zeros((n, half), F32)
    zeros_r = jnp.zeros((n, A_DH - ROT_DIM), F32)
    cos64 = jnp.concatenate([cos, cos, ones], axis=1)
    sprev64 = jnp.concatenate([zeros_h, sin, zeros_r], axis=1)
    snext64 = jnp.concatenate([-sin, zeros_h, zeros_r], axis=1)
    two = lambda a: jnp.concatenate([a, a], axis=1)
    return two(cos64), two(sprev64), two(snext64)


def _prep_weights(w_in, b_in, q_norm_g, k_norm_g, cmp_pe_k, cmp_pe_v, cmp_w_k, cmp_w_v,
                  w_up_m, w_up_a, w_out, w_router, b_router):
    b2 = b_in.reshape(1, N_IN)
    wm = w_in[:, OFF_MQ:OFF_MO].astype(BF16)
    bm = b2[:, OFF_MQ:OFF_MO]
    wq = w_in[:, OFF_AQ:OFF_AKV].astype(BF16)
    bq = b2[:, OFF_AQ:OFF_AKV]
    wkv = w_in[:, OFF_AKV:OFF_AG].astype(BF16)
    bkv = b2[:, OFF_AKV:OFF_AG]
    n_small = 2 * M_HEADS + 3 * A_HEADS
    ws = jnp.concatenate([w_in[:, OFF_MI:OFF_AQ], w_in[:, OFF_AG:OFF_GA],
                          jnp.zeros((D_MODEL, LANES - n_small), F32)], axis=1)
    bs = jnp.concatenate([b2[:, OFF_MI:OFF_AQ], b2[:, OFF_AG:OFF_GA], jnp.zeros((1, LANES - n_small), F32)], axis=1)
    qg = jnp.tile(q_norm_g, A_HEADS).reshape(1, A_WIDTH)
    kg = jnp.stack([jnp.tile(k_norm_g[1], A_KV), jnp.tile(k_norm_g[2], A_KV)], axis=0)
    kg0 = jnp.tile(k_norm_g[0], A_KV).reshape(1, LANES)
    hid = jnp.arange(A_WIDTH) // A_DH
    bd = jnp.where(hid[:, None] == hid[None, :], 1.0 / A_DH, 0.0).astype(BF16)
    inproj_w = (wm, bm, wq, bq, wkv, bkv, ws, bs, qg, kg, bd)

    z = jnp.zeros((CMP_LEN, A_DH, A_DH), F32)
    r0 = jnp.concatenate([cmp_w_k, z, z, z], axis=2)
    r1 = jnp.concatenate([z, cmp_w_k, z, z], axis=2)
    r2 = jnp.concatenate([z, z, cmp_w_v, z], axis=2)
    r3 = jnp.concatenate([z, z, z, cmp_w_v], axis=2)
    wbd = jnp.concatenate([r0, r1, r2, r3], axis=1).astype(BF16)
    pe = jnp.concatenate([cmp_pe_k, cmp_pe_k, cmp_pe_v, cmp_pe_v], axis=1).reshape(CMP_LEN, 1, 2 * LANES)

    wog = jnp.concatenate([w_in[:, OFF_MO:OFF_MI], w_in[:, OFF_GA:N_IN]], axis=1).astype(BF16)
    bog = jnp.concatenate([b2[:, OFF_MO:OFF_MI], b2[:, OFF_GA:N_IN]], axis=1)
    wr = jnp.concatenate([w_router, jnp.zeros((D_MODEL, LANES - N_EXPERTS), F32)], axis=1)
    br = jnp.concatenate([b_router, jnp.zeros((LANES - N_EXPERTS,), F32)]).reshape(1, LANES)
    mixout_w = (wog, bog, w_up_m.astype(BF16), w_up_a.astype(BF16), w_out.astype(BF16), wr, br)
    return inproj_w, (wbd, pe, kg0), mixout_w


def _pick_tile(m, pref):
    t = pref
    while m % t:
        t //= 2
    return t


def kernel(x_prompt, x_sample, cache_nsa_kv, state_win_kv, state_mlstm_C, state_mlstm_n, state_mlstm_m, page_table, c_prompt, c_sample, w_ada, b_ada, g_mix, g_ffn, w_in, b_in, q_norm_g, k_norm_g, cmp_pe_k, cmp_pe_v, cmp_w_k, cmp_w_v, w_up_m, w_up_a, w_out, w_router, b_router, w_gu, b_gu, w_dn, b_dn):
    depth = w_in.shape[0]
    assert depth == 1
    B, T, D = x_prompt.shape
    DB, TS, _ = x_sample.shape
    n_pages = page_table.shape[1]
    past_len = n_pages * PAGE_SIZE
    wbuf = state_win_kv.shape[2]
    tp = SAMPLE_PAD_T
    assert TS <= tp and wbuf % tp == 0 and T % 128 == 0

    l = 0
    inproj_w, cmp_w, mixout_w = _prep_weights(
        w_in[l], b_in[l], q_norm_g[l], k_norm_g[l], cmp_pe_k[l], cmp_pe_v[l], cmp_w_k[l], cmp_w_v[l],
        w_up_m[l], w_up_a[l], w_out[l], w_router[l], b_router[l])
    wbd, pe, kg0 = cmp_w
    gmix = g_mix[l].reshape(1, D)
    gffn = g_ffn[l].reshape(1, D)

    nc = B + DB
    nc_pad = -(-nc // SUBLANES) * SUBLANES
    c_all = jnp.concatenate([c_prompt, c_sample, jnp.zeros((nc_pad - nc, D), F32)], axis=0)
    mod = _adaln(c_all, w_ada[l], b_ada[l])
    mod_p = mod[:B].reshape(B, 1, 6 * D)
    mod_s = jnp.repeat(mod[B:B + DB], tp, axis=0).reshape(1, DB * tp, 6 * D)

    mp = B * T
    tm = _pick_tile(T, 256)
    xp = x_prompt.reshape(mp, D)
    tabs_p = _rope_tables(jnp.arange(T, dtype=jnp.int32))
    mq, mk, mv, q, qr, rows, win, small = _inproj(xp, mod_p, gmix, tabs_p, inproj_w, tm, T // tm, T // tm)
    Lp = _pick_tile(T, 128)
    hm, C_p, n_p, m_p = _mlstm(mq, mk, mv, small, B, T, T, Lp)
    o_nsa = _nsa_prompt(q, qr, small, rows, win, wbd, pe, kg0, B, T)
    x1_p, h2_p, lg_p = _mixout(xp, hm, o_nsa, mod_p, gmix, gffn, mixout_w, tm, T // tm)

    ms = DB * tp
    xs_pad = jnp.concatenate([x_sample, jnp.zeros((DB, tp - TS, D), F32)], axis=1).reshape(ms, D)
    pos_s = past_len + jnp.tile(jnp.arange(tp, dtype=jnp.int32), DB)
    tabs_s = _rope_tables(pos_s)
    mq_s, mk_s, mv_s, q_s, qr_s, rows_s, win_s, small_s = _inproj(xs_pad, mod_s, gmix, tabs_s, inproj_w, ms, 1, 1)
    hm_s, C_s, n_s, m_s = _mlstm(mq_s, mk_s, mv_s, small_s, DB, tp, TS, tp,
                                 state=(state_mlstm_C[l], state_mlstm_n[l], state_mlstm_m[l]))
    cache2 = cache_nsa_kv[l].reshape(cache_nsa_kv.shape[1], PAGE_SIZE, 4 * LANES)
    winbuf = state_win_kv[l].reshape(DB, wbuf, 2 * LANES)
    o_nsa_s, win_out_s = _nsa_sample(page_table, cache2, q_s, qr_s, small_s, rows_s, win_s, winbuf,
                                     wbd, pe, kg0, TS)
    x1_s, h2_s, lg_s = _mixout(xs_pad, hm_s, o_nsa_s, mod_s, gmix, gffn, mixout_w, ms, 1)

    valid = lambda a: a.reshape(DB, tp, -1)[:, :TS].reshape(DB * TS, -1)
    h2_all = jnp.concatenate([h2_p, valid(h2_s)], axis=0)
    lg_all = jnp.concatenate([lg_p, valid(lg_s)], axis=0)
    moe = _moe(h2_all, lg_all, w_gu[l], b_gu[l], w_dn[l], b_dn[l])
    gt2_p = mod[:B, 5 * D:6 * D][:, None, :]
    gt2_s = mod[B:B + DB, 5 * D:6 * D][:, None, :]
    y_p = x1_p.reshape(B, T, D) + gt2_p * moe[:mp].reshape(B, T, D)
    y_s = valid(x1_s).reshape(DB, TS, D) + gt2_s * moe[mp:].reshape(DB, TS, D)

    kv_p = rows.reshape(1, B, T, 4, A_KV, A_DH)
    kv_s = valid(rows_s).reshape(1, DB, TS, 4, A_KV, A_DH)
    wp = min(WINDOW, T)
    win_p = win.reshape(B, T, 2, A_KV, A_DH)[:, T - wp:][None]
    win_s_out = win_out_s.reshape(1, DB, wbuf, 2, A_KV, A_DH)
    return (y_p, y_s, kv_p, kv_s, win_p, win_s_out,
            C_p[None], n_p[None], m_p[None], C_s[None], n_s[None], m_s[None])
```

```python
import functools
import math

import jax
import jax.numpy as jnp
from jax import lax
from jax.experimental import pallas as pl
from jax.experimental.pallas import tpu as pltpu

F32 = jnp.float32
BF16 = jnp.bfloat16

D_MODEL = 1024
M_HEADS = 4
M_DH = 128
M_WIDTH = M_HEADS * M_DH
A_HEADS = 8
A_KV = 2
A_HPG = A_HEADS // A_KV
A_DH = 64
A_WIDTH = A_HEADS * A_DH
CMP_STRIDE = 16
CMP_LEN = 32
SEL_LEN = 64
N_SEL = 16
WINDOW = 512
PAGE_SIZE = 128
ROPE_THETA = 500000.0
ROT_DIM = A_DH // 4
ATT_SCALE = A_DH ** -0.5
N_EXPERTS = 32
TOP_K = 4
D_EXPERT = D_MODEL
SWIGLU_LIMIT = 7.0
SWIGLU_ALPHA = 1.702
EPS = 1e-6

OFF_MQ, OFF_MK, OFF_MV, OFF_MO = 0, M_WIDTH, 2 * M_WIDTH, 3 * M_WIDTH
OFF_MI = 4 * M_WIDTH
OFF_MF = OFF_MI + M_HEADS
OFF_AQ = OFF_MF + M_HEADS
OFF_AKV = OFF_AQ + A_WIDTH
OFF_AG = OFF_AKV + 6 * A_KV * A_DH
OFF_GA = OFF_AG + 3 * A_HEADS
OFF_GB = OFF_GA + D_MODEL
N_IN = OFF_GB + D_MODEL

LANES = 128
SUBLANES = 8
VMEM_LIMIT = 56 * 1024 * 1024

NEG_BIG = -1e30
M_INIT = -1e29
LOG2E = 1.4426950408889634
SAMPLE_PAD_T = 8


def _cparams(sem):
    return pltpu.CompilerParams(dimension_semantics=sem, vmem_limit_bytes=VMEM_LIMIT)


def _bdot(a, b):
    return jnp.dot(a.astype(BF16), b.astype(BF16), preferred_element_type=F32)


def _bdot_t(a, b):
    return lax.dot_general(a.astype(BF16), b.astype(BF16), (((1,), (1,)), ((), ())),
                           preferred_element_type=F32)


def _split(a):
    hi = a.astype(BF16)
    lo = (a - hi.astype(F32)).astype(BF16)
    return hi, lo


def _dot3(a, b):
    ah, al = _split(a)
    bh, bl = _split(b)
    return (jnp.dot(ah, bh, preferred_element_type=F32) + jnp.dot(al, bh, preferred_element_type=F32)
            + jnp.dot(ah, bl, preferred_element_type=F32))


def _dot2_exact_rhs(a, b_bf16):
    ah, al = _split(a)
    return jnp.dot(ah, b_bf16, preferred_element_type=F32) + jnp.dot(al, b_bf16, preferred_element_type=F32)


def _sigmoid(x):
    return 1.0 / (1.0 + jnp.exp(-x))


def _rmsnorm_rows(x, g):
    return x * lax.rsqrt(jnp.mean(x * x, axis=-1, keepdims=True) + EPS) * g


def _adaln_kernel(c_ref, w_ref, b_ref, o_ref):
    c = c_ref[...]
    s = c * _sigmoid(c)
    o_ref[...] = _dot3(s, w_ref[...]) + b_ref[...]


def _adaln(c, w, b):
    mc, d = c.shape
    n = w.shape[1]
    tn = 1024
    return pl.pallas_call(
        _adaln_kernel,
        grid=(n // tn,),
        in_specs=[pl.BlockSpec((mc, d), lambda j: (0, 0)),
                  pl.BlockSpec((d, tn), lambda j: (0, j)),
                  pl.BlockSpec((1, tn), lambda j: (0, j))],
        out_specs=pl.BlockSpec((mc, tn), lambda j: (0, j)),
        out_shape=jax.ShapeDtypeStruct((mc, n), F32),
        compiler_params=_cparams(("parallel",)),
        name="adaln",
    )(c, w, b.reshape(1, n))


def _head_norm(z, bd, gain):
    ms = _dot2_exact_rhs(z * z, bd)
    return z * lax.rsqrt(ms + EPS) * gain


def _rope(z, cos, s_prev, s_next):
    w = z.shape[1]
    rep = w // LANES
    if rep > 1:
        cos = jnp.concatenate([cos] * rep, axis=1)
        s_prev = jnp.concatenate([s_prev] * rep, axis=1)
        s_next = jnp.concatenate([s_next] * rep, axis=1)
    z_prev = pltpu.roll(z, ROT_DIM // 2, 1)
    z_next = pltpu.roll(z, w - ROT_DIM // 2, 1)
    return z * cos + z_prev * s_prev + z_next * s_next


def _inproj_kernel(x_ref, mod_ref, gmix_ref, cos_ref, sp_ref, sn_ref,
                   wm_ref, bm_ref, wq_ref, bq_ref, wkv_ref, bkv_ref, ws_ref, bs_ref,
                   qg_ref, kg_ref, bd_ref,
                   mq_ref, mk_ref, mv_ref, q_ref, qr_ref, rows_ref, win_ref, small_ref):
    x = x_ref[...]
    sh1 = mod_ref[:, 0:D_MODEL]
    sc1 = mod_ref[:, D_MODEL:2 * D_MODEL]
    h = _rmsnorm_rows(x, gmix_ref[...]) * (1.0 + sc1) + sh1
    hb = h.astype(BF16)

    mq_ref[...] = jnp.dot(hb, wm_ref[:, 0:M_WIDTH], preferred_element_type=F32) + bm_ref[:, 0:M_WIDTH]
    mk = jnp.dot(hb, wm_ref[:, M_WIDTH:2 * M_WIDTH], preferred_element_type=F32) + bm_ref[:, M_WIDTH:2 * M_WIDTH]
    mk_ref[...] = mk * (M_DH ** -0.5)
    mv_ref[...] = (jnp.dot(hb, wm_ref[:, 2 * M_WIDTH:3 * M_WIDTH], preferred_element_type=F32)
                   + bm_ref[:, 2 * M_WIDTH:3 * M_WIDTH])

    cos, sp, sn = cos_ref[...], sp_ref[...], sn_ref[...]
    zq = jnp.dot(hb, wq_ref[...], preferred_element_type=F32) + bq_ref[...]
    qn = _head_norm(zq, bd_ref[...], qg_ref[...])
    q_ref[...] = qn
    qr_ref[...] = _rope(qn, cos, sp, sn)

    zkv = jnp.dot(hb, wkv_ref[...], preferred_element_type=F32) + bkv_ref[...]
    bd2 = bd_ref[0:LANES, 0:LANES]
    rows_ref[:, 0:2 * LANES] = zkv[:, 0:2 * LANES]
    ksel = _head_norm(zkv[:, 2 * LANES:3 * LANES], bd2, kg_ref[0:1, :])
    rows_ref[:, 2 * LANES:3 * LANES] = _rope(ksel, cos, sp, sn)
    rows_ref[:, 3 * LANES:4 * LANES] = zkv[:, 3 * LANES:4 * LANES]
    kwin = _head_norm(zkv[:, 4 * LANES:5 * LANES], bd2, kg_ref[1:2, :])
    win_ref[:, 0:LANES] = _rope(kwin, cos, sp, sn)
    win_ref[:, LANES:2 * LANES] = zkv[:, 5 * LANES:6 * LANES]

    small_ref[...] = _dot3(h, ws_ref[...]) + bs_ref[...]


def _inproj(x2, mod3, gmix, tabs, wts, tm, tiles_per_mod, pos_tiles):
    m = x2.shape[0]
    cos_t, sp_t, sn_t = tabs
    (wm, bm, wq, bq, wkv, bkv, ws, bs, qg, kg, bd) = wts
    r = mod3.shape[1]
    row = lambda i: (i, 0)
    const = lambda i: (0, 0)
    tab = lambda i: (i % pos_tiles, 0)
    in_specs = [
        pl.BlockSpec((tm, D_MODEL), row),
        pl.BlockSpec((None, r, 6 * D_MODEL), lambda i: (i // tiles_per_mod, 0, 0)),
        pl.BlockSpec((1, D_MODEL), const),
        pl.BlockSpec((tm, LANES), tab), pl.BlockSpec((tm, LANES), tab), pl.BlockSpec((tm, LANES), tab),
        pl.BlockSpec(wm.shape, const), pl.BlockSpec(bm.shape, const),
        pl.BlockSpec(wq.shape, const), pl.BlockSpec(bq.shape, const),
        pl.BlockSpec(wkv.shape, const), pl.BlockSpec(bkv.shape, const),
        pl.BlockSpec(ws.shape, const), pl.BlockSpec(bs.shape, const),
        pl.BlockSpec(qg.shape, const), pl.BlockSpec(kg.shape, const), pl.BlockSpec(bd.shape, const),
    ]
    widths = (M_WIDTH, M_WIDTH, M_WIDTH, A_WIDTH, A_WIDTH, 4 * LANES, 2 * LANES, LANES)
    return pl.pallas_call(
        _inproj_kernel,
        grid=(m // tm,),
        in_specs=in_specs,
        out_specs=[pl.BlockSpec((tm, w), row) for w in widths],
        out_shape=[jax.ShapeDtypeStruct((m, w), F32) for w in widths],
        compiler_params=_cparams(("parallel",)),
        name="inproj",
    )(x2, mod3, gmix, cos_t, sp_t, sn_t, wm, bm, wq, bq, wkv, bkv, ws, bs, qg, kg, bd)


def _log_sigmoid(x):
    return jnp.minimum(x, 0.0) - jnp.log(1.0 + jnp.exp(-jnp.abs(x)))


def _mlstm_kernel(*refs, L, t_valid, has_state):
    if has_state:
        q_ref, k_ref, v_ref, s_ref, c0_ref, n0_ref, m0_ref, h_ref, c_ref, n_ref, m_ref = refs
    else:
        q_ref, k_ref, v_ref, s_ref, h_ref, c_ref, n_ref, m_ref = refs
    c = pl.program_id(1)

    @pl.when(c == 0)
    def _():
        if has_state:
            c_ref[...] = c0_ref[...]
            n_ref[...] = n0_ref[...]
            m_ref[...] = m0_ref[...]
        else:
            c_ref[...] = jnp.zeros(c_ref.shape, F32)
            n_ref[...] = jnp.zeros(n_ref.shape, F32)
            m_ref[...] = jnp.zeros(m_ref.shape, F32)

    row = lax.broadcasted_iota(jnp.int32, (L, L), 0)
    col = lax.broadcasted_iota(jnp.int32, (L, L), 1)
    causal = col <= row
    eye = col == row
    tok_col = c * L + lax.broadcasted_iota(jnp.int32, (L, 1), 0)
    valid_col = tok_col < t_valid
    for hd in range(M_HEADS):
        lo, hi = hd * M_DH, (hd + 1) * M_DH
        q = q_ref[:, lo:hi]
        k = k_ref[:, lo:hi]
        v = v_ref[:, lo:hi]
        i_col = s_ref[:, hd:hd + 1]
        lf_col = _log_sigmoid(s_ref[:, M_HEADS + hd:M_HEADS + hd + 1])
        lf_col = jnp.where(valid_col, lf_col, 0.0)
        i_col = jnp.where(valid_col, i_col, -jnp.inf)
        i_row = jnp.sum(jnp.where(eye, i_col, 0.0), axis=0, keepdims=True)
        lf_row = jnp.sum(jnp.where(eye, lf_col, 0.0), axis=0, keepdims=True)
        b_col = jnp.sum(jnp.where(causal, lf_row, 0.0), axis=1, keepdims=True)
        b_row = jnp.sum(jnp.where(row <= col, lf_col, 0.0), axis=0, keepdims=True)
        m_prev = m_ref[:, hd:hd + 1]
        dmat = jnp.where(causal, b_col - b_row + i_row, -jnp.inf)
        inter = b_col + m_prev
        m_row = jnp.maximum(jnp.max(dmat, axis=1, keepdims=True), inter)
        w = jnp.exp(dmat - m_row)
        w_inter = jnp.exp(inter - m_row)
        s = _bdot_t(q, k) * w
        cm = c_ref[hd]
        nv = n_ref[hd]
        num = _bdot(s, v) + w_inter * _bdot_t(q, cm)
        den = jnp.sum(s, axis=1, keepdims=True) + w_inter * jnp.sum(q * nv, axis=1, keepdims=True)
        h_ref[:, lo:hi] = num / jnp.maximum(jnp.abs(den), jnp.exp(-m_row))
        b_last = b_col[L - 1:L, :]
        dec_col = b_last - b_col + i_col
        dec_row = b_last - b_row + i_row
        m_new = jnp.maximum(b_last + m_prev, jnp.max(dec_row, axis=1, keepdims=True))
        ws_col = jnp.exp(dec_col - m_new)
        wc = jnp.exp(b_last + m_prev - m_new)
        vw = (v * ws_col).astype(BF16)
        upd = lax.dot_general(vw, k.astype(BF16), (((0,), (0,)), ((), ())), preferred_element_type=F32)
        c_ref[hd] = wc * cm + upd
        n_ref[hd] = wc * nv + jnp.sum(k * ws_col, axis=0, keepdims=True)
        m_ref[:, hd:hd + 1] = m_new


def _mlstm(mq, mk, mv, small, nb, t_pad, t_valid, L, state=None):
    nc = t_pad // L
    has_state = state is not None
    blk = lambda b, c: (b * nc + c, 0)
    st4 = lambda b, c: (b, 0, 0, 0)
    st3 = lambda b, c: (b, 0, 0)
    in_specs = [pl.BlockSpec((L, M_WIDTH), blk)] * 3 + [pl.BlockSpec((L, LANES), blk)]
    args = [mq, mk, mv, small]
    if has_state:
        c0, n0, m0 = state
        in_specs += [pl.BlockSpec((None, M_HEADS, M_DH, M_DH), st4),
                     pl.BlockSpec((None, M_HEADS, 1, M_DH), st4),
                     pl.BlockSpec((None, 1, M_HEADS), st3)]
        args += [c0, n0.reshape(nb, M_HEADS, 1, M_DH), m0.reshape(nb, 1, M_HEADS)]
    out_specs = [pl.BlockSpec((L, M_WIDTH), blk),
                 pl.BlockSpec((None, M_HEADS, M_DH, M_DH), st4),
                 pl.BlockSpec((None, M_HEADS, 1, M_DH), st4),
                 pl.BlockSpec((None, 1, M_HEADS), st3)]
    out_shape = [jax.ShapeDtypeStruct((nb * t_pad, M_WIDTH), F32),
                 jax.ShapeDtypeStruct((nb, M_HEADS, M_DH, M_DH), F32),
                 jax.ShapeDtypeStruct((nb, M_HEADS, 1, M_DH), F32),
                 jax.ShapeDtypeStruct((nb, 1, M_HEADS), F32)]
    h, cs, ns, ms = pl.pallas_call(
        functools.partial(_mlstm_kernel, L=L, t_valid=t_valid, has_state=has_state),
        grid=(nb, nc),
        in_specs=in_specs,
        out_specs=out_specs,
        out_shape=out_shape,
        compiler_params=_cparams(("parallel", "arbitrary")),
        name="mlstm",
    )(*args)
    return h, cs, ns.reshape(nb, M_HEADS, M_DH), ms.reshape(nb, M_HEADS)


def _stack_heads(qt, g):
    t = qt.shape[0]
    z = jnp.zeros((t, A_DH), F32)
    parts = []
    for hh in range(A_HPG):
        hd = g * A_HPG + hh
        qh = qt[:, hd * A_DH:(hd + 1) * A_DH] * (ATT_SCALE * LOG2E)
        parts.append(jnp.concatenate([qh, z], axis=1) if g == 0 else jnp.concatenate([z, qh], axis=1))
    return jnp.concatenate(parts, axis=0).astype(BF16)


def _gate_cols(small, g, br):
    cols = []
    for hh in range(A_HPG):
        c0 = 2 * M_HEADS + (g * A_HPG + hh) * 3 + br
        cols.append(_sigmoid(small[:, c0:c0 + 1]))
    return jnp.concatenate(cols, axis=0)


def _compress(k_ref, v_ref, nseg, wbd_ref, pe_ref, kg0):
    acc_lo = jnp.zeros((nseg, 2 * LANES), F32)
    acc_hi = jnp.zeros((nseg, 2 * LANES), F32)
    for l in range(CMP_STRIDE):
        xl = jnp.concatenate([k_ref[pl.ds(l, nseg, stride=CMP_STRIDE), :],
                              v_ref[pl.ds(l, nseg, stride=CMP_STRIDE), :]], axis=1)
        acc_lo = acc_lo + _bdot(xl + pe_ref[l], wbd_ref[l])
        acc_hi = acc_hi + _bdot(xl + pe_ref[CMP_STRIDE + l], wbd_ref[CMP_STRIDE + l])
    kv = acc_lo + pltpu.roll(acc_hi, nseg - 1, 0)
    kc = kv[:, 0:LANES]
    vc = kv[:, LANES:2 * LANES]
    lane = lax.broadcasted_iota(jnp.int32, (nseg, LANES), 1)
    sq = kc * kc
    ms0 = jnp.sum(jnp.where(lane < A_DH, sq, 0.0), axis=1, keepdims=True) * (1.0 / A_DH)
    ms1 = jnp.sum(jnp.where(lane >= A_DH, sq, 0.0), axis=1, keepdims=True) * (1.0 / A_DH)
    ms = jnp.where(lane < A_DH, ms0, ms1)
    kc = kc * lax.rsqrt(ms + EPS) * kg0
    return kc, vc


def _cmp_branch(qn_g, kc_b, vc_b, tpos_rows, nseg, n_tok):
    s = _bdot_t(qn_g, kc_b)
    nidx = lax.broadcasted_iota(jnp.int32, (1, nseg), 1)
    vis = (nidx * CMP_STRIDE + (CMP_LEN - 1)) <= tpos_rows
    sm = jnp.where(vis, s, NEG_BIG)
    mx = jnp.max(sm, axis=1, keepdims=True)
    e = jnp.where(vis, jnp.exp2(sm - mx), 0.0)
    d = jnp.sum(e, axis=1, keepdims=True)
    p = e / jnp.where(d > 0, d, 1.0)
    o = _bdot(p, vc_b)
    imp = p[0:n_tok]
    for hh in range(1, A_HPG):
        imp = imp + p[hh * n_tok:(hh + 1) * n_tok]
    return o, imp


def _masked_attn_direct(q_g, k_parts, v_parts, allowed_parts):
    ss = [jnp.where(al, _bdot_t(q_g, kk), NEG_BIG) for kk, al in zip(k_parts, allowed_parts)]
    mx = ss[0].max(axis=1, keepdims=True)
    for s in ss[1:]:
        mx = jnp.maximum(mx, s.max(axis=1, keepdims=True))
    num = None
    den = None
    for s, al, vv in zip(ss, allowed_parts, v_parts):
        e = jnp.where(al, jnp.exp2(s - mx), 0.0)
        dd = jnp.sum(e, axis=1, keepdims=True)
        oo = _bdot(e, vv)
        num = oo if num is None else num + oo
        den = dd if den is None else den + dd
    return num / jnp.where(den > 0, den, 1.0)


def _assemble_heads(o_groups, n_tok):
    pieces = []
    for g in range(A_KV):
        for hh in range(A_HPG):
            pieces.append(o_groups[g][hh * n_tok:(hh + 1) * n_tok, g * A_DH:(g + 1) * A_DH])
    return jnp.concatenate(pieces, axis=1)


def _lane_tile(a, width):
    rep = width // LANES
    return a if rep == 1 else jnp.concatenate([a] * rep, axis=1)


def _add_bias(s, bias):
    t, k = bias.shape
    return (s.reshape(A_HPG, t, k) + bias[None]).reshape(A_HPG * t, k)


def _nsa_prompt_kernel(q_ref, qr_ref, small_ref, rows_ref, win_ref, wbd_ref, pe_ref, kg0_ref,
                       pool_ref, o_ref,
                       kraw_sc, vraw_sc, kc_sc, vc_sc, m_sc, l_sc, acc_sc, *, T, tq, kc_len):
    qi = pl.program_id(1)
    nseg = T // CMP_STRIDE
    nsb = T // SEL_LEN

    @pl.when(qi == 0)
    def _():
        kraw_sc[...] = rows_ref[:, 0:LANES]
        vraw_sc[...] = rows_ref[:, LANES:2 * LANES]
        kc, vc = _compress(kraw_sc, vraw_sc, nseg, wbd_ref, pe_ref, kg0_ref[...])
        kc_sc[...] = kc
        vc_sc[...] = vc

    t0 = qi * tq
    tpos_col = t0 + lax.broadcasted_iota(jnp.int32, (tq, 1), 0)
    tpos_rows = jnp.concatenate([tpos_col] * A_HPG, axis=0)
    tpos_lane = t0 + lax.broadcasted_iota(jnp.int32, (1, tq), 1)
    q = q_ref[...]
    qr = qr_ref[...]
    small = small_ref[...]
    kc_b = kc_sc[...].astype(BF16)
    vc_b = vc_sc[...].astype(BF16)
    bidx = lax.broadcasted_iota(jnp.int32, (nsb, tq), 0)
    cur = tpos_lane // SEL_LEN
    r4 = A_HPG * tq
    qr_gs = [_stack_heads(qr, g) for g in range(A_KV)]
    o_cmps = []
    sel_bs = []
    for g in range(A_KV):
        qn_g = _stack_heads(q, g)
        o_cmp, imp = _cmp_branch(qn_g, kc_b, vc_b, tpos_rows, nseg, tq)
        o_cmps.append(o_cmp)
        imp_sel = _dot2_exact_rhs(imp, pool_ref[...])
        imp_t = jnp.transpose(imp_sel)[0:nsb, :]
        val = jnp.where(bidx < cur, imp_t, -1.0)
        rank = jnp.zeros((nsb, tq), F32)
        for bp in range(nsb):
            vb = val[bp:bp + 1, :]
            ahead = jnp.where(vb > val, 1.0, jnp.where((vb == val) & (bidx > bp), 1.0, 0.0))
            rank = rank + ahead
        sel_t = jnp.where(((rank < (N_SEL - 1)) & (bidx < cur)) | (bidx == cur), 1.0, 0.0)
        if nsb < LANES:
            sel_t = jnp.concatenate([sel_t, jnp.zeros((LANES - nsb, tq), F32)], axis=0)
        sel_bs.append(jnp.transpose(sel_t).astype(BF16))

    m_sc[...] = jnp.full(m_sc.shape, M_INIT, F32)
    l_sc[...] = jnp.zeros(l_sc.shape, F32)
    acc_sc[...] = jnp.zeros(acc_sc.shape, F32)

    def sel_body(c, carry):
        k0 = pl.multiple_of(c * kc_len, kc_len)
        kb = rows_ref[pl.ds(k0, kc_len), 2 * LANES:3 * LANES].astype(BF16)
        vb = rows_ref[pl.ds(k0, kc_len), 3 * LANES:4 * LANES].astype(BF16)
        kpos = k0 + lax.broadcasted_iota(jnp.int32, (1, kc_len), 1)
        causal = kpos <= tpos_col
        kblk = (k0 + lax.broadcasted_iota(jnp.int32, (LANES, kc_len), 1)) // SEL_LEN
        expand = jnp.where(kblk == lax.broadcasted_iota(jnp.int32, (LANES, kc_len), 0), 1.0, 0.0).astype(BF16)
        for g in range(A_KV):
            mk = jnp.dot(sel_bs[g], expand, preferred_element_type=F32)
            bias = jnp.where(causal, (mk - 1.0) * (-NEG_BIG), NEG_BIG)
            sm = _add_bias(_bdot_t(qr_gs[g], kb), bias)
            m_prev = m_sc[g]
            m_new = jnp.maximum(m_prev, jnp.max(sm, axis=1, keepdims=True))
            alpha = jnp.exp2(m_prev - m_new)
            p = jnp.exp2(sm - _lane_tile(m_new, kc_len))
            l_sc[g] = alpha * l_sc[g] + jnp.sum(p, axis=1, keepdims=True)
            acc_sc[g] = alpha * acc_sc[g] + _bdot(p, vb)
            m_sc[g] = m_new
        return carry

    lax.fori_loop(0, (t0 + tq + kc_len - 1) // kc_len, sel_body, 0)

    wk = min(WINDOW + tq, T)
    w0 = pl.multiple_of(jnp.clip(t0 + tq - wk, 0, T - wk), tq)
    kw = win_ref[pl.ds(w0, wk), 0:LANES].astype(BF16)
    vw = win_ref[pl.ds(w0, wk), LANES:2 * LANES].astype(BF16)
    wdiff = tpos_col - (w0 + lax.broadcasted_iota(jnp.int32, (1, wk), 1))
    wbias = jnp.where((wdiff >= 0) & (wdiff < WINDOW), 0.0, NEG_BIG)

    o_groups = []
    for g in range(A_KV):
        l = l_sc[g]
        o_sel = acc_sc[g] / jnp.where(l > 0, l, 1.0)
        sw = _add_bias(_bdot_t(qr_gs[g], kw), wbias)
        mw = jnp.broadcast_to(jnp.max(sw, axis=1, keepdims=True), (r4, LANES))
        pw = jnp.exp2(sw - _lane_tile(mw, wk))
        o_win = _bdot(pw, vw) / jnp.broadcast_to(jnp.sum(pw, axis=1, keepdims=True), (r4, LANES))
        o_groups.append(_gate_cols(small, g, 0) * o_cmps[g] + _gate_cols(small, g, 1) * o_sel
                        + _gate_cols(small, g, 2) * o_win)
    o_ref[...] = _assemble_heads(o_groups, tq)


def _nsa_prompt(q, qr, small, rows, win, wbd, pe, kg0, nb, T):
    tq = 128
    kc_len = _pick_tile(T, 512)
    nq = T // tq
    nseg = T // CMP_STRIDE
    nsb = T // SEL_LEN
    pool = (jnp.arange(nseg)[:, None] // (SEL_LEN // CMP_STRIDE) == jnp.arange(LANES)[None, :]).astype(BF16)
    tile = lambda b, i: (b * nq + i, 0)
    per_b = lambda b, i: (b, 0)
    c2 = lambda b, i: (0, 0)
    c3 = lambda b, i: (0, 0, 0)
    r4 = A_HPG * tq
    return pl.pallas_call(
        functools.partial(_nsa_prompt_kernel, T=T, tq=tq, kc_len=kc_len),
        grid=(nb, nq),
        in_specs=[pl.BlockSpec((tq, A_WIDTH), tile), pl.BlockSpec((tq, A_WIDTH), tile),
                  pl.BlockSpec((tq, LANES), tile),
                  pl.BlockSpec((T, 4 * LANES), per_b), pl.BlockSpec((T, 2 * LANES), per_b),
                  pl.BlockSpec(wbd.shape, c3), pl.BlockSpec(pe.shape, c3), pl.BlockSpec(kg0.shape, c2),
                  pl.BlockSpec(pool.shape, c2)],
        out_specs=pl.BlockSpec((tq, A_WIDTH), tile),
        out_shape=jax.ShapeDtypeStruct((nb * T, A_WIDTH), F32),
        scratch_shapes=[pltpu.VMEM((T, LANES), F32), pltpu.VMEM((T, LANES), F32),
                        pltpu.VMEM((nseg, LANES), F32), pltpu.VMEM((nseg, LANES), F32),
                        pltpu.VMEM((A_KV, r4, LANES), F32), pltpu.VMEM((A_KV, r4, LANES), F32),
                        pltpu.VMEM((A_KV, r4, LANES), F32)],
        compiler_params=_cparams(("parallel", "arbitrary")),
        name="nsa_prompt",
    )(q, qr, small, rows, win, wbd, pe, kg0, pool)


def _nsa_sample_kernel(pt_ref, cache_ref, q_ref, qr_ref, small_ref, rows_ref, winnew_ref, winbuf_ref,
                       wbd_ref, pe_ref, kg0_ref, pool_ref, expand_ref,
                       o_ref, winout_ref,
                       cmp_buf, sel_buf, sems, *, n_pages, past_len, t_valid):
    b = pl.program_id(0)
    nb = pl.num_programs(0)
    tp = SAMPLE_PAD_T
    nseg = past_len // CMP_STRIDE
    nsb = past_len // SEL_LEN
    wbuf = winbuf_ref.shape[0]

    def page_copies(bb, p, phase):
        page = pt_ref[bb * n_pages + p]
        dst_rows = pl.ds(p * PAGE_SIZE, PAGE_SIZE)
        if phase == 0:
            return [pltpu.make_async_copy(cache_ref.at[page, :, pl.ds(j * LANES, LANES)],
                                          cmp_buf.at[j, dst_rows, :], sems.at[0]) for j in range(2)]
        return [pltpu.make_async_copy(cache_ref.at[page, :, pl.ds(2 * LANES, 2 * LANES)],
                                      sel_buf.at[dst_rows, :], sems.at[1])]

    def start_all(bb, phase):
        def body(p, c):
            for cp in page_copies(bb, p, phase):
                cp.start()
            return c
        lax.fori_loop(0, n_pages, body, 0)

    def wait_all(bb, phase):
        def body(p, c):
            for cp in page_copies(bb, p, phase):
                cp.wait()
            return c
        lax.fori_loop(0, n_pages, body, 0)

    @pl.when(b == 0)
    def _():
        start_all(b, 0)

    start_all(b, 1)
    wait_all(b, 0)

    kc, vc = _compress(cmp_buf.at[0], cmp_buf.at[1], nseg, wbd_ref, pe_ref, kg0_ref[...])
    kc_b = kc.astype(BF16)
    vc_b = vc.astype(BF16)
    q = q_ref[...]
    qr = qr_ref[...]
    small = small_ref[...]
    tpos_col = past_len + lax.broadcasted_iota(jnp.int32, (tp, 1), 0)
    tpos_rows = jnp.concatenate([tpos_col] * A_HPG, axis=0)
    bp_idx = lax.broadcasted_iota(jnp.int32, (nsb, nsb), 0)
    b_idx = lax.broadcasted_iota(jnp.int32, (nsb, nsb), 1)
    o_cmps = []
    sels = []
    for g in range(A_KV):
        qn_g = _stack_heads(q, g)
        o_cmp, imp = _cmp_branch(qn_g, kc_b, vc_b, tpos_rows, nseg, tp)
        o_cmps.append(o_cmp)
        imp_sel = _dot2_exact_rhs(imp, pool_ref[...])
        imp_pad = jnp.concatenate([imp_sel, jnp.zeros((nsb - tp, nsb), F32)], axis=0)
        imp_t = jnp.transpose(imp_pad)
        rows_sel = []
        for t in range(tp):
            if t < t_valid:
                row_t = imp_sel[t:t + 1, :]
                col_t = imp_t[:, t:t + 1]
                ahead = jnp.where(col_t > row_t, 1.0, jnp.where((col_t == row_t) & (bp_idx < b_idx), 1.0, 0.0))
                rank = jnp.sum(ahead, axis=0, keepdims=True)
                rows_sel.append(jnp.where(rank < (N_SEL - 1), 1.0, 0.0))
            else:
                rows_sel.append(jnp.zeros((1, nsb), F32))
        sels.append(jnp.concatenate(rows_sel, axis=0).astype(BF16))

    @pl.when(b + 1 < nb)
    def _():
        start_all(b + 1, 0)

    wait_all(b, 1)

    new_idx = lax.broadcasted_iota(jnp.int32, (tp, tp), 1)
    tok_idx = lax.broadcasted_iota(jnp.int32, (tp, tp), 0)
    new_ok = jnp.concatenate([jnp.where(new_idx <= tok_idx, 1.0, 0.0)] * A_HPG, axis=0) > 0.5
    wpos = past_len - wbuf + lax.broadcasted_iota(jnp.int32, (1, wbuf), 1)
    wdiff = tpos_col - wpos
    win_ok = jnp.concatenate([jnp.where((wdiff >= 0) & (wdiff < WINDOW), 1.0, 0.0)] * A_HPG, axis=0) > 0.5
    k_past = sel_buf[:, 0:LANES].astype(BF16)
    v_past = sel_buf[:, LANES:2 * LANES].astype(BF16)
    k_new = rows_ref[:, 2 * LANES:3 * LANES]
    v_new = rows_ref[:, 3 * LANES:4 * LANES]
    kw_past = winbuf_ref[:, 0:LANES]
    vw_past = winbuf_ref[:, LANES:2 * LANES]
    kw_new = winnew_ref[:, 0:LANES]
    vw_new = winnew_ref[:, LANES:2 * LANES]
    o_groups = []
    for g in range(A_KV):
        qr_g = _stack_heads(qr, g)
        mk = jnp.dot(sels[g], expand_ref[...], preferred_element_type=F32)
        past_ok = jnp.concatenate([mk] * A_HPG, axis=0) > 0.5
        o_sel = _masked_attn_direct(qr_g, [k_past, k_new], [v_past, v_new], [past_ok, new_ok])
        o_win = _masked_attn_direct(qr_g, [kw_past, kw_new], [vw_past, vw_new], [win_ok, new_ok])
        o_groups.append(_gate_cols(small, g, 0) * o_cmps[g] + _gate_cols(small, g, 1) * o_sel
                        + _gate_cols(small, g, 2) * o_win)
    o_ref[...] = _assemble_heads(o_groups, tp)

    wb = winbuf_ref[...]
    rolled = pltpu.roll(wb, wbuf - t_valid, 0)
    newr = pltpu.roll(winnew_ref[...], tp - t_valid, 0)
    sub = lax.broadcasted_iota(jnp.int32, (tp, 2 * LANES), 0)
    winout_ref[0:wbuf - tp, :] = rolled[0:wbuf - tp, :]
    winout_ref[wbuf - tp:wbuf, :] = jnp.where(sub < tp - t_valid, rolled[wbuf - tp:wbuf, :], newr)


def _nsa_sample(page_table, cache, q, qr, small, rows, winnew, winbuf, wbd, pe, kg0, t_valid):
    nb, n_pages = page_table.shape
    past_len = n_pages * PAGE_SIZE
    nseg = past_len // CMP_STRIDE
    nsb = past_len // SEL_LEN
    tp = SAMPLE_PAD_T
    wbuf = winbuf.shape[1]
    pool = (jnp.arange(nseg)[:, None] // (SEL_LEN // CMP_STRIDE) == jnp.arange(nsb)[None, :]).astype(BF16)
    expand = (jnp.arange(nsb)[:, None] == jnp.arange(past_len)[None, :] // SEL_LEN).astype(BF16)
    tile = lambda b, pt: (b, 0)
    c2 = lambda b, pt: (0, 0)
    c3 = lambda b, pt: (0, 0, 0)
    gs = pltpu.PrefetchScalarGridSpec(
        num_scalar_prefetch=1,
        grid=(nb,),
        in_specs=[pl.BlockSpec(memory_space=pl.ANY),
                  pl.BlockSpec((tp, A_WIDTH), tile), pl.BlockSpec((tp, A_WIDTH), tile),
                  pl.BlockSpec((tp, LANES), tile), pl.BlockSpec((tp, 4 * LANES), tile),
                  pl.BlockSpec((tp, 2 * LANES), tile),
                  pl.BlockSpec((None, wbuf, 2 * LANES), lambda b, pt: (b, 0, 0)),
                  pl.BlockSpec(wbd.shape, c3), pl.BlockSpec(pe.shape, c3), pl.BlockSpec(kg0.shape, c2),
                  pl.BlockSpec(pool.shape, c2), pl.BlockSpec(expand.shape, c2)],
        out_specs=[pl.BlockSpec((tp, A_WIDTH), tile),
                   pl.BlockSpec((None, wbuf, 2 * LANES), lambda b, pt: (b, 0, 0))],
        scratch_shapes=[pltpu.VMEM((2, past_len, LANES), F32), pltpu.VMEM((past_len, 2 * LANES), F32),
                        pltpu.SemaphoreType.DMA((2,))],
    )
    return pl.pallas_call(
        functools.partial(_nsa_sample_kernel, n_pages=n_pages, past_len=past_len, t_valid=t_valid),
        grid_spec=gs,
        out_shape=[jax.ShapeDtypeStruct((nb * tp, A_WIDTH), F32),
                   jax.ShapeDtypeStruct((nb, wbuf, 2 * LANES), F32)],
        compiler_params=_cparams(("arbitrary",)),
        name="nsa_sample",
    )(page_table.reshape(-1), cache, q, qr, small, rows, winnew, winbuf, wbd, pe, kg0, pool, expand)


MOE_TM = 256
SEG_ALIGN = 8
SEG_BITS = (256, 128, 64, 32, 16, 8)
MOE_RL = -(-(MOE_TM * TOP_K + N_EXPERTS * (SEG_ALIGN - 1)) // LANES) * LANES


def _pack_halves(x):
    w = x.shape[1] // 2
    bits = lax.bitcast_convert_type(x.astype(BF16).astype(F32), jnp.uint32)
    return (bits[:, :w] & jnp.uint32(0xFFFF0000)) | (bits[:, w:] >> 16)


def _unpack_halves(u):
    hi = lax.bitcast_convert_type(u & jnp.uint32(0xFFFF0000), F32).astype(BF16)
    lo = lax.bitcast_convert_type(u << 16, F32).astype(BF16)
    return hi, lo


def _route_and_sort(h2, wrt_ref, brt_ref, xsl_ref, info_ref, cnt_ref, tm, t_mod, t_valid, m_valid):
    ne = N_EXPERTS
    h2b = h2.astype(BF16)
    h2l = (h2 - h2b.astype(F32)).astype(BF16)
    wh, wl = _split(wrt_ref[...])
    lt = _bdot_t(wh, h2b) + _bdot_t(wl, h2b) + _bdot_t(wh, h2l) + brt_ref[...]
    eidx = lax.broadcasted_iota(jnp.int32, (ne, tm), 0)
    rank = jnp.zeros((ne, tm), F32)
    for ep in range(ne):
        v = lt[ep:ep + 1, :]
        rank = rank + jnp.where(v > lt, 1.0, jnp.where((v == lt) & (eidx > ep), 1.0, 0.0))
    sel = rank < TOP_K
    if t_mod is not None:
        tok = pl.program_id(0) * tm + lax.broadcasted_iota(jnp.int32, (1, tm), 1)
        sel = sel & ((tok % t_mod) < t_valid) & (tok < m_valid)
    mx = jnp.max(jnp.where(sel, lt, NEG_BIG), axis=0, keepdims=True)
    ex = jnp.where(sel, jnp.exp(lt - mx), 0.0)
    den = jnp.sum(ex, axis=0, keepdims=True)
    gate = ex / jnp.where(den > 0, den, 1.0)
    self_ = jnp.where(sel, 1.0, 0.0)
    selb = self_.astype(BF16)
    er = lax.broadcasted_iota(jnp.int32, (ne, ne), 0)
    ec = lax.broadcasted_iota(jnp.int32, (ne, ne), 1)
    c = jnp.dot(jnp.where(ec <= er, 1.0, 0.0).astype(BF16), selb, preferred_element_type=F32)
    tr = lax.broadcasted_iota(jnp.int32, (tm, tm), 0)
    tc = lax.broadcasted_iota(jnp.int32, (tm, tm), 1)
    rk = jnp.dot(selb, jnp.where(tr < tc, 1.0, 0.0).astype(BF16), preferred_element_type=F32)
    cnt = jnp.sum(self_, axis=1, keepdims=True)
    cnt_al = jnp.floor((cnt + (SEG_ALIGN - 1)) * (1.0 / SEG_ALIGN)) * SEG_ALIGN
    cnt_b = jnp.broadcast_to(cnt_al, (ne, LANES))
    cnt_ref[...] = cnt_b
    off = jnp.dot(jnp.where(ec < er, 1.0, 0.0).astype(BF16), cnt_b.astype(BF16), preferred_element_type=F32)
    rowidx = off[:, 0:1] + rk
    rows_k, gates_k, exps_k = [], [], []
    for k in range(1, TOP_K + 1):
        mk = sel & (c == k)
        has = jnp.sum(jnp.where(mk, 1.0, 0.0), axis=0, keepdims=True)
        rows_k.append(jnp.sum(jnp.where(mk, rowidx, 0.0), axis=0, keepdims=True) + has - 1.0)
        gates_k.append(jnp.sum(jnp.where(mk, gate, 0.0), axis=0, keepdims=True))
        exps_k.append(jnp.sum(jnp.where(mk, eidx.astype(F32), 0.0), axis=0, keepdims=True))
    info_ref[...] = jnp.concatenate(rows_k + gates_k + exps_k + [jnp.zeros((4, tm), F32)], axis=0)
    ridx = lax.broadcasted_iota(jnp.int32, (MOE_RL, tm), 0).astype(F32)
    perm = jnp.zeros((MOE_RL, tm), F32)
    for k in range(TOP_K):
        perm = perm + jnp.where(ridx == rows_k[k], 1.0, 0.0)
    xs = jnp.dot(perm.astype(BF16), h2b, preferred_element_type=F32)
    xsl_ref[...] = _pack_halves(xs)


def _mixout_kernel(x_ref, hm_ref, on_ref, mod_ref, gmix_ref, gffn_ref,
                   wog_ref, bog_ref, wum_ref, wua_ref, wout_ref, wrt_ref, brt_ref,
                   x1_ref, xsl_ref, info_ref, cnt_ref, *, tm, t_mod, t_valid, m_valid):
    d = D_MODEL
    x = x_ref[...]
    sh1, sc1, gt1 = mod_ref[:, 0:d], mod_ref[:, d:2 * d], mod_ref[:, 2 * d:3 * d]
    sh2, sc2 = mod_ref[:, 3 * d:4 * d], mod_ref[:, 4 * d:5 * d]
    h = _rmsnorm_rows(x, gmix_ref[...]) * (1.0 + sc1) + sh1
    hb = h.astype(BF16)
    mo = jnp.dot(hb, wog_ref[:, 0:M_WIDTH], preferred_element_type=F32) + bog_ref[:, 0:M_WIDTH]
    ym = _bdot(_sigmoid(mo) * hm_ref[...], wum_ref[...])
    ya = _bdot(on_ref[...], wua_ref[...])
    ga = jnp.dot(hb, wog_ref[:, M_WIDTH:M_WIDTH + d], preferred_element_type=F32) + bog_ref[:, M_WIDTH:M_WIDTH + d]
    u = _sigmoid(ga) * ym
    gb = (jnp.dot(hb, wog_ref[:, M_WIDTH + d:M_WIDTH + 2 * d], preferred_element_type=F32)
          + bog_ref[:, M_WIDTH + d:M_WIDTH + 2 * d])
    u = u + _sigmoid(gb) * ya
    x1 = x + gt1 * _bdot(u, wout_ref[...])
    x1_ref[...] = x1
    h2 = _rmsnorm_rows(x1, gffn_ref[...]) * (1.0 + sc2) + sh2
    _route_and_sort(h2, wrt_ref, brt_ref, xsl_ref, info_ref, cnt_ref, tm, t_mod, t_valid, m_valid)


def _mixout(x2, hm, on, mod3, gmix, gffn, wts, tiles_per_mod, t_mod=None, t_valid=None, m_valid=None):
    m = x2.shape[0]
    tm = MOE_TM
    nt = m // tm
    (wog, bog, wum, wua, wout, wr, br) = wts
    r = mod3.shape[1]
    row = lambda i: (i, 0)
    const = lambda i: (0, 0)
    return pl.pallas_call(
        functools.partial(_mixout_kernel, tm=tm, t_mod=t_mod, t_valid=t_valid, m_valid=m_valid),
        grid=(nt,),
        in_specs=[pl.BlockSpec((tm, D_MODEL), row), pl.BlockSpec((tm, M_WIDTH), row),
                  pl.BlockSpec((tm, A_WIDTH), row),
                  pl.BlockSpec((None, r, 6 * D_MODEL), lambda i: (i // tiles_per_mod, 0, 0)),
                  pl.BlockSpec((1, D_MODEL), const), pl.BlockSpec((1, D_MODEL), const),
                  pl.BlockSpec(wog.shape, const), pl.BlockSpec(bog.shape, const),
                  pl.BlockSpec(wum.shape, const), pl.BlockSpec(wua.shape, const),
                  pl.BlockSpec(wout.shape, const), pl.BlockSpec(wr.shape, const),
                  pl.BlockSpec(br.shape, const)],
        out_specs=[pl.BlockSpec((tm, D_MODEL), row), pl.BlockSpec((MOE_RL, D_MODEL // 2), row),
                   pl.BlockSpec((16, tm), lambda i: (0, i)),
                   pl.BlockSpec((None, N_EXPERTS, LANES), lambda i: (i, 0, 0))],
        out_shape=[jax.ShapeDtypeStruct((m, D_MODEL), F32),
                   jax.ShapeDtypeStruct((nt * MOE_RL, D_MODEL // 2), jnp.uint32),
                   jax.ShapeDtypeStruct((16, m), F32),
                   jax.ShapeDtypeStruct((nt, N_EXPERTS, LANES), F32)],
        compiler_params=_cparams(("parallel",)),
        name="mixout",
    )(x2, hm, on, mod3, gmix, gffn, wog, bog, wum, wua, wout, wr, br)


MOE_BM = 256
MOE_CH = 512


def _moe_kernel(be_ref, na_ref, xs_ref, wgu_ref, bgu_ref, wdn_ref, bdn_ref, y_ref, wgu_bf, wdn_bf):
    i = pl.program_id(0)
    e = be_ref[i]
    prev = be_ref[jnp.maximum(i - 1, 0)]

    @pl.when((i == 0) | (e != prev))
    def _():
        for j in range(2 * D_EXPERT // MOE_CH):
            wgu_bf[:, j * MOE_CH:(j + 1) * MOE_CH] = wgu_ref[:, j * MOE_CH:(j + 1) * MOE_CH].astype(BF16)
        for j in range(D_EXPERT // MOE_CH):
            wdn_bf[j * MOE_CH:(j + 1) * MOE_CH, :] = wdn_ref[j * MOE_CH:(j + 1) * MOE_CH, :].astype(BF16)

    @pl.when(i < na_ref[0])
    def _():
        half = D_MODEL // 2
        xh, xl = _unpack_halves(xs_ref[...])

        def xdot(c0, c1):
            return (jnp.dot(xh, wgu_bf[0:half, c0:c1], preferred_element_type=F32)
                    + jnp.dot(xl, wgu_bf[half:D_MODEL, c0:c1], preferred_element_type=F32))

        acc = jnp.zeros((MOE_BM, D_MODEL), F32) + bdn_ref[...]
        for j in range(D_EXPERT // MOE_CH):
            lo, hi = j * MOE_CH, (j + 1) * MOE_CH
            gj = xdot(lo, hi) + bgu_ref[:, lo:hi]
            uj = xdot(D_EXPERT + lo, D_EXPERT + hi) + bgu_ref[:, D_EXPERT + lo:D_EXPERT + hi]
            gj = jnp.minimum(gj, SWIGLU_LIMIT)
            uj = jnp.clip(uj, -SWIGLU_LIMIT, SWIGLU_LIMIT)
            act = gj * _sigmoid(SWIGLU_ALPHA * gj) * (uj + 1.0)
            acc = acc + jnp.dot(act.astype(BF16), wdn_bf[lo:hi, :], preferred_element_type=F32)
        y_ref[...] = _pack_halves(acc)

    @pl.when(i >= na_ref[0])
    def _():
        y_ref[...] = jnp.zeros(y_ref.shape, jnp.uint32)


def _moe_experts(block_e, n_active, xs, w_gu, b_gu, w_dn, b_dn):
    p = xs.shape[0]
    nblk = p // MOE_BM
    gs = pltpu.PrefetchScalarGridSpec(
        num_scalar_prefetch=2,
        grid=(nblk,),
        in_specs=[pl.BlockSpec((MOE_BM, D_MODEL // 2), lambda i, be, na: (i, 0)),
                  pl.BlockSpec((None, D_MODEL, 2 * D_EXPERT), lambda i, be, na: (be[i], 0, 0)),
                  pl.BlockSpec((None, 1, 2 * D_EXPERT), lambda i, be, na: (be[i], 0, 0)),
                  pl.BlockSpec((None, D_EXPERT, D_MODEL), lambda i, be, na: (be[i], 0, 0)),
                  pl.BlockSpec((None, 1, D_MODEL), lambda i, be, na: (be[i], 0, 0))],
        out_specs=pl.BlockSpec((MOE_BM, D_MODEL // 2), lambda i, be, na: (i, 0)),
        scratch_shapes=[pltpu.VMEM((D_MODEL, 2 * D_EXPERT), BF16), pltpu.VMEM((D_EXPERT, D_MODEL), BF16)],
    )
    return pl.pallas_call(
        _moe_kernel,
        grid_spec=gs,
        out_shape=jax.ShapeDtypeStruct((p, D_MODEL // 2), jnp.uint32),
        compiler_params=_cparams(("arbitrary",)),
        name="moe_experts",
    )(block_e, n_active, xs, w_gu, b_gu.reshape(N_EXPERTS, 1, -1), w_dn, b_dn.reshape(N_EXPERTS, 1, -1))


def _segment_copies(i, n_ref, so_ref, do_ref, src, dst, sem, wait):
    def body(j, c):
        idx = i * N_EXPERTS + j
        n = n_ref[idx]
        so = so_ref[idx]
        do = do_ref[idx]
        pos = jnp.int32(0)
        for bit in SEG_BITS:
            @pl.when((n & bit) != 0)
            def _(pos=pos, bit=bit):
                cp = pltpu.make_async_copy(src.at[pl.ds(pl.multiple_of(so + pos, SEG_ALIGN), bit), :],
                                           dst.at[pl.ds(pl.multiple_of(do + pos, SEG_ALIGN), bit), :], sem)
                if wait:
                    cp.wait()
                else:
                    cp.start()
            pos = pos + (n & bit)
        return c
    lax.fori_loop(0, N_EXPERTS, body, 0)


def _segmove_kernel(n_ref, so_ref, do_ref, *refs, n_a, scatter):
    if scatter:
        big, _, _, a_ref, b_ref, sems = refs
    else:
        a_ref, b_ref, _, big, sems = refs
    i = pl.program_id(0)
    nt = pl.num_programs(0)

    def copies(t, wait):
        def run(local):
            if scatter:
                _segment_copies(t, n_ref, do_ref, so_ref, big, local, sems.at[t % 2], wait)
            else:
                _segment_copies(t, n_ref, so_ref, do_ref, local, big, sems.at[t % 2], wait)

        @pl.when(t < n_a)
        def _():
            run(a_ref)

        @pl.when(t >= n_a)
        def _():
            run(b_ref)

    copies(i, False)

    @pl.when(i > 0)
    def _():
        copies(i - 1, True)

    @pl.when(i == nt - 1)
    def _():
        copies(i, True)


def _gather_segments(n, so, do, xsl_a, xsl_b, n_rows):
    nt = n.shape[0] // N_EXPERTS
    n_a = xsl_a.shape[0] // MOE_RL
    anyspec = pl.BlockSpec(memory_space=pl.ANY)
    zeros = jnp.zeros((n_rows, D_MODEL // 2), jnp.uint32)
    gs = pltpu.PrefetchScalarGridSpec(
        num_scalar_prefetch=3, grid=(nt,), in_specs=[anyspec, anyspec, anyspec], out_specs=anyspec,
        scratch_shapes=[pltpu.SemaphoreType.DMA((2,))])
    return pl.pallas_call(
        functools.partial(_segmove_kernel, n_a=n_a, scatter=False),
        grid_spec=gs,
        out_shape=jax.ShapeDtypeStruct(zeros.shape, jnp.uint32),
        input_output_aliases={5: 0},
        compiler_params=_cparams(("arbitrary",)),
        name="moe_gather_segments",
    )(n, so, do, xsl_a, xsl_b, zeros)


def _scatter_segments(n, so, do, ys, rows_a, rows_b):
    nt = n.shape[0] // N_EXPERTS
    n_a = rows_a // MOE_RL
    anyspec = pl.BlockSpec(memory_space=pl.ANY)
    za = jnp.zeros((rows_a, D_MODEL // 2), jnp.uint32)
    zb = jnp.zeros((rows_b, D_MODEL // 2), jnp.uint32)
    gs = pltpu.PrefetchScalarGridSpec(
        num_scalar_prefetch=3, grid=(nt,), in_specs=[anyspec, anyspec, anyspec], out_specs=[anyspec, anyspec],
        scratch_shapes=[pltpu.SemaphoreType.DMA((2,))])
    return pl.pallas_call(
        functools.partial(_segmove_kernel, n_a=n_a, scatter=True),
        grid_spec=gs,
        out_shape=[jax.ShapeDtypeStruct(za.shape, jnp.uint32), jax.ShapeDtypeStruct(zb.shape, jnp.uint32)],
        input_output_aliases={4: 0, 5: 1},
        compiler_params=_cparams(("arbitrary",)),
        name="moe_scatter_segments",
    )(n, so, do, ys, za, zb)


def _combine_kernel(ysl_ref, info_ref, x1_ref, mod_ref, y_ref, *, tm):
    info = info_ref[...]
    info_t = jnp.transpose(jnp.concatenate([info, jnp.zeros((LANES - info.shape[0], tm), F32)], axis=0))
    ridx = lax.broadcasted_iota(jnp.int32, (tm, MOE_RL), 1).astype(F32)
    pg = jnp.zeros((tm, MOE_RL), F32)
    for k in range(TOP_K):
        pg = pg + jnp.where(ridx == info_t[:, k:k + 1], info_t[:, TOP_K + k:TOP_K + k + 1], 0.0)
    pg_hi, pg_lo = _split(pg)
    yh, yl = _unpack_halves(ysl_ref[...])
    half = D_MODEL // 2
    gt2 = mod_ref[:, 5 * D_MODEL:6 * D_MODEL]
    for c, yy in ((0, yh), (1, yl)):
        moe = jnp.dot(pg_hi, yy, preferred_element_type=F32) + jnp.dot(pg_lo, yy, preferred_element_type=F32)
        y_ref[:, c * half:(c + 1) * half] = (x1_ref[:, c * half:(c + 1) * half]
                                             + gt2[:, c * half:(c + 1) * half] * moe)


def _combine(ysl, info, x1, mod3, tiles_per_mod):
    m = x1.shape[0]
    tm = MOE_TM
    r = mod3.shape[1]
    return pl.pallas_call(
        functools.partial(_combine_kernel, tm=tm),
        grid=(m // tm,),
        in_specs=[pl.BlockSpec((MOE_RL, D_MODEL // 2), lambda i: (i, 0)),
                  pl.BlockSpec((16, tm), lambda i: (0, i)),
                  pl.BlockSpec((tm, D_MODEL), lambda i: (i, 0)),
                  pl.BlockSpec((None, r, 6 * D_MODEL), lambda i: (i // tiles_per_mod, 0, 0))],
        out_specs=pl.BlockSpec((tm, D_MODEL), lambda i: (i, 0)),
        out_shape=jax.ShapeDtypeStruct((m, D_MODEL), F32),
        compiler_params=_cparams(("parallel",)),
        name="moe_combine",
    )(ysl, info, x1, mod3)


def _moe_plan(cnt_a, cnt_b):
    n_a = cnt_a.shape[0]
    cnt = jnp.concatenate([cnt_a, cnt_b], axis=0).astype(jnp.int32)
    nt = cnt.shape[0]
    loff = jnp.cumsum(cnt, axis=1) - cnt
    tile_base = jnp.where(jnp.arange(nt) < n_a, jnp.arange(nt), jnp.arange(nt) - n_a) * MOE_RL
    so = loff + tile_base[:, None]
    tot = jnp.sum(cnt, axis=0)
    padded = (tot + MOE_BM - 1) // MOE_BM * MOE_BM
    pad_end = jnp.cumsum(padded)
    gstart = pad_end - padded
    do = gstart[None, :] + jnp.cumsum(cnt, axis=0) - cnt
    max_rows = nt * MOE_TM * TOP_K + nt * N_EXPERTS * (SEG_ALIGN - 1) + N_EXPERTS * (MOE_BM - 1)
    n_blocks = -(-max_rows // MOE_BM)
    block_e = jnp.minimum(jnp.sum(pad_end[None, :] <= (jnp.arange(n_blocks) * MOE_BM)[:, None], axis=1),
                          N_EXPERTS - 1).astype(jnp.int32)
    n_active = (pad_end[-1] // MOE_BM).astype(jnp.int32).reshape(1)
    flat = lambda a: a.reshape(-1).astype(jnp.int32)
    return flat(cnt), flat(so), flat(do), block_e, n_active, n_blocks * MOE_BM


def _rope_tables(pos):
    half = ROT_DIM // 2
    inv = ROPE_THETA ** (-jnp.arange(half, dtype=F32) * (2.0 / ROT_DIM))
    ang = pos.astype(F32)[:, None] * inv[None, :]
    cos, sin = jnp.cos(ang), jnp.sin(ang)
    n = pos.shape[0]
    ones = jnp.ones((n, A_DH - ROT_DIM), F32)
    zeros_h = jnp.zeros((n, half), F32)
    zeros_r = jnp.zeros((n, A_DH - ROT_DIM), F32)
    cos64 = jnp.concatenate([cos, cos, ones], axis=1)
    sprev64 = jnp.concatenate([zeros_h, sin, zeros_r], axis=1)
    snext64 = jnp.concatenate([-sin, zeros_h, zeros_r], axis=1)
    two = lambda a: jnp.concatenate([a, a], axis=1)
    return two(cos64), two(sprev64), two(snext64)


def _prep_weights(w_in, b_in, q_norm_g, k_norm_g, cmp_pe_k, cmp_pe_v, cmp_w_k, cmp_w_v,
                  w_up_m, w_up_a, w_out, w_router, b_router):
    b2 = b_in.reshape(1, N_IN)
    wm = w_in[:, OFF_MQ:OFF_MO].astype(BF16)
    bm = b2[:, OFF_MQ:OFF_MO]
    wq = w_in[:, OFF_AQ:OFF_AKV].astype(BF16)
    bq = b2[:, OFF_AQ:OFF_AKV]
    wkv = w_in[:, OFF_AKV:OFF_AG].astype(BF16)
    bkv = b2[:, OFF_AKV:OFF_AG]
    n_small = 2 * M_HEADS + 3 * A_HEADS
    ws = jnp.concatenate([w_in[:, OFF_MI:OFF_AQ], w_in[:, OFF_AG:OFF_GA],
                          jnp.zeros((D_MODEL, LANES - n_small), F32)], axis=1)
    bs = jnp.concatenate([b2[:, OFF_MI:OFF_AQ], b2[:, OFF_AG:OFF_GA], jnp.zeros((1, LANES - n_small), F32)], axis=1)
    qg = jnp.tile(q_norm_g, A_HEADS).reshape(1, A_WIDTH)
    kg = jnp.stack([jnp.tile(k_norm_g[1], A_KV), jnp.tile(k_norm_g[2], A_KV)], axis=0)
    kg0 = jnp.tile(k_norm_g[0], A_KV).reshape(1, LANES)
    hid = jnp.arange(A_WIDTH) // A_DH
    bd = jnp.where(hid[:, None] == hid[None, :], 1.0 / A_DH, 0.0).astype(BF16)
    inproj_w = (wm, bm, wq, bq, wkv, bkv, ws, bs, qg, kg, bd)

    z = jnp.zeros((CMP_LEN, A_DH, A_DH), F32)
    r0 = jnp.concatenate([cmp_w_k, z, z, z], axis=2)
    r1 = jnp.concatenate([z, cmp_w_k, z, z], axis=2)
    r2 = jnp.concatenate([z, z, cmp_w_v, z], axis=2)
    r3 = jnp.concatenate([z, z, z, cmp_w_v], axis=2)
    wbd = jnp.concatenate([r0, r1, r2, r3], axis=1).astype(BF16)
    pe = jnp.concatenate([cmp_pe_k, cmp_pe_k, cmp_pe_v, cmp_pe_v], axis=1).reshape(CMP_LEN, 1, 2 * LANES)

    wog = jnp.concatenate([w_in[:, OFF_MO:OFF_MI], w_in[:, OFF_GA:N_IN]], axis=1).astype(BF16)
    bog = jnp.concatenate([b2[:, OFF_MO:OFF_MI], b2[:, OFF_GA:N_IN]], axis=1)
    mixout_w = (wog, bog, w_up_m.astype(BF16), w_up_a.astype(BF16), w_out.astype(BF16),
                w_router.T, b_router.reshape(N_EXPERTS, 1))
    return inproj_w, (wbd, pe, kg0), mixout_w


def _pick_tile(m, pref):
    t = pref
    while m % t:
        t //= 2
    return t


def kernel(x_prompt, x_sample, cache_nsa_kv, state_win_kv, state_mlstm_C, state_mlstm_n, state_mlstm_m, page_table, c_prompt, c_sample, w_ada, b_ada, g_mix, g_ffn, w_in, b_in, q_norm_g, k_norm_g, cmp_pe_k, cmp_pe_v, cmp_w_k, cmp_w_v, w_up_m, w_up_a, w_out, w_router, b_router, w_gu, b_gu, w_dn, b_dn):
    depth = w_in.shape[0]
    assert depth == 1
    B, T, D = x_prompt.shape
    DB, TS, _ = x_sample.shape
    n_pages = page_table.shape[1]
    past_len = n_pages * PAGE_SIZE
    wbuf = state_win_kv.shape[2]
    tp = SAMPLE_PAD_T
    assert TS <= tp and wbuf % tp == 0 and T % 128 == 0

    l = 0
    inproj_w, cmp_w, mixout_w = _prep_weights(
        w_in[l], b_in[l], q_norm_g[l], k_norm_g[l], cmp_pe_k[l], cmp_pe_v[l], cmp_w_k[l], cmp_w_v[l],
        w_up_m[l], w_up_a[l], w_out[l], w_router[l], b_router[l])
    wbd, pe, kg0 = cmp_w
    gmix = g_mix[l].reshape(1, D)
    gffn = g_ffn[l].reshape(1, D)

    nc = B + DB
    nc_pad = -(-nc // SUBLANES) * SUBLANES
    c_all = jnp.concatenate([c_prompt, c_sample, jnp.zeros((nc_pad - nc, D), F32)], axis=0)
    mod = _adaln(c_all, w_ada[l], b_ada[l])
    mod_p = mod[:B].reshape(B, 1, 6 * D)
    mod_s = jnp.repeat(mod[B:B + DB], tp, axis=0).reshape(1, DB * tp, 6 * D)

    mp = B * T
    tm = _pick_tile(T, 256)
    xp = x_prompt.reshape(mp, D)
    tabs_p = _rope_tables(jnp.arange(T, dtype=jnp.int32))
    mq, mk, mv, q, qr, rows, win, small = _inproj(xp, mod_p, gmix, tabs_p, inproj_w, tm, T // tm, T // tm)
    Lp = _pick_tile(T, 128)
    hm, C_p, n_p, m_p = _mlstm(mq, mk, mv, small, B, T, T, Lp)
    o_nsa = _nsa_prompt(q, qr, small, rows, win, wbd, pe, kg0, B, T)
    assert T % MOE_TM == 0
    x1_p, xsl_p, info_p, cnt_p = _mixout(xp, hm, o_nsa, mod_p, gmix, gffn, mixout_w, T // MOE_TM)

    ms = DB * tp
    xs_pad = jnp.concatenate([x_sample, jnp.zeros((DB, tp - TS, D), F32)], axis=1).reshape(ms, D)
    pos_s = past_len + jnp.tile(jnp.arange(tp, dtype=jnp.int32), DB)
    tabs_s = _rope_tables(pos_s)
    mq_s, mk_s, mv_s, q_s, qr_s, rows_s, win_s, small_s = _inproj(xs_pad, mod_s, gmix, tabs_s, inproj_w, ms, 1, 1)
    hm_s, C_s, n_s, m_s = _mlstm(mq_s, mk_s, mv_s, small_s, DB, tp, TS, tp,
                                 state=(state_mlstm_C[l], state_mlstm_n[l], state_mlstm_m[l]))
    cache2 = cache_nsa_kv[l].reshape(cache_nsa_kv.shape[1], PAGE_SIZE, 4 * LANES)
    winbuf = state_win_kv[l].reshape(DB, wbuf, 2 * LANES)
    o_nsa_s, win_out_s = _nsa_sample(page_table, cache2, q_s, qr_s, small_s, rows_s, win_s, winbuf,
                                     wbd, pe, kg0, TS)
    ms_pad = -(-ms // MOE_TM) * MOE_TM
    rpad = lambda a: jnp.concatenate([a, jnp.zeros((ms_pad - ms, a.shape[1]), a.dtype)], axis=0) if ms_pad > ms else a
    mod_sp = rpad(mod_s[0])[None]
    x1_s, xsl_s, info_s, cnt_s = _mixout(rpad(xs_pad), rpad(hm_s), rpad(o_nsa_s), mod_sp, gmix, gffn, mixout_w,
                                         ms_pad // MOE_TM, t_mod=tp, t_valid=TS, m_valid=ms)

    seg_n, seg_so, seg_do, block_e, n_active, n_rows = _moe_plan(cnt_p[:, :, 0], cnt_s[:, :, 0])
    xs_sorted = _gather_segments(seg_n, seg_so, seg_do, xsl_p, xsl_s, n_rows)
    ys_sorted = _moe_experts(block_e, n_active, xs_sorted, w_gu[l], b_gu[l], w_dn[l], b_dn[l])
    ysl_p, ysl_s = _scatter_segments(seg_n, seg_so, seg_do, ys_sorted, xsl_p.shape[0], xsl_s.shape[0])
    y_p = _combine(ysl_p, info_p, x1_p, mod_p, T // MOE_TM).reshape(B, T, D)
    y_s_all = _combine(ysl_s, info_s, x1_s, mod_sp, ms_pad // MOE_TM)
    valid = lambda a: a.reshape(DB, tp, -1)[:, :TS].reshape(DB * TS, -1)
    y_s = valid(y_s_all[:ms]).reshape(DB, TS, D)

    kv_p = rows.reshape(1, B, T, 4, A_KV, A_DH)
    kv_s = valid(rows_s).reshape(1, DB, TS, 4, A_KV, A_DH)
    wp = min(WINDOW, T)
    win_p = win.reshape(B, T, 2, A_KV, A_DH)[:, T - wp:][None]
    win_s_out = win_out_s.reshape(1, DB, wbuf, 2, A_KV, A_DH)
    return (y_p, y_s, kv_p, kv_s, win_p, win_s_out,
            C_p[None], n_p[None], m_p[None], C_s[None], n_s[None], m_s[None])
```

```python
import functools
import math

import jax
import jax.numpy as jnp
from jax import lax
from jax.experimental import pallas as pl
from jax.experimental.pallas import tpu as pltpu

F32 = jnp.float32
BF16 = jnp.bfloat16

D_MODEL = 1024
M_HEADS = 4
M_DH = 128
M_WIDTH = M_HEADS * M_DH
A_HEADS = 8
A_KV = 2
A_HPG = A_HEADS // A_KV
A_DH = 64
A_WIDTH = A_HEADS * A_DH
CMP_STRIDE = 16
CMP_LEN = 32
SEL_LEN = 64
N_SEL = 16
WINDOW = 512
PAGE_SIZE = 128
ROPE_THETA = 500000.0
ROT_DIM = A_DH // 4
ATT_SCALE = A_DH ** -0.5
N_EXPERTS = 32
TOP_K = 4
D_EXPERT = D_MODEL
SWIGLU_LIMIT = 7.0
SWIGLU_ALPHA = 1.702
EPS = 1e-6

OFF_MQ, OFF_MK, OFF_MV, OFF_MO = 0, M_WIDTH, 2 * M_WIDTH, 3 * M_WIDTH
OFF_MI = 4 * M_WIDTH
OFF_MF = OFF_MI + M_HEADS
OFF_AQ = OFF_MF + M_HEADS
OFF_AKV = OFF_AQ + A_WIDTH
OFF_AG = OFF_AKV + 6 * A_KV * A_DH
OFF_GA = OFF_AG + 3 * A_HEADS
OFF_GB = OFF_GA + D_MODEL
N_IN = OFF_GB + D_MODEL

LANES = 128
SUBLANES = 8
VMEM_LIMIT = 56 * 1024 * 1024

NEG_BIG = -1e30
M_INIT = -1e29
LOG2E = 1.4426950408889634
SAMPLE_PAD_T = 8


def _cparams(sem):
    return pltpu.CompilerParams(dimension_semantics=sem, vmem_limit_bytes=VMEM_LIMIT)


def _bdot(a, b):
    return jnp.dot(a.astype(BF16), b.astype(BF16), preferred_element_type=F32)


def _bdot_t(a, b):
    return lax.dot_general(a.astype(BF16), b.astype(BF16), (((1,), (1,)), ((), ())),
                           preferred_element_type=F32)


def _split(a):
    hi = a.astype(BF16)
    lo = (a - hi.astype(F32)).astype(BF16)
    return hi, lo


def _dot3(a, b):
    ah, al = _split(a)
    bh, bl = _split(b)
    return (jnp.dot(ah, bh, preferred_element_type=F32) + jnp.dot(al, bh, preferred_element_type=F32)
            + jnp.dot(ah, bl, preferred_element_type=F32))


def _dot2_exact_rhs(a, b_bf16):
    ah, al = _split(a)
    return jnp.dot(ah, b_bf16, preferred_element_type=F32) + jnp.dot(al, b_bf16, preferred_element_type=F32)


def _sigmoid(x):
    return 1.0 / (1.0 + jnp.exp(-x))


def _rmsnorm_rows(x, g):
    return x * lax.rsqrt(jnp.mean(x * x, axis=-1, keepdims=True) + EPS) * g


def _adaln_kernel(c_ref, w_ref, b_ref, o_ref):
    c = c_ref[...]
    s = c * _sigmoid(c)
    o_ref[...] = _dot3(s, w_ref[...]) + b_ref[...]


def _adaln(c, w, b):
    mc, d = c.shape
    n = w.shape[1]
    tn = 1024
    return pl.pallas_call(
        _adaln_kernel,
        grid=(n // tn,),
        in_specs=[pl.BlockSpec((mc, d), lambda j: (0, 0)),
                  pl.BlockSpec((d, tn), lambda j: (0, j)),
                  pl.BlockSpec((1, tn), lambda j: (0, j))],
        out_specs=pl.BlockSpec((mc, tn), lambda j: (0, j)),
        out_shape=jax.ShapeDtypeStruct((mc, n), F32),
        compiler_params=_cparams(("parallel",)),
        name="adaln",
    )(c, w, b.reshape(1, n))


def _head_norm(z, bd, gain):
    ms = _dot2_exact_rhs(z * z, bd)
    return z * lax.rsqrt(ms + EPS) * gain


def _rope(z, cos, s_prev, s_next):
    w = z.shape[1]
    rep = w // LANES
    if rep > 1:
        cos = jnp.concatenate([cos] * rep, axis=1)
        s_prev = jnp.concatenate([s_prev] * rep, axis=1)
        s_next = jnp.concatenate([s_next] * rep, axis=1)
    z_prev = pltpu.roll(z, ROT_DIM // 2, 1)
    z_next = pltpu.roll(z, w - ROT_DIM // 2, 1)
    return z * cos + z_prev * s_prev + z_next * s_next


def _inproj_kernel(x_ref, mod_ref, gmix_ref, cos_ref, sp_ref, sn_ref,
                   wm_ref, bm_ref, wq_ref, bq_ref, wkv_ref, bkv_ref, ws_ref, bs_ref,
                   qg_ref, kg_ref, bd_ref,
                   mq_ref, mk_ref, mv_ref, q_ref, qr_ref, rows_ref, win_ref, small_ref):
    x = x_ref[...]
    sh1 = mod_ref[:, 0:D_MODEL]
    sc1 = mod_ref[:, D_MODEL:2 * D_MODEL]
    h = _rmsnorm_rows(x, gmix_ref[...]) * (1.0 + sc1) + sh1
    hb = h.astype(BF16)

    mq_ref[...] = jnp.dot(hb, wm_ref[:, 0:M_WIDTH], preferred_element_type=F32) + bm_ref[:, 0:M_WIDTH]
    mk = jnp.dot(hb, wm_ref[:, M_WIDTH:2 * M_WIDTH], preferred_element_type=F32) + bm_ref[:, M_WIDTH:2 * M_WIDTH]
    mk_ref[...] = mk * (M_DH ** -0.5)
    mv_ref[...] = (jnp.dot(hb, wm_ref[:, 2 * M_WIDTH:3 * M_WIDTH], preferred_element_type=F32)
                   + bm_ref[:, 2 * M_WIDTH:3 * M_WIDTH])

    cos, sp, sn = cos_ref[...], sp_ref[...], sn_ref[...]
    zq = jnp.dot(hb, wq_ref[...], preferred_element_type=F32) + bq_ref[...]
    qn = _head_norm(zq, bd_ref[...], qg_ref[...])
    q_ref[...] = qn
    qr_ref[...] = _rope(qn, cos, sp, sn)

    zkv = jnp.dot(hb, wkv_ref[...], preferred_element_type=F32) + bkv_ref[...]
    bd2 = bd_ref[0:LANES, 0:LANES]
    rows_ref[:, 0:2 * LANES] = zkv[:, 0:2 * LANES]
    ksel = _head_norm(zkv[:, 2 * LANES:3 * LANES], bd2, kg_ref[0:1, :])
    rows_ref[:, 2 * LANES:3 * LANES] = _rope(ksel, cos, sp, sn)
    rows_ref[:, 3 * LANES:4 * LANES] = zkv[:, 3 * LANES:4 * LANES]
    kwin = _head_norm(zkv[:, 4 * LANES:5 * LANES], bd2, kg_ref[1:2, :])
    win_ref[:, 0:LANES] = _rope(kwin, cos, sp, sn)
    win_ref[:, LANES:2 * LANES] = zkv[:, 5 * LANES:6 * LANES]

    small_ref[...] = _dot3(h, ws_ref[...]) + bs_ref[...]


def _inproj(x2, mod3, gmix, tabs, wts, tm, tiles_per_mod, pos_tiles):
    m = x2.shape[0]
    cos_t, sp_t, sn_t = tabs
    (wm, bm, wq, bq, wkv, bkv, ws, bs, qg, kg, bd) = wts
    r = mod3.shape[1]
    row = lambda i: (i, 0)
    const = lambda i: (0, 0)
    tab = lambda i: (i % pos_tiles, 0)
    in_specs = [
        pl.BlockSpec((tm, D_MODEL), row),
        pl.BlockSpec((None, r, 6 * D_MODEL), lambda i: (i // tiles_per_mod, 0, 0)),
        pl.BlockSpec((1, D_MODEL), const),
        pl.BlockSpec((tm, LANES), tab), pl.BlockSpec((tm, LANES), tab), pl.BlockSpec((tm, LANES), tab),
        pl.BlockSpec(wm.shape, const), pl.BlockSpec(bm.shape, const),
        pl.BlockSpec(wq.shape, const), pl.BlockSpec(bq.shape, const),
        pl.BlockSpec(wkv.shape, const), pl.BlockSpec(bkv.shape, const),
        pl.BlockSpec(ws.shape, const), pl.BlockSpec(bs.shape, const),
        pl.BlockSpec(qg.shape, const), pl.BlockSpec(kg.shape, const), pl.BlockSpec(bd.shape, const),
    ]
    widths = (M_WIDTH, M_WIDTH, M_WIDTH, A_WIDTH, A_WIDTH, 4 * LANES, 2 * LANES, LANES)
    return pl.pallas_call(
        _inproj_kernel,
        grid=(m // tm,),
        in_specs=in_specs,
        out_specs=[pl.BlockSpec((tm, w), row) for w in widths],
        out_shape=[jax.ShapeDtypeStruct((m, w), F32) for w in widths],
        compiler_params=_cparams(("parallel",)),
        name="inproj",
    )(x2, mod3, gmix, cos_t, sp_t, sn_t, wm, bm, wq, bq, wkv, bkv, ws, bs, qg, kg, bd)


def _log_sigmoid(x):
    return jnp.minimum(x, 0.0) - jnp.log(1.0 + jnp.exp(-jnp.abs(x)))


def _mlstm_kernel(*refs, L, t_valid, has_state):
    if has_state:
        q_ref, k_ref, v_ref, s_ref, c0_ref, n0_ref, m0_ref, h_ref, c_ref, n_ref, m_ref = refs
    else:
        q_ref, k_ref, v_ref, s_ref, h_ref, c_ref, n_ref, m_ref = refs
    c = pl.program_id(1)

    @pl.when(c == 0)
    def _():
        if has_state:
            c_ref[...] = c0_ref[...]
            n_ref[...] = n0_ref[...]
            m_ref[...] = m0_ref[...]
        else:
            c_ref[...] = jnp.zeros(c_ref.shape, F32)
            n_ref[...] = jnp.zeros(n_ref.shape, F32)
            m_ref[...] = jnp.zeros(m_ref.shape, F32)

    row = lax.broadcasted_iota(jnp.int32, (L, L), 0)
    col = lax.broadcasted_iota(jnp.int32, (L, L), 1)
    causal = col <= row
    eye = col == row
    tok_col = c * L + lax.broadcasted_iota(jnp.int32, (L, 1), 0)
    valid_col = tok_col < t_valid
    for hd in range(M_HEADS):
        lo, hi = hd * M_DH, (hd + 1) * M_DH
        q = q_ref[:, lo:hi]
        k = k_ref[:, lo:hi]
        v = v_ref[:, lo:hi]
        i_col = s_ref[:, hd:hd + 1]
        lf_col = _log_sigmoid(s_ref[:, M_HEADS + hd:M_HEADS + hd + 1])
        lf_col = jnp.where(valid_col, lf_col, 0.0)
        i_col = jnp.where(valid_col, i_col, -jnp.inf)
        i_row = jnp.sum(jnp.where(eye, i_col, 0.0), axis=0, keepdims=True)
        lf_row = jnp.sum(jnp.where(eye, lf_col, 0.0), axis=0, keepdims=True)
        b_col = jnp.sum(jnp.where(causal, lf_row, 0.0), axis=1, keepdims=True)
        b_row = jnp.sum(jnp.where(row <= col, lf_col, 0.0), axis=0, keepdims=True)
        m_prev = m_ref[:, hd:hd + 1]
        dmat = jnp.where(causal, b_col - b_row + i_row, -jnp.inf)
        inter = b_col + m_prev
        m_row = jnp.maximum(jnp.max(dmat, axis=1, keepdims=True), inter)
        w = jnp.exp(dmat - m_row)
        w_inter = jnp.exp(inter - m_row)
        s = _bdot_t(q, k) * w
        cm = c_ref[hd]
        nv = n_ref[hd]
        num = _bdot(s, v) + w_inter * _bdot_t(q, cm)
        den = jnp.sum(s, axis=1, keepdims=True) + w_inter * jnp.sum(q * nv, axis=1, keepdims=True)
        h_ref[:, lo:hi] = num / jnp.maximum(jnp.abs(den), jnp.exp(-m_row))
        b_last = b_col[L - 1:L, :]
        dec_col = b_last - b_col + i_col
        dec_row = b_last - b_row + i_row
        m_new = jnp.maximum(b_last + m_prev, jnp.max(dec_row, axis=1, keepdims=True))
        ws_col = jnp.exp(dec_col - m_new)
        wc = jnp.exp(b_last + m_prev - m_new)
        vw = (v * ws_col).astype(BF16)
        upd = lax.dot_general(vw, k.astype(BF16), (((0,), (0,)), ((), ())), preferred_element_type=F32)
        c_ref[hd] = wc * cm + upd
        n_ref[hd] = wc * nv + jnp.sum(k * ws_col, axis=0, keepdims=True)
        m_ref[:, hd:hd + 1] = m_new


def _mlstm(mq, mk, mv, small, nb, t_pad, t_valid, L, state=None):
    nc = t_pad // L
    has_state = state is not None
    blk = lambda b, c: (b * nc + c, 0)
    st4 = lambda b, c: (b, 0, 0, 0)
    st3 = lambda b, c: (b, 0, 0)
    in_specs = [pl.BlockSpec((L, M_WIDTH), blk)] * 3 + [pl.BlockSpec((L, LANES), blk)]
    args = [mq, mk, mv, small]
    if has_state:
        c0, n0, m0 = state
        in_specs += [pl.BlockSpec((None, M_HEADS, M_DH, M_DH), st4),
                     pl.BlockSpec((None, M_HEADS, 1, M_DH), st4),
                     pl.BlockSpec((None, 1, M_HEADS), st3)]
        args += [c0, n0.reshape(nb, M_HEADS, 1, M_DH), m0.reshape(nb, 1, M_HEADS)]
    out_specs = [pl.BlockSpec((L, M_WIDTH), blk),
                 pl.BlockSpec((None, M_HEADS, M_DH, M_DH), st4),
                 pl.BlockSpec((None, M_HEADS, 1, M_DH), st4),
                 pl.BlockSpec((None, 1, M_HEADS), st3)]
    out_shape = [jax.ShapeDtypeStruct((nb * t_pad, M_WIDTH), F32),
                 jax.ShapeDtypeStruct((nb, M_HEADS, M_DH, M_DH), F32),
                 jax.ShapeDtypeStruct((nb, M_HEADS, 1, M_DH), F32),
                 jax.ShapeDtypeStruct((nb, 1, M_HEADS), F32)]
    h, cs, ns, ms = pl.pallas_call(
        functools.partial(_mlstm_kernel, L=L, t_valid=t_valid, has_state=has_state),
        grid=(nb, nc),
        in_specs=in_specs,
        out_specs=out_specs,
        out_shape=out_shape,
        compiler_params=_cparams(("parallel", "arbitrary")),
        name="mlstm",
    )(*args)
    return h, cs, ns.reshape(nb, M_HEADS, M_DH), ms.reshape(nb, M_HEADS)


def _stack_heads(qt, g):
    t = qt.shape[0]
    z = jnp.zeros((t, A_DH), F32)
    parts = []
    for hh in range(A_HPG):
        hd = g * A_HPG + hh
        qh = qt[:, hd * A_DH:(hd + 1) * A_DH] * (ATT_SCALE * LOG2E)
        parts.append(jnp.concatenate([qh, z], axis=1) if g == 0 else jnp.concatenate([z, qh], axis=1))
    return jnp.concatenate(parts, axis=0).astype(BF16)


def _gate_cols(small, g, br):
    cols = []
    for hh in range(A_HPG):
        c0 = 2 * M_HEADS + (g * A_HPG + hh) * 3 + br
        cols.append(_sigmoid(small[:, c0:c0 + 1]))
    return jnp.concatenate(cols, axis=0)


def _compress(k_ref, v_ref, nseg, wbd_ref, pe_ref, kg0):
    acc_lo = jnp.zeros((nseg, 2 * LANES), F32)
    acc_hi = jnp.zeros((nseg, 2 * LANES), F32)
    for l in range(CMP_STRIDE):
        xl = jnp.concatenate([k_ref[pl.ds(l, nseg, stride=CMP_STRIDE), :],
                              v_ref[pl.ds(l, nseg, stride=CMP_STRIDE), :]], axis=1)
        acc_lo = acc_lo + _bdot(xl + pe_ref[l], wbd_ref[l])
        acc_hi = acc_hi + _bdot(xl + pe_ref[CMP_STRIDE + l], wbd_ref[CMP_STRIDE + l])
    kv = acc_lo + pltpu.roll(acc_hi, nseg - 1, 0)
    kc = kv[:, 0:LANES]
    vc = kv[:, LANES:2 * LANES]
    lane = lax.broadcasted_iota(jnp.int32, (nseg, LANES), 1)
    sq = kc * kc
    ms0 = jnp.sum(jnp.where(lane < A_DH, sq, 0.0), axis=1, keepdims=True) * (1.0 / A_DH)
    ms1 = jnp.sum(jnp.where(lane >= A_DH, sq, 0.0), axis=1, keepdims=True) * (1.0 / A_DH)
    ms = jnp.where(lane < A_DH, ms0, ms1)
    kc = kc * lax.rsqrt(ms + EPS) * kg0
    return kc, vc


def _cmp_branch(qn_g, kc_b, vc_b, tpos_rows, nseg, n_tok):
    s = _bdot_t(qn_g, kc_b)
    nidx = lax.broadcasted_iota(jnp.int32, (1, nseg), 1)
    vis = (nidx * CMP_STRIDE + (CMP_LEN - 1)) <= tpos_rows
    sm = jnp.where(vis, s, NEG_BIG)
    mx = jnp.max(sm, axis=1, keepdims=True)
    e = jnp.where(vis, jnp.exp2(sm - mx), 0.0)
    d = jnp.sum(e, axis=1, keepdims=True)
    p = e / jnp.where(d > 0, d, 1.0)
    o = _bdot(p, vc_b)
    imp = p[0:n_tok]
    for hh in range(1, A_HPG):
        imp = imp + p[hh * n_tok:(hh + 1) * n_tok]
    return o, imp


def _masked_attn_direct(q_g, k_parts, v_parts, allowed_parts):
    ss = [jnp.where(al, _bdot_t(q_g, kk), NEG_BIG) for kk, al in zip(k_parts, allowed_parts)]
    mx = ss[0].max(axis=1, keepdims=True)
    for s in ss[1:]:
        mx = jnp.maximum(mx, s.max(axis=1, keepdims=True))
    num = None
    den = None
    for s, al, vv in zip(ss, allowed_parts, v_parts):
        e = jnp.where(al, jnp.exp2(s - mx), 0.0)
        dd = jnp.sum(e, axis=1, keepdims=True)
        oo = _bdot(e, vv)
        num = oo if num is None else num + oo
        den = dd if den is None else den + dd
    return num / jnp.where(den > 0, den, 1.0)


def _assemble_heads(o_groups, n_tok):
    pieces = []
    for g in range(A_KV):
        for hh in range(A_HPG):
            pieces.append(o_groups[g][hh * n_tok:(hh + 1) * n_tok, g * A_DH:(g + 1) * A_DH])
    return jnp.concatenate(pieces, axis=1)


def _lane_tile(a, width):
    rep = width // LANES
    return a if rep == 1 else jnp.concatenate([a] * rep, axis=1)


def _add_bias(s, bias):
    t, k = bias.shape
    return (s.reshape(A_HPG, t, k) + bias[None]).reshape(A_HPG * t, k)


def _nsa_prompt_kernel(q_ref, qr_ref, small_ref, rows_ref, win_ref, wbd_ref, pe_ref, kg0_ref,
                       pool_ref, o_ref,
                       kraw_sc, vraw_sc, kc_sc, vc_sc, m_sc, l_sc, acc_sc, *, T, tq, kc_len):
    qi = pl.program_id(1)
    nseg = T // CMP_STRIDE
    nsb = T // SEL_LEN

    @pl.when(qi == 0)
    def _():
        kraw_sc[...] = rows_ref[:, 0:LANES]
        vraw_sc[...] = rows_ref[:, LANES:2 * LANES]
        kc, vc = _compress(kraw_sc, vraw_sc, nseg, wbd_ref, pe_ref, kg0_ref[...])
        kc_sc[...] = kc
        vc_sc[...] = vc

    t0 = qi * tq
    tpos_col = t0 + lax.broadcasted_iota(jnp.int32, (tq, 1), 0)
    tpos_rows = jnp.concatenate([tpos_col] * A_HPG, axis=0)
    tpos_lane = t0 + lax.broadcasted_iota(jnp.int32, (1, tq), 1)
    q = q_ref[...]
    qr = qr_ref[...]
    small = small_ref[...]
    kc_b = kc_sc[...].astype(BF16)
    vc_b = vc_sc[...].astype(BF16)
    bidx = lax.broadcasted_iota(jnp.int32, (nsb, tq), 0)
    cur = tpos_lane // SEL_LEN
    r4 = A_HPG * tq
    qr_gs = [_stack_heads(qr, g) for g in range(A_KV)]
    o_cmps = []
    sel_bs = []
    for g in range(A_KV):
        qn_g = _stack_heads(q, g)
        o_cmp, imp = _cmp_branch(qn_g, kc_b, vc_b, tpos_rows, nseg, tq)
        o_cmps.append(o_cmp)
        imp_sel = _dot2_exact_rhs(imp, pool_ref[...])
        imp_t = jnp.transpose(imp_sel)[0:nsb, :]
        val = jnp.where(bidx < cur, imp_t, -1.0)
        rank = jnp.zeros((nsb, tq), F32)
        for bp in range(nsb):
            vb = val[bp:bp + 1, :]
            ahead = jnp.where(vb > val, 1.0, jnp.where((vb == val) & (bidx > bp), 1.0, 0.0))
            rank = rank + ahead
        sel_t = jnp.where(((rank < (N_SEL - 1)) & (bidx < cur)) | (bidx == cur), 1.0, 0.0)
        if nsb < LANES:
            sel_t = jnp.concatenate([sel_t, jnp.zeros((LANES - nsb, tq), F32)], axis=0)
        sel_bs.append(jnp.transpose(sel_t).astype(BF16))

    m_sc[...] = jnp.full(m_sc.shape, M_INIT, F32)
    l_sc[...] = jnp.zeros(l_sc.shape, F32)
    acc_sc[...] = jnp.zeros(acc_sc.shape, F32)

    def sel_body(c, carry):
        k0 = pl.multiple_of(c * kc_len, kc_len)
        kb = rows_ref[pl.ds(k0, kc_len), 2 * LANES:3 * LANES].astype(BF16)
        vb = rows_ref[pl.ds(k0, kc_len), 3 * LANES:4 * LANES].astype(BF16)
        kpos = k0 + lax.broadcasted_iota(jnp.int32, (1, kc_len), 1)
        causal = kpos <= tpos_col
        kblk = (k0 + lax.broadcasted_iota(jnp.int32, (LANES, kc_len), 1)) // SEL_LEN
        expand = jnp.where(kblk == lax.broadcasted_iota(jnp.int32, (LANES, kc_len), 0), 1.0, 0.0).astype(BF16)
        for g in range(A_KV):
            mk = jnp.dot(sel_bs[g], expand, preferred_element_type=F32)
            bias = jnp.where(causal, (mk - 1.0) * (-NEG_BIG), NEG_BIG)
            sm = _add_bias(_bdot_t(qr_gs[g], kb), bias)
            m_prev = m_sc[g]
            m_new = jnp.maximum(m_prev, jnp.max(sm, axis=1, keepdims=True))
            alpha = jnp.exp2(m_prev - m_new)
            p = jnp.exp2(sm - _lane_tile(m_new, kc_len))
            l_sc[g] = alpha * l_sc[g] + jnp.sum(p, axis=1, keepdims=True)
            acc_sc[g] = alpha * acc_sc[g] + _bdot(p, vb)
            m_sc[g] = m_new
        return carry

    lax.fori_loop(0, (t0 + tq + kc_len - 1) // kc_len, sel_body, 0)

    wk = min(WINDOW + tq, T)
    w0 = pl.multiple_of(jnp.clip(t0 + tq - wk, 0, T - wk), tq)
    kw = win_ref[pl.ds(w0, wk), 0:LANES].astype(BF16)
    vw = win_ref[pl.ds(w0, wk), LANES:2 * LANES].astype(BF16)
    wdiff = tpos_col - (w0 + lax.broadcasted_iota(jnp.int32, (1, wk), 1))
    wbias = jnp.where((wdiff >= 0) & (wdiff < WINDOW), 0.0, NEG_BIG)

    o_groups = []
    for g in range(A_KV):
        l = l_sc[g]
        o_sel = acc_sc[g] / jnp.where(l > 0, l, 1.0)
        sw = _add_bias(_bdot_t(qr_gs[g], kw), wbias)
        mw = jnp.broadcast_to(jnp.max(sw, axis=1, keepdims=True), (r4, LANES))
        pw = jnp.exp2(sw - _lane_tile(mw, wk))
        o_win = _bdot(pw, vw) / jnp.broadcast_to(jnp.sum(pw, axis=1, keepdims=True), (r4, LANES))
        o_groups.append(_gate_cols(small, g, 0) * o_cmps[g] + _gate_cols(small, g, 1) * o_sel
                        + _gate_cols(small, g, 2) * o_win)
    o_ref[...] = _assemble_heads(o_groups, tq)


def _nsa_prompt(q, qr, small, rows, win, wbd, pe, kg0, nb, T):
    tq = 128
    kc_len = _pick_tile(T, 512)
    nq = T // tq
    nseg = T // CMP_STRIDE
    nsb = T // SEL_LEN
    pool = (jnp.arange(nseg)[:, None] // (SEL_LEN // CMP_STRIDE) == jnp.arange(LANES)[None, :]).astype(BF16)
    tile = lambda b, i: (b * nq + i, 0)
    per_b = lambda b, i: (b, 0)
    c2 = lambda b, i: (0, 0)
    c3 = lambda b, i: (0, 0, 0)
    r4 = A_HPG * tq
    return pl.pallas_call(
        functools.partial(_nsa_prompt_kernel, T=T, tq=tq, kc_len=kc_len),
        grid=(nb, nq),
        in_specs=[pl.BlockSpec((tq, A_WIDTH), tile), pl.BlockSpec((tq, A_WIDTH), tile),
                  pl.BlockSpec((tq, LANES), tile),
                  pl.BlockSpec((T, 4 * LANES), per_b), pl.BlockSpec((T, 2 * LANES), per_b),
                  pl.BlockSpec(wbd.shape, c3), pl.BlockSpec(pe.shape, c3), pl.BlockSpec(kg0.shape, c2),
                  pl.BlockSpec(pool.shape, c2)],
        out_specs=pl.BlockSpec((tq, A_WIDTH), tile),
        out_shape=jax.ShapeDtypeStruct((nb * T, A_WIDTH), F32),
        scratch_shapes=[pltpu.VMEM((T, LANES), F32), pltpu.VMEM((T, LANES), F32),
                        pltpu.VMEM((nseg, LANES), F32), pltpu.VMEM((nseg, LANES), F32),
                        pltpu.VMEM((A_KV, r4, LANES), F32), pltpu.VMEM((A_KV, r4, LANES), F32),
                        pltpu.VMEM((A_KV, r4, LANES), F32)],
        compiler_params=_cparams(("parallel", "arbitrary")),
        name="nsa_prompt",
    )(q, qr, small, rows, win, wbd, pe, kg0, pool)


def _nsa_sample_kernel(pt_ref, cache_ref, q_ref, qr_ref, small_ref, rows_ref, winnew_ref, winbuf_ref,
                       wbd_ref, pe_ref, kg0_ref, pool_ref, expand_ref,
                       o_ref, winout_ref,
                       cmp_buf, sel_buf, sems, *, n_pages, past_len, t_valid):
    b = pl.program_id(0)
    nb = pl.num_programs(0)
    tp = SAMPLE_PAD_T
    nseg = past_len // CMP_STRIDE
    nsb = past_len // SEL_LEN
    wbuf = winbuf_ref.shape[0]

    def page_copies(bb, p, phase):
        page = pt_ref[bb * n_pages + p]
        dst_rows = pl.ds(p * PAGE_SIZE, PAGE_SIZE)
        if phase == 0:
            return [pltpu.make_async_copy(cache_ref.at[page, :, pl.ds(j * LANES, LANES)],
                                          cmp_buf.at[j, dst_rows, :], sems.at[0]) for j in range(2)]
        return [pltpu.make_async_copy(cache_ref.at[page, :, pl.ds(2 * LANES, 2 * LANES)],
                                      sel_buf.at[dst_rows, :], sems.at[1])]

    def start_all(bb, phase):
        def body(p, c):
            for cp in page_copies(bb, p, phase):
                cp.start()
            return c
        lax.fori_loop(0, n_pages, body, 0)

    def wait_all(bb, phase):
        def body(p, c):
            for cp in page_copies(bb, p, phase):
                cp.wait()
            return c
        lax.fori_loop(0, n_pages, body, 0)

    @pl.when(b == 0)
    def _():
        start_all(b, 0)

    start_all(b, 1)
    wait_all(b, 0)

    kc, vc = _compress(cmp_buf.at[0], cmp_buf.at[1], nseg, wbd_ref, pe_ref, kg0_ref[...])
    kc_b = kc.astype(BF16)
    vc_b = vc.astype(BF16)
    q = q_ref[...]
    qr = qr_ref[...]
    small = small_ref[...]
    tpos_col = past_len + lax.broadcasted_iota(jnp.int32, (tp, 1), 0)
    tpos_rows = jnp.concatenate([tpos_col] * A_HPG, axis=0)
    bp_idx = lax.broadcasted_iota(jnp.int32, (nsb, nsb), 0)
    b_idx = lax.broadcasted_iota(jnp.int32, (nsb, nsb), 1)
    o_cmps = []
    sels = []
    for g in range(A_KV):
        qn_g = _stack_heads(q, g)
        o_cmp, imp = _cmp_branch(qn_g, kc_b, vc_b, tpos_rows, nseg, tp)
        o_cmps.append(o_cmp)
        imp_sel = _dot2_exact_rhs(imp, pool_ref[...])
        imp_pad = jnp.concatenate([imp_sel, jnp.zeros((nsb - tp, nsb), F32)], axis=0)
        imp_t = jnp.transpose(imp_pad)
        rows_sel = []
        for t in range(tp):
            if t < t_valid:
                row_t = imp_sel[t:t + 1, :]
                col_t = imp_t[:, t:t + 1]
                ahead = jnp.where(col_t > row_t, 1.0, jnp.where((col_t == row_t) & (bp_idx < b_idx), 1.0, 0.0))
                rank = jnp.sum(ahead, axis=0, keepdims=True)
                rows_sel.append(jnp.where(rank < (N_SEL - 1), 1.0, 0.0))
            else:
                rows_sel.append(jnp.zeros((1, nsb), F32))
        sels.append(jnp.concatenate(rows_sel, axis=0).astype(BF16))

    @pl.when(b + 1 < nb)
    def _():
        start_all(b + 1, 0)

    wait_all(b, 1)

    new_idx = lax.broadcasted_iota(jnp.int32, (tp, tp), 1)
    tok_idx = lax.broadcasted_iota(jnp.int32, (tp, tp), 0)
    new_ok = jnp.concatenate([jnp.where(new_idx <= tok_idx, 1.0, 0.0)] * A_HPG, axis=0) > 0.5
    wpos = past_len - wbuf + lax.broadcasted_iota(jnp.int32, (1, wbuf), 1)
    wdiff = tpos_col - wpos
    win_ok = jnp.concatenate([jnp.where((wdiff >= 0) & (wdiff < WINDOW), 1.0, 0.0)] * A_HPG, axis=0) > 0.5
    k_past = sel_buf[:, 0:LANES].astype(BF16)
    v_past = sel_buf[:, LANES:2 * LANES].astype(BF16)
    k_new = rows_ref[:, 2 * LANES:3 * LANES]
    v_new = rows_ref[:, 3 * LANES:4 * LANES]
    kw_past = winbuf_ref[:, 0:LANES]
    vw_past = winbuf_ref[:, LANES:2 * LANES]
    kw_new = winnew_ref[:, 0:LANES]
    vw_new = winnew_ref[:, LANES:2 * LANES]
    o_groups = []
    for g in range(A_KV):
        qr_g = _stack_heads(qr, g)
        mk = jnp.dot(sels[g], expand_ref[...], preferred_element_type=F32)
        past_ok = jnp.concatenate([mk] * A_HPG, axis=0) > 0.5
        o_sel = _masked_attn_direct(qr_g, [k_past, k_new], [v_past, v_new], [past_ok, new_ok])
        o_win = _masked_attn_direct(qr_g, [kw_past, kw_new], [vw_past, vw_new], [win_ok, new_ok])
        o_groups.append(_gate_cols(small, g, 0) * o_cmps[g] + _gate_cols(small, g, 1) * o_sel
                        + _gate_cols(small, g, 2) * o_win)
    o_ref[...] = _assemble_heads(o_groups, tp)

    wb = winbuf_ref[...]
    rolled = pltpu.roll(wb, wbuf - t_valid, 0)
    newr = pltpu.roll(winnew_ref[...], tp - t_valid, 0)
    sub = lax.broadcasted_iota(jnp.int32, (tp, 2 * LANES), 0)
    winout_ref[0:wbuf - tp, :] = rolled[0:wbuf - tp, :]
    winout_ref[wbuf - tp:wbuf, :] = jnp.where(sub < tp - t_valid, rolled[wbuf - tp:wbuf, :], newr)


def _nsa_sample(page_table, cache, q, qr, small, rows, winnew, winbuf, wbd, pe, kg0, t_valid):
    nb, n_pages = page_table.shape
    past_len = n_pages * PAGE_SIZE
    nseg = past_len // CMP_STRIDE
    nsb = past_len // SEL_LEN
    tp = SAMPLE_PAD_T
    wbuf = winbuf.shape[1]
    pool = (jnp.arange(nseg)[:, None] // (SEL_LEN // CMP_STRIDE) == jnp.arange(nsb)[None, :]).astype(BF16)
    expand = (jnp.arange(nsb)[:, None] == jnp.arange(past_len)[None, :] // SEL_LEN).astype(BF16)
    tile = lambda b, pt: (b, 0)
    c2 = lambda b, pt: (0, 0)
    c3 = lambda b, pt: (0, 0, 0)
    gs = pltpu.PrefetchScalarGridSpec(
        num_scalar_prefetch=1,
        grid=(nb,),
        in_specs=[pl.BlockSpec(memory_space=pl.ANY),
                  pl.BlockSpec((tp, A_WIDTH), tile), pl.BlockSpec((tp, A_WIDTH), tile),
                  pl.BlockSpec((tp, LANES), tile), pl.BlockSpec((tp, 4 * LANES), tile),
                  pl.BlockSpec((tp, 2 * LANES), tile),
                  pl.BlockSpec((None, wbuf, 2 * LANES), lambda b, pt: (b, 0, 0)),
                  pl.BlockSpec(wbd.shape, c3), pl.BlockSpec(pe.shape, c3), pl.BlockSpec(kg0.shape, c2),
                  pl.BlockSpec(pool.shape, c2), pl.BlockSpec(expand.shape, c2)],
        out_specs=[pl.BlockSpec((tp, A_WIDTH), tile),
                   pl.BlockSpec((None, wbuf, 2 * LANES), lambda b, pt: (b, 0, 0))],
        scratch_shapes=[pltpu.VMEM((2, past_len, LANES), F32), pltpu.VMEM((past_len, 2 * LANES), F32),
                        pltpu.SemaphoreType.DMA((2,))],
    )
    return pl.pallas_call(
        functools.partial(_nsa_sample_kernel, n_pages=n_pages, past_len=past_len, t_valid=t_valid),
        grid_spec=gs,
        out_shape=[jax.ShapeDtypeStruct((nb * tp, A_WIDTH), F32),
                   jax.ShapeDtypeStruct((nb, wbuf, 2 * LANES), F32)],
        compiler_params=_cparams(("arbitrary",)),
        name="nsa_sample",
    )(page_table.reshape(-1), cache, q, qr, small, rows, winnew, winbuf, wbd, pe, kg0, pool, expand)


MOE_TM = 256
SEG_ALIGN = 8
SEG_BITS = (256, 128, 64, 32, 16, 8)
MOE_RL = -(-(MOE_TM * TOP_K + N_EXPERTS * (SEG_ALIGN - 1)) // LANES) * LANES


def _pack_halves(x):
    w = x.shape[1] // 2
    bits = lax.bitcast_convert_type(x.astype(BF16).astype(F32), jnp.uint32)
    return (bits[:, :w] & jnp.uint32(0xFFFF0000)) | (bits[:, w:] >> 16)


def _unpack_halves(u):
    hi = lax.bitcast_convert_type(u & jnp.uint32(0xFFFF0000), F32).astype(BF16)
    lo = lax.bitcast_convert_type(u << 16, F32).astype(BF16)
    return hi, lo


def _route_and_sort(h2, wrt_ref, brt_ref, xsl_ref, info_ref, cnt_ref, tm, t_mod, t_valid, m_valid):
    ne = N_EXPERTS
    h2b = h2.astype(BF16)
    h2l = (h2 - h2b.astype(F32)).astype(BF16)
    wh, wl = _split(wrt_ref[...])
    lt = _bdot_t(wh, h2b) + _bdot_t(wl, h2b) + _bdot_t(wh, h2l) + brt_ref[...]
    eidx = lax.broadcasted_iota(jnp.int32, (ne, tm), 0)
    rank = jnp.zeros((ne, tm), F32)
    for ep in range(ne):
        v = lt[ep:ep + 1, :]
        rank = rank + jnp.where(v > lt, 1.0, jnp.where((v == lt) & (eidx > ep), 1.0, 0.0))
    sel = rank < TOP_K
    if t_mod is not None:
        tok = pl.program_id(0) * tm + lax.broadcasted_iota(jnp.int32, (1, tm), 1)
        sel = sel & ((tok % t_mod) < t_valid) & (tok < m_valid)
    mx = jnp.max(jnp.where(sel, lt, NEG_BIG), axis=0, keepdims=True)
    ex = jnp.where(sel, jnp.exp(lt - mx), 0.0)
    den = jnp.sum(ex, axis=0, keepdims=True)
    gate = ex / jnp.where(den > 0, den, 1.0)
    self_ = jnp.where(sel, 1.0, 0.0)
    selb = self_.astype(BF16)
    er = lax.broadcasted_iota(jnp.int32, (ne, ne), 0)
    ec = lax.broadcasted_iota(jnp.int32, (ne, ne), 1)
    c = jnp.dot(jnp.where(ec <= er, 1.0, 0.0).astype(BF16), selb, preferred_element_type=F32)
    tr = lax.broadcasted_iota(jnp.int32, (tm, tm), 0)
    tc = lax.broadcasted_iota(jnp.int32, (tm, tm), 1)
    rk = jnp.dot(selb, jnp.where(tr < tc, 1.0, 0.0).astype(BF16), preferred_element_type=F32)
    cnt = jnp.sum(self_, axis=1, keepdims=True)
    cnt_al = jnp.floor((cnt + (SEG_ALIGN - 1)) * (1.0 / SEG_ALIGN)) * SEG_ALIGN
    cnt_b = jnp.broadcast_to(cnt_al, (ne, LANES))
    cnt_ref[...] = cnt_b
    off = jnp.dot(jnp.where(ec < er, 1.0, 0.0).astype(BF16), cnt_b.astype(BF16), preferred_element_type=F32)
    rowidx = off[:, 0:1] + rk
    rows_k, gates_k, exps_k = [], [], []
    for k in range(1, TOP_K + 1):
        mk = sel & (c == k)
        has = jnp.sum(jnp.where(mk, 1.0, 0.0), axis=0, keepdims=True)
        rows_k.append(jnp.sum(jnp.where(mk, rowidx, 0.0), axis=0, keepdims=True) + has - 1.0)
        gates_k.append(jnp.sum(jnp.where(mk, gate, 0.0), axis=0, keepdims=True))
        exps_k.append(jnp.sum(jnp.where(mk, eidx.astype(F32), 0.0), axis=0, keepdims=True))
    info_ref[...] = jnp.concatenate(rows_k + gates_k + exps_k + [jnp.zeros((4, tm), F32)], axis=0)
    ridx = lax.broadcasted_iota(jnp.int32, (MOE_RL, tm), 0).astype(F32)
    perm = jnp.zeros((MOE_RL, tm), F32)
    for k in range(TOP_K):
        perm = perm + jnp.where(ridx == rows_k[k], 1.0, 0.0)
    xs = jnp.dot(perm.astype(BF16), h2b, preferred_element_type=F32)
    xsl_ref[...] = _pack_halves(xs)


def _mixout_kernel(x_ref, hm_ref, on_ref, mod_ref, gmix_ref, gffn_ref,
                   wog_ref, bog_ref, wum_ref, wua_ref, wout_ref, wrt_ref, brt_ref,
                   x1_ref, xsl_ref, info_ref, cnt_ref, *, tm, t_mod, t_valid, m_valid, n_real):
    if n_real is not None:
        @pl.when(pl.program_id(0) >= n_real)
        def _():
            xsl_ref[...] = jnp.zeros(xsl_ref.shape, jnp.uint32)
            info_ref[...] = jnp.zeros(info_ref.shape, F32)
            cnt_ref[...] = jnp.zeros(cnt_ref.shape, F32)

        @pl.when(pl.program_id(0) < n_real)
        def _():
            _mixout_body(x_ref, hm_ref, on_ref, mod_ref, gmix_ref, gffn_ref, wog_ref, bog_ref, wum_ref,
                         wua_ref, wout_ref, wrt_ref, brt_ref, x1_ref, xsl_ref, info_ref, cnt_ref,
                         tm, t_mod, t_valid, m_valid)
    else:
        _mixout_body(x_ref, hm_ref, on_ref, mod_ref, gmix_ref, gffn_ref, wog_ref, bog_ref, wum_ref,
                     wua_ref, wout_ref, wrt_ref, brt_ref, x1_ref, xsl_ref, info_ref, cnt_ref,
                     tm, t_mod, t_valid, m_valid)


def _mixout_body(x_ref, hm_ref, on_ref, mod_ref, gmix_ref, gffn_ref,
                 wog_ref, bog_ref, wum_ref, wua_ref, wout_ref, wrt_ref, brt_ref,
                 x1_ref, xsl_ref, info_ref, cnt_ref, tm, t_mod, t_valid, m_valid):
    d = D_MODEL
    x = x_ref[...]
    sh1, sc1, gt1 = mod_ref[:, 0:d], mod_ref[:, d:2 * d], mod_ref[:, 2 * d:3 * d]
    sh2, sc2 = mod_ref[:, 3 * d:4 * d], mod_ref[:, 4 * d:5 * d]
    h = _rmsnorm_rows(x, gmix_ref[...]) * (1.0 + sc1) + sh1
    hb = h.astype(BF16)
    mo = jnp.dot(hb, wog_ref[:, 0:M_WIDTH], preferred_element_type=F32) + bog_ref[:, 0:M_WIDTH]
    ym = _bdot(_sigmoid(mo) * hm_ref[...], wum_ref[...])
    ya = _bdot(on_ref[...], wua_ref[...])
    ga = jnp.dot(hb, wog_ref[:, M_WIDTH:M_WIDTH + d], preferred_element_type=F32) + bog_ref[:, M_WIDTH:M_WIDTH + d]
    u = _sigmoid(ga) * ym
    gb = (jnp.dot(hb, wog_ref[:, M_WIDTH + d:M_WIDTH + 2 * d], preferred_element_type=F32)
          + bog_ref[:, M_WIDTH + d:M_WIDTH + 2 * d])
    u = u + _sigmoid(gb) * ya
    x1 = x + gt1 * _bdot(u, wout_ref[...])
    x1_ref[...] = x1
    h2 = _rmsnorm_rows(x1, gffn_ref[...]) * (1.0 + sc2) + sh2
    _route_and_sort(h2, wrt_ref, brt_ref, xsl_ref, info_ref, cnt_ref, tm, t_mod, t_valid, m_valid)


def _mixout_with_shared(*refs, n_shared, **kw):
    n_in = 13
    _mixout_kernel(*refs[:n_in], *refs[n_in + n_shared:], **kw)


def _mixout(x2, hm, on, mod3, gmix, gffn, wts, tiles_per_mod, nt_total, tile0=0, shared=None,
            t_mod=None, t_valid=None, m_valid=None):
    m = x2.shape[0]
    tm = MOE_TM
    nt = m // tm
    (wog, bog, wum, wua, wout, wr, br) = wts
    r = mod3.shape[1]
    n_extra = nt_total - tile0 - nt if shared is None else 0
    row = lambda i: (jnp.minimum(i, nt - 1), 0)
    const = lambda i: (0, 0)
    in_specs = [pl.BlockSpec((tm, D_MODEL), row), pl.BlockSpec((tm, M_WIDTH), row),
                pl.BlockSpec((tm, A_WIDTH), row),
                pl.BlockSpec((None, r, 6 * D_MODEL), lambda i: (jnp.minimum(i, nt - 1) // tiles_per_mod, 0, 0)),
                pl.BlockSpec((1, D_MODEL), const), pl.BlockSpec((1, D_MODEL), const),
                pl.BlockSpec(wog.shape, const), pl.BlockSpec(bog.shape, const),
                pl.BlockSpec(wum.shape, const), pl.BlockSpec(wua.shape, const),
                pl.BlockSpec(wout.shape, const), pl.BlockSpec(wr.shape, const),
                pl.BlockSpec(br.shape, const)]
    args = [x2, hm, on, mod3, gmix, gffn, wog, bog, wum, wua, wout, wr, br]
    kw = dict(tm=tm, t_mod=t_mod, t_valid=t_valid, m_valid=m_valid, n_real=nt if n_extra else None)
    body = functools.partial(_mixout_kernel, **kw)
    aliases = {}
    if shared is not None:
        in_specs += [pl.BlockSpec(memory_space=pl.ANY)] * len(shared)
        aliases = {len(args) + j: 1 + j for j in range(len(shared))}
        args += list(shared)
        body = functools.partial(_mixout_with_shared, n_shared=len(shared), **kw)
    return pl.pallas_call(
        body,
        grid=(nt + n_extra,),
        in_specs=in_specs,
        out_specs=[pl.BlockSpec((tm, D_MODEL), row),
                   pl.BlockSpec((MOE_RL, D_MODEL // 2), lambda i: (tile0 + i, 0)),
                   pl.BlockSpec((16, tm), lambda i: (0, tile0 + i)),
                   pl.BlockSpec((None, N_EXPERTS, LANES), lambda i: (tile0 + i, 0, 0))],
        out_shape=[jax.ShapeDtypeStruct((m, D_MODEL), F32),
                   jax.ShapeDtypeStruct((nt_total * MOE_RL, D_MODEL // 2), jnp.uint32),
                   jax.ShapeDtypeStruct((16, nt_total * tm), F32),
                   jax.ShapeDtypeStruct((nt_total, N_EXPERTS, LANES), F32)],
        input_output_aliases=aliases,
        compiler_params=_cparams(("arbitrary" if n_extra else "parallel",)),
        name="mixout",
    )(*args)


MOE_BM = 256
MOE_CH = 512


def _moe_kernel(be_ref, r0_ref, rows_ref, tf_ref, tl_ref, na_ref, n_ref, cs_ref, so_ref,
                xsl_ref, wgu_ref, bgu_ref, wdn_ref, bdn_ref, ysl_in_ref, ysl_ref,
                wgu_bf, wdn_bf, xbuf, ybuf, sem_in, sem_out):
    del ysl_in_ref
    i = pl.program_id(0)
    na = na_ref[0]
    e = be_ref[i]
    prev = be_ref[jnp.maximum(i - 1, 0)]

    def for_pieces(blk, fn):
        eb = be_ref[blk]
        r0 = r0_ref[blk]

        def body(t, c):
            idx = t * N_EXPERTS + eb
            cs = cs_ref[idx]
            lo = jnp.maximum(cs, r0)
            ln = jnp.maximum(jnp.minimum(cs + n_ref[idx], r0 + MOE_BM) - lo, 0)
            src = so_ref[idx] + (lo - cs)
            dst = lo - r0
            pos = jnp.int32(0)
            for bit in SEG_BITS:
                @pl.when((ln & bit) != 0)
                def _(pos=pos, bit=bit):
                    fn(pl.multiple_of(src + pos, SEG_ALIGN), pl.multiple_of(dst + pos, SEG_ALIGN), bit)
                pos = pos + (ln & bit)
            return c
        lax.fori_loop(tf_ref[blk], tl_ref[blk], body, 0)

    def start_gather(blk):
        slot = blk % 2
        for_pieces(blk, lambda s, d, nb: pltpu.make_async_copy(
            xsl_ref.at[pl.ds(s, nb), :], xbuf.at[slot, pl.ds(d, nb), :], sem_in.at[slot]).start())

    def start_scatter(blk):
        slot = blk % 2
        for_pieces(blk, lambda s, d, nb: pltpu.make_async_copy(
            ybuf.at[slot, pl.ds(d, nb), :], ysl_ref.at[pl.ds(s, nb), :], sem_out.at[slot]).start())

    def wait_rows(blk, sem, inbound):
        slot = blk % 2
        rows = rows_ref[blk]
        for bit in SEG_BITS:
            @pl.when((rows & bit) != 0)
            def _(bit=bit):
                if inbound:
                    pltpu.make_async_copy(xsl_ref.at[pl.ds(0, bit), :], xbuf.at[slot, pl.ds(0, bit), :],
                                          sem.at[slot]).wait()
                else:
                    pltpu.make_async_copy(ybuf.at[slot, pl.ds(0, bit), :], ysl_ref.at[pl.ds(0, bit), :],
                                          sem.at[slot]).wait()

    @pl.when(i == 0)
    def _():
        xbuf[...] = jnp.zeros(xbuf.shape, jnp.uint32)
        start_gather(i)

    @pl.when(i + 1 < na)
    def _():
        start_gather(i + 1)

    @pl.when((i < na) & ((i == 0) | (e != prev)))
    def _():
        for j in range(2 * D_EXPERT // MOE_CH):
            wgu_bf[:, j * MOE_CH:(j + 1) * MOE_CH] = wgu_ref[:, j * MOE_CH:(j + 1) * MOE_CH].astype(BF16)
        for j in range(D_EXPERT // MOE_CH):
            wdn_bf[j * MOE_CH:(j + 1) * MOE_CH, :] = wdn_ref[j * MOE_CH:(j + 1) * MOE_CH, :].astype(BF16)

    @pl.when(i < na)
    def _():
        slot = i % 2
        wait_rows(i, sem_in, True)

        @pl.when(i >= 2)
        def _():
            wait_rows(i - 2, sem_out, False)

        half = D_MODEL // 2
        xh, xl = _unpack_halves(xbuf[slot])

        def xdot(c0, c1):
            return (jnp.dot(xh, wgu_bf[0:half, c0:c1], preferred_element_type=F32)
                    + jnp.dot(xl, wgu_bf[half:D_MODEL, c0:c1], preferred_element_type=F32))

        acc = jnp.zeros((MOE_BM, D_MODEL), F32) + bdn_ref[...]
        for j in range(D_EXPERT // MOE_CH):
            lo, hi = j * MOE_CH, (j + 1) * MOE_CH
            gj = xdot(lo, hi) + bgu_ref[:, lo:hi]
            uj = xdot(D_EXPERT + lo, D_EXPERT + hi) + bgu_ref[:, D_EXPERT + lo:D_EXPERT + hi]
            gj = jnp.minimum(gj, SWIGLU_LIMIT)
            uj = jnp.clip(uj, -SWIGLU_LIMIT, SWIGLU_LIMIT)
            act = gj * _sigmoid(SWIGLU_ALPHA * gj) * (uj + 1.0)
            acc = acc + jnp.dot(act.astype(BF16), wdn_bf[lo:hi, :], preferred_element_type=F32)
        ybuf[slot] = _pack_halves(acc)
        start_scatter(i)

        @pl.when(i == na - 1)
        def _():
            @pl.when(i >= 1)
            def _():
                wait_rows(i - 1, sem_out, False)
            wait_rows(i, sem_out, False)


def _moe_experts(plan, xsl, w_gu, b_gu, w_dn, b_dn):
    block_e, block_r0, block_rows, tf, tl, n_active, seg_n, seg_cs, seg_so = plan
    nblk = block_e.shape[0]
    wmap = lambda i, be, *_: (be[i], 0, 0)
    anyspec = pl.BlockSpec(memory_space=pl.ANY)
    gs = pltpu.PrefetchScalarGridSpec(
        num_scalar_prefetch=9,
        grid=(nblk,),
        in_specs=[anyspec,
                  pl.BlockSpec((None, D_MODEL, 2 * D_EXPERT), wmap),
                  pl.BlockSpec((None, 1, 2 * D_EXPERT), wmap),
                  pl.BlockSpec((None, D_EXPERT, D_MODEL), wmap),
                  pl.BlockSpec((None, 1, D_MODEL), wmap),
                  anyspec],
        out_specs=anyspec,
        scratch_shapes=[pltpu.VMEM((D_MODEL, 2 * D_EXPERT), BF16), pltpu.VMEM((D_EXPERT, D_MODEL), BF16),
                        pltpu.VMEM((2, MOE_BM, D_MODEL // 2), jnp.uint32),
                        pltpu.VMEM((2, MOE_BM, D_MODEL // 2), jnp.uint32),
                        pltpu.SemaphoreType.DMA((2,)), pltpu.SemaphoreType.DMA((2,))],
    )
    return pl.pallas_call(
        _moe_kernel,
        grid_spec=gs,
        out_shape=jax.ShapeDtypeStruct(xsl.shape, jnp.uint32),
        input_output_aliases={14: 0},
        compiler_params=_cparams(("arbitrary",)),
        name="moe_experts",
    )(*plan, xsl, w_gu, b_gu.reshape(N_EXPERTS, 1, -1), w_dn, b_dn.reshape(N_EXPERTS, 1, -1),
      jnp.zeros(xsl.shape, jnp.uint32))


def _combine_kernel(ysl_ref, info_ref, x1_ref, mod_ref, y_ref, *, tm):
    info = info_ref[...]
    info_t = jnp.transpose(jnp.concatenate([info, jnp.zeros((LANES - info.shape[0], tm), F32)], axis=0))
    ridx = lax.broadcasted_iota(jnp.int32, (tm, MOE_RL), 1).astype(F32)
    pg = jnp.zeros((tm, MOE_RL), F32)
    for k in range(TOP_K):
        pg = pg + jnp.where(ridx == info_t[:, k:k + 1], info_t[:, TOP_K + k:TOP_K + k + 1], 0.0)
    pg_hi, pg_lo = _split(pg)
    yh, yl = _unpack_halves(ysl_ref[...])
    half = D_MODEL // 2
    gt2 = mod_ref[:, 5 * D_MODEL:6 * D_MODEL]
    for c, yy in ((0, yh), (1, yl)):
        moe = jnp.dot(pg_hi, yy, preferred_element_type=F32) + jnp.dot(pg_lo, yy, preferred_element_type=F32)
        y_ref[:, c * half:(c + 1) * half] = (x1_ref[:, c * half:(c + 1) * half]
                                             + gt2[:, c * half:(c + 1) * half] * moe)


def _combine(ysl, info, x1, mod3, tiles_per_mod, tile0=0):
    m = x1.shape[0]
    tm = MOE_TM
    r = mod3.shape[1]
    return pl.pallas_call(
        functools.partial(_combine_kernel, tm=tm),
        grid=(m // tm,),
        in_specs=[pl.BlockSpec((MOE_RL, D_MODEL // 2), lambda i: (tile0 + i, 0)),
                  pl.BlockSpec((16, tm), lambda i: (0, tile0 + i)),
                  pl.BlockSpec((tm, D_MODEL), lambda i: (i, 0)),
                  pl.BlockSpec((None, r, 6 * D_MODEL), lambda i: (i // tiles_per_mod, 0, 0))],
        out_specs=pl.BlockSpec((tm, D_MODEL), lambda i: (i, 0)),
        out_shape=jax.ShapeDtypeStruct((m, D_MODEL), F32),
        compiler_params=_cparams(("parallel",)),
        name="moe_combine",
    )(ysl, info, x1, mod3)


def _moe_plan(cnt):
    cnt = cnt.astype(jnp.int32)
    nt = cnt.shape[0]
    so = jnp.cumsum(cnt, axis=1) - cnt + (jnp.arange(nt) * MOE_RL)[:, None]
    ce = jnp.cumsum(cnt, axis=0)
    cs = ce - cnt
    tot = ce[-1]
    nblk_e = (tot + MOE_BM - 1) // MOE_BM
    blk_end = jnp.cumsum(nblk_e)
    max_rows = nt * MOE_TM * TOP_K + nt * N_EXPERTS * (SEG_ALIGN - 1)
    n_blocks = -(-max_rows // MOE_BM) + N_EXPERTS
    bidx = jnp.arange(n_blocks)
    block_e = jnp.minimum(jnp.sum(blk_end[None, :] <= bidx[:, None], axis=1), N_EXPERTS - 1).astype(jnp.int32)
    block_r0 = (bidx - (blk_end - nblk_e)[block_e]) * MOE_BM
    block_rows = jnp.clip(tot[block_e] - block_r0, 0, MOE_BM)
    ce_b = ce[:, block_e]
    cs_b = cs[:, block_e]
    tf = jnp.sum(ce_b <= block_r0[None, :], axis=0)
    tl = jnp.sum(cs_b < (block_r0 + MOE_BM)[None, :], axis=0)
    n_active = blk_end[-1].reshape(1)
    i32 = lambda a: a.reshape(-1).astype(jnp.int32)
    return (block_e, i32(block_r0), i32(block_rows), i32(tf), i32(tl), i32(n_active),
            i32(cnt), i32(cs), i32(so))


def _rope_tables(pos):
    half = ROT_DIM // 2
    inv = ROPE_THETA ** (-jnp.arange(half, dtype=F32) * (2.0 / ROT_DIM))
    ang = pos.astype(F32)[:, None] * inv[None, :]
    cos, sin = jnp.cos(ang), jnp.sin(ang)
    n = pos.shape[0]
    ones = jnp.ones((n, A_DH - ROT_DIM), F32)
    zeros_h = jnp.zeros((n, half), F32)
    zeros_r = jnp.zeros((n, A_DH - ROT_DIM), F32)
    cos64 = jnp.concatenate([cos, cos, ones], axis=1)
    sprev64 = jnp.concatenate([zeros_h, sin, zeros_r], axis=1)
    snext64 = jnp.concatenate([-sin, zeros_h, zeros_r], axis=1)
    two = lambda a: jnp.concatenate([a, a], axis=1)
    return two(cos64), two(sprev64), two(snext64)


def _prep_weights(w_in, b_in, q_norm_g, k_norm_g, cmp_pe_k, cmp_pe_v, cmp_w_k, cmp_w_v,
                  w_up_m, w_up_a, w_out, w_router, b_router):
    b2 = b_in.reshape(1, N_IN)
    wm = w_in[:, OFF_MQ:OFF_MO].astype(BF16)
    bm = b2[:, OFF_MQ:OFF_MO]
    wq = w_in[:, OFF_AQ:OFF_AKV].astype(BF16)
    bq = b2[:, OFF_AQ:OFF_AKV]
    wkv = w_in[:, OFF_AKV:OFF_AG].astype(BF16)
    bkv = b2[:, OFF_AKV:OFF_AG]
    n_small = 2 * M_HEADS + 3 * A_HEADS
    ws = jnp.concatenate([w_in[:, OFF_MI:OFF_AQ], w_in[:, OFF_AG:OFF_GA],
                          jnp.zeros((D_MODEL, LANES - n_small), F32)], axis=1)
    bs = jnp.concatenate([b2[:, OFF_MI:OFF_AQ], b2[:, OFF_AG:OFF_GA], jnp.zeros((1, LANES - n_small), F32)], axis=1)
    qg = jnp.tile(q_norm_g, A_HEADS).reshape(1, A_WIDTH)
    kg = jnp.stack([jnp.tile(k_norm_g[1], A_KV), jnp.tile(k_norm_g[2], A_KV)], axis=0)
    kg0 = jnp.tile(k_norm_g[0], A_KV).reshape(1, LANES)
    hid = jnp.arange(A_WIDTH) // A_DH
    bd = jnp.where(hid[:, None] == hid[None, :], 1.0 / A_DH, 0.0).astype(BF16)
    inproj_w = (wm, bm, wq, bq, wkv, bkv, ws, bs, qg, kg, bd)

    z = jnp.zeros((CMP_LEN, A_DH, A_DH), F32)
    r0 = jnp.concatenate([cmp_w_k, z, z, z], axis=2)
    r1 = jnp.concatenate([z, cmp_w_k, z, z], axis=2)
    r2 = jnp.concatenate([z, z, cmp_w_v, z], axis=2)
    r3 = jnp.concatenate([z, z, z, cmp_w_v], axis=2)
    wbd = jnp.concatenate([r0, r1, r2, r3], axis=1).astype(BF16)
    pe = jnp.concatenate([cmp_pe_k, cmp_pe_k, cmp_pe_v, cmp_pe_v], axis=1).reshape(CMP_LEN, 1, 2 * LANES)

    wog = jnp.concatenate([w_in[:, OFF_MO:OFF_MI], w_in[:, OFF_GA:N_IN]], axis=1).astype(BF16)
    bog = jnp.concatenate([b2[:, OFF_MO:OFF_MI], b2[:, OFF_GA:N_IN]], axis=1)
    mixout_w = (wog, bog, w_up_m.astype(BF16), w_up_a.astype(BF16), w_out.astype(BF16),
                w_router.T, b_router.reshape(N_EXPERTS, 1))
    return inproj_w, (wbd, pe, kg0), mixout_w


def _pick_tile(m, pref):
    t = pref
    while m % t:
        t //= 2
    return t


def kernel(x_prompt, x_sample, cache_nsa_kv, state_win_kv, state_mlstm_C, state_mlstm_n, state_mlstm_m, page_table, c_prompt, c_sample, w_ada, b_ada, g_mix, g_ffn, w_in, b_in, q_norm_g, k_norm_g, cmp_pe_k, cmp_pe_v, cmp_w_k, cmp_w_v, w_up_m, w_up_a, w_out, w_router, b_router, w_gu, b_gu, w_dn, b_dn):
    depth = w_in.shape[0]
    assert depth == 1
    B, T, D = x_prompt.shape
    DB, TS, _ = x_sample.shape
    n_pages = page_table.shape[1]
    past_len = n_pages * PAGE_SIZE
    wbuf = state_win_kv.shape[2]
    tp = SAMPLE_PAD_T
    assert TS <= tp and wbuf % tp == 0 and T % 128 == 0

    l = 0
    inproj_w, cmp_w, mixout_w = _prep_weights(
        w_in[l], b_in[l], q_norm_g[l], k_norm_g[l], cmp_pe_k[l], cmp_pe_v[l], cmp_w_k[l], cmp_w_v[l],
        w_up_m[l], w_up_a[l], w_out[l], w_router[l], b_router[l])
    wbd, pe, kg0 = cmp_w
    gmix = g_mix[l].reshape(1, D)
    gffn = g_ffn[l].reshape(1, D)

    nc = B + DB
    nc_pad = -(-nc // SUBLANES) * SUBLANES
    c_all = jnp.concatenate([c_prompt, c_sample, jnp.zeros((nc_pad - nc, D), F32)], axis=0)
    mod = _adaln(c_all, w_ada[l], b_ada[l])
    mod_p = mod[:B].reshape(B, 1, 6 * D)
    mod_s = jnp.repeat(mod[B:B + DB], tp, axis=0).reshape(1, DB * tp, 6 * D)

    mp = B * T
    tm = _pick_tile(T, 256)
    xp = x_prompt.reshape(mp, D)
    tabs_p = _rope_tables(jnp.arange(T, dtype=jnp.int32))
    mq, mk, mv, q, qr, rows, win, small = _inproj(xp, mod_p, gmix, tabs_p, inproj_w, tm, T // tm, T // tm)
    Lp = _pick_tile(T, 128)
    hm, C_p, n_p, m_p = _mlstm(mq, mk, mv, small, B, T, T, Lp)
    o_nsa = _nsa_prompt(q, qr, small, rows, win, wbd, pe, kg0, B, T)
    assert T % MOE_TM == 0
    ms_pad = -(-(DB * tp) // MOE_TM) * MOE_TM
    nt_p = mp // MOE_TM
    nt_all = nt_p + ms_pad // MOE_TM
    x1_p, xsl, info, cnt = _mixout(xp, hm, o_nsa, mod_p, gmix, gffn, mixout_w, T // MOE_TM, nt_all)

    ms = DB * tp
    xs_pad = jnp.concatenate([x_sample, jnp.zeros((DB, tp - TS, D), F32)], axis=1).reshape(ms, D)
    pos_s = past_len + jnp.tile(jnp.arange(tp, dtype=jnp.int32), DB)
    tabs_s = _rope_tables(pos_s)
    mq_s, mk_s, mv_s, q_s, qr_s, rows_s, win_s, small_s = _inproj(xs_pad, mod_s, gmix, tabs_s, inproj_w, ms, 1, 1)
    hm_s, C_s, n_s, m_s = _mlstm(mq_s, mk_s, mv_s, small_s, DB, tp, TS, tp,
                                 state=(state_mlstm_C[l], state_mlstm_n[l], state_mlstm_m[l]))
    cache2 = cache_nsa_kv[l].reshape(cache_nsa_kv.shape[1], PAGE_SIZE, 4 * LANES)
    winbuf = state_win_kv[l].reshape(DB, wbuf, 2 * LANES)
    o_nsa_s, win_out_s = _nsa_sample(page_table, cache2, q_s, qr_s, small_s, rows_s, win_s, winbuf,
                                     wbd, pe, kg0, TS)
    assert ms_pad == MOE_TM
    rpad = lambda a: jnp.concatenate([a, jnp.zeros((ms_pad - ms, a.shape[1]), a.dtype)], axis=0) if ms_pad > ms else a
    mod_sp = rpad(mod_s[0])[None]
    x1_s, xsl, info, cnt = _mixout(rpad(xs_pad), rpad(hm_s), rpad(o_nsa_s), mod_sp, gmix, gffn, mixout_w,
                                   1, nt_all, tile0=nt_p, shared=(xsl, info, cnt),
                                   t_mod=tp, t_valid=TS, m_valid=ms)

    ysl = _moe_experts(_moe_plan(cnt[:, :, 0]), xsl, w_gu[l], b_gu[l], w_dn[l], b_dn[l])
    y_p = _combine(ysl, info, x1_p, mod_p, T // MOE_TM).reshape(B, T, D)
    y_s_all = _combine(ysl, info, x1_s, mod_sp, 1, tile0=nt_p)
    valid = lambda a: a.reshape(DB, tp, -1)[:, :TS].reshape(DB * TS, -1)
    y_s = valid(y_s_all[:ms]).reshape(DB, TS, D)

    kv_p = rows.reshape(1, B, T, 4, A_KV, A_DH)
    kv_s = valid(rows_s).reshape(1, DB, TS, 4, A_KV, A_DH)
    wp = min(WINDOW, T)
    win_p = win.reshape(B, T, 2, A_KV, A_DH)[:, T - wp:][None]
    win_s_out = win_out_s.reshape(1, DB, wbuf, 2, A_KV, A_DH)
    return (y_p, y_s, kv_p, kv_s, win_p, win_s_out,
            C_p[None], n_p[None], m_p[None], C_s[None], n_s[None], m_s[None])
```

```python
import functools
import math

import jax
import jax.numpy as jnp
from jax import lax
from jax.experimental import pallas as pl
from jax.experimental.pallas import tpu as pltpu

F32 = jnp.float32
BF16 = jnp.bfloat16

D_MODEL = 1024
M_HEADS = 4
M_DH = 128
M_WIDTH = M_HEADS * M_DH
A_HEADS = 8
A_KV = 2
A_HPG = A_HEADS // A_KV
A_DH = 64
A_WIDTH = A_HEADS * A_DH
CMP_STRIDE = 16
CMP_LEN = 32
SEL_LEN = 64
N_SEL = 16
WINDOW = 512
PAGE_SIZE = 128
ROPE_THETA = 500000.0
ROT_DIM = A_DH // 4
ATT_SCALE = A_DH ** -0.5
N_EXPERTS = 32
TOP_K = 4
D_EXPERT = D_MODEL
SWIGLU_LIMIT = 7.0
SWIGLU_ALPHA = 1.702
EPS = 1e-6

OFF_MQ, OFF_MK, OFF_MV, OFF_MO = 0, M_WIDTH, 2 * M_WIDTH, 3 * M_WIDTH
OFF_MI = 4 * M_WIDTH
OFF_MF = OFF_MI + M_HEADS
OFF_AQ = OFF_MF + M_HEADS
OFF_AKV = OFF_AQ + A_WIDTH
OFF_AG = OFF_AKV + 6 * A_KV * A_DH
OFF_GA = OFF_AG + 3 * A_HEADS
OFF_GB = OFF_GA + D_MODEL
N_IN = OFF_GB + D_MODEL

LANES = 128
SUBLANES = 8
VMEM_LIMIT = 56 * 1024 * 1024

NEG_BIG = -1e30
M_INIT = -1e29
LOG2E = 1.4426950408889634
SAMPLE_PAD_T = 8


def _cparams(sem):
    return pltpu.CompilerParams(dimension_semantics=sem, vmem_limit_bytes=VMEM_LIMIT)


def _bdot(a, b):
    return jnp.dot(a.astype(BF16), b.astype(BF16), preferred_element_type=F32)


def _bdot_t(a, b):
    return lax.dot_general(a.astype(BF16), b.astype(BF16), (((1,), (1,)), ((), ())),
                           preferred_element_type=F32)


def _split(a):
    hi = a.astype(BF16)
    lo = (a - hi.astype(F32)).astype(BF16)
    return hi, lo


def _dot3(a, b):
    ah, al = _split(a)
    bh, bl = _split(b)
    return (jnp.dot(ah, bh, preferred_element_type=F32) + jnp.dot(al, bh, preferred_element_type=F32)
            + jnp.dot(ah, bl, preferred_element_type=F32))


def _dot2_exact_rhs(a, b_bf16):
    ah, al = _split(a)
    return jnp.dot(ah, b_bf16, preferred_element_type=F32) + jnp.dot(al, b_bf16, preferred_element_type=F32)


def _sigmoid(x):
    return 1.0 / (1.0 + jnp.exp(-x))


def _rmsnorm_rows(x, g):
    return x * lax.rsqrt(jnp.mean(x * x, axis=-1, keepdims=True) + EPS) * g


def _adaln_kernel(c_ref, w_ref, b_ref, o_ref):
    c = c_ref[...]
    s = c * _sigmoid(c)
    o_ref[...] = _dot3(s, w_ref[...]) + b_ref[...]


def _adaln(c, w, b):
    mc, d = c.shape
    n = w.shape[1]
    tn = 1024
    return pl.pallas_call(
        _adaln_kernel,
        grid=(n // tn,),
        in_specs=[pl.BlockSpec((mc, d), lambda j: (0, 0)),
                  pl.BlockSpec((d, tn), lambda j: (0, j)),
                  pl.BlockSpec((1, tn), lambda j: (0, j))],
        out_specs=pl.BlockSpec((mc, tn), lambda j: (0, j)),
        out_shape=jax.ShapeDtypeStruct((mc, n), F32),
        compiler_params=_cparams(("parallel",)),
        name="adaln",
    )(c, w, b.reshape(1, n))


def _head_norm(z, bd, gain):
    ms = _dot2_exact_rhs(z * z, bd)
    return z * lax.rsqrt(ms + EPS) * gain


def _rope(z, cos, s_prev, s_next):
    w = z.shape[1]
    rep = w // LANES
    if rep > 1:
        cos = jnp.concatenate([cos] * rep, axis=1)
        s_prev = jnp.concatenate([s_prev] * rep, axis=1)
        s_next = jnp.concatenate([s_next] * rep, axis=1)
    z_prev = pltpu.roll(z, ROT_DIM // 2, 1)
    z_next = pltpu.roll(z, w - ROT_DIM // 2, 1)
    return z * cos + z_prev * s_prev + z_next * s_next


def _inproj_kernel(x_ref, mod_ref, gmix_ref, cos_ref, sp_ref, sn_ref,
                   wm_ref, bm_ref, wq_ref, bq_ref, wkv_ref, bkv_ref, ws_ref, bs_ref,
                   qg_ref, kg_ref, bd_ref,
                   mq_ref, mk_ref, mv_ref, q_ref, qr_ref, rows_ref, win_ref, small_ref, rows_t_ref=None):
    x = x_ref[...]
    sh1 = mod_ref[:, 0:D_MODEL]
    sc1 = mod_ref[:, D_MODEL:2 * D_MODEL]
    h = _rmsnorm_rows(x, gmix_ref[...]) * (1.0 + sc1) + sh1
    hb = h.astype(BF16)

    mq_ref[...] = jnp.dot(hb, wm_ref[:, 0:M_WIDTH], preferred_element_type=F32) + bm_ref[:, 0:M_WIDTH]
    mk = jnp.dot(hb, wm_ref[:, M_WIDTH:2 * M_WIDTH], preferred_element_type=F32) + bm_ref[:, M_WIDTH:2 * M_WIDTH]
    mk_ref[...] = mk * (M_DH ** -0.5)
    mv_ref[...] = (jnp.dot(hb, wm_ref[:, 2 * M_WIDTH:3 * M_WIDTH], preferred_element_type=F32)
                   + bm_ref[:, 2 * M_WIDTH:3 * M_WIDTH])

    cos, sp, sn = cos_ref[...], sp_ref[...], sn_ref[...]
    zq = jnp.dot(hb, wq_ref[...], preferred_element_type=F32) + bq_ref[...]
    qn = _head_norm(zq, bd_ref[...], qg_ref[...])
    q_ref[...] = qn
    qr_ref[...] = _rope(qn, cos, sp, sn)

    zkv = jnp.dot(hb, wkv_ref[...], preferred_element_type=F32) + bkv_ref[...]
    bd2 = bd_ref[0:LANES, 0:LANES]
    ksel = _head_norm(zkv[:, 2 * LANES:3 * LANES], bd2, kg_ref[0:1, :])
    rows = jnp.concatenate([zkv[:, 0:2 * LANES], _rope(ksel, cos, sp, sn), zkv[:, 3 * LANES:4 * LANES]], axis=1)
    rows_ref[...] = rows
    if rows_t_ref is not None:
        rows_t_ref[...] = jnp.transpose(rows)
    kwin = _head_norm(zkv[:, 4 * LANES:5 * LANES], bd2, kg_ref[1:2, :])
    win_ref[:, 0:LANES] = _rope(kwin, cos, sp, sn)
    win_ref[:, LANES:2 * LANES] = zkv[:, 5 * LANES:6 * LANES]

    small_ref[...] = _dot3(h, ws_ref[...]) + bs_ref[...]


def _inproj(x2, mod3, gmix, tabs, wts, tm, tiles_per_mod, pos_tiles, rows_t_batches=None):
    m = x2.shape[0]
    cos_t, sp_t, sn_t = tabs
    (wm, bm, wq, bq, wkv, bkv, ws, bs, qg, kg, bd) = wts
    r = mod3.shape[1]
    row = lambda i: (i, 0)
    const = lambda i: (0, 0)
    tab = lambda i: (i % pos_tiles, 0)
    in_specs = [
        pl.BlockSpec((tm, D_MODEL), row),
        pl.BlockSpec((None, r, 6 * D_MODEL), lambda i: (i // tiles_per_mod, 0, 0)),
        pl.BlockSpec((1, D_MODEL), const),
        pl.BlockSpec((tm, LANES), tab), pl.BlockSpec((tm, LANES), tab), pl.BlockSpec((tm, LANES), tab),
        pl.BlockSpec(wm.shape, const), pl.BlockSpec(bm.shape, const),
        pl.BlockSpec(wq.shape, const), pl.BlockSpec(bq.shape, const),
        pl.BlockSpec(wkv.shape, const), pl.BlockSpec(bkv.shape, const),
        pl.BlockSpec(ws.shape, const), pl.BlockSpec(bs.shape, const),
        pl.BlockSpec(qg.shape, const), pl.BlockSpec(kg.shape, const), pl.BlockSpec(bd.shape, const),
    ]
    widths = (M_WIDTH, M_WIDTH, M_WIDTH, A_WIDTH, A_WIDTH, 4 * LANES, 2 * LANES, LANES)
    out_specs = [pl.BlockSpec((tm, w), row) for w in widths]
    out_shape = [jax.ShapeDtypeStruct((m, w), F32) for w in widths]
    if rows_t_batches is not None:
        out_specs.append(pl.BlockSpec((None, 4 * LANES, tm), lambda i: (i // tiles_per_mod, 0, i % tiles_per_mod)))
        out_shape.append(jax.ShapeDtypeStruct((rows_t_batches, 4 * LANES, m // rows_t_batches), F32))
    return pl.pallas_call(
        _inproj_kernel,
        grid=(m // tm,),
        in_specs=in_specs,
        out_specs=out_specs,
        out_shape=out_shape,
        compiler_params=_cparams(("parallel",)),
        name="inproj",
    )(x2, mod3, gmix, cos_t, sp_t, sn_t, wm, bm, wq, bq, wkv, bkv, ws, bs, qg, kg, bd)


def _log_sigmoid(x):
    return jnp.minimum(x, 0.0) - jnp.log(1.0 + jnp.exp(-jnp.abs(x)))


def _mlstm_kernel(*refs, L, t_valid, has_state):
    if has_state:
        q_ref, k_ref, v_ref, s_ref, c0_ref, n0_ref, m0_ref, h_ref, c_ref, n_ref, m_ref = refs
    else:
        q_ref, k_ref, v_ref, s_ref, h_ref, c_ref, n_ref, m_ref = refs
    c = pl.program_id(1)

    @pl.when(c == 0)
    def _():
        if has_state:
            c_ref[...] = c0_ref[...]
            n_ref[...] = n0_ref[...]
            m_ref[...] = m0_ref[...]
        else:
            c_ref[...] = jnp.zeros(c_ref.shape, F32)
            n_ref[...] = jnp.zeros(n_ref.shape, F32)
            m_ref[...] = jnp.zeros(m_ref.shape, F32)

    row = lax.broadcasted_iota(jnp.int32, (L, L), 0)
    col = lax.broadcasted_iota(jnp.int32, (L, L), 1)
    causal = col <= row
    eye = col == row
    tok_col = c * L + lax.broadcasted_iota(jnp.int32, (L, 1), 0)
    valid_col = tok_col < t_valid
    for hd in range(M_HEADS):
        lo, hi = hd * M_DH, (hd + 1) * M_DH
        q = q_ref[:, lo:hi]
        k = k_ref[:, lo:hi]
        v = v_ref[:, lo:hi]
        i_col = s_ref[:, hd:hd + 1]
        lf_col = _log_sigmoid(s_ref[:, M_HEADS + hd:M_HEADS + hd + 1])
        lf_col = jnp.where(valid_col, lf_col, 0.0)
        i_col = jnp.where(valid_col, i_col, -jnp.inf)
        i_row = jnp.sum(jnp.where(eye, i_col, 0.0), axis=0, keepdims=True)
        lf_row = jnp.sum(jnp.where(eye, lf_col, 0.0), axis=0, keepdims=True)
        b_col = jnp.sum(jnp.where(causal, lf_row, 0.0), axis=1, keepdims=True)
        b_row = jnp.sum(jnp.where(row <= col, lf_col, 0.0), axis=0, keepdims=True)
        m_prev = m_ref[:, hd:hd + 1]
        dmat = jnp.where(causal, b_col - b_row + i_row, -jnp.inf)
        inter = b_col + m_prev
        m_row = jnp.maximum(jnp.max(dmat, axis=1, keepdims=True), inter)
        w = jnp.exp(dmat - m_row)
        w_inter = jnp.exp(inter - m_row)
        s = _bdot_t(q, k) * w
        cm = c_ref[hd]
        nv = n_ref[hd]
        num = _bdot(s, v) + w_inter * _bdot_t(q, cm)
        den = jnp.sum(s, axis=1, keepdims=True) + w_inter * jnp.sum(q * nv, axis=1, keepdims=True)
        h_ref[:, lo:hi] = num / jnp.maximum(jnp.abs(den), jnp.exp(-m_row))
        b_last = b_col[L - 1:L, :]
        dec_col = b_last - b_col + i_col
        dec_row = b_last - b_row + i_row
        m_new = jnp.maximum(b_last + m_prev, jnp.max(dec_row, axis=1, keepdims=True))
        ws_col = jnp.exp(dec_col - m_new)
        wc = jnp.exp(b_last + m_prev - m_new)
        vw = (v * ws_col).astype(BF16)
        upd = lax.dot_general(vw, k.astype(BF16), (((0,), (0,)), ((), ())), preferred_element_type=F32)
        c_ref[hd] = wc * cm + upd
        n_ref[hd] = wc * nv + jnp.sum(k * ws_col, axis=0, keepdims=True)
        m_ref[:, hd:hd + 1] = m_new


def _mlstm(mq, mk, mv, small, nb, t_pad, t_valid, L, state=None):
    nc = t_pad // L
    has_state = state is not None
    blk = lambda b, c: (b * nc + c, 0)
    st4 = lambda b, c: (b, 0, 0, 0)
    st3 = lambda b, c: (b, 0, 0)
    in_specs = [pl.BlockSpec((L, M_WIDTH), blk)] * 3 + [pl.BlockSpec((L, LANES), blk)]
    args = [mq, mk, mv, small]
    if has_state:
        c0, n0, m0 = state
        in_specs += [pl.BlockSpec((None, M_HEADS, M_DH, M_DH), st4),
                     pl.BlockSpec((None, M_HEADS, 1, M_DH), st4),
                     pl.BlockSpec((None, 1, M_HEADS), st3)]
        args += [c0, n0.reshape(nb, M_HEADS, 1, M_DH), m0.reshape(nb, 1, M_HEADS)]
    out_specs = [pl.BlockSpec((L, M_WIDTH), blk),
                 pl.BlockSpec((None, M_HEADS, M_DH, M_DH), st4),
                 pl.BlockSpec((None, M_HEADS, 1, M_DH), st4),
                 pl.BlockSpec((None, 1, M_HEADS), st3)]
    out_shape = [jax.ShapeDtypeStruct((nb * t_pad, M_WIDTH), F32),
                 jax.ShapeDtypeStruct((nb, M_HEADS, M_DH, M_DH), F32),
                 jax.ShapeDtypeStruct((nb, M_HEADS, 1, M_DH), F32),
                 jax.ShapeDtypeStruct((nb, 1, M_HEADS), F32)]
    h, cs, ns, ms = pl.pallas_call(
        functools.partial(_mlstm_kernel, L=L, t_valid=t_valid, has_state=has_state),
        grid=(nb, nc),
        in_specs=in_specs,
        out_specs=out_specs,
        out_shape=out_shape,
        compiler_params=_cparams(("parallel", "arbitrary")),
        name="mlstm",
    )(*args)
    return h, cs, ns.reshape(nb, M_HEADS, M_DH), ms.reshape(nb, M_HEADS)


def _stack_heads(qt, g):
    t = qt.shape[0]
    z = jnp.zeros((t, A_DH), F32)
    parts = []
    for hh in range(A_HPG):
        hd = g * A_HPG + hh
        qh = qt[:, hd * A_DH:(hd + 1) * A_DH] * (ATT_SCALE * LOG2E)
        parts.append(jnp.concatenate([qh, z], axis=1) if g == 0 else jnp.concatenate([z, qh], axis=1))
    return jnp.concatenate(parts, axis=0).astype(BF16)


def _gate_cols(small, g, br):
    cols = []
    for hh in range(A_HPG):
        c0 = 2 * M_HEADS + (g * A_HPG + hh) * 3 + br
        cols.append(_sigmoid(small[:, c0:c0 + 1]))
    return jnp.concatenate(cols, axis=0)


def _compress(k_ref, v_ref, nseg, wbd_ref, pe_ref, kg0):
    acc_lo = jnp.zeros((nseg, 2 * LANES), F32)
    acc_hi = jnp.zeros((nseg, 2 * LANES), F32)
    for l in range(CMP_STRIDE):
        xl = jnp.concatenate([k_ref[pl.ds(l, nseg, stride=CMP_STRIDE), :],
                              v_ref[pl.ds(l, nseg, stride=CMP_STRIDE), :]], axis=1)
        acc_lo = acc_lo + _bdot(xl + pe_ref[l], wbd_ref[l])
        acc_hi = acc_hi + _bdot(xl + pe_ref[CMP_STRIDE + l], wbd_ref[CMP_STRIDE + l])
    kv = acc_lo + pltpu.roll(acc_hi, nseg - 1, 0)
    kc = kv[:, 0:LANES]
    vc = kv[:, LANES:2 * LANES]
    lane = lax.broadcasted_iota(jnp.int32, (nseg, LANES), 1)
    sq = kc * kc
    ms0 = jnp.sum(jnp.where(lane < A_DH, sq, 0.0), axis=1, keepdims=True) * (1.0 / A_DH)
    ms1 = jnp.sum(jnp.where(lane >= A_DH, sq, 0.0), axis=1, keepdims=True) * (1.0 / A_DH)
    ms = jnp.where(lane < A_DH, ms0, ms1)
    kc = kc * lax.rsqrt(ms + EPS) * kg0
    return kc, vc


def _cmp_branch(qn_g, kc_b, vc_b, tpos_rows, nseg, n_tok):
    s = _bdot_t(qn_g, kc_b)
    nidx = lax.broadcasted_iota(jnp.int32, (1, nseg), 1)
    vis = (nidx * CMP_STRIDE + (CMP_LEN - 1)) <= tpos_rows
    sm = jnp.where(vis, s, NEG_BIG)
    mx = jnp.max(sm, axis=1, keepdims=True)
    e = jnp.where(vis, jnp.exp2(sm - mx), 0.0)
    d = jnp.sum(e, axis=1, keepdims=True)
    p = e / jnp.where(d > 0, d, 1.0)
    o = _bdot(p, vc_b)
    imp = p[0:n_tok]
    for hh in range(1, A_HPG):
        imp = imp + p[hh * n_tok:(hh + 1) * n_tok]
    return o, imp


def _masked_attn_direct(q_g, k_parts, v_parts, allowed_parts, feature_major):
    ss = [jnp.where(al, _bdot(q_g, kk) if fm else _bdot_t(q_g, kk), NEG_BIG)
          for kk, al, fm in zip(k_parts, allowed_parts, feature_major)]
    mx = ss[0].max(axis=1, keepdims=True)
    for s in ss[1:]:
        mx = jnp.maximum(mx, s.max(axis=1, keepdims=True))
    num = None
    den = None
    for s, al, vv, fm in zip(ss, allowed_parts, v_parts, feature_major):
        e = jnp.where(al, jnp.exp2(s - mx), 0.0)
        dd = jnp.sum(e, axis=1, keepdims=True)
        oo = _bdot_t(e, vv) if fm else _bdot(e, vv)
        num = oo if num is None else num + oo
        den = dd if den is None else den + dd
    return num / jnp.where(den > 0, den, 1.0)


def _assemble_heads(o_groups, n_tok):
    pieces = []
    for g in range(A_KV):
        for hh in range(A_HPG):
            pieces.append(o_groups[g][hh * n_tok:(hh + 1) * n_tok, g * A_DH:(g + 1) * A_DH])
    return jnp.concatenate(pieces, axis=1)


def _lane_tile(a, width):
    rep = width // LANES
    return a if rep == 1 else jnp.concatenate([a] * rep, axis=1)


def _add_bias(s, bias):
    t, k = bias.shape
    return (s.reshape(A_HPG, t, k) + bias[None]).reshape(A_HPG * t, k)


def _nsa_prompt_kernel(q_ref, qr_ref, small_ref, rows_ref, win_ref, wbd_ref, pe_ref, kg0_ref,
                       pool_ref, o_ref,
                       kraw_sc, vraw_sc, kc_sc, vc_sc, m_sc, l_sc, acc_sc, *, T, tq, kc_len):
    qi = pl.program_id(1)
    nseg = T // CMP_STRIDE
    nsb = T // SEL_LEN

    @pl.when(qi == 0)
    def _():
        kraw_sc[...] = rows_ref[:, 0:LANES]
        vraw_sc[...] = rows_ref[:, LANES:2 * LANES]
        kc, vc = _compress(kraw_sc, vraw_sc, nseg, wbd_ref, pe_ref, kg0_ref[...])
        kc_sc[...] = kc
        vc_sc[...] = vc

    t0 = qi * tq
    tpos_col = t0 + lax.broadcasted_iota(jnp.int32, (tq, 1), 0)
    tpos_rows = jnp.concatenate([tpos_col] * A_HPG, axis=0)
    tpos_lane = t0 + lax.broadcasted_iota(jnp.int32, (1, tq), 1)
    q = q_ref[...]
    qr = qr_ref[...]
    small = small_ref[...]
    kc_b = kc_sc[...].astype(BF16)
    vc_b = vc_sc[...].astype(BF16)
    bidx = lax.broadcasted_iota(jnp.int32, (nsb, tq), 0)
    cur = tpos_lane // SEL_LEN
    r4 = A_HPG * tq
    qr_gs = [_stack_heads(qr, g) for g in range(A_KV)]
    o_cmps = []
    sel_bs = []
    for g in range(A_KV):
        qn_g = _stack_heads(q, g)
        o_cmp, imp = _cmp_branch(qn_g, kc_b, vc_b, tpos_rows, nseg, tq)
        o_cmps.append(o_cmp)
        imp_sel = _dot2_exact_rhs(imp, pool_ref[...])
        imp_t = jnp.transpose(imp_sel)[0:nsb, :]
        val = jnp.where(bidx < cur, imp_t, -1.0)
        rank = jnp.zeros((nsb, tq), F32)
        for bp in range(nsb):
            vb = val[bp:bp + 1, :]
            ahead = jnp.where(vb > val, 1.0, jnp.where((vb == val) & (bidx > bp), 1.0, 0.0))
            rank = rank + ahead
        sel_t = jnp.where(((rank < (N_SEL - 1)) & (bidx < cur)) | (bidx == cur), 1.0, 0.0)
        if nsb < LANES:
            sel_t = jnp.concatenate([sel_t, jnp.zeros((LANES - nsb, tq), F32)], axis=0)
        sel_bs.append(jnp.transpose(sel_t).astype(BF16))

    m_sc[...] = jnp.full(m_sc.shape, M_INIT, F32)
    l_sc[...] = jnp.zeros(l_sc.shape, F32)
    acc_sc[...] = jnp.zeros(acc_sc.shape, F32)

    def sel_body(c, carry):
        k0 = pl.multiple_of(c * kc_len, kc_len)
        kb = rows_ref[pl.ds(k0, kc_len), 2 * LANES:3 * LANES].astype(BF16)
        vb = rows_ref[pl.ds(k0, kc_len), 3 * LANES:4 * LANES].astype(BF16)
        kpos = k0 + lax.broadcasted_iota(jnp.int32, (1, kc_len), 1)
        causal = kpos <= tpos_col
        kblk = (k0 + lax.broadcasted_iota(jnp.int32, (LANES, kc_len), 1)) // SEL_LEN
        expand = jnp.where(kblk == lax.broadcasted_iota(jnp.int32, (LANES, kc_len), 0), 1.0, 0.0).astype(BF16)
        for g in range(A_KV):
            mk = jnp.dot(sel_bs[g], expand, preferred_element_type=F32)
            bias = jnp.where(causal, (mk - 1.0) * (-NEG_BIG), NEG_BIG)
            sm = _add_bias(_bdot_t(qr_gs[g], kb), bias)
            m_prev = m_sc[g]
            m_new = jnp.maximum(m_prev, jnp.max(sm, axis=1, keepdims=True))
            alpha = jnp.exp2(m_prev - m_new)
            p = jnp.exp2(sm - _lane_tile(m_new, kc_len))
            l_sc[g] = alpha * l_sc[g] + jnp.sum(p, axis=1, keepdims=True)
            acc_sc[g] = alpha * acc_sc[g] + _bdot(p, vb)
            m_sc[g] = m_new
        return carry

    lax.fori_loop(0, (t0 + tq + kc_len - 1) // kc_len, sel_body, 0)

    wk = min(WINDOW + tq, T)
    w0 = pl.multiple_of(jnp.clip(t0 + tq - wk, 0, T - wk), tq)
    kw = win_ref[pl.ds(w0, wk), 0:LANES].astype(BF16)
    vw = win_ref[pl.ds(w0, wk), LANES:2 * LANES].astype(BF16)
    wdiff = tpos_col - (w0 + lax.broadcasted_iota(jnp.int32, (1, wk), 1))
    wbias = jnp.where((wdiff >= 0) & (wdiff < WINDOW), 0.0, NEG_BIG)

    o_groups = []
    for g in range(A_KV):
        l = l_sc[g]
        o_sel = acc_sc[g] / jnp.where(l > 0, l, 1.0)
        sw = _add_bias(_bdot_t(qr_gs[g], kw), wbias)
        mw = jnp.broadcast_to(jnp.max(sw, axis=1, keepdims=True), (r4, LANES))
        pw = jnp.exp2(sw - _lane_tile(mw, wk))
        o_win = _bdot(pw, vw) / jnp.broadcast_to(jnp.sum(pw, axis=1, keepdims=True), (r4, LANES))
        o_groups.append(_gate_cols(small, g, 0) * o_cmps[g] + _gate_cols(small, g, 1) * o_sel
                        + _gate_cols(small, g, 2) * o_win)
    o_ref[...] = _assemble_heads(o_groups, tq)


def _nsa_prompt(q, qr, small, rows, win, wbd, pe, kg0, nb, T):
    tq = 128
    kc_len = _pick_tile(T, 512)
    nq = T // tq
    nseg = T // CMP_STRIDE
    nsb = T // SEL_LEN
    pool = (jnp.arange(nseg)[:, None] // (SEL_LEN // CMP_STRIDE) == jnp.arange(LANES)[None, :]).astype(BF16)
    tile = lambda b, i: (b * nq + i, 0)
    per_b = lambda b, i: (b, 0)
    c2 = lambda b, i: (0, 0)
    c3 = lambda b, i: (0, 0, 0)
    r4 = A_HPG * tq
    return pl.pallas_call(
        functools.partial(_nsa_prompt_kernel, T=T, tq=tq, kc_len=kc_len),
        grid=(nb, nq),
        in_specs=[pl.BlockSpec((tq, A_WIDTH), tile), pl.BlockSpec((tq, A_WIDTH), tile),
                  pl.BlockSpec((tq, LANES), tile),
                  pl.BlockSpec((T, 4 * LANES), per_b), pl.BlockSpec((T, 2 * LANES), per_b),
                  pl.BlockSpec(wbd.shape, c3), pl.BlockSpec(pe.shape, c3), pl.BlockSpec(kg0.shape, c2),
                  pl.BlockSpec(pool.shape, c2)],
        out_specs=pl.BlockSpec((tq, A_WIDTH), tile),
        out_shape=jax.ShapeDtypeStruct((nb * T, A_WIDTH), F32),
        scratch_shapes=[pltpu.VMEM((T, LANES), F32), pltpu.VMEM((T, LANES), F32),
                        pltpu.VMEM((nseg, LANES), F32), pltpu.VMEM((nseg, LANES), F32),
                        pltpu.VMEM((A_KV, r4, LANES), F32), pltpu.VMEM((A_KV, r4, LANES), F32),
                        pltpu.VMEM((A_KV, r4, LANES), F32)],
        compiler_params=_cparams(("parallel", "arbitrary")),
        name="nsa_prompt",
    )(q, qr, small, rows, win, wbd, pe, kg0, pool)


def _nsa_sample_kernel(pt_ref, cache_ref, q_ref, qr_ref, small_ref, rows_ref, winnew_ref, winbuf_ref,
                       wbd_ref, pe_ref, kg0_ref, pool_ref, expand_ref,
                       o_ref, winout_ref,
                       cmp_buf, sel_buf, kraw_sc, vraw_sc, sems, *, n_pages, past_len, t_valid):
    b = pl.program_id(0)
    nb = pl.num_programs(0)
    tp = SAMPLE_PAD_T
    nseg = past_len // CMP_STRIDE
    nsb = past_len // SEL_LEN
    wbuf = winbuf_ref.shape[0]

    def page_copies(bb, p, phase):
        page = pt_ref[bb * n_pages + p]
        dst_lanes = pl.ds(pl.multiple_of(p * PAGE_SIZE, PAGE_SIZE), PAGE_SIZE)
        if phase == 0:
            return [pltpu.make_async_copy(cache_ref.at[page, pl.ds(0, 2 * LANES), :],
                                          cmp_buf.at[:, dst_lanes], sems.at[0])]
        return [pltpu.make_async_copy(cache_ref.at[page, pl.ds(2 * LANES, 2 * LANES), :],
                                      sel_buf.at[:, dst_lanes], sems.at[1])]

    def start_all(bb, phase):
        def body(p, c):
            for cp in page_copies(bb, p, phase):
                cp.start()
            return c
        lax.fori_loop(0, n_pages, body, 0)

    def wait_all(bb, phase):
        def body(p, c):
            for cp in page_copies(bb, p, phase):
                cp.wait()
            return c
        lax.fori_loop(0, n_pages, body, 0)

    @pl.when(b == 0)
    def _():
        start_all(b, 0)

    start_all(b, 1)
    wait_all(b, 0)

    for p in range(n_pages):
        rs = slice(p * PAGE_SIZE, (p + 1) * PAGE_SIZE)
        kraw_sc[rs, :] = jnp.transpose(cmp_buf[0:LANES, rs])
        vraw_sc[rs, :] = jnp.transpose(cmp_buf[LANES:2 * LANES, rs])
    kc, vc = _compress(kraw_sc, vraw_sc, nseg, wbd_ref, pe_ref, kg0_ref[...])
    kc_b = kc.astype(BF16)
    vc_b = vc.astype(BF16)
    q = q_ref[...]
    qr = qr_ref[...]
    small = small_ref[...]
    tpos_col = past_len + lax.broadcasted_iota(jnp.int32, (tp, 1), 0)
    tpos_rows = jnp.concatenate([tpos_col] * A_HPG, axis=0)
    bp_idx = lax.broadcasted_iota(jnp.int32, (nsb, nsb), 0)
    b_idx = lax.broadcasted_iota(jnp.int32, (nsb, nsb), 1)
    o_cmps = []
    sels = []
    for g in range(A_KV):
        qn_g = _stack_heads(q, g)
        o_cmp, imp = _cmp_branch(qn_g, kc_b, vc_b, tpos_rows, nseg, tp)
        o_cmps.append(o_cmp)
        imp_sel = _dot2_exact_rhs(imp, pool_ref[...])
        imp_pad = jnp.concatenate([imp_sel, jnp.zeros((nsb - tp, nsb), F32)], axis=0)
        imp_t = jnp.transpose(imp_pad)
        rows_sel = []
        for t in range(tp):
            if t < t_valid:
                row_t = imp_sel[t:t + 1, :]
                col_t = imp_t[:, t:t + 1]
                ahead = jnp.where(col_t > row_t, 1.0, jnp.where((col_t == row_t) & (bp_idx < b_idx), 1.0, 0.0))
                rank = jnp.sum(ahead, axis=0, keepdims=True)
                rows_sel.append(jnp.where(rank < (N_SEL - 1), 1.0, 0.0))
            else:
                rows_sel.append(jnp.zeros((1, nsb), F32))
        sels.append(jnp.concatenate(rows_sel, axis=0).astype(BF16))

    @pl.when(b + 1 < nb)
    def _():
        start_all(b + 1, 0)

    wait_all(b, 1)

    new_idx = lax.broadcasted_iota(jnp.int32, (tp, tp), 1)
    tok_idx = lax.broadcasted_iota(jnp.int32, (tp, tp), 0)
    new_ok = jnp.concatenate([jnp.where(new_idx <= tok_idx, 1.0, 0.0)] * A_HPG, axis=0) > 0.5
    wpos = past_len - wbuf + lax.broadcasted_iota(jnp.int32, (1, wbuf), 1)
    wdiff = tpos_col - wpos
    win_ok = jnp.concatenate([jnp.where((wdiff >= 0) & (wdiff < WINDOW), 1.0, 0.0)] * A_HPG, axis=0) > 0.5
    k_past = sel_buf[0:LANES, :].astype(BF16)
    v_past = sel_buf[LANES:2 * LANES, :].astype(BF16)
    k_new = rows_ref[:, 2 * LANES:3 * LANES]
    v_new = rows_ref[:, 3 * LANES:4 * LANES]
    kw_past = winbuf_ref[:, 0:LANES]
    vw_past = winbuf_ref[:, LANES:2 * LANES]
    kw_new = winnew_ref[:, 0:LANES]
    vw_new = winnew_ref[:, LANES:2 * LANES]
    o_groups = []
    for g in range(A_KV):
        qr_g = _stack_heads(qr, g)
        mk = jnp.dot(sels[g], expand_ref[...], preferred_element_type=F32)
        past_ok = jnp.concatenate([mk] * A_HPG, axis=0) > 0.5
        o_sel = _masked_attn_direct(qr_g, [k_past, k_new], [v_past, v_new], [past_ok, new_ok], [True, False])
        o_win = _masked_attn_direct(qr_g, [kw_past, kw_new], [vw_past, vw_new], [win_ok, new_ok], [False, False])
        o_groups.append(_gate_cols(small, g, 0) * o_cmps[g] + _gate_cols(small, g, 1) * o_sel
                        + _gate_cols(small, g, 2) * o_win)
    o_ref[...] = _assemble_heads(o_groups, tp)

    wb = winbuf_ref[...]
    rolled = pltpu.roll(wb, wbuf - t_valid, 0)
    newr = pltpu.roll(winnew_ref[...], tp - t_valid, 0)
    sub = lax.broadcasted_iota(jnp.int32, (tp, 2 * LANES), 0)
    winout_ref[0:wbuf - tp, :] = rolled[0:wbuf - tp, :]
    winout_ref[wbuf - tp:wbuf, :] = jnp.where(sub < tp - t_valid, rolled[wbuf - tp:wbuf, :], newr)


def _nsa_sample(page_table, cache, q, qr, small, rows, winnew, winbuf, wbd, pe, kg0, t_valid):
    nb, n_pages = page_table.shape
    past_len = n_pages * PAGE_SIZE
    nseg = past_len // CMP_STRIDE
    nsb = past_len // SEL_LEN
    tp = SAMPLE_PAD_T
    wbuf = winbuf.shape[1]
    pool = (jnp.arange(nseg)[:, None] // (SEL_LEN // CMP_STRIDE) == jnp.arange(nsb)[None, :]).astype(BF16)
    expand = (jnp.arange(nsb)[:, None] == jnp.arange(past_len)[None, :] // SEL_LEN).astype(BF16)
    tile = lambda b, pt: (b, 0)
    c2 = lambda b, pt: (0, 0)
    c3 = lambda b, pt: (0, 0, 0)
    gs = pltpu.PrefetchScalarGridSpec(
        num_scalar_prefetch=1,
        grid=(nb,),
        in_specs=[pl.BlockSpec(memory_space=pl.ANY),
                  pl.BlockSpec((tp, A_WIDTH), tile), pl.BlockSpec((tp, A_WIDTH), tile),
                  pl.BlockSpec((tp, LANES), tile), pl.BlockSpec((tp, 4 * LANES), tile),
                  pl.BlockSpec((tp, 2 * LANES), tile),
                  pl.BlockSpec((None, wbuf, 2 * LANES), lambda b, pt: (b, 0, 0)),
                  pl.BlockSpec(wbd.shape, c3), pl.BlockSpec(pe.shape, c3), pl.BlockSpec(kg0.shape, c2),
                  pl.BlockSpec(pool.shape, c2), pl.BlockSpec(expand.shape, c2)],
        out_specs=[pl.BlockSpec((tp, A_WIDTH), tile),
                   pl.BlockSpec((None, wbuf, 2 * LANES), lambda b, pt: (b, 0, 0))],
        scratch_shapes=[pltpu.VMEM((2 * LANES, past_len), F32), pltpu.VMEM((2 * LANES, past_len), F32),
                        pltpu.VMEM((past_len, LANES), F32), pltpu.VMEM((past_len, LANES), F32),
                        pltpu.SemaphoreType.DMA((2,))],
    )
    return pl.pallas_call(
        functools.partial(_nsa_sample_kernel, n_pages=n_pages, past_len=past_len, t_valid=t_valid),
        grid_spec=gs,
        out_shape=[jax.ShapeDtypeStruct((nb * tp, A_WIDTH), F32),
                   jax.ShapeDtypeStruct((nb, wbuf, 2 * LANES), F32)],
        compiler_params=_cparams(("arbitrary",)),
        name="nsa_sample",
    )(page_table.reshape(-1), cache, q, qr, small, rows, winnew, winbuf, wbd, pe, kg0, pool, expand)


MOE_TM = 256
SEG_ALIGN = 8
SEG_BITS = (256, 128, 64, 32, 16, 8)
MOE_RL = -(-(MOE_TM * TOP_K + N_EXPERTS * (SEG_ALIGN - 1)) // LANES) * LANES


def _pack_halves(x):
    w = x.shape[1] // 2
    bits = lax.bitcast_convert_type(x.astype(BF16).astype(F32), jnp.uint32)
    return (bits[:, :w] & jnp.uint32(0xFFFF0000)) | (bits[:, w:] >> 16)


def _unpack_halves(u):
    hi = lax.bitcast_convert_type(u & jnp.uint32(0xFFFF0000), F32).astype(BF16)
    lo = lax.bitcast_convert_type(u << 16, F32).astype(BF16)
    return hi, lo


def _route_and_sort(h2, wrt_ref, brt_ref, xsl_ref, info_ref, cnt_ref, tm, t_mod, t_valid, m_valid):
    ne = N_EXPERTS
    h2b = h2.astype(BF16)
    h2l = (h2 - h2b.astype(F32)).astype(BF16)
    wh, wl = _split(wrt_ref[...])
    lt = _bdot_t(wh, h2b) + _bdot_t(wl, h2b) + _bdot_t(wh, h2l) + brt_ref[...]
    eidx = lax.broadcasted_iota(jnp.int32, (ne, tm), 0)
    rank = jnp.zeros((ne, tm), F32)
    for ep in range(ne):
        v = lt[ep:ep + 1, :]
        rank = rank + jnp.where(v > lt, 1.0, jnp.where((v == lt) & (eidx > ep), 1.0, 0.0))
    sel = rank < TOP_K
    if t_mod is not None:
        tok = pl.program_id(0) * tm + lax.broadcasted_iota(jnp.int32, (1, tm), 1)
        sel = sel & ((tok % t_mod) < t_valid) & (tok < m_valid)
    mx = jnp.max(jnp.where(sel, lt, NEG_BIG), axis=0, keepdims=True)
    ex = jnp.where(sel, jnp.exp(lt - mx), 0.0)
    den = jnp.sum(ex, axis=0, keepdims=True)
    gate = ex / jnp.where(den > 0, den, 1.0)
    self_ = jnp.where(sel, 1.0, 0.0)
    selb = self_.astype(BF16)
    er = lax.broadcasted_iota(jnp.int32, (ne, ne), 0)
    ec = lax.broadcasted_iota(jnp.int32, (ne, ne), 1)
    c = jnp.dot(jnp.where(ec <= er, 1.0, 0.0).astype(BF16), selb, preferred_element_type=F32)
    tr = lax.broadcasted_iota(jnp.int32, (tm, tm), 0)
    tc = lax.broadcasted_iota(jnp.int32, (tm, tm), 1)
    rk = jnp.dot(selb, jnp.where(tr < tc, 1.0, 0.0).astype(BF16), preferred_element_type=F32)
    cnt = jnp.sum(self_, axis=1, keepdims=True)
    cnt_al = jnp.floor((cnt + (SEG_ALIGN - 1)) * (1.0 / SEG_ALIGN)) * SEG_ALIGN
    cnt_b = jnp.broadcast_to(cnt_al, (ne, LANES))
    cnt_ref[...] = cnt_b
    off = jnp.dot(jnp.where(ec < er, 1.0, 0.0).astype(BF16), cnt_b.astype(BF16), preferred_element_type=F32)
    rowidx = off[:, 0:1] + rk
    rows_k, gates_k, exps_k = [], [], []
    for k in range(1, TOP_K + 1):
        mk = sel & (c == k)
        has = jnp.sum(jnp.where(mk, 1.0, 0.0), axis=0, keepdims=True)
        rows_k.append(jnp.sum(jnp.where(mk, rowidx, 0.0), axis=0, keepdims=True) + has - 1.0)
        gates_k.append(jnp.sum(jnp.where(mk, gate, 0.0), axis=0, keepdims=True))
        exps_k.append(jnp.sum(jnp.where(mk, eidx.astype(F32), 0.0), axis=0, keepdims=True))
    info_ref[...] = jnp.concatenate(rows_k + gates_k + exps_k + [jnp.zeros((4, tm), F32)], axis=0)
    ridx = lax.broadcasted_iota(jnp.int32, (MOE_RL, tm), 0).astype(F32)
    perm = jnp.zeros((MOE_RL, tm), F32)
    for k in range(TOP_K):
        perm = perm + jnp.where(ridx == rows_k[k], 1.0, 0.0)
    xs = jnp.dot(perm.astype(BF16), h2b, preferred_element_type=F32)
    xsl_ref[...] = _pack_halves(xs)


def _mixout_kernel(x_ref, hm_ref, on_ref, mod_ref, gmix_ref, gffn_ref,
                   wog_ref, bog_ref, wum_ref, wua_ref, wout_ref, wrt_ref, brt_ref,
                   x1_ref, xsl_ref, info_ref, cnt_ref, *, tm, t_mod, t_valid, m_valid, n_real):
    if n_real is not None:
        @pl.when(pl.program_id(0) >= n_real)
        def _():
            xsl_ref[...] = jnp.zeros(xsl_ref.shape, jnp.uint32)
            info_ref[...] = jnp.zeros(info_ref.shape, F32)
            cnt_ref[...] = jnp.zeros(cnt_ref.shape, F32)

        @pl.when(pl.program_id(0) < n_real)
        def _():
            _mixout_body(x_ref, hm_ref, on_ref, mod_ref, gmix_ref, gffn_ref, wog_ref, bog_ref, wum_ref,
                         wua_ref, wout_ref, wrt_ref, brt_ref, x1_ref, xsl_ref, info_ref, cnt_ref,
                         tm, t_mod, t_valid, m_valid)
    else:
        _mixout_body(x_ref, hm_ref, on_ref, mod_ref, gmix_ref, gffn_ref, wog_ref, bog_ref, wum_ref,
                     wua_ref, wout_ref, wrt_ref, brt_ref, x1_ref, xsl_ref, info_ref, cnt_ref,
                     tm, t_mod, t_valid, m_valid)


def _mixout_body(x_ref, hm_ref, on_ref, mod_ref, gmix_ref, gffn_ref,
                 wog_ref, bog_ref, wum_ref, wua_ref, wout_ref, wrt_ref, brt_ref,
                 x1_ref, xsl_ref, info_ref, cnt_ref, tm, t_mod, t_valid, m_valid):
    d = D_MODEL
    x = x_ref[...]
    sh1, sc1, gt1 = mod_ref[:, 0:d], mod_ref[:, d:2 * d], mod_ref[:, 2 * d:3 * d]
    sh2, sc2 = mod_ref[:, 3 * d:4 * d], mod_ref[:, 4 * d:5 * d]
    h = _rmsnorm_rows(x, gmix_ref[...]) * (1.0 + sc1) + sh1
    hb = h.astype(BF16)
    mo = jnp.dot(hb, wog_ref[:, 0:M_WIDTH], preferred_element_type=F32) + bog_ref[:, 0:M_WIDTH]
    ym = _bdot(_sigmoid(mo) * hm_ref[...], wum_ref[...])
    ya = _bdot(on_ref[...], wua_ref[...])
    ga = jnp.dot(hb, wog_ref[:, M_WIDTH:M_WIDTH + d], preferred_element_type=F32) + bog_ref[:, M_WIDTH:M_WIDTH + d]
    u = _sigmoid(ga) * ym
    gb = (jnp.dot(hb, wog_ref[:, M_WIDTH + d:M_WIDTH + 2 * d], preferred_element_type=F32)
          + bog_ref[:, M_WIDTH + d:M_WIDTH + 2 * d])
    u = u + _sigmoid(gb) * ya
    x1 = x + gt1 * _bdot(u, wout_ref[...])
    x1_ref[...] = x1
    h2 = _rmsnorm_rows(x1, gffn_ref[...]) * (1.0 + sc2) + sh2
    _route_and_sort(h2, wrt_ref, brt_ref, xsl_ref, info_ref, cnt_ref, tm, t_mod, t_valid, m_valid)


def _mixout_with_shared(*refs, n_shared, **kw):
    n_in = 13
    _mixout_kernel(*refs[:n_in], *refs[n_in + n_shared:], **kw)


def _mixout(x2, hm, on, mod3, gmix, gffn, wts, tiles_per_mod, nt_total, tile0=0, shared=None,
            t_mod=None, t_valid=None, m_valid=None):
    m = x2.shape[0]
    tm = MOE_TM
    nt = m // tm
    (wog, bog, wum, wua, wout, wr, br) = wts
    r = mod3.shape[1]
    n_extra = nt_total - tile0 - nt if shared is None else 0
    row = lambda i: (jnp.minimum(i, nt - 1), 0)
    const = lambda i: (0, 0)
    in_specs = [pl.BlockSpec((tm, D_MODEL), row), pl.BlockSpec((tm, M_WIDTH), row),
                pl.BlockSpec((tm, A_WIDTH), row),
                pl.BlockSpec((None, r, 6 * D_MODEL), lambda i: (jnp.minimum(i, nt - 1) // tiles_per_mod, 0, 0)),
                pl.BlockSpec((1, D_MODEL), const), pl.BlockSpec((1, D_MODEL), const),
                pl.BlockSpec(wog.shape, const), pl.BlockSpec(bog.shape, const),
                pl.BlockSpec(wum.shape, const), pl.BlockSpec(wua.shape, const),
                pl.BlockSpec(wout.shape, const), pl.BlockSpec(wr.shape, const),
                pl.BlockSpec(br.shape, const)]
    args = [x2, hm, on, mod3, gmix, gffn, wog, bog, wum, wua, wout, wr, br]
    kw = dict(tm=tm, t_mod=t_mod, t_valid=t_valid, m_valid=m_valid, n_real=nt if n_extra else None)
    body = functools.partial(_mixout_kernel, **kw)
    aliases = {}
    if shared is not None:
        in_specs += [pl.BlockSpec(memory_space=pl.ANY)] * len(shared)
        aliases = {len(args) + j: 1 + j for j in range(len(shared))}
        args += list(shared)
        body = functools.partial(_mixout_with_shared, n_shared=len(shared), **kw)
    return pl.pallas_call(
        body,
        grid=(nt + n_extra,),
        in_specs=in_specs,
        out_specs=[pl.BlockSpec((tm, D_MODEL), row),
                   pl.BlockSpec((MOE_RL, D_MODEL // 2), lambda i: (tile0 + i, 0)),
                   pl.BlockSpec((16, tm), lambda i: (0, tile0 + i)),
                   pl.BlockSpec((None, N_EXPERTS, LANES), lambda i: (tile0 + i, 0, 0))],
        out_shape=[jax.ShapeDtypeStruct((m, D_MODEL), F32),
                   jax.ShapeDtypeStruct((nt_total * MOE_RL, D_MODEL // 2), jnp.uint32),
                   jax.ShapeDtypeStruct((16, nt_total * tm), F32),
                   jax.ShapeDtypeStruct((nt_total, N_EXPERTS, LANES), F32)],
        input_output_aliases=aliases,
        compiler_params=_cparams(("arbitrary" if n_extra else "parallel",)),
        name="mixout",
    )(*args)


MOE_BM = 256
MOE_CH = 512


def _moe_kernel(be_ref, r0_ref, rows_ref, tf_ref, tl_ref, na_ref, n_ref, cs_ref, so_ref,
                xsl_ref, wgu_ref, bgu_ref, wdn_ref, bdn_ref, ysl_in_ref, ysl_ref,
                wgu_bf, wdn_bf, xbuf, ybuf, sem_in, sem_out):
    del ysl_in_ref
    i = pl.program_id(0)
    na = na_ref[0]
    e = be_ref[i]
    prev = be_ref[jnp.maximum(i - 1, 0)]

    def for_pieces(blk, fn):
        eb = be_ref[blk]
        r0 = r0_ref[blk]

        def body(t, c):
            idx = t * N_EXPERTS + eb
            cs = cs_ref[idx]
            lo = jnp.maximum(cs, r0)
            ln = jnp.maximum(jnp.minimum(cs + n_ref[idx], r0 + MOE_BM) - lo, 0)
            src = so_ref[idx] + (lo - cs)
            dst = lo - r0
            pos = jnp.int32(0)
            for bit in SEG_BITS:
                @pl.when((ln & bit) != 0)
                def _(pos=pos, bit=bit):
                    fn(pl.multiple_of(src + pos, SEG_ALIGN), pl.multiple_of(dst + pos, SEG_ALIGN), bit)
                pos = pos + (ln & bit)
            return c
        lax.fori_loop(tf_ref[blk], tl_ref[blk], body, 0)

    def start_gather(blk):
        slot = blk % 2
        for_pieces(blk, lambda s, d, nb: pltpu.make_async_copy(
            xsl_ref.at[pl.ds(s, nb), :], xbuf.at[slot, pl.ds(d, nb), :], sem_in.at[slot]).start())

    def start_scatter(blk):
        slot = blk % 2
        for_pieces(blk, lambda s, d, nb: pltpu.make_async_copy(
            ybuf.at[slot, pl.ds(d, nb), :], ysl_ref.at[pl.ds(s, nb), :], sem_out.at[slot]).start())

    def wait_rows(blk, sem, inbound):
        slot = blk % 2
        rows = rows_ref[blk]
        for bit in SEG_BITS:
            @pl.when((rows & bit) != 0)
            def _(bit=bit):
                if inbound:
                    pltpu.make_async_copy(xsl_ref.at[pl.ds(0, bit), :], xbuf.at[slot, pl.ds(0, bit), :],
                                          sem.at[slot]).wait()
                else:
                    pltpu.make_async_copy(ybuf.at[slot, pl.ds(0, bit), :], ysl_ref.at[pl.ds(0, bit), :],
                                          sem.at[slot]).wait()

    @pl.when(i == 0)
    def _():
        xbuf[...] = jnp.zeros(xbuf.shape, jnp.uint32)
        start_gather(i)

    @pl.when(i + 1 < na)
    def _():
        start_gather(i + 1)

    @pl.when((i < na) & ((i == 0) | (e != prev)))
    def _():
        for j in range(2 * D_EXPERT // MOE_CH):
            wgu_bf[:, j * MOE_CH:(j + 1) * MOE_CH] = wgu_ref[:, j * MOE_CH:(j + 1) * MOE_CH].astype(BF16)
        for j in range(D_EXPERT // MOE_CH):
            wdn_bf[j * MOE_CH:(j + 1) * MOE_CH, :] = wdn_ref[j * MOE_CH:(j + 1) * MOE_CH, :].astype(BF16)

    @pl.when(i < na)
    def _():
        slot = i % 2
        wait_rows(i, sem_in, True)

        @pl.when(i >= 2)
        def _():
            wait_rows(i - 2, sem_out, False)

        half = D_MODEL // 2
        xh, xl = _unpack_halves(xbuf[slot])

        def xdot(c0, c1):
            return (jnp.dot(xh, wgu_bf[0:half, c0:c1], preferred_element_type=F32)
                    + jnp.dot(xl, wgu_bf[half:D_MODEL, c0:c1], preferred_element_type=F32))

        acc = jnp.zeros((MOE_BM, D_MODEL), F32) + bdn_ref[...]
        for j in range(D_EXPERT // MOE_CH):
            lo, hi = j * MOE_CH, (j + 1) * MOE_CH
            gj = xdot(lo, hi) + bgu_ref[:, lo:hi]
            uj = xdot(D_EXPERT + lo, D_EXPERT + hi) + bgu_ref[:, D_EXPERT + lo:D_EXPERT + hi]
            gj = jnp.minimum(gj, SWIGLU_LIMIT)
            uj = jnp.clip(uj, -SWIGLU_LIMIT, SWIGLU_LIMIT)
            act = gj * _sigmoid(SWIGLU_ALPHA * gj) * (uj + 1.0)
            acc = acc + jnp.dot(act.astype(BF16), wdn_bf[lo:hi, :], preferred_element_type=F32)
        ybuf[slot] = _pack_halves(acc)
        start_scatter(i)

        @pl.when(i == na - 1)
        def _():
            @pl.when(i >= 1)
            def _():
                wait_rows(i - 1, sem_out, False)
            wait_rows(i, sem_out, False)


def _moe_experts(plan, xsl, w_gu, b_gu, w_dn, b_dn):
    block_e, block_r0, block_rows, tf, tl, n_active, seg_n, seg_cs, seg_so = plan
    nblk = block_e.shape[0]
    wmap = lambda i, be, *_: (be[i], 0, 0)
    anyspec = pl.BlockSpec(memory_space=pl.ANY)
    gs = pltpu.PrefetchScalarGridSpec(
        num_scalar_prefetch=9,
        grid=(nblk,),
        in_specs=[anyspec,
                  pl.BlockSpec((None, D_MODEL, 2 * D_EXPERT), wmap),
                  pl.BlockSpec((None, 1, 2 * D_EXPERT), wmap),
                  pl.BlockSpec((None, D_EXPERT, D_MODEL), wmap),
                  pl.BlockSpec((None, 1, D_MODEL), wmap),
                  anyspec],
        out_specs=anyspec,
        scratch_shapes=[pltpu.VMEM((D_MODEL, 2 * D_EXPERT), BF16), pltpu.VMEM((D_EXPERT, D_MODEL), BF16),
                        pltpu.VMEM((2, MOE_BM, D_MODEL // 2), jnp.uint32),
                        pltpu.VMEM((2, MOE_BM, D_MODEL // 2), jnp.uint32),
                        pltpu.SemaphoreType.DMA((2,)), pltpu.SemaphoreType.DMA((2,))],
    )
    return pl.pallas_call(
        _moe_kernel,
        grid_spec=gs,
        out_shape=jax.ShapeDtypeStruct(xsl.shape, jnp.uint32),
        input_output_aliases={14: 0},
        compiler_params=_cparams(("arbitrary",)),
        name="moe_experts",
    )(*plan, xsl, w_gu, b_gu.reshape(N_EXPERTS, 1, -1), w_dn, b_dn.reshape(N_EXPERTS, 1, -1),
      jnp.zeros(xsl.shape, jnp.uint32))


def _combine_kernel(ysl_ref, info_ref, x1_ref, mod_ref, y_ref, *, tm):
    info = info_ref[...]
    info_t = jnp.transpose(jnp.concatenate([info, jnp.zeros((LANES - info.shape[0], tm), F32)], axis=0))
    ridx = lax.broadcasted_iota(jnp.int32, (tm, MOE_RL), 1).astype(F32)
    pg = jnp.zeros((tm, MOE_RL), F32)
    for k in range(TOP_K):
        pg = pg + jnp.where(ridx == info_t[:, k:k + 1], info_t[:, TOP_K + k:TOP_K + k + 1], 0.0)
    pg_hi, pg_lo = _split(pg)
    yh, yl = _unpack_halves(ysl_ref[...])
    half = D_MODEL // 2
    gt2 = mod_ref[:, 5 * D_MODEL:6 * D_MODEL]
    for c, yy in ((0, yh), (1, yl)):
        moe = jnp.dot(pg_hi, yy, preferred_element_type=F32) + jnp.dot(pg_lo, yy, preferred_element_type=F32)
        y_ref[:, c * half:(c + 1) * half] = (x1_ref[:, c * half:(c + 1) * half]
                                             + gt2[:, c * half:(c + 1) * half] * moe)


def _combine(ysl, info, x1, mod3, tiles_per_mod, tile0=0):
    m = x1.shape[0]
    tm = MOE_TM
    r = mod3.shape[1]
    return pl.pallas_call(
        functools.partial(_combine_kernel, tm=tm),
        grid=(m // tm,),
        in_specs=[pl.BlockSpec((MOE_RL, D_MODEL // 2), lambda i: (tile0 + i, 0)),
                  pl.BlockSpec((16, tm), lambda i: (0, tile0 + i)),
                  pl.BlockSpec((tm, D_MODEL), lambda i: (i, 0)),
                  pl.BlockSpec((None, r, 6 * D_MODEL), lambda i: (i // tiles_per_mod, 0, 0))],
        out_specs=pl.BlockSpec((tm, D_MODEL), lambda i: (i, 0)),
        out_shape=jax.ShapeDtypeStruct((m, D_MODEL), F32),
        compiler_params=_cparams(("parallel",)),
        name="moe_combine",
    )(ysl, info, x1, mod3)


def _moe_plan(cnt):
    cnt = cnt.astype(jnp.int32)
    nt = cnt.shape[0]
    so = jnp.cumsum(cnt, axis=1) - cnt + (jnp.arange(nt) * MOE_RL)[:, None]
    ce = jnp.cumsum(cnt, axis=0)
    cs = ce - cnt
    tot = ce[-1]
    nblk_e = (tot + MOE_BM - 1) // MOE_BM
    blk_end = jnp.cumsum(nblk_e)
    max_rows = nt * MOE_TM * TOP_K + nt * N_EXPERTS * (SEG_ALIGN - 1)
    n_blocks = -(-max_rows // MOE_BM) + N_EXPERTS
    bidx = jnp.arange(n_blocks)
    block_e = jnp.minimum(jnp.sum(blk_end[None, :] <= bidx[:, None], axis=1), N_EXPERTS - 1).astype(jnp.int32)
    block_r0 = (bidx - (blk_end - nblk_e)[block_e]) * MOE_BM
    block_rows = jnp.clip(tot[block_e] - block_r0, 0, MOE_BM)
    ce_b = ce[:, block_e]
    cs_b = cs[:, block_e]
    tf = jnp.sum(ce_b <= block_r0[None, :], axis=0)
    tl = jnp.sum(cs_b < (block_r0 + MOE_BM)[None, :], axis=0)
    n_active = blk_end[-1].reshape(1)
    i32 = lambda a: a.reshape(-1).astype(jnp.int32)
    return (block_e, i32(block_r0), i32(block_rows), i32(tf), i32(tl), i32(n_active),
            i32(cnt), i32(cs), i32(so))


def _rope_tables(pos):
    half = ROT_DIM // 2
    inv = ROPE_THETA ** (-jnp.arange(half, dtype=F32) * (2.0 / ROT_DIM))
    ang = pos.astype(F32)[:, None] * inv[None, :]
    cos, sin = jnp.cos(ang), jnp.sin(ang)
    n = pos.shape[0]
    ones = jnp.ones((n, A_DH - ROT_DIM), F32)
    zeros_h = jnp.zeros((n, half), F32)
    zeros_r = jnp.zeros((n, A_DH - ROT_DIM), F32)
    cos64 = jnp.concatenate([cos, cos, ones], axis=1)
    sprev64 = jnp.concatenate([zeros_h, sin, zeros_r], axis=1)
    snext64 = jnp.concatenate([-sin, zeros_h, zeros_r], axis=1)
    two = lambda a: jnp.concatenate([a, a], axis=1)
    return two(cos64), two(sprev64), two(snext64)


def _prep_weights(w_in, b_in, q_norm_g, k_norm_g, cmp_pe_k, cmp_pe_v, cmp_w_k, cmp_w_v,
                  w_up_m, w_up_a, w_out, w_router, b_router):
    b2 = b_in.reshape(1, N_IN)
    wm = w_in[:, OFF_MQ:OFF_MO].astype(BF16)
    bm = b2[:, OFF_MQ:OFF_MO]
    wq = w_in[:, OFF_AQ:OFF_AKV].astype(BF16)
    bq = b2[:, OFF_AQ:OFF_AKV]
    wkv = w_in[:, OFF_AKV:OFF_AG].astype(BF16)
    bkv = b2[:, OFF_AKV:OFF_AG]
    n_small = 2 * M_HEADS + 3 * A_HEADS
    ws = jnp.concatenate([w_in[:, OFF_MI:OFF_AQ], w_in[:, OFF_AG:OFF_GA],
                          jnp.zeros((D_MODEL, LANES - n_small), F32)], axis=1)
    bs = jnp.concatenate([b2[:, OFF_MI:OFF_AQ], b2[:, OFF_AG:OFF_GA], jnp.zeros((1, LANES - n_small), F32)], axis=1)
    qg = jnp.tile(q_norm_g, A_HEADS).reshape(1, A_WIDTH)
    kg = jnp.stack([jnp.tile(k_norm_g[1], A_KV), jnp.tile(k_norm_g[2], A_KV)], axis=0)
    kg0 = jnp.tile(k_norm_g[0], A_KV).reshape(1, LANES)
    hid = jnp.arange(A_WIDTH) // A_DH
    bd = jnp.where(hid[:, None] == hid[None, :], 1.0 / A_DH, 0.0).astype(BF16)
    inproj_w = (wm, bm, wq, bq, wkv, bkv, ws, bs, qg, kg, bd)

    z = jnp.zeros((CMP_LEN, A_DH, A_DH), F32)
    r0 = jnp.concatenate([cmp_w_k, z, z, z], axis=2)
    r1 = jnp.concatenate([z, cmp_w_k, z, z], axis=2)
    r2 = jnp.concatenate([z, z, cmp_w_v, z], axis=2)
    r3 = jnp.concatenate([z, z, z, cmp_w_v], axis=2)
    wbd = jnp.concatenate([r0, r1, r2, r3], axis=1).astype(BF16)
    pe = jnp.concatenate([cmp_pe_k, cmp_pe_k, cmp_pe_v, cmp_pe_v], axis=1).reshape(CMP_LEN, 1, 2 * LANES)

    wog = jnp.concatenate([w_in[:, OFF_MO:OFF_MI], w_in[:, OFF_GA:N_IN]], axis=1).astype(BF16)
    bog = jnp.concatenate([b2[:, OFF_MO:OFF_MI], b2[:, OFF_GA:N_IN]], axis=1)
    mixout_w = (wog, bog, w_up_m.astype(BF16), w_up_a.astype(BF16), w_out.astype(BF16),
                w_router.T, b_router.reshape(N_EXPERTS, 1))
    return inproj_w, (wbd, pe, kg0), mixout_w


def _pick_tile(m, pref):
    t = pref
    while m % t:
        t //= 2
    return t


def kernel(x_prompt, x_sample, cache_nsa_kv, state_win_kv, state_mlstm_C, state_mlstm_n, state_mlstm_m, page_table, c_prompt, c_sample, w_ada, b_ada, g_mix, g_ffn, w_in, b_in, q_norm_g, k_norm_g, cmp_pe_k, cmp_pe_v, cmp_w_k, cmp_w_v, w_up_m, w_up_a, w_out, w_router, b_router, w_gu, b_gu, w_dn, b_dn):
    depth = w_in.shape[0]
    assert depth == 1
    B, T, D = x_prompt.shape
    DB, TS, _ = x_sample.shape
    n_pages = page_table.shape[1]
    past_len = n_pages * PAGE_SIZE
    wbuf = state_win_kv.shape[2]
    tp = SAMPLE_PAD_T
    assert TS <= tp and wbuf % tp == 0 and T % 128 == 0

    l = 0
    inproj_w, cmp_w, mixout_w = _prep_weights(
        w_in[l], b_in[l], q_norm_g[l], k_norm_g[l], cmp_pe_k[l], cmp_pe_v[l], cmp_w_k[l], cmp_w_v[l],
        w_up_m[l], w_up_a[l], w_out[l], w_router[l], b_router[l])
    wbd, pe, kg0 = cmp_w
    gmix = g_mix[l].reshape(1, D)
    gffn = g_ffn[l].reshape(1, D)

    nc = B + DB
    nc_pad = -(-nc // SUBLANES) * SUBLANES
    c_all = jnp.concatenate([c_prompt, c_sample, jnp.zeros((nc_pad - nc, D), F32)], axis=0)
    mod = _adaln(c_all, w_ada[l], b_ada[l])
    mod_p = mod[:B].reshape(B, 1, 6 * D)
    mod_s = jnp.repeat(mod[B:B + DB], tp, axis=0).reshape(1, DB * tp, 6 * D)

    mp = B * T
    tm = _pick_tile(T, 256)
    xp = x_prompt.reshape(mp, D)
    tabs_p = _rope_tables(jnp.arange(T, dtype=jnp.int32))
    mq, mk, mv, q, qr, rows, win, small, rows_t = _inproj(xp, mod_p, gmix, tabs_p, inproj_w, tm, T // tm, T // tm,
                                                          rows_t_batches=B)
    Lp = _pick_tile(T, 128)
    hm, C_p, n_p, m_p = _mlstm(mq, mk, mv, small, B, T, T, Lp)
    o_nsa = _nsa_prompt(q, qr, small, rows, win, wbd, pe, kg0, B, T)
    assert T % MOE_TM == 0
    ms_pad = -(-(DB * tp) // MOE_TM) * MOE_TM
    nt_p = mp // MOE_TM
    nt_all = nt_p + ms_pad // MOE_TM
    x1_p, xsl, info, cnt = _mixout(xp, hm, o_nsa, mod_p, gmix, gffn, mixout_w, T // MOE_TM, nt_all)

    ms = DB * tp
    xs_pad = jnp.concatenate([x_sample, jnp.zeros((DB, tp - TS, D), F32)], axis=1).reshape(ms, D)
    pos_s = past_len + jnp.tile(jnp.arange(tp, dtype=jnp.int32), DB)
    tabs_s = _rope_tables(pos_s)
    mq_s, mk_s, mv_s, q_s, qr_s, rows_s, win_s, small_s = _inproj(xs_pad, mod_s, gmix, tabs_s, inproj_w, ms, 1, 1)
    hm_s, C_s, n_s, m_s = _mlstm(mq_s, mk_s, mv_s, small_s, DB, tp, TS, tp,
                                 state=(state_mlstm_C[l], state_mlstm_n[l], state_mlstm_m[l]))
    cache2 = jnp.transpose(cache_nsa_kv[l], (0, 2, 3, 4, 1)).reshape(cache_nsa_kv.shape[1], 4 * LANES, PAGE_SIZE)
    winbuf = state_win_kv[l].reshape(DB, wbuf, 2 * LANES)
    o_nsa_s, win_out_s = _nsa_sample(page_table, cache2, q_s, qr_s, small_s, rows_s, win_s, winbuf,
                                     wbd, pe, kg0, TS)
    assert ms_pad == MOE_TM
    rpad = lambda a: jnp.concatenate([a, jnp.zeros((ms_pad - ms, a.shape[1]), a.dtype)], axis=0) if ms_pad > ms else a
    mod_sp = rpad(mod_s[0])[None]
    x1_s, xsl, info, cnt = _mixout(rpad(xs_pad), rpad(hm_s), rpad(o_nsa_s), mod_sp, gmix, gffn, mixout_w,
                                   1, nt_all, tile0=nt_p, shared=(xsl, info, cnt),
                                   t_mod=tp, t_valid=TS, m_valid=ms)

    ysl = _moe_experts(_moe_plan(cnt[:, :, 0]), xsl, w_gu[l], b_gu[l], w_dn[l], b_dn[l])
    y_p = _combine(ysl, info, x1_p, mod_p, T // MOE_TM).reshape(B, T, D)
    y_s_all = _combine(ysl, info, x1_s, mod_sp, 1, tile0=nt_p)
    valid = lambda a: a.reshape(DB, tp, -1)[:, :TS].reshape(DB * TS, -1)
    y_s = valid(y_s_all[:ms]).reshape(DB, TS, D)

    kv_p = jnp.transpose(rows_t.reshape(B, 4, A_KV, A_DH, T), (0, 4, 1, 2, 3))[None]
    kv_s = valid(rows_s).reshape(1, DB, TS, 4, A_KV, A_DH)
    wp = min(WINDOW, T)
    win_p = win.reshape(B, T, 2, A_KV, A_DH)[:, T - wp:][None]
    win_s_out = win_out_s.reshape(1, DB, wbuf, 2, A_KV, A_DH)
    return (y_p, y_s, kv_p, kv_s, win_p, win_s_out,
            C_p[None], n_p[None], m_p[None], C_s[None], n_s[None], m_s[None])
```

```python
import functools
import math

import jax
import jax.numpy as jnp
from jax import lax
from jax.experimental import pallas as pl
from jax.experimental.pallas import tpu as pltpu

F32 = jnp.float32
BF16 = jnp.bfloat16

D_MODEL = 1024
M_HEADS = 4
M_DH = 128
M_WIDTH = M_HEADS * M_DH
A_HEADS = 8
A_KV = 2
A_HPG = A_HEADS // A_KV
A_DH = 64
A_WIDTH = A_HEADS * A_DH
CMP_STRIDE = 16
CMP_LEN = 32
SEL_LEN = 64
N_SEL = 16
WINDOW = 512
PAGE_SIZE = 128
ROPE_THETA = 500000.0
ROT_DIM = A_DH // 4
ATT_SCALE = A_DH ** -0.5
N_EXPERTS = 32
TOP_K = 4
D_EXPERT = D_MODEL
SWIGLU_LIMIT = 7.0
SWIGLU_ALPHA = 1.702
EPS = 1e-6

OFF_MQ, OFF_MK, OFF_MV, OFF_MO = 0, M_WIDTH, 2 * M_WIDTH, 3 * M_WIDTH
OFF_MI = 4 * M_WIDTH
OFF_MF = OFF_MI + M_HEADS
OFF_AQ = OFF_MF + M_HEADS
OFF_AKV = OFF_AQ + A_WIDTH
OFF_AG = OFF_AKV + 6 * A_KV * A_DH
OFF_GA = OFF_AG + 3 * A_HEADS
OFF_GB = OFF_GA + D_MODEL
N_IN = OFF_GB + D_MODEL

LANES = 128
SUBLANES = 8
VMEM_LIMIT = 56 * 1024 * 1024

NEG_BIG = -1e30
M_INIT = -1e29
LOG2E = 1.4426950408889634
SAMPLE_PAD_T = 8


def _cparams(sem):
    return pltpu.CompilerParams(dimension_semantics=sem, vmem_limit_bytes=VMEM_LIMIT)


def _bdot(a, b):
    return jnp.dot(a.astype(BF16), b.astype(BF16), preferred_element_type=F32)


def _bdot_t(a, b):
    return lax.dot_general(a.astype(BF16), b.astype(BF16), (((1,), (1,)), ((), ())),
                           preferred_element_type=F32)


def _split(a):
    hi = a.astype(BF16)
    lo = (a - hi.astype(F32)).astype(BF16)
    return hi, lo


def _dot3(a, b):
    ah, al = _split(a)
    bh, bl = _split(b)
    return (jnp.dot(ah, bh, preferred_element_type=F32) + jnp.dot(al, bh, preferred_element_type=F32)
            + jnp.dot(ah, bl, preferred_element_type=F32))


def _dot2_exact_rhs(a, b_bf16):
    ah, al = _split(a)
    return jnp.dot(ah, b_bf16, preferred_element_type=F32) + jnp.dot(al, b_bf16, preferred_element_type=F32)


def _sigmoid(x):
    return 1.0 / (1.0 + jnp.exp(-x))


def _rmsnorm_rows(x, g):
    return x * lax.rsqrt(jnp.mean(x * x, axis=-1, keepdims=True) + EPS) * g


def _adaln_kernel(c_ref, w_ref, b_ref, o_ref):
    c = c_ref[...]
    s = c * _sigmoid(c)
    o_ref[...] = _dot3(s, w_ref[...]) + b_ref[...]


def _adaln(c, w, b):
    mc, d = c.shape
    n = w.shape[1]
    tn = 1024
    return pl.pallas_call(
        _adaln_kernel,
        grid=(n // tn,),
        in_specs=[pl.BlockSpec((mc, d), lambda j: (0, 0)),
                  pl.BlockSpec((d, tn), lambda j: (0, j)),
                  pl.BlockSpec((1, tn), lambda j: (0, j))],
        out_specs=pl.BlockSpec((mc, tn), lambda j: (0, j)),
        out_shape=jax.ShapeDtypeStruct((mc, n), F32),
        compiler_params=_cparams(("parallel",)),
        name="adaln",
    )(c, w, b.reshape(1, n))


def _head_norm(z, bd, gain):
    ms = _dot2_exact_rhs(z * z, bd)
    return z * lax.rsqrt(ms + EPS) * gain


def _rope(z, cos, s_prev, s_next):
    w = z.shape[1]
    rep = w // LANES
    if rep > 1:
        cos = jnp.concatenate([cos] * rep, axis=1)
        s_prev = jnp.concatenate([s_prev] * rep, axis=1)
        s_next = jnp.concatenate([s_next] * rep, axis=1)
    z_prev = pltpu.roll(z, ROT_DIM // 2, 1)
    z_next = pltpu.roll(z, w - ROT_DIM // 2, 1)
    return z * cos + z_prev * s_prev + z_next * s_next


def _inproj_kernel(x_ref, mod_ref, gmix_ref, cos_ref, sp_ref, sn_ref,
                   wm_ref, bm_ref, wq_ref, bq_ref, wkv_ref, bkv_ref, ws_ref, bs_ref,
                   qg_ref, kg_ref, bd_ref,
                   mq_ref, mk_ref, mv_ref, q_ref, qr_ref, rows_ref, win_ref, small_ref, rows_t_ref=None):
    x = x_ref[...]
    sh1 = mod_ref[:, 0:D_MODEL]
    sc1 = mod_ref[:, D_MODEL:2 * D_MODEL]
    h = _rmsnorm_rows(x, gmix_ref[...]) * (1.0 + sc1) + sh1
    hb = h.astype(BF16)

    mq_ref[...] = jnp.dot(hb, wm_ref[:, 0:M_WIDTH], preferred_element_type=F32) + bm_ref[:, 0:M_WIDTH]
    mk = jnp.dot(hb, wm_ref[:, M_WIDTH:2 * M_WIDTH], preferred_element_type=F32) + bm_ref[:, M_WIDTH:2 * M_WIDTH]
    mk_ref[...] = mk * (M_DH ** -0.5)
    mv_ref[...] = (jnp.dot(hb, wm_ref[:, 2 * M_WIDTH:3 * M_WIDTH], preferred_element_type=F32)
                   + bm_ref[:, 2 * M_WIDTH:3 * M_WIDTH])

    cos, sp, sn = cos_ref[...], sp_ref[...], sn_ref[...]
    zq = jnp.dot(hb, wq_ref[...], preferred_element_type=F32) + bq_ref[...]
    qn = _head_norm(zq, bd_ref[...], qg_ref[...])
    q_ref[...] = qn
    qr_ref[...] = _rope(qn, cos, sp, sn)

    zkv = jnp.dot(hb, wkv_ref[...], preferred_element_type=F32) + bkv_ref[...]
    bd2 = bd_ref[0:LANES, 0:LANES]
    ksel = _head_norm(zkv[:, 2 * LANES:3 * LANES], bd2, kg_ref[0:1, :])
    rows = jnp.concatenate([zkv[:, 0:2 * LANES], _rope(ksel, cos, sp, sn), zkv[:, 3 * LANES:4 * LANES]], axis=1)
    rows_ref[...] = rows
    if rows_t_ref is not None:
        rows_t_ref[...] = jnp.transpose(rows)
    kwin = _head_norm(zkv[:, 4 * LANES:5 * LANES], bd2, kg_ref[1:2, :])
    win_ref[:, 0:LANES] = _rope(kwin, cos, sp, sn)
    win_ref[:, LANES:2 * LANES] = zkv[:, 5 * LANES:6 * LANES]

    small_ref[...] = _dot3(h, ws_ref[...]) + bs_ref[...]


def _inproj(x2, mod3, gmix, tabs, wts, tm, tiles_per_mod, pos_tiles, rows_t_batches=None):
    m = x2.shape[0]
    cos_t, sp_t, sn_t = tabs
    (wm, bm, wq, bq, wkv, bkv, ws, bs, qg, kg, bd) = wts
    r = mod3.shape[1]
    row = lambda i: (i, 0)
    const = lambda i: (0, 0)
    tab = lambda i: (i % pos_tiles, 0)
    in_specs = [
        pl.BlockSpec((tm, D_MODEL), row),
        pl.BlockSpec((None, r, 6 * D_MODEL), lambda i: (i // tiles_per_mod, 0, 0)),
        pl.BlockSpec((1, D_MODEL), const),
        pl.BlockSpec((tm, LANES), tab), pl.BlockSpec((tm, LANES), tab), pl.BlockSpec((tm, LANES), tab),
        pl.BlockSpec(wm.shape, const), pl.BlockSpec(bm.shape, const),
        pl.BlockSpec(wq.shape, const), pl.BlockSpec(bq.shape, const),
        pl.BlockSpec(wkv.shape, const), pl.BlockSpec(bkv.shape, const),
        pl.BlockSpec(ws.shape, const), pl.BlockSpec(bs.shape, const),
        pl.BlockSpec(qg.shape, const), pl.BlockSpec(kg.shape, const), pl.BlockSpec(bd.shape, const),
    ]
    widths = (M_WIDTH, M_WIDTH, M_WIDTH, A_WIDTH, A_WIDTH, 4 * LANES, 2 * LANES, LANES)
    out_specs = [pl.BlockSpec((tm, w), row) for w in widths]
    out_shape = [jax.ShapeDtypeStruct((m, w), F32) for w in widths]
    if rows_t_batches is not None:
        out_specs.append(pl.BlockSpec((None, 4 * LANES, tm), lambda i: (i // tiles_per_mod, 0, i % tiles_per_mod)))
        out_shape.append(jax.ShapeDtypeStruct((rows_t_batches, 4 * LANES, m // rows_t_batches), F32))
    return pl.pallas_call(
        _inproj_kernel,
        grid=(m // tm,),
        in_specs=in_specs,
        out_specs=out_specs,
        out_shape=out_shape,
        compiler_params=_cparams(("parallel",)),
        name="inproj",
    )(x2, mod3, gmix, cos_t, sp_t, sn_t, wm, bm, wq, bq, wkv, bkv, ws, bs, qg, kg, bd)


def _log_sigmoid(x):
    return jnp.minimum(x, 0.0) - jnp.log(1.0 + jnp.exp(-jnp.abs(x)))


def _mlstm_kernel(*refs, L, t_valid, has_state):
    if has_state:
        q_ref, k_ref, v_ref, s_ref, c0_ref, n0_ref, m0_ref, h_ref, c_ref, n_ref, m_ref = refs
    else:
        q_ref, k_ref, v_ref, s_ref, h_ref, c_ref, n_ref, m_ref = refs
    c = pl.program_id(1)

    @pl.when(c == 0)
    def _():
        if has_state:
            c_ref[...] = c0_ref[...]
            n_ref[...] = n0_ref[...]
            m_ref[...] = m0_ref[...]
        else:
            c_ref[...] = jnp.zeros(c_ref.shape, F32)
            n_ref[...] = jnp.zeros(n_ref.shape, F32)
            m_ref[...] = jnp.zeros(m_ref.shape, F32)

    row = lax.broadcasted_iota(jnp.int32, (L, L), 0)
    col = lax.broadcasted_iota(jnp.int32, (L, L), 1)
    causal = col <= row
    eye = col == row
    tok_col = c * L + lax.broadcasted_iota(jnp.int32, (L, 1), 0)
    valid_col = tok_col < t_valid
    for hd in range(M_HEADS):
        lo, hi = hd * M_DH, (hd + 1) * M_DH
        q = q_ref[:, lo:hi]
        k = k_ref[:, lo:hi]
        v = v_ref[:, lo:hi]
        i_col = s_ref[:, hd:hd + 1]
        lf_col = _log_sigmoid(s_ref[:, M_HEADS + hd:M_HEADS + hd + 1])
        lf_col = jnp.where(valid_col, lf_col, 0.0)
        i_col = jnp.where(valid_col, i_col, -jnp.inf)
        i_row = jnp.sum(jnp.where(eye, i_col, 0.0), axis=0, keepdims=True)
        lf_row = jnp.sum(jnp.where(eye, lf_col, 0.0), axis=0, keepdims=True)
        b_col = jnp.sum(jnp.where(causal, lf_row, 0.0), axis=1, keepdims=True)
        b_row = jnp.sum(jnp.where(row <= col, lf_col, 0.0), axis=0, keepdims=True)
        m_prev = m_ref[:, hd:hd + 1]
        dmat = jnp.where(causal, b_col - b_row + i_row, -jnp.inf)
        inter = b_col + m_prev
        m_row = jnp.maximum(jnp.max(dmat, axis=1, keepdims=True), inter)
        w = jnp.exp(dmat - m_row)
        w_inter = jnp.exp(inter - m_row)
        s = _bdot_t(q, k) * w
        cm = c_ref[hd]
        nv = n_ref[hd]
        num = _bdot(s, v) + w_inter * _bdot_t(q, cm)
        den = jnp.sum(s, axis=1, keepdims=True) + w_inter * jnp.sum(q * nv, axis=1, keepdims=True)
        h_ref[:, lo:hi] = num / jnp.maximum(jnp.abs(den), jnp.exp(-m_row))
        b_last = b_col[L - 1:L, :]
        dec_col = b_last - b_col + i_col
        dec_row = b_last - b_row + i_row
        m_new = jnp.maximum(b_last + m_prev, jnp.max(dec_row, axis=1, keepdims=True))
        ws_col = jnp.exp(dec_col - m_new)
        wc = jnp.exp(b_last + m_prev - m_new)
        vw = (v * ws_col).astype(BF16)
        upd = lax.dot_general(vw, k.astype(BF16), (((0,), (0,)), ((), ())), preferred_element_type=F32)
        c_ref[hd] = wc * cm + upd
        n_ref[hd] = wc * nv + jnp.sum(k * ws_col, axis=0, keepdims=True)
        m_ref[:, hd:hd + 1] = m_new


def _mlstm(mq, mk, mv, small, nb, t_pad, t_valid, L, state=None):
    nc = t_pad // L
    has_state = state is not None
    blk = lambda b, c: (b * nc + c, 0)
    st4 = lambda b, c: (b, 0, 0, 0)
    st3 = lambda b, c: (b, 0, 0)
    in_specs = [pl.BlockSpec((L, M_WIDTH), blk)] * 3 + [pl.BlockSpec((L, LANES), blk)]
    args = [mq, mk, mv, small]
    if has_state:
        c0, n0, m0 = state
        in_specs += [pl.BlockSpec((None, M_HEADS, M_DH, M_DH), st4),
                     pl.BlockSpec((None, M_HEADS, 1, M_DH), st4),
                     pl.BlockSpec((None, 1, M_HEADS), st3)]
        args += [c0, n0.reshape(nb, M_HEADS, 1, M_DH), m0.reshape(nb, 1, M_HEADS)]
    out_specs = [pl.BlockSpec((L, M_WIDTH), blk),
                 pl.BlockSpec((None, M_HEADS, M_DH, M_DH), st4),
                 pl.BlockSpec((None, M_HEADS, 1, M_DH), st4),
                 pl.BlockSpec((None, 1, M_HEADS), st3)]
    out_shape = [jax.ShapeDtypeStruct((nb * t_pad, M_WIDTH), F32),
                 jax.ShapeDtypeStruct((nb, M_HEADS, M_DH, M_DH), F32),
                 jax.ShapeDtypeStruct((nb, M_HEADS, 1, M_DH), F32),
                 jax.ShapeDtypeStruct((nb, 1, M_HEADS), F32)]
    h, cs, ns, ms = pl.pallas_call(
        functools.partial(_mlstm_kernel, L=L, t_valid=t_valid, has_state=has_state),
        grid=(nb, nc),
        in_specs=in_specs,
        out_specs=out_specs,
        out_shape=out_shape,
        compiler_params=_cparams(("parallel", "arbitrary")),
        name="mlstm",
    )(*args)
    return h, cs, ns.reshape(nb, M_HEADS, M_DH), ms.reshape(nb, M_HEADS)


def _stack_heads(qt, g):
    t = qt.shape[0]
    z = jnp.zeros((t, A_DH), F32)
    parts = []
    for hh in range(A_HPG):
        hd = g * A_HPG + hh
        qh = qt[:, hd * A_DH:(hd + 1) * A_DH] * (ATT_SCALE * LOG2E)
        parts.append(jnp.concatenate([qh, z], axis=1) if g == 0 else jnp.concatenate([z, qh], axis=1))
    return jnp.concatenate(parts, axis=0).astype(BF16)


def _gate_cols(small, g, br):
    cols = []
    for hh in range(A_HPG):
        c0 = 2 * M_HEADS + (g * A_HPG + hh) * 3 + br
        cols.append(_sigmoid(small[:, c0:c0 + 1]))
    return jnp.concatenate(cols, axis=0)


def _compress(k_ref, v_ref, nseg, wbd_ref, pe_ref, kg0):
    acc_lo = jnp.zeros((nseg, 2 * LANES), F32)
    acc_hi = jnp.zeros((nseg, 2 * LANES), F32)
    for l in range(CMP_STRIDE):
        xl = jnp.concatenate([k_ref[pl.ds(l, nseg, stride=CMP_STRIDE), :],
                              v_ref[pl.ds(l, nseg, stride=CMP_STRIDE), :]], axis=1)
        acc_lo = acc_lo + _bdot(xl + pe_ref[l], wbd_ref[l])
        acc_hi = acc_hi + _bdot(xl + pe_ref[CMP_STRIDE + l], wbd_ref[CMP_STRIDE + l])
    return _compress_finish(acc_lo, acc_hi, nseg, kg0)


def _compress_grouped(x_ref, nseg, wbd_ref, pe_ref, kg0):
    acc_lo = jnp.zeros((nseg, 2 * LANES), F32)
    acc_hi = jnp.zeros((nseg, 2 * LANES), F32)
    pe_lo = jnp.zeros((SUBLANES, 2 * LANES), F32)
    pe_hi = jnp.zeros((SUBLANES, 2 * LANES), F32)
    for l in range(CMP_STRIDE):
        xl = x_ref[l].astype(BF16)
        acc_lo = acc_lo + jnp.dot(xl, wbd_ref[l], preferred_element_type=F32)
        acc_hi = acc_hi + jnp.dot(xl, wbd_ref[CMP_STRIDE + l], preferred_element_type=F32)
        pe_lo = pe_lo + _bdot(jnp.broadcast_to(pe_ref[l], (SUBLANES, 2 * LANES)), wbd_ref[l])
        pe_hi = pe_hi + _bdot(jnp.broadcast_to(pe_ref[CMP_STRIDE + l], (SUBLANES, 2 * LANES)),
                              wbd_ref[CMP_STRIDE + l])
    return _compress_finish(acc_lo + pe_lo[0:1, :], acc_hi + pe_hi[0:1, :], nseg, kg0)


def _compress_finish(acc_lo, acc_hi, nseg, kg0):
    kv = acc_lo + pltpu.roll(acc_hi, nseg - 1, 0)
    kc = kv[:, 0:LANES]
    vc = kv[:, LANES:2 * LANES]
    lane = lax.broadcasted_iota(jnp.int32, (nseg, LANES), 1)
    sq = kc * kc
    ms0 = jnp.sum(jnp.where(lane < A_DH, sq, 0.0), axis=1, keepdims=True) * (1.0 / A_DH)
    ms1 = jnp.sum(jnp.where(lane >= A_DH, sq, 0.0), axis=1, keepdims=True) * (1.0 / A_DH)
    ms = jnp.where(lane < A_DH, ms0, ms1)
    kc = kc * lax.rsqrt(ms + EPS) * kg0
    return kc, vc


def _cmp_branch(qn_g, kc_b, vc_b, tpos_rows, nseg, n_tok):
    s = _bdot_t(qn_g, kc_b)
    nidx = lax.broadcasted_iota(jnp.int32, (1, nseg), 1)
    vis = (nidx * CMP_STRIDE + (CMP_LEN - 1)) <= tpos_rows
    sm = jnp.where(vis, s, NEG_BIG)
    mx = jnp.max(sm, axis=1, keepdims=True)
    e = jnp.where(vis, jnp.exp2(sm - mx), 0.0)
    d = jnp.sum(e, axis=1, keepdims=True)
    p = e / jnp.where(d > 0, d, 1.0)
    o = _bdot(p, vc_b)
    imp = p[0:n_tok]
    for hh in range(1, A_HPG):
        imp = imp + p[hh * n_tok:(hh + 1) * n_tok]
    return o, imp


def _masked_attn_direct(q_g, k_parts, v_parts, allowed_parts, feature_major):
    ss = [jnp.where(al, _bdot(q_g, kk) if fm else _bdot_t(q_g, kk), NEG_BIG)
          for kk, al, fm in zip(k_parts, allowed_parts, feature_major)]
    mx = ss[0].max(axis=1, keepdims=True)
    for s in ss[1:]:
        mx = jnp.maximum(mx, s.max(axis=1, keepdims=True))
    num = None
    den = None
    for s, al, vv, fm in zip(ss, allowed_parts, v_parts, feature_major):
        e = jnp.where(al, jnp.exp2(s - mx), 0.0)
        dd = jnp.sum(e, axis=1, keepdims=True)
        oo = _bdot_t(e, vv) if fm else _bdot(e, vv)
        num = oo if num is None else num + oo
        den = dd if den is None else den + dd
    return num / jnp.where(den > 0, den, 1.0)


def _assemble_heads(o_groups, n_tok):
    pieces = []
    for g in range(A_KV):
        for hh in range(A_HPG):
            pieces.append(o_groups[g][hh * n_tok:(hh + 1) * n_tok, g * A_DH:(g + 1) * A_DH])
    return jnp.concatenate(pieces, axis=1)


def _lane_tile(a, width):
    rep = width // LANES
    return a if rep == 1 else jnp.concatenate([a] * rep, axis=1)


def _add_bias(s, bias):
    t, k = bias.shape
    return (s.reshape(A_HPG, t, k) + bias[None]).reshape(A_HPG * t, k)


def _nsa_prompt_kernel(q_ref, qr_ref, small_ref, rows_ref, win_ref, wbd_ref, pe_ref, kg0_ref,
                       pool_ref, o_ref,
                       kraw_sc, vraw_sc, kc_sc, vc_sc, m_sc, l_sc, acc_sc, *, T, tq, kc_len):
    qi = pl.program_id(1)
    nseg = T // CMP_STRIDE
    nsb = T // SEL_LEN

    @pl.when(qi == 0)
    def _():
        kraw_sc[...] = rows_ref[:, 0:LANES]
        vraw_sc[...] = rows_ref[:, LANES:2 * LANES]
        kc, vc = _compress(kraw_sc, vraw_sc, nseg, wbd_ref, pe_ref, kg0_ref[...])
        kc_sc[...] = kc
        vc_sc[...] = vc

    t0 = qi * tq
    tpos_col = t0 + lax.broadcasted_iota(jnp.int32, (tq, 1), 0)
    tpos_rows = jnp.concatenate([tpos_col] * A_HPG, axis=0)
    tpos_lane = t0 + lax.broadcasted_iota(jnp.int32, (1, tq), 1)
    q = q_ref[...]
    qr = qr_ref[...]
    small = small_ref[...]
    kc_b = kc_sc[...].astype(BF16)
    vc_b = vc_sc[...].astype(BF16)
    bidx = lax.broadcasted_iota(jnp.int32, (nsb, tq), 0)
    cur = tpos_lane // SEL_LEN
    r4 = A_HPG * tq
    qr_gs = [_stack_heads(qr, g) for g in range(A_KV)]
    o_cmps = []
    sel_bs = []
    for g in range(A_KV):
        qn_g = _stack_heads(q, g)
        o_cmp, imp = _cmp_branch(qn_g, kc_b, vc_b, tpos_rows, nseg, tq)
        o_cmps.append(o_cmp)
        imp_sel = _dot2_exact_rhs(imp, pool_ref[...])
        imp_t = jnp.transpose(imp_sel)[0:nsb, :]
        val = jnp.where(bidx < cur, imp_t, -1.0)
        rank = jnp.zeros((nsb, tq), F32)
        for bp in range(nsb):
            vb = val[bp:bp + 1, :]
            ahead = jnp.where(vb > val, 1.0, jnp.where((vb == val) & (bidx > bp), 1.0, 0.0))
            rank = rank + ahead
        sel_t = jnp.where(((rank < (N_SEL - 1)) & (bidx < cur)) | (bidx == cur), 1.0, 0.0)
        if nsb < LANES:
            sel_t = jnp.concatenate([sel_t, jnp.zeros((LANES - nsb, tq), F32)], axis=0)
        sel_bs.append(jnp.transpose(sel_t).astype(BF16))

    m_sc[...] = jnp.full(m_sc.shape, M_INIT, F32)
    l_sc[...] = jnp.zeros(l_sc.shape, F32)
    acc_sc[...] = jnp.zeros(acc_sc.shape, F32)

    def sel_body(c, carry):
        k0 = pl.multiple_of(c * kc_len, kc_len)
        kb = rows_ref[pl.ds(k0, kc_len), 2 * LANES:3 * LANES].astype(BF16)
        vb = rows_ref[pl.ds(k0, kc_len), 3 * LANES:4 * LANES].astype(BF16)
        kpos = k0 + lax.broadcasted_iota(jnp.int32, (1, kc_len), 1)
        causal = kpos <= tpos_col
        kblk = (k0 + lax.broadcasted_iota(jnp.int32, (LANES, kc_len), 1)) // SEL_LEN
        expand = jnp.where(kblk == lax.broadcasted_iota(jnp.int32, (LANES, kc_len), 0), 1.0, 0.0).astype(BF16)
        for g in range(A_KV):
            mk = jnp.dot(sel_bs[g], expand, preferred_element_type=F32)
            bias = jnp.where(causal, (mk - 1.0) * (-NEG_BIG), NEG_BIG)
            sm = _add_bias(_bdot_t(qr_gs[g], kb), bias)
            m_prev = m_sc[g]
            m_new = jnp.maximum(m_prev, jnp.max(sm, axis=1, keepdims=True))
            alpha = jnp.exp2(m_prev - m_new)
            p = jnp.exp2(sm - _lane_tile(m_new, kc_len))
            l_sc[g] = alpha * l_sc[g] + jnp.sum(p, axis=1, keepdims=True)
            acc_sc[g] = alpha * acc_sc[g] + _bdot(p, vb)
            m_sc[g] = m_new
        return carry

    lax.fori_loop(0, (t0 + tq + kc_len - 1) // kc_len, sel_body, 0)

    wk = min(WINDOW + tq, T)
    w0 = pl.multiple_of(jnp.clip(t0 + tq - wk, 0, T - wk), tq)
    kw = win_ref[pl.ds(w0, wk), 0:LANES].astype(BF16)
    vw = win_ref[pl.ds(w0, wk), LANES:2 * LANES].astype(BF16)
    wdiff = tpos_col - (w0 + lax.broadcasted_iota(jnp.int32, (1, wk), 1))
    wbias = jnp.where((wdiff >= 0) & (wdiff < WINDOW), 0.0, NEG_BIG)

    o_groups = []
    for g in range(A_KV):
        l = l_sc[g]
        o_sel = acc_sc[g] / jnp.where(l > 0, l, 1.0)
        sw = _add_bias(_bdot_t(qr_gs[g], kw), wbias)
        mw = jnp.broadcast_to(jnp.max(sw, axis=1, keepdims=True), (r4, LANES))
        pw = jnp.exp2(sw - _lane_tile(mw, wk))
        o_win = _bdot(pw, vw) / jnp.broadcast_to(jnp.sum(pw, axis=1, keepdims=True), (r4, LANES))
        o_groups.append(_gate_cols(small, g, 0) * o_cmps[g] + _gate_cols(small, g, 1) * o_sel
                        + _gate_cols(small, g, 2) * o_win)
    o_ref[...] = _assemble_heads(o_groups, tq)


def _nsa_prompt(q, qr, small, rows, win, wbd, pe, kg0, nb, T):
    tq = 128
    kc_len = _pick_tile(T, 512)
    nq = T // tq
    nseg = T // CMP_STRIDE
    nsb = T // SEL_LEN
    pool = (jnp.arange(nseg)[:, None] // (SEL_LEN // CMP_STRIDE) == jnp.arange(LANES)[None, :]).astype(BF16)
    tile = lambda b, i: (b * nq + i, 0)
    per_b = lambda b, i: (b, 0)
    c2 = lambda b, i: (0, 0)
    c3 = lambda b, i: (0, 0, 0)
    r4 = A_HPG * tq
    return pl.pallas_call(
        functools.partial(_nsa_prompt_kernel, T=T, tq=tq, kc_len=kc_len),
        grid=(nb, nq),
        in_specs=[pl.BlockSpec((tq, A_WIDTH), tile), pl.BlockSpec((tq, A_WIDTH), tile),
                  pl.BlockSpec((tq, LANES), tile),
                  pl.BlockSpec((T, 4 * LANES), per_b), pl.BlockSpec((T, 2 * LANES), per_b),
                  pl.BlockSpec(wbd.shape, c3), pl.BlockSpec(pe.shape, c3), pl.BlockSpec(kg0.shape, c2),
                  pl.BlockSpec(pool.shape, c2)],
        out_specs=pl.BlockSpec((tq, A_WIDTH), tile),
        out_shape=jax.ShapeDtypeStruct((nb * T, A_WIDTH), F32),
        scratch_shapes=[pltpu.VMEM((T, LANES), F32), pltpu.VMEM((T, LANES), F32),
                        pltpu.VMEM((nseg, LANES), F32), pltpu.VMEM((nseg, LANES), F32),
                        pltpu.VMEM((A_KV, r4, LANES), F32), pltpu.VMEM((A_KV, r4, LANES), F32),
                        pltpu.VMEM((A_KV, r4, LANES), F32)],
        compiler_params=_cparams(("parallel", "arbitrary")),
        name="nsa_prompt",
    )(q, qr, small, rows, win, wbd, pe, kg0, pool)


def _nsa_sample_kernel(pt_ref, cache_ref, q_ref, qr_ref, small_ref, rows_ref, winnew_ref, winbuf_ref,
                       wbd_ref, pe_ref, kg0_ref, pool_ref, expand_ref,
                       o_ref, winout_ref,
                       cmp_buf, sel_buf, xperm_sc, sems, *, n_pages, past_len, t_valid):
    b = pl.program_id(0)
    nb = pl.num_programs(0)
    tp = SAMPLE_PAD_T
    nseg = past_len // CMP_STRIDE
    nsb = past_len // SEL_LEN
    wbuf = winbuf_ref.shape[0]

    def page_copies(bb, p, phase):
        page = pt_ref[bb * n_pages + p]
        dst_lanes = pl.ds(pl.multiple_of(p * PAGE_SIZE, PAGE_SIZE), PAGE_SIZE)
        if phase == 0:
            return [pltpu.make_async_copy(cache_ref.at[page, pl.ds(0, 2 * LANES), :],
                                          cmp_buf.at[:, dst_lanes], sems.at[0])]
        return [pltpu.make_async_copy(cache_ref.at[page, pl.ds(2 * LANES, 2 * LANES), :],
                                      sel_buf.at[:, dst_lanes], sems.at[1])]

    def start_all(bb, phase):
        def body(p, c):
            for cp in page_copies(bb, p, phase):
                cp.start()
            return c
        lax.fori_loop(0, n_pages, body, 0)

    def wait_all(bb, phase):
        def body(p, c):
            for cp in page_copies(bb, p, phase):
                cp.wait()
            return c
        lax.fori_loop(0, n_pages, body, 0)

    @pl.when(b == 0)
    def _():
        start_all(b, 0)

    start_all(b, 1)
    wait_all(b, 0)

    seg_pp = PAGE_SIZE // CMP_STRIDE
    pr = lax.broadcasted_iota(jnp.int32, (PAGE_SIZE, PAGE_SIZE), 0)
    pc = lax.broadcasted_iota(jnp.int32, (PAGE_SIZE, PAGE_SIZE), 1)
    perm = jnp.where(pc == CMP_STRIDE * (pr % seg_pp) + pr // seg_pp, 1.0, 0.0).astype(BF16)
    for p in range(n_pages):
        xp = _bdot_t(perm, cmp_buf[:, p * PAGE_SIZE:(p + 1) * PAGE_SIZE])
        for l in range(CMP_STRIDE):
            xperm_sc[l, p * seg_pp:(p + 1) * seg_pp, :] = xp[l * seg_pp:(l + 1) * seg_pp, :]
    kc, vc = _compress_grouped(xperm_sc, nseg, wbd_ref, pe_ref, kg0_ref[...])
    kc_b = kc.astype(BF16)
    vc_b = vc.astype(BF16)
    q = q_ref[...]
    qr = qr_ref[...]
    small = small_ref[...]
    tpos_col = past_len + lax.broadcasted_iota(jnp.int32, (tp, 1), 0)
    tpos_rows = jnp.concatenate([tpos_col] * A_HPG, axis=0)
    bp_idx = lax.broadcasted_iota(jnp.int32, (nsb, nsb), 0)
    b_idx = lax.broadcasted_iota(jnp.int32, (nsb, nsb), 1)
    o_cmps = []
    sels = []
    for g in range(A_KV):
        qn_g = _stack_heads(q, g)
        o_cmp, imp = _cmp_branch(qn_g, kc_b, vc_b, tpos_rows, nseg, tp)
        o_cmps.append(o_cmp)
        imp_sel = _dot2_exact_rhs(imp, pool_ref[...])
        imp_pad = jnp.concatenate([imp_sel, jnp.zeros((nsb - tp, nsb), F32)], axis=0)
        imp_t = jnp.transpose(imp_pad)
        rows_sel = []
        for t in range(tp):
            if t < t_valid:
                row_t = imp_sel[t:t + 1, :]
                col_t = imp_t[:, t:t + 1]
                ahead = jnp.where(col_t > row_t, 1.0, jnp.where((col_t == row_t) & (bp_idx < b_idx), 1.0, 0.0))
                rank = jnp.sum(ahead, axis=0, keepdims=True)
                rows_sel.append(jnp.where(rank < (N_SEL - 1), 1.0, 0.0))
            else:
                rows_sel.append(jnp.zeros((1, nsb), F32))
        sels.append(jnp.concatenate(rows_sel, axis=0).astype(BF16))

    @pl.when(b + 1 < nb)
    def _():
        start_all(b + 1, 0)

    wait_all(b, 1)

    new_idx = lax.broadcasted_iota(jnp.int32, (tp, tp), 1)
    tok_idx = lax.broadcasted_iota(jnp.int32, (tp, tp), 0)
    new_ok = jnp.concatenate([jnp.where(new_idx <= tok_idx, 1.0, 0.0)] * A_HPG, axis=0) > 0.5
    wpos = past_len - wbuf + lax.broadcasted_iota(jnp.int32, (1, wbuf), 1)
    wdiff = tpos_col - wpos
    win_ok = jnp.concatenate([jnp.where((wdiff >= 0) & (wdiff < WINDOW), 1.0, 0.0)] * A_HPG, axis=0) > 0.5
    k_past = sel_buf[0:LANES, :].astype(BF16)
    v_past = sel_buf[LANES:2 * LANES, :].astype(BF16)
    k_new = rows_ref[:, 2 * LANES:3 * LANES]
    v_new = rows_ref[:, 3 * LANES:4 * LANES]
    kw_past = winbuf_ref[:, 0:LANES]
    vw_past = winbuf_ref[:, LANES:2 * LANES]
    kw_new = winnew_ref[:, 0:LANES]
    vw_new = winnew_ref[:, LANES:2 * LANES]
    o_groups = []
    for g in range(A_KV):
        qr_g = _stack_heads(qr, g)
        mk = jnp.dot(sels[g], expand_ref[...], preferred_element_type=F32)
        past_ok = jnp.concatenate([mk] * A_HPG, axis=0) > 0.5
        o_sel = _masked_attn_direct(qr_g, [k_past, k_new], [v_past, v_new], [past_ok, new_ok], [True, False])
        o_win = _masked_attn_direct(qr_g, [kw_past, kw_new], [vw_past, vw_new], [win_ok, new_ok], [False, False])
        o_groups.append(_gate_cols(small, g, 0) * o_cmps[g] + _gate_cols(small, g, 1) * o_sel
                        + _gate_cols(small, g, 2) * o_win)
    o_ref[...] = _assemble_heads(o_groups, tp)

    wb = winbuf_ref[...]
    rolled = pltpu.roll(wb, wbuf - t_valid, 0)
    newr = pltpu.roll(winnew_ref[...], tp - t_valid, 0)
    sub = lax.broadcasted_iota(jnp.int32, (tp, 2 * LANES), 0)
    winout_ref[0:wbuf - tp, :] = rolled[0:wbuf - tp, :]
    winout_ref[wbuf - tp:wbuf, :] = jnp.where(sub < tp - t_valid, rolled[wbuf - tp:wbuf, :], newr)


def _nsa_sample(page_table, cache, q, qr, small, rows, winnew, winbuf, wbd, pe, kg0, t_valid):
    nb, n_pages = page_table.shape
    past_len = n_pages * PAGE_SIZE
    nseg = past_len // CMP_STRIDE
    nsb = past_len // SEL_LEN
    tp = SAMPLE_PAD_T
    wbuf = winbuf.shape[1]
    pool = (jnp.arange(nseg)[:, None] // (SEL_LEN // CMP_STRIDE) == jnp.arange(nsb)[None, :]).astype(BF16)
    expand = (jnp.arange(nsb)[:, None] == jnp.arange(past_len)[None, :] // SEL_LEN).astype(BF16)
    tile = lambda b, pt: (b, 0)
    c2 = lambda b, pt: (0, 0)
    c3 = lambda b, pt: (0, 0, 0)
    gs = pltpu.PrefetchScalarGridSpec(
        num_scalar_prefetch=1,
        grid=(nb,),
        in_specs=[pl.BlockSpec(memory_space=pl.ANY),
                  pl.BlockSpec((tp, A_WIDTH), tile), pl.BlockSpec((tp, A_WIDTH), tile),
                  pl.BlockSpec((tp, LANES), tile), pl.BlockSpec((tp, 4 * LANES), tile),
                  pl.BlockSpec((tp, 2 * LANES), tile),
                  pl.BlockSpec((None, wbuf, 2 * LANES), lambda b, pt: (b, 0, 0)),
                  pl.BlockSpec(wbd.shape, c3), pl.BlockSpec(pe.shape, c3), pl.BlockSpec(kg0.shape, c2),
                  pl.BlockSpec(pool.shape, c2), pl.BlockSpec(expand.shape, c2)],
        out_specs=[pl.BlockSpec((tp, A_WIDTH), tile),
                   pl.BlockSpec((None, wbuf, 2 * LANES), lambda b, pt: (b, 0, 0))],
        scratch_shapes=[pltpu.VMEM((2 * LANES, past_len), F32), pltpu.VMEM((2 * LANES, past_len), F32),
                        pltpu.VMEM((CMP_STRIDE, past_len // CMP_STRIDE, 2 * LANES), F32),
                        pltpu.SemaphoreType.DMA((2,))],
    )
    return pl.pallas_call(
        functools.partial(_nsa_sample_kernel, n_pages=n_pages, past_len=past_len, t_valid=t_valid),
        grid_spec=gs,
        out_shape=[jax.ShapeDtypeStruct((nb * tp, A_WIDTH), F32),
                   jax.ShapeDtypeStruct((nb, wbuf, 2 * LANES), F32)],
        compiler_params=_cparams(("arbitrary",)),
        name="nsa_sample",
    )(page_table.reshape(-1), cache, q, qr, small, rows, winnew, winbuf, wbd, pe, kg0, pool, expand)


MOE_TM = 256
SEG_ALIGN = 8
SEG_BITS = (256, 128, 64, 32, 16, 8)
MOE_RL = -(-(MOE_TM * TOP_K + N_EXPERTS * (SEG_ALIGN - 1)) // LANES) * LANES


def _pack_halves(x):
    w = x.shape[1] // 2
    bits = lax.bitcast_convert_type(x.astype(BF16).astype(F32), jnp.uint32)
    return (bits[:, :w] & jnp.uint32(0xFFFF0000)) | (bits[:, w:] >> 16)


def _unpack_halves(u):
    hi = lax.bitcast_convert_type(u & jnp.uint32(0xFFFF0000), F32).astype(BF16)
    lo = lax.bitcast_convert_type(u << 16, F32).astype(BF16)
    return hi, lo


def _route_and_sort(h2, wrt_ref, brt_ref, xsl_ref, info_ref, cnt_ref, tm, t_mod, t_valid, m_valid):
    ne = N_EXPERTS
    h2b = h2.astype(BF16)
    h2l = (h2 - h2b.astype(F32)).astype(BF16)
    wh, wl = _split(wrt_ref[...])
    lt = _bdot_t(wh, h2b) + _bdot_t(wl, h2b) + _bdot_t(wh, h2l) + brt_ref[...]
    eidx = lax.broadcasted_iota(jnp.int32, (ne, tm), 0)
    rank = jnp.zeros((ne, tm), F32)
    for ep in range(ne):
        v = lt[ep:ep + 1, :]
        rank = rank + jnp.where(v > lt, 1.0, jnp.where((v == lt) & (eidx > ep), 1.0, 0.0))
    sel = rank < TOP_K
    if t_mod is not None:
        tok = pl.program_id(0) * tm + lax.broadcasted_iota(jnp.int32, (1, tm), 1)
        sel = sel & ((tok % t_mod) < t_valid) & (tok < m_valid)
    mx = jnp.max(jnp.where(sel, lt, NEG_BIG), axis=0, keepdims=True)
    ex = jnp.where(sel, jnp.exp(lt - mx), 0.0)
    den = jnp.sum(ex, axis=0, keepdims=True)
    gate = ex / jnp.where(den > 0, den, 1.0)
    self_ = jnp.where(sel, 1.0, 0.0)
    selb = self_.astype(BF16)
    er = lax.broadcasted_iota(jnp.int32, (ne, ne), 0)
    ec = lax.broadcasted_iota(jnp.int32, (ne, ne), 1)
    c = jnp.dot(jnp.where(ec <= er, 1.0, 0.0).astype(BF16), selb, preferred_element_type=F32)
    tr = lax.broadcasted_iota(jnp.int32, (tm, tm), 0)
    tc = lax.broadcasted_iota(jnp.int32, (tm, tm), 1)
    rk = jnp.dot(selb, jnp.where(tr < tc, 1.0, 0.0).astype(BF16), preferred_element_type=F32)
    cnt = jnp.sum(self_, axis=1, keepdims=True)
    cnt_al = jnp.floor((cnt + (SEG_ALIGN - 1)) * (1.0 / SEG_ALIGN)) * SEG_ALIGN
    cnt_b = jnp.broadcast_to(cnt_al, (ne, LANES))
    cnt_ref[...] = cnt_b
    off = jnp.dot(jnp.where(ec < er, 1.0, 0.0).astype(BF16), cnt_b.astype(BF16), preferred_element_type=F32)
    rowidx = off[:, 0:1] + rk
    rows_k, gates_k, exps_k = [], [], []
    for k in range(1, TOP_K + 1):
        mk = sel & (c == k)
        has = jnp.sum(jnp.where(mk, 1.0, 0.0), axis=0, keepdims=True)
        rows_k.append(jnp.sum(jnp.where(mk, rowidx, 0.0), axis=0, keepdims=True) + has - 1.0)
        gates_k.append(jnp.sum(jnp.where(mk, gate, 0.0), axis=0, keepdims=True))
        exps_k.append(jnp.sum(jnp.where(mk, eidx.astype(F32), 0.0), axis=0, keepdims=True))
    info_ref[...] = jnp.concatenate(rows_k + gates_k + exps_k + [jnp.zeros((4, tm), F32)], axis=0)
    ridx = lax.broadcasted_iota(jnp.int32, (MOE_RL, tm), 0).astype(F32)
    perm = jnp.zeros((MOE_RL, tm), F32)
    for k in range(TOP_K):
        perm = perm + jnp.where(ridx == rows_k[k], 1.0, 0.0)
    xs = jnp.dot(perm.astype(BF16), h2b, preferred_element_type=F32)
    xsl_ref[...] = _pack_halves(xs)


def _mixout_kernel(x_ref, hm_ref, on_ref, mod_ref, gmix_ref, gffn_ref,
                   wog_ref, bog_ref, wum_ref, wua_ref, wout_ref, wrt_ref, brt_ref,
                   x1_ref, xsl_ref, info_ref, cnt_ref, *, tm, t_mod, t_valid, m_valid, n_real):
    if n_real is not None:
        @pl.when(pl.program_id(0) >= n_real)
        def _():
            xsl_ref[...] = jnp.zeros(xsl_ref.shape, jnp.uint32)
            info_ref[...] = jnp.zeros(info_ref.shape, F32)
            cnt_ref[...] = jnp.zeros(cnt_ref.shape, F32)

        @pl.when(pl.program_id(0) < n_real)
        def _():
            _mixout_body(x_ref, hm_ref, on_ref, mod_ref, gmix_ref, gffn_ref, wog_ref, bog_ref, wum_ref,
                         wua_ref, wout_ref, wrt_ref, brt_ref, x1_ref, xsl_ref, info_ref, cnt_ref,
                         tm, t_mod, t_valid, m_valid)
    else:
        _mixout_body(x_ref, hm_ref, on_ref, mod_ref, gmix_ref, gffn_ref, wog_ref, bog_ref, wum_ref,
                     wua_ref, wout_ref, wrt_ref, brt_ref, x1_ref, xsl_ref, info_ref, cnt_ref,
                     tm, t_mod, t_valid, m_valid)


def _mixout_body(x_ref, hm_ref, on_ref, mod_ref, gmix_ref, gffn_ref,
                 wog_ref, bog_ref, wum_ref, wua_ref, wout_ref, wrt_ref, brt_ref,
                 x1_ref, xsl_ref, info_ref, cnt_ref, tm, t_mod, t_valid, m_valid):
    d = D_MODEL
    x = x_ref[...]
    sh1, sc1, gt1 = mod_ref[:, 0:d], mod_ref[:, d:2 * d], mod_ref[:, 2 * d:3 * d]
    sh2, sc2 = mod_ref[:, 3 * d:4 * d], mod_ref[:, 4 * d:5 * d]
    h = _rmsnorm_rows(x, gmix_ref[...]) * (1.0 + sc1) + sh1
    hb = h.astype(BF16)
    mo = jnp.dot(hb, wog_ref[:, 0:M_WIDTH], preferred_element_type=F32) + bog_ref[:, 0:M_WIDTH]
    ym = _bdot(_sigmoid(mo) * hm_ref[...], wum_ref[...])
    ya = _bdot(on_ref[...], wua_ref[...])
    ga = jnp.dot(hb, wog_ref[:, M_WIDTH:M_WIDTH + d], preferred_element_type=F32) + bog_ref[:, M_WIDTH:M_WIDTH + d]
    u = _sigmoid(ga) * ym
    gb = (jnp.dot(hb, wog_ref[:, M_WIDTH + d:M_WIDTH + 2 * d], preferred_element_type=F32)
          + bog_ref[:, M_WIDTH + d:M_WIDTH + 2 * d])
    u = u + _sigmoid(gb) * ya
    x1 = x + gt1 * _bdot(u, wout_ref[...])
    x1_ref[...] = x1
    h2 = _rmsnorm_rows(x1, gffn_ref[...]) * (1.0 + sc2) + sh2
    _route_and_sort(h2, wrt_ref, brt_ref, xsl_ref, info_ref, cnt_ref, tm, t_mod, t_valid, m_valid)


def _mixout_with_shared(*refs, n_shared, **kw):
    n_in = 13
    _mixout_kernel(*refs[:n_in], *refs[n_in + n_shared:], **kw)


def _mixout(x2, hm, on, mod3, gmix, gffn, wts, tiles_per_mod, nt_total, tile0=0, shared=None,
            t_mod=None, t_valid=None, m_valid=None):
    m = x2.shape[0]
    tm = MOE_TM
    nt = m // tm
    (wog, bog, wum, wua, wout, wr, br) = wts
    r = mod3.shape[1]
    n_extra = nt_total - tile0 - nt if shared is None else 0
    row = lambda i: (jnp.minimum(i, nt - 1), 0)
    const = lambda i: (0, 0)
    in_specs = [pl.BlockSpec((tm, D_MODEL), row), pl.BlockSpec((tm, M_WIDTH), row),
                pl.BlockSpec((tm, A_WIDTH), row),
                pl.BlockSpec((None, r, 6 * D_MODEL), lambda i: (jnp.minimum(i, nt - 1) // tiles_per_mod, 0, 0)),
                pl.BlockSpec((1, D_MODEL), const), pl.BlockSpec((1, D_MODEL), const),
                pl.BlockSpec(wog.shape, const), pl.BlockSpec(bog.shape, const),
                pl.BlockSpec(wum.shape, const), pl.BlockSpec(wua.shape, const),
                pl.BlockSpec(wout.shape, const), pl.BlockSpec(wr.shape, const),
                pl.BlockSpec(br.shape, const)]
    args = [x2, hm, on, mod3, gmix, gffn, wog, bog, wum, wua, wout, wr, br]
    kw = dict(tm=tm, t_mod=t_mod, t_valid=t_valid, m_valid=m_valid, n_real=nt if n_extra else None)
    body = functools.partial(_mixout_kernel, **kw)
    aliases = {}
    if shared is not None:
        in_specs += [pl.BlockSpec(memory_space=pl.ANY)] * len(shared)
        aliases = {len(args) + j: 1 + j for j in range(len(shared))}
        args += list(shared)
        body = functools.partial(_mixout_with_shared, n_shared=len(shared), **kw)
    return pl.pallas_call(
        body,
        grid=(nt + n_extra,),
        in_specs=in_specs,
        out_specs=[pl.BlockSpec((tm, D_MODEL), row),
                   pl.BlockSpec((MOE_RL, D_MODEL // 2), lambda i: (tile0 + i, 0)),
                   pl.BlockSpec((16, tm), lambda i: (0, tile0 + i)),
                   pl.BlockSpec((None, N_EXPERTS, LANES), lambda i: (tile0 + i, 0, 0))],
        out_shape=[jax.ShapeDtypeStruct((m, D_MODEL), F32),
                   jax.ShapeDtypeStruct((nt_total * MOE_RL, D_MODEL // 2), jnp.uint32),
                   jax.ShapeDtypeStruct((16, nt_total * tm), F32),
                   jax.ShapeDtypeStruct((nt_total, N_EXPERTS, LANES), F32)],
        input_output_aliases=aliases,
        compiler_params=_cparams(("arbitrary" if n_extra else "parallel",)),
        name="mixout",
    )(*args)


MOE_BM = 256
MOE_CH = 512


def _moe_kernel(be_ref, na_ref, grp_ref,
                xsl_ref, wgu_ref, bgu_ref, wdn_ref, bdn_ref, ysl_in_ref, ysl_ref,
                wgu_bf, wdn_bf, xbuf, ybuf, sem_in, sem_out, *, trash_row0):
    del ysl_in_ref
    i = pl.program_id(0)
    na = na_ref[0]
    e = be_ref[i]
    prev = be_ref[jnp.maximum(i - 1, 0)]
    n_grp = MOE_BM // SEG_ALIGN

    def group_copies(blk, inbound):
        slot = blk % 2
        cps = []
        for r in range(n_grp):
            v = grp_ref[blk * n_grp + r]
            vm_rows = pl.ds(r * SEG_ALIGN, SEG_ALIGN)
            if inbound:
                row = pl.multiple_of(jnp.maximum(v, 0), SEG_ALIGN)
                cps.append(pltpu.make_async_copy(xsl_ref.at[pl.ds(row, SEG_ALIGN), :],
                                                 xbuf.at[slot, vm_rows, :], sem_in.at[slot]))
            else:
                spare = trash_row0 + slot * MOE_BM + r * SEG_ALIGN
                row = pl.multiple_of(jnp.where(v >= 0, v, spare), SEG_ALIGN)
                cps.append(pltpu.make_async_copy(ybuf.at[slot, vm_rows, :],
                                                 ysl_ref.at[pl.ds(row, SEG_ALIGN), :], sem_out.at[slot]))
        return cps

    def start_gather(blk):
        for cp in group_copies(blk, True):
            cp.start()

    def start_scatter(blk):
        for cp in group_copies(blk, False):
            cp.start()

    def wait_rows(blk, sem, inbound):
        slot = blk % 2
        if inbound:
            pltpu.make_async_copy(xsl_ref.at[pl.ds(0, MOE_BM), :], xbuf.at[slot], sem.at[slot]).wait()
        else:
            pltpu.make_async_copy(ybuf.at[slot], ysl_ref.at[pl.ds(0, MOE_BM), :], sem.at[slot]).wait()

    @pl.when(i == 0)
    def _():
        start_gather(i)

    @pl.when(i + 1 < na)
    def _():
        start_gather(i + 1)

    @pl.when((i < na) & ((i == 0) | (e != prev)))
    def _():
        for j in range(2 * D_EXPERT // MOE_CH):
            wgu_bf[:, j * MOE_CH:(j + 1) * MOE_CH] = wgu_ref[:, j * MOE_CH:(j + 1) * MOE_CH].astype(BF16)
        for j in range(D_EXPERT // MOE_CH):
            wdn_bf[j * MOE_CH:(j + 1) * MOE_CH, :] = wdn_ref[j * MOE_CH:(j + 1) * MOE_CH, :].astype(BF16)

    @pl.when(i < na)
    def _():
        slot = i % 2
        wait_rows(i, sem_in, True)

        @pl.when(i >= 2)
        def _():
            wait_rows(i - 2, sem_out, False)

        half = D_MODEL // 2
        xh, xl = _unpack_halves(xbuf[slot])

        def xdot(c0, c1):
            return (jnp.dot(xh, wgu_bf[0:half, c0:c1], preferred_element_type=F32)
                    + jnp.dot(xl, wgu_bf[half:D_MODEL, c0:c1], preferred_element_type=F32))

        acc = jnp.zeros((MOE_BM, D_MODEL), F32) + bdn_ref[...]
        for j in range(D_EXPERT // MOE_CH):
            lo, hi = j * MOE_CH, (j + 1) * MOE_CH
            gj = xdot(lo, hi) + bgu_ref[:, lo:hi]
            uj = xdot(D_EXPERT + lo, D_EXPERT + hi) + bgu_ref[:, D_EXPERT + lo:D_EXPERT + hi]
            gj = jnp.minimum(gj, SWIGLU_LIMIT)
            uj = jnp.clip(uj, -SWIGLU_LIMIT, SWIGLU_LIMIT)
            act = gj * _sigmoid(SWIGLU_ALPHA * gj) * (uj + 1.0)
            acc = acc + jnp.dot(act.astype(BF16), wdn_bf[lo:hi, :], preferred_element_type=F32)
        ybuf[slot] = _pack_halves(acc)
        start_scatter(i)

        @pl.when(i == na - 1)
        def _():
            @pl.when(i >= 1)
            def _():
                wait_rows(i - 1, sem_out, False)
            wait_rows(i, sem_out, False)


def _moe_experts(plan, xsl, w_gu, b_gu, w_dn, b_dn):
    block_e, n_active, grp_rows = plan
    nblk = block_e.shape[0]
    n_rows = xsl.shape[0]
    out_rows = n_rows + 2 * MOE_BM
    wmap = lambda i, be, *_: (be[i], 0, 0)
    anyspec = pl.BlockSpec(memory_space=pl.ANY)
    gs = pltpu.PrefetchScalarGridSpec(
        num_scalar_prefetch=3,
        grid=(nblk,),
        in_specs=[anyspec,
                  pl.BlockSpec((None, D_MODEL, 2 * D_EXPERT), wmap),
                  pl.BlockSpec((None, 1, 2 * D_EXPERT), wmap),
                  pl.BlockSpec((None, D_EXPERT, D_MODEL), wmap),
                  pl.BlockSpec((None, 1, D_MODEL), wmap),
                  anyspec],
        out_specs=anyspec,
        scratch_shapes=[pltpu.VMEM((D_MODEL, 2 * D_EXPERT), BF16), pltpu.VMEM((D_EXPERT, D_MODEL), BF16),
                        pltpu.VMEM((2, MOE_BM, D_MODEL // 2), jnp.uint32),
                        pltpu.VMEM((2, MOE_BM, D_MODEL // 2), jnp.uint32),
                        pltpu.SemaphoreType.DMA((2,)), pltpu.SemaphoreType.DMA((2,))],
    )
    return pl.pallas_call(
        functools.partial(_moe_kernel, trash_row0=n_rows),
        grid_spec=gs,
        out_shape=jax.ShapeDtypeStruct((out_rows, D_MODEL // 2), jnp.uint32),
        input_output_aliases={8: 0},
        compiler_params=_cparams(("arbitrary",)),
        name="moe_experts",
    )(*plan, xsl, w_gu, b_gu.reshape(N_EXPERTS, 1, -1), w_dn, b_dn.reshape(N_EXPERTS, 1, -1),
      jnp.zeros((out_rows, D_MODEL // 2), jnp.uint32))


def _combine_kernel(ysl_ref, info_ref, x1_ref, mod_ref, y_ref, *, tm):
    info = info_ref[...]
    info_t = jnp.transpose(jnp.concatenate([info, jnp.zeros((LANES - info.shape[0], tm), F32)], axis=0))
    ridx = lax.broadcasted_iota(jnp.int32, (tm, MOE_RL), 1).astype(F32)
    pg = jnp.zeros((tm, MOE_RL), F32)
    for k in range(TOP_K):
        pg = pg + jnp.where(ridx == info_t[:, k:k + 1], info_t[:, TOP_K + k:TOP_K + k + 1], 0.0)
    pg_hi, pg_lo = _split(pg)
    yh, yl = _unpack_halves(ysl_ref[...])
    half = D_MODEL // 2
    gt2 = mod_ref[:, 5 * D_MODEL:6 * D_MODEL]
    for c, yy in ((0, yh), (1, yl)):
        moe = jnp.dot(pg_hi, yy, preferred_element_type=F32) + jnp.dot(pg_lo, yy, preferred_element_type=F32)
        y_ref[:, c * half:(c + 1) * half] = (x1_ref[:, c * half:(c + 1) * half]
                                             + gt2[:, c * half:(c + 1) * half] * moe)


def _combine(ysl, info, x1, mod3, tiles_per_mod, tile0=0):
    m = x1.shape[0]
    tm = MOE_TM
    r = mod3.shape[1]
    return pl.pallas_call(
        functools.partial(_combine_kernel, tm=tm),
        grid=(m // tm,),
        in_specs=[pl.BlockSpec((MOE_RL, D_MODEL // 2), lambda i: (tile0 + i, 0)),
                  pl.BlockSpec((16, tm), lambda i: (0, tile0 + i)),
                  pl.BlockSpec((tm, D_MODEL), lambda i: (i, 0)),
                  pl.BlockSpec((None, r, 6 * D_MODEL), lambda i: (i // tiles_per_mod, 0, 0))],
        out_specs=pl.BlockSpec((tm, D_MODEL), lambda i: (i, 0)),
        out_shape=jax.ShapeDtypeStruct((m, D_MODEL), F32),
        compiler_params=_cparams(("parallel",)),
        name="moe_combine",
    )(ysl, info, x1, mod3)


def _moe_plan(cnt):
    cnt = cnt.astype(jnp.int32)
    nt = cnt.shape[0]
    so = jnp.cumsum(cnt, axis=1) - cnt + (jnp.arange(nt) * MOE_RL)[:, None]
    ce = jnp.cumsum(cnt, axis=0)
    cs = ce - cnt
    tot = ce[-1]
    nblk_e = (tot + MOE_BM - 1) // MOE_BM
    blk_end = jnp.cumsum(nblk_e)
    max_rows = nt * MOE_TM * TOP_K + nt * N_EXPERTS * (SEG_ALIGN - 1)
    n_blocks = -(-max_rows // MOE_BM) + N_EXPERTS
    bidx = jnp.arange(n_blocks)
    block_e = jnp.minimum(jnp.sum(blk_end[None, :] <= bidx[:, None], axis=1), N_EXPERTS - 1).astype(jnp.int32)
    block_r0 = (bidx - (blk_end - nblk_e)[block_e]) * MOE_BM
    x = block_r0[:, None] + jnp.arange(MOE_BM // SEG_ALIGN)[None, :] * SEG_ALIGN
    ce_b = ce[:, block_e]
    tile = jnp.minimum(jnp.sum(ce_b[:, :, None] <= x[None], axis=0), nt - 1)
    eb = block_e[:, None]
    grp = so[tile, eb] + x - cs[tile, eb]
    grp = jnp.where(x < tot[block_e][:, None], grp, -1)
    n_active = blk_end[-1].reshape(1)
    i32 = lambda a: a.reshape(-1).astype(jnp.int32)
    return block_e, i32(n_active), i32(grp)


def _rope_tables(pos):
    half = ROT_DIM // 2
    inv = ROPE_THETA ** (-jnp.arange(half, dtype=F32) * (2.0 / ROT_DIM))
    ang = pos.astype(F32)[:, None] * inv[None, :]
    cos, sin = jnp.cos(ang), jnp.sin(ang)
    n = pos.shape[0]
    ones = jnp.ones((n, A_DH - ROT_DIM), F32)
    zeros_h = jnp.zeros((n, half), F32)
    zeros_r = jnp.zeros((n, A_DH - ROT_DIM), F32)
    cos64 = jnp.concatenate([cos, cos, ones], axis=1)
    sprev64 = jnp.concatenate([zeros_h, sin, zeros_r], axis=1)
    snext64 = jnp.concatenate([-sin, zeros_h, zeros_r], axis=1)
    two = lambda a: jnp.concatenate([a, a], axis=1)
    return two(cos64), two(sprev64), two(snext64)


def _prep_weights(w_in, b_in, q_norm_g, k_norm_g, cmp_pe_k, cmp_pe_v, cmp_w_k, cmp_w_v,
                  w_up_m, w_up_a, w_out, w_router, b_router):
    b2 = b_in.reshape(1, N_IN)
    wm = w_in[:, OFF_MQ:OFF_MO].astype(BF16)
    bm = b2[:, OFF_MQ:OFF_MO]
    wq = w_in[:, OFF_AQ:OFF_AKV].astype(BF16)
    bq = b2[:, OFF_AQ:OFF_AKV]
    wkv = w_in[:, OFF_AKV:OFF_AG].astype(BF16)
    bkv = b2[:, OFF_AKV:OFF_AG]
    n_small = 2 * M_HEADS + 3 * A_HEADS
    ws = jnp.concatenate([w_in[:, OFF_MI:OFF_AQ], w_in[:, OFF_AG:OFF_GA],
                          jnp.zeros((D_MODEL, LANES - n_small), F32)], axis=1)
    bs = jnp.concatenate([b2[:, OFF_MI:OFF_AQ], b2[:, OFF_AG:OFF_GA], jnp.zeros((1, LANES - n_small), F32)], axis=1)
    qg = jnp.tile(q_norm_g, A_HEADS).reshape(1, A_WIDTH)
    kg = jnp.stack([jnp.tile(k_norm_g[1], A_KV), jnp.tile(k_norm_g[2], A_KV)], axis=0)
    kg0 = jnp.tile(k_norm_g[0], A_KV).reshape(1, LANES)
    hid = jnp.arange(A_WIDTH) // A_DH
    bd = jnp.where(hid[:, None] == hid[None, :], 1.0 / A_DH, 0.0).astype(BF16)
    inproj_w = (wm, bm, wq, bq, wkv, bkv, ws, bs, qg, kg, bd)

    z = jnp.zeros((CMP_LEN, A_DH, A_DH), F32)
    r0 = jnp.concatenate([cmp_w_k, z, z, z], axis=2)
    r1 = jnp.concatenate([z, cmp_w_k, z, z], axis=2)
    r2 = jnp.concatenate([z, z, cmp_w_v, z], axis=2)
    r3 = jnp.concatenate([z, z, z, cmp_w_v], axis=2)
    wbd = jnp.concatenate([r0, r1, r2, r3], axis=1).astype(BF16)
    pe = jnp.concatenate([cmp_pe_k, cmp_pe_k, cmp_pe_v, cmp_pe_v], axis=1).reshape(CMP_LEN, 1, 2 * LANES)

    wog = jnp.concatenate([w_in[:, OFF_MO:OFF_MI], w_in[:, OFF_GA:N_IN]], axis=1).astype(BF16)
    bog = jnp.concatenate([b2[:, OFF_MO:OFF_MI], b2[:, OFF_GA:N_IN]], axis=1)
    mixout_w = (wog, bog, w_up_m.astype(BF16), w_up_a.astype(BF16), w_out.astype(BF16),
                w_router.T, b_router.reshape(N_EXPERTS, 1))
    return inproj_w, (wbd, pe, kg0), mixout_w


def _pick_tile(m, pref):
    t = pref
    while m % t:
        t //= 2
    return t


def kernel(x_prompt, x_sample, cache_nsa_kv, state_win_kv, state_mlstm_C, state_mlstm_n, state_mlstm_m, page_table, c_prompt, c_sample, w_ada, b_ada, g_mix, g_ffn, w_in, b_in, q_norm_g, k_norm_g, cmp_pe_k, cmp_pe_v, cmp_w_k, cmp_w_v, w_up_m, w_up_a, w_out, w_router, b_router, w_gu, b_gu, w_dn, b_dn):
    depth = w_in.shape[0]
    assert depth == 1
    B, T, D = x_prompt.shape
    DB, TS, _ = x_sample.shape
    n_pages = page_table.shape[1]
    past_len = n_pages * PAGE_SIZE
    wbuf = state_win_kv.shape[2]
    tp = SAMPLE_PAD_T
    assert TS <= tp and wbuf % tp == 0 and T % 128 == 0

    l = 0
    inproj_w, cmp_w, mixout_w = _prep_weights(
        w_in[l], b_in[l], q_norm_g[l], k_norm_g[l], cmp_pe_k[l], cmp_pe_v[l], cmp_w_k[l], cmp_w_v[l],
        w_up_m[l], w_up_a[l], w_out[l], w_router[l], b_router[l])
    wbd, pe, kg0 = cmp_w
    gmix = g_mix[l].reshape(1, D)
    gffn = g_ffn[l].reshape(1, D)

    nc = B + DB
    nc_pad = -(-nc // SUBLANES) * SUBLANES
    c_all = jnp.concatenate([c_prompt, c_sample, jnp.zeros((nc_pad - nc, D), F32)], axis=0)
    mod = _adaln(c_all, w_ada[l], b_ada[l])
    mod_p = mod[:B].reshape(B, 1, 6 * D)
    mod_s = jnp.repeat(mod[B:B + DB], tp, axis=0).reshape(1, DB * tp, 6 * D)

    mp = B * T
    tm = _pick_tile(T, 256)
    xp = x_prompt.reshape(mp, D)
    tabs_p = _rope_tables(jnp.arange(T, dtype=jnp.int32))
    mq, mk, mv, q, qr, rows, win, small, rows_t = _inproj(xp, mod_p, gmix, tabs_p, inproj_w, tm, T // tm, T // tm,
                                                          rows_t_batches=B)
    Lp = _pick_tile(T, 128)
    hm, C_p, n_p, m_p = _mlstm(mq, mk, mv, small, B, T, T, Lp)
    o_nsa = _nsa_prompt(q, qr, small, rows, win, wbd, pe, kg0, B, T)
    assert T % MOE_TM == 0
    ms_pad = -(-(DB * tp) // MOE_TM) * MOE_TM
    nt_p = mp // MOE_TM
    nt_all = nt_p + ms_pad // MOE_TM
    x1_p, xsl, info, cnt = _mixout(xp, hm, o_nsa, mod_p, gmix, gffn, mixout_w, T // MOE_TM, nt_all)

    ms = DB * tp
    xs_pad = jnp.concatenate([x_sample, jnp.zeros((DB, tp - TS, D), F32)], axis=1).reshape(ms, D)
    pos_s = past_len + jnp.tile(jnp.arange(tp, dtype=jnp.int32), DB)
    tabs_s = _rope_tables(pos_s)
    mq_s, mk_s, mv_s, q_s, qr_s, rows_s, win_s, small_s = _inproj(xs_pad, mod_s, gmix, tabs_s, inproj_w, ms, 1, 1)
    hm_s, C_s, n_s, m_s = _mlstm(mq_s, mk_s, mv_s, small_s, DB, tp, TS, tp,
                                 state=(state_mlstm_C[l], state_mlstm_n[l], state_mlstm_m[l]))
    cache2 = jnp.transpose(cache_nsa_kv[l], (0, 2, 3, 4, 1)).reshape(cache_nsa_kv.shape[1], 4 * LANES, PAGE_SIZE)
    winbuf = state_win_kv[l].reshape(DB, wbuf, 2 * LANES)
    o_nsa_s, win_out_s = _nsa_sample(page_table, cache2, q_s, qr_s, small_s, rows_s, win_s, winbuf,
                                     wbd, pe, kg0, TS)
    assert ms_pad == MOE_TM
    rpad = lambda a: jnp.concatenate([a, jnp.zeros((ms_pad - ms, a.shape[1]), a.dtype)], axis=0) if ms_pad > ms else a
    mod_sp = rpad(mod_s[0])[None]
    x1_s, xsl, info, cnt = _mixout(rpad(xs_pad), rpad(hm_s), rpad(o_nsa_s), mod_sp, gmix, gffn, mixout_w,
                                   1, nt_all, tile0=nt_p, shared=(xsl, info, cnt),
                                   t_mod=tp, t_valid=TS, m_valid=ms)

    ysl = _moe_experts(_moe_plan(cnt[:, :, 0]), xsl, w_gu[l], b_gu[l], w_dn[l], b_dn[l])
    y_p = _combine(ysl, info, x1_p, mod_p, T // MOE_TM).reshape(B, T, D)
    y_s_all = _combine(ysl, info, x1_s, mod_sp, 1, tile0=nt_p)
    valid = lambda a: a.reshape(DB, tp, -1)[:, :TS].reshape(DB * TS, -1)
    y_s = valid(y_s_all[:ms]).reshape(DB, TS, D)

    kv_p = jnp.transpose(rows_t.reshape(B, 4, A_KV, A_DH, T), (0, 4, 1, 2, 3))[None]
    kv_s = valid(rows_s).reshape(1, DB, TS, 4, A_KV, A_DH)
    wp = min(WINDOW, T)
    win_p = win.reshape(B, T, 2, A_KV, A_DH)[:, T - wp:][None]
    win_s_out = win_out_s.reshape(1, DB, wbuf, 2, A_KV, A_DH)
    return (y_p, y_s, kv_p, kv_s, win_p, win_s_out,
            C_p[None], n_p[None], m_p[None], C_s[None], n_s[None], m_s[None])
```

```python
import functools
import math

import jax
import jax.numpy as jnp
from jax import lax
from jax.experimental import pallas as pl
from jax.experimental.pallas import tpu as pltpu

F32 = jnp.float32
BF16 = jnp.bfloat16

D_MODEL = 1024
M_HEADS = 4
M_DH = 128
M_WIDTH = M_HEADS * M_DH
A_HEADS = 8
A_KV = 2
A_HPG = A_HEADS // A_KV
A_DH = 64
A_WIDTH = A_HEADS * A_DH
CMP_STRIDE = 16
CMP_LEN = 32
SEL_LEN = 64
N_SEL = 16
WINDOW = 512
PAGE_SIZE = 128
ROPE_THETA = 500000.0
ROT_DIM = A_DH // 4
ATT_SCALE = A_DH ** -0.5
N_EXPERTS = 32
TOP_K = 4
D_EXPERT = D_MODEL
SWIGLU_LIMIT = 7.0
SWIGLU_ALPHA = 1.702
EPS = 1e-6

OFF_MQ, OFF_MK, OFF_MV, OFF_MO = 0, M_WIDTH, 2 * M_WIDTH, 3 * M_WIDTH
OFF_MI = 4 * M_WIDTH
OFF_MF = OFF_MI + M_HEADS
OFF_AQ = OFF_MF + M_HEADS
OFF_AKV = OFF_AQ + A_WIDTH
OFF_AG = OFF_AKV + 6 * A_KV * A_DH
OFF_GA = OFF_AG + 3 * A_HEADS
OFF_GB = OFF_GA + D_MODEL
N_IN = OFF_GB + D_MODEL

LANES = 128
SUBLANES = 8
VMEM_LIMIT = 56 * 1024 * 1024

NEG_BIG = -1e30
M_INIT = -1e29
LOG2E = 1.4426950408889634
SAMPLE_PAD_T = 8


def _cparams(sem):
    return pltpu.CompilerParams(dimension_semantics=sem, vmem_limit_bytes=VMEM_LIMIT)


def _bdot(a, b):
    return jnp.dot(a.astype(BF16), b.astype(BF16), preferred_element_type=F32)


def _bdot_t(a, b):
    return lax.dot_general(a.astype(BF16), b.astype(BF16), (((1,), (1,)), ((), ())),
                           preferred_element_type=F32)


def _split(a):
    hi = a.astype(BF16)
    lo = (a - hi.astype(F32)).astype(BF16)
    return hi, lo


def _dot3(a, b):
    ah, al = _split(a)
    bh, bl = _split(b)
    return (jnp.dot(ah, bh, preferred_element_type=F32) + jnp.dot(al, bh, preferred_element_type=F32)
            + jnp.dot(ah, bl, preferred_element_type=F32))


def _dot2_exact_rhs(a, b_bf16):
    ah, al = _split(a)
    return jnp.dot(ah, b_bf16, preferred_element_type=F32) + jnp.dot(al, b_bf16, preferred_element_type=F32)


def _sigmoid(x):
    return 1.0 / (1.0 + jnp.exp(-x))


def _rmsnorm_rows(x, g):
    return x * lax.rsqrt(jnp.mean(x * x, axis=-1, keepdims=True) + EPS) * g


def _adaln_kernel(c_ref, w_ref, b_ref, o_ref):
    c = c_ref[...]
    s = c * _sigmoid(c)
    o_ref[...] = _dot3(s, w_ref[...]) + b_ref[...]


def _adaln(c, w, b):
    mc, d = c.shape
    n = w.shape[1]
    tn = 1024
    return pl.pallas_call(
        _adaln_kernel,
        grid=(n // tn,),
        in_specs=[pl.BlockSpec((mc, d), lambda j: (0, 0)),
                  pl.BlockSpec((d, tn), lambda j: (0, j)),
                  pl.BlockSpec((1, tn), lambda j: (0, j))],
        out_specs=pl.BlockSpec((mc, tn), lambda j: (0, j)),
        out_shape=jax.ShapeDtypeStruct((mc, n), F32),
        compiler_params=_cparams(("parallel",)),
        name="adaln",
    )(c, w, b.reshape(1, n))


def _head_norm(z, bd, gain):
    ms = _dot2_exact_rhs(z * z, bd)
    return z * lax.rsqrt(ms + EPS) * gain


def _rope(z, cos, s_prev, s_next):
    w = z.shape[1]
    rep = w // LANES
    if rep > 1:
        cos = jnp.concatenate([cos] * rep, axis=1)
        s_prev = jnp.concatenate([s_prev] * rep, axis=1)
        s_next = jnp.concatenate([s_next] * rep, axis=1)
    z_prev = pltpu.roll(z, ROT_DIM // 2, 1)
    z_next = pltpu.roll(z, w - ROT_DIM // 2, 1)
    return z * cos + z_prev * s_prev + z_next * s_next


def _inproj_kernel(x_ref, mod_ref, gmix_ref, cos_ref, sp_ref, sn_ref,
                   wm_ref, bm_ref, wq_ref, bq_ref, wkv_ref, bkv_ref, ws_ref, bs_ref,
                   qg_ref, kg_ref, bd_ref,
                   mq_ref, mk_ref, mv_ref, q_ref, qr_ref, rows_ref, win_ref, small_ref, rows_t_ref=None):
    x = x_ref[...]
    sh1 = mod_ref[:, 0:D_MODEL]
    sc1 = mod_ref[:, D_MODEL:2 * D_MODEL]
    h = _rmsnorm_rows(x, gmix_ref[...]) * (1.0 + sc1) + sh1
    hb = h.astype(BF16)

    mq_ref[...] = jnp.dot(hb, wm_ref[:, 0:M_WIDTH], preferred_element_type=F32) + bm_ref[:, 0:M_WIDTH]
    mk = jnp.dot(hb, wm_ref[:, M_WIDTH:2 * M_WIDTH], preferred_element_type=F32) + bm_ref[:, M_WIDTH:2 * M_WIDTH]
    mk_ref[...] = mk * (M_DH ** -0.5)
    mv_ref[...] = (jnp.dot(hb, wm_ref[:, 2 * M_WIDTH:3 * M_WIDTH], preferred_element_type=F32)
                   + bm_ref[:, 2 * M_WIDTH:3 * M_WIDTH])

    cos, sp, sn = cos_ref[...], sp_ref[...], sn_ref[...]
    zq = jnp.dot(hb, wq_ref[...], preferred_element_type=F32) + bq_ref[...]
    qn = _head_norm(zq, bd_ref[...], qg_ref[...])
    q_ref[...] = qn
    qr_ref[...] = _rope(qn, cos, sp, sn)

    zkv = jnp.dot(hb, wkv_ref[...], preferred_element_type=F32) + bkv_ref[...]
    bd2 = bd_ref[0:LANES, 0:LANES]
    ksel = _head_norm(zkv[:, 2 * LANES:3 * LANES], bd2, kg_ref[0:1, :])
    rows = jnp.concatenate([zkv[:, 0:2 * LANES], _rope(ksel, cos, sp, sn), zkv[:, 3 * LANES:4 * LANES]], axis=1)
    rows_ref[...] = rows
    if rows_t_ref is not None:
        rows_t_ref[...] = jnp.transpose(rows)
    kwin = _head_norm(zkv[:, 4 * LANES:5 * LANES], bd2, kg_ref[1:2, :])
    win_ref[:, 0:LANES] = _rope(kwin, cos, sp, sn)
    win_ref[:, LANES:2 * LANES] = zkv[:, 5 * LANES:6 * LANES]

    small_ref[...] = _dot3(h, ws_ref[...]) + bs_ref[...]


def _inproj(x2, mod3, gmix, tabs, wts, tm, tiles_per_mod, pos_tiles, rows_t_batches=None):
    m = x2.shape[0]
    cos_t, sp_t, sn_t = tabs
    (wm, bm, wq, bq, wkv, bkv, ws, bs, qg, kg, bd) = wts
    r = mod3.shape[1]
    row = lambda i: (i, 0)
    const = lambda i: (0, 0)
    tab = lambda i: (i % pos_tiles, 0)
    in_specs = [
        pl.BlockSpec((tm, D_MODEL), row),
        pl.BlockSpec((None, r, 6 * D_MODEL), lambda i: (i // tiles_per_mod, 0, 0)),
        pl.BlockSpec((1, D_MODEL), const),
        pl.BlockSpec((tm, LANES), tab), pl.BlockSpec((tm, LANES), tab), pl.BlockSpec((tm, LANES), tab),
        pl.BlockSpec(wm.shape, const), pl.BlockSpec(bm.shape, const),
        pl.BlockSpec(wq.shape, const), pl.BlockSpec(bq.shape, const),
        pl.BlockSpec(wkv.shape, const), pl.BlockSpec(bkv.shape, const),
        pl.BlockSpec(ws.shape, const), pl.BlockSpec(bs.shape, const),
        pl.BlockSpec(qg.shape, const), pl.BlockSpec(kg.shape, const), pl.BlockSpec(bd.shape, const),
    ]
    widths = (M_WIDTH, M_WIDTH, M_WIDTH, A_WIDTH, A_WIDTH, 4 * LANES, 2 * LANES, LANES)
    out_specs = [pl.BlockSpec((tm, w), row) for w in widths]
    out_shape = [jax.ShapeDtypeStruct((m, w), F32) for w in widths]
    if rows_t_batches is not None:
        out_specs.append(pl.BlockSpec((None, 4 * LANES, tm), lambda i: (i // tiles_per_mod, 0, i % tiles_per_mod)))
        out_shape.append(jax.ShapeDtypeStruct((rows_t_batches, 4 * LANES, m // rows_t_batches), F32))
    return pl.pallas_call(
        _inproj_kernel,
        grid=(m // tm,),
        in_specs=in_specs,
        out_specs=out_specs,
        out_shape=out_shape,
        compiler_params=_cparams(("parallel",)),
        name="inproj",
    )(x2, mod3, gmix, cos_t, sp_t, sn_t, wm, bm, wq, bq, wkv, bkv, ws, bs, qg, kg, bd)


def _log_sigmoid(x):
    return jnp.minimum(x, 0.0) - jnp.log(1.0 + jnp.exp(-jnp.abs(x)))


def _mlstm_kernel(*refs, L, t_valid, has_state):
    if has_state:
        q_ref, k_ref, v_ref, s_ref, c0_ref, n0_ref, m0_ref, h_ref, c_ref, n_ref, m_ref = refs
    else:
        q_ref, k_ref, v_ref, s_ref, h_ref, c_ref, n_ref, m_ref = refs
    c = pl.program_id(1)

    @pl.when(c == 0)
    def _():
        if has_state:
            c_ref[...] = c0_ref[...]
            n_ref[...] = n0_ref[...]
            m_ref[...] = m0_ref[...]
        else:
            c_ref[...] = jnp.zeros(c_ref.shape, F32)
            n_ref[...] = jnp.zeros(n_ref.shape, F32)
            m_ref[...] = jnp.zeros(m_ref.shape, F32)

    row = lax.broadcasted_iota(jnp.int32, (L, L), 0)
    col = lax.broadcasted_iota(jnp.int32, (L, L), 1)
    causal = col <= row
    eye = col == row
    tok_col = c * L + lax.broadcasted_iota(jnp.int32, (L, 1), 0)
    valid_col = tok_col < t_valid
    for hd in range(M_HEADS):
        lo, hi = hd * M_DH, (hd + 1) * M_DH
        q = q_ref[:, lo:hi]
        k = k_ref[:, lo:hi]
        v = v_ref[:, lo:hi]
        i_col = s_ref[:, hd:hd + 1]
        lf_col = _log_sigmoid(s_ref[:, M_HEADS + hd:M_HEADS + hd + 1])
        lf_col = jnp.where(valid_col, lf_col, 0.0)
        i_col = jnp.where(valid_col, i_col, -jnp.inf)
        i_row = jnp.sum(jnp.where(eye, i_col, 0.0), axis=0, keepdims=True)
        lf_row = jnp.sum(jnp.where(eye, lf_col, 0.0), axis=0, keepdims=True)
        b_col = jnp.sum(jnp.where(causal, lf_row, 0.0), axis=1, keepdims=True)
        b_row = jnp.sum(jnp.where(row <= col, lf_col, 0.0), axis=0, keepdims=True)
        m_prev = m_ref[:, hd:hd + 1]
        dmat = jnp.where(causal, b_col - b_row + i_row, -jnp.inf)
        inter = b_col + m_prev
        m_row = jnp.maximum(jnp.max(dmat, axis=1, keepdims=True), inter)
        w = jnp.exp(dmat - m_row)
        w_inter = jnp.exp(inter - m_row)
        s = _bdot_t(q, k) * w
        cm = c_ref[hd]
        nv = n_ref[hd]
        num = _bdot(s, v) + w_inter * _bdot_t(q, cm)
        den = jnp.sum(s, axis=1, keepdims=True) + w_inter * jnp.sum(q * nv, axis=1, keepdims=True)
        h_ref[:, lo:hi] = num / jnp.maximum(jnp.abs(den), jnp.exp(-m_row))
        b_last = b_col[L - 1:L, :]
        dec_col = b_last - b_col + i_col
        dec_row = b_last - b_row + i_row
        m_new = jnp.maximum(b_last + m_prev, jnp.max(dec_row, axis=1, keepdims=True))
        ws_col = jnp.exp(dec_col - m_new)
        wc = jnp.exp(b_last + m_prev - m_new)
        vw = (v * ws_col).astype(BF16)
        upd = lax.dot_general(vw, k.astype(BF16), (((0,), (0,)), ((), ())), preferred_element_type=F32)
        c_ref[hd] = wc * cm + upd
        n_ref[hd] = wc * nv + jnp.sum(k * ws_col, axis=0, keepdims=True)
        m_ref[:, hd:hd + 1] = m_new


def _mlstm(mq, mk, mv, small, nb, t_pad, t_valid, L, state=None):
    nc = t_pad // L
    has_state = state is not None
    blk = lambda b, c: (b * nc + c, 0)
    st4 = lambda b, c: (b, 0, 0, 0)
    st3 = lambda b, c: (b, 0, 0)
    in_specs = [pl.BlockSpec((L, M_WIDTH), blk)] * 3 + [pl.BlockSpec((L, LANES), blk)]
    args = [mq, mk, mv, small]
    if has_state:
        c0, n0, m0 = state
        in_specs += [pl.BlockSpec((None, M_HEADS, M_DH, M_DH), st4),
                     pl.BlockSpec((None, M_HEADS, 1, M_DH), st4),
                     pl.BlockSpec((None, 1, M_HEADS), st3)]
        args += [c0, n0.reshape(nb, M_HEADS, 1, M_DH), m0.reshape(nb, 1, M_HEADS)]
    out_specs = [pl.BlockSpec((L, M_WIDTH), blk),
                 pl.BlockSpec((None, M_HEADS, M_DH, M_DH), st4),
                 pl.BlockSpec((None, M_HEADS, 1, M_DH), st4),
                 pl.BlockSpec((None, 1, M_HEADS), st3)]
    out_shape = [jax.ShapeDtypeStruct((nb * t_pad, M_WIDTH), F32),
                 jax.ShapeDtypeStruct((nb, M_HEADS, M_DH, M_DH), F32),
                 jax.ShapeDtypeStruct((nb, M_HEADS, 1, M_DH), F32),
                 jax.ShapeDtypeStruct((nb, 1, M_HEADS), F32)]
    h, cs, ns, ms = pl.pallas_call(
        functools.partial(_mlstm_kernel, L=L, t_valid=t_valid, has_state=has_state),
        grid=(nb, nc),
        in_specs=in_specs,
        out_specs=out_specs,
        out_shape=out_shape,
        compiler_params=_cparams(("parallel", "arbitrary")),
        name="mlstm",
    )(*args)
    return h, cs, ns.reshape(nb, M_HEADS, M_DH), ms.reshape(nb, M_HEADS)


def _stack_heads(qt, g):
    t = qt.shape[0]
    z = jnp.zeros((t, A_DH), F32)
    parts = []
    for hh in range(A_HPG):
        hd = g * A_HPG + hh
        qh = qt[:, hd * A_DH:(hd + 1) * A_DH] * (ATT_SCALE * LOG2E)
        parts.append(jnp.concatenate([qh, z], axis=1) if g == 0 else jnp.concatenate([z, qh], axis=1))
    return jnp.concatenate(parts, axis=0).astype(BF16)


def _gate_cols(small, g, br):
    cols = []
    for hh in range(A_HPG):
        c0 = 2 * M_HEADS + (g * A_HPG + hh) * 3 + br
        cols.append(_sigmoid(small[:, c0:c0 + 1]))
    return jnp.concatenate(cols, axis=0)


def _compress(k_ref, v_ref, nseg, wbd_ref, pe_ref, kg0):
    acc_lo = jnp.zeros((nseg, 2 * LANES), F32)
    acc_hi = jnp.zeros((nseg, 2 * LANES), F32)
    for l in range(CMP_STRIDE):
        xl = jnp.concatenate([k_ref[pl.ds(l, nseg, stride=CMP_STRIDE), :],
                              v_ref[pl.ds(l, nseg, stride=CMP_STRIDE), :]], axis=1)
        acc_lo = acc_lo + _bdot(xl + pe_ref[l], wbd_ref[l])
        acc_hi = acc_hi + _bdot(xl + pe_ref[CMP_STRIDE + l], wbd_ref[CMP_STRIDE + l])
    return _compress_finish(acc_lo, acc_hi, nseg, kg0)


def _compress_grouped(x_ref, nseg, wbd_ref, pe_ref, kg0):
    acc_lo = jnp.zeros((nseg, 2 * LANES), F32)
    acc_hi = jnp.zeros((nseg, 2 * LANES), F32)
    pe_lo = jnp.zeros((SUBLANES, 2 * LANES), F32)
    pe_hi = jnp.zeros((SUBLANES, 2 * LANES), F32)
    for l in range(CMP_STRIDE):
        xl = x_ref[l].astype(BF16)
        acc_lo = acc_lo + jnp.dot(xl, wbd_ref[l], preferred_element_type=F32)
        acc_hi = acc_hi + jnp.dot(xl, wbd_ref[CMP_STRIDE + l], preferred_element_type=F32)
        pe_lo = pe_lo + _bdot(jnp.broadcast_to(pe_ref[l], (SUBLANES, 2 * LANES)), wbd_ref[l])
        pe_hi = pe_hi + _bdot(jnp.broadcast_to(pe_ref[CMP_STRIDE + l], (SUBLANES, 2 * LANES)),
                              wbd_ref[CMP_STRIDE + l])
    return _compress_finish(acc_lo + pe_lo[0:1, :], acc_hi + pe_hi[0:1, :], nseg, kg0)


def _compress_finish(acc_lo, acc_hi, nseg, kg0):
    kv = acc_lo + pltpu.roll(acc_hi, nseg - 1, 0)
    kc = kv[:, 0:LANES]
    vc = kv[:, LANES:2 * LANES]
    lane = lax.broadcasted_iota(jnp.int32, (nseg, LANES), 1)
    sq = kc * kc
    ms0 = jnp.sum(jnp.where(lane < A_DH, sq, 0.0), axis=1, keepdims=True) * (1.0 / A_DH)
    ms1 = jnp.sum(jnp.where(lane >= A_DH, sq, 0.0), axis=1, keepdims=True) * (1.0 / A_DH)
    ms = jnp.where(lane < A_DH, ms0, ms1)
    kc = kc * lax.rsqrt(ms + EPS) * kg0
    return kc, vc


def _cmp_branch(qn_g, kc_b, vc_b, tpos_rows, nseg, n_tok):
    s = _bdot_t(qn_g, kc_b)
    nidx = lax.broadcasted_iota(jnp.int32, (1, nseg), 1)
    vis = (nidx * CMP_STRIDE + (CMP_LEN - 1)) <= tpos_rows
    sm = jnp.where(vis, s, NEG_BIG)
    mx = jnp.max(sm, axis=1, keepdims=True)
    e = jnp.where(vis, jnp.exp2(sm - mx), 0.0)
    d = jnp.sum(e, axis=1, keepdims=True)
    p = e / jnp.where(d > 0, d, 1.0)
    o = _bdot(p, vc_b)
    imp = p[0:n_tok]
    for hh in range(1, A_HPG):
        imp = imp + p[hh * n_tok:(hh + 1) * n_tok]
    return o, imp


def _masked_attn_direct(q_g, k_parts, v_parts, allowed_parts, feature_major):
    ss = [jnp.where(al, _bdot(q_g, kk) if fm else _bdot_t(q_g, kk), NEG_BIG)
          for kk, al, fm in zip(k_parts, allowed_parts, feature_major)]
    mx = ss[0].max(axis=1, keepdims=True)
    for s in ss[1:]:
        mx = jnp.maximum(mx, s.max(axis=1, keepdims=True))
    num = None
    den = None
    for s, al, vv, fm in zip(ss, allowed_parts, v_parts, feature_major):
        e = jnp.where(al, jnp.exp2(s - mx), 0.0)
        dd = jnp.sum(e, axis=1, keepdims=True)
        oo = _bdot_t(e, vv) if fm else _bdot(e, vv)
        num = oo if num is None else num + oo
        den = dd if den is None else den + dd
    return num / jnp.where(den > 0, den, 1.0)


def _assemble_heads(o_groups, n_tok):
    pieces = []
    for g in range(A_KV):
        for hh in range(A_HPG):
            pieces.append(o_groups[g][hh * n_tok:(hh + 1) * n_tok, g * A_DH:(g + 1) * A_DH])
    return jnp.concatenate(pieces, axis=1)


def _lane_tile(a, width):
    rep = width // LANES
    return a if rep == 1 else jnp.concatenate([a] * rep, axis=1)


def _add_bias(s, bias):
    t, k = bias.shape
    return (s.reshape(A_HPG, t, k) + bias[None]).reshape(A_HPG * t, k)


def _nsa_prompt_kernel(q_ref, qr_ref, small_ref, rows_ref, win_ref, wbd_ref, pe_ref, kg0_ref,
                       pool_ref, o_ref,
                       kraw_sc, vraw_sc, kc_sc, vc_sc, m_sc, l_sc, acc_sc, *, T, tq, kc_len):
    qi = pl.program_id(1)
    nseg = T // CMP_STRIDE
    nsb = T // SEL_LEN

    @pl.when(qi == 0)
    def _():
        kraw_sc[...] = rows_ref[:, 0:LANES]
        vraw_sc[...] = rows_ref[:, LANES:2 * LANES]
        kc, vc = _compress(kraw_sc, vraw_sc, nseg, wbd_ref, pe_ref, kg0_ref[...])
        kc_sc[...] = kc
        vc_sc[...] = vc

    t0 = qi * tq
    tpos_col = t0 + lax.broadcasted_iota(jnp.int32, (tq, 1), 0)
    tpos_rows = jnp.concatenate([tpos_col] * A_HPG, axis=0)
    tpos_lane = t0 + lax.broadcasted_iota(jnp.int32, (1, tq), 1)
    q = q_ref[...]
    qr = qr_ref[...]
    small = small_ref[...]
    kc_b = kc_sc[...].astype(BF16)
    vc_b = vc_sc[...].astype(BF16)
    bidx = lax.broadcasted_iota(jnp.int32, (nsb, tq), 0)
    cur = tpos_lane // SEL_LEN
    r4 = A_HPG * tq
    qr_gs = [_stack_heads(qr, g) for g in range(A_KV)]
    o_cmps = []
    sel_bs = []
    for g in range(A_KV):
        qn_g = _stack_heads(q, g)
        o_cmp, imp = _cmp_branch(qn_g, kc_b, vc_b, tpos_rows, nseg, tq)
        o_cmps.append(o_cmp)
        imp_sel = _dot2_exact_rhs(imp, pool_ref[...])
        imp_t = jnp.transpose(imp_sel)[0:nsb, :]
        val = jnp.where(bidx < cur, imp_t, -1.0)
        rank = jnp.zeros((nsb, tq), F32)
        for bp in range(nsb):
            vb = val[bp:bp + 1, :]
            ahead = jnp.where(vb > val, 1.0, jnp.where((vb == val) & (bidx > bp), 1.0, 0.0))
            rank = rank + ahead
        sel_t = jnp.where(((rank < (N_SEL - 1)) & (bidx < cur)) | (bidx == cur), 1.0, 0.0)
        if nsb < LANES:
            sel_t = jnp.concatenate([sel_t, jnp.zeros((LANES - nsb, tq), F32)], axis=0)
        sel_bs.append(jnp.transpose(sel_t).astype(BF16))

    m_sc[...] = jnp.full(m_sc.shape, M_INIT, F32)
    l_sc[...] = jnp.zeros(l_sc.shape, F32)
    acc_sc[...] = jnp.zeros(acc_sc.shape, F32)

    def sel_body(c, carry):
        k0 = pl.multiple_of(c * kc_len, kc_len)
        kb = rows_ref[pl.ds(k0, kc_len), 2 * LANES:3 * LANES].astype(BF16)
        vb = rows_ref[pl.ds(k0, kc_len), 3 * LANES:4 * LANES].astype(BF16)
        kpos = k0 + lax.broadcasted_iota(jnp.int32, (1, kc_len), 1)
        causal = kpos <= tpos_col
        kblk = (k0 + lax.broadcasted_iota(jnp.int32, (LANES, kc_len), 1)) // SEL_LEN
        expand = jnp.where(kblk == lax.broadcasted_iota(jnp.int32, (LANES, kc_len), 0), 1.0, 0.0).astype(BF16)
        for g in range(A_KV):
            mk = jnp.dot(sel_bs[g], expand, preferred_element_type=F32)
            bias = jnp.where(causal, (mk - 1.0) * (-NEG_BIG), NEG_BIG)
            sm = _add_bias(_bdot_t(qr_gs[g], kb), bias)
            m_prev = m_sc[g]
            m_new = jnp.maximum(m_prev, jnp.max(sm, axis=1, keepdims=True))
            alpha = jnp.exp2(m_prev - m_new)
            p = jnp.exp2(sm - _lane_tile(m_new, kc_len))
            l_sc[g] = alpha * l_sc[g] + jnp.sum(p, axis=1, keepdims=True)
            acc_sc[g] = alpha * acc_sc[g] + _bdot(p, vb)
            m_sc[g] = m_new
        return carry

    lax.fori_loop(0, (t0 + tq + kc_len - 1) // kc_len, sel_body, 0)

    wk = min(WINDOW + tq, T)
    w0 = pl.multiple_of(jnp.clip(t0 + tq - wk, 0, T - wk), tq)
    kw = win_ref[pl.ds(w0, wk), 0:LANES].astype(BF16)
    vw = win_ref[pl.ds(w0, wk), LANES:2 * LANES].astype(BF16)
    wdiff = tpos_col - (w0 + lax.broadcasted_iota(jnp.int32, (1, wk), 1))
    wbias = jnp.where((wdiff >= 0) & (wdiff < WINDOW), 0.0, NEG_BIG)

    o_groups = []
    for g in range(A_KV):
        l = l_sc[g]
        o_sel = acc_sc[g] / jnp.where(l > 0, l, 1.0)
        sw = _add_bias(_bdot_t(qr_gs[g], kw), wbias)
        mw = jnp.broadcast_to(jnp.max(sw, axis=1, keepdims=True), (r4, LANES))
        pw = jnp.exp2(sw - _lane_tile(mw, wk))
        o_win = _bdot(pw, vw) / jnp.broadcast_to(jnp.sum(pw, axis=1, keepdims=True), (r4, LANES))
        o_groups.append(_gate_cols(small, g, 0) * o_cmps[g] + _gate_cols(small, g, 1) * o_sel
                        + _gate_cols(small, g, 2) * o_win)
    o_ref[...] = _assemble_heads(o_groups, tq)


def _nsa_prompt(q, qr, small, rows, win, wbd, pe, kg0, nb, T):
    tq = 128
    kc_len = _pick_tile(T, 512)
    nq = T // tq
    nseg = T // CMP_STRIDE
    nsb = T // SEL_LEN
    pool = (jnp.arange(nseg)[:, None] // (SEL_LEN // CMP_STRIDE) == jnp.arange(LANES)[None, :]).astype(BF16)
    tile = lambda b, i: (b * nq + i, 0)
    per_b = lambda b, i: (b, 0)
    c2 = lambda b, i: (0, 0)
    c3 = lambda b, i: (0, 0, 0)
    r4 = A_HPG * tq
    return pl.pallas_call(
        functools.partial(_nsa_prompt_kernel, T=T, tq=tq, kc_len=kc_len),
        grid=(nb, nq),
        in_specs=[pl.BlockSpec((tq, A_WIDTH), tile), pl.BlockSpec((tq, A_WIDTH), tile),
                  pl.BlockSpec((tq, LANES), tile),
                  pl.BlockSpec((T, 4 * LANES), per_b), pl.BlockSpec((T, 2 * LANES), per_b),
                  pl.BlockSpec(wbd.shape, c3), pl.BlockSpec(pe.shape, c3), pl.BlockSpec(kg0.shape, c2),
                  pl.BlockSpec(pool.shape, c2)],
        out_specs=pl.BlockSpec((tq, A_WIDTH), tile),
        out_shape=jax.ShapeDtypeStruct((nb * T, A_WIDTH), F32),
        scratch_shapes=[pltpu.VMEM((T, LANES), F32), pltpu.VMEM((T, LANES), F32),
                        pltpu.VMEM((nseg, LANES), F32), pltpu.VMEM((nseg, LANES), F32),
                        pltpu.VMEM((A_KV, r4, LANES), F32), pltpu.VMEM((A_KV, r4, LANES), F32),
                        pltpu.VMEM((A_KV, r4, LANES), F32)],
        compiler_params=_cparams(("parallel", "arbitrary")),
        name="nsa_prompt",
    )(q, qr, small, rows, win, wbd, pe, kg0, pool)


def _nsa_sample_kernel(pt_ref, cache_ref, q_ref, qr_ref, small_ref, rows_ref, winnew_ref, winbuf_ref,
                       wbd_ref, pe_ref, kg0_ref, pool_ref, expand_ref,
                       o_ref, winout_ref,
                       cmp_buf, sel_buf, xperm_sc, sems, *, n_pages, past_len, t_valid):
    b = pl.program_id(0)
    nb = pl.num_programs(0)
    tp = SAMPLE_PAD_T
    nseg = past_len // CMP_STRIDE
    nsb = past_len // SEL_LEN
    wbuf = winbuf_ref.shape[0]

    def page_copies(bb, p, phase):
        page = pt_ref[bb * n_pages + p]
        dst_lanes = pl.ds(pl.multiple_of(p * PAGE_SIZE, PAGE_SIZE), PAGE_SIZE)
        if phase == 0:
            return [pltpu.make_async_copy(cache_ref.at[page, pl.ds(0, 2 * LANES), :],
                                          cmp_buf.at[:, dst_lanes], sems.at[0])]
        return [pltpu.make_async_copy(cache_ref.at[page, pl.ds(2 * LANES, 2 * LANES), :],
                                      sel_buf.at[:, dst_lanes], sems.at[1])]

    def start_all(bb, phase):
        def body(p, c):
            for cp in page_copies(bb, p, phase):
                cp.start()
            return c
        lax.fori_loop(0, n_pages, body, 0)

    def wait_all(bb, phase):
        def body(p, c):
            for cp in page_copies(bb, p, phase):
                cp.wait()
            return c
        lax.fori_loop(0, n_pages, body, 0)

    @pl.when(b == 0)
    def _():
        start_all(b, 0)

    start_all(b, 1)
    wait_all(b, 0)

    seg_pp = PAGE_SIZE // CMP_STRIDE
    pr = lax.broadcasted_iota(jnp.int32, (PAGE_SIZE, PAGE_SIZE), 0)
    pc = lax.broadcasted_iota(jnp.int32, (PAGE_SIZE, PAGE_SIZE), 1)
    perm = jnp.where(pc == CMP_STRIDE * (pr % seg_pp) + pr // seg_pp, 1.0, 0.0).astype(BF16)
    for p in range(n_pages):
        xp = _bdot_t(perm, cmp_buf[:, p * PAGE_SIZE:(p + 1) * PAGE_SIZE])
        for l in range(CMP_STRIDE):
            xperm_sc[l, p * seg_pp:(p + 1) * seg_pp, :] = xp[l * seg_pp:(l + 1) * seg_pp, :]
    kc, vc = _compress_grouped(xperm_sc, nseg, wbd_ref, pe_ref, kg0_ref[...])
    kc_b = kc.astype(BF16)
    vc_b = vc.astype(BF16)
    q = q_ref[...]
    qr = qr_ref[...]
    small = small_ref[...]
    tpos_col = past_len + lax.broadcasted_iota(jnp.int32, (tp, 1), 0)
    tpos_rows = jnp.concatenate([tpos_col] * A_HPG, axis=0)
    bp_idx = lax.broadcasted_iota(jnp.int32, (nsb, nsb), 0)
    b_idx = lax.broadcasted_iota(jnp.int32, (nsb, nsb), 1)
    o_cmps = []
    sels = []
    for g in range(A_KV):
        qn_g = _stack_heads(q, g)
        o_cmp, imp = _cmp_branch(qn_g, kc_b, vc_b, tpos_rows, nseg, tp)
        o_cmps.append(o_cmp)
        imp_sel = _dot2_exact_rhs(imp, pool_ref[...])
        imp_pad = jnp.concatenate([imp_sel, jnp.zeros((nsb - tp, nsb), F32)], axis=0)
        imp_t = jnp.transpose(imp_pad)
        rows_sel = []
        for t in range(tp):
            if t < t_valid:
                row_t = imp_sel[t:t + 1, :]
                col_t = imp_t[:, t:t + 1]
                ahead = jnp.where(col_t > row_t, 1.0, jnp.where((col_t == row_t) & (bp_idx < b_idx), 1.0, 0.0))
                rank = jnp.sum(ahead, axis=0, keepdims=True)
                rows_sel.append(jnp.where(rank < (N_SEL - 1), 1.0, 0.0))
            else:
                rows_sel.append(jnp.zeros((1, nsb), F32))
        sels.append(jnp.concatenate(rows_sel, axis=0).astype(BF16))

    @pl.when(b + 1 < nb)
    def _():
        start_all(b + 1, 0)

    wait_all(b, 1)

    new_idx = lax.broadcasted_iota(jnp.int32, (tp, tp), 1)
    tok_idx = lax.broadcasted_iota(jnp.int32, (tp, tp), 0)
    new_ok = jnp.concatenate([jnp.where(new_idx <= tok_idx, 1.0, 0.0)] * A_HPG, axis=0) > 0.5
    wpos = past_len - wbuf + lax.broadcasted_iota(jnp.int32, (1, wbuf), 1)
    wdiff = tpos_col - wpos
    win_ok = jnp.concatenate([jnp.where((wdiff >= 0) & (wdiff < WINDOW), 1.0, 0.0)] * A_HPG, axis=0) > 0.5
    k_past = sel_buf[0:LANES, :].astype(BF16)
    v_past = sel_buf[LANES:2 * LANES, :].astype(BF16)
    k_new = rows_ref[:, 2 * LANES:3 * LANES]
    v_new = rows_ref[:, 3 * LANES:4 * LANES]
    kw_past = winbuf_ref[:, 0:LANES]
    vw_past = winbuf_ref[:, LANES:2 * LANES]
    kw_new = winnew_ref[:, 0:LANES]
    vw_new = winnew_ref[:, LANES:2 * LANES]
    o_groups = []
    for g in range(A_KV):
        qr_g = _stack_heads(qr, g)
        mk = jnp.dot(sels[g], expand_ref[...], preferred_element_type=F32)
        past_ok = jnp.concatenate([mk] * A_HPG, axis=0) > 0.5
        o_sel = _masked_attn_direct(qr_g, [k_past, k_new], [v_past, v_new], [past_ok, new_ok], [True, False])
        o_win = _masked_attn_direct(qr_g, [kw_past, kw_new], [vw_past, vw_new], [win_ok, new_ok], [False, False])
        o_groups.append(_gate_cols(small, g, 0) * o_cmps[g] + _gate_cols(small, g, 1) * o_sel
                        + _gate_cols(small, g, 2) * o_win)
    o_ref[...] = _assemble_heads(o_groups, tp)

    wb = winbuf_ref[...]
    rolled = pltpu.roll(wb, wbuf - t_valid, 0)
    newr = pltpu.roll(winnew_ref[...], tp - t_valid, 0)
    sub = lax.broadcasted_iota(jnp.int32, (tp, 2 * LANES), 0)
    winout_ref[0:wbuf - tp, :] = rolled[0:wbuf - tp, :]
    winout_ref[wbuf - tp:wbuf, :] = jnp.where(sub < tp - t_valid, rolled[wbuf - tp:wbuf, :], newr)


def _nsa_sample(page_table, cache, q, qr, small, rows, winnew, winbuf, wbd, pe, kg0, t_valid):
    nb, n_pages = page_table.shape
    past_len = n_pages * PAGE_SIZE
    nseg = past_len // CMP_STRIDE
    nsb = past_len // SEL_LEN
    tp = SAMPLE_PAD_T
    wbuf = winbuf.shape[1]
    pool = (jnp.arange(nseg)[:, None] // (SEL_LEN // CMP_STRIDE) == jnp.arange(nsb)[None, :]).astype(BF16)
    expand = (jnp.arange(nsb)[:, None] == jnp.arange(past_len)[None, :] // SEL_LEN).astype(BF16)
    tile = lambda b, pt: (b, 0)
    c2 = lambda b, pt: (0, 0)
    c3 = lambda b, pt: (0, 0, 0)
    gs = pltpu.PrefetchScalarGridSpec(
        num_scalar_prefetch=1,
        grid=(nb,),
        in_specs=[pl.BlockSpec(memory_space=pl.ANY),
                  pl.BlockSpec((tp, A_WIDTH), tile), pl.BlockSpec((tp, A_WIDTH), tile),
                  pl.BlockSpec((tp, LANES), tile), pl.BlockSpec((tp, 4 * LANES), tile),
                  pl.BlockSpec((tp, 2 * LANES), tile),
                  pl.BlockSpec((None, wbuf, 2 * LANES), lambda b, pt: (b, 0, 0)),
                  pl.BlockSpec(wbd.shape, c3), pl.BlockSpec(pe.shape, c3), pl.BlockSpec(kg0.shape, c2),
                  pl.BlockSpec(pool.shape, c2), pl.BlockSpec(expand.shape, c2)],
        out_specs=[pl.BlockSpec((tp, A_WIDTH), tile),
                   pl.BlockSpec((None, wbuf, 2 * LANES), lambda b, pt: (b, 0, 0))],
        scratch_shapes=[pltpu.VMEM((2 * LANES, past_len), F32), pltpu.VMEM((2 * LANES, past_len), F32),
                        pltpu.VMEM((CMP_STRIDE, past_len // CMP_STRIDE, 2 * LANES), F32),
                        pltpu.SemaphoreType.DMA((2,))],
    )
    return pl.pallas_call(
        functools.partial(_nsa_sample_kernel, n_pages=n_pages, past_len=past_len, t_valid=t_valid),
        grid_spec=gs,
        out_shape=[jax.ShapeDtypeStruct((nb * tp, A_WIDTH), F32),
                   jax.ShapeDtypeStruct((nb, wbuf, 2 * LANES), F32)],
        compiler_params=_cparams(("arbitrary",)),
        name="nsa_sample",
    )(page_table.reshape(-1), cache, q, qr, small, rows, winnew, winbuf, wbd, pe, kg0, pool, expand)


MOE_TM = 256
SEG_ALIGN = 8
SEG_BITS = (256, 128, 64, 32, 16, 8)
MOE_RL = -(-(MOE_TM * TOP_K + N_EXPERTS * (SEG_ALIGN - 1)) // LANES) * LANES


def _pack_halves(x):
    w = x.shape[1] // 2
    bits = lax.bitcast_convert_type(x.astype(BF16).astype(F32), jnp.uint32)
    return (bits[:, :w] & jnp.uint32(0xFFFF0000)) | (bits[:, w:] >> 16)


def _unpack_halves(u):
    hi = lax.bitcast_convert_type(u & jnp.uint32(0xFFFF0000), F32).astype(BF16)
    lo = lax.bitcast_convert_type(u << 16, F32).astype(BF16)
    return hi, lo


def _route_and_sort(h2, wrt_ref, brt_ref, xsl_ref, info_ref, cnt_ref, tm, t_mod, t_valid, m_valid):
    ne = N_EXPERTS
    h2b = h2.astype(BF16)
    h2l = (h2 - h2b.astype(F32)).astype(BF16)
    wh, wl = _split(wrt_ref[...])
    lt = _bdot_t(wh, h2b) + _bdot_t(wl, h2b) + _bdot_t(wh, h2l) + brt_ref[...]
    eidx = lax.broadcasted_iota(jnp.int32, (ne, tm), 0)
    rank = jnp.zeros((ne, tm), F32)
    for ep in range(ne):
        v = lt[ep:ep + 1, :]
        rank = rank + jnp.where(v > lt, 1.0, jnp.where((v == lt) & (eidx > ep), 1.0, 0.0))
    sel = rank < TOP_K
    if t_mod is not None:
        tok = pl.program_id(0) * tm + lax.broadcasted_iota(jnp.int32, (1, tm), 1)
        sel = sel & ((tok % t_mod) < t_valid) & (tok < m_valid)
    mx = jnp.max(jnp.where(sel, lt, NEG_BIG), axis=0, keepdims=True)
    ex = jnp.where(sel, jnp.exp(lt - mx), 0.0)
    den = jnp.sum(ex, axis=0, keepdims=True)
    gate = ex / jnp.where(den > 0, den, 1.0)
    self_ = jnp.where(sel, 1.0, 0.0)
    selb = self_.astype(BF16)
    er = lax.broadcasted_iota(jnp.int32, (ne, ne), 0)
    ec = lax.broadcasted_iota(jnp.int32, (ne, ne), 1)
    c = jnp.dot(jnp.where(ec <= er, 1.0, 0.0).astype(BF16), selb, preferred_element_type=F32)
    tr = lax.broadcasted_iota(jnp.int32, (tm, tm), 0)
    tc = lax.broadcasted_iota(jnp.int32, (tm, tm), 1)
    rk = jnp.dot(selb, jnp.where(tr < tc, 1.0, 0.0).astype(BF16), preferred_element_type=F32)
    cnt = jnp.sum(self_, axis=1, keepdims=True)
    cnt_al = jnp.floor((cnt + (SEG_ALIGN - 1)) * (1.0 / SEG_ALIGN)) * SEG_ALIGN
    cnt_b = jnp.broadcast_to(cnt_al, (ne, LANES))
    cnt_ref[...] = cnt_b
    off = jnp.dot(jnp.where(ec < er, 1.0, 0.0).astype(BF16), cnt_b.astype(BF16), preferred_element_type=F32)
    rowidx = off[:, 0:1] + rk
    rows_k, gates_k, exps_k = [], [], []
    for k in range(1, TOP_K + 1):
        mk = sel & (c == k)
        has = jnp.sum(jnp.where(mk, 1.0, 0.0), axis=0, keepdims=True)
        rows_k.append(jnp.sum(jnp.where(mk, rowidx, 0.0), axis=0, keepdims=True) + has - 1.0)
        gates_k.append(jnp.sum(jnp.where(mk, gate, 0.0), axis=0, keepdims=True))
        exps_k.append(jnp.sum(jnp.where(mk, eidx.astype(F32), 0.0), axis=0, keepdims=True))
    info_ref[...] = jnp.concatenate(rows_k + gates_k + exps_k + [jnp.zeros((4, tm), F32)], axis=0)
    ridx = lax.broadcasted_iota(jnp.int32, (MOE_RL, tm), 0).astype(F32)
    perm = jnp.zeros((MOE_RL, tm), F32)
    for k in range(TOP_K):
        perm = perm + jnp.where(ridx == rows_k[k], 1.0, 0.0)
    xs = jnp.dot(perm.astype(BF16), h2b, preferred_element_type=F32)
    xsl_ref[...] = _pack_halves(xs)


def _mixout_kernel(x_ref, hm_ref, on_ref, mod_ref, gmix_ref, gffn_ref,
                   wog_ref, bog_ref, wum_ref, wua_ref, wout_ref, wrt_ref, brt_ref,
                   x1_ref, xsl_ref, info_ref, cnt_ref, *, tm, t_mod, t_valid, m_valid, n_real):
    if n_real is not None:
        @pl.when(pl.program_id(0) >= n_real)
        def _():
            xsl_ref[...] = jnp.zeros(xsl_ref.shape, jnp.uint32)
            info_ref[...] = jnp.zeros(info_ref.shape, F32)
            cnt_ref[...] = jnp.zeros(cnt_ref.shape, F32)

        @pl.when(pl.program_id(0) < n_real)
        def _():
            _mixout_body(x_ref, hm_ref, on_ref, mod_ref, gmix_ref, gffn_ref, wog_ref, bog_ref, wum_ref,
                         wua_ref, wout_ref, wrt_ref, brt_ref, x1_ref, xsl_ref, info_ref, cnt_ref,
                         tm, t_mod, t_valid, m_valid)
    else:
        _mixout_body(x_ref, hm_ref, on_ref, mod_ref, gmix_ref, gffn_ref, wog_ref, bog_ref, wum_ref,
                     wua_ref, wout_ref, wrt_ref, brt_ref, x1_ref, xsl_ref, info_ref, cnt_ref,
                     tm, t_mod, t_valid, m_valid)


def _mixout_body(x_ref, hm_ref, on_ref, mod_ref, gmix_ref, gffn_ref,
                 wog_ref, bog_ref, wum_ref, wua_ref, wout_ref, wrt_ref, brt_ref,
                 x1_ref, xsl_ref, info_ref, cnt_ref, tm, t_mod, t_valid, m_valid):
    d = D_MODEL
    x = x_ref[...]
    sh1, sc1, gt1 = mod_ref[:, 0:d], mod_ref[:, d:2 * d], mod_ref[:, 2 * d:3 * d]
    sh2, sc2 = mod_ref[:, 3 * d:4 * d], mod_ref[:, 4 * d:5 * d]
    h = _rmsnorm_rows(x, gmix_ref[...]) * (1.0 + sc1) + sh1
    hb = h.astype(BF16)
    mo = jnp.dot(hb, wog_ref[:, 0:M_WIDTH], preferred_element_type=F32) + bog_ref[:, 0:M_WIDTH]
    ym = _bdot(_sigmoid(mo) * hm_ref[...], wum_ref[...])
    ya = _bdot(on_ref[...], wua_ref[...])
    ga = jnp.dot(hb, wog_ref[:, M_WIDTH:M_WIDTH + d], preferred_element_type=F32) + bog_ref[:, M_WIDTH:M_WIDTH + d]
    u = _sigmoid(ga) * ym
    gb = (jnp.dot(hb, wog_ref[:, M_WIDTH + d:M_WIDTH + 2 * d], preferred_element_type=F32)
          + bog_ref[:, M_WIDTH + d:M_WIDTH + 2 * d])
    u = u + _sigmoid(gb) * ya
    x1 = x + gt1 * _bdot(u, wout_ref[...])
    x1_ref[...] = x1
    h2 = _rmsnorm_rows(x1, gffn_ref[...]) * (1.0 + sc2) + sh2
    _route_and_sort(h2, wrt_ref, brt_ref, xsl_ref, info_ref, cnt_ref, tm, t_mod, t_valid, m_valid)


def _mixout_with_shared(*refs, n_shared, **kw):
    n_in = 13
    _mixout_kernel(*refs[:n_in], *refs[n_in + n_shared:], **kw)


def _mixout(x2, hm, on, mod3, gmix, gffn, wts, tiles_per_mod, nt_total, tile0=0, shared=None,
            t_mod=None, t_valid=None, m_valid=None):
    m = x2.shape[0]
    tm = MOE_TM
    nt = m // tm
    (wog, bog, wum, wua, wout, wr, br) = wts
    r = mod3.shape[1]
    n_extra = nt_total - tile0 - nt if shared is None else 0
    row = lambda i: (jnp.minimum(i, nt - 1), 0)
    const = lambda i: (0, 0)
    in_specs = [pl.BlockSpec((tm, D_MODEL), row), pl.BlockSpec((tm, M_WIDTH), row),
                pl.BlockSpec((tm, A_WIDTH), row),
                pl.BlockSpec((None, r, 6 * D_MODEL), lambda i: (jnp.minimum(i, nt - 1) // tiles_per_mod, 0, 0)),
                pl.BlockSpec((1, D_MODEL), const), pl.BlockSpec((1, D_MODEL), const),
                pl.BlockSpec(wog.shape, const), pl.BlockSpec(bog.shape, const),
                pl.BlockSpec(wum.shape, const), pl.BlockSpec(wua.shape, const),
                pl.BlockSpec(wout.shape, const), pl.BlockSpec(wr.shape, const),
                pl.BlockSpec(br.shape, const)]
    args = [x2, hm, on, mod3, gmix, gffn, wog, bog, wum, wua, wout, wr, br]
    kw = dict(tm=tm, t_mod=t_mod, t_valid=t_valid, m_valid=m_valid, n_real=nt if n_extra else None)
    body = functools.partial(_mixout_kernel, **kw)
    aliases = {}
    if shared is not None:
        in_specs += [pl.BlockSpec(memory_space=pl.ANY)] * len(shared)
        aliases = {len(args) + j: 1 + j for j in range(len(shared))}
        args += list(shared)
        body = functools.partial(_mixout_with_shared, n_shared=len(shared), **kw)
    return pl.pallas_call(
        body,
        grid=(nt + n_extra,),
        in_specs=in_specs,
        out_specs=[pl.BlockSpec((tm, D_MODEL), row),
                   pl.BlockSpec((MOE_RL, D_MODEL // 2), lambda i: (tile0 + i, 0)),
                   pl.BlockSpec((16, tm), lambda i: (0, tile0 + i)),
                   pl.BlockSpec((None, N_EXPERTS, LANES), lambda i: (tile0 + i, 0, 0))],
        out_shape=[jax.ShapeDtypeStruct((m, D_MODEL), F32),
                   jax.ShapeDtypeStruct((nt_total * MOE_RL, D_MODEL // 2), jnp.uint32),
                   jax.ShapeDtypeStruct((16, nt_total * tm), F32),
                   jax.ShapeDtypeStruct((nt_total, N_EXPERTS, LANES), F32)],
        input_output_aliases=aliases,
        compiler_params=_cparams(("arbitrary" if n_extra else "parallel",)),
        name="mixout",
    )(*args)


MOE_BM = 256
MOE_CH = 512


def _moe_kernel(be_ref, na_ref, grp_ref,
                xsl_ref, wgu_ref, bgu_ref, wdn_ref, bdn_ref, ysl_in_ref, ysl_ref,
                wgu_bf, wdn_bf, xbuf, ybuf, sem_in, sem_out, *, trash_row0):
    del ysl_in_ref
    i = pl.program_id(0)
    na = na_ref[0]
    e = be_ref[i]
    prev = be_ref[jnp.maximum(i - 1, 0)]
    n_grp = MOE_BM // SEG_ALIGN

    def group_copies(blk, inbound, slot=None):
        slot = blk % 2 if slot is None else slot
        cps = []
        for r in range(n_grp):
            v = grp_ref[blk * n_grp + r]
            vm_rows = pl.ds(r * SEG_ALIGN, SEG_ALIGN)
            if inbound:
                row = pl.multiple_of(jnp.maximum(v, 0), SEG_ALIGN)
                cps.append(pltpu.make_async_copy(xsl_ref.at[pl.ds(row, SEG_ALIGN), :],
                                                 xbuf.at[slot, vm_rows, :], sem_in.at[slot]))
            else:
                spare = trash_row0 + slot * MOE_BM + r * SEG_ALIGN
                row = pl.multiple_of(jnp.where(v >= 0, v, spare), SEG_ALIGN)
                cps.append(pltpu.make_async_copy(ybuf.at[slot, vm_rows, :],
                                                 ysl_ref.at[pl.ds(row, SEG_ALIGN), :], sem_out.at[slot]))
        return cps

    def start_gather(blk):
        for cp in group_copies(blk, True):
            cp.start()

    def start_scatter(blk):
        for cp in group_copies(blk, False):
            cp.start()

    def wait_rows(blk, sem, inbound):
        slot = blk % 2
        if inbound:
            pltpu.make_async_copy(xsl_ref.at[pl.ds(0, MOE_BM), :], xbuf.at[slot], sem.at[slot]).wait()
        else:
            pltpu.make_async_copy(ybuf.at[slot], ysl_ref.at[pl.ds(0, MOE_BM), :], sem.at[slot]).wait()

    @pl.when(i == 0)
    def _():
        start_gather(i)

    @pl.when((i < na) & ((i == 0) | (e != prev)))
    def _():
        for j in range(2 * D_EXPERT // MOE_CH):
            wgu_bf[:, j * MOE_CH:(j + 1) * MOE_CH] = wgu_ref[:, j * MOE_CH:(j + 1) * MOE_CH].astype(BF16)
        for j in range(D_EXPERT // MOE_CH):
            wdn_bf[j * MOE_CH:(j + 1) * MOE_CH, :] = wdn_ref[j * MOE_CH:(j + 1) * MOE_CH, :].astype(BF16)

    @pl.when(i < na)
    def _():
        slot = i % 2
        wait_rows(i, sem_in, True)

        @pl.when(i >= 2)
        def _():
            wait_rows(i - 2, sem_out, False)

        half = D_MODEL // 2
        xh, xl = _unpack_halves(xbuf[slot])

        def xdot(c0, c1):
            return (jnp.dot(xh, wgu_bf[0:half, c0:c1], preferred_element_type=F32)
                    + jnp.dot(xl, wgu_bf[half:D_MODEL, c0:c1], preferred_element_type=F32))

        nxt = jnp.minimum(i + 1, na - 1)
        fetch = group_copies(nxt, True, (i + 1) % 2)
        n_chunk = D_EXPERT // MOE_CH
        per = -(-len(fetch) // n_chunk)
        acc = jnp.zeros((MOE_BM, D_MODEL), F32) + bdn_ref[...]
        for j in range(n_chunk):
            lo, hi = j * MOE_CH, (j + 1) * MOE_CH
            gj = xdot(lo, hi) + bgu_ref[:, lo:hi]
            uj = xdot(D_EXPERT + lo, D_EXPERT + hi) + bgu_ref[:, D_EXPERT + lo:D_EXPERT + hi]
            for cp in fetch[j * per:(j + 1) * per]:
                cp.start()
            gj = jnp.minimum(gj, SWIGLU_LIMIT)
            uj = jnp.clip(uj, -SWIGLU_LIMIT, SWIGLU_LIMIT)
            act = gj * _sigmoid(SWIGLU_ALPHA * gj) * (uj + 1.0)
            acc = acc + jnp.dot(act.astype(BF16), wdn_bf[lo:hi, :], preferred_element_type=F32)
        ybuf[slot] = _pack_halves(acc)
        start_scatter(i)

        @pl.when(i == na - 1)
        def _():
            wait_rows(i + 1, sem_in, True)

            @pl.when(i >= 1)
            def _():
                wait_rows(i - 1, sem_out, False)
            wait_rows(i, sem_out, False)


def _moe_experts(plan, xsl, w_gu, b_gu, w_dn, b_dn):
    block_e, n_active, grp_rows = plan
    nblk = block_e.shape[0]
    n_rows = xsl.shape[0]
    out_rows = n_rows + 2 * MOE_BM
    wmap = lambda i, be, *_: (be[i], 0, 0)
    anyspec = pl.BlockSpec(memory_space=pl.ANY)
    gs = pltpu.PrefetchScalarGridSpec(
        num_scalar_prefetch=3,
        grid=(nblk,),
        in_specs=[anyspec,
                  pl.BlockSpec((None, D_MODEL, 2 * D_EXPERT), wmap),
                  pl.BlockSpec((None, 1, 2 * D_EXPERT), wmap),
                  pl.BlockSpec((None, D_EXPERT, D_MODEL), wmap),
                  pl.BlockSpec((None, 1, D_MODEL), wmap),
                  anyspec],
        out_specs=anyspec,
        scratch_shapes=[pltpu.VMEM((D_MODEL, 2 * D_EXPERT), BF16), pltpu.VMEM((D_EXPERT, D_MODEL), BF16),
                        pltpu.VMEM((2, MOE_BM, D_MODEL // 2), jnp.uint32),
                        pltpu.VMEM((2, MOE_BM, D_MODEL // 2), jnp.uint32),
                        pltpu.SemaphoreType.DMA((2,)), pltpu.SemaphoreType.DMA((2,))],
    )
    return pl.pallas_call(
        functools.partial(_moe_kernel, trash_row0=n_rows),
        grid_spec=gs,
        out_shape=jax.ShapeDtypeStruct((out_rows, D_MODEL // 2), jnp.uint32),
        input_output_aliases={8: 0},
        compiler_params=_cparams(("arbitrary",)),
        name="moe_experts",
    )(*plan, xsl, w_gu, b_gu.reshape(N_EXPERTS, 1, -1), w_dn, b_dn.reshape(N_EXPERTS, 1, -1),
      jnp.zeros((out_rows, D_MODEL // 2), jnp.uint32))


def _combine_kernel(ysl_ref, info_ref, x1_ref, mod_ref, y_ref, *, tm):
    info = info_ref[...]
    info_t = jnp.transpose(jnp.concatenate([info, jnp.zeros((LANES - info.shape[0], tm), F32)], axis=0))
    ridx = lax.broadcasted_iota(jnp.int32, (tm, MOE_RL), 1).astype(F32)
    pg = jnp.zeros((tm, MOE_RL), F32)
    for k in range(TOP_K):
        pg = pg + jnp.where(ridx == info_t[:, k:k + 1], info_t[:, TOP_K + k:TOP_K + k + 1], 0.0)
    pg_hi, pg_lo = _split(pg)
    yh, yl = _unpack_halves(ysl_ref[...])
    half = D_MODEL // 2
    gt2 = mod_ref[:, 5 * D_MODEL:6 * D_MODEL]
    for c, yy in ((0, yh), (1, yl)):
        moe = jnp.dot(pg_hi, yy, preferred_element_type=F32) + jnp.dot(pg_lo, yy, preferred_element_type=F32)
        y_ref[:, c * half:(c + 1) * half] = (x1_ref[:, c * half:(c + 1) * half]
                                             + gt2[:, c * half:(c + 1) * half] * moe)


def _combine(ysl, info, x1, mod3, tiles_per_mod, tile0=0):
    m = x1.shape[0]
    tm = MOE_TM
    r = mod3.shape[1]
    return pl.pallas_call(
        functools.partial(_combine_kernel, tm=tm),
        grid=(m // tm,),
        in_specs=[pl.BlockSpec((MOE_RL, D_MODEL // 2), lambda i: (tile0 + i, 0)),
                  pl.BlockSpec((16, tm), lambda i: (0, tile0 + i)),
                  pl.BlockSpec((tm, D_MODEL), lambda i: (i, 0)),
                  pl.BlockSpec((None, r, 6 * D_MODEL), lambda i: (i // tiles_per_mod, 0, 0))],
        out_specs=pl.BlockSpec((tm, D_MODEL), lambda i: (i, 0)),
        out_shape=jax.ShapeDtypeStruct((m, D_MODEL), F32),
        compiler_params=_cparams(("parallel",)),
        name="moe_combine",
    )(ysl, info, x1, mod3)


def _moe_plan(cnt):
    cnt = cnt.astype(jnp.int32)
    nt = cnt.shape[0]
    so = jnp.cumsum(cnt, axis=1) - cnt + (jnp.arange(nt) * MOE_RL)[:, None]
    ce = jnp.cumsum(cnt, axis=0)
    cs = ce - cnt
    tot = ce[-1]
    nblk_e = (tot + MOE_BM - 1) // MOE_BM
    blk_end = jnp.cumsum(nblk_e)
    max_rows = nt * MOE_TM * TOP_K + nt * N_EXPERTS * (SEG_ALIGN - 1)
    n_blocks = -(-max_rows // MOE_BM) + N_EXPERTS
    bidx = jnp.arange(n_blocks)
    block_e = jnp.minimum(jnp.sum(blk_end[None, :] <= bidx[:, None], axis=1), N_EXPERTS - 1).astype(jnp.int32)
    is_e = (jnp.arange(N_EXPERTS)[:, None] == block_e[None, :]).astype(jnp.int32)
    per_block = lambda a: jnp.sum(a[..., :, None] * is_e, axis=-2)
    block_r0 = (bidx - per_block(blk_end - nblk_e)) * MOE_BM
    x = block_r0[:, None] + jnp.arange(MOE_BM // SEG_ALIGN)[None, :] * SEG_ALIGN
    ce_b = per_block(ce)[:, :, None]
    cs_b = per_block(cs)[:, :, None]
    inside = (cs_b <= x[None]) & (x[None] < ce_b)
    grp = x + jnp.sum(jnp.where(inside, per_block(so - cs)[:, :, None], 0), axis=0)
    grp = jnp.where(x < per_block(tot)[:, None], grp, -1)
    n_active = blk_end[-1].reshape(1)
    i32 = lambda a: a.reshape(-1).astype(jnp.int32)
    return block_e, i32(n_active), i32(grp)


def _rope_tables(pos):
    half = ROT_DIM // 2
    inv = ROPE_THETA ** (-jnp.arange(half, dtype=F32) * (2.0 / ROT_DIM))
    ang = pos.astype(F32)[:, None] * inv[None, :]
    cos, sin = jnp.cos(ang), jnp.sin(ang)
    n = pos.shape[0]
    ones = jnp.ones((n, A_DH - ROT_DIM), F32)
    zeros_h = jnp.zeros((n, half), F32)
    zeros_r = jnp.zeros((n, A_DH - ROT_DIM), F32)
    cos64 = jnp.concatenate([cos, cos, ones], axis=1)
    sprev64 = jnp.concatenate([zeros_h, sin, zeros_r], axis=1)
    snext64 = jnp.concatenate([-sin, zeros_h, zeros_r], axis=1)
    two = lambda a: jnp.concatenate([a, a], axis=1)
    return two(cos64), two(sprev64), two(snext64)


def _prep_weights(w_in, b_in, q_norm_g, k_norm_g, cmp_pe_k, cmp_pe_v, cmp_w_k, cmp_w_v,
                  w_up_m, w_up_a, w_out, w_router, b_router):
    b2 = b_in.reshape(1, N_IN)
    wm = w_in[:, OFF_MQ:OFF_MO].astype(BF16)
    bm = b2[:, OFF_MQ:OFF_MO]
    wq = w_in[:, OFF_AQ:OFF_AKV].astype(BF16)
    bq = b2[:, OFF_AQ:OFF_AKV]
    wkv = w_in[:, OFF_AKV:OFF_AG].astype(BF16)
    bkv = b2[:, OFF_AKV:OFF_AG]
    n_small = 2 * M_HEADS + 3 * A_HEADS
    ws = jnp.concatenate([w_in[:, OFF_MI:OFF_AQ], w_in[:, OFF_AG:OFF_GA],
                          jnp.zeros((D_MODEL, LANES - n_small), F32)], axis=1)
    bs = jnp.concatenate([b2[:, OFF_MI:OFF_AQ], b2[:, OFF_AG:OFF_GA], jnp.zeros((1, LANES - n_small), F32)], axis=1)
    qg = jnp.tile(q_norm_g, A_HEADS).reshape(1, A_WIDTH)
    kg = jnp.stack([jnp.tile(k_norm_g[1], A_KV), jnp.tile(k_norm_g[2], A_KV)], axis=0)
    kg0 = jnp.tile(k_norm_g[0], A_KV).reshape(1, LANES)
    hid = jnp.arange(A_WIDTH) // A_DH
    bd = jnp.where(hid[:, None] == hid[None, :], 1.0 / A_DH, 0.0).astype(BF16)
    inproj_w = (wm, bm, wq, bq, wkv, bkv, ws, bs, qg, kg, bd)

    z = jnp.zeros((CMP_LEN, A_DH, A_DH), F32)
    r0 = jnp.concatenate([cmp_w_k, z, z, z], axis=2)
    r1 = jnp.concatenate([z, cmp_w_k, z, z], axis=2)
    r2 = jnp.concatenate([z, z, cmp_w_v, z], axis=2)
    r3 = jnp.concatenate([z, z, z, cmp_w_v], axis=2)
    wbd = jnp.concatenate([r0, r1, r2, r3], axis=1).astype(BF16)
    pe = jnp.concatenate([cmp_pe_k, cmp_pe_k, cmp_pe_v, cmp_pe_v], axis=1).reshape(CMP_LEN, 1, 2 * LANES)

    wog = jnp.concatenate([w_in[:, OFF_MO:OFF_MI], w_in[:, OFF_GA:N_IN]], axis=1).astype(BF16)
    bog = jnp.concatenate([b2[:, OFF_MO:OFF_MI], b2[:, OFF_GA:N_IN]], axis=1)
    mixout_w = (wog, bog, w_up_m.astype(BF16), w_up_a.astype(BF16), w_out.astype(BF16),
                w_router.T, b_router.reshape(N_EXPERTS, 1))
    return inproj_w, (wbd, pe, kg0), mixout_w


def _pick_tile(m, pref):
    t = pref
    while m % t:
        t //= 2
    return t


def kernel(x_prompt, x_sample, cache_nsa_kv, state_win_kv, state_mlstm_C, state_mlstm_n, state_mlstm_m, page_table, c_prompt, c_sample, w_ada, b_ada, g_mix, g_ffn, w_in, b_in, q_norm_g, k_norm_g, cmp_pe_k, cmp_pe_v, cmp_w_k, cmp_w_v, w_up_m, w_up_a, w_out, w_router, b_router, w_gu, b_gu, w_dn, b_dn):
    depth = w_in.shape[0]
    assert depth == 1
    B, T, D = x_prompt.shape
    DB, TS, _ = x_sample.shape
    n_pages = page_table.shape[1]
    past_len = n_pages * PAGE_SIZE
    wbuf = state_win_kv.shape[2]
    tp = SAMPLE_PAD_T
    assert TS <= tp and wbuf % tp == 0 and T % 128 == 0

    l = 0
    inproj_w, cmp_w, mixout_w = _prep_weights(
        w_in[l], b_in[l], q_norm_g[l], k_norm_g[l], cmp_pe_k[l], cmp_pe_v[l], cmp_w_k[l], cmp_w_v[l],
        w_up_m[l], w_up_a[l], w_out[l], w_router[l], b_router[l])
    wbd, pe, kg0 = cmp_w
    gmix = g_mix[l].reshape(1, D)
    gffn = g_ffn[l].reshape(1, D)

    nc = B + DB
    nc_pad = -(-nc // SUBLANES) * SUBLANES
    c_all = jnp.concatenate([c_prompt, c_sample, jnp.zeros((nc_pad - nc, D), F32)], axis=0)
    mod = _adaln(c_all, w_ada[l], b_ada[l])
    mod_p = mod[:B].reshape(B, 1, 6 * D)
    mod_s = jnp.repeat(mod[B:B + DB], tp, axis=0).reshape(1, DB * tp, 6 * D)

    mp = B * T
    tm = _pick_tile(T, 256)
    xp = x_prompt.reshape(mp, D)
    tabs_p = _rope_tables(jnp.arange(T, dtype=jnp.int32))
    mq, mk, mv, q, qr, rows, win, small, rows_t = _inproj(xp, mod_p, gmix, tabs_p, inproj_w, tm, T // tm, T // tm,
                                                          rows_t_batches=B)
    Lp = _pick_tile(T, 128)
    hm, C_p, n_p, m_p = _mlstm(mq, mk, mv, small, B, T, T, Lp)
    o_nsa = _nsa_prompt(q, qr, small, rows, win, wbd, pe, kg0, B, T)
    assert T % MOE_TM == 0
    ms_pad = -(-(DB * tp) // MOE_TM) * MOE_TM
    nt_p = mp // MOE_TM
    nt_all = nt_p + ms_pad // MOE_TM
    x1_p, xsl, info, cnt = _mixout(xp, hm, o_nsa, mod_p, gmix, gffn, mixout_w, T // MOE_TM, nt_all)

    ms = DB * tp
    xs_pad = jnp.concatenate([x_sample, jnp.zeros((DB, tp - TS, D), F32)], axis=1).reshape(ms, D)
    pos_s = past_len + jnp.tile(jnp.arange(tp, dtype=jnp.int32), DB)
    tabs_s = _rope_tables(pos_s)
    mq_s, mk_s, mv_s, q_s, qr_s, rows_s, win_s, small_s = _inproj(xs_pad, mod_s, gmix, tabs_s, inproj_w, ms, 1, 1)
    hm_s, C_s, n_s, m_s = _mlstm(mq_s, mk_s, mv_s, small_s, DB, tp, TS, tp,
                                 state=(state_mlstm_C[l], state_mlstm_n[l], state_mlstm_m[l]))
    cache2 = jnp.transpose(cache_nsa_kv[l], (0, 2, 3, 4, 1)).reshape(cache_nsa_kv.shape[1], 4 * LANES, PAGE_SIZE)
    winbuf = state_win_kv[l].reshape(DB, wbuf, 2 * LANES)
    o_nsa_s, win_out_s = _nsa_sample(page_table, cache2, q_s, qr_s, small_s, rows_s, win_s, winbuf,
                                     wbd, pe, kg0, TS)
    assert ms_pad == MOE_TM
    rpad = lambda a: jnp.concatenate([a, jnp.zeros((ms_pad - ms, a.shape[1]), a.dtype)], axis=0) if ms_pad > ms else a
    mod_sp = rpad(mod_s[0])[None]
    x1_s, xsl, info, cnt = _mixout(rpad(xs_pad), rpad(hm_s), rpad(o_nsa_s), mod_sp, gmix, gffn, mixout_w,
                                   1, nt_all, tile0=nt_p, shared=(xsl, info, cnt),
                                   t_mod=tp, t_valid=TS, m_valid=ms)

    ysl = _moe_experts(_moe_plan(cnt[:, :, 0]), xsl, w_gu[l], b_gu[l], w_dn[l], b_dn[l])
    y_p = _combine(ysl, info, x1_p, mod_p, T // MOE_TM).reshape(B, T, D)
    y_s_all = _combine(ysl, info, x1_s, mod_sp, 1, tile0=nt_p)
    valid = lambda a: a.reshape(DB, tp, -1)[:, :TS].reshape(DB * TS, -1)
    y_s = valid(y_s_all[:ms]).reshape(DB, TS, D)

    kv_p = jnp.transpose(rows_t.reshape(B, 4, A_KV, A_DH, T), (0, 4, 1, 2, 3))[None]
    kv_s = valid(rows_s).reshape(1, DB, TS, 4, A_KV, A_DH)
    wp = min(WINDOW, T)
    win_p = win.reshape(B, T, 2, A_KV, A_DH)[:, T - wp:][None]
    win_s_out = win_out_s.reshape(1, DB, wbuf, 2, A_KV, A_DH)
    return (y_p, y_s, kv_p, kv_s, win_p, win_s_out,
            C_p[None], n_p[None], m_p[None], C_s[None], n_s[None], m_s[None])
```

```python
import functools
import math

import jax
import jax.numpy as jnp
from jax import lax
from jax.experimental import pallas as pl
from jax.experimental.pallas import tpu as pltpu

F32 = jnp.float32
BF16 = jnp.bfloat16

D_MODEL = 1024
M_HEADS = 4
M_DH = 128
M_WIDTH = M_HEADS * M_DH
A_HEADS = 8
A_KV = 2
A_HPG = A_HEADS // A_KV
A_DH = 64
A_WIDTH = A_HEADS * A_DH
CMP_STRIDE = 16
CMP_LEN = 32
SEL_LEN = 64
N_SEL = 16
WINDOW = 512
PAGE_SIZE = 128
ROPE_THETA = 500000.0
ROT_DIM = A_DH // 4
ATT_SCALE = A_DH ** -0.5
N_EXPERTS = 32
TOP_K = 4
D_EXPERT = D_MODEL
SWIGLU_LIMIT = 7.0
SWIGLU_ALPHA = 1.702
EPS = 1e-6

OFF_MQ, OFF_MK, OFF_MV, OFF_MO = 0, M_WIDTH, 2 * M_WIDTH, 3 * M_WIDTH
OFF_MI = 4 * M_WIDTH
OFF_MF = OFF_MI + M_HEADS
OFF_AQ = OFF_MF + M_HEADS
OFF_AKV = OFF_AQ + A_WIDTH
OFF_AG = OFF_AKV + 6 * A_KV * A_DH
OFF_GA = OFF_AG + 3 * A_HEADS
OFF_GB = OFF_GA + D_MODEL
N_IN = OFF_GB + D_MODEL

LANES = 128
SUBLANES = 8
VMEM_LIMIT = 56 * 1024 * 1024

NEG_BIG = -1e30
M_INIT = -1e29
LOG2E = 1.4426950408889634
SAMPLE_PAD_T = 8


def _cparams(sem):
    return pltpu.CompilerParams(dimension_semantics=sem, vmem_limit_bytes=VMEM_LIMIT)


def _bdot(a, b):
    return jnp.dot(a.astype(BF16), b.astype(BF16), preferred_element_type=F32)


def _bdot_t(a, b):
    return lax.dot_general(a.astype(BF16), b.astype(BF16), (((1,), (1,)), ((), ())),
                           preferred_element_type=F32)


def _split(a):
    hi = a.astype(BF16)
    lo = (a - hi.astype(F32)).astype(BF16)
    return hi, lo


def _dot3(a, b):
    ah, al = _split(a)
    bh, bl = _split(b)
    return (jnp.dot(ah, bh, preferred_element_type=F32) + jnp.dot(al, bh, preferred_element_type=F32)
            + jnp.dot(ah, bl, preferred_element_type=F32))


def _dot2_exact_rhs(a, b_bf16):
    ah, al = _split(a)
    return jnp.dot(ah, b_bf16, preferred_element_type=F32) + jnp.dot(al, b_bf16, preferred_element_type=F32)


def _sigmoid(x):
    return 0.5 * jnp.tanh(0.5 * x) + 0.5


def _rmsnorm_rows(x, g):
    return x * lax.rsqrt(jnp.mean(x * x, axis=-1, keepdims=True) + EPS) * g


def _adaln_kernel(c_ref, w_ref, b_ref, o_ref):
    c = c_ref[...]
    s = c * _sigmoid(c)
    o_ref[...] = _dot3(s, w_ref[...]) + b_ref[...]


def _adaln(c, w, b):
    mc, d = c.shape
    n = w.shape[1]
    tn = 1024
    return pl.pallas_call(
        _adaln_kernel,
        grid=(n // tn,),
        in_specs=[pl.BlockSpec((mc, d), lambda j: (0, 0)),
                  pl.BlockSpec((d, tn), lambda j: (0, j)),
                  pl.BlockSpec((1, tn), lambda j: (0, j))],
        out_specs=pl.BlockSpec((mc, tn), lambda j: (0, j)),
        out_shape=jax.ShapeDtypeStruct((mc, n), F32),
        compiler_params=_cparams(("parallel",)),
        name="adaln",
    )(c, w, b.reshape(1, n))


def _head_norm(z, bd, gain):
    ms = _dot2_exact_rhs(z * z, bd)
    return z * lax.rsqrt(ms + EPS) * gain


def _rope(z, cos, s_prev, s_next):
    w = z.shape[1]
    rep = w // LANES
    if rep > 1:
        cos = jnp.concatenate([cos] * rep, axis=1)
        s_prev = jnp.concatenate([s_prev] * rep, axis=1)
        s_next = jnp.concatenate([s_next] * rep, axis=1)
    z_prev = pltpu.roll(z, ROT_DIM // 2, 1)
    z_next = pltpu.roll(z, w - ROT_DIM // 2, 1)
    return z * cos + z_prev * s_prev + z_next * s_next


def _inproj_kernel(x_ref, mod_ref, gmix_ref, cos_ref, sp_ref, sn_ref,
                   wm_ref, bm_ref, wq_ref, bq_ref, wkv_ref, bkv_ref, ws_ref, bs_ref,
                   qg_ref, kg_ref, bd_ref,
                   mq_ref, mk_ref, mv_ref, q_ref, qr_ref, rows_ref, win_ref, small_ref, rows_t_ref=None):
    x = x_ref[...]
    sh1 = mod_ref[:, 0:D_MODEL]
    sc1 = mod_ref[:, D_MODEL:2 * D_MODEL]
    h = _rmsnorm_rows(x, gmix_ref[...]) * (1.0 + sc1) + sh1
    hb = h.astype(BF16)

    mq_ref[...] = jnp.dot(hb, wm_ref[:, 0:M_WIDTH], preferred_element_type=F32) + bm_ref[:, 0:M_WIDTH]
    mk = jnp.dot(hb, wm_ref[:, M_WIDTH:2 * M_WIDTH], preferred_element_type=F32) + bm_ref[:, M_WIDTH:2 * M_WIDTH]
    mk_ref[...] = mk * (M_DH ** -0.5)
    mv_ref[...] = (jnp.dot(hb, wm_ref[:, 2 * M_WIDTH:3 * M_WIDTH], preferred_element_type=F32)
                   + bm_ref[:, 2 * M_WIDTH:3 * M_WIDTH])

    cos, sp, sn = cos_ref[...], sp_ref[...], sn_ref[...]
    zq = jnp.dot(hb, wq_ref[...], preferred_element_type=F32) + bq_ref[...]
    qn = _head_norm(zq, bd_ref[...], qg_ref[...])
    q_ref[...] = qn
    qr_ref[...] = _rope(qn, cos, sp, sn)

    zkv = jnp.dot(hb, wkv_ref[...], preferred_element_type=F32) + bkv_ref[...]
    bd2 = bd_ref[0:LANES, 0:LANES]
    ksel = _head_norm(zkv[:, 2 * LANES:3 * LANES], bd2, kg_ref[0:1, :])
    rows = jnp.concatenate([zkv[:, 0:2 * LANES], _rope(ksel, cos, sp, sn), zkv[:, 3 * LANES:4 * LANES]], axis=1)
    rows_ref[...] = rows
    if rows_t_ref is not None:
        rows_t_ref[...] = jnp.transpose(rows)
    kwin = _head_norm(zkv[:, 4 * LANES:5 * LANES], bd2, kg_ref[1:2, :])
    win_ref[:, 0:LANES] = _rope(kwin, cos, sp, sn)
    win_ref[:, LANES:2 * LANES] = zkv[:, 5 * LANES:6 * LANES]

    small_ref[...] = _dot3(h, ws_ref[...]) + bs_ref[...]


def _inproj(x2, mod3, gmix, tabs, wts, tm, tiles_per_mod, pos_tiles, rows_t_batches=None):
    m = x2.shape[0]
    cos_t, sp_t, sn_t = tabs
    (wm, bm, wq, bq, wkv, bkv, ws, bs, qg, kg, bd) = wts
    r = mod3.shape[1]
    row = lambda i: (i, 0)
    const = lambda i: (0, 0)
    tab = lambda i: (i % pos_tiles, 0)
    in_specs = [
        pl.BlockSpec((tm, D_MODEL), row),
        pl.BlockSpec((None, r, 6 * D_MODEL), lambda i: (i // tiles_per_mod, 0, 0)),
        pl.BlockSpec((1, D_MODEL), const),
        pl.BlockSpec((tm, LANES), tab), pl.BlockSpec((tm, LANES), tab), pl.BlockSpec((tm, LANES), tab),
        pl.BlockSpec(wm.shape, const), pl.BlockSpec(bm.shape, const),
        pl.BlockSpec(wq.shape, const), pl.BlockSpec(bq.shape, const),
        pl.BlockSpec(wkv.shape, const), pl.BlockSpec(bkv.shape, const),
        pl.BlockSpec(ws.shape, const), pl.BlockSpec(bs.shape, const),
        pl.BlockSpec(qg.shape, const), pl.BlockSpec(kg.shape, const), pl.BlockSpec(bd.shape, const),
    ]
    widths = (M_WIDTH, M_WIDTH, M_WIDTH, A_WIDTH, A_WIDTH, 4 * LANES, 2 * LANES, LANES)
    out_specs = [pl.BlockSpec((tm, w), row) for w in widths]
    out_shape = [jax.ShapeDtypeStruct((m, w), F32) for w in widths]
    if rows_t_batches is not None:
        out_specs.append(pl.BlockSpec((None, 4 * LANES, tm), lambda i: (i // tiles_per_mod, 0, i % tiles_per_mod)))
        out_shape.append(jax.ShapeDtypeStruct((rows_t_batches, 4 * LANES, m // rows_t_batches), F32))
    return pl.pallas_call(
        _inproj_kernel,
        grid=(m // tm,),
        in_specs=in_specs,
        out_specs=out_specs,
        out_shape=out_shape,
        compiler_params=_cparams(("parallel",)),
        name="inproj",
    )(x2, mod3, gmix, cos_t, sp_t, sn_t, wm, bm, wq, bq, wkv, bkv, ws, bs, qg, kg, bd)


def _log_sigmoid(x):
    return jnp.minimum(x, 0.0) - jnp.log(1.0 + jnp.exp(-jnp.abs(x)))


def _mlstm_kernel(*refs, L, t_valid, has_state):
    if has_state:
        q_ref, k_ref, v_ref, s_ref, c0_ref, n0_ref, m0_ref, h_ref, c_ref, n_ref, m_ref = refs
    else:
        q_ref, k_ref, v_ref, s_ref, h_ref, c_ref, n_ref, m_ref = refs
    c = pl.program_id(1)

    @pl.when(c == 0)
    def _():
        if has_state:
            c_ref[...] = c0_ref[...]
            n_ref[...] = n0_ref[...]
            m_ref[...] = m0_ref[...]
        else:
            c_ref[...] = jnp.zeros(c_ref.shape, F32)
            n_ref[...] = jnp.zeros(n_ref.shape, F32)
            m_ref[...] = jnp.zeros(m_ref.shape, F32)

    row = lax.broadcasted_iota(jnp.int32, (L, L), 0)
    col = lax.broadcasted_iota(jnp.int32, (L, L), 1)
    causal = col <= row
    eye = col == row
    tok_col = c * L + lax.broadcasted_iota(jnp.int32, (L, 1), 0)
    valid_col = tok_col < t_valid
    for hd in range(M_HEADS):
        lo, hi = hd * M_DH, (hd + 1) * M_DH
        q = q_ref[:, lo:hi]
        k = k_ref[:, lo:hi]
        v = v_ref[:, lo:hi]
        i_col = s_ref[:, hd:hd + 1]
        lf_col = _log_sigmoid(s_ref[:, M_HEADS + hd:M_HEADS + hd + 1])
        lf_col = jnp.where(valid_col, lf_col, 0.0)
        i_col = jnp.where(valid_col, i_col, -jnp.inf)
        i_row = jnp.sum(jnp.where(eye, i_col, 0.0), axis=0, keepdims=True)
        lf_row = jnp.sum(jnp.where(eye, lf_col, 0.0), axis=0, keepdims=True)
        b_col = jnp.sum(jnp.where(causal, lf_row, 0.0), axis=1, keepdims=True)
        b_row = jnp.sum(jnp.where(row <= col, lf_col, 0.0), axis=0, keepdims=True)
        m_prev = m_ref[:, hd:hd + 1]
        dmat = jnp.where(causal, b_col - b_row + i_row, -jnp.inf)
        inter = b_col + m_prev
        m_row = jnp.maximum(jnp.max(dmat, axis=1, keepdims=True), inter)
        w = jnp.exp(dmat - m_row)
        w_inter = jnp.exp(inter - m_row)
        s = _bdot_t(q, k) * w
        cm = c_ref[hd]
        nv = n_ref[hd]
        num = _bdot(s, v) + w_inter * _bdot_t(q, cm)
        den = jnp.sum(s, axis=1, keepdims=True) + w_inter * jnp.sum(q * nv, axis=1, keepdims=True)
        h_ref[:, lo:hi] = num / jnp.maximum(jnp.abs(den), jnp.exp(-m_row))
        b_last = b_col[L - 1:L, :]
        dec_col = b_last - b_col + i_col
        dec_row = b_last - b_row + i_row
        m_new = jnp.maximum(b_last + m_prev, jnp.max(dec_row, axis=1, keepdims=True))
        ws_col = jnp.exp(dec_col - m_new)
        wc = jnp.exp(b_last + m_prev - m_new)
        vw = (v * ws_col).astype(BF16)
        upd = lax.dot_general(vw, k.astype(BF16), (((0,), (0,)), ((), ())), preferred_element_type=F32)
        c_ref[hd] = wc * cm + upd
        n_ref[hd] = wc * nv + jnp.sum(k * ws_col, axis=0, keepdims=True)
        m_ref[:, hd:hd + 1] = m_new


def _mlstm(mq, mk, mv, small, nb, t_pad, t_valid, L, state=None):
    nc = t_pad // L
    has_state = state is not None
    blk = lambda b, c: (b * nc + c, 0)
    st4 = lambda b, c: (b, 0, 0, 0)
    st3 = lambda b, c: (b, 0, 0)
    in_specs = [pl.BlockSpec((L, M_WIDTH), blk)] * 3 + [pl.BlockSpec((L, LANES), blk)]
    args = [mq, mk, mv, small]
    if has_state:
        c0, n0, m0 = state
        in_specs += [pl.BlockSpec((None, M_HEADS, M_DH, M_DH), st4),
                     pl.BlockSpec((None, M_HEADS, 1, M_DH), st4),
                     pl.BlockSpec((None, 1, M_HEADS), st3)]
        args += [c0, n0.reshape(nb, M_HEADS, 1, M_DH), m0.reshape(nb, 1, M_HEADS)]
    out_specs = [pl.BlockSpec((L, M_WIDTH), blk),
                 pl.BlockSpec((None, M_HEADS, M_DH, M_DH), st4),
                 pl.BlockSpec((None, M_HEADS, 1, M_DH), st4),
                 pl.BlockSpec((None, 1, M_HEADS), st3)]
    out_shape = [jax.ShapeDtypeStruct((nb * t_pad, M_WIDTH), F32),
                 jax.ShapeDtypeStruct((nb, M_HEADS, M_DH, M_DH), F32),
                 jax.ShapeDtypeStruct((nb, M_HEADS, 1, M_DH), F32),
                 jax.ShapeDtypeStruct((nb, 1, M_HEADS), F32)]
    h, cs, ns, ms = pl.pallas_call(
        functools.partial(_mlstm_kernel, L=L, t_valid=t_valid, has_state=has_state),
        grid=(nb, nc),
        in_specs=in_specs,
        out_specs=out_specs,
        out_shape=out_shape,
        compiler_params=_cparams(("parallel", "arbitrary")),
        name="mlstm",
    )(*args)
    return h, cs, ns.reshape(nb, M_HEADS, M_DH), ms.reshape(nb, M_HEADS)


def _stack_heads(qt, g):
    t = qt.shape[0]
    z = jnp.zeros((t, A_DH), F32)
    parts = []
    for hh in range(A_HPG):
        hd = g * A_HPG + hh
        qh = qt[:, hd * A_DH:(hd + 1) * A_DH] * (ATT_SCALE * LOG2E)
        parts.append(jnp.concatenate([qh, z], axis=1) if g == 0 else jnp.concatenate([z, qh], axis=1))
    return jnp.concatenate(parts, axis=0).astype(BF16)


def _gate_cols(small, g, br):
    cols = []
    for hh in range(A_HPG):
        c0 = 2 * M_HEADS + (g * A_HPG + hh) * 3 + br
        cols.append(_sigmoid(small[:, c0:c0 + 1]))
    return jnp.concatenate(cols, axis=0)


def _compress(k_ref, v_ref, nseg, wbd_ref, pe_ref, kg0):
    acc_lo = jnp.zeros((nseg, 2 * LANES), F32)
    acc_hi = jnp.zeros((nseg, 2 * LANES), F32)
    for l in range(CMP_STRIDE):
        xl = jnp.concatenate([k_ref[pl.ds(l, nseg, stride=CMP_STRIDE), :],
                              v_ref[pl.ds(l, nseg, stride=CMP_STRIDE), :]], axis=1)
        acc_lo = acc_lo + _bdot(xl + pe_ref[l], wbd_ref[l])
        acc_hi = acc_hi + _bdot(xl + pe_ref[CMP_STRIDE + l], wbd_ref[CMP_STRIDE + l])
    return _compress_finish(acc_lo, acc_hi, nseg, kg0)


def _compress_grouped(x_ref, nseg, wbd_ref, pe_ref, kg0):
    acc_lo = jnp.zeros((nseg, 2 * LANES), F32)
    acc_hi = jnp.zeros((nseg, 2 * LANES), F32)
    pe_lo = jnp.zeros((SUBLANES, 2 * LANES), F32)
    pe_hi = jnp.zeros((SUBLANES, 2 * LANES), F32)
    for l in range(CMP_STRIDE):
        xl = x_ref[l].astype(BF16)
        acc_lo = acc_lo + jnp.dot(xl, wbd_ref[l], preferred_element_type=F32)
        acc_hi = acc_hi + jnp.dot(xl, wbd_ref[CMP_STRIDE + l], preferred_element_type=F32)
        pe_lo = pe_lo + _bdot(jnp.broadcast_to(pe_ref[l], (SUBLANES, 2 * LANES)), wbd_ref[l])
        pe_hi = pe_hi + _bdot(jnp.broadcast_to(pe_ref[CMP_STRIDE + l], (SUBLANES, 2 * LANES)),
                              wbd_ref[CMP_STRIDE + l])
    return _compress_finish(acc_lo + pe_lo[0:1, :], acc_hi + pe_hi[0:1, :], nseg, kg0)


def _compress_finish(acc_lo, acc_hi, nseg, kg0):
    kv = acc_lo + pltpu.roll(acc_hi, nseg - 1, 0)
    kc = kv[:, 0:LANES]
    vc = kv[:, LANES:2 * LANES]
    lane = lax.broadcasted_iota(jnp.int32, (nseg, LANES), 1)
    sq = kc * kc
    ms0 = jnp.sum(jnp.where(lane < A_DH, sq, 0.0), axis=1, keepdims=True) * (1.0 / A_DH)
    ms1 = jnp.sum(jnp.where(lane >= A_DH, sq, 0.0), axis=1, keepdims=True) * (1.0 / A_DH)
    ms = jnp.where(lane < A_DH, ms0, ms1)
    kc = kc * lax.rsqrt(ms + EPS) * kg0
    return kc, vc


def _cmp_branch(qn_g, kc_b, vc_b, tpos_rows, nseg, n_tok):
    s = _bdot_t(qn_g, kc_b)
    nidx = lax.broadcasted_iota(jnp.int32, (1, nseg), 1)
    vis = (nidx * CMP_STRIDE + (CMP_LEN - 1)) <= tpos_rows
    sm = jnp.where(vis, s, NEG_BIG)
    mx = jnp.max(sm, axis=1, keepdims=True)
    e = jnp.where(vis, jnp.exp2(sm - mx), 0.0)
    d = jnp.sum(e, axis=1, keepdims=True)
    p = e / jnp.where(d > 0, d, 1.0)
    o = _bdot(p, vc_b)
    imp = p[0:n_tok]
    for hh in range(1, A_HPG):
        imp = imp + p[hh * n_tok:(hh + 1) * n_tok]
    return o, imp


def _masked_attn_direct(q_g, k_parts, v_parts, allowed_parts, feature_major):
    ss = [jnp.where(al, _bdot(q_g, kk) if fm else _bdot_t(q_g, kk), NEG_BIG)
          for kk, al, fm in zip(k_parts, allowed_parts, feature_major)]
    mx = ss[0].max(axis=1, keepdims=True)
    for s in ss[1:]:
        mx = jnp.maximum(mx, s.max(axis=1, keepdims=True))
    num = None
    den = None
    for s, al, vv, fm in zip(ss, allowed_parts, v_parts, feature_major):
        e = jnp.where(al, jnp.exp2(s - mx), 0.0)
        dd = jnp.sum(e, axis=1, keepdims=True)
        oo = _bdot_t(e, vv) if fm else _bdot(e, vv)
        num = oo if num is None else num + oo
        den = dd if den is None else den + dd
    return num / jnp.where(den > 0, den, 1.0)


def _assemble_heads(o_groups, n_tok):
    pieces = []
    for g in range(A_KV):
        for hh in range(A_HPG):
            pieces.append(o_groups[g][hh * n_tok:(hh + 1) * n_tok, g * A_DH:(g + 1) * A_DH])
    return jnp.concatenate(pieces, axis=1)


def _lane_tile(a, width):
    rep = width // LANES
    return a if rep == 1 else jnp.concatenate([a] * rep, axis=1)


def _add_bias(s, bias):
    t, k = bias.shape
    return (s.reshape(A_HPG, t, k) + bias[None]).reshape(A_HPG * t, k)


def _nsa_prompt_kernel(q_ref, qr_ref, small_ref, rows_ref, win_ref, wbd_ref, pe_ref, kg0_ref,
                       pool_ref, o_ref,
                       kraw_sc, vraw_sc, kc_sc, vc_sc, m_sc, l_sc, acc_sc, *, T, tq, kc_len):
    qi = pl.program_id(1)
    nseg = T // CMP_STRIDE
    nsb = T // SEL_LEN

    @pl.when(qi == 0)
    def _():
        kraw_sc[...] = rows_ref[:, 0:LANES]
        vraw_sc[...] = rows_ref[:, LANES:2 * LANES]
        kc, vc = _compress(kraw_sc, vraw_sc, nseg, wbd_ref, pe_ref, kg0_ref[...])
        kc_sc[...] = kc
        vc_sc[...] = vc

    t0 = qi * tq
    tpos_col = t0 + lax.broadcasted_iota(jnp.int32, (tq, 1), 0)
    tpos_rows = jnp.concatenate([tpos_col] * A_HPG, axis=0)
    tpos_lane = t0 + lax.broadcasted_iota(jnp.int32, (1, tq), 1)
    q = q_ref[...]
    qr = qr_ref[...]
    small = small_ref[...]
    kc_b = kc_sc[...].astype(BF16)
    vc_b = vc_sc[...].astype(BF16)
    bidx = lax.broadcasted_iota(jnp.int32, (nsb, tq), 0)
    cur = tpos_lane // SEL_LEN
    r4 = A_HPG * tq
    qr_gs = [_stack_heads(qr, g) for g in range(A_KV)]
    o_cmps = []
    sel_bs = []
    for g in range(A_KV):
        qn_g = _stack_heads(q, g)
        o_cmp, imp = _cmp_branch(qn_g, kc_b, vc_b, tpos_rows, nseg, tq)
        o_cmps.append(o_cmp)
        imp_sel = _dot2_exact_rhs(imp, pool_ref[...])
        imp_t = jnp.transpose(imp_sel)[0:nsb, :]
        val = jnp.where(bidx < cur, imp_t, -1.0)
        rank = jnp.zeros((nsb, tq), F32)
        for bp in range(nsb):
            vb = val[bp:bp + 1, :]
            ahead = jnp.where(vb > val, 1.0, jnp.where((vb == val) & (bidx > bp), 1.0, 0.0))
            rank = rank + ahead
        sel_t = jnp.where(((rank < (N_SEL - 1)) & (bidx < cur)) | (bidx == cur), 1.0, 0.0)
        if nsb < LANES:
            sel_t = jnp.concatenate([sel_t, jnp.zeros((LANES - nsb, tq), F32)], axis=0)
        sel_bs.append(jnp.transpose(sel_t).astype(BF16))

    m_sc[...] = jnp.full(m_sc.shape, M_INIT, F32)
    l_sc[...] = jnp.zeros(l_sc.shape, F32)
    acc_sc[...] = jnp.zeros(acc_sc.shape, F32)

    def sel_body(c, carry):
        k0 = pl.multiple_of(c * kc_len, kc_len)
        kb = rows_ref[pl.ds(k0, kc_len), 2 * LANES:3 * LANES].astype(BF16)
        vb = rows_ref[pl.ds(k0, kc_len), 3 * LANES:4 * LANES].astype(BF16)
        kpos = k0 + lax.broadcasted_iota(jnp.int32, (1, kc_len), 1)
        causal = kpos <= tpos_col
        kblk = (k0 + lax.broadcasted_iota(jnp.int32, (LANES, kc_len), 1)) // SEL_LEN
        expand = jnp.where(kblk == lax.broadcasted_iota(jnp.int32, (LANES, kc_len), 0), 1.0, 0.0).astype(BF16)
        for g in range(A_KV):
            mk = jnp.dot(sel_bs[g], expand, preferred_element_type=F32)
            bias = jnp.where(causal, (mk - 1.0) * (-NEG_BIG), NEG_BIG)
            sm = _add_bias(_bdot_t(qr_gs[g], kb), bias)
            m_prev = m_sc[g]
            m_new = jnp.maximum(m_prev, jnp.max(sm, axis=1, keepdims=True))
            alpha = jnp.exp2(m_prev - m_new)
            p = jnp.exp2(sm - _lane_tile(m_new, kc_len))
            l_sc[g] = alpha * l_sc[g] + jnp.sum(p, axis=1, keepdims=True)
            acc_sc[g] = alpha * acc_sc[g] + _bdot(p, vb)
            m_sc[g] = m_new
        return carry

    lax.fori_loop(0, (t0 + tq + kc_len - 1) // kc_len, sel_body, 0)

    wk = min(WINDOW + tq, T)
    w0 = pl.multiple_of(jnp.clip(t0 + tq - wk, 0, T - wk), tq)
    kw = win_ref[pl.ds(w0, wk), 0:LANES].astype(BF16)
    vw = win_ref[pl.ds(w0, wk), LANES:2 * LANES].astype(BF16)
    wdiff = tpos_col - (w0 + lax.broadcasted_iota(jnp.int32, (1, wk), 1))
    wbias = jnp.where((wdiff >= 0) & (wdiff < WINDOW), 0.0, NEG_BIG)

    o_groups = []
    for g in range(A_KV):
        l = l_sc[g]
        o_sel = acc_sc[g] / jnp.where(l > 0, l, 1.0)
        sw = _add_bias(_bdot_t(qr_gs[g], kw), wbias)
        mw = jnp.broadcast_to(jnp.max(sw, axis=1, keepdims=True), (r4, LANES))
        pw = jnp.exp2(sw - _lane_tile(mw, wk))
        o_win = _bdot(pw, vw) / jnp.broadcast_to(jnp.sum(pw, axis=1, keepdims=True), (r4, LANES))
        o_groups.append(_gate_cols(small, g, 0) * o_cmps[g] + _gate_cols(small, g, 1) * o_sel
                        + _gate_cols(small, g, 2) * o_win)
    o_ref[...] = _assemble_heads(o_groups, tq)


def _nsa_prompt(q, qr, small, rows, win, wbd, pe, kg0, nb, T):
    tq = 128
    kc_len = _pick_tile(T, 512)
    nq = T // tq
    nseg = T // CMP_STRIDE
    nsb = T // SEL_LEN
    pool = (jnp.arange(nseg)[:, None] // (SEL_LEN // CMP_STRIDE) == jnp.arange(LANES)[None, :]).astype(BF16)
    tile = lambda b, i: (b * nq + i, 0)
    per_b = lambda b, i: (b, 0)
    c2 = lambda b, i: (0, 0)
    c3 = lambda b, i: (0, 0, 0)
    r4 = A_HPG * tq
    return pl.pallas_call(
        functools.partial(_nsa_prompt_kernel, T=T, tq=tq, kc_len=kc_len),
        grid=(nb, nq),
        in_specs=[pl.BlockSpec((tq, A_WIDTH), tile), pl.BlockSpec((tq, A_WIDTH), tile),
                  pl.BlockSpec((tq, LANES), tile),
                  pl.BlockSpec((T, 4 * LANES), per_b), pl.BlockSpec((T, 2 * LANES), per_b),
                  pl.BlockSpec(wbd.shape, c3), pl.BlockSpec(pe.shape, c3), pl.BlockSpec(kg0.shape, c2),
                  pl.BlockSpec(pool.shape, c2)],
        out_specs=pl.BlockSpec((tq, A_WIDTH), tile),
        out_shape=jax.ShapeDtypeStruct((nb * T, A_WIDTH), F32),
        scratch_shapes=[pltpu.VMEM((T, LANES), F32), pltpu.VMEM((T, LANES), F32),
                        pltpu.VMEM((nseg, LANES), F32), pltpu.VMEM((nseg, LANES), F32),
                        pltpu.VMEM((A_KV, r4, LANES), F32), pltpu.VMEM((A_KV, r4, LANES), F32),
                        pltpu.VMEM((A_KV, r4, LANES), F32)],
        compiler_params=_cparams(("parallel", "arbitrary")),
        name="nsa_prompt",
    )(q, qr, small, rows, win, wbd, pe, kg0, pool)


def _nsa_sample_kernel(pt_ref, cache_ref, q_ref, qr_ref, small_ref, rows_ref, winnew_ref, winbuf_ref,
                       wbd_ref, pe_ref, kg0_ref, pool_ref, expand_ref,
                       o_ref, winout_ref,
                       cmp_buf, sel_buf, xperm_sc, sems, *, n_pages, past_len, t_valid):
    b = pl.program_id(0)
    nb = pl.num_programs(0)
    tp = SAMPLE_PAD_T
    nseg = past_len // CMP_STRIDE
    nsb = past_len // SEL_LEN
    wbuf = winbuf_ref.shape[0]

    def page_copies(bb, p, phase):
        page = pt_ref[bb * n_pages + p]
        dst_lanes = pl.ds(pl.multiple_of(p * PAGE_SIZE, PAGE_SIZE), PAGE_SIZE)
        if phase == 0:
            return [pltpu.make_async_copy(cache_ref.at[page, pl.ds(0, 2 * LANES), :],
                                          cmp_buf.at[:, dst_lanes], sems.at[0])]
        return [pltpu.make_async_copy(cache_ref.at[page, pl.ds(2 * LANES, 2 * LANES), :],
                                      sel_buf.at[:, dst_lanes], sems.at[1])]

    def start_all(bb, phase):
        def body(p, c):
            for cp in page_copies(bb, p, phase):
                cp.start()
            return c
        lax.fori_loop(0, n_pages, body, 0)

    def wait_all(bb, phase):
        def body(p, c):
            for cp in page_copies(bb, p, phase):
                cp.wait()
            return c
        lax.fori_loop(0, n_pages, body, 0)

    @pl.when(b == 0)
    def _():
        start_all(b, 0)

    start_all(b, 1)
    wait_all(b, 0)

    seg_pp = PAGE_SIZE // CMP_STRIDE
    pr = lax.broadcasted_iota(jnp.int32, (PAGE_SIZE, PAGE_SIZE), 0)
    pc = lax.broadcasted_iota(jnp.int32, (PAGE_SIZE, PAGE_SIZE), 1)
    perm = jnp.where(pc == CMP_STRIDE * (pr % seg_pp) + pr // seg_pp, 1.0, 0.0).astype(BF16)
    for p in range(n_pages):
        xp = _bdot_t(perm, cmp_buf[:, p * PAGE_SIZE:(p + 1) * PAGE_SIZE])
        for l in range(CMP_STRIDE):
            xperm_sc[l, p * seg_pp:(p + 1) * seg_pp, :] = xp[l * seg_pp:(l + 1) * seg_pp, :]
    kc, vc = _compress_grouped(xperm_sc, nseg, wbd_ref, pe_ref, kg0_ref[...])
    kc_b = kc.astype(BF16)
    vc_b = vc.astype(BF16)
    q = q_ref[...]
    qr = qr_ref[...]
    small = small_ref[...]
    tpos_col = past_len + lax.broadcasted_iota(jnp.int32, (tp, 1), 0)
    tpos_rows = jnp.concatenate([tpos_col] * A_HPG, axis=0)
    bp_idx = lax.broadcasted_iota(jnp.int32, (nsb, nsb), 0)
    b_idx = lax.broadcasted_iota(jnp.int32, (nsb, nsb), 1)
    o_cmps = []
    sels = []
    for g in range(A_KV):
        qn_g = _stack_heads(q, g)
        o_cmp, imp = _cmp_branch(qn_g, kc_b, vc_b, tpos_rows, nseg, tp)
        o_cmps.append(o_cmp)
        imp_sel = _dot2_exact_rhs(imp, pool_ref[...])
        imp_pad = jnp.concatenate([imp_sel, jnp.zeros((nsb - tp, nsb), F32)], axis=0)
        imp_t = jnp.transpose(imp_pad)
        rows_sel = []
        for t in range(tp):
            if t < t_valid:
                row_t = imp_sel[t:t + 1, :]
                col_t = imp_t[:, t:t + 1]
                ahead = jnp.where(col_t > row_t, 1.0, jnp.where((col_t == row_t) & (bp_idx < b_idx), 1.0, 0.0))
                rank = jnp.sum(ahead, axis=0, keepdims=True)
                rows_sel.append(jnp.where(rank < (N_SEL - 1), 1.0, 0.0))
            else:
                rows_sel.append(jnp.zeros((1, nsb), F32))
        sels.append(jnp.concatenate(rows_sel, axis=0).astype(BF16))

    @pl.when(b + 1 < nb)
    def _():
        start_all(b + 1, 0)

    wait_all(b, 1)

    new_idx = lax.broadcasted_iota(jnp.int32, (tp, tp), 1)
    tok_idx = lax.broadcasted_iota(jnp.int32, (tp, tp), 0)
    new_ok = jnp.concatenate([jnp.where(new_idx <= tok_idx, 1.0, 0.0)] * A_HPG, axis=0) > 0.5
    wpos = past_len - wbuf + lax.broadcasted_iota(jnp.int32, (1, wbuf), 1)
    wdiff = tpos_col - wpos
    win_ok = jnp.concatenate([jnp.where((wdiff >= 0) & (wdiff < WINDOW), 1.0, 0.0)] * A_HPG, axis=0) > 0.5
    k_past = sel_buf[0:LANES, :].astype(BF16)
    v_past = sel_buf[LANES:2 * LANES, :].astype(BF16)
    k_new = rows_ref[:, 2 * LANES:3 * LANES]
    v_new = rows_ref[:, 3 * LANES:4 * LANES]
    kw_past = winbuf_ref[:, 0:LANES]
    vw_past = winbuf_ref[:, LANES:2 * LANES]
    kw_new = winnew_ref[:, 0:LANES]
    vw_new = winnew_ref[:, LANES:2 * LANES]
    o_groups = []
    for g in range(A_KV):
        qr_g = _stack_heads(qr, g)
        mk = jnp.dot(sels[g], expand_ref[...], preferred_element_type=F32)
        past_ok = jnp.concatenate([mk] * A_HPG, axis=0) > 0.5
        o_sel = _masked_attn_direct(qr_g, [k_past, k_new], [v_past, v_new], [past_ok, new_ok], [True, False])
        o_win = _masked_attn_direct(qr_g, [kw_past, kw_new], [vw_past, vw_new], [win_ok, new_ok], [False, False])
        o_groups.append(_gate_cols(small, g, 0) * o_cmps[g] + _gate_cols(small, g, 1) * o_sel
                        + _gate_cols(small, g, 2) * o_win)
    o_ref[...] = _assemble_heads(o_groups, tp)

    wb = winbuf_ref[...]
    rolled = pltpu.roll(wb, wbuf - t_valid, 0)
    newr = pltpu.roll(winnew_ref[...], tp - t_valid, 0)
    sub = lax.broadcasted_iota(jnp.int32, (tp, 2 * LANES), 0)
    winout_ref[0:wbuf - tp, :] = rolled[0:wbuf - tp, :]
    winout_ref[wbuf - tp:wbuf, :] = jnp.where(sub < tp - t_valid, rolled[wbuf - tp:wbuf, :], newr)


def _nsa_sample(page_table, cache, q, qr, small, rows, winnew, winbuf, wbd, pe, kg0, t_valid):
    nb, n_pages = page_table.shape
    past_len = n_pages * PAGE_SIZE
    nseg = past_len // CMP_STRIDE
    nsb = past_len // SEL_LEN
    tp = SAMPLE_PAD_T
    wbuf = winbuf.shape[1]
    pool = (jnp.arange(nseg)[:, None] // (SEL_LEN // CMP_STRIDE) == jnp.arange(nsb)[None, :]).astype(BF16)
    expand = (jnp.arange(nsb)[:, None] == jnp.arange(past_len)[None, :] // SEL_LEN).astype(BF16)
    tile = lambda b, pt: (b, 0)
    c2 = lambda b, pt: (0, 0)
    c3 = lambda b, pt: (0, 0, 0)
    gs = pltpu.PrefetchScalarGridSpec(
        num_scalar_prefetch=1,
        grid=(nb,),
        in_specs=[pl.BlockSpec(memory_space=pl.ANY),
                  pl.BlockSpec((tp, A_WIDTH), tile), pl.BlockSpec((tp, A_WIDTH), tile),
                  pl.BlockSpec((tp, LANES), tile), pl.BlockSpec((tp, 4 * LANES), tile),
                  pl.BlockSpec((tp, 2 * LANES), tile),
                  pl.BlockSpec((None, wbuf, 2 * LANES), lambda b, pt: (b, 0, 0)),
                  pl.BlockSpec(wbd.shape, c3), pl.BlockSpec(pe.shape, c3), pl.BlockSpec(kg0.shape, c2),
                  pl.BlockSpec(pool.shape, c2), pl.BlockSpec(expand.shape, c2)],
        out_specs=[pl.BlockSpec((tp, A_WIDTH), tile),
                   pl.BlockSpec((None, wbuf, 2 * LANES), lambda b, pt: (b, 0, 0))],
        scratch_shapes=[pltpu.VMEM((2 * LANES, past_len), F32), pltpu.VMEM((2 * LANES, past_len), F32),
                        pltpu.VMEM((CMP_STRIDE, past_len // CMP_STRIDE, 2 * LANES), F32),
                        pltpu.SemaphoreType.DMA((2,))],
    )
    return pl.pallas_call(
        functools.partial(_nsa_sample_kernel, n_pages=n_pages, past_len=past_len, t_valid=t_valid),
        grid_spec=gs,
        out_shape=[jax.ShapeDtypeStruct((nb * tp, A_WIDTH), F32),
                   jax.ShapeDtypeStruct((nb, wbuf, 2 * LANES), F32)],
        compiler_params=_cparams(("arbitrary",)),
        name="nsa_sample",
    )(page_table.reshape(-1), cache, q, qr, small, rows, winnew, winbuf, wbd, pe, kg0, pool, expand)


MOE_TM = 256
SEG_ALIGN = 8
SEG_BITS = (256, 128, 64, 32, 16, 8)
MOE_RL = -(-(MOE_TM * TOP_K + N_EXPERTS * (SEG_ALIGN - 1)) // LANES) * LANES


def _pack_halves(x, bf16_exact=False):
    w = x.shape[1] // 2
    bits = lax.bitcast_convert_type(x if bf16_exact else x.astype(BF16).astype(F32), jnp.uint32)
    return (bits[:, :w] & jnp.uint32(0xFFFF0000)) | (bits[:, w:] >> 16)


def _unpack_halves(u):
    hi = lax.bitcast_convert_type(u & jnp.uint32(0xFFFF0000), F32).astype(BF16)
    lo = lax.bitcast_convert_type(u << 16, F32).astype(BF16)
    return hi, lo


def _route_and_sort(h2, wrt_ref, brt_ref, xsl_ref, info_ref, cnt_ref, tm, t_mod, t_valid, m_valid):
    ne = N_EXPERTS
    h2b = h2.astype(BF16)
    h2l = (h2 - h2b.astype(F32)).astype(BF16)
    wh, wl = _split(wrt_ref[...])
    lt = _bdot_t(wh, h2b) + _bdot_t(wl, h2b) + _bdot_t(wh, h2l) + brt_ref[...]
    eidx = lax.broadcasted_iota(jnp.int32, (ne, tm), 0)
    rank = jnp.zeros((ne, tm), F32)
    for ep in range(ne):
        v = lt[ep:ep + 1, :]
        rank = rank + jnp.where(v > lt, 1.0, jnp.where((v == lt) & (eidx > ep), 1.0, 0.0))
    sel = rank < TOP_K
    if t_mod is not None:
        tok = pl.program_id(0) * tm + lax.broadcasted_iota(jnp.int32, (1, tm), 1)
        sel = sel & ((tok % t_mod) < t_valid) & (tok < m_valid)
    mx = jnp.max(jnp.where(sel, lt, NEG_BIG), axis=0, keepdims=True)
    ex = jnp.where(sel, jnp.exp(lt - mx), 0.0)
    den = jnp.sum(ex, axis=0, keepdims=True)
    gate = ex / jnp.where(den > 0, den, 1.0)
    self_ = jnp.where(sel, 1.0, 0.0)
    selb = self_.astype(BF16)
    er = lax.broadcasted_iota(jnp.int32, (ne, ne), 0)
    ec = lax.broadcasted_iota(jnp.int32, (ne, ne), 1)
    c = jnp.dot(jnp.where(ec <= er, 1.0, 0.0).astype(BF16), selb, preferred_element_type=F32)
    tr = lax.broadcasted_iota(jnp.int32, (tm, tm), 0)
    tc = lax.broadcasted_iota(jnp.int32, (tm, tm), 1)
    rk = jnp.dot(selb, jnp.where(tr < tc, 1.0, 0.0).astype(BF16), preferred_element_type=F32)
    cnt = jnp.sum(self_, axis=1, keepdims=True)
    cnt_al = jnp.floor((cnt + (SEG_ALIGN - 1)) * (1.0 / SEG_ALIGN)) * SEG_ALIGN
    cnt_b = jnp.broadcast_to(cnt_al, (ne, LANES))
    cnt_ref[...] = cnt_b
    off = jnp.dot(jnp.where(ec < er, 1.0, 0.0).astype(BF16), cnt_b.astype(BF16), preferred_element_type=F32)
    rowidx = off[:, 0:1] + rk
    rows_k, gates_k, exps_k = [], [], []
    for k in range(1, TOP_K + 1):
        mk = sel & (c == k)
        has = jnp.sum(jnp.where(mk, 1.0, 0.0), axis=0, keepdims=True)
        rows_k.append(jnp.sum(jnp.where(mk, rowidx, 0.0), axis=0, keepdims=True) + has - 1.0)
        gates_k.append(jnp.sum(jnp.where(mk, gate, 0.0), axis=0, keepdims=True))
        exps_k.append(jnp.sum(jnp.where(mk, eidx.astype(F32), 0.0), axis=0, keepdims=True))
    info_ref[...] = jnp.concatenate(rows_k + gates_k + exps_k + [jnp.zeros((4, tm), F32)], axis=0)
    ridx = lax.broadcasted_iota(jnp.int32, (MOE_RL, tm), 0).astype(F32)
    perm = jnp.zeros((MOE_RL, tm), F32)
    for k in range(TOP_K):
        perm = perm + jnp.where(ridx == rows_k[k], 1.0, 0.0)
    xs = jnp.dot(perm.astype(BF16), h2b, preferred_element_type=F32)
    xsl_ref[...] = _pack_halves(xs, bf16_exact=True)


def _mixout_kernel(x_ref, hm_ref, on_ref, mod_ref, gmix_ref, gffn_ref,
                   wog_ref, bog_ref, wum_ref, wua_ref, wout_ref, wrt_ref, brt_ref,
                   x1_ref, xsl_ref, info_ref, cnt_ref, *, tm, t_mod, t_valid, m_valid, n_real):
    if n_real is not None:
        @pl.when(pl.program_id(0) >= n_real)
        def _():
            xsl_ref[...] = jnp.zeros(xsl_ref.shape, jnp.uint32)
            info_ref[...] = jnp.zeros(info_ref.shape, F32)
            cnt_ref[...] = jnp.zeros(cnt_ref.shape, F32)

        @pl.when(pl.program_id(0) < n_real)
        def _():
            _mixout_body(x_ref, hm_ref, on_ref, mod_ref, gmix_ref, gffn_ref, wog_ref, bog_ref, wum_ref,
                         wua_ref, wout_ref, wrt_ref, brt_ref, x1_ref, xsl_ref, info_ref, cnt_ref,
                         tm, t_mod, t_valid, m_valid)
    else:
        _mixout_body(x_ref, hm_ref, on_ref, mod_ref, gmix_ref, gffn_ref, wog_ref, bog_ref, wum_ref,
                     wua_ref, wout_ref, wrt_ref, brt_ref, x1_ref, xsl_ref, info_ref, cnt_ref,
                     tm, t_mod, t_valid, m_valid)


def _mixout_body(x_ref, hm_ref, on_ref, mod_ref, gmix_ref, gffn_ref,
                 wog_ref, bog_ref, wum_ref, wua_ref, wout_ref, wrt_ref, brt_ref,
                 x1_ref, xsl_ref, info_ref, cnt_ref, tm, t_mod, t_valid, m_valid):
    d = D_MODEL
    x = x_ref[...]
    sh1, sc1, gt1 = mod_ref[:, 0:d], mod_ref[:, d:2 * d], mod_ref[:, 2 * d:3 * d]
    sh2, sc2 = mod_ref[:, 3 * d:4 * d], mod_ref[:, 4 * d:5 * d]
    h = _rmsnorm_rows(x, gmix_ref[...]) * (1.0 + sc1) + sh1
    hb = h.astype(BF16)
    mo = jnp.dot(hb, wog_ref[:, 0:M_WIDTH], preferred_element_type=F32) + bog_ref[:, 0:M_WIDTH]
    ym = _bdot(_sigmoid(mo) * hm_ref[...], wum_ref[...])
    ya = _bdot(on_ref[...], wua_ref[...])
    ga = jnp.dot(hb, wog_ref[:, M_WIDTH:M_WIDTH + d], preferred_element_type=F32) + bog_ref[:, M_WIDTH:M_WIDTH + d]
    u = _sigmoid(ga) * ym
    gb = (jnp.dot(hb, wog_ref[:, M_WIDTH + d:M_WIDTH + 2 * d], preferred_element_type=F32)
          + bog_ref[:, M_WIDTH + d:M_WIDTH + 2 * d])
    u = u + _sigmoid(gb) * ya
    x1 = x + gt1 * _bdot(u, wout_ref[...])
    x1_ref[...] = x1
    h2 = _rmsnorm_rows(x1, gffn_ref[...]) * (1.0 + sc2) + sh2
    _route_and_sort(h2, wrt_ref, brt_ref, xsl_ref, info_ref, cnt_ref, tm, t_mod, t_valid, m_valid)


def _mixout_with_shared(*refs, n_shared, **kw):
    n_in = 13
    _mixout_kernel(*refs[:n_in], *refs[n_in + n_shared:], **kw)


def _mixout(x2, hm, on, mod3, gmix, gffn, wts, tiles_per_mod, nt_total, tile0=0, shared=None,
            t_mod=None, t_valid=None, m_valid=None):
    m = x2.shape[0]
    tm = MOE_TM
    nt = m // tm
    (wog, bog, wum, wua, wout, wr, br) = wts
    r = mod3.shape[1]
    n_extra = nt_total - tile0 - nt if shared is None else 0
    row = lambda i: (jnp.minimum(i, nt - 1), 0)
    const = lambda i: (0, 0)
    in_specs = [pl.BlockSpec((tm, D_MODEL), row), pl.BlockSpec((tm, M_WIDTH), row),
                pl.BlockSpec((tm, A_WIDTH), row),
                pl.BlockSpec((None, r, 6 * D_MODEL), lambda i: (jnp.minimum(i, nt - 1) // tiles_per_mod, 0, 0)),
                pl.BlockSpec((1, D_MODEL), const), pl.BlockSpec((1, D_MODEL), const),
                pl.BlockSpec(wog.shape, const), pl.BlockSpec(bog.shape, const),
                pl.BlockSpec(wum.shape, const), pl.BlockSpec(wua.shape, const),
                pl.BlockSpec(wout.shape, const), pl.BlockSpec(wr.shape, const),
                pl.BlockSpec(br.shape, const)]
    args = [x2, hm, on, mod3, gmix, gffn, wog, bog, wum, wua, wout, wr, br]
    kw = dict(tm=tm, t_mod=t_mod, t_valid=t_valid, m_valid=m_valid, n_real=nt if n_extra else None)
    body = functools.partial(_mixout_kernel, **kw)
    aliases = {}
    if shared is not None:
        in_specs += [pl.BlockSpec(memory_space=pl.ANY)] * len(shared)
        aliases = {len(args) + j: 1 + j for j in range(len(shared))}
        args += list(shared)
        body = functools.partial(_mixout_with_shared, n_shared=len(shared), **kw)
    return pl.pallas_call(
        body,
        grid=(nt + n_extra,),
        in_specs=in_specs,
        out_specs=[pl.BlockSpec((tm, D_MODEL), row),
                   pl.BlockSpec((MOE_RL, D_MODEL // 2), lambda i: (tile0 + i, 0)),
                   pl.BlockSpec((16, tm), lambda i: (0, tile0 + i)),
                   pl.BlockSpec((None, N_EXPERTS, LANES), lambda i: (tile0 + i, 0, 0))],
        out_shape=[jax.ShapeDtypeStruct((m, D_MODEL), F32),
                   jax.ShapeDtypeStruct((nt_total * MOE_RL, D_MODEL // 2), jnp.uint32),
                   jax.ShapeDtypeStruct((16, nt_total * tm), F32),
                   jax.ShapeDtypeStruct((nt_total, N_EXPERTS, LANES), F32)],
        input_output_aliases=aliases,
        compiler_params=_cparams(("arbitrary" if n_extra else "parallel",)),
        name="mixout",
    )(*args)


MOE_BM = 256
MOE_CH = 512


def _moe_kernel(be_ref, na_ref, grp_ref,
                xsl_ref, wgu_ref, bgu_ref, wdn_ref, bdn_ref, ysl_in_ref, ysl_ref,
                wgu_bf, wdn_bf, xbuf, ybuf, sem_in, sem_out, *, trash_row0):
    del ysl_in_ref
    i = pl.program_id(0)
    na = na_ref[0]
    e = be_ref[i]
    prev = be_ref[jnp.maximum(i - 1, 0)]
    n_grp = MOE_BM // SEG_ALIGN

    def group_copies(blk, inbound, slot=None):
        slot = blk % 2 if slot is None else slot
        cps = []
        for r in range(n_grp):
            v = grp_ref[blk * n_grp + r]
            vm_rows = pl.ds(r * SEG_ALIGN, SEG_ALIGN)
            if inbound:
                row = pl.multiple_of(jnp.maximum(v, 0), SEG_ALIGN)
                cps.append(pltpu.make_async_copy(xsl_ref.at[pl.ds(row, SEG_ALIGN), :],
                                                 xbuf.at[slot, vm_rows, :], sem_in.at[slot]))
            else:
                spare = trash_row0 + slot * MOE_BM + r * SEG_ALIGN
                row = pl.multiple_of(jnp.where(v >= 0, v, spare), SEG_ALIGN)
                cps.append(pltpu.make_async_copy(ybuf.at[slot, vm_rows, :],
                                                 ysl_ref.at[pl.ds(row, SEG_ALIGN), :], sem_out.at[slot]))
        return cps

    def start_gather(blk):
        for cp in group_copies(blk, True):
            cp.start()

    def start_scatter(blk):
        for cp in group_copies(blk, False):
            cp.start()

    def wait_rows(blk, sem, inbound):
        slot = blk % 2
        if inbound:
            pltpu.make_async_copy(xsl_ref.at[pl.ds(0, MOE_BM), :], xbuf.at[slot], sem.at[slot]).wait()
        else:
            pltpu.make_async_copy(ybuf.at[slot], ysl_ref.at[pl.ds(0, MOE_BM), :], sem.at[slot]).wait()

    @pl.when(i == 0)
    def _():
        start_gather(i)

    @pl.when(i + 1 < na)
    def _():
        start_gather(i + 1)

    @pl.when((i < na) & ((i == 0) | (e != prev)))
    def _():
        for j in range(2 * D_EXPERT // MOE_CH):
            wgu_bf[:, j * MOE_CH:(j + 1) * MOE_CH] = wgu_ref[:, j * MOE_CH:(j + 1) * MOE_CH].astype(BF16)
        for j in range(D_EXPERT // MOE_CH):
            wdn_bf[j * MOE_CH:(j + 1) * MOE_CH, :] = wdn_ref[j * MOE_CH:(j + 1) * MOE_CH, :].astype(BF16)

    @pl.when(i < na)
    def _():
        slot = i % 2
        wait_rows(i, sem_in, True)

        @pl.when(i >= 2)
        def _():
            wait_rows(i - 2, sem_out, False)

        half = D_MODEL // 2
        xh, xl = _unpack_halves(xbuf[slot])

        def xdot(c0, c1):
            return (jnp.dot(xh, wgu_bf[0:half, c0:c1], preferred_element_type=F32)
                    + jnp.dot(xl, wgu_bf[half:D_MODEL, c0:c1], preferred_element_type=F32))

        acc = jnp.zeros((MOE_BM, D_MODEL), F32) + bdn_ref[...]
        for j in range(D_EXPERT // MOE_CH):
            lo, hi = j * MOE_CH, (j + 1) * MOE_CH
            gj = xdot(lo, hi) + bgu_ref[:, lo:hi]
            uj = xdot(D_EXPERT + lo, D_EXPERT + hi) + bgu_ref[:, D_EXPERT + lo:D_EXPERT + hi]
            gj = jnp.minimum(gj, SWIGLU_LIMIT)
            uj = jnp.clip(uj, -SWIGLU_LIMIT, SWIGLU_LIMIT)
            act = gj * _sigmoid(SWIGLU_ALPHA * gj) * (uj + 1.0)
            acc = acc + jnp.dot(act.astype(BF16), wdn_bf[lo:hi, :], preferred_element_type=F32)
        ybuf[slot] = _pack_halves(acc)
        start_scatter(i)

        @pl.when(i == na - 1)
        def _():
            @pl.when(i >= 1)
            def _():
                wait_rows(i - 1, sem_out, False)
            wait_rows(i, sem_out, False)


def _moe_experts(plan, xsl, w_gu, b_gu, w_dn, b_dn):
    block_e, n_active, grp_rows = plan
    nblk = block_e.shape[0]
    n_rows = xsl.shape[0]
    out_rows = n_rows + 2 * MOE_BM
    wmap = lambda i, be, *_: (be[i], 0, 0)
    anyspec = pl.BlockSpec(memory_space=pl.ANY)
    gs = pltpu.PrefetchScalarGridSpec(
        num_scalar_prefetch=3,
        grid=(nblk,),
        in_specs=[anyspec,
                  pl.BlockSpec((None, D_MODEL, 2 * D_EXPERT), wmap),
                  pl.BlockSpec((None, 1, 2 * D_EXPERT), wmap),
                  pl.BlockSpec((None, D_EXPERT, D_MODEL), wmap),
                  pl.BlockSpec((None, 1, D_MODEL), wmap),
                  anyspec],
        out_specs=anyspec,
        scratch_shapes=[pltpu.VMEM((D_MODEL, 2 * D_EXPERT), BF16), pltpu.VMEM((D_EXPERT, D_MODEL), BF16),
                        pltpu.VMEM((2, MOE_BM, D_MODEL // 2), jnp.uint32),
                        pltpu.VMEM((2, MOE_BM, D_MODEL // 2), jnp.uint32),
                        pltpu.SemaphoreType.DMA((2,)), pltpu.SemaphoreType.DMA((2,))],
    )
    return pl.pallas_call(
        functools.partial(_moe_kernel, trash_row0=n_rows),
        grid_spec=gs,
        out_shape=jax.ShapeDtypeStruct((out_rows, D_MODEL // 2), jnp.uint32),
        input_output_aliases={8: 0},
        compiler_params=_cparams(("arbitrary",)),
        name="moe_experts",
    )(*plan, xsl, w_gu, b_gu.reshape(N_EXPERTS, 1, -1), w_dn, b_dn.reshape(N_EXPERTS, 1, -1),
      jnp.zeros((out_rows, D_MODEL // 2), jnp.uint32))


def _combine_kernel(ysl_ref, info_ref, x1_ref, mod_ref, y_ref, *, tm):
    info = info_ref[...]
    info_t = jnp.transpose(jnp.concatenate([info, jnp.zeros((LANES - info.shape[0], tm), F32)], axis=0))
    ridx = lax.broadcasted_iota(jnp.int32, (tm, MOE_RL), 1).astype(F32)
    pg = jnp.zeros((tm, MOE_RL), F32)
    for k in range(TOP_K):
        pg = pg + jnp.where(ridx == info_t[:, k:k + 1], info_t[:, TOP_K + k:TOP_K + k + 1], 0.0)
    pgb = pg.astype(BF16)
    yh, yl = _unpack_halves(ysl_ref[...])
    half = D_MODEL // 2
    gt2 = mod_ref[:, 5 * D_MODEL:6 * D_MODEL]
    for c, yy in ((0, yh), (1, yl)):
        moe = jnp.dot(pgb, yy, preferred_element_type=F32)
        y_ref[:, c * half:(c + 1) * half] = (x1_ref[:, c * half:(c + 1) * half]
                                             + gt2[:, c * half:(c + 1) * half] * moe)


def _combine(ysl, info, x1, mod3, tiles_per_mod, tile0=0):
    m = x1.shape[0]
    tm = MOE_TM
    r = mod3.shape[1]
    return pl.pallas_call(
        functools.partial(_combine_kernel, tm=tm),
        grid=(m // tm,),
        in_specs=[pl.BlockSpec((MOE_RL, D_MODEL // 2), lambda i: (tile0 + i, 0)),
                  pl.BlockSpec((16, tm), lambda i: (0, tile0 + i)),
                  pl.BlockSpec((tm, D_MODEL), lambda i: (i, 0)),
                  pl.BlockSpec((None, r, 6 * D_MODEL), lambda i: (i // tiles_per_mod, 0, 0))],
        out_specs=pl.BlockSpec((tm, D_MODEL), lambda i: (i, 0)),
        out_shape=jax.ShapeDtypeStruct((m, D_MODEL), F32),
        compiler_params=_cparams(("parallel",)),
        name="moe_combine",
    )(ysl, info, x1, mod3)


def _moe_plan(cnt):
    cnt = cnt.astype(jnp.int32)
    nt = cnt.shape[0]
    so = jnp.cumsum(cnt, axis=1) - cnt + (jnp.arange(nt) * MOE_RL)[:, None]
    ce = jnp.cumsum(cnt, axis=0)
    cs = ce - cnt
    tot = ce[-1]
    nblk_e = (tot + MOE_BM - 1) // MOE_BM
    blk_end = jnp.cumsum(nblk_e)
    max_rows = nt * MOE_TM * TOP_K + nt * N_EXPERTS * (SEG_ALIGN - 1)
    n_blocks = -(-max_rows // MOE_BM) + N_EXPERTS
    bidx = jnp.arange(n_blocks)
    block_e = jnp.minimum(jnp.sum(blk_end[None, :] <= bidx[:, None], axis=1), N_EXPERTS - 1).astype(jnp.int32)
    is_e = (jnp.arange(N_EXPERTS)[:, None] == block_e[None, :]).astype(jnp.int32)
    per_block = lambda a: jnp.sum(a[..., :, None] * is_e, axis=-2)
    block_r0 = (bidx - per_block(blk_end - nblk_e)) * MOE_BM
    x = block_r0[:, None] + jnp.arange(MOE_BM // SEG_ALIGN)[None, :] * SEG_ALIGN
    ce_b = per_block(ce)[:, :, None]
    cs_b = per_block(cs)[:, :, None]
    inside = (cs_b <= x[None]) & (x[None] < ce_b)
    grp = x + jnp.sum(jnp.where(inside, per_block(so - cs)[:, :, None], 0), axis=0)
    grp = jnp.where(x < per_block(tot)[:, None], grp, -1)
    n_active = blk_end[-1].reshape(1)
    i32 = lambda a: a.reshape(-1).astype(jnp.int32)
    return block_e, i32(n_active), i32(grp)


def _rope_tables(pos):
    half = ROT_DIM // 2
    inv = ROPE_THETA ** (-jnp.arange(half, dtype=F32) * (2.0 / ROT_DIM))
    ang = pos.astype(F32)[:, None] * inv[None, :]
    cos, sin = jnp.cos(ang), jnp.sin(ang)
    n = pos.shape[0]
    ones = jnp.ones((n, A_DH - ROT_DIM), F32)
    zeros_h = jnp.zeros((n, half), F32)
    zeros_r = jnp.zeros((n, A_DH - ROT_DIM), F32)
    cos64 = jnp.concatenate([cos, cos, ones], axis=1)
    sprev64 = jnp.concatenate([zeros_h, sin, zeros_r], axis=1)
    snext64 = jnp.concatenate([-sin, zeros_h, zeros_r], axis=1)
    two = lambda a: jnp.concatenate([a, a], axis=1)
    return two(cos64), two(sprev64), two(snext64)


def _prep_weights(w_in, b_in, q_norm_g, k_norm_g, cmp_pe_k, cmp_pe_v, cmp_w_k, cmp_w_v,
                  w_up_m, w_up_a, w_out, w_router, b_router):
    b2 = b_in.reshape(1, N_IN)
    wm = w_in[:, OFF_MQ:OFF_MO].astype(BF16)
    bm = b2[:, OFF_MQ:OFF_MO]
    wq = w_in[:, OFF_AQ:OFF_AKV].astype(BF16)
    bq = b2[:, OFF_AQ:OFF_AKV]
    wkv = w_in[:, OFF_AKV:OFF_AG].astype(BF16)
    bkv = b2[:, OFF_AKV:OFF_AG]
    n_small = 2 * M_HEADS + 3 * A_HEADS
    ws = jnp.concatenate([w_in[:, OFF_MI:OFF_AQ], w_in[:, OFF_AG:OFF_GA],
                          jnp.zeros((D_MODEL, LANES - n_small), F32)], axis=1)
    bs = jnp.concatenate([b2[:, OFF_MI:OFF_AQ], b2[:, OFF_AG:OFF_GA], jnp.zeros((1, LANES - n_small), F32)], axis=1)
    qg = jnp.tile(q_norm_g, A_HEADS).reshape(1, A_WIDTH)
    kg = jnp.stack([jnp.tile(k_norm_g[1], A_KV), jnp.tile(k_norm_g[2], A_KV)], axis=0)
    kg0 = jnp.tile(k_norm_g[0], A_KV).reshape(1, LANES)
    hid = jnp.arange(A_WIDTH) // A_DH
    bd = jnp.where(hid[:, None] == hid[None, :], 1.0 / A_DH, 0.0).astype(BF16)
    inproj_w = (wm, bm, wq, bq, wkv, bkv, ws, bs, qg, kg, bd)

    z = jnp.zeros((CMP_LEN, A_DH, A_DH), F32)
    r0 = jnp.concatenate([cmp_w_k, z, z, z], axis=2)
    r1 = jnp.concatenate([z, cmp_w_k, z, z], axis=2)
    r2 = jnp.concatenate([z, z, cmp_w_v, z], axis=2)
    r3 = jnp.concatenate([z, z, z, cmp_w_v], axis=2)
    wbd = jnp.concatenate([r0, r1, r2, r3], axis=1).astype(BF16)
    pe = jnp.concatenate([cmp_pe_k, cmp_pe_k, cmp_pe_v, cmp_pe_v], axis=1).reshape(CMP_LEN, 1, 2 * LANES)

    wog = jnp.concatenate([w_in[:, OFF_MO:OFF_MI], w_in[:, OFF_GA:N_IN]], axis=1).astype(BF16)
    bog = jnp.concatenate([b2[:, OFF_MO:OFF_MI], b2[:, OFF_GA:N_IN]], axis=1)
    mixout_w = (wog, bog, w_up_m.astype(BF16), w_up_a.astype(BF16), w_out.astype(BF16),
                w_router.T, b_router.reshape(N_EXPERTS, 1))
    return inproj_w, (wbd, pe, kg0), mixout_w


def _pick_tile(m, pref):
    t = pref
    while m % t:
        t //= 2
    return t


def kernel(x_prompt, x_sample, cache_nsa_kv, state_win_kv, state_mlstm_C, state_mlstm_n, state_mlstm_m, page_table, c_prompt, c_sample, w_ada, b_ada, g_mix, g_ffn, w_in, b_in, q_norm_g, k_norm_g, cmp_pe_k, cmp_pe_v, cmp_w_k, cmp_w_v, w_up_m, w_up_a, w_out, w_router, b_router, w_gu, b_gu, w_dn, b_dn):
    depth = w_in.shape[0]
    assert depth == 1
    B, T, D = x_prompt.shape
    DB, TS, _ = x_sample.shape
    n_pages = page_table.shape[1]
    past_len = n_pages * PAGE_SIZE
    wbuf = state_win_kv.shape[2]
    tp = SAMPLE_PAD_T
    assert TS <= tp and wbuf % tp == 0 and T % 128 == 0

    l = 0
    inproj_w, cmp_w, mixout_w = _prep_weights(
        w_in[l], b_in[l], q_norm_g[l], k_norm_g[l], cmp_pe_k[l], cmp_pe_v[l], cmp_w_k[l], cmp_w_v[l],
        w_up_m[l], w_up_a[l], w_out[l], w_router[l], b_router[l])
    wbd, pe, kg0 = cmp_w
    gmix = g_mix[l].reshape(1, D)
    gffn = g_ffn[l].reshape(1, D)

    nc = B + DB
    nc_pad = -(-nc // SUBLANES) * SUBLANES
    c_all = jnp.concatenate([c_prompt, c_sample, jnp.zeros((nc_pad - nc, D), F32)], axis=0)
    mod = _adaln(c_all, w_ada[l], b_ada[l])
    mod_p = mod[:B].reshape(B, 1, 6 * D)
    mod_s = jnp.repeat(mod[B:B + DB], tp, axis=0).reshape(1, DB * tp, 6 * D)

    mp = B * T
    tm = _pick_tile(T, 256)
    xp = x_prompt.reshape(mp, D)
    tabs_p = _rope_tables(jnp.arange(T, dtype=jnp.int32))
    mq, mk, mv, q, qr, rows, win, small, rows_t = _inproj(xp, mod_p, gmix, tabs_p, inproj_w, tm, T // tm, T // tm,
                                                          rows_t_batches=B)
    Lp = _pick_tile(T, 128)
    hm, C_p, n_p, m_p = _mlstm(mq, mk, mv, small, B, T, T, Lp)
    o_nsa = _nsa_prompt(q, qr, small, rows, win, wbd, pe, kg0, B, T)
    assert T % MOE_TM == 0
    ms_pad = -(-(DB * tp) // MOE_TM) * MOE_TM
    nt_p = mp // MOE_TM
    nt_all = nt_p + ms_pad // MOE_TM
    x1_p, xsl, info, cnt = _mixout(xp, hm, o_nsa, mod_p, gmix, gffn, mixout_w, T // MOE_TM, nt_all)

    ms = DB * tp
    xs_pad = jnp.concatenate([x_sample, jnp.zeros((DB, tp - TS, D), F32)], axis=1).reshape(ms, D)
    pos_s = past_len + jnp.tile(jnp.arange(tp, dtype=jnp.int32), DB)
    tabs_s = _rope_tables(pos_s)
    mq_s, mk_s, mv_s, q_s, qr_s, rows_s, win_s, small_s = _inproj(xs_pad, mod_s, gmix, tabs_s, inproj_w, ms, 1, 1)
    hm_s, C_s, n_s, m_s = _mlstm(mq_s, mk_s, mv_s, small_s, DB, tp, TS, tp,
                                 state=(state_mlstm_C[l], state_mlstm_n[l], state_mlstm_m[l]))
    cache2 = jnp.transpose(cache_nsa_kv[l], (0, 2, 3, 4, 1)).reshape(cache_nsa_kv.shape[1], 4 * LANES, PAGE_SIZE)
    winbuf = state_win_kv[l].reshape(DB, wbuf, 2 * LANES)
    o_nsa_s, win_out_s = _nsa_sample(page_table, cache2, q_s, qr_s, small_s, rows_s, win_s, winbuf,
                                     wbd, pe, kg0, TS)
    assert ms_pad == MOE_TM
    rpad = lambda a: jnp.concatenate([a, jnp.zeros((ms_pad - ms, a.shape[1]), a.dtype)], axis=0) if ms_pad > ms else a
    mod_sp = rpad(mod_s[0])[None]
    x1_s, xsl, info, cnt = _mixout(rpad(xs_pad), rpad(hm_s), rpad(o_nsa_s), mod_sp, gmix, gffn, mixout_w,
                                   1, nt_all, tile0=nt_p, shared=(xsl, info, cnt),
                                   t_mod=tp, t_valid=TS, m_valid=ms)

    ysl = _moe_experts(_moe_plan(cnt[:, :, 0]), xsl, w_gu[l], b_gu[l], w_dn[l], b_dn[l])
    y_p = _combine(ysl, info, x1_p, mod_p, T // MOE_TM).reshape(B, T, D)
    y_s_all = _combine(ysl, info, x1_s, mod_sp, 1, tile0=nt_p)
    valid = lambda a: a.reshape(DB, tp, -1)[:, :TS].reshape(DB * TS, -1)
    y_s = valid(y_s_all[:ms]).reshape(DB, TS, D)

    kv_p = jnp.transpose(rows_t.reshape(B, 4, A_KV, A_DH, T), (0, 4, 1, 2, 3))[None]
    kv_s = valid(rows_s).reshape(1, DB, TS, 4, A_KV, A_DH)
    wp = min(WINDOW, T)
    win_p = win.reshape(B, T, 2, A_KV, A_DH)[:, T - wp:][None]
    win_s_out = win_out_s.reshape(1, DB, wbuf, 2, A_KV, A_DH)
    return (y_p, y_s, kv_p, kv_s, win_p, win_s_out,
            C_p[None], n_p[None], m_p[None], C_s[None], n_s[None], m_s[None])
```

```python
import functools
import math

import jax
import jax.numpy as jnp
from jax import lax
from jax.experimental import pallas as pl
from jax.experimental.pallas import tpu as pltpu

F32 = jnp.float32
BF16 = jnp.bfloat16

D_MODEL = 1024
M_HEADS = 4
M_DH = 128
M_WIDTH = M_HEADS * M_DH
A_HEADS = 8
A_KV = 2
A_HPG = A_HEADS // A_KV
A_DH = 64
A_WIDTH = A_HEADS * A_DH
CMP_STRIDE = 16
CMP_LEN = 32
SEL_LEN = 64
N_SEL = 16
WINDOW = 512
PAGE_SIZE = 128
ROPE_THETA = 500000.0
ROT_DIM = A_DH // 4
ATT_SCALE = A_DH ** -0.5
N_EXPERTS = 32
TOP_K = 4
D_EXPERT = D_MODEL
SWIGLU_LIMIT = 7.0
SWIGLU_ALPHA = 1.702
EPS = 1e-6

OFF_MQ, OFF_MK, OFF_MV, OFF_MO = 0, M_WIDTH, 2 * M_WIDTH, 3 * M_WIDTH
OFF_MI = 4 * M_WIDTH
OFF_MF = OFF_MI + M_HEADS
OFF_AQ = OFF_MF + M_HEADS
OFF_AKV = OFF_AQ + A_WIDTH
OFF_AG = OFF_AKV + 6 * A_KV * A_DH
OFF_GA = OFF_AG + 3 * A_HEADS
OFF_GB = OFF_GA + D_MODEL
N_IN = OFF_GB + D_MODEL

LANES = 128
SUBLANES = 8
VMEM_LIMIT = 56 * 1024 * 1024

NEG_BIG = -1e30
M_INIT = -1e29
LOG2E = 1.4426950408889634
SAMPLE_PAD_T = 8


def _cparams(sem):
    return pltpu.CompilerParams(dimension_semantics=sem, vmem_limit_bytes=VMEM_LIMIT)


def _bdot(a, b):
    return jnp.dot(a.astype(BF16), b.astype(BF16), preferred_element_type=F32)


def _bdot_t(a, b):
    return lax.dot_general(a.astype(BF16), b.astype(BF16), (((1,), (1,)), ((), ())),
                           preferred_element_type=F32)


def _split(a):
    hi = a.astype(BF16)
    lo = (a - hi.astype(F32)).astype(BF16)
    return hi, lo


def _dot3(a, b):
    ah, al = _split(a)
    bh, bl = _split(b)
    return (jnp.dot(ah, bh, preferred_element_type=F32) + jnp.dot(al, bh, preferred_element_type=F32)
            + jnp.dot(ah, bl, preferred_element_type=F32))


def _dot2_exact_rhs(a, b_bf16):
    ah, al = _split(a)
    return jnp.dot(ah, b_bf16, preferred_element_type=F32) + jnp.dot(al, b_bf16, preferred_element_type=F32)


def _sigmoid(x):
    return 0.5 * jnp.tanh(0.5 * x) + 0.5


def _rmsnorm_rows(x, g):
    return x * lax.rsqrt(jnp.mean(x * x, axis=-1, keepdims=True) + EPS) * g


def _adaln_kernel(c_ref, w_ref, b_ref, o_ref):
    c = c_ref[...]
    s = c * _sigmoid(c)
    o_ref[...] = _dot3(s, w_ref[...]) + b_ref[...]


def _adaln(c, w, b):
    mc, d = c.shape
    n = w.shape[1]
    tn = 1024
    return pl.pallas_call(
        _adaln_kernel,
        grid=(n // tn,),
        in_specs=[pl.BlockSpec((mc, d), lambda j: (0, 0)),
                  pl.BlockSpec((d, tn), lambda j: (0, j)),
                  pl.BlockSpec((1, tn), lambda j: (0, j))],
        out_specs=pl.BlockSpec((mc, tn), lambda j: (0, j)),
        out_shape=jax.ShapeDtypeStruct((mc, n), F32),
        compiler_params=_cparams(("parallel",)),
        name="adaln",
    )(c, w, b.reshape(1, n))


def _head_norm(z, bd, gain):
    ms = _dot2_exact_rhs(z * z, bd)
    return z * lax.rsqrt(ms + EPS) * gain


def _rope(z, cos, s_prev, s_next):
    w = z.shape[1]
    rep = w // LANES
    if rep > 1:
        cos = jnp.concatenate([cos] * rep, axis=1)
        s_prev = jnp.concatenate([s_prev] * rep, axis=1)
        s_next = jnp.concatenate([s_next] * rep, axis=1)
    z_prev = pltpu.roll(z, ROT_DIM // 2, 1)
    z_next = pltpu.roll(z, w - ROT_DIM // 2, 1)
    return z * cos + z_prev * s_prev + z_next * s_next


def _inproj_kernel(x_ref, mod_ref, gmix_ref, cos_ref, sp_ref, sn_ref,
                   wm_ref, bm_ref, wq_ref, bq_ref, wkv_ref, bkv_ref, ws_ref, bs_ref,
                   qg_ref, kg_ref, bd_ref,
                   mq_ref, mk_ref, mv_ref, q_ref, qr_ref, rows_ref, win_ref, small_ref, rows_t_ref=None):
    x = x_ref[...]
    sh1 = mod_ref[:, 0:D_MODEL]
    sc1 = mod_ref[:, D_MODEL:2 * D_MODEL]
    h = _rmsnorm_rows(x, gmix_ref[...]) * (1.0 + sc1) + sh1
    hb = h.astype(BF16)

    mq_ref[...] = jnp.dot(hb, wm_ref[:, 0:M_WIDTH], preferred_element_type=F32) + bm_ref[:, 0:M_WIDTH]
    mk = jnp.dot(hb, wm_ref[:, M_WIDTH:2 * M_WIDTH], preferred_element_type=F32) + bm_ref[:, M_WIDTH:2 * M_WIDTH]
    mk_ref[...] = mk * (M_DH ** -0.5)
    mv_ref[...] = (jnp.dot(hb, wm_ref[:, 2 * M_WIDTH:3 * M_WIDTH], preferred_element_type=F32)
                   + bm_ref[:, 2 * M_WIDTH:3 * M_WIDTH])

    cos, sp, sn = cos_ref[...], sp_ref[...], sn_ref[...]
    zq = jnp.dot(hb, wq_ref[...], preferred_element_type=F32) + bq_ref[...]
    qn = _head_norm(zq, bd_ref[...], qg_ref[...])
    q_ref[...] = qn
    qr_ref[...] = _rope(qn, cos, sp, sn)

    zkv = jnp.dot(hb, wkv_ref[...], preferred_element_type=F32) + bkv_ref[...]
    bd2 = bd_ref[0:LANES, 0:LANES]
    ksel = _head_norm(zkv[:, 2 * LANES:3 * LANES], bd2, kg_ref[0:1, :])
    rows = jnp.concatenate([zkv[:, 0:2 * LANES], _rope(ksel, cos, sp, sn), zkv[:, 3 * LANES:4 * LANES]], axis=1)
    rows_ref[...] = rows
    if rows_t_ref is not None:
        rows_t_ref[...] = jnp.transpose(rows)
    kwin = _head_norm(zkv[:, 4 * LANES:5 * LANES], bd2, kg_ref[1:2, :])
    win_ref[:, 0:LANES] = _rope(kwin, cos, sp, sn)
    win_ref[:, LANES:2 * LANES] = zkv[:, 5 * LANES:6 * LANES]

    small_ref[...] = _dot3(h, ws_ref[...]) + bs_ref[...]


def _inproj(x2, mod3, gmix, tabs, wts, tm, tiles_per_mod, pos_tiles, rows_t_batches=None):
    m = x2.shape[0]
    cos_t, sp_t, sn_t = tabs
    (wm, bm, wq, bq, wkv, bkv, ws, bs, qg, kg, bd) = wts
    r = mod3.shape[1]
    row = lambda i: (i, 0)
    const = lambda i: (0, 0)
    tab = lambda i: (i % pos_tiles, 0)
    in_specs = [
        pl.BlockSpec((tm, D_MODEL), row),
        pl.BlockSpec((None, r, 6 * D_MODEL), lambda i: (i // tiles_per_mod, 0, 0)),
        pl.BlockSpec((1, D_MODEL), const),
        pl.BlockSpec((tm, LANES), tab), pl.BlockSpec((tm, LANES), tab), pl.BlockSpec((tm, LANES), tab),
        pl.BlockSpec(wm.shape, const), pl.BlockSpec(bm.shape, const),
        pl.BlockSpec(wq.shape, const), pl.BlockSpec(bq.shape, const),
        pl.BlockSpec(wkv.shape, const), pl.BlockSpec(bkv.shape, const),
        pl.BlockSpec(ws.shape, const), pl.BlockSpec(bs.shape, const),
        pl.BlockSpec(qg.shape, const), pl.BlockSpec(kg.shape, const), pl.BlockSpec(bd.shape, const),
    ]
    widths = (M_WIDTH, M_WIDTH, M_WIDTH, A_WIDTH, A_WIDTH, 4 * LANES, 2 * LANES, LANES)
    out_specs = [pl.BlockSpec((tm, w), row) for w in widths]
    out_shape = [jax.ShapeDtypeStruct((m, w), F32) for w in widths]
    if rows_t_batches is not None:
        out_specs.append(pl.BlockSpec((None, 4 * LANES, tm), lambda i: (i // tiles_per_mod, 0, i % tiles_per_mod)))
        out_shape.append(jax.ShapeDtypeStruct((rows_t_batches, 4 * LANES, m // rows_t_batches), F32))
    return pl.pallas_call(
        _inproj_kernel,
        grid=(m // tm,),
        in_specs=in_specs,
        out_specs=out_specs,
        out_shape=out_shape,
        compiler_params=_cparams(("parallel",)),
        name="inproj",
    )(x2, mod3, gmix, cos_t, sp_t, sn_t, wm, bm, wq, bq, wkv, bkv, ws, bs, qg, kg, bd)


def _log_sigmoid(x):
    return jnp.minimum(x, 0.0) - jnp.log(1.0 + jnp.exp(-jnp.abs(x)))


def _mlstm_kernel(*refs, L, t_valid, has_state):
    if has_state:
        q_ref, k_ref, v_ref, s_ref, c0_ref, n0_ref, m0_ref, h_ref, c_ref, n_ref, m_ref = refs
    else:
        q_ref, k_ref, v_ref, s_ref, h_ref, c_ref, n_ref, m_ref = refs
    c = pl.program_id(1)

    @pl.when(c == 0)
    def _():
        if has_state:
            c_ref[...] = c0_ref[...]
            n_ref[...] = n0_ref[...]
            m_ref[...] = m0_ref[...]
        else:
            c_ref[...] = jnp.zeros(c_ref.shape, F32)
            n_ref[...] = jnp.zeros(n_ref.shape, F32)
            m_ref[...] = jnp.zeros(m_ref.shape, F32)

    row = lax.broadcasted_iota(jnp.int32, (L, L), 0)
    col = lax.broadcasted_iota(jnp.int32, (L, L), 1)
    causal = col <= row
    eye = col == row
    tok_col = c * L + lax.broadcasted_iota(jnp.int32, (L, 1), 0)
    valid_col = tok_col < t_valid
    for hd in range(M_HEADS):
        lo, hi = hd * M_DH, (hd + 1) * M_DH
        q = q_ref[:, lo:hi]
        k = k_ref[:, lo:hi]
        v = v_ref[:, lo:hi]
        i_col = s_ref[:, hd:hd + 1]
        lf_col = _log_sigmoid(s_ref[:, M_HEADS + hd:M_HEADS + hd + 1])
        lf_col = jnp.where(valid_col, lf_col, 0.0)
        i_col = jnp.where(valid_col, i_col, -jnp.inf)
        i_row = jnp.sum(jnp.where(eye, i_col, 0.0), axis=0, keepdims=True)
        lf_row = jnp.sum(jnp.where(eye, lf_col, 0.0), axis=0, keepdims=True)
        b_col = jnp.sum(jnp.where(causal, lf_row, 0.0), axis=1, keepdims=True)
        b_row = jnp.sum(jnp.where(row <= col, lf_col, 0.0), axis=0, keepdims=True)
        m_prev = m_ref[:, hd:hd + 1]
        dmat = jnp.where(causal, b_col - b_row + i_row, -jnp.inf)
        inter = b_col + m_prev
        m_row = jnp.maximum(jnp.max(dmat, axis=1, keepdims=True), inter)
        w = jnp.exp(dmat - m_row)
        w_inter = jnp.exp(inter - m_row)
        s = _bdot_t(q, k) * w
        cm = c_ref[hd]
        nv = n_ref[hd]
        num = _bdot(s, v) + w_inter * _bdot_t(q, cm)
        den = jnp.sum(s, axis=1, keepdims=True) + w_inter * jnp.sum(q * nv, axis=1, keepdims=True)
        h_ref[:, lo:hi] = num / jnp.maximum(jnp.abs(den), jnp.exp(-m_row))
        b_last = b_col[L - 1:L, :]
        dec_col = b_last - b_col + i_col
        dec_row = b_last - b_row + i_row
        m_new = jnp.maximum(b_last + m_prev, jnp.max(dec_row, axis=1, keepdims=True))
        ws_col = jnp.exp(dec_col - m_new)
        wc = jnp.exp(b_last + m_prev - m_new)
        vw = (v * ws_col).astype(BF16)
        upd = lax.dot_general(vw, k.astype(BF16), (((0,), (0,)), ((), ())), preferred_element_type=F32)
        c_ref[hd] = wc * cm + upd
        n_ref[hd] = wc * nv + jnp.sum(k * ws_col, axis=0, keepdims=True)
        m_ref[:, hd:hd + 1] = m_new


def _mlstm(mq, mk, mv, small, nb, t_pad, t_valid, L, state=None):
    nc = t_pad // L
    has_state = state is not None
    blk = lambda b, c: (b * nc + c, 0)
    st4 = lambda b, c: (b, 0, 0, 0)
    st3 = lambda b, c: (b, 0, 0)
    in_specs = [pl.BlockSpec((L, M_WIDTH), blk)] * 3 + [pl.BlockSpec((L, LANES), blk)]
    args = [mq, mk, mv, small]
    if has_state:
        c0, n0, m0 = state
        in_specs += [pl.BlockSpec((None, M_HEADS, M_DH, M_DH), st4),
                     pl.BlockSpec((None, M_HEADS, 1, M_DH), st4),
                     pl.BlockSpec((None, 1, M_HEADS), st3)]
        args += [c0, n0.reshape(nb, M_HEADS, 1, M_DH), m0.reshape(nb, 1, M_HEADS)]
    out_specs = [pl.BlockSpec((L, M_WIDTH), blk),
                 pl.BlockSpec((None, M_HEADS, M_DH, M_DH), st4),
                 pl.BlockSpec((None, M_HEADS, 1, M_DH), st4),
                 pl.BlockSpec((None, 1, M_HEADS), st3)]
    out_shape = [jax.ShapeDtypeStruct((nb * t_pad, M_WIDTH), F32),
                 jax.ShapeDtypeStruct((nb, M_HEADS, M_DH, M_DH), F32),
                 jax.ShapeDtypeStruct((nb, M_HEADS, 1, M_DH), F32),
                 jax.ShapeDtypeStruct((nb, 1, M_HEADS), F32)]
    h, cs, ns, ms = pl.pallas_call(
        functools.partial(_mlstm_kernel, L=L, t_valid=t_valid, has_state=has_state),
        grid=(nb, nc),
        in_specs=in_specs,
        out_specs=out_specs,
        out_shape=out_shape,
        compiler_params=_cparams(("parallel", "arbitrary")),
        name="mlstm",
    )(*args)
    return h, cs, ns.reshape(nb, M_HEADS, M_DH), ms.reshape(nb, M_HEADS)


def _stack_heads(qt, g):
    t = qt.shape[0]
    z = jnp.zeros((t, A_DH), F32)
    parts = []
    for hh in range(A_HPG):
        hd = g * A_HPG + hh
        qh = qt[:, hd * A_DH:(hd + 1) * A_DH] * (ATT_SCALE * LOG2E)
        parts.append(jnp.concatenate([qh, z], axis=1) if g == 0 else jnp.concatenate([z, qh], axis=1))
    return jnp.concatenate(parts, axis=0).astype(BF16)


def _gate_cols(small, g, br):
    cols = []
    for hh in range(A_HPG):
        c0 = 2 * M_HEADS + (g * A_HPG + hh) * 3 + br
        cols.append(_sigmoid(small[:, c0:c0 + 1]))
    return jnp.concatenate(cols, axis=0)


def _compress(k_ref, v_ref, nseg, wbd_ref, pe_ref, kg0):
    acc_lo = jnp.zeros((nseg, 2 * LANES), F32)
    acc_hi = jnp.zeros((nseg, 2 * LANES), F32)
    for l in range(CMP_STRIDE):
        xl = jnp.concatenate([k_ref[pl.ds(l, nseg, stride=CMP_STRIDE), :],
                              v_ref[pl.ds(l, nseg, stride=CMP_STRIDE), :]], axis=1)
        acc_lo = acc_lo + _bdot(xl + pe_ref[l], wbd_ref[l])
        acc_hi = acc_hi + _bdot(xl + pe_ref[CMP_STRIDE + l], wbd_ref[CMP_STRIDE + l])
    return _compress_finish(acc_lo, acc_hi, nseg, kg0)


def _compress_grouped(x_ref, nseg, wbd_ref, pe_ref, kg0):
    acc_lo = jnp.zeros((nseg, 2 * LANES), F32)
    acc_hi = jnp.zeros((nseg, 2 * LANES), F32)
    pe_lo = jnp.zeros((SUBLANES, 2 * LANES), F32)
    pe_hi = jnp.zeros((SUBLANES, 2 * LANES), F32)
    for l in range(CMP_STRIDE):
        xl = x_ref[l].astype(BF16)
        acc_lo = acc_lo + jnp.dot(xl, wbd_ref[l], preferred_element_type=F32)
        acc_hi = acc_hi + jnp.dot(xl, wbd_ref[CMP_STRIDE + l], preferred_element_type=F32)
        pe_lo = pe_lo + _bdot(jnp.broadcast_to(pe_ref[l], (SUBLANES, 2 * LANES)), wbd_ref[l])
        pe_hi = pe_hi + _bdot(jnp.broadcast_to(pe_ref[CMP_STRIDE + l], (SUBLANES, 2 * LANES)),
                              wbd_ref[CMP_STRIDE + l])
    return _compress_finish(acc_lo + pe_lo[0:1, :], acc_hi + pe_hi[0:1, :], nseg, kg0)


def _compress_finish(acc_lo, acc_hi, nseg, kg0):
    kv = acc_lo + pltpu.roll(acc_hi, nseg - 1, 0)
    kc = kv[:, 0:LANES]
    vc = kv[:, LANES:2 * LANES]
    lane = lax.broadcasted_iota(jnp.int32, (nseg, LANES), 1)
    sq = kc * kc
    ms0 = jnp.sum(jnp.where(lane < A_DH, sq, 0.0), axis=1, keepdims=True) * (1.0 / A_DH)
    ms1 = jnp.sum(jnp.where(lane >= A_DH, sq, 0.0), axis=1, keepdims=True) * (1.0 / A_DH)
    ms = jnp.where(lane < A_DH, ms0, ms1)
    kc = kc * lax.rsqrt(ms + EPS) * kg0
    return kc, vc


def _cmp_branch(qn_g, kc_b, vc_b, tpos_rows, nseg, n_tok):
    s = _bdot_t(qn_g, kc_b)
    nidx = lax.broadcasted_iota(jnp.int32, (1, nseg), 1)
    vis = (nidx * CMP_STRIDE + (CMP_LEN - 1)) <= tpos_rows
    sm = jnp.where(vis, s, NEG_BIG)
    mx = jnp.max(sm, axis=1, keepdims=True)
    e = jnp.where(vis, jnp.exp2(sm - mx), 0.0)
    d = jnp.sum(e, axis=1, keepdims=True)
    p = e / jnp.where(d > 0, d, 1.0)
    o = _bdot(p, vc_b)
    imp = p[0:n_tok]
    for hh in range(1, A_HPG):
        imp = imp + p[hh * n_tok:(hh + 1) * n_tok]
    return o, imp


def _masked_attn_direct(q_g, k_parts, v_parts, allowed_parts, feature_major):
    ss = [jnp.where(al, _bdot(q_g, kk) if fm else _bdot_t(q_g, kk), NEG_BIG)
          for kk, al, fm in zip(k_parts, allowed_parts, feature_major)]
    mx = ss[0].max(axis=1, keepdims=True)
    for s in ss[1:]:
        mx = jnp.maximum(mx, s.max(axis=1, keepdims=True))
    num = None
    den = None
    for s, al, vv, fm in zip(ss, allowed_parts, v_parts, feature_major):
        e = jnp.where(al, jnp.exp2(s - mx), 0.0)
        dd = jnp.sum(e, axis=1, keepdims=True)
        oo = _bdot_t(e, vv) if fm else _bdot(e, vv)
        num = oo if num is None else num + oo
        den = dd if den is None else den + dd
    return num / jnp.where(den > 0, den, 1.0)


def _assemble_heads(o_groups, n_tok):
    pieces = []
    for g in range(A_KV):
        for hh in range(A_HPG):
            pieces.append(o_groups[g][hh * n_tok:(hh + 1) * n_tok, g * A_DH:(g + 1) * A_DH])
    return jnp.concatenate(pieces, axis=1)


def _lane_rep(a, rep):
    return a if rep == 1 else jnp.concatenate([a] * rep, axis=1)


def _nsa_prompt_kernel(q_ref, qr_ref, small_ref, rows_ref, win_ref, wbd_ref, pe_ref, kg0_ref,
                       pool_ref, o_ref,
                       kraw_sc, vraw_sc, kc_sc, vct_sc, sel_sc, m_sc, acc_sc, s_sc, *, T, tq, kc_len):
    qi = pl.program_id(1)
    nseg = T // CMP_STRIDE
    nsb = T // SEL_LEN
    bpc = kc_len // SEL_LEN

    @pl.when(qi == 0)
    def _():
        kraw_sc[...] = rows_ref[:, 0:LANES]
        vraw_sc[...] = rows_ref[:, LANES:2 * LANES]
        kc, vc = _compress(kraw_sc, vraw_sc, nseg, wbd_ref, pe_ref, kg0_ref[...])
        kc_sc[...] = kc
        vct_sc[...] = jnp.transpose(vc)

    t0 = qi * tq
    tpos = t0 + lax.broadcasted_iota(jnp.int32, (1, tq), 1)
    tpos4 = _lane_rep(tpos, A_HPG)
    q = q_ref[...]
    qr = qr_ref[...]
    small_t = jnp.transpose(small_ref[...])
    kc_b = kc_sc[...].astype(BF16)
    vct_b = vct_sc[...].astype(BF16)
    bidx = lax.broadcasted_iota(jnp.int32, (nsb, tq), 0)
    cur = tpos // SEL_LEN
    vis = (lax.broadcasted_iota(jnp.int32, (nseg, 1), 0) * CMP_STRIDE + (CMP_LEN - 1)) <= tpos4
    qr_gs = [_stack_heads(qr, g) for g in range(A_KV)]
    o_cmps = []
    for g in range(A_KV):
        sm = jnp.where(vis, _bdot_t(kc_b, _stack_heads(q, g)), NEG_BIG)
        mx = jnp.max(sm, axis=0, keepdims=True)
        e = jnp.where(vis, jnp.exp2(sm - mx), 0.0)
        d = jnp.sum(e, axis=0, keepdims=True)
        p = e / jnp.where(d > 0, d, 1.0)
        o_cmps.append(jnp.dot(vct_b, p.astype(BF16), preferred_element_type=F32))
        imp = p[:, 0:tq]
        for hh in range(1, A_HPG):
            imp = imp + p[:, hh * tq:(hh + 1) * tq]
        ih, il = _split(imp)
        imp_t = (jnp.dot(pool_ref[...], ih, preferred_element_type=F32)
                 + jnp.dot(pool_ref[...], il, preferred_element_type=F32))[0:nsb]
        val = jnp.where(bidx < cur, imp_t, -1.0)
        rank = jnp.zeros((nsb, tq), F32)
        for bp in range(nsb):
            vb = val[bp:bp + 1, :]
            rank = rank + jnp.where(vb > val, 1.0, jnp.where((vb == val) & (bidx > bp), 1.0, 0.0))
        sel_sc[g] = jnp.where(((rank < (N_SEL - 1)) & (bidx < cur)) | (bidx == cur), 1.0, 0.0)

    m_sc[...] = jnp.full(m_sc.shape, M_INIT, F32)
    acc_sc[...] = jnp.zeros(acc_sc.shape, F32)

    def with_ones_row(vt_, g):
        vb = vt_.astype(BF16)
        r0, pad = (1 - g) * A_DH, 2 * SUBLANES
        ones = jnp.ones((pad, vb.shape[1]), BF16)
        return jnp.concatenate(([vb[0:r0]] if r0 else []) + [ones, vb[r0 + pad:]], axis=0)

    def sel_body(c, carry):
        k0 = pl.multiple_of(c * kc_len, kc_len)
        kb = rows_ref[pl.ds(k0, kc_len), 2 * LANES:3 * LANES].astype(BF16)
        vt = jnp.transpose(rows_ref[pl.ds(k0, kc_len), 3 * LANES:4 * LANES])
        causal = (k0 + lax.broadcasted_iota(jnp.int32, (kc_len, 1), 0)) <= tpos
        for g in range(A_KV):
            s_sc[g, 0:kc_len, :] = _bdot_t(kb, qr_gs[g])
        for g in range(A_KV):
            selc = sel_sc[g, pl.ds(pl.multiple_of(c * bpc, bpc), bpc), :]
            selx = jnp.concatenate([jnp.broadcast_to(selc[j:j + 1, :], (SEL_LEN, tq)) for j in range(bpc)], axis=0)
            bias = jnp.where(causal & (selx > 0.5), 0.0, NEG_BIG)
            sm = s_sc[g, 0:kc_len, :] + _lane_rep(bias, A_HPG)
            m_prev = m_sc[g]
            m_new = jnp.maximum(m_prev, jnp.max(sm, axis=0, keepdims=True))
            alpha = jnp.exp2(m_prev - m_new)
            p = jnp.exp2(sm - m_new)
            acc_sc[g] = alpha * acc_sc[g] + jnp.dot(with_ones_row(vt, g), p.astype(BF16),
                                                    preferred_element_type=F32)
            m_sc[g] = m_new
        return carry

    lax.fori_loop(0, (t0 + tq + kc_len - 1) // kc_len, sel_body, 0)

    wk = min(WINDOW + tq, T)
    w0 = pl.multiple_of(jnp.clip(t0 + tq - wk, 0, T - wk), tq)
    kw = win_ref[pl.ds(w0, wk), 0:LANES].astype(BF16)
    vwt = jnp.transpose(win_ref[pl.ds(w0, wk), LANES:2 * LANES])
    wdiff = tpos - (w0 + lax.broadcasted_iota(jnp.int32, (wk, 1), 0))
    wbias = _lane_rep(jnp.where((wdiff >= 0) & (wdiff < WINDOW), 0.0, NEG_BIG), A_HPG)

    def gate_row(g, br):
        cols = [2 * M_HEADS + (g * A_HPG + hh) * 3 + br for hh in range(A_HPG)]
        return jnp.concatenate([_sigmoid(small_t[c0:c0 + 1, :]) for c0 in cols], axis=1)

    for g in range(A_KV):
        s_sc[g, 0:wk, :] = _bdot_t(kw, qr_gs[g])
    o_ts = []
    for g in range(A_KV):
        den = (1 - g) * A_DH
        acc = acc_sc[g]
        l = acc[den:den + 1, :]
        o_sel = acc / jnp.where(l > 0, l, 1.0)
        sw = s_sc[g, 0:wk, :] + wbias
        pw = jnp.exp2(sw - jnp.max(sw, axis=0, keepdims=True))
        ow = jnp.dot(with_ones_row(vwt, g), pw.astype(BF16), preferred_element_type=F32)
        o_win = ow / ow[den:den + 1, :]
        o_ts.append(gate_row(g, 0) * o_cmps[g] + gate_row(g, 1) * o_sel + gate_row(g, 2) * o_win)
    for j in range(A_HEADS // 2):
        g, h0 = j // (A_HPG // 2), 2 * (j % (A_HPG // 2))
        og = o_ts[g][g * A_DH:(g + 1) * A_DH, :]
        pair = jnp.concatenate([og[:, h0 * tq:(h0 + 1) * tq], og[:, (h0 + 1) * tq:(h0 + 2) * tq]], axis=0)
        o_ref[:, j * LANES:(j + 1) * LANES] = jnp.transpose(pair)


def _nsa_prompt(q, qr, small, rows, win, wbd, pe, kg0, nb, T):
    tq = 128
    kc_len = _pick_tile(T, 512)
    nq = T // tq
    nseg = T // CMP_STRIDE
    nsb = T // SEL_LEN
    nsb_p = -(-nsb // SUBLANES) * SUBLANES
    pool = (jnp.arange(nsb_p)[:, None] == jnp.arange(nseg)[None, :] // (SEL_LEN // CMP_STRIDE)).astype(BF16)
    tile = lambda b, i: (b * nq + i, 0)
    per_b = lambda b, i: (b, 0)
    c2 = lambda b, i: (0, 0)
    c3 = lambda b, i: (0, 0, 0)
    c4 = A_HPG * tq
    return pl.pallas_call(
        functools.partial(_nsa_prompt_kernel, T=T, tq=tq, kc_len=kc_len),
        grid=(nb, nq),
        in_specs=[pl.BlockSpec((tq, A_WIDTH), tile), pl.BlockSpec((tq, A_WIDTH), tile),
                  pl.BlockSpec((tq, LANES), tile),
                  pl.BlockSpec((T, 4 * LANES), per_b), pl.BlockSpec((T, 2 * LANES), per_b),
                  pl.BlockSpec(wbd.shape, c3), pl.BlockSpec(pe.shape, c3), pl.BlockSpec(kg0.shape, c2),
                  pl.BlockSpec(pool.shape, c2)],
        out_specs=pl.BlockSpec((tq, A_WIDTH), tile),
        out_shape=jax.ShapeDtypeStruct((nb * T, A_WIDTH), F32),
        scratch_shapes=[pltpu.VMEM((T, LANES), F32), pltpu.VMEM((T, LANES), F32),
                        pltpu.VMEM((nseg, LANES), F32), pltpu.VMEM((LANES, nseg), F32),
                        pltpu.VMEM((A_KV, nsb, tq), F32),
                        pltpu.VMEM((A_KV, 1, c4), F32),
                        pltpu.VMEM((A_KV, LANES, c4), F32),
                        pltpu.VMEM((A_KV, max(kc_len, min(WINDOW + tq, T)), c4), F32)],
        compiler_params=_cparams(("parallel", "arbitrary")),
        name="nsa_prompt",
    )(q, qr, small, rows, win, wbd, pe, kg0, pool)


def _nsa_sample_kernel(pt_ref, cache_ref, q_ref, qr_ref, small_ref, rows_ref, winnew_ref, winbuf_ref,
                       wbd_ref, pe_ref, kg0_ref, pool_ref, expand_ref,
                       o_ref, winout_ref,
                       cmp_buf, sel_buf, xperm_sc, sems, *, n_pages, past_len, t_valid):
    b = pl.program_id(0)
    nb = pl.num_programs(0)
    tp = SAMPLE_PAD_T
    nseg = past_len // CMP_STRIDE
    nsb = past_len // SEL_LEN
    wbuf = winbuf_ref.shape[0]

    def page_copies(bb, p, phase):
        page = pt_ref[bb * n_pages + p]
        dst_lanes = pl.ds(pl.multiple_of(p * PAGE_SIZE, PAGE_SIZE), PAGE_SIZE)
        if phase == 0:
            return [pltpu.make_async_copy(cache_ref.at[page, pl.ds(0, 2 * LANES), :],
                                          cmp_buf.at[:, dst_lanes], sems.at[0])]
        return [pltpu.make_async_copy(cache_ref.at[page, pl.ds(2 * LANES, 2 * LANES), :],
                                      sel_buf.at[:, dst_lanes], sems.at[1])]

    def start_all(bb, phase):
        def body(p, c):
            for cp in page_copies(bb, p, phase):
                cp.start()
            return c
        lax.fori_loop(0, n_pages, body, 0)

    def wait_all(bb, phase):
        def body(p, c):
            for cp in page_copies(bb, p, phase):
                cp.wait()
            return c
        lax.fori_loop(0, n_pages, body, 0)

    @pl.when(b == 0)
    def _():
        start_all(b, 0)

    start_all(b, 1)
    wait_all(b, 0)

    seg_pp = PAGE_SIZE // CMP_STRIDE
    pr = lax.broadcasted_iota(jnp.int32, (PAGE_SIZE, PAGE_SIZE), 0)
    pc = lax.broadcasted_iota(jnp.int32, (PAGE_SIZE, PAGE_SIZE), 1)
    perm = jnp.where(pc == CMP_STRIDE * (pr % seg_pp) + pr // seg_pp, 1.0, 0.0).astype(BF16)
    for p in range(n_pages):
        xp = _bdot_t(perm, cmp_buf[:, p * PAGE_SIZE:(p + 1) * PAGE_SIZE])
        for l in range(CMP_STRIDE):
            xperm_sc[l, p * seg_pp:(p + 1) * seg_pp, :] = xp[l * seg_pp:(l + 1) * seg_pp, :]
    kc, vc = _compress_grouped(xperm_sc, nseg, wbd_ref, pe_ref, kg0_ref[...])
    kc_b = kc.astype(BF16)
    vc_b = vc.astype(BF16)
    q = q_ref[...]
    qr = qr_ref[...]
    small = small_ref[...]
    tpos_col = past_len + lax.broadcasted_iota(jnp.int32, (tp, 1), 0)
    tpos_rows = jnp.concatenate([tpos_col] * A_HPG, axis=0)
    bp_idx = lax.broadcasted_iota(jnp.int32, (nsb, nsb), 0)
    b_idx = lax.broadcasted_iota(jnp.int32, (nsb, nsb), 1)
    o_cmps = []
    sels = []
    for g in range(A_KV):
        qn_g = _stack_heads(q, g)
        o_cmp, imp = _cmp_branch(qn_g, kc_b, vc_b, tpos_rows, nseg, tp)
        o_cmps.append(o_cmp)
        imp_sel = _dot2_exact_rhs(imp, pool_ref[...])
        imp_pad = jnp.concatenate([imp_sel, jnp.zeros((nsb - tp, nsb), F32)], axis=0)
        imp_t = jnp.transpose(imp_pad)
        rows_sel = []
        for t in range(tp):
            if t < t_valid:
                row_t = imp_sel[t:t + 1, :]
                col_t = imp_t[:, t:t + 1]
                ahead = jnp.where(col_t > row_t, 1.0, jnp.where((col_t == row_t) & (bp_idx < b_idx), 1.0, 0.0))
                rank = jnp.sum(ahead, axis=0, keepdims=True)
                rows_sel.append(jnp.where(rank < (N_SEL - 1), 1.0, 0.0))
            else:
                rows_sel.append(jnp.zeros((1, nsb), F32))
        sels.append(jnp.concatenate(rows_sel, axis=0).astype(BF16))

    @pl.when(b + 1 < nb)
    def _():
        start_all(b + 1, 0)

    wait_all(b, 1)

    new_idx = lax.broadcasted_iota(jnp.int32, (tp, tp), 1)
    tok_idx = lax.broadcasted_iota(jnp.int32, (tp, tp), 0)
    new_ok = jnp.concatenate([jnp.where(new_idx <= tok_idx, 1.0, 0.0)] * A_HPG, axis=0) > 0.5
    wpos = past_len - wbuf + lax.broadcasted_iota(jnp.int32, (1, wbuf), 1)
    wdiff = tpos_col - wpos
    win_ok = jnp.concatenate([jnp.where((wdiff >= 0) & (wdiff < WINDOW), 1.0, 0.0)] * A_HPG, axis=0) > 0.5
    k_past = sel_buf[0:LANES, :].astype(BF16)
    v_past = sel_buf[LANES:2 * LANES, :].astype(BF16)
    k_new = rows_ref[:, 2 * LANES:3 * LANES]
    v_new = rows_ref[:, 3 * LANES:4 * LANES]
    kw_past = winbuf_ref[:, 0:LANES]
    vw_past = winbuf_ref[:, LANES:2 * LANES]
    kw_new = winnew_ref[:, 0:LANES]
    vw_new = winnew_ref[:, LANES:2 * LANES]
    o_groups = []
    for g in range(A_KV):
        qr_g = _stack_heads(qr, g)
        mk = jnp.dot(sels[g], expand_ref[...], preferred_element_type=F32)
        past_ok = jnp.concatenate([mk] * A_HPG, axis=0) > 0.5
        o_sel = _masked_attn_direct(qr_g, [k_past, k_new], [v_past, v_new], [past_ok, new_ok], [True, False])
        o_win = _masked_attn_direct(qr_g, [kw_past, kw_new], [vw_past, vw_new], [win_ok, new_ok], [False, False])
        o_groups.append(_gate_cols(small, g, 0) * o_cmps[g] + _gate_cols(small, g, 1) * o_sel
                        + _gate_cols(small, g, 2) * o_win)
    o_ref[...] = _assemble_heads(o_groups, tp)

    wb = winbuf_ref[...]
    rolled = pltpu.roll(wb, wbuf - t_valid, 0)
    newr = pltpu.roll(winnew_ref[...], tp - t_valid, 0)
    sub = lax.broadcasted_iota(jnp.int32, (tp, 2 * LANES), 0)
    winout_ref[0:wbuf - tp, :] = rolled[0:wbuf - tp, :]
    winout_ref[wbuf - tp:wbuf, :] = jnp.where(sub < tp - t_valid, rolled[wbuf - tp:wbuf, :], newr)


def _nsa_sample(page_table, cache, q, qr, small, rows, winnew, winbuf, wbd, pe, kg0, t_valid):
    nb, n_pages = page_table.shape
    past_len = n_pages * PAGE_SIZE
    nseg = past_len // CMP_STRIDE
    nsb = past_len // SEL_LEN
    tp = SAMPLE_PAD_T
    wbuf = winbuf.shape[1]
    pool = (jnp.arange(nseg)[:, None] // (SEL_LEN // CMP_STRIDE) == jnp.arange(nsb)[None, :]).astype(BF16)
    expand = (jnp.arange(nsb)[:, None] == jnp.arange(past_len)[None, :] // SEL_LEN).astype(BF16)
    tile = lambda b, pt: (b, 0)
    c2 = lambda b, pt: (0, 0)
    c3 = lambda b, pt: (0, 0, 0)
    gs = pltpu.PrefetchScalarGridSpec(
        num_scalar_prefetch=1,
        grid=(nb,),
        in_specs=[pl.BlockSpec(memory_space=pl.ANY),
                  pl.BlockSpec((tp, A_WIDTH), tile), pl.BlockSpec((tp, A_WIDTH), tile),
                  pl.BlockSpec((tp, LANES), tile), pl.BlockSpec((tp, 4 * LANES), tile),
                  pl.BlockSpec((tp, 2 * LANES), tile),
                  pl.BlockSpec((None, wbuf, 2 * LANES), lambda b, pt: (b, 0, 0)),
                  pl.BlockSpec(wbd.shape, c3), pl.BlockSpec(pe.shape, c3), pl.BlockSpec(kg0.shape, c2),
                  pl.BlockSpec(pool.shape, c2), pl.BlockSpec(expand.shape, c2)],
        out_specs=[pl.BlockSpec((tp, A_WIDTH), tile),
                   pl.BlockSpec((None, wbuf, 2 * LANES), lambda b, pt: (b, 0, 0))],
        scratch_shapes=[pltpu.VMEM((2 * LANES, past_len), F32), pltpu.VMEM((2 * LANES, past_len), F32),
                        pltpu.VMEM((CMP_STRIDE, past_len // CMP_STRIDE, 2 * LANES), F32),
                        pltpu.SemaphoreType.DMA((2,))],
    )
    return pl.pallas_call(
        functools.partial(_nsa_sample_kernel, n_pages=n_pages, past_len=past_len, t_valid=t_valid),
        grid_spec=gs,
        out_shape=[jax.ShapeDtypeStruct((nb * tp, A_WIDTH), F32),
                   jax.ShapeDtypeStruct((nb, wbuf, 2 * LANES), F32)],
        compiler_params=_cparams(("arbitrary",)),
        name="nsa_sample",
    )(page_table.reshape(-1), cache, q, qr, small, rows, winnew, winbuf, wbd, pe, kg0, pool, expand)


MOE_TM = 256
SEG_ALIGN = 8
SEG_BITS = (256, 128, 64, 32, 16, 8)
MOE_RL = -(-(MOE_TM * TOP_K + N_EXPERTS * (SEG_ALIGN - 1)) // LANES) * LANES


def _pack_halves(x, bf16_exact=False):
    w = x.shape[1] // 2
    bits = lax.bitcast_convert_type(x if bf16_exact else x.astype(BF16).astype(F32), jnp.uint32)
    return (bits[:, :w] & jnp.uint32(0xFFFF0000)) | (bits[:, w:] >> 16)


def _unpack_halves(u):
    hi = lax.bitcast_convert_type(u & jnp.uint32(0xFFFF0000), F32).astype(BF16)
    lo = lax.bitcast_convert_type(u << 16, F32).astype(BF16)
    return hi, lo


def _route_and_sort(h2, wrt_ref, brt_ref, xsl_ref, info_ref, cnt_ref, tm, t_mod, t_valid, m_valid):
    ne = N_EXPERTS
    h2b = h2.astype(BF16)
    h2l = (h2 - h2b.astype(F32)).astype(BF16)
    wh, wl = _split(wrt_ref[...])
    lt = _bdot_t(wh, h2b) + _bdot_t(wl, h2b) + _bdot_t(wh, h2l) + brt_ref[...]
    eidx = lax.broadcasted_iota(jnp.int32, (ne, tm), 0)
    rank = jnp.zeros((ne, tm), F32)
    for ep in range(ne):
        v = lt[ep:ep + 1, :]
        rank = rank + jnp.where(v > lt, 1.0, jnp.where((v == lt) & (eidx > ep), 1.0, 0.0))
    sel = rank < TOP_K
    if t_mod is not None:
        tok = pl.program_id(0) * tm + lax.broadcasted_iota(jnp.int32, (1, tm), 1)
        sel = sel & ((tok % t_mod) < t_valid) & (tok < m_valid)
    mx = jnp.max(jnp.where(sel, lt, NEG_BIG), axis=0, keepdims=True)
    ex = jnp.where(sel, jnp.exp(lt - mx), 0.0)
    den = jnp.sum(ex, axis=0, keepdims=True)
    gate = ex / jnp.where(den > 0, den, 1.0)
    self_ = jnp.where(sel, 1.0, 0.0)
    selb = self_.astype(BF16)
    er = lax.broadcasted_iota(jnp.int32, (ne, ne), 0)
    ec = lax.broadcasted_iota(jnp.int32, (ne, ne), 1)
    c = jnp.dot(jnp.where(ec <= er, 1.0, 0.0).astype(BF16), selb, preferred_element_type=F32)
    tr = lax.broadcasted_iota(jnp.int32, (tm, tm), 0)
    tc = lax.broadcasted_iota(jnp.int32, (tm, tm), 1)
    rk = jnp.dot(selb, jnp.where(tr < tc, 1.0, 0.0).astype(BF16), preferred_element_type=F32)
    cnt = jnp.sum(self_, axis=1, keepdims=True)
    cnt_al = jnp.floor((cnt + (SEG_ALIGN - 1)) * (1.0 / SEG_ALIGN)) * SEG_ALIGN
    cnt_b = jnp.broadcast_to(cnt_al, (ne, LANES))
    cnt_ref[...] = cnt_b
    off = jnp.dot(jnp.where(ec < er, 1.0, 0.0).astype(BF16), cnt_b.astype(BF16), preferred_element_type=F32)
    rowidx = off[:, 0:1] + rk
    rows_k, gates_k, exps_k = [], [], []
    for k in range(1, TOP_K + 1):
        mk = sel & (c == k)
        has = jnp.sum(jnp.where(mk, 1.0, 0.0), axis=0, keepdims=True)
        rows_k.append(jnp.sum(jnp.where(mk, rowidx, 0.0), axis=0, keepdims=True) + has - 1.0)
        gates_k.append(jnp.sum(jnp.where(mk, gate, 0.0), axis=0, keepdims=True))
        exps_k.append(jnp.sum(jnp.where(mk, eidx.astype(F32), 0.0), axis=0, keepdims=True))
    info_ref[...] = jnp.concatenate(rows_k + gates_k + exps_k + [jnp.zeros((4, tm), F32)], axis=0)
    ridx = lax.broadcasted_iota(jnp.int32, (MOE_RL, tm), 0).astype(F32)
    perm = jnp.zeros((MOE_RL, tm), F32)
    for k in range(TOP_K):
        perm = perm + jnp.where(ridx == rows_k[k], 1.0, 0.0)
    xs = jnp.dot(perm.astype(BF16), h2b, preferred_element_type=F32)
    xsl_ref[...] = _pack_halves(xs, bf16_exact=True)


def _mixout_kernel(x_ref, hm_ref, on_ref, mod_ref, gmix_ref, gffn_ref,
                   wog_ref, bog_ref, wum_ref, wua_ref, wout_ref, wrt_ref, brt_ref,
                   x1_ref, xsl_ref, info_ref, cnt_ref, *, tm, t_mod, t_valid, m_valid, n_real):
    if n_real is not None:
        @pl.when(pl.program_id(0) >= n_real)
        def _():
            xsl_ref[...] = jnp.zeros(xsl_ref.shape, jnp.uint32)
            info_ref[...] = jnp.zeros(info_ref.shape, F32)
            cnt_ref[...] = jnp.zeros(cnt_ref.shape, F32)

        @pl.when(pl.program_id(0) < n_real)
        def _():
            _mixout_body(x_ref, hm_ref, on_ref, mod_ref, gmix_ref, gffn_ref, wog_ref, bog_ref, wum_ref,
                         wua_ref, wout_ref, wrt_ref, brt_ref, x1_ref, xsl_ref, info_ref, cnt_ref,
                         tm, t_mod, t_valid, m_valid)
    else:
        _mixout_body(x_ref, hm_ref, on_ref, mod_ref, gmix_ref, gffn_ref, wog_ref, bog_ref, wum_ref,
                     wua_ref, wout_ref, wrt_ref, brt_ref, x1_ref, xsl_ref, info_ref, cnt_ref,
                     tm, t_mod, t_valid, m_valid)


def _mixout_body(x_ref, hm_ref, on_ref, mod_ref, gmix_ref, gffn_ref,
                 wog_ref, bog_ref, wum_ref, wua_ref, wout_ref, wrt_ref, brt_ref,
                 x1_ref, xsl_ref, info_ref, cnt_ref, tm, t_mod, t_valid, m_valid):
    d = D_MODEL
    x = x_ref[...]
    sh1, sc1, gt1 = mod_ref[:, 0:d], mod_ref[:, d:2 * d], mod_ref[:, 2 * d:3 * d]
    sh2, sc2 = mod_ref[:, 3 * d:4 * d], mod_ref[:, 4 * d:5 * d]
    h = _rmsnorm_rows(x, gmix_ref[...]) * (1.0 + sc1) + sh1
    hb = h.astype(BF16)
    mo = jnp.dot(hb, wog_ref[:, 0:M_WIDTH], preferred_element_type=F32) + bog_ref[:, 0:M_WIDTH]
    ym = _bdot(_sigmoid(mo) * hm_ref[...], wum_ref[...])
    ya = _bdot(on_ref[...], wua_ref[...])
    ga = jnp.dot(hb, wog_ref[:, M_WIDTH:M_WIDTH + d], preferred_element_type=F32) + bog_ref[:, M_WIDTH:M_WIDTH + d]
    u = _sigmoid(ga) * ym
    gb = (jnp.dot(hb, wog_ref[:, M_WIDTH + d:M_WIDTH + 2 * d], preferred_element_type=F32)
          + bog_ref[:, M_WIDTH + d:M_WIDTH + 2 * d])
    u = u + _sigmoid(gb) * ya
    x1 = x + gt1 * _bdot(u, wout_ref[...])
    x1_ref[...] = x1
    h2 = _rmsnorm_rows(x1, gffn_ref[...]) * (1.0 + sc2) + sh2
    _route_and_sort(h2, wrt_ref, brt_ref, xsl_ref, info_ref, cnt_ref, tm, t_mod, t_valid, m_valid)


def _mixout_with_shared(*refs, n_shared, **kw):
    n_in = 13
    _mixout_kernel(*refs[:n_in], *refs[n_in + n_shared:], **kw)


def _mixout(x2, hm, on, mod3, gmix, gffn, wts, tiles_per_mod, nt_total, tile0=0, shared=None,
            t_mod=None, t_valid=None, m_valid=None):
    m = x2.shape[0]
    tm = MOE_TM
    nt = m // tm
    (wog, bog, wum, wua, wout, wr, br) = wts
    r = mod3.shape[1]
    n_extra = nt_total - tile0 - nt if shared is None else 0
    row = lambda i: (jnp.minimum(i, nt - 1), 0)
    const = lambda i: (0, 0)
    in_specs = [pl.BlockSpec((tm, D_MODEL), row), pl.BlockSpec((tm, M_WIDTH), row),
                pl.BlockSpec((tm, A_WIDTH), row),
                pl.BlockSpec((None, r, 6 * D_MODEL), lambda i: (jnp.minimum(i, nt - 1) // tiles_per_mod, 0, 0)),
                pl.BlockSpec((1, D_MODEL), const), pl.BlockSpec((1, D_MODEL), const),
                pl.BlockSpec(wog.shape, const), pl.BlockSpec(bog.shape, const),
                pl.BlockSpec(wum.shape, const), pl.BlockSpec(wua.shape, const),
                pl.BlockSpec(wout.shape, const), pl.BlockSpec(wr.shape, const),
                pl.BlockSpec(br.shape, const)]
    args = [x2, hm, on, mod3, gmix, gffn, wog, bog, wum, wua, wout, wr, br]
    kw = dict(tm=tm, t_mod=t_mod, t_valid=t_valid, m_valid=m_valid, n_real=nt if n_extra else None)
    body = functools.partial(_mixout_kernel, **kw)
    aliases = {}
    if shared is not None:
        in_specs += [pl.BlockSpec(memory_space=pl.ANY)] * len(shared)
        aliases = {len(args) + j: 1 + j for j in range(len(shared))}
        args += list(shared)
        body = functools.partial(_mixout_with_shared, n_shared=len(shared), **kw)
    return pl.pallas_call(
        body,
        grid=(nt + n_extra,),
        in_specs=in_specs,
        out_specs=[pl.BlockSpec((tm, D_MODEL), row),
                   pl.BlockSpec((MOE_RL, D_MODEL // 2), lambda i: (tile0 + i, 0)),
                   pl.BlockSpec((16, tm), lambda i: (0, tile0 + i)),
                   pl.BlockSpec((None, N_EXPERTS, LANES), lambda i: (tile0 + i, 0, 0))],
        out_shape=[jax.ShapeDtypeStruct((m, D_MODEL), F32),
                   jax.ShapeDtypeStruct((nt_total * MOE_RL, D_MODEL // 2), jnp.uint32),
                   jax.ShapeDtypeStruct((16, nt_total * tm), F32),
                   jax.ShapeDtypeStruct((nt_total, N_EXPERTS, LANES), F32)],
        input_output_aliases=aliases,
        compiler_params=_cparams(("arbitrary" if n_extra else "parallel",)),
        name="mixout",
    )(*args)


MOE_BM = 256
MOE_CH = 512


def _moe_kernel(be_ref, na_ref, grp_ref,
                xsl_ref, wgu_ref, bgu_ref, wdn_ref, bdn_ref, ysl_in_ref, ysl_ref,
                wgu_bf, wdn_bf, xbuf, ybuf, sem_in, sem_out, *, trash_row0):
    del ysl_in_ref
    i = pl.program_id(0)
    na = na_ref[0]
    e = be_ref[i]
    prev = be_ref[jnp.maximum(i - 1, 0)]
    n_grp = MOE_BM // SEG_ALIGN

    def group_copies(blk, inbound, slot=None):
        slot = blk % 2 if slot is None else slot
        cps = []
        for r in range(n_grp):
            v = grp_ref[blk * n_grp + r]
            vm_rows = pl.ds(r * SEG_ALIGN, SEG_ALIGN)
            if inbound:
                row = pl.multiple_of(jnp.maximum(v, 0), SEG_ALIGN)
                cps.append(pltpu.make_async_copy(xsl_ref.at[pl.ds(row, SEG_ALIGN), :],
                                                 xbuf.at[slot, vm_rows, :], sem_in.at[slot]))
            else:
                spare = trash_row0 + slot * MOE_BM + r * SEG_ALIGN
                row = pl.multiple_of(jnp.where(v >= 0, v, spare), SEG_ALIGN)
                cps.append(pltpu.make_async_copy(ybuf.at[slot, vm_rows, :],
                                                 ysl_ref.at[pl.ds(row, SEG_ALIGN), :], sem_out.at[slot]))
        return cps

    def start_gather(blk):
        for cp in group_copies(blk, True):
            cp.start()

    def start_scatter(blk):
        for cp in group_copies(blk, False):
            cp.start()

    def wait_rows(blk, sem, inbound):
        slot = blk % 2
        if inbound:
            pltpu.make_async_copy(xsl_ref.at[pl.ds(0, MOE_BM), :], xbuf.at[slot], sem.at[slot]).wait()
        else:
            pltpu.make_async_copy(ybuf.at[slot], ysl_ref.at[pl.ds(0, MOE_BM), :], sem.at[slot]).wait()

    @pl.when(i == 0)
    def _():
        start_gather(i)

    @pl.when(i + 1 < na)
    def _():
        start_gather(i + 1)

    @pl.when((i < na) & ((i == 0) | (e != prev)))
    def _():
        for j in range(2 * D_EXPERT // MOE_CH):
            wgu_bf[:, j * MOE_CH:(j + 1) * MOE_CH] = wgu_ref[:, j * MOE_CH:(j + 1) * MOE_CH].astype(BF16)
        for j in range(D_EXPERT // MOE_CH):
            wdn_bf[j * MOE_CH:(j + 1) * MOE_CH, :] = wdn_ref[j * MOE_CH:(j + 1) * MOE_CH, :].astype(BF16)

    @pl.when(i < na)
    def _():
        slot = i % 2
        wait_rows(i, sem_in, True)

        @pl.when(i >= 2)
        def _():
            wait_rows(i - 2, sem_out, False)

        half = D_MODEL // 2
        xh, xl = _unpack_halves(xbuf[slot])

        def xdot(c0, c1):
            return (jnp.dot(xh, wgu_bf[0:half, c0:c1], preferred_element_type=F32)
                    + jnp.dot(xl, wgu_bf[half:D_MODEL, c0:c1], preferred_element_type=F32))

        acc = jnp.zeros((MOE_BM, D_MODEL), F32) + bdn_ref[...]
        for j in range(D_EXPERT // MOE_CH):
            lo, hi = j * MOE_CH, (j + 1) * MOE_CH
            gj = xdot(lo, hi) + bgu_ref[:, lo:hi]
            uj = xdot(D_EXPERT + lo, D_EXPERT + hi) + bgu_ref[:, D_EXPERT + lo:D_EXPERT + hi]
            gj = jnp.minimum(gj, SWIGLU_LIMIT)
            uj = jnp.clip(uj, -SWIGLU_LIMIT, SWIGLU_LIMIT)
            act = gj * _sigmoid(SWIGLU_ALPHA * gj) * (uj + 1.0)
            acc = acc + jnp.dot(act.astype(BF16), wdn_bf[lo:hi, :], preferred_element_type=F32)
        ybuf[slot] = _pack_halves(acc)
        start_scatter(i)

        @pl.when(i == na - 1)
        def _():
            @pl.when(i >= 1)
            def _():
                wait_rows(i - 1, sem_out, False)
            wait_rows(i, sem_out, False)


def _moe_experts(plan, xsl, w_gu, b_gu, w_dn, b_dn):
    block_e, n_active, grp_rows = plan
    nblk = block_e.shape[0]
    n_rows = xsl.shape[0]
    out_rows = n_rows + 2 * MOE_BM
    wmap = lambda i, be, *_: (be[i], 0, 0)
    anyspec = pl.BlockSpec(memory_space=pl.ANY)
    gs = pltpu.PrefetchScalarGridSpec(
        num_scalar_prefetch=3,
        grid=(nblk,),
        in_specs=[anyspec,
                  pl.BlockSpec((None, D_MODEL, 2 * D_EXPERT), wmap),
                  pl.BlockSpec((None, 1, 2 * D_EXPERT), wmap),
                  pl.BlockSpec((None, D_EXPERT, D_MODEL), wmap),
                  pl.BlockSpec((None, 1, D_MODEL), wmap),
                  anyspec],
        out_specs=anyspec,
        scratch_shapes=[pltpu.VMEM((D_MODEL, 2 * D_EXPERT), BF16), pltpu.VMEM((D_EXPERT, D_MODEL), BF16),
                        pltpu.VMEM((2, MOE_BM, D_MODEL // 2), jnp.uint32),
                        pltpu.VMEM((2, MOE_BM, D_MODEL // 2), jnp.uint32),
                        pltpu.SemaphoreType.DMA((2,)), pltpu.SemaphoreType.DMA((2,))],
    )
    return pl.pallas_call(
        functools.partial(_moe_kernel, trash_row0=n_rows),
        grid_spec=gs,
        out_shape=jax.ShapeDtypeStruct((out_rows, D_MODEL // 2), jnp.uint32),
        input_output_aliases={8: 0},
        compiler_params=_cparams(("arbitrary",)),
        name="moe_experts",
    )(*plan, xsl, w_gu, b_gu.reshape(N_EXPERTS, 1, -1), w_dn, b_dn.reshape(N_EXPERTS, 1, -1),
      jnp.zeros((out_rows, D_MODEL // 2), jnp.uint32))


def _combine_kernel(ysl_ref, info_ref, x1_ref, mod_ref, y_ref, *, tm):
    info = info_ref[...]
    info_t = jnp.transpose(jnp.concatenate([info, jnp.zeros((LANES - info.shape[0], tm), F32)], axis=0))
    ridx = lax.broadcasted_iota(jnp.int32, (tm, MOE_RL), 1).astype(F32)
    pg = jnp.zeros((tm, MOE_RL), F32)
    for k in range(TOP_K):
        pg = pg + jnp.where(ridx == info_t[:, k:k + 1], info_t[:, TOP_K + k:TOP_K + k + 1], 0.0)
    pgb = pg.astype(BF16)
    yh, yl = _unpack_halves(ysl_ref[...])
    half = D_MODEL // 2
    gt2 = mod_ref[:, 5 * D_MODEL:6 * D_MODEL]
    for c, yy in ((0, yh), (1, yl)):
        moe = jnp.dot(pgb, yy, preferred_element_type=F32)
        y_ref[:, c * half:(c + 1) * half] = (x1_ref[:, c * half:(c + 1) * half]
                                             + gt2[:, c * half:(c + 1) * half] * moe)


def _combine(ysl, info, x1, mod3, tiles_per_mod, tile0=0):
    m = x1.shape[0]
    tm = MOE_TM
    r = mod3.shape[1]
    return pl.pallas_call(
        functools.partial(_combine_kernel, tm=tm),
        grid=(m // tm,),
        in_specs=[pl.BlockSpec((MOE_RL, D_MODEL // 2), lambda i: (tile0 + i, 0)),
                  pl.BlockSpec((16, tm), lambda i: (0, tile0 + i)),
                  pl.BlockSpec((tm, D_MODEL), lambda i: (i, 0)),
                  pl.BlockSpec((None, r, 6 * D_MODEL), lambda i: (i // tiles_per_mod, 0, 0))],
        out_specs=pl.BlockSpec((tm, D_MODEL), lambda i: (i, 0)),
        out_shape=jax.ShapeDtypeStruct((m, D_MODEL), F32),
        compiler_params=_cparams(("parallel",)),
        name="moe_combine",
    )(ysl, info, x1, mod3)


def _moe_plan(cnt):
    cnt = cnt.astype(jnp.int32)
    nt = cnt.shape[0]
    so = jnp.cumsum(cnt, axis=1) - cnt + (jnp.arange(nt) * MOE_RL)[:, None]
    ce = jnp.cumsum(cnt, axis=0)
    cs = ce - cnt
    tot = ce[-1]
    nblk_e = (tot + MOE_BM - 1) // MOE_BM
    blk_end = jnp.cumsum(nblk_e)
    max_rows = nt * MOE_TM * TOP_K + nt * N_EXPERTS * (SEG_ALIGN - 1)
    n_blocks = -(-max_rows // MOE_BM) + N_EXPERTS
    bidx = jnp.arange(n_blocks)
    block_e = jnp.minimum(jnp.sum(blk_end[None, :] <= bidx[:, None], axis=1), N_EXPERTS - 1).astype(jnp.int32)
    is_e = (jnp.arange(N_EXPERTS)[:, None] == block_e[None, :]).astype(jnp.int32)
    per_block = lambda a: jnp.sum(a[..., :, None] * is_e, axis=-2)
    block_r0 = (bidx - per_block(blk_end - nblk_e)) * MOE_BM
    x = block_r0[:, None] + jnp.arange(MOE_BM // SEG_ALIGN)[None, :] * SEG_ALIGN
    ce_b = per_block(ce)[:, :, None]
    cs_b = per_block(cs)[:, :, None]
    inside = (cs_b <= x[None]) & (x[None] < ce_b)
    grp = x + jnp.sum(jnp.where(inside, per_block(so - cs)[:, :, None], 0), axis=0)
    grp = jnp.where(x < per_block(tot)[:, None], grp, -1)
    n_active = blk_end[-1].reshape(1)
    i32 = lambda a: a.reshape(-1).astype(jnp.int32)
    return block_e, i32(n_active), i32(grp)


def _rope_tables(pos):
    half = ROT_DIM // 2
    inv = ROPE_THETA ** (-jnp.arange(half, dtype=F32) * (2.0 / ROT_DIM))
    ang = pos.astype(F32)[:, None] * inv[None, :]
    cos, sin = jnp.cos(ang), jnp.sin(ang)
    n = pos.shape[0]
    ones = jnp.ones((n, A_DH - ROT_DIM), F32)
    zeros_h = jnp.zeros((n, half), F32)
    zeros_r = jnp.zeros((n, A_DH - ROT_DIM), F32)
    cos64 = jnp.concatenate([cos, cos, ones], axis=1)
    sprev64 = jnp.concatenate([zeros_h, sin, zeros_r], axis=1)
    snext64 = jnp.concatenate([-sin, zeros_h, zeros_r], axis=1)
    two = lambda a: jnp.concatenate([a, a], axis=1)
    return two(cos64), two(sprev64), two(snext64)


def _prep_weights(w_in, b_in, q_norm_g, k_norm_g, cmp_pe_k, cmp_pe_v, cmp_w_k, cmp_w_v,
                  w_up_m, w_up_a, w_out, w_router, b_router):
    b2 = b_in.reshape(1, N_IN)
    wm = w_in[:, OFF_MQ:OFF_MO].astype(BF16)
    bm = b2[:, OFF_MQ:OFF_MO]
    wq = w_in[:, OFF_AQ:OFF_AKV].astype(BF16)
    bq = b2[:, OFF_AQ:OFF_AKV]
    wkv = w_in[:, OFF_AKV:OFF_AG].astype(BF16)
    bkv = b2[:, OFF_AKV:OFF_AG]
    n_small = 2 * M_HEADS + 3 * A_HEADS
    ws = jnp.concatenate([w_in[:, OFF_MI:OFF_AQ], w_in[:, OFF_AG:OFF_GA],
                          jnp.zeros((D_MODEL, LANES - n_small), F32)], axis=1)
    bs = jnp.concatenate([b2[:, OFF_MI:OFF_AQ], b2[:, OFF_AG:OFF_GA], jnp.zeros((1, LANES - n_small), F32)], axis=1)
    qg = jnp.tile(q_norm_g, A_HEADS).reshape(1, A_WIDTH)
    kg = jnp.stack([jnp.tile(k_norm_g[1], A_KV), jnp.tile(k_norm_g[2], A_KV)], axis=0)
    kg0 = jnp.tile(k_norm_g[0], A_KV).reshape(1, LANES)
    hid = jnp.arange(A_WIDTH) // A_DH
    bd = jnp.where(hid[:, None] == hid[None, :], 1.0 / A_DH, 0.0).astype(BF16)
    inproj_w = (wm, bm, wq, bq, wkv, bkv, ws, bs, qg, kg, bd)

    z = jnp.zeros((CMP_LEN, A_DH, A_DH), F32)
    r0 = jnp.concatenate([cmp_w_k, z, z, z], axis=2)
    r1 = jnp.concatenate([z, cmp_w_k, z, z], axis=2)
    r2 = jnp.concatenate([z, z, cmp_w_v, z], axis=2)
    r3 = jnp.concatenate([z, z, z, cmp_w_v], axis=2)
    wbd = jnp.concatenate([r0, r1, r2, r3], axis=1).astype(BF16)
    pe = jnp.concatenate([cmp_pe_k, cmp_pe_k, cmp_pe_v, cmp_pe_v], axis=1).reshape(CMP_LEN, 1, 2 * LANES)

    wog = jnp.concatenate([w_in[:, OFF_MO:OFF_MI], w_in[:, OFF_GA:N_IN]], axis=1).astype(BF16)
    bog = jnp.concatenate([b2[:, OFF_MO:OFF_MI], b2[:, OFF_GA:N_IN]], axis=1)
    mixout_w = (wog, bog, w_up_m.astype(BF16), w_up_a.astype(BF16), w_out.astype(BF16),
                w_router.T, b_router.reshape(N_EXPERTS, 1))
    return inproj_w, (wbd, pe, kg0), mixout_w


def _pick_tile(m, pref):
    t = pref
    while m % t:
        t //= 2
    return t


def kernel(x_prompt, x_sample, cache_nsa_kv, state_win_kv, state_mlstm_C, state_mlstm_n, state_mlstm_m, page_table, c_prompt, c_sample, w_ada, b_ada, g_mix, g_ffn, w_in, b_in, q_norm_g, k_norm_g, cmp_pe_k, cmp_pe_v, cmp_w_k, cmp_w_v, w_up_m, w_up_a, w_out, w_router, b_router, w_gu, b_gu, w_dn, b_dn):
    depth = w_in.shape[0]
    assert depth == 1
    B, T, D = x_prompt.shape
    DB, TS, _ = x_sample.shape
    n_pages = page_table.shape[1]
    past_len = n_pages * PAGE_SIZE
    wbuf = state_win_kv.shape[2]
    tp = SAMPLE_PAD_T
    assert TS <= tp and wbuf % tp == 0 and T % 128 == 0

    l = 0
    inproj_w, cmp_w, mixout_w = _prep_weights(
        w_in[l], b_in[l], q_norm_g[l], k_norm_g[l], cmp_pe_k[l], cmp_pe_v[l], cmp_w_k[l], cmp_w_v[l],
        w_up_m[l], w_up_a[l], w_out[l], w_router[l], b_router[l])
    wbd, pe, kg0 = cmp_w
    gmix = g_mix[l].reshape(1, D)
    gffn = g_ffn[l].reshape(1, D)

    nc = B + DB
    nc_pad = -(-nc // SUBLANES) * SUBLANES
    c_all = jnp.concatenate([c_prompt, c_sample, jnp.zeros((nc_pad - nc, D), F32)], axis=0)
    mod = _adaln(c_all, w_ada[l], b_ada[l])
    mod_p = mod[:B].reshape(B, 1, 6 * D)
    mod_s = jnp.repeat(mod[B:B + DB], tp, axis=0).reshape(1, DB * tp, 6 * D)

    mp = B * T
    tm = _pick_tile(T, 256)
    xp = x_prompt.reshape(mp, D)
    tabs_p = _rope_tables(jnp.arange(T, dtype=jnp.int32))
    mq, mk, mv, q, qr, rows, win, small, rows_t = _inproj(xp, mod_p, gmix, tabs_p, inproj_w, tm, T // tm, T // tm,
                                                          rows_t_batches=B)
    Lp = _pick_tile(T, 128)
    hm, C_p, n_p, m_p = _mlstm(mq, mk, mv, small, B, T, T, Lp)
    o_nsa = _nsa_prompt(q, qr, small, rows, win, wbd, pe, kg0, B, T)
    assert T % MOE_TM == 0
    ms_pad = -(-(DB * tp) // MOE_TM) * MOE_TM
    nt_p = mp // MOE_TM
    nt_all = nt_p + ms_pad // MOE_TM
    x1_p, xsl, info, cnt = _mixout(xp, hm, o_nsa, mod_p, gmix, gffn, mixout_w, T // MOE_TM, nt_all)

    ms = DB * tp
    xs_pad = jnp.concatenate([x_sample, jnp.zeros((DB, tp - TS, D), F32)], axis=1).reshape(ms, D)
    pos_s = past_len + jnp.tile(jnp.arange(tp, dtype=jnp.int32), DB)
    tabs_s = _rope_tables(pos_s)
    mq_s, mk_s, mv_s, q_s, qr_s, rows_s, win_s, small_s = _inproj(xs_pad, mod_s, gmix, tabs_s, inproj_w, ms, 1, 1)
    hm_s, C_s, n_s, m_s = _mlstm(mq_s, mk_s, mv_s, small_s, DB, tp, TS, tp,
                                 state=(state_mlstm_C[l], state_mlstm_n[l], state_mlstm_m[l]))
    cache2 = jnp.transpose(cache_nsa_kv[l], (0, 2, 3, 4, 1)).reshape(cache_nsa_kv.shape[1], 4 * LANES, PAGE_SIZE)
    winbuf = state_win_kv[l].reshape(DB, wbuf, 2 * LANES)
    o_nsa_s, win_out_s = _nsa_sample(page_table, cache2, q_s, qr_s, small_s, rows_s, win_s, winbuf,
                                     wbd, pe, kg0, TS)
    assert ms_pad == MOE_TM
    rpad = lambda a: jnp.concatenate([a, jnp.zeros((ms_pad - ms, a.shape[1]), a.dtype)], axis=0) if ms_pad > ms else a
    mod_sp = rpad(mod_s[0])[None]
    x1_s, xsl, info, cnt = _mixout(rpad(xs_pad), rpad(hm_s), rpad(o_nsa_s), mod_sp, gmix, gffn, mixout_w,
                                   1, nt_all, tile0=nt_p, shared=(xsl, info, cnt),
                                   t_mod=tp, t_valid=TS, m_valid=ms)

    ysl = _moe_experts(_moe_plan(cnt[:, :, 0]), xsl, w_gu[l], b_gu[l], w_dn[l], b_dn[l])
    y_p = _combine(ysl, info, x1_p, mod_p, T // MOE_TM).reshape(B, T, D)
    y_s_all = _combine(ysl, info, x1_s, mod_sp, 1, tile0=nt_p)
    valid = lambda a: a.reshape(DB, tp, -1)[:, :TS].reshape(DB * TS, -1)
    y_s = valid(y_s_all[:ms]).reshape(DB, TS, D)

    kv_p = jnp.transpose(rows_t.reshape(B, 4, A_KV, A_DH, T), (0, 4, 1, 2, 3))[None]
    kv_s = valid(rows_s).reshape(1, DB, TS, 4, A_KV, A_DH)
    wp = min(WINDOW, T)
    win_p = win.reshape(B, T, 2, A_KV, A_DH)[:, T - wp:][None]
    win_s_out = win_out_s.reshape(1, DB, wbuf, 2, A_KV, A_DH)
    return (y_p, y_s, kv_p, kv_s, win_p, win_s_out,
            C_p[None], n_p[None], m_p[None], C_s[None], n_s[None], m_s[None])
```

```python
import functools
import math

import jax
import jax.numpy as jnp
from jax import lax
from jax.experimental import pallas as pl
from jax.experimental.pallas import tpu as pltpu

F32 = jnp.float32
BF16 = jnp.bfloat16

D_MODEL = 1024
M_HEADS = 4
M_DH = 128
M_WIDTH = M_HEADS * M_DH
A_HEADS = 8
A_KV = 2
A_HPG = A_HEADS // A_KV
A_DH = 64
A_WIDTH = A_HEADS * A_DH
CMP_STRIDE = 16
CMP_LEN = 32
SEL_LEN = 64
N_SEL = 16
WINDOW = 512
PAGE_SIZE = 128
ROPE_THETA = 500000.0
ROT_DIM = A_DH // 4
ATT_SCALE = A_DH ** -0.5
N_EXPERTS = 32
TOP_K = 4
D_EXPERT = D_MODEL
SWIGLU_LIMIT = 7.0
SWIGLU_ALPHA = 1.702
EPS = 1e-6

OFF_MQ, OFF_MK, OFF_MV, OFF_MO = 0, M_WIDTH, 2 * M_WIDTH, 3 * M_WIDTH
OFF_MI = 4 * M_WIDTH
OFF_MF = OFF_MI + M_HEADS
OFF_AQ = OFF_MF + M_HEADS
OFF_AKV = OFF_AQ + A_WIDTH
OFF_AG = OFF_AKV + 6 * A_KV * A_DH
OFF_GA = OFF_AG + 3 * A_HEADS
OFF_GB = OFF_GA + D_MODEL
N_IN = OFF_GB + D_MODEL

LANES = 128
SUBLANES = 8
VMEM_LIMIT = 56 * 1024 * 1024

NEG_BIG = -1e30
M_INIT = -1e29
LOG2E = 1.4426950408889634
SAMPLE_PAD_T = 8


def _cparams(sem):
    return pltpu.CompilerParams(dimension_semantics=sem, vmem_limit_bytes=VMEM_LIMIT)


def _bdot(a, b):
    return jnp.dot(a.astype(BF16), b.astype(BF16), preferred_element_type=F32)


def _bdot_t(a, b):
    return lax.dot_general(a.astype(BF16), b.astype(BF16), (((1,), (1,)), ((), ())),
                           preferred_element_type=F32)


def _split(a):
    hi = a.astype(BF16)
    lo = (a - hi.astype(F32)).astype(BF16)
    return hi, lo


def _dot3(a, b):
    ah, al = _split(a)
    bh, bl = _split(b)
    return (jnp.dot(ah, bh, preferred_element_type=F32) + jnp.dot(al, bh, preferred_element_type=F32)
            + jnp.dot(ah, bl, preferred_element_type=F32))


def _dot2_exact_rhs(a, b_bf16):
    ah, al = _split(a)
    return jnp.dot(ah, b_bf16, preferred_element_type=F32) + jnp.dot(al, b_bf16, preferred_element_type=F32)


def _sigmoid(x):
    return 0.5 * jnp.tanh(0.5 * x) + 0.5


def _rmsnorm_rows(x, g):
    return x * lax.rsqrt(jnp.mean(x * x, axis=-1, keepdims=True) + EPS) * g


def _adaln_kernel(c_ref, w_ref, b_ref, o_ref):
    c = c_ref[...]
    s = c * _sigmoid(c)
    o_ref[...] = _dot3(s, w_ref[...]) + b_ref[...]


def _adaln(c, w, b):
    mc, d = c.shape
    n = w.shape[1]
    tn = 1024
    return pl.pallas_call(
        _adaln_kernel,
        grid=(n // tn,),
        in_specs=[pl.BlockSpec((mc, d), lambda j: (0, 0)),
                  pl.BlockSpec((d, tn), lambda j: (0, j)),
                  pl.BlockSpec((1, tn), lambda j: (0, j))],
        out_specs=pl.BlockSpec((mc, tn), lambda j: (0, j)),
        out_shape=jax.ShapeDtypeStruct((mc, n), F32),
        compiler_params=_cparams(("parallel",)),
        name="adaln",
    )(c, w, b.reshape(1, n))


def _head_norm(z, bd, gain):
    ms = _dot2_exact_rhs(z * z, bd)
    return z * lax.rsqrt(ms + EPS) * gain


def _rope(z, cos, s_prev, s_next):
    w = z.shape[1]
    rep = w // LANES
    if rep > 1:
        cos = jnp.concatenate([cos] * rep, axis=1)
        s_prev = jnp.concatenate([s_prev] * rep, axis=1)
        s_next = jnp.concatenate([s_next] * rep, axis=1)
    z_prev = pltpu.roll(z, ROT_DIM // 2, 1)
    z_next = pltpu.roll(z, w - ROT_DIM // 2, 1)
    return z * cos + z_prev * s_prev + z_next * s_next


def _inproj_kernel(x_ref, mod_ref, gmix_ref, cos_ref, sp_ref, sn_ref,
                   wm_ref, bm_ref, wq_ref, bq_ref, wkv_ref, bkv_ref, ws_ref, bs_ref,
                   qg_ref, kg_ref, bd_ref,
                   mq_ref, mk_ref, mv_ref, q_ref, qr_ref, rows_ref, win_ref, small_ref, rows_t_ref=None):
    x = x_ref[...]
    sh1 = mod_ref[:, 0:D_MODEL]
    sc1 = mod_ref[:, D_MODEL:2 * D_MODEL]
    h = _rmsnorm_rows(x, gmix_ref[...]) * (1.0 + sc1) + sh1
    hb = h.astype(BF16)

    mq_ref[...] = jnp.dot(hb, wm_ref[:, 0:M_WIDTH], preferred_element_type=F32) + bm_ref[:, 0:M_WIDTH]
    mk = jnp.dot(hb, wm_ref[:, M_WIDTH:2 * M_WIDTH], preferred_element_type=F32) + bm_ref[:, M_WIDTH:2 * M_WIDTH]
    mk_ref[...] = mk * (M_DH ** -0.5)
    mv_ref[...] = (jnp.dot(hb, wm_ref[:, 2 * M_WIDTH:3 * M_WIDTH], preferred_element_type=F32)
                   + bm_ref[:, 2 * M_WIDTH:3 * M_WIDTH])

    cos, sp, sn = cos_ref[...], sp_ref[...], sn_ref[...]
    zq = jnp.dot(hb, wq_ref[...], preferred_element_type=F32) + bq_ref[...]
    qn = _head_norm(zq, bd_ref[...], qg_ref[...])
    q_ref[...] = qn
    qr_ref[...] = _rope(qn, cos, sp, sn)

    zkv = jnp.dot(hb, wkv_ref[...], preferred_element_type=F32) + bkv_ref[...]
    bd2 = bd_ref[0:LANES, 0:LANES]
    ksel = _head_norm(zkv[:, 2 * LANES:3 * LANES], bd2, kg_ref[0:1, :])
    rows = jnp.concatenate([zkv[:, 0:2 * LANES], _rope(ksel, cos, sp, sn), zkv[:, 3 * LANES:4 * LANES]], axis=1)
    rows_ref[...] = rows
    if rows_t_ref is not None:
        rows_t_ref[...] = jnp.transpose(rows)
    kwin = _head_norm(zkv[:, 4 * LANES:5 * LANES], bd2, kg_ref[1:2, :])
    win_ref[:, 0:LANES] = _rope(kwin, cos, sp, sn)
    win_ref[:, LANES:2 * LANES] = zkv[:, 5 * LANES:6 * LANES]

    small_ref[...] = _dot3(h, ws_ref[...]) + bs_ref[...]


def _inproj(x2, mod3, gmix, tabs, wts, tm, tiles_per_mod, pos_tiles, rows_t_batches=None):
    m = x2.shape[0]
    cos_t, sp_t, sn_t = tabs
    (wm, bm, wq, bq, wkv, bkv, ws, bs, qg, kg, bd) = wts
    r = mod3.shape[1]
    row = lambda i: (i, 0)
    const = lambda i: (0, 0)
    tab = lambda i: (i % pos_tiles, 0)
    in_specs = [
        pl.BlockSpec((tm, D_MODEL), row),
        pl.BlockSpec((None, r, 6 * D_MODEL), lambda i: (i // tiles_per_mod, 0, 0)),
        pl.BlockSpec((1, D_MODEL), const),
        pl.BlockSpec((tm, LANES), tab), pl.BlockSpec((tm, LANES), tab), pl.BlockSpec((tm, LANES), tab),
        pl.BlockSpec(wm.shape, const), pl.BlockSpec(bm.shape, const),
        pl.BlockSpec(wq.shape, const), pl.BlockSpec(bq.shape, const),
        pl.BlockSpec(wkv.shape, const), pl.BlockSpec(bkv.shape, const),
        pl.BlockSpec(ws.shape, const), pl.BlockSpec(bs.shape, const),
        pl.BlockSpec(qg.shape, const), pl.BlockSpec(kg.shape, const), pl.BlockSpec(bd.shape, const),
    ]
    widths = (M_WIDTH, M_WIDTH, M_WIDTH, A_WIDTH, A_WIDTH, 4 * LANES, 2 * LANES, LANES)
    out_specs = [pl.BlockSpec((tm, w), row) for w in widths]
    out_shape = [jax.ShapeDtypeStruct((m, w), F32) for w in widths]
    if rows_t_batches is not None:
        out_specs.append(pl.BlockSpec((None, 4 * LANES, tm), lambda i: (i // tiles_per_mod, 0, i % tiles_per_mod)))
        out_shape.append(jax.ShapeDtypeStruct((rows_t_batches, 4 * LANES, m // rows_t_batches), F32))
    return pl.pallas_call(
        _inproj_kernel,
        grid=(m // tm,),
        in_specs=in_specs,
        out_specs=out_specs,
        out_shape=out_shape,
        compiler_params=_cparams(("parallel",)),
        name="inproj",
    )(x2, mod3, gmix, cos_t, sp_t, sn_t, wm, bm, wq, bq, wkv, bkv, ws, bs, qg, kg, bd)


def _log_sigmoid(x):
    return jnp.minimum(x, 0.0) - jnp.log(1.0 + jnp.exp(-jnp.abs(x)))


def _mlstm_kernel(*refs, L, t_valid, has_state):
    if has_state:
        q_ref, k_ref, v_ref, s_ref, c0_ref, n0_ref, m0_ref, h_ref, c_ref, n_ref, m_ref = refs
    else:
        q_ref, k_ref, v_ref, s_ref, h_ref, c_ref, n_ref, m_ref = refs
    c = pl.program_id(1)

    @pl.when(c == 0)
    def _():
        if has_state:
            c_ref[...] = c0_ref[...]
            n_ref[...] = n0_ref[...]
            m_ref[...] = m0_ref[...]
        else:
            c_ref[...] = jnp.zeros(c_ref.shape, F32)
            n_ref[...] = jnp.zeros(n_ref.shape, F32)
            m_ref[...] = jnp.zeros(m_ref.shape, F32)

    row = lax.broadcasted_iota(jnp.int32, (L, L), 0)
    col = lax.broadcasted_iota(jnp.int32, (L, L), 1)
    causal = col <= row
    eye = col == row
    tok_col = c * L + lax.broadcasted_iota(jnp.int32, (L, 1), 0)
    valid_col = tok_col < t_valid
    for hd in range(M_HEADS):
        lo, hi = hd * M_DH, (hd + 1) * M_DH
        q = q_ref[:, lo:hi]
        k = k_ref[:, lo:hi]
        v = v_ref[:, lo:hi]
        i_col = s_ref[:, hd:hd + 1]
        lf_col = _log_sigmoid(s_ref[:, M_HEADS + hd:M_HEADS + hd + 1])
        lf_col = jnp.where(valid_col, lf_col, 0.0)
        i_col = jnp.where(valid_col, i_col, -jnp.inf)
        if L == LANES:
            i_col = jnp.broadcast_to(i_col, (L, L))
            lf_c = jnp.broadcast_to(lf_col, (L, L))
            p0 = lf_c.astype(BF16)
            r1 = lf_c - p0.astype(F32)
            p1 = r1.astype(BF16)
            p2 = (r1 - p1.astype(F32)).astype(BF16)
            tril = jnp.where(causal, 1.0, 0.0).astype(BF16)
            b_col = (jnp.dot(tril, p0, preferred_element_type=F32) + jnp.dot(tril, p1, preferred_element_type=F32)
                     + jnp.dot(tril, p2, preferred_element_type=F32))
            i_row = jnp.transpose(i_col)[0:1, :]
            b_row = jnp.transpose(b_col)[0:1, :]
        else:
            i_row = jnp.sum(jnp.where(eye, i_col, 0.0), axis=0, keepdims=True)
            lf_row = jnp.sum(jnp.where(eye, lf_col, 0.0), axis=0, keepdims=True)
            b_col = jnp.sum(jnp.where(causal, lf_row, 0.0), axis=1, keepdims=True)
            b_row = jnp.sum(jnp.where(row <= col, lf_col, 0.0), axis=0, keepdims=True)
        m_prev = m_ref[:, hd:hd + 1]
        dmat = jnp.where(causal, b_col - b_row + i_row, -jnp.inf)
        inter = b_col + m_prev
        m_row = jnp.maximum(jnp.max(dmat, axis=1, keepdims=True), inter)
        w = jnp.exp(dmat - m_row)
        w_inter = jnp.exp(inter - m_row)
        s = _bdot_t(q, k) * w
        cm = c_ref[hd]
        nv = n_ref[hd]
        num = _bdot(s, v) + w_inter * _bdot_t(q, cm)
        den = jnp.sum(s, axis=1, keepdims=True) + w_inter * jnp.sum(q * nv, axis=1, keepdims=True)
        h_ref[:, lo:hi] = num / jnp.maximum(jnp.abs(den), jnp.exp(-m_row))
        b_last = b_col[L - 1:L, 0:1]
        dec_col = b_last - b_col + i_col
        dec_row = b_last - b_row + i_row
        m_new = jnp.maximum(b_last + m_prev, jnp.max(dec_row, axis=1, keepdims=True))
        ws_col = jnp.exp(dec_col - m_new)
        wc = jnp.exp(b_last + m_prev - m_new)
        vw = (v * ws_col).astype(BF16)
        upd = lax.dot_general(vw, k.astype(BF16), (((0,), (0,)), ((), ())), preferred_element_type=F32)
        c_ref[hd] = wc * cm + upd
        n_ref[hd] = wc * nv + jnp.sum(k * ws_col, axis=0, keepdims=True)
        m_ref[:, hd:hd + 1] = m_new


def _mlstm(mq, mk, mv, small, nb, t_pad, t_valid, L, state=None):
    nc = t_pad // L
    has_state = state is not None
    blk = lambda b, c: (b * nc + c, 0)
    st4 = lambda b, c: (b, 0, 0, 0)
    st3 = lambda b, c: (b, 0, 0)
    in_specs = [pl.BlockSpec((L, M_WIDTH), blk)] * 3 + [pl.BlockSpec((L, LANES), blk)]
    args = [mq, mk, mv, small]
    if has_state:
        c0, n0, m0 = state
        in_specs += [pl.BlockSpec((None, M_HEADS, M_DH, M_DH), st4),
                     pl.BlockSpec((None, M_HEADS, 1, M_DH), st4),
                     pl.BlockSpec((None, 1, M_HEADS), st3)]
        args += [c0, n0.reshape(nb, M_HEADS, 1, M_DH), m0.reshape(nb, 1, M_HEADS)]
    out_specs = [pl.BlockSpec((L, M_WIDTH), blk),
                 pl.BlockSpec((None, M_HEADS, M_DH, M_DH), st4),
                 pl.BlockSpec((None, M_HEADS, 1, M_DH), st4),
                 pl.BlockSpec((None, 1, M_HEADS), st3)]
    out_shape = [jax.ShapeDtypeStruct((nb * t_pad, M_WIDTH), F32),
                 jax.ShapeDtypeStruct((nb, M_HEADS, M_DH, M_DH), F32),
                 jax.ShapeDtypeStruct((nb, M_HEADS, 1, M_DH), F32),
                 jax.ShapeDtypeStruct((nb, 1, M_HEADS), F32)]
    h, cs, ns, ms = pl.pallas_call(
        functools.partial(_mlstm_kernel, L=L, t_valid=t_valid, has_state=has_state),
        grid=(nb, nc),
        in_specs=in_specs,
        out_specs=out_specs,
        out_shape=out_shape,
        compiler_params=_cparams(("parallel", "arbitrary")),
        name="mlstm",
    )(*args)
    return h, cs, ns.reshape(nb, M_HEADS, M_DH), ms.reshape(nb, M_HEADS)


def _stack_heads(qt, g):
    t = qt.shape[0]
    z = jnp.zeros((t, A_DH), F32)
    parts = []
    for hh in range(A_HPG):
        hd = g * A_HPG + hh
        qh = qt[:, hd * A_DH:(hd + 1) * A_DH] * (ATT_SCALE * LOG2E)
        parts.append(jnp.concatenate([qh, z], axis=1) if g == 0 else jnp.concatenate([z, qh], axis=1))
    return jnp.concatenate(parts, axis=0).astype(BF16)


def _gate_cols(small, g, br):
    cols = []
    for hh in range(A_HPG):
        c0 = 2 * M_HEADS + (g * A_HPG + hh) * 3 + br
        cols.append(_sigmoid(small[:, c0:c0 + 1]))
    return jnp.concatenate(cols, axis=0)


def _compress(k_ref, v_ref, nseg, wbd_ref, pe_ref, kg0):
    acc_lo = jnp.zeros((nseg, 2 * LANES), F32)
    acc_hi = jnp.zeros((nseg, 2 * LANES), F32)
    for l in range(CMP_STRIDE):
        xl = jnp.concatenate([k_ref[pl.ds(l, nseg, stride=CMP_STRIDE), :],
                              v_ref[pl.ds(l, nseg, stride=CMP_STRIDE), :]], axis=1)
        acc_lo = acc_lo + _bdot(xl + pe_ref[l], wbd_ref[l])
        acc_hi = acc_hi + _bdot(xl + pe_ref[CMP_STRIDE + l], wbd_ref[CMP_STRIDE + l])
    return _compress_finish(acc_lo, acc_hi, nseg, kg0)


def _compress_grouped(x_ref, nseg, wbd_ref, pe_ref, kg0):
    acc_lo = jnp.zeros((nseg, 2 * LANES), F32)
    acc_hi = jnp.zeros((nseg, 2 * LANES), F32)
    pe_lo = jnp.zeros((SUBLANES, 2 * LANES), F32)
    pe_hi = jnp.zeros((SUBLANES, 2 * LANES), F32)
    for l in range(CMP_STRIDE):
        xl = x_ref[l].astype(BF16)
        acc_lo = acc_lo + jnp.dot(xl, wbd_ref[l], preferred_element_type=F32)
        acc_hi = acc_hi + jnp.dot(xl, wbd_ref[CMP_STRIDE + l], preferred_element_type=F32)
        pe_lo = pe_lo + _bdot(jnp.broadcast_to(pe_ref[l], (SUBLANES, 2 * LANES)), wbd_ref[l])
        pe_hi = pe_hi + _bdot(jnp.broadcast_to(pe_ref[CMP_STRIDE + l], (SUBLANES, 2 * LANES)),
                              wbd_ref[CMP_STRIDE + l])
    return _compress_finish(acc_lo + pe_lo[0:1, :], acc_hi + pe_hi[0:1, :], nseg, kg0)


def _compress_finish(acc_lo, acc_hi, nseg, kg0):
    kv = acc_lo + pltpu.roll(acc_hi, nseg - 1, 0)
    kc = kv[:, 0:LANES]
    vc = kv[:, LANES:2 * LANES]
    lane = lax.broadcasted_iota(jnp.int32, (nseg, LANES), 1)
    sq = kc * kc
    ms0 = jnp.sum(jnp.where(lane < A_DH, sq, 0.0), axis=1, keepdims=True) * (1.0 / A_DH)
    ms1 = jnp.sum(jnp.where(lane >= A_DH, sq, 0.0), axis=1, keepdims=True) * (1.0 / A_DH)
    ms = jnp.where(lane < A_DH, ms0, ms1)
    kc = kc * lax.rsqrt(ms + EPS) * kg0
    return kc, vc


def _cmp_branch(qn_g, kc_b, vc_b, tpos_rows, nseg, n_tok):
    s = _bdot_t(qn_g, kc_b)
    nidx = lax.broadcasted_iota(jnp.int32, (1, nseg), 1)
    vis = (nidx * CMP_STRIDE + (CMP_LEN - 1)) <= tpos_rows
    sm = jnp.where(vis, s, NEG_BIG)
    mx = jnp.max(sm, axis=1, keepdims=True)
    e = jnp.where(vis, jnp.exp2(sm - mx), 0.0)
    d = jnp.sum(e, axis=1, keepdims=True)
    p = e / jnp.where(d > 0, d, 1.0)
    o = _bdot(p, vc_b)
    imp = p[0:n_tok]
    for hh in range(1, A_HPG):
        imp = imp + p[hh * n_tok:(hh + 1) * n_tok]
    return o, imp


def _masked_attn_direct(q_g, k_parts, v_parts, allowed_parts, feature_major):
    ss = [jnp.where(al, _bdot(q_g, kk) if fm else _bdot_t(q_g, kk), NEG_BIG)
          for kk, al, fm in zip(k_parts, allowed_parts, feature_major)]
    mx = ss[0].max(axis=1, keepdims=True)
    for s in ss[1:]:
        mx = jnp.maximum(mx, s.max(axis=1, keepdims=True))
    num = None
    den = None
    for s, al, vv, fm in zip(ss, allowed_parts, v_parts, feature_major):
        e = jnp.where(al, jnp.exp2(s - mx), 0.0)
        dd = jnp.sum(e, axis=1, keepdims=True)
        oo = _bdot_t(e, vv) if fm else _bdot(e, vv)
        num = oo if num is None else num + oo
        den = dd if den is None else den + dd
    return num / jnp.where(den > 0, den, 1.0)


def _assemble_heads(o_groups, n_tok):
    pieces = []
    for g in range(A_KV):
        for hh in range(A_HPG):
            pieces.append(o_groups[g][hh * n_tok:(hh + 1) * n_tok, g * A_DH:(g + 1) * A_DH])
    return jnp.concatenate(pieces, axis=1)


def _lane_rep(a, rep):
    return a if rep == 1 else jnp.concatenate([a] * rep, axis=1)


def _nsa_prompt_kernel(q_ref, qr_ref, small_ref, rows_ref, win_ref, wbd_ref, pe_ref, kg0_ref,
                       pool_ref, o_ref,
                       kraw_sc, vraw_sc, kc_sc, vct_sc, sel_sc, m_sc, acc_sc, s_sc, *, T, tq, kc_len):
    qi = pl.program_id(1)
    nseg = T // CMP_STRIDE
    nsb = T // SEL_LEN
    bpc = kc_len // SEL_LEN

    @pl.when(qi == 0)
    def _():
        kraw_sc[...] = rows_ref[:, 0:LANES]
        vraw_sc[...] = rows_ref[:, LANES:2 * LANES]
        kc, vc = _compress(kraw_sc, vraw_sc, nseg, wbd_ref, pe_ref, kg0_ref[...])
        kc_sc[...] = kc
        vct_sc[...] = jnp.transpose(vc)

    t0 = qi * tq
    tpos = t0 + lax.broadcasted_iota(jnp.int32, (1, tq), 1)
    tpos4 = _lane_rep(tpos, A_HPG)
    q = q_ref[...]
    qr = qr_ref[...]
    small_t = jnp.transpose(small_ref[...])
    kc_b = kc_sc[...].astype(BF16)
    vct_b = vct_sc[...].astype(BF16)
    bidx = lax.broadcasted_iota(jnp.int32, (nsb, tq), 0)
    cur = tpos // SEL_LEN
    vis = (lax.broadcasted_iota(jnp.int32, (nseg, 1), 0) * CMP_STRIDE + (CMP_LEN - 1)) <= tpos4
    qr_gs = [_stack_heads(qr, g) for g in range(A_KV)]
    o_cmps = []
    for g in range(A_KV):
        sm = jnp.where(vis, _bdot_t(kc_b, _stack_heads(q, g)), NEG_BIG)
        mx = jnp.max(sm, axis=0, keepdims=True)
        e = jnp.where(vis, jnp.exp2(sm - mx), 0.0)
        d = jnp.sum(e, axis=0, keepdims=True)
        p = e / jnp.where(d > 0, d, 1.0)
        o_cmps.append(jnp.dot(vct_b, p.astype(BF16), preferred_element_type=F32))
        imp = p[:, 0:tq]
        for hh in range(1, A_HPG):
            imp = imp + p[:, hh * tq:(hh + 1) * tq]
        ih, il = _split(imp)
        imp_t = (jnp.dot(pool_ref[...], ih, preferred_element_type=F32)
                 + jnp.dot(pool_ref[...], il, preferred_element_type=F32))[0:nsb]
        val = jnp.where(bidx < cur, imp_t, -1.0)
        rank = jnp.zeros((nsb, tq), F32)
        for bp in range(nsb):
            vb = val[bp:bp + 1, :]
            rank = rank + jnp.where(vb > val, 1.0, jnp.where((vb == val) & (bidx > bp), 1.0, 0.0))
        sel_sc[g] = jnp.where(((rank < (N_SEL - 1)) & (bidx < cur)) | (bidx == cur), 1.0, 0.0)

    m_sc[...] = jnp.full(m_sc.shape, M_INIT, F32)
    acc_sc[...] = jnp.zeros(acc_sc.shape, F32)

    def with_ones_row(vt_, g):
        vb = vt_.astype(BF16)
        r0, pad = (1 - g) * A_DH, 2 * SUBLANES
        ones = jnp.ones((pad, vb.shape[1]), BF16)
        return jnp.concatenate(([vb[0:r0]] if r0 else []) + [ones, vb[r0 + pad:]], axis=0)

    def sel_body(c, carry):
        k0 = pl.multiple_of(c * kc_len, kc_len)
        kb = rows_ref[pl.ds(k0, kc_len), 2 * LANES:3 * LANES].astype(BF16)
        vt = jnp.transpose(rows_ref[pl.ds(k0, kc_len), 3 * LANES:4 * LANES])
        causal = (k0 + lax.broadcasted_iota(jnp.int32, (kc_len, 1), 0)) <= tpos
        for g in range(A_KV):
            s_sc[g, 0:kc_len, :] = _bdot_t(kb, qr_gs[g])
        for g in range(A_KV):
            selc = sel_sc[g, pl.ds(pl.multiple_of(c * bpc, bpc), bpc), :]
            selx = jnp.concatenate([jnp.broadcast_to(selc[j:j + 1, :], (SEL_LEN, tq)) for j in range(bpc)], axis=0)
            bias = jnp.where(causal & (selx > 0.5), 0.0, NEG_BIG)
            sm = s_sc[g, 0:kc_len, :] + _lane_rep(bias, A_HPG)
            m_prev = m_sc[g]
            m_new = jnp.maximum(m_prev, jnp.max(sm, axis=0, keepdims=True))
            alpha = jnp.exp2(m_prev - m_new)
            p = jnp.exp2(sm - m_new)
            acc_sc[g] = alpha * acc_sc[g] + jnp.dot(with_ones_row(vt, g), p.astype(BF16),
                                                    preferred_element_type=F32)
            m_sc[g] = m_new
        return carry

    lax.fori_loop(0, (t0 + tq + kc_len - 1) // kc_len, sel_body, 0)

    wk = min(WINDOW + tq, T)
    w0 = pl.multiple_of(jnp.clip(t0 + tq - wk, 0, T - wk), tq)
    kw = win_ref[pl.ds(w0, wk), 0:LANES].astype(BF16)
    vwt = jnp.transpose(win_ref[pl.ds(w0, wk), LANES:2 * LANES])
    wdiff = tpos - (w0 + lax.broadcasted_iota(jnp.int32, (wk, 1), 0))
    wbias = _lane_rep(jnp.where((wdiff >= 0) & (wdiff < WINDOW), 0.0, NEG_BIG), A_HPG)

    def gate_row(g, br):
        cols = [2 * M_HEADS + (g * A_HPG + hh) * 3 + br for hh in range(A_HPG)]
        return jnp.concatenate([_sigmoid(small_t[c0:c0 + 1, :]) for c0 in cols], axis=1)

    for g in range(A_KV):
        s_sc[g, 0:wk, :] = _bdot_t(kw, qr_gs[g])
    o_ts = []
    for g in range(A_KV):
        den = (1 - g) * A_DH
        acc = acc_sc[g]
        l = acc[den:den + 1, :]
        o_sel = acc / jnp.where(l > 0, l, 1.0)
        sw = s_sc[g, 0:wk, :] + wbias
        pw = jnp.exp2(sw - jnp.max(sw, axis=0, keepdims=True))
        ow = jnp.dot(with_ones_row(vwt, g), pw.astype(BF16), preferred_element_type=F32)
        o_win = ow / ow[den:den + 1, :]
        o_ts.append(gate_row(g, 0) * o_cmps[g] + gate_row(g, 1) * o_sel + gate_row(g, 2) * o_win)
    for j in range(A_HEADS // 2):
        g, h0 = j // (A_HPG // 2), 2 * (j % (A_HPG // 2))
        og = o_ts[g][g * A_DH:(g + 1) * A_DH, :]
        pair = jnp.concatenate([og[:, h0 * tq:(h0 + 1) * tq], og[:, (h0 + 1) * tq:(h0 + 2) * tq]], axis=0)
        o_ref[:, j * LANES:(j + 1) * LANES] = jnp.transpose(pair)


def _nsa_prompt(q, qr, small, rows, win, wbd, pe, kg0, nb, T):
    tq = 128
    kc_len = _pick_tile(T, 512)
    nq = T // tq
    nseg = T // CMP_STRIDE
    nsb = T // SEL_LEN
    nsb_p = -(-nsb // SUBLANES) * SUBLANES
    pool = (jnp.arange(nsb_p)[:, None] == jnp.arange(nseg)[None, :] // (SEL_LEN // CMP_STRIDE)).astype(BF16)
    tile = lambda b, i: (b * nq + i, 0)
    per_b = lambda b, i: (b, 0)
    c2 = lambda b, i: (0, 0)
    c3 = lambda b, i: (0, 0, 0)
    c4 = A_HPG * tq
    return pl.pallas_call(
        functools.partial(_nsa_prompt_kernel, T=T, tq=tq, kc_len=kc_len),
        grid=(nb, nq),
        in_specs=[pl.BlockSpec((tq, A_WIDTH), tile), pl.BlockSpec((tq, A_WIDTH), tile),
                  pl.BlockSpec((tq, LANES), tile),
                  pl.BlockSpec((T, 4 * LANES), per_b), pl.BlockSpec((T, 2 * LANES), per_b),
                  pl.BlockSpec(wbd.shape, c3), pl.BlockSpec(pe.shape, c3), pl.BlockSpec(kg0.shape, c2),
                  pl.BlockSpec(pool.shape, c2)],
        out_specs=pl.BlockSpec((tq, A_WIDTH), tile),
        out_shape=jax.ShapeDtypeStruct((nb * T, A_WIDTH), F32),
        scratch_shapes=[pltpu.VMEM((T, LANES), F32), pltpu.VMEM((T, LANES), F32),
                        pltpu.VMEM((nseg, LANES), F32), pltpu.VMEM((LANES, nseg), F32),
                        pltpu.VMEM((A_KV, nsb, tq), F32),
                        pltpu.VMEM((A_KV, 1, c4), F32),
                        pltpu.VMEM((A_KV, LANES, c4), F32),
                        pltpu.VMEM((A_KV, max(kc_len, min(WINDOW + tq, T)), c4), F32)],
        compiler_params=_cparams(("parallel", "arbitrary")),
        name="nsa_prompt",
    )(q, qr, small, rows, win, wbd, pe, kg0, pool)


def _nsa_sample_kernel(pt_ref, cache_ref, q_ref, qr_ref, small_ref, rows_ref, winnew_ref, winbuf_ref,
                       wbd_ref, pe_ref, kg0_ref, pool_ref, expand_ref,
                       o_ref, winout_ref,
                       cmp_buf, sel_buf, xperm_sc, sems, *, n_pages, past_len, t_valid):
    b = pl.program_id(0)
    nb = pl.num_programs(0)
    tp = SAMPLE_PAD_T
    nseg = past_len // CMP_STRIDE
    nsb = past_len // SEL_LEN
    wbuf = winbuf_ref.shape[0]

    def page_copies(bb, p, phase):
        page = pt_ref[bb * n_pages + p]
        dst_lanes = pl.ds(pl.multiple_of(p * PAGE_SIZE, PAGE_SIZE), PAGE_SIZE)
        if phase == 0:
            return [pltpu.make_async_copy(cache_ref.at[page, pl.ds(0, 2 * LANES), :],
                                          cmp_buf.at[:, dst_lanes], sems.at[0])]
        return [pltpu.make_async_copy(cache_ref.at[page, pl.ds(2 * LANES, 2 * LANES), :],
                                      sel_buf.at[:, dst_lanes], sems.at[1])]

    def start_all(bb, phase):
        def body(p, c):
            for cp in page_copies(bb, p, phase):
                cp.start()
            return c
        lax.fori_loop(0, n_pages, body, 0)

    def wait_all(bb, phase):
        def body(p, c):
            for cp in page_copies(bb, p, phase):
                cp.wait()
            return c
        lax.fori_loop(0, n_pages, body, 0)

    @pl.when(b == 0)
    def _():
        start_all(b, 0)

    start_all(b, 1)
    wait_all(b, 0)

    seg_pp = PAGE_SIZE // CMP_STRIDE
    pr = lax.broadcasted_iota(jnp.int32, (PAGE_SIZE, PAGE_SIZE), 0)
    pc = lax.broadcasted_iota(jnp.int32, (PAGE_SIZE, PAGE_SIZE), 1)
    perm = jnp.where(pc == CMP_STRIDE * (pr % seg_pp) + pr // seg_pp, 1.0, 0.0).astype(BF16)
    for p in range(n_pages):
        xp = _bdot_t(perm, cmp_buf[:, p * PAGE_SIZE:(p + 1) * PAGE_SIZE])
        for l in range(CMP_STRIDE):
            xperm_sc[l, p * seg_pp:(p + 1) * seg_pp, :] = xp[l * seg_pp:(l + 1) * seg_pp, :]
    kc, vc = _compress_grouped(xperm_sc, nseg, wbd_ref, pe_ref, kg0_ref[...])
    kc_b = kc.astype(BF16)
    vc_b = vc.astype(BF16)
    q = q_ref[...]
    qr = qr_ref[...]
    small = small_ref[...]
    tpos_col = past_len + lax.broadcasted_iota(jnp.int32, (tp, 1), 0)
    tpos_rows = jnp.concatenate([tpos_col] * A_HPG, axis=0)
    bp_idx = lax.broadcasted_iota(jnp.int32, (nsb, nsb), 0)
    b_idx = lax.broadcasted_iota(jnp.int32, (nsb, nsb), 1)
    o_cmps = []
    sels = []
    for g in range(A_KV):
        qn_g = _stack_heads(q, g)
        o_cmp, imp = _cmp_branch(qn_g, kc_b, vc_b, tpos_rows, nseg, tp)
        o_cmps.append(o_cmp)
        imp_sel = _dot2_exact_rhs(imp, pool_ref[...])
        imp_pad = jnp.concatenate([imp_sel, jnp.zeros((nsb - tp, nsb), F32)], axis=0)
        imp_t = jnp.transpose(imp_pad)
        rows_sel = []
        for t in range(tp):
            if t < t_valid:
                row_t = imp_sel[t:t + 1, :]
                col_t = imp_t[:, t:t + 1]
                ahead = jnp.where(col_t > row_t, 1.0, jnp.where((col_t == row_t) & (bp_idx < b_idx), 1.0, 0.0))
                rank = jnp.sum(ahead, axis=0, keepdims=True)
                rows_sel.append(jnp.where(rank < (N_SEL - 1), 1.0, 0.0))
            else:
                rows_sel.append(jnp.zeros((1, nsb), F32))
        sels.append(jnp.concatenate(rows_sel, axis=0).astype(BF16))

    @pl.when(b + 1 < nb)
    def _():
        start_all(b + 1, 0)

    wait_all(b, 1)

    new_idx = lax.broadcasted_iota(jnp.int32, (tp, tp), 1)
    tok_idx = lax.broadcasted_iota(jnp.int32, (tp, tp), 0)
    new_ok = jnp.concatenate([jnp.where(new_idx <= tok_idx, 1.0, 0.0)] * A_HPG, axis=0) > 0.5
    wpos = past_len - wbuf + lax.broadcasted_iota(jnp.int32, (1, wbuf), 1)
    wdiff = tpos_col - wpos
    win_ok = jnp.concatenate([jnp.where((wdiff >= 0) & (wdiff < WINDOW), 1.0, 0.0)] * A_HPG, axis=0) > 0.5
    k_past = sel_buf[0:LANES, :].astype(BF16)
    v_past = sel_buf[LANES:2 * LANES, :].astype(BF16)
    k_new = rows_ref[:, 2 * LANES:3 * LANES]
    v_new = rows_ref[:, 3 * LANES:4 * LANES]
    kw_past = winbuf_ref[:, 0:LANES]
    vw_past = winbuf_ref[:, LANES:2 * LANES]
    kw_new = winnew_ref[:, 0:LANES]
    vw_new = winnew_ref[:, LANES:2 * LANES]
    o_groups = []
    for g in range(A_KV):
        qr_g = _stack_heads(qr, g)
        mk = jnp.dot(sels[g], expand_ref[...], preferred_element_type=F32)
        past_ok = jnp.concatenate([mk] * A_HPG, axis=0) > 0.5
        o_sel = _masked_attn_direct(qr_g, [k_past, k_new], [v_past, v_new], [past_ok, new_ok], [True, False])
        o_win = _masked_attn_direct(qr_g, [kw_past, kw_new], [vw_past, vw_new], [win_ok, new_ok], [False, False])
        o_groups.append(_gate_cols(small, g, 0) * o_cmps[g] + _gate_cols(small, g, 1) * o_sel
                        + _gate_cols(small, g, 2) * o_win)
    o_ref[...] = _assemble_heads(o_groups, tp)

    wb = winbuf_ref[...]
    rolled = pltpu.roll(wb, wbuf - t_valid, 0)
    newr = pltpu.roll(winnew_ref[...], tp - t_valid, 0)
    sub = lax.broadcasted_iota(jnp.int32, (tp, 2 * LANES), 0)
    winout_ref[0:wbuf - tp, :] = rolled[0:wbuf - tp, :]
    winout_ref[wbuf - tp:wbuf, :] = jnp.where(sub < tp - t_valid, rolled[wbuf - tp:wbuf, :], newr)


def _nsa_sample(page_table, cache, q, qr, small, rows, winnew, winbuf, wbd, pe, kg0, t_valid):
    nb, n_pages = page_table.shape
    past_len = n_pages * PAGE_SIZE
    nseg = past_len // CMP_STRIDE
    nsb = past_len // SEL_LEN
    tp = SAMPLE_PAD_T
    wbuf = winbuf.shape[1]
    pool = (jnp.arange(nseg)[:, None] // (SEL_LEN // CMP_STRIDE) == jnp.arange(nsb)[None, :]).astype(BF16)
    expand = (jnp.arange(nsb)[:, None] == jnp.arange(past_len)[None, :] // SEL_LEN).astype(BF16)
    tile = lambda b, pt: (b, 0)
    c2 = lambda b, pt: (0, 0)
    c3 = lambda b, pt: (0, 0, 0)
    gs = pltpu.PrefetchScalarGridSpec(
        num_scalar_prefetch=1,
        grid=(nb,),
        in_specs=[pl.BlockSpec(memory_space=pl.ANY),
                  pl.BlockSpec((tp, A_WIDTH), tile), pl.BlockSpec((tp, A_WIDTH), tile),
                  pl.BlockSpec((tp, LANES), tile), pl.BlockSpec((tp, 4 * LANES), tile),
                  pl.BlockSpec((tp, 2 * LANES), tile),
                  pl.BlockSpec((None, wbuf, 2 * LANES), lambda b, pt: (b, 0, 0)),
                  pl.BlockSpec(wbd.shape, c3), pl.BlockSpec(pe.shape, c3), pl.BlockSpec(kg0.shape, c2),
                  pl.BlockSpec(pool.shape, c2), pl.BlockSpec(expand.shape, c2)],
        out_specs=[pl.BlockSpec((tp, A_WIDTH), tile),
                   pl.BlockSpec((None, wbuf, 2 * LANES), lambda b, pt: (b, 0, 0))],
        scratch_shapes=[pltpu.VMEM((2 * LANES, past_len), F32), pltpu.VMEM((2 * LANES, past_len), F32),
                        pltpu.VMEM((CMP_STRIDE, past_len // CMP_STRIDE, 2 * LANES), F32),
                        pltpu.SemaphoreType.DMA((2,))],
    )
    return pl.pallas_call(
        functools.partial(_nsa_sample_kernel, n_pages=n_pages, past_len=past_len, t_valid=t_valid),
        grid_spec=gs,
        out_shape=[jax.ShapeDtypeStruct((nb * tp, A_WIDTH), F32),
                   jax.ShapeDtypeStruct((nb, wbuf, 2 * LANES), F32)],
        compiler_params=_cparams(("arbitrary",)),
        name="nsa_sample",
    )(page_table.reshape(-1), cache, q, qr, small, rows, winnew, winbuf, wbd, pe, kg0, pool, expand)


MOE_TM = 256
SEG_ALIGN = 8
SEG_BITS = (256, 128, 64, 32, 16, 8)
MOE_RL = -(-(MOE_TM * TOP_K + N_EXPERTS * (SEG_ALIGN - 1)) // LANES) * LANES


def _pack_halves(x, bf16_exact=False):
    w = x.shape[1] // 2
    bits = lax.bitcast_convert_type(x if bf16_exact else x.astype(BF16).astype(F32), jnp.uint32)
    return (bits[:, :w] & jnp.uint32(0xFFFF0000)) | (bits[:, w:] >> 16)


def _unpack_halves(u):
    hi = lax.bitcast_convert_type(u & jnp.uint32(0xFFFF0000), F32).astype(BF16)
    lo = lax.bitcast_convert_type(u << 16, F32).astype(BF16)
    return hi, lo


def _route_and_sort(h2, wrt_ref, brt_ref, xsl_ref, info_ref, cnt_ref, tm, t_mod, t_valid, m_valid):
    ne = N_EXPERTS
    h2b = h2.astype(BF16)
    h2l = (h2 - h2b.astype(F32)).astype(BF16)
    wh, wl = _split(wrt_ref[...])
    lt = _bdot_t(wh, h2b) + _bdot_t(wl, h2b) + _bdot_t(wh, h2l) + brt_ref[...]
    eidx = lax.broadcasted_iota(jnp.int32, (ne, tm), 0)
    rank = jnp.zeros((ne, tm), F32)
    for ep in range(ne):
        v = lt[ep:ep + 1, :]
        rank = rank + jnp.where(v > lt, 1.0, jnp.where((v == lt) & (eidx > ep), 1.0, 0.0))
    sel = rank < TOP_K
    if t_mod is not None:
        tok = pl.program_id(0) * tm + lax.broadcasted_iota(jnp.int32, (1, tm), 1)
        sel = sel & ((tok % t_mod) < t_valid) & (tok < m_valid)
    mx = jnp.max(jnp.where(sel, lt, NEG_BIG), axis=0, keepdims=True)
    ex = jnp.where(sel, jnp.exp(lt - mx), 0.0)
    den = jnp.sum(ex, axis=0, keepdims=True)
    gate = ex / jnp.where(den > 0, den, 1.0)
    self_ = jnp.where(sel, 1.0, 0.0)
    selb = self_.astype(BF16)
    er = lax.broadcasted_iota(jnp.int32, (ne, ne), 0)
    ec = lax.broadcasted_iota(jnp.int32, (ne, ne), 1)
    c = jnp.dot(jnp.where(ec <= er, 1.0, 0.0).astype(BF16), selb, preferred_element_type=F32)
    tr = lax.broadcasted_iota(jnp.int32, (tm, tm), 0)
    tc = lax.broadcasted_iota(jnp.int32, (tm, tm), 1)
    rk = jnp.dot(selb, jnp.where(tr < tc, 1.0, 0.0).astype(BF16), preferred_element_type=F32)
    cnt = jnp.sum(self_, axis=1, keepdims=True)
    cnt_al = jnp.floor((cnt + (SEG_ALIGN - 1)) * (1.0 / SEG_ALIGN)) * SEG_ALIGN
    cnt_b = jnp.broadcast_to(cnt_al, (ne, LANES))
    cnt_ref[...] = cnt_b
    off = jnp.dot(jnp.where(ec < er, 1.0, 0.0).astype(BF16), cnt_b.astype(BF16), preferred_element_type=F32)
    rowidx = off[:, 0:1] + rk
    rows_k, gates_k, exps_k = [], [], []
    for k in range(1, TOP_K + 1):
        mk = sel & (c == k)
        has = jnp.sum(jnp.where(mk, 1.0, 0.0), axis=0, keepdims=True)
        rows_k.append(jnp.sum(jnp.where(mk, rowidx, 0.0), axis=0, keepdims=True) + has - 1.0)
        gates_k.append(jnp.sum(jnp.where(mk, gate, 0.0), axis=0, keepdims=True))
        exps_k.append(jnp.sum(jnp.where(mk, eidx.astype(F32), 0.0), axis=0, keepdims=True))
    info_ref[...] = jnp.concatenate(rows_k + gates_k + exps_k + [jnp.zeros((4, tm), F32)], axis=0)
    ridx = lax.broadcasted_iota(jnp.int32, (MOE_RL, tm), 0).astype(F32)
    perm = jnp.zeros((MOE_RL, tm), F32)
    for k in range(TOP_K):
        perm = perm + jnp.where(ridx == rows_k[k], 1.0, 0.0)
    xs = jnp.dot(perm.astype(BF16), h2b, preferred_element_type=F32)
    xsl_ref[...] = _pack_halves(xs, bf16_exact=True)


def _mixout_kernel(x_ref, hm_ref, on_ref, mod_ref, gmix_ref, gffn_ref,
                   wog_ref, bog_ref, wum_ref, wua_ref, wout_ref, wrt_ref, brt_ref,
                   x1_ref, xsl_ref, info_ref, cnt_ref, *, tm, t_mod, t_valid, m_valid, n_real):
    if n_real is not None:
        @pl.when(pl.program_id(0) >= n_real)
        def _():
            xsl_ref[...] = jnp.zeros(xsl_ref.shape, jnp.uint32)
            info_ref[...] = jnp.zeros(info_ref.shape, F32)
            cnt_ref[...] = jnp.zeros(cnt_ref.shape, F32)

        @pl.when(pl.program_id(0) < n_real)
        def _():
            _mixout_body(x_ref, hm_ref, on_ref, mod_ref, gmix_ref, gffn_ref, wog_ref, bog_ref, wum_ref,
                         wua_ref, wout_ref, wrt_ref, brt_ref, x1_ref, xsl_ref, info_ref, cnt_ref,
                         tm, t_mod, t_valid, m_valid)
    else:
        _mixout_body(x_ref, hm_ref, on_ref, mod_ref, gmix_ref, gffn_ref, wog_ref, bog_ref, wum_ref,
                     wua_ref, wout_ref, wrt_ref, brt_ref, x1_ref, xsl_ref, info_ref, cnt_ref,
                     tm, t_mod, t_valid, m_valid)


def _mixout_body(x_ref, hm_ref, on_ref, mod_ref, gmix_ref, gffn_ref,
                 wog_ref, bog_ref, wum_ref, wua_ref, wout_ref, wrt_ref, brt_ref,
                 x1_ref, xsl_ref, info_ref, cnt_ref, tm, t_mod, t_valid, m_valid):
    d = D_MODEL
    x = x_ref[...]
    sh1, sc1, gt1 = mod_ref[:, 0:d], mod_ref[:, d:2 * d], mod_ref[:, 2 * d:3 * d]
    sh2, sc2 = mod_ref[:, 3 * d:4 * d], mod_ref[:, 4 * d:5 * d]
    h = _rmsnorm_rows(x, gmix_ref[...]) * (1.0 + sc1) + sh1
    hb = h.astype(BF16)
    mo = jnp.dot(hb, wog_ref[:, 0:M_WIDTH], preferred_element_type=F32) + bog_ref[:, 0:M_WIDTH]
    ym = _bdot(_sigmoid(mo) * hm_ref[...], wum_ref[...])
    ya = _bdot(on_ref[...], wua_ref[...])
    ga = jnp.dot(hb, wog_ref[:, M_WIDTH:M_WIDTH + d], preferred_element_type=F32) + bog_ref[:, M_WIDTH:M_WIDTH + d]
    u = _sigmoid(ga) * ym
    gb = (jnp.dot(hb, wog_ref[:, M_WIDTH + d:M_WIDTH + 2 * d], preferred_element_type=F32)
          + bog_ref[:, M_WIDTH + d:M_WIDTH + 2 * d])
    u = u + _sigmoid(gb) * ya
    x1 = x + gt1 * _bdot(u, wout_ref[...])
    x1_ref[...] = x1
    h2 = _rmsnorm_rows(x1, gffn_ref[...]) * (1.0 + sc2) + sh2
    _route_and_sort(h2, wrt_ref, brt_ref, xsl_ref, info_ref, cnt_ref, tm, t_mod, t_valid, m_valid)


def _mixout_with_shared(*refs, n_shared, **kw):
    n_in = 13
    _mixout_kernel(*refs[:n_in], *refs[n_in + n_shared:], **kw)


def _mixout(x2, hm, on, mod3, gmix, gffn, wts, tiles_per_mod, nt_total, tile0=0, shared=None,
            t_mod=None, t_valid=None, m_valid=None):
    m = x2.shape[0]
    tm = MOE_TM
    nt = m // tm
    (wog, bog, wum, wua, wout, wr, br) = wts
    r = mod3.shape[1]
    n_extra = nt_total - tile0 - nt if shared is None else 0
    row = lambda i: (jnp.minimum(i, nt - 1), 0)
    const = lambda i: (0, 0)
    in_specs = [pl.BlockSpec((tm, D_MODEL), row), pl.BlockSpec((tm, M_WIDTH), row),
                pl.BlockSpec((tm, A_WIDTH), row),
                pl.BlockSpec((None, r, 6 * D_MODEL), lambda i: (jnp.minimum(i, nt - 1) // tiles_per_mod, 0, 0)),
                pl.BlockSpec((1, D_MODEL), const), pl.BlockSpec((1, D_MODEL), const),
                pl.BlockSpec(wog.shape, const), pl.BlockSpec(bog.shape, const),
                pl.BlockSpec(wum.shape, const), pl.BlockSpec(wua.shape, const),
                pl.BlockSpec(wout.shape, const), pl.BlockSpec(wr.shape, const),
                pl.BlockSpec(br.shape, const)]
    args = [x2, hm, on, mod3, gmix, gffn, wog, bog, wum, wua, wout, wr, br]
    kw = dict(tm=tm, t_mod=t_mod, t_valid=t_valid, m_valid=m_valid, n_real=nt if n_extra else None)
    body = functools.partial(_mixout_kernel, **kw)
    aliases = {}
    if shared is not None:
        in_specs += [pl.BlockSpec(memory_space=pl.ANY)] * len(shared)
        aliases = {len(args) + j: 1 + j for j in range(len(shared))}
        args += list(shared)
        body = functools.partial(_mixout_with_shared, n_shared=len(shared), **kw)
    return pl.pallas_call(
        body,
        grid=(nt + n_extra,),
        in_specs=in_specs,
        out_specs=[pl.BlockSpec((tm, D_MODEL), row),
                   pl.BlockSpec((MOE_RL, D_MODEL // 2), lambda i: (tile0 + i, 0)),
                   pl.BlockSpec((16, tm), lambda i: (0, tile0 + i)),
                   pl.BlockSpec((None, N_EXPERTS, LANES), lambda i: (tile0 + i, 0, 0))],
        out_shape=[jax.ShapeDtypeStruct((m, D_MODEL), F32),
                   jax.ShapeDtypeStruct((nt_total * MOE_RL, D_MODEL // 2), jnp.uint32),
                   jax.ShapeDtypeStruct((16, nt_total * tm), F32),
                   jax.ShapeDtypeStruct((nt_total, N_EXPERTS, LANES), F32)],
        input_output_aliases=aliases,
        compiler_params=_cparams(("arbitrary" if n_extra else "parallel",)),
        name="mixout",
    )(*args)


MOE_BM = 256
MOE_CH = 512


def _moe_kernel(be_ref, na_ref, grp_ref,
                xsl_ref, wgu_ref, bgu_ref, wdn_ref, bdn_ref, ysl_ref,
                wgu_bf, wdn_bf, xbuf, ybuf, sem_in, sem_out, *, trash_row0):
    i = pl.program_id(0)
    na = na_ref[0]
    e = be_ref[i]
    prev = be_ref[jnp.maximum(i - 1, 0)]
    n_grp = MOE_BM // SEG_ALIGN

    def group_copies(blk, inbound, slot=None):
        slot = blk % 2 if slot is None else slot
        cps = []
        for r in range(n_grp):
            v = grp_ref[blk * n_grp + r]
            vm_rows = pl.ds(r * SEG_ALIGN, SEG_ALIGN)
            if inbound:
                row = pl.multiple_of(jnp.where(v >= 0, v, trash_row0 + 2 * MOE_BM), SEG_ALIGN)
                cps.append(pltpu.make_async_copy(xsl_ref.at[pl.ds(row, SEG_ALIGN), :],
                                                 xbuf.at[slot, vm_rows, :], sem_in.at[slot]))
            else:
                spare = trash_row0 + slot * MOE_BM + r * SEG_ALIGN
                row = pl.multiple_of(jnp.where(v >= 0, v, spare), SEG_ALIGN)
                cps.append(pltpu.make_async_copy(ybuf.at[slot, vm_rows, :],
                                                 ysl_ref.at[pl.ds(row, SEG_ALIGN), :], sem_out.at[slot]))
        return cps

    def start_gather(blk):
        for cp in group_copies(blk, True):
            cp.start()

    def start_scatter(blk):
        for cp in group_copies(blk, False):
            cp.start()

    def wait_rows(blk, sem, inbound):
        slot = blk % 2
        if inbound:
            pltpu.make_async_copy(xsl_ref.at[pl.ds(0, MOE_BM), :], xbuf.at[slot], sem.at[slot]).wait()
        else:
            pltpu.make_async_copy(ybuf.at[slot], ysl_ref.at[pl.ds(0, MOE_BM), :], sem.at[slot]).wait()

    @pl.when(i == 0)
    def _():
        start_gather(i)

    @pl.when(i + 1 < na)
    def _():
        start_gather(i + 1)

    @pl.when((i < na) & ((i == 0) | (e != prev)))
    def _():
        for j in range(2 * D_EXPERT // MOE_CH):
            wgu_bf[:, j * MOE_CH:(j + 1) * MOE_CH] = wgu_ref[:, j * MOE_CH:(j + 1) * MOE_CH].astype(BF16)
        for j in range(D_EXPERT // MOE_CH):
            wdn_bf[j * MOE_CH:(j + 1) * MOE_CH, :] = wdn_ref[j * MOE_CH:(j + 1) * MOE_CH, :].astype(BF16)

    @pl.when(i < na)
    def _():
        slot = i % 2
        wait_rows(i, sem_in, True)

        @pl.when(i >= 2)
        def _():
            wait_rows(i - 2, sem_out, False)

        half = D_MODEL // 2
        xh, xl = _unpack_halves(xbuf[slot])

        def xdot(c0, c1):
            return (jnp.dot(xh, wgu_bf[0:half, c0:c1], preferred_element_type=F32)
                    + jnp.dot(xl, wgu_bf[half:D_MODEL, c0:c1], preferred_element_type=F32))

        acc = jnp.zeros((MOE_BM, D_MODEL), F32) + bdn_ref[...]
        for j in range(D_EXPERT // MOE_CH):
            lo, hi = j * MOE_CH, (j + 1) * MOE_CH
            gj = xdot(lo, hi) + bgu_ref[:, lo:hi]
            uj = xdot(D_EXPERT + lo, D_EXPERT + hi) + bgu_ref[:, D_EXPERT + lo:D_EXPERT + hi]
            gj = jnp.minimum(gj, SWIGLU_LIMIT)
            uj = jnp.clip(uj, -SWIGLU_LIMIT, SWIGLU_LIMIT)
            act = gj * _sigmoid(SWIGLU_ALPHA * gj) * (uj + 1.0)
            acc = acc + jnp.dot(act.astype(BF16), wdn_bf[lo:hi, :], preferred_element_type=F32)
        ybuf[slot] = _pack_halves(acc)
        start_scatter(i)

        @pl.when(i == na - 1)
        def _():
            @pl.when(i >= 1)
            def _():
                wait_rows(i - 1, sem_out, False)
            wait_rows(i, sem_out, False)


def _moe_experts(plan, xsl, w_gu, b_gu, w_dn, b_dn):
    block_e, n_active, grp_rows = plan
    nblk = block_e.shape[0]
    spare_row0 = xsl.shape[0] - MOE_RL
    assert MOE_RL >= 2 * MOE_BM
    wmap = lambda i, be, *_: (be[i], 0, 0)
    anyspec = pl.BlockSpec(memory_space=pl.ANY)
    gs = pltpu.PrefetchScalarGridSpec(
        num_scalar_prefetch=3,
        grid=(nblk,),
        in_specs=[anyspec,
                  pl.BlockSpec((None, D_MODEL, 2 * D_EXPERT), wmap),
                  pl.BlockSpec((None, 1, 2 * D_EXPERT), wmap),
                  pl.BlockSpec((None, D_EXPERT, D_MODEL), wmap),
                  pl.BlockSpec((None, 1, D_MODEL), wmap)],
        out_specs=anyspec,
        scratch_shapes=[pltpu.VMEM((D_MODEL, 2 * D_EXPERT), BF16), pltpu.VMEM((D_EXPERT, D_MODEL), BF16),
                        pltpu.VMEM((2, MOE_BM, D_MODEL // 2), jnp.uint32),
                        pltpu.VMEM((2, MOE_BM, D_MODEL // 2), jnp.uint32),
                        pltpu.SemaphoreType.DMA((2,)), pltpu.SemaphoreType.DMA((2,))],
    )
    return pl.pallas_call(
        functools.partial(_moe_kernel, trash_row0=spare_row0),
        grid_spec=gs,
        out_shape=jax.ShapeDtypeStruct(xsl.shape, jnp.uint32),
        input_output_aliases={3: 0},
        compiler_params=_cparams(("arbitrary",)),
        name="moe_experts",
    )(*plan, xsl, w_gu, b_gu.reshape(N_EXPERTS, 1, -1), w_dn, b_dn.reshape(N_EXPERTS, 1, -1))


def _combine_kernel(ysl_ref, info_ref, x1_ref, mod_ref, y_ref, *, tm):
    info = info_ref[...]
    info_t = jnp.transpose(jnp.concatenate([info, jnp.zeros((LANES - info.shape[0], tm), F32)], axis=0))
    ridx = lax.broadcasted_iota(jnp.int32, (tm, MOE_RL), 1).astype(F32)
    pg = jnp.zeros((tm, MOE_RL), F32)
    for k in range(TOP_K):
        pg = pg + jnp.where(ridx == info_t[:, k:k + 1], info_t[:, TOP_K + k:TOP_K + k + 1], 0.0)
    pgb = pg.astype(BF16)
    yh, yl = _unpack_halves(ysl_ref[...])
    half = D_MODEL // 2
    gt2 = mod_ref[:, 5 * D_MODEL:6 * D_MODEL]
    for c, yy in ((0, yh), (1, yl)):
        moe = jnp.dot(pgb, yy, preferred_element_type=F32)
        y_ref[:, c * half:(c + 1) * half] = (x1_ref[:, c * half:(c + 1) * half]
                                             + gt2[:, c * half:(c + 1) * half] * moe)


def _combine(ysl, info, x1, mod3, tiles_per_mod, tile0=0):
    m = x1.shape[0]
    tm = MOE_TM
    r = mod3.shape[1]
    return pl.pallas_call(
        functools.partial(_combine_kernel, tm=tm),
        grid=(m // tm,),
        in_specs=[pl.BlockSpec((MOE_RL, D_MODEL // 2), lambda i: (tile0 + i, 0)),
                  pl.BlockSpec((16, tm), lambda i: (0, tile0 + i)),
                  pl.BlockSpec((tm, D_MODEL), lambda i: (i, 0)),
                  pl.BlockSpec((None, r, 6 * D_MODEL), lambda i: (i // tiles_per_mod, 0, 0))],
        out_specs=pl.BlockSpec((tm, D_MODEL), lambda i: (i, 0)),
        out_shape=jax.ShapeDtypeStruct((m, D_MODEL), F32),
        compiler_params=_cparams(("parallel",)),
        name="moe_combine",
    )(ysl, info, x1, mod3)


def _moe_plan(cnt):
    cnt = cnt.astype(jnp.int32)
    nt = cnt.shape[0]
    so = jnp.cumsum(cnt, axis=1) - cnt + (jnp.arange(nt) * MOE_RL)[:, None]
    ce = jnp.cumsum(cnt, axis=0)
    cs = ce - cnt
    tot = ce[-1]
    nblk_e = (tot + MOE_BM - 1) // MOE_BM
    blk_end = jnp.cumsum(nblk_e)
    max_rows = nt * MOE_TM * TOP_K + nt * N_EXPERTS * (SEG_ALIGN - 1)
    n_blocks = -(-max_rows // MOE_BM) + N_EXPERTS
    bidx = jnp.arange(n_blocks)
    block_e = jnp.minimum(jnp.sum(blk_end[None, :] <= bidx[:, None], axis=1), N_EXPERTS - 1).astype(jnp.int32)
    is_e = (jnp.arange(N_EXPERTS)[:, None] == block_e[None, :]).astype(jnp.int32)
    per_block = lambda a: jnp.sum(a[..., :, None] * is_e, axis=-2)
    block_r0 = (bidx - per_block(blk_end - nblk_e)) * MOE_BM
    x = block_r0[:, None] + jnp.arange(MOE_BM // SEG_ALIGN)[None, :] * SEG_ALIGN
    ce_b = per_block(ce)[:, :, None]
    cs_b = per_block(cs)[:, :, None]
    inside = (cs_b <= x[None]) & (x[None] < ce_b)
    grp = x + jnp.sum(jnp.where(inside, per_block(so - cs)[:, :, None], 0), axis=0)
    grp = jnp.where(x < per_block(tot)[:, None], grp, -1)
    n_active = blk_end[-1].reshape(1)
    i32 = lambda a: a.reshape(-1).astype(jnp.int32)
    return block_e, i32(n_active), i32(grp)


def _rope_tables(pos):
    half = ROT_DIM // 2
    inv = ROPE_THETA ** (-jnp.arange(half, dtype=F32) * (2.0 / ROT_DIM))
    ang = pos.astype(F32)[:, None] * inv[None, :]
    cos, sin = jnp.cos(ang), jnp.sin(ang)
    n = pos.shape[0]
    ones = jnp.ones((n, A_DH - ROT_DIM), F32)
    zeros_h = jnp.zeros((n, half), F32)
    zeros_r = jnp.zeros((n, A_DH - ROT_DIM), F32)
    cos64 = jnp.concatenate([cos, cos, ones], axis=1)
    sprev64 = jnp.concatenate([zeros_h, sin, zeros_r], axis=1)
    snext64 = jnp.concatenate([-sin, zeros_h, zeros_r], axis=1)
    two = lambda a: jnp.concatenate([a, a], axis=1)
    return two(cos64), two(sprev64), two(snext64)


def _prep_weights(w_in, b_in, q_norm_g, k_norm_g, cmp_pe_k, cmp_pe_v, cmp_w_k, cmp_w_v,
                  w_up_m, w_up_a, w_out, w_router, b_router):
    b2 = b_in.reshape(1, N_IN)
    wm = w_in[:, OFF_MQ:OFF_MO].astype(BF16)
    bm = b2[:, OFF_MQ:OFF_MO]
    wq = w_in[:, OFF_AQ:OFF_AKV].astype(BF16)
    bq = b2[:, OFF_AQ:OFF_AKV]
    wkv = w_in[:, OFF_AKV:OFF_AG].astype(BF16)
    bkv = b2[:, OFF_AKV:OFF_AG]
    n_small = 2 * M_HEADS + 3 * A_HEADS
    ws = jnp.concatenate([w_in[:, OFF_MI:OFF_AQ], w_in[:, OFF_AG:OFF_GA],
                          jnp.zeros((D_MODEL, LANES - n_small), F32)], axis=1)
    bs = jnp.concatenate([b2[:, OFF_MI:OFF_AQ], b2[:, OFF_AG:OFF_GA], jnp.zeros((1, LANES - n_small), F32)], axis=1)
    qg = jnp.tile(q_norm_g, A_HEADS).reshape(1, A_WIDTH)
    kg = jnp.stack([jnp.tile(k_norm_g[1], A_KV), jnp.tile(k_norm_g[2], A_KV)], axis=0)
    kg0 = jnp.tile(k_norm_g[0], A_KV).reshape(1, LANES)
    hid = jnp.arange(A_WIDTH) // A_DH
    bd = jnp.where(hid[:, None] == hid[None, :], 1.0 / A_DH, 0.0).astype(BF16)
    inproj_w = (wm, bm, wq, bq, wkv, bkv, ws, bs, qg, kg, bd)

    z = jnp.zeros((CMP_LEN, A_DH, A_DH), F32)
    r0 = jnp.concatenate([cmp_w_k, z, z, z], axis=2)
    r1 = jnp.concatenate([z, cmp_w_k, z, z], axis=2)
    r2 = jnp.concatenate([z, z, cmp_w_v, z], axis=2)
    r3 = jnp.concatenate([z, z, z, cmp_w_v], axis=2)
    wbd = jnp.concatenate([r0, r1, r2, r3], axis=1).astype(BF16)
    pe = jnp.concatenate([cmp_pe_k, cmp_pe_k, cmp_pe_v, cmp_pe_v], axis=1).reshape(CMP_LEN, 1, 2 * LANES)

    wog = jnp.concatenate([w_in[:, OFF_MO:OFF_MI], w_in[:, OFF_GA:N_IN]], axis=1).astype(BF16)
    bog = jnp.concatenate([b2[:, OFF_MO:OFF_MI], b2[:, OFF_GA:N_IN]], axis=1)
    mixout_w = (wog, bog, w_up_m.astype(BF16), w_up_a.astype(BF16), w_out.astype(BF16),
                w_router.T, b_router.reshape(N_EXPERTS, 1))
    return inproj_w, (wbd, pe, kg0), mixout_w


def _pick_tile(m, pref):
    t = pref
    while m % t:
        t //= 2
    return t


def kernel(x_prompt, x_sample, cache_nsa_kv, state_win_kv, state_mlstm_C, state_mlstm_n, state_mlstm_m, page_table, c_prompt, c_sample, w_ada, b_ada, g_mix, g_ffn, w_in, b_in, q_norm_g, k_norm_g, cmp_pe_k, cmp_pe_v, cmp_w_k, cmp_w_v, w_up_m, w_up_a, w_out, w_router, b_router, w_gu, b_gu, w_dn, b_dn):
    depth = w_in.shape[0]
    assert depth == 1
    B, T, D = x_prompt.shape
    DB, TS, _ = x_sample.shape
    n_pages = page_table.shape[1]
    past_len = n_pages * PAGE_SIZE
    wbuf = state_win_kv.shape[2]
    tp = SAMPLE_PAD_T
    assert TS <= tp and wbuf % tp == 0 and T % 128 == 0

    l = 0
    inproj_w, cmp_w, mixout_w = _prep_weights(
        w_in[l], b_in[l], q_norm_g[l], k_norm_g[l], cmp_pe_k[l], cmp_pe_v[l], cmp_w_k[l], cmp_w_v[l],
        w_up_m[l], w_up_a[l], w_out[l], w_router[l], b_router[l])
    wbd, pe, kg0 = cmp_w
    gmix = g_mix[l].reshape(1, D)
    gffn = g_ffn[l].reshape(1, D)

    nc = B + DB
    nc_pad = -(-nc // SUBLANES) * SUBLANES
    c_all = jnp.concatenate([c_prompt, c_sample, jnp.zeros((nc_pad - nc, D), F32)], axis=0)
    mod = _adaln(c_all, w_ada[l], b_ada[l])
    mod_p = mod[:B].reshape(B, 1, 6 * D)
    mod_s = jnp.repeat(mod[B:B + DB], tp, axis=0).reshape(1, DB * tp, 6 * D)

    mp = B * T
    tm = _pick_tile(T, 256)
    xp = x_prompt.reshape(mp, D)
    tabs_p = _rope_tables(jnp.arange(T, dtype=jnp.int32))
    mq, mk, mv, q, qr, rows, win, small, rows_t = _inproj(xp, mod_p, gmix, tabs_p, inproj_w, tm, T // tm, T // tm,
                                                          rows_t_batches=B)
    Lp = _pick_tile(T, 128)
    hm, C_p, n_p, m_p = _mlstm(mq, mk, mv, small, B, T, T, Lp)
    o_nsa = _nsa_prompt(q, qr, small, rows, win, wbd, pe, kg0, B, T)
    assert T % MOE_TM == 0
    ms_pad = -(-(DB * tp) // MOE_TM) * MOE_TM
    nt_p = mp // MOE_TM
    nt_all = nt_p + ms_pad // MOE_TM + 1
    x1_p, xsl, info, cnt = _mixout(xp, hm, o_nsa, mod_p, gmix, gffn, mixout_w, T // MOE_TM, nt_all)

    ms = DB * tp
    xs_pad = jnp.concatenate([x_sample, jnp.zeros((DB, tp - TS, D), F32)], axis=1).reshape(ms, D)
    pos_s = past_len + jnp.tile(jnp.arange(tp, dtype=jnp.int32), DB)
    tabs_s = _rope_tables(pos_s)
    mq_s, mk_s, mv_s, q_s, qr_s, rows_s, win_s, small_s = _inproj(xs_pad, mod_s, gmix, tabs_s, inproj_w, ms, 1, 1)
    hm_s, C_s, n_s, m_s = _mlstm(mq_s, mk_s, mv_s, small_s, DB, tp, TS, tp,
                                 state=(state_mlstm_C[l], state_mlstm_n[l], state_mlstm_m[l]))
    cache2 = jnp.transpose(cache_nsa_kv[l], (0, 2, 3, 4, 1)).reshape(cache_nsa_kv.shape[1], 4 * LANES, PAGE_SIZE)
    winbuf = state_win_kv[l].reshape(DB, wbuf, 2 * LANES)
    o_nsa_s, win_out_s = _nsa_sample(page_table, cache2, q_s, qr_s, small_s, rows_s, win_s, winbuf,
                                     wbd, pe, kg0, TS)
    assert ms_pad == MOE_TM
    rpad = lambda a: jnp.concatenate([a, jnp.zeros((ms_pad - ms, a.shape[1]), a.dtype)], axis=0) if ms_pad > ms else a
    mod_sp = rpad(mod_s[0])[None]
    x1_s, xsl, info, cnt = _mixout(rpad(xs_pad), rpad(hm_s), rpad(o_nsa_s), mod_sp, gmix, gffn, mixout_w,
                                   1, nt_all, tile0=nt_p, shared=(xsl, info, cnt),
                                   t_mod=tp, t_valid=TS, m_valid=ms)

    ysl = _moe_experts(_moe_plan(cnt[:, :, 0]), xsl, w_gu[l], b_gu[l], w_dn[l], b_dn[l])
    y_p = _combine(ysl, info, x1_p, mod_p, T // MOE_TM).reshape(B, T, D)
    y_s_all = _combine(ysl, info, x1_s, mod_sp, 1, tile0=nt_p)
    valid = lambda a: a.reshape(DB, tp, -1)[:, :TS].reshape(DB * TS, -1)
    y_s = valid(y_s_all[:ms]).reshape(DB, TS, D)

    kv_p = jnp.transpose(rows_t.reshape(B, 4, A_KV, A_DH, T), (0, 4, 1, 2, 3))[None]
    kv_s = valid(rows_s).reshape(1, DB, TS, 4, A_KV, A_DH)
    wp = min(WINDOW, T)
    win_p = win.reshape(B, T, 2, A_KV, A_DH)[:, T - wp:][None]
    win_s_out = win_out_s.reshape(1, DB, wbuf, 2, A_KV, A_DH)
    return (y_p, y_s, kv_p, kv_s, win_p, win_s_out,
            C_p[None], n_p[None], m_p[None], C_s[None], n_s[None], m_s[None])
```

```python
import functools
import math

import jax
import jax.numpy as jnp
from jax import lax
from jax.experimental import pallas as pl
from jax.experimental.pallas import tpu as pltpu

F32 = jnp.float32
BF16 = jnp.bfloat16

D_MODEL = 1024
M_HEADS = 4
M_DH = 128
M_WIDTH = M_HEADS * M_DH
A_HEADS = 8
A_KV = 2
A_HPG = A_HEADS // A_KV
A_DH = 64
A_WIDTH = A_HEADS * A_DH
CMP_STRIDE = 16
CMP_LEN = 32
SEL_LEN = 64
N_SEL = 16
WINDOW = 512
PAGE_SIZE = 128
ROPE_THETA = 500000.0
ROT_DIM = A_DH // 4
ATT_SCALE = A_DH ** -0.5
N_EXPERTS = 32
TOP_K = 4
D_EXPERT = D_MODEL
SWIGLU_LIMIT = 7.0
SWIGLU_ALPHA = 1.702
EPS = 1e-6

OFF_MQ, OFF_MK, OFF_MV, OFF_MO = 0, M_WIDTH, 2 * M_WIDTH, 3 * M_WIDTH
OFF_MI = 4 * M_WIDTH
OFF_MF = OFF_MI + M_HEADS
OFF_AQ = OFF_MF + M_HEADS
OFF_AKV = OFF_AQ + A_WIDTH
OFF_AG = OFF_AKV + 6 * A_KV * A_DH
OFF_GA = OFF_AG + 3 * A_HEADS
OFF_GB = OFF_GA + D_MODEL
N_IN = OFF_GB + D_MODEL

LANES = 128
SUBLANES = 8
VMEM_LIMIT = 56 * 1024 * 1024

NEG_BIG = -1e30
M_INIT = -1e29
LOG2E = 1.4426950408889634
SAMPLE_PAD_T = 8


def _cparams(sem):
    return pltpu.CompilerParams(dimension_semantics=sem, vmem_limit_bytes=VMEM_LIMIT)


def _bdot(a, b):
    return jnp.dot(a.astype(BF16), b.astype(BF16), preferred_element_type=F32)


def _bdot_t(a, b):
    return lax.dot_general(a.astype(BF16), b.astype(BF16), (((1,), (1,)), ((), ())),
                           preferred_element_type=F32)


def _split(a):
    hi = a.astype(BF16)
    lo = (a - hi.astype(F32)).astype(BF16)
    return hi, lo


def _dot3(a, b):
    ah, al = _split(a)
    bh, bl = _split(b)
    return (jnp.dot(ah, bh, preferred_element_type=F32) + jnp.dot(al, bh, preferred_element_type=F32)
            + jnp.dot(ah, bl, preferred_element_type=F32))


def _dot2_exact_rhs(a, b_bf16):
    ah, al = _split(a)
    return jnp.dot(ah, b_bf16, preferred_element_type=F32) + jnp.dot(al, b_bf16, preferred_element_type=F32)


def _sigmoid(x):
    return 0.5 * jnp.tanh(0.5 * x) + 0.5


def _rmsnorm_rows(x, g):
    return x * lax.rsqrt(jnp.mean(x * x, axis=-1, keepdims=True) + EPS) * g


def _adaln_kernel(c_ref, w_ref, b_ref, o_ref):
    c = c_ref[...]
    s = c * _sigmoid(c)
    o_ref[...] = _dot3(s, w_ref[...]) + b_ref[...]


def _adaln(c, w, b):
    mc, d = c.shape
    n = w.shape[1]
    tn = 1024
    return pl.pallas_call(
        _adaln_kernel,
        grid=(n // tn,),
        in_specs=[pl.BlockSpec((mc, d), lambda j: (0, 0)),
                  pl.BlockSpec((d, tn), lambda j: (0, j)),
                  pl.BlockSpec((1, tn), lambda j: (0, j))],
        out_specs=pl.BlockSpec((mc, tn), lambda j: (0, j)),
        out_shape=jax.ShapeDtypeStruct((mc, n), F32),
        compiler_params=_cparams(("parallel",)),
        name="adaln",
    )(c, w, b.reshape(1, n))


def _head_norm(z, bd, gain):
    ms = _dot2_exact_rhs(z * z, bd)
    return z * lax.rsqrt(ms + EPS) * gain


def _rope(z, cos, s_prev, s_next):
    w = z.shape[1]
    rep = w // LANES
    if rep > 1:
        cos = jnp.concatenate([cos] * rep, axis=1)
        s_prev = jnp.concatenate([s_prev] * rep, axis=1)
        s_next = jnp.concatenate([s_next] * rep, axis=1)
    z_prev = pltpu.roll(z, ROT_DIM // 2, 1)
    z_next = pltpu.roll(z, w - ROT_DIM // 2, 1)
    return z * cos + z_prev * s_prev + z_next * s_next


def _inproj_kernel(x_ref, mod_ref, gmix_ref, cos_ref, sp_ref, sn_ref,
                   wm_ref, bm_ref, wq_ref, bq_ref, wkv_ref, bkv_ref, ws_ref, bs_ref,
                   qg_ref, kg_ref, bd_ref,
                   mq_ref, mk_ref, mv_ref, q_ref, qr_ref, rows_ref, win_ref, small_ref, rows_t_ref=None):
    x = x_ref[...]
    sh1 = mod_ref[:, 0:D_MODEL]
    sc1 = mod_ref[:, D_MODEL:2 * D_MODEL]
    h = _rmsnorm_rows(x, gmix_ref[...]) * (1.0 + sc1) + sh1
    hb = h.astype(BF16)

    mq_ref[...] = jnp.dot(hb, wm_ref[:, 0:M_WIDTH], preferred_element_type=F32) + bm_ref[:, 0:M_WIDTH]
    mk = jnp.dot(hb, wm_ref[:, M_WIDTH:2 * M_WIDTH], preferred_element_type=F32) + bm_ref[:, M_WIDTH:2 * M_WIDTH]
    mk_ref[...] = mk * (M_DH ** -0.5)
    mv_ref[...] = (jnp.dot(hb, wm_ref[:, 2 * M_WIDTH:3 * M_WIDTH], preferred_element_type=F32)
                   + bm_ref[:, 2 * M_WIDTH:3 * M_WIDTH])

    cos, sp, sn = cos_ref[...], sp_ref[...], sn_ref[...]
    zq = jnp.dot(hb, wq_ref[...], preferred_element_type=F32) + bq_ref[...]
    qn = _head_norm(zq, bd_ref[...], qg_ref[...])
    q_ref[...] = qn
    qr_ref[...] = _rope(qn, cos, sp, sn)

    zkv = jnp.dot(hb, wkv_ref[...], preferred_element_type=F32) + bkv_ref[...]
    bd2 = bd_ref[0:LANES, 0:LANES]
    ksel = _head_norm(zkv[:, 2 * LANES:3 * LANES], bd2, kg_ref[0:1, :])
    rows = jnp.concatenate([zkv[:, 0:2 * LANES], _rope(ksel, cos, sp, sn), zkv[:, 3 * LANES:4 * LANES]], axis=1)
    rows_ref[...] = rows
    if rows_t_ref is not None:
        rows_t_ref[...] = jnp.transpose(rows)
    kwin = _head_norm(zkv[:, 4 * LANES:5 * LANES], bd2, kg_ref[1:2, :])
    win_ref[:, 0:LANES] = _rope(kwin, cos, sp, sn)
    win_ref[:, LANES:2 * LANES] = zkv[:, 5 * LANES:6 * LANES]

    small_ref[...] = _dot3(h, ws_ref[...]) + bs_ref[...]


def _inproj(x2, mod3, gmix, tabs, wts, tm, tiles_per_mod, pos_tiles, rows_t_batches=None):
    m = x2.shape[0]
    cos_t, sp_t, sn_t = tabs
    (wm, bm, wq, bq, wkv, bkv, ws, bs, qg, kg, bd) = wts
    r = mod3.shape[1]
    row = lambda i: (i, 0)
    const = lambda i: (0, 0)
    tab = lambda i: (i % pos_tiles, 0)
    in_specs = [
        pl.BlockSpec((tm, D_MODEL), row),
        pl.BlockSpec((None, r, 6 * D_MODEL), lambda i: (i // tiles_per_mod, 0, 0)),
        pl.BlockSpec((1, D_MODEL), const),
        pl.BlockSpec((tm, LANES), tab), pl.BlockSpec((tm, LANES), tab), pl.BlockSpec((tm, LANES), tab),
        pl.BlockSpec(wm.shape, const), pl.BlockSpec(bm.shape, const),
        pl.BlockSpec(wq.shape, const), pl.BlockSpec(bq.shape, const),
        pl.BlockSpec(wkv.shape, const), pl.BlockSpec(bkv.shape, const),
        pl.BlockSpec(ws.shape, const), pl.BlockSpec(bs.shape, const),
        pl.BlockSpec(qg.shape, const), pl.BlockSpec(kg.shape, const), pl.BlockSpec(bd.shape, const),
    ]
    widths = (M_WIDTH, M_WIDTH, M_WIDTH, A_WIDTH, A_WIDTH, 4 * LANES, 2 * LANES, LANES)
    out_specs = [pl.BlockSpec((tm, w), row) for w in widths]
    out_shape = [jax.ShapeDtypeStruct((m, w), F32) for w in widths]
    if rows_t_batches is not None:
        out_specs.append(pl.BlockSpec((None, 4 * LANES, tm), lambda i: (i // tiles_per_mod, 0, i % tiles_per_mod)))
        out_shape.append(jax.ShapeDtypeStruct((rows_t_batches, 4 * LANES, m // rows_t_batches), F32))
    return pl.pallas_call(
        _inproj_kernel,
        grid=(m // tm,),
        in_specs=in_specs,
        out_specs=out_specs,
        out_shape=out_shape,
        compiler_params=_cparams(("parallel",)),
        name="inproj",
    )(x2, mod3, gmix, cos_t, sp_t, sn_t, wm, bm, wq, bq, wkv, bkv, ws, bs, qg, kg, bd)


def _log_sigmoid(x):
    return jnp.minimum(x, 0.0) - jnp.log(1.0 + jnp.exp(-jnp.abs(x)))


def _mlstm_kernel(*refs, L, t_valid, has_state):
    if has_state:
        q_ref, k_ref, v_ref, s_ref, c0_ref, n0_ref, m0_ref, h_ref, c_ref, n_ref, m_ref = refs
    else:
        q_ref, k_ref, v_ref, s_ref, h_ref, c_ref, n_ref, m_ref = refs
    c = pl.program_id(1)

    @pl.when(c == 0)
    def _():
        if has_state:
            c_ref[...] = c0_ref[...]
            n_ref[...] = n0_ref[...]
            m_ref[...] = m0_ref[...]
        else:
            c_ref[...] = jnp.zeros(c_ref.shape, F32)
            n_ref[...] = jnp.zeros(n_ref.shape, F32)
            m_ref[...] = jnp.zeros(m_ref.shape, F32)

    row = lax.broadcasted_iota(jnp.int32, (L, L), 0)
    col = lax.broadcasted_iota(jnp.int32, (L, L), 1)
    causal = col <= row
    eye = col == row
    tok_col = c * L + lax.broadcasted_iota(jnp.int32, (L, 1), 0)
    valid_col = tok_col < t_valid
    for hd in range(M_HEADS):
        lo, hi = hd * M_DH, (hd + 1) * M_DH
        q = q_ref[:, lo:hi]
        k = k_ref[:, lo:hi]
        v = v_ref[:, lo:hi]
        i_col = s_ref[:, hd:hd + 1]
        lf_col = _log_sigmoid(s_ref[:, M_HEADS + hd:M_HEADS + hd + 1])
        lf_col = jnp.where(valid_col, lf_col, 0.0)
        i_col = jnp.where(valid_col, i_col, -jnp.inf)
        if L == LANES:
            i_col = jnp.broadcast_to(i_col, (L, L))
            lf_c = jnp.broadcast_to(lf_col, (L, L))
            p0 = lf_c.astype(BF16)
            r1 = lf_c - p0.astype(F32)
            p1 = r1.astype(BF16)
            p2 = (r1 - p1.astype(F32)).astype(BF16)
            tril = jnp.where(causal, 1.0, 0.0).astype(BF16)
            b_col = (jnp.dot(tril, p0, preferred_element_type=F32) + jnp.dot(tril, p1, preferred_element_type=F32)
                     + jnp.dot(tril, p2, preferred_element_type=F32))
            i_row = jnp.transpose(i_col)[0:1, :]
            b_row = jnp.transpose(b_col)[0:1, :]
        else:
            i_row = jnp.sum(jnp.where(eye, i_col, 0.0), axis=0, keepdims=True)
            lf_row = jnp.sum(jnp.where(eye, lf_col, 0.0), axis=0, keepdims=True)
            b_col = jnp.sum(jnp.where(causal, lf_row, 0.0), axis=1, keepdims=True)
            b_row = jnp.sum(jnp.where(row <= col, lf_col, 0.0), axis=0, keepdims=True)
        m_prev = m_ref[:, hd:hd + 1]
        dmat = jnp.where(causal, b_col - b_row + i_row, -jnp.inf)
        inter = b_col + m_prev
        m_row = jnp.maximum(jnp.max(dmat, axis=1, keepdims=True), inter)
        w = jnp.exp(dmat - m_row)
        w_inter = jnp.exp(inter - m_row)
        s = _bdot_t(q, k) * w
        cm = c_ref[hd]
        nv = n_ref[hd]
        num = _bdot(s, v) + w_inter * _bdot_t(q, cm)
        den = jnp.sum(s, axis=1, keepdims=True) + w_inter * jnp.sum(q * nv, axis=1, keepdims=True)
        h_ref[:, lo:hi] = num / jnp.maximum(jnp.abs(den), jnp.exp(-m_row))
        b_last = b_col[L - 1:L, 0:1]
        dec_col = b_last - b_col + i_col
        dec_row = b_last - b_row + i_row
        m_new = jnp.maximum(b_last + m_prev, jnp.max(dec_row, axis=1, keepdims=True))
        ws_col = jnp.exp(dec_col - m_new)
        wc = jnp.exp(b_last + m_prev - m_new)
        vw = (v * ws_col).astype(BF16)
        upd = lax.dot_general(vw, k.astype(BF16), (((0,), (0,)), ((), ())), preferred_element_type=F32)
        c_ref[hd] = wc * cm + upd
        n_ref[hd] = wc * nv + jnp.sum(k * ws_col, axis=0, keepdims=True)
        m_ref[:, hd:hd + 1] = m_new


def _mlstm(mq, mk, mv, small, nb, t_pad, t_valid, L, state=None):
    nc = t_pad // L
    has_state = state is not None
    blk = lambda b, c: (b * nc + c, 0)
    st4 = lambda b, c: (b, 0, 0, 0)
    st3 = lambda b, c: (b, 0, 0)
    in_specs = [pl.BlockSpec((L, M_WIDTH), blk)] * 3 + [pl.BlockSpec((L, LANES), blk)]
    args = [mq, mk, mv, small]
    if has_state:
        c0, n0, m0 = state
        in_specs += [pl.BlockSpec((None, M_HEADS, M_DH, M_DH), st4),
                     pl.BlockSpec((None, M_HEADS, 1, M_DH), st4),
                     pl.BlockSpec((None, 1, M_HEADS), st3)]
        args += [c0, n0.reshape(nb, M_HEADS, 1, M_DH), m0.reshape(nb, 1, M_HEADS)]
    out_specs = [pl.BlockSpec((L, M_WIDTH), blk),
                 pl.BlockSpec((None, M_HEADS, M_DH, M_DH), st4),
                 pl.BlockSpec((None, M_HEADS, 1, M_DH), st4),
                 pl.BlockSpec((None, 1, M_HEADS), st3)]
    out_shape = [jax.ShapeDtypeStruct((nb * t_pad, M_WIDTH), F32),
                 jax.ShapeDtypeStruct((nb, M_HEADS, M_DH, M_DH), F32),
                 jax.ShapeDtypeStruct((nb, M_HEADS, 1, M_DH), F32),
                 jax.ShapeDtypeStruct((nb, 1, M_HEADS), F32)]
    h, cs, ns, ms = pl.pallas_call(
        functools.partial(_mlstm_kernel, L=L, t_valid=t_valid, has_state=has_state),
        grid=(nb, nc),
        in_specs=in_specs,
        out_specs=out_specs,
        out_shape=out_shape,
        compiler_params=_cparams(("parallel", "arbitrary")),
        name="mlstm",
    )(*args)
    return h, cs, ns.reshape(nb, M_HEADS, M_DH), ms.reshape(nb, M_HEADS)


def _stack_heads(qt, g):
    t = qt.shape[0]
    z = jnp.zeros((t, A_DH), F32)
    parts = []
    for hh in range(A_HPG):
        hd = g * A_HPG + hh
        qh = qt[:, hd * A_DH:(hd + 1) * A_DH] * (ATT_SCALE * LOG2E)
        parts.append(jnp.concatenate([qh, z], axis=1) if g == 0 else jnp.concatenate([z, qh], axis=1))
    return jnp.concatenate(parts, axis=0).astype(BF16)


def _gate_cols(small, g, br):
    cols = []
    for hh in range(A_HPG):
        c0 = 2 * M_HEADS + (g * A_HPG + hh) * 3 + br
        cols.append(_sigmoid(small[:, c0:c0 + 1]))
    return jnp.concatenate(cols, axis=0)


def _compress(k_ref, v_ref, nseg, wbd_ref, pe_ref, kg0):
    acc_lo = jnp.zeros((nseg, 2 * LANES), F32)
    acc_hi = jnp.zeros((nseg, 2 * LANES), F32)
    for l in range(CMP_STRIDE):
        xl = jnp.concatenate([k_ref[pl.ds(l, nseg, stride=CMP_STRIDE), :],
                              v_ref[pl.ds(l, nseg, stride=CMP_STRIDE), :]], axis=1)
        acc_lo = acc_lo + _bdot(xl + pe_ref[l], wbd_ref[l])
        acc_hi = acc_hi + _bdot(xl + pe_ref[CMP_STRIDE + l], wbd_ref[CMP_STRIDE + l])
    return _compress_finish(acc_lo, acc_hi, nseg, kg0)


def _compress_grouped(x_ref, nseg, wbd_ref, pe_ref, kg0):
    acc_lo = jnp.zeros((nseg, 2 * LANES), F32)
    acc_hi = jnp.zeros((nseg, 2 * LANES), F32)
    pe_lo = jnp.zeros((SUBLANES, 2 * LANES), F32)
    pe_hi = jnp.zeros((SUBLANES, 2 * LANES), F32)
    for l in range(CMP_STRIDE):
        xl = x_ref[l].astype(BF16)
        acc_lo = acc_lo + jnp.dot(xl, wbd_ref[l], preferred_element_type=F32)
        acc_hi = acc_hi + jnp.dot(xl, wbd_ref[CMP_STRIDE + l], preferred_element_type=F32)
        pe_lo = pe_lo + _bdot(jnp.broadcast_to(pe_ref[l], (SUBLANES, 2 * LANES)), wbd_ref[l])
        pe_hi = pe_hi + _bdot(jnp.broadcast_to(pe_ref[CMP_STRIDE + l], (SUBLANES, 2 * LANES)),
                              wbd_ref[CMP_STRIDE + l])
    return _compress_finish(acc_lo + pe_lo[0:1, :], acc_hi + pe_hi[0:1, :], nseg, kg0)


def _compress_finish(acc_lo, acc_hi, nseg, kg0):
    kv = acc_lo + pltpu.roll(acc_hi, nseg - 1, 0)
    kc = kv[:, 0:LANES]
    vc = kv[:, LANES:2 * LANES]
    lane = lax.broadcasted_iota(jnp.int32, (nseg, LANES), 1)
    sq = kc * kc
    ms0 = jnp.sum(jnp.where(lane < A_DH, sq, 0.0), axis=1, keepdims=True) * (1.0 / A_DH)
    ms1 = jnp.sum(jnp.where(lane >= A_DH, sq, 0.0), axis=1, keepdims=True) * (1.0 / A_DH)
    ms = jnp.where(lane < A_DH, ms0, ms1)
    kc = kc * lax.rsqrt(ms + EPS) * kg0
    return kc, vc


def _cmp_branch(qn_g, kc_b, vc_b, tpos_rows, nseg, n_tok):
    s = _bdot_t(qn_g, kc_b)
    nidx = lax.broadcasted_iota(jnp.int32, (1, nseg), 1)
    vis = (nidx * CMP_STRIDE + (CMP_LEN - 1)) <= tpos_rows
    sm = jnp.where(vis, s, NEG_BIG)
    mx = jnp.max(sm, axis=1, keepdims=True)
    e = jnp.where(vis, jnp.exp2(sm - mx), 0.0)
    d = jnp.sum(e, axis=1, keepdims=True)
    p = e / jnp.where(d > 0, d, 1.0)
    o = _bdot(p, vc_b)
    imp = p[0:n_tok]
    for hh in range(1, A_HPG):
        imp = imp + p[hh * n_tok:(hh + 1) * n_tok]
    return o, imp


def _masked_attn_direct(q_g, k_parts, v_parts, allowed_parts, feature_major):
    ss = [jnp.where(al, _bdot(q_g, kk) if fm else _bdot_t(q_g, kk), NEG_BIG)
          for kk, al, fm in zip(k_parts, allowed_parts, feature_major)]
    mx = ss[0].max(axis=1, keepdims=True)
    for s in ss[1:]:
        mx = jnp.maximum(mx, s.max(axis=1, keepdims=True))
    num = None
    den = None
    for s, al, vv, fm in zip(ss, allowed_parts, v_parts, feature_major):
        e = jnp.where(al, jnp.exp2(s - mx), 0.0)
        dd = jnp.sum(e, axis=1, keepdims=True)
        oo = _bdot_t(e, vv) if fm else _bdot(e, vv)
        num = oo if num is None else num + oo
        den = dd if den is None else den + dd
    return num / jnp.where(den > 0, den, 1.0)


def _assemble_heads(o_groups, n_tok):
    pieces = []
    for g in range(A_KV):
        for hh in range(A_HPG):
            pieces.append(o_groups[g][hh * n_tok:(hh + 1) * n_tok, g * A_DH:(g + 1) * A_DH])
    return jnp.concatenate(pieces, axis=1)


def _lane_rep(a, rep):
    return a if rep == 1 else jnp.concatenate([a] * rep, axis=1)


def _nsa_prompt_kernel(q_ref, qr_ref, small_ref, rows_ref, win_ref, wbd_ref, pe_ref, kg0_ref,
                       pool_ref, o_ref,
                       kraw_sc, vraw_sc, kc_sc, vct_sc, sel_sc, m_sc, acc_sc, s_sc, *, T, tq, kc_len):
    qi = pl.program_id(1)
    nseg = T // CMP_STRIDE
    nsb = T // SEL_LEN
    bpc = kc_len // SEL_LEN

    @pl.when(qi == 0)
    def _():
        kraw_sc[...] = rows_ref[:, 0:LANES]
        vraw_sc[...] = rows_ref[:, LANES:2 * LANES]
        kc, vc = _compress(kraw_sc, vraw_sc, nseg, wbd_ref, pe_ref, kg0_ref[...])
        kc_sc[...] = kc
        vct_sc[...] = jnp.transpose(vc)

    t0 = qi * tq
    tpos = t0 + lax.broadcasted_iota(jnp.int32, (1, tq), 1)
    tpos4 = _lane_rep(tpos, A_HPG)
    q = q_ref[...]
    qr = qr_ref[...]
    small_t = jnp.transpose(small_ref[...])
    kc_b = kc_sc[...].astype(BF16)
    vct_b = vct_sc[...].astype(BF16)
    bidx = lax.broadcasted_iota(jnp.int32, (nsb, tq), 0)
    cur = tpos // SEL_LEN
    vis = (lax.broadcasted_iota(jnp.int32, (nseg, 1), 0) * CMP_STRIDE + (CMP_LEN - 1)) <= tpos4
    qr_gs = [_stack_heads(qr, g) for g in range(A_KV)]
    o_cmps = []
    for g in range(A_KV):
        sm = jnp.where(vis, _bdot_t(kc_b, _stack_heads(q, g)), NEG_BIG)
        mx = jnp.max(sm, axis=0, keepdims=True)
        e = jnp.where(vis, jnp.exp2(sm - mx), 0.0)
        d = jnp.sum(e, axis=0, keepdims=True)
        p = e / jnp.where(d > 0, d, 1.0)
        o_cmps.append(jnp.dot(vct_b, p.astype(BF16), preferred_element_type=F32))
        imp = p[:, 0:tq]
        for hh in range(1, A_HPG):
            imp = imp + p[:, hh * tq:(hh + 1) * tq]
        ih, il = _split(imp)
        imp_t = (jnp.dot(pool_ref[...], ih, preferred_element_type=F32)
                 + jnp.dot(pool_ref[...], il, preferred_element_type=F32))[0:nsb]
        val = jnp.where(bidx < cur, imp_t, -1.0)
        rank = jnp.zeros((nsb, tq), F32)
        for bp in range(nsb):
            vb = val[bp:bp + 1, :]
            rank = rank + jnp.where(vb > val, 1.0, jnp.where((vb == val) & (bidx > bp), 1.0, 0.0))
        sel_sc[g] = jnp.where(((rank < (N_SEL - 1)) & (bidx < cur)) | (bidx == cur), 1.0, 0.0)

    m_sc[...] = jnp.full(m_sc.shape, M_INIT, F32)
    acc_sc[...] = jnp.zeros(acc_sc.shape, F32)

    def with_ones_row(vt_, g):
        vb = vt_.astype(BF16)
        r0, pad = (1 - g) * A_DH, 2 * SUBLANES
        ones = jnp.ones((pad, vb.shape[1]), BF16)
        return jnp.concatenate(([vb[0:r0]] if r0 else []) + [ones, vb[r0 + pad:]], axis=0)

    def sel_body(c, carry):
        k0 = pl.multiple_of(c * kc_len, kc_len)
        kb = rows_ref[pl.ds(k0, kc_len), 2 * LANES:3 * LANES].astype(BF16)
        vt = jnp.transpose(rows_ref[pl.ds(k0, kc_len), 3 * LANES:4 * LANES])
        causal = (k0 + lax.broadcasted_iota(jnp.int32, (kc_len, 1), 0)) <= tpos
        for g in range(A_KV):
            s_sc[g, 0:kc_len, :] = _bdot_t(kb, qr_gs[g])
        for g in range(A_KV):
            selc = sel_sc[g, pl.ds(pl.multiple_of(c * bpc, bpc), bpc), :]
            selx = jnp.concatenate([jnp.broadcast_to(selc[j:j + 1, :], (SEL_LEN, tq)) for j in range(bpc)], axis=0)
            bias = jnp.where(causal & (selx > 0.5), 0.0, NEG_BIG)
            sm = s_sc[g, 0:kc_len, :] + _lane_rep(bias, A_HPG)
            m_prev = m_sc[g]
            m_new = jnp.maximum(m_prev, jnp.max(sm, axis=0, keepdims=True))
            alpha = jnp.exp2(m_prev - m_new)
            p = jnp.exp2(sm - m_new)
            acc_sc[g] = alpha * acc_sc[g] + jnp.dot(with_ones_row(vt, g), p.astype(BF16),
                                                    preferred_element_type=F32)
            m_sc[g] = m_new
        return carry

    lax.fori_loop(0, (t0 + tq + kc_len - 1) // kc_len, sel_body, 0)

    wk = min(WINDOW + tq, T)
    w0 = pl.multiple_of(jnp.clip(t0 + tq - wk, 0, T - wk), tq)
    kw = win_ref[pl.ds(w0, wk), 0:LANES].astype(BF16)
    vwt = jnp.transpose(win_ref[pl.ds(w0, wk), LANES:2 * LANES])
    wdiff = tpos - (w0 + lax.broadcasted_iota(jnp.int32, (wk, 1), 0))
    wbias = _lane_rep(jnp.where((wdiff >= 0) & (wdiff < WINDOW), 0.0, NEG_BIG), A_HPG)

    def gate_row(g, br):
        cols = [2 * M_HEADS + (g * A_HPG + hh) * 3 + br for hh in range(A_HPG)]
        return jnp.concatenate([_sigmoid(small_t[c0:c0 + 1, :]) for c0 in cols], axis=1)

    for g in range(A_KV):
        s_sc[g, 0:wk, :] = _bdot_t(kw, qr_gs[g])
    o_ts = []
    for g in range(A_KV):
        den = (1 - g) * A_DH
        acc = acc_sc[g]
        l = acc[den:den + 1, :]
        o_sel = acc / jnp.where(l > 0, l, 1.0)
        sw = s_sc[g, 0:wk, :] + wbias
        pw = jnp.exp2(sw - jnp.max(sw, axis=0, keepdims=True))
        ow = jnp.dot(with_ones_row(vwt, g), pw.astype(BF16), preferred_element_type=F32)
        o_win = ow / ow[den:den + 1, :]
        o_ts.append(gate_row(g, 0) * o_cmps[g] + gate_row(g, 1) * o_sel + gate_row(g, 2) * o_win)
    for j in range(A_HEADS // 2):
        g, h0 = j // (A_HPG // 2), 2 * (j % (A_HPG // 2))
        og = o_ts[g][g * A_DH:(g + 1) * A_DH, :]
        pair = jnp.concatenate([og[:, h0 * tq:(h0 + 1) * tq], og[:, (h0 + 1) * tq:(h0 + 2) * tq]], axis=0)
        o_ref[:, j * LANES:(j + 1) * LANES] = jnp.transpose(pair)


def _nsa_prompt(q, qr, small, rows, win, wbd, pe, kg0, nb, T):
    tq = 128
    kc_len = _pick_tile(T, 512)
    nq = T // tq
    nseg = T // CMP_STRIDE
    nsb = T // SEL_LEN
    nsb_p = -(-nsb // SUBLANES) * SUBLANES
    pool = (jnp.arange(nsb_p)[:, None] == jnp.arange(nseg)[None, :] // (SEL_LEN // CMP_STRIDE)).astype(BF16)
    tile = lambda b, i: (b * nq + i, 0)
    per_b = lambda b, i: (b, 0)
    c2 = lambda b, i: (0, 0)
    c3 = lambda b, i: (0, 0, 0)
    c4 = A_HPG * tq
    return pl.pallas_call(
        functools.partial(_nsa_prompt_kernel, T=T, tq=tq, kc_len=kc_len),
        grid=(nb, nq),
        in_specs=[pl.BlockSpec((tq, A_WIDTH), tile), pl.BlockSpec((tq, A_WIDTH), tile),
                  pl.BlockSpec((tq, LANES), tile),
                  pl.BlockSpec((T, 4 * LANES), per_b), pl.BlockSpec((T, 2 * LANES), per_b),
                  pl.BlockSpec(wbd.shape, c3), pl.BlockSpec(pe.shape, c3), pl.BlockSpec(kg0.shape, c2),
                  pl.BlockSpec(pool.shape, c2)],
        out_specs=pl.BlockSpec((tq, A_WIDTH), tile),
        out_shape=jax.ShapeDtypeStruct((nb * T, A_WIDTH), F32),
        scratch_shapes=[pltpu.VMEM((T, LANES), F32), pltpu.VMEM((T, LANES), F32),
                        pltpu.VMEM((nseg, LANES), F32), pltpu.VMEM((LANES, nseg), F32),
                        pltpu.VMEM((A_KV, nsb, tq), F32),
                        pltpu.VMEM((A_KV, 1, c4), F32),
                        pltpu.VMEM((A_KV, LANES, c4), F32),
                        pltpu.VMEM((A_KV, max(kc_len, min(WINDOW + tq, T)), c4), F32)],
        compiler_params=_cparams(("parallel", "arbitrary")),
        name="nsa_prompt",
    )(q, qr, small, rows, win, wbd, pe, kg0, pool)


def _nsa_sample_kernel(pt_ref, cache_ref, q_ref, qr_ref, small_ref, rows_ref, winnew_ref, winbuf_ref,
                       wbd_ref, pe_ref, kg0_ref, pool_ref, expand_ref,
                       o_ref, winout_ref,
                       cmp_buf, sel_buf, xperm_sc, sems, *, n_pages, past_len, t_valid):
    b = pl.program_id(0)
    nb = pl.num_programs(0)
    tp = SAMPLE_PAD_T
    nseg = past_len // CMP_STRIDE
    nsb = past_len // SEL_LEN
    wbuf = winbuf_ref.shape[0]

    def page_copies(bb, p, phase):
        page = pt_ref[bb * n_pages + p]
        dst_lanes = pl.ds(pl.multiple_of(p * PAGE_SIZE, PAGE_SIZE), PAGE_SIZE)
        if phase == 0:
            return [pltpu.make_async_copy(cache_ref.at[page, pl.ds(0, 2 * LANES), :],
                                          cmp_buf.at[:, dst_lanes], sems.at[0])]
        return [pltpu.make_async_copy(cache_ref.at[page, pl.ds(2 * LANES, 2 * LANES), :],
                                      sel_buf.at[:, dst_lanes], sems.at[1])]

    def start_all(bb, phase):
        def body(p, c):
            for cp in page_copies(bb, p, phase):
                cp.start()
            return c
        lax.fori_loop(0, n_pages, body, 0)

    def wait_all(bb, phase):
        def body(p, c):
            for cp in page_copies(bb, p, phase):
                cp.wait()
            return c
        lax.fori_loop(0, n_pages, body, 0)

    @pl.when(b == 0)
    def _():
        start_all(b, 0)

    start_all(b, 1)
    wait_all(b, 0)

    seg_pp = PAGE_SIZE // CMP_STRIDE
    pr = lax.broadcasted_iota(jnp.int32, (PAGE_SIZE, PAGE_SIZE), 0)
    pc = lax.broadcasted_iota(jnp.int32, (PAGE_SIZE, PAGE_SIZE), 1)
    perm = jnp.where(pc == CMP_STRIDE * (pr % seg_pp) + pr // seg_pp, 1.0, 0.0).astype(BF16)
    for p in range(n_pages):
        xp = _bdot_t(perm, cmp_buf[:, p * PAGE_SIZE:(p + 1) * PAGE_SIZE])
        for l in range(CMP_STRIDE):
            xperm_sc[l, p * seg_pp:(p + 1) * seg_pp, :] = xp[l * seg_pp:(l + 1) * seg_pp, :]
    kc, vc = _compress_grouped(xperm_sc, nseg, wbd_ref, pe_ref, kg0_ref[...])
    kc_b = kc.astype(BF16)
    vc_b = vc.astype(BF16)
    q = q_ref[...]
    qr = qr_ref[...]
    small = small_ref[...]
    tpos_col = past_len + lax.broadcasted_iota(jnp.int32, (tp, 1), 0)
    tpos_rows = jnp.concatenate([tpos_col] * A_HPG, axis=0)
    bp_idx = lax.broadcasted_iota(jnp.int32, (nsb, nsb), 0)
    b_idx = lax.broadcasted_iota(jnp.int32, (nsb, nsb), 1)
    o_cmps = []
    sels = []
    for g in range(A_KV):
        qn_g = _stack_heads(q, g)
        o_cmp, imp = _cmp_branch(qn_g, kc_b, vc_b, tpos_rows, nseg, tp)
        o_cmps.append(o_cmp)
        imp_sel = _dot2_exact_rhs(imp, pool_ref[...])
        imp_pad = jnp.concatenate([imp_sel, jnp.zeros((nsb - tp, nsb), F32)], axis=0)
        imp_t = jnp.transpose(imp_pad)
        rows_sel = []
        for t in range(tp):
            if t < t_valid:
                row_t = imp_sel[t:t + 1, :]
                col_t = imp_t[:, t:t + 1]
                ahead = jnp.where(col_t > row_t, 1.0, jnp.where((col_t == row_t) & (bp_idx < b_idx), 1.0, 0.0))
                rank = jnp.sum(ahead, axis=0, keepdims=True)
                rows_sel.append(jnp.where(rank < (N_SEL - 1), 1.0, 0.0))
            else:
                rows_sel.append(jnp.zeros((1, nsb), F32))
        sels.append(jnp.concatenate(rows_sel, axis=0).astype(BF16))

    @pl.when(b + 1 < nb)
    def _():
        start_all(b + 1, 0)

    wait_all(b, 1)

    new_idx = lax.broadcasted_iota(jnp.int32, (tp, tp), 1)
    tok_idx = lax.broadcasted_iota(jnp.int32, (tp, tp), 0)
    new_ok = jnp.concatenate([jnp.where(new_idx <= tok_idx, 1.0, 0.0)] * A_HPG, axis=0) > 0.5
    wpos = past_len - wbuf + lax.broadcasted_iota(jnp.int32, (1, wbuf), 1)
    wdiff = tpos_col - wpos
    win_ok = jnp.concatenate([jnp.where((wdiff >= 0) & (wdiff < WINDOW), 1.0, 0.0)] * A_HPG, axis=0) > 0.5
    k_past = sel_buf[0:LANES, :].astype(BF16)
    v_past = sel_buf[LANES:2 * LANES, :].astype(BF16)
    k_new = rows_ref[:, 2 * LANES:3 * LANES]
    v_new = rows_ref[:, 3 * LANES:4 * LANES]
    kw_past = winbuf_ref[:, 0:LANES]
    vw_past = winbuf_ref[:, LANES:2 * LANES]
    kw_new = winnew_ref[:, 0:LANES]
    vw_new = winnew_ref[:, LANES:2 * LANES]
    o_groups = []
    for g in range(A_KV):
        qr_g = _stack_heads(qr, g)
        mk = jnp.dot(sels[g], expand_ref[...], preferred_element_type=F32)
        past_ok = jnp.concatenate([mk] * A_HPG, axis=0) > 0.5
        o_sel = _masked_attn_direct(qr_g, [k_past, k_new], [v_past, v_new], [past_ok, new_ok], [True, False])
        o_win = _masked_attn_direct(qr_g, [kw_past, kw_new], [vw_past, vw_new], [win_ok, new_ok], [False, False])
        o_groups.append(_gate_cols(small, g, 0) * o_cmps[g] + _gate_cols(small, g, 1) * o_sel
                        + _gate_cols(small, g, 2) * o_win)
    o_ref[...] = _assemble_heads(o_groups, tp)

    wb = winbuf_ref[...]
    rolled = pltpu.roll(wb, wbuf - t_valid, 0)
    newr = pltpu.roll(winnew_ref[...], tp - t_valid, 0)
    sub = lax.broadcasted_iota(jnp.int32, (tp, 2 * LANES), 0)
    winout_ref[0:wbuf - tp, :] = rolled[0:wbuf - tp, :]
    winout_ref[wbuf - tp:wbuf, :] = jnp.where(sub < tp - t_valid, rolled[wbuf - tp:wbuf, :], newr)


def _nsa_sample(page_table, cache, q, qr, small, rows, winnew, winbuf, wbd, pe, kg0, t_valid):
    nb, n_pages = page_table.shape
    past_len = n_pages * PAGE_SIZE
    nseg = past_len // CMP_STRIDE
    nsb = past_len // SEL_LEN
    tp = SAMPLE_PAD_T
    wbuf = winbuf.shape[1]
    pool = (jnp.arange(nseg)[:, None] // (SEL_LEN // CMP_STRIDE) == jnp.arange(nsb)[None, :]).astype(BF16)
    expand = (jnp.arange(nsb)[:, None] == jnp.arange(past_len)[None, :] // SEL_LEN).astype(BF16)
    tile = lambda b, pt: (b, 0)
    c2 = lambda b, pt: (0, 0)
    c3 = lambda b, pt: (0, 0, 0)
    gs = pltpu.PrefetchScalarGridSpec(
        num_scalar_prefetch=1,
        grid=(nb,),
        in_specs=[pl.BlockSpec(memory_space=pl.ANY),
                  pl.BlockSpec((tp, A_WIDTH), tile), pl.BlockSpec((tp, A_WIDTH), tile),
                  pl.BlockSpec((tp, LANES), tile), pl.BlockSpec((tp, 4 * LANES), tile),
                  pl.BlockSpec((tp, 2 * LANES), tile),
                  pl.BlockSpec((None, wbuf, 2 * LANES), lambda b, pt: (b, 0, 0)),
                  pl.BlockSpec(wbd.shape, c3), pl.BlockSpec(pe.shape, c3), pl.BlockSpec(kg0.shape, c2),
                  pl.BlockSpec(pool.shape, c2), pl.BlockSpec(expand.shape, c2)],
        out_specs=[pl.BlockSpec((tp, A_WIDTH), tile),
                   pl.BlockSpec((None, wbuf, 2 * LANES), lambda b, pt: (b, 0, 0))],
        scratch_shapes=[pltpu.VMEM((2 * LANES, past_len), F32), pltpu.VMEM((2 * LANES, past_len), F32),
                        pltpu.VMEM((CMP_STRIDE, past_len // CMP_STRIDE, 2 * LANES), F32),
                        pltpu.SemaphoreType.DMA((2,))],
    )
    return pl.pallas_call(
        functools.partial(_nsa_sample_kernel, n_pages=n_pages, past_len=past_len, t_valid=t_valid),
        grid_spec=gs,
        out_shape=[jax.ShapeDtypeStruct((nb * tp, A_WIDTH), F32),
                   jax.ShapeDtypeStruct((nb, wbuf, 2 * LANES), F32)],
        compiler_params=_cparams(("arbitrary",)),
        name="nsa_sample",
    )(page_table.reshape(-1), cache, q, qr, small, rows, winnew, winbuf, wbd, pe, kg0, pool, expand)


MOE_TM = 256
SEG_ALIGN = 16
SEG_BITS = (256, 128, 64, 32, 16, 8)
MOE_RL = -(-(MOE_TM * TOP_K + N_EXPERTS * (SEG_ALIGN - 1)) // LANES) * LANES


def _pack_halves(x, bf16_exact=False):
    w = x.shape[1] // 2
    bits = lax.bitcast_convert_type(x if bf16_exact else x.astype(BF16).astype(F32), jnp.uint32)
    return (bits[:, :w] & jnp.uint32(0xFFFF0000)) | (bits[:, w:] >> 16)


def _unpack_halves(u):
    hi = lax.bitcast_convert_type(u & jnp.uint32(0xFFFF0000), F32).astype(BF16)
    lo = lax.bitcast_convert_type(u << 16, F32).astype(BF16)
    return hi, lo


def _route_and_sort(h2, wrt_ref, brt_ref, xsl_ref, info_ref, cnt_ref, tm, t_mod, t_valid, m_valid):
    ne = N_EXPERTS
    h2b = h2.astype(BF16)
    h2l = (h2 - h2b.astype(F32)).astype(BF16)
    wh, wl = _split(wrt_ref[...])
    lt = _bdot_t(wh, h2b) + _bdot_t(wl, h2b) + _bdot_t(wh, h2l) + brt_ref[...]
    eidx = lax.broadcasted_iota(jnp.int32, (ne, tm), 0)
    rank = jnp.zeros((ne, tm), F32)
    for ep in range(ne):
        v = lt[ep:ep + 1, :]
        rank = rank + jnp.where(v > lt, 1.0, jnp.where((v == lt) & (eidx > ep), 1.0, 0.0))
    sel = rank < TOP_K
    if t_mod is not None:
        tok = pl.program_id(0) * tm + lax.broadcasted_iota(jnp.int32, (1, tm), 1)
        sel = sel & ((tok % t_mod) < t_valid) & (tok < m_valid)
    mx = jnp.max(jnp.where(sel, lt, NEG_BIG), axis=0, keepdims=True)
    ex = jnp.where(sel, jnp.exp(lt - mx), 0.0)
    den = jnp.sum(ex, axis=0, keepdims=True)
    gate = ex / jnp.where(den > 0, den, 1.0)
    self_ = jnp.where(sel, 1.0, 0.0)
    selb = self_.astype(BF16)
    er = lax.broadcasted_iota(jnp.int32, (ne, ne), 0)
    ec = lax.broadcasted_iota(jnp.int32, (ne, ne), 1)
    c = jnp.dot(jnp.where(ec <= er, 1.0, 0.0).astype(BF16), selb, preferred_element_type=F32)
    tr = lax.broadcasted_iota(jnp.int32, (tm, tm), 0)
    tc = lax.broadcasted_iota(jnp.int32, (tm, tm), 1)
    rk = jnp.dot(selb, jnp.where(tr < tc, 1.0, 0.0).astype(BF16), preferred_element_type=F32)
    cnt = jnp.sum(self_, axis=1, keepdims=True)
    cnt_al = jnp.floor((cnt + (SEG_ALIGN - 1)) * (1.0 / SEG_ALIGN)) * SEG_ALIGN
    cnt_b = jnp.broadcast_to(cnt_al, (ne, LANES))
    cnt_ref[...] = cnt_b
    off = jnp.dot(jnp.where(ec < er, 1.0, 0.0).astype(BF16), cnt_b.astype(BF16), preferred_element_type=F32)
    rowidx = off[:, 0:1] + rk
    rows_k, gates_k, exps_k = [], [], []
    for k in range(1, TOP_K + 1):
        mk = sel & (c == k)
        has = jnp.sum(jnp.where(mk, 1.0, 0.0), axis=0, keepdims=True)
        rows_k.append(jnp.sum(jnp.where(mk, rowidx, 0.0), axis=0, keepdims=True) + has - 1.0)
        gates_k.append(jnp.sum(jnp.where(mk, gate, 0.0), axis=0, keepdims=True))
        exps_k.append(jnp.sum(jnp.where(mk, eidx.astype(F32), 0.0), axis=0, keepdims=True))
    info_ref[...] = jnp.concatenate(rows_k + gates_k + exps_k + [jnp.zeros((4, tm), F32)], axis=0)
    ridx = lax.broadcasted_iota(jnp.int32, (MOE_RL, tm), 0).astype(F32)
    perm = jnp.zeros((MOE_RL, tm), F32)
    for k in range(TOP_K):
        perm = perm + jnp.where(ridx == rows_k[k], 1.0, 0.0)
    xs = jnp.dot(perm.astype(BF16), h2b, preferred_element_type=F32)
    xsl_ref[...] = _pack_halves(xs, bf16_exact=True)


def _mixout_kernel(x_ref, hm_ref, on_ref, mod_ref, gmix_ref, gffn_ref,
                   wog_ref, bog_ref, wum_ref, wua_ref, wout_ref, wrt_ref, brt_ref,
                   x1_ref, xsl_ref, info_ref, cnt_ref, *, tm, t_mod, t_valid, m_valid, n_real):
    if n_real is not None:
        @pl.when(pl.program_id(0) >= n_real)
        def _():
            xsl_ref[...] = jnp.zeros(xsl_ref.shape, jnp.uint32)
            info_ref[...] = jnp.zeros(info_ref.shape, F32)
            cnt_ref[...] = jnp.zeros(cnt_ref.shape, F32)

        @pl.when(pl.program_id(0) < n_real)
        def _():
            _mixout_body(x_ref, hm_ref, on_ref, mod_ref, gmix_ref, gffn_ref, wog_ref, bog_ref, wum_ref,
                         wua_ref, wout_ref, wrt_ref, brt_ref, x1_ref, xsl_ref, info_ref, cnt_ref,
                         tm, t_mod, t_valid, m_valid)
    else:
        _mixout_body(x_ref, hm_ref, on_ref, mod_ref, gmix_ref, gffn_ref, wog_ref, bog_ref, wum_ref,
                     wua_ref, wout_ref, wrt_ref, brt_ref, x1_ref, xsl_ref, info_ref, cnt_ref,
                     tm, t_mod, t_valid, m_valid)


def _mixout_body(x_ref, hm_ref, on_ref, mod_ref, gmix_ref, gffn_ref,
                 wog_ref, bog_ref, wum_ref, wua_ref, wout_ref, wrt_ref, brt_ref,
                 x1_ref, xsl_ref, info_ref, cnt_ref, tm, t_mod, t_valid, m_valid):
    d = D_MODEL
    x = x_ref[...]
    sh1, sc1, gt1 = mod_ref[:, 0:d], mod_ref[:, d:2 * d], mod_ref[:, 2 * d:3 * d]
    sh2, sc2 = mod_ref[:, 3 * d:4 * d], mod_ref[:, 4 * d:5 * d]
    h = _rmsnorm_rows(x, gmix_ref[...]) * (1.0 + sc1) + sh1
    hb = h.astype(BF16)
    mo = jnp.dot(hb, wog_ref[:, 0:M_WIDTH], preferred_element_type=F32) + bog_ref[:, 0:M_WIDTH]
    ym = _bdot(_sigmoid(mo) * hm_ref[...], wum_ref[...])
    ya = _bdot(on_ref[...], wua_ref[...])
    ga = jnp.dot(hb, wog_ref[:, M_WIDTH:M_WIDTH + d], preferred_element_type=F32) + bog_ref[:, M_WIDTH:M_WIDTH + d]
    u = _sigmoid(ga) * ym
    gb = (jnp.dot(hb, wog_ref[:, M_WIDTH + d:M_WIDTH + 2 * d], preferred_element_type=F32)
          + bog_ref[:, M_WIDTH + d:M_WIDTH + 2 * d])
    u = u + _sigmoid(gb) * ya
    x1 = x + gt1 * _bdot(u, wout_ref[...])
    x1_ref[...] = x1
    h2 = _rmsnorm_rows(x1, gffn_ref[...]) * (1.0 + sc2) + sh2
    _route_and_sort(h2, wrt_ref, brt_ref, xsl_ref, info_ref, cnt_ref, tm, t_mod, t_valid, m_valid)


def _mixout_with_shared(*refs, n_shared, **kw):
    n_in = 13
    _mixout_kernel(*refs[:n_in], *refs[n_in + n_shared:], **kw)


def _mixout(x2, hm, on, mod3, gmix, gffn, wts, tiles_per_mod, nt_total, tile0=0, shared=None,
            t_mod=None, t_valid=None, m_valid=None):
    m = x2.shape[0]
    tm = MOE_TM
    nt = m // tm
    (wog, bog, wum, wua, wout, wr, br) = wts
    r = mod3.shape[1]
    n_extra = nt_total - tile0 - nt if shared is None else 0
    row = lambda i: (jnp.minimum(i, nt - 1), 0)
    const = lambda i: (0, 0)
    in_specs = [pl.BlockSpec((tm, D_MODEL), row), pl.BlockSpec((tm, M_WIDTH), row),
                pl.BlockSpec((tm, A_WIDTH), row),
                pl.BlockSpec((None, r, 6 * D_MODEL), lambda i: (jnp.minimum(i, nt - 1) // tiles_per_mod, 0, 0)),
                pl.BlockSpec((1, D_MODEL), const), pl.BlockSpec((1, D_MODEL), const),
                pl.BlockSpec(wog.shape, const), pl.BlockSpec(bog.shape, const),
                pl.BlockSpec(wum.shape, const), pl.BlockSpec(wua.shape, const),
                pl.BlockSpec(wout.shape, const), pl.BlockSpec(wr.shape, const),
                pl.BlockSpec(br.shape, const)]
    args = [x2, hm, on, mod3, gmix, gffn, wog, bog, wum, wua, wout, wr, br]
    kw = dict(tm=tm, t_mod=t_mod, t_valid=t_valid, m_valid=m_valid, n_real=nt if n_extra else None)
    body = functools.partial(_mixout_kernel, **kw)
    aliases = {}
    if shared is not None:
        in_specs += [pl.BlockSpec(memory_space=pl.ANY)] * len(shared)
        aliases = {len(args) + j: 1 + j for j in range(len(shared))}
        args += list(shared)
        body = functools.partial(_mixout_with_shared, n_shared=len(shared), **kw)
    return pl.pallas_call(
        body,
        grid=(nt + n_extra,),
        in_specs=in_specs,
        out_specs=[pl.BlockSpec((tm, D_MODEL), row),
                   pl.BlockSpec((MOE_RL, D_MODEL // 2), lambda i: (tile0 + i, 0)),
                   pl.BlockSpec((16, tm), lambda i: (0, tile0 + i)),
                   pl.BlockSpec((None, N_EXPERTS, LANES), lambda i: (tile0 + i, 0, 0))],
        out_shape=[jax.ShapeDtypeStruct((m, D_MODEL), F32),
                   jax.ShapeDtypeStruct((nt_total * MOE_RL, D_MODEL // 2), jnp.uint32),
                   jax.ShapeDtypeStruct((16, nt_total * tm), F32),
                   jax.ShapeDtypeStruct((nt_total, N_EXPERTS, LANES), F32)],
        input_output_aliases=aliases,
        compiler_params=_cparams(("arbitrary" if n_extra else "parallel",)),
        name="mixout",
    )(*args)


MOE_BM = 256
MOE_CH = 512


def _moe_kernel(be_ref, na_ref, grp_ref,
                xsl_ref, wgu_ref, bgu_ref, wdn_ref, bdn_ref, ysl_ref,
                wgu_bf, wdn_bf, xbuf, ybuf, sem_in, sem_out, *, trash_row0):
    i = pl.program_id(0)
    na = na_ref[0]
    e = be_ref[i]
    prev = be_ref[jnp.maximum(i - 1, 0)]
    n_grp = MOE_BM // SEG_ALIGN

    def group_copies(blk, inbound, slot=None):
        slot = blk % 2 if slot is None else slot
        cps = []
        for r in range(n_grp):
            v = grp_ref[blk * n_grp + r]
            vm_rows = pl.ds(r * SEG_ALIGN, SEG_ALIGN)
            if inbound:
                row = pl.multiple_of(jnp.where(v >= 0, v, trash_row0 + 2 * MOE_BM), SEG_ALIGN)
                cps.append(pltpu.make_async_copy(xsl_ref.at[pl.ds(row, SEG_ALIGN), :],
                                                 xbuf.at[slot, vm_rows, :], sem_in.at[slot]))
            else:
                spare = trash_row0 + slot * MOE_BM + r * SEG_ALIGN
                row = pl.multiple_of(jnp.where(v >= 0, v, spare), SEG_ALIGN)
                cps.append(pltpu.make_async_copy(ybuf.at[slot, vm_rows, :],
                                                 ysl_ref.at[pl.ds(row, SEG_ALIGN), :], sem_out.at[slot]))
        return cps

    def start_gather(blk):
        for cp in group_copies(blk, True):
            cp.start()

    def start_scatter(blk):
        for cp in group_copies(blk, False):
            cp.start()

    def wait_rows(blk, sem, inbound):
        slot = blk % 2
        if inbound:
            pltpu.make_async_copy(xsl_ref.at[pl.ds(0, MOE_BM), :], xbuf.at[slot], sem.at[slot]).wait()
        else:
            pltpu.make_async_copy(ybuf.at[slot], ysl_ref.at[pl.ds(0, MOE_BM), :], sem.at[slot]).wait()

    @pl.when(i == 0)
    def _():
        start_gather(i)

    @pl.when(i + 1 < na)
    def _():
        start_gather(i + 1)

    @pl.when((i < na) & ((i == 0) | (e != prev)))
    def _():
        for j in range(2 * D_EXPERT // MOE_CH):
            wgu_bf[:, j * MOE_CH:(j + 1) * MOE_CH] = wgu_ref[:, j * MOE_CH:(j + 1) * MOE_CH].astype(BF16)
        for j in range(D_EXPERT // MOE_CH):
            wdn_bf[j * MOE_CH:(j + 1) * MOE_CH, :] = wdn_ref[j * MOE_CH:(j + 1) * MOE_CH, :].astype(BF16)

    @pl.when(i < na)
    def _():
        slot = i % 2
        wait_rows(i, sem_in, True)

        @pl.when(i >= 2)
        def _():
            wait_rows(i - 2, sem_out, False)

        half = D_MODEL // 2
        xh, xl = _unpack_halves(xbuf[slot])

        def xdot(c0, c1):
            return (jnp.dot(xh, wgu_bf[0:half, c0:c1], preferred_element_type=F32)
                    + jnp.dot(xl, wgu_bf[half:D_MODEL, c0:c1], preferred_element_type=F32))

        acc = jnp.zeros((MOE_BM, D_MODEL), F32) + bdn_ref[...]
        for j in range(D_EXPERT // MOE_CH):
            lo, hi = j * MOE_CH, (j + 1) * MOE_CH
            gj = xdot(lo, hi) + bgu_ref[:, lo:hi]
            uj = xdot(D_EXPERT + lo, D_EXPERT + hi) + bgu_ref[:, D_EXPERT + lo:D_EXPERT + hi]
            gj = jnp.minimum(gj, SWIGLU_LIMIT)
            uj = jnp.clip(uj, -SWIGLU_LIMIT, SWIGLU_LIMIT)
            act = gj * _sigmoid(SWIGLU_ALPHA * gj) * (uj + 1.0)
            acc = acc + jnp.dot(act.astype(BF16), wdn_bf[lo:hi, :], preferred_element_type=F32)
        ybuf[slot] = _pack_halves(acc)
        start_scatter(i)

        @pl.when(i == na - 1)
        def _():
            @pl.when(i >= 1)
            def _():
                wait_rows(i - 1, sem_out, False)
            wait_rows(i, sem_out, False)


def _moe_experts(plan, xsl, w_gu, b_gu, w_dn, b_dn):
    block_e, n_active, grp_rows = plan
    nblk = block_e.shape[0]
    spare_row0 = xsl.shape[0] - MOE_RL
    assert MOE_RL >= 2 * MOE_BM
    wmap = lambda i, be, *_: (be[i], 0, 0)
    anyspec = pl.BlockSpec(memory_space=pl.ANY)
    gs = pltpu.PrefetchScalarGridSpec(
        num_scalar_prefetch=3,
        grid=(nblk,),
        in_specs=[anyspec,
                  pl.BlockSpec((None, D_MODEL, 2 * D_EXPERT), wmap),
                  pl.BlockSpec((None, 1, 2 * D_EXPERT), wmap),
                  pl.BlockSpec((None, D_EXPERT, D_MODEL), wmap),
                  pl.BlockSpec((None, 1, D_MODEL), wmap)],
        out_specs=anyspec,
        scratch_shapes=[pltpu.VMEM((D_MODEL, 2 * D_EXPERT), BF16), pltpu.VMEM((D_EXPERT, D_MODEL), BF16),
                        pltpu.VMEM((2, MOE_BM, D_MODEL // 2), jnp.uint32),
                        pltpu.VMEM((2, MOE_BM, D_MODEL // 2), jnp.uint32),
                        pltpu.SemaphoreType.DMA((2,)), pltpu.SemaphoreType.DMA((2,))],
    )
    return pl.pallas_call(
        functools.partial(_moe_kernel, trash_row0=spare_row0),
        grid_spec=gs,
        out_shape=jax.ShapeDtypeStruct(xsl.shape, jnp.uint32),
        input_output_aliases={3: 0},
        compiler_params=_cparams(("arbitrary",)),
        name="moe_experts",
    )(*plan, xsl, w_gu, b_gu.reshape(N_EXPERTS, 1, -1), w_dn, b_dn.reshape(N_EXPERTS, 1, -1))


def _combine_kernel(ysl_ref, info_ref, x1_ref, mod_ref, y_ref, *, tm):
    info = info_ref[...]
    info_t = jnp.transpose(jnp.concatenate([info, jnp.zeros((LANES - info.shape[0], tm), F32)], axis=0))
    ridx = lax.broadcasted_iota(jnp.int32, (tm, MOE_RL), 1).astype(F32)
    pg = jnp.zeros((tm, MOE_RL), F32)
    for k in range(TOP_K):
        pg = pg + jnp.where(ridx == info_t[:, k:k + 1], info_t[:, TOP_K + k:TOP_K + k + 1], 0.0)
    pgb = pg.astype(BF16)
    yh, yl = _unpack_halves(ysl_ref[...])
    half = D_MODEL // 2
    gt2 = mod_ref[:, 5 * D_MODEL:6 * D_MODEL]
    for c, yy in ((0, yh), (1, yl)):
        moe = jnp.dot(pgb, yy, preferred_element_type=F32)
        y_ref[:, c * half:(c + 1) * half] = (x1_ref[:, c * half:(c + 1) * half]
                                             + gt2[:, c * half:(c + 1) * half] * moe)


def _combine(ysl, info, x1, mod3, tiles_per_mod, tile0=0):
    m = x1.shape[0]
    tm = MOE_TM
    r = mod3.shape[1]
    return pl.pallas_call(
        functools.partial(_combine_kernel, tm=tm),
        grid=(m // tm,),
        in_specs=[pl.BlockSpec((MOE_RL, D_MODEL // 2), lambda i: (tile0 + i, 0)),
                  pl.BlockSpec((16, tm), lambda i: (0, tile0 + i)),
                  pl.BlockSpec((tm, D_MODEL), lambda i: (i, 0)),
                  pl.BlockSpec((None, r, 6 * D_MODEL), lambda i: (i // tiles_per_mod, 0, 0))],
        out_specs=pl.BlockSpec((tm, D_MODEL), lambda i: (i, 0)),
        out_shape=jax.ShapeDtypeStruct((m, D_MODEL), F32),
        compiler_params=_cparams(("parallel",)),
        name="moe_combine",
    )(ysl, info, x1, mod3)


def _moe_plan(cnt):
    cnt = cnt.astype(jnp.int32)
    nt = cnt.shape[0]
    so = jnp.cumsum(cnt, axis=1) - cnt + (jnp.arange(nt) * MOE_RL)[:, None]
    ce = jnp.cumsum(cnt, axis=0)
    cs = ce - cnt
    tot = ce[-1]
    nblk_e = (tot + MOE_BM - 1) // MOE_BM
    blk_end = jnp.cumsum(nblk_e)
    max_rows = nt * MOE_TM * TOP_K + nt * N_EXPERTS * (SEG_ALIGN - 1)
    n_blocks = -(-max_rows // MOE_BM) + N_EXPERTS
    bidx = jnp.arange(n_blocks)
    block_e = jnp.minimum(jnp.sum(blk_end[None, :] <= bidx[:, None], axis=1), N_EXPERTS - 1).astype(jnp.int32)
    is_e = (jnp.arange(N_EXPERTS)[:, None] == block_e[None, :]).astype(jnp.int32)
    per_block = lambda a: jnp.sum(a[..., :, None] * is_e, axis=-2)
    block_r0 = (bidx - per_block(blk_end - nblk_e)) * MOE_BM
    x = block_r0[:, None] + jnp.arange(MOE_BM // SEG_ALIGN)[None, :] * SEG_ALIGN
    ce_b = per_block(ce)[:, :, None]
    cs_b = per_block(cs)[:, :, None]
    inside = (cs_b <= x[None]) & (x[None] < ce_b)
    grp = x + jnp.sum(jnp.where(inside, per_block(so - cs)[:, :, None], 0), axis=0)
    grp = jnp.where(x < per_block(tot)[:, None], grp, -1)
    n_active = blk_end[-1].reshape(1)
    i32 = lambda a: a.reshape(-1).astype(jnp.int32)
    return block_e, i32(n_active), i32(grp)


def _rope_tables(pos):
    half = ROT_DIM // 2
    inv = ROPE_THETA ** (-jnp.arange(half, dtype=F32) * (2.0 / ROT_DIM))
    ang = pos.astype(F32)[:, None] * inv[None, :]
    cos, sin = jnp.cos(ang), jnp.sin(ang)
    n = pos.shape[0]
    ones = jnp.ones((n, A_DH - ROT_DIM), F32)
    zeros_h = jnp.zeros((n, half), F32)
    zeros_r = jnp.zeros((n, A_DH - ROT_DIM), F32)
    cos64 = jnp.concatenate([cos, cos, ones], axis=1)
    sprev64 = jnp.concatenate([zeros_h, sin, zeros_r], axis=1)
    snext64 = jnp.concatenate([-sin, zeros_h, zeros_r], axis=1)
    two = lambda a: jnp.concatenate([a, a], axis=1)
    return two(cos64), two(sprev64), two(snext64)


def _prep_weights(w_in, b_in, q_norm_g, k_norm_g, cmp_pe_k, cmp_pe_v, cmp_w_k, cmp_w_v,
                  w_up_m, w_up_a, w_out, w_router, b_router):
    b2 = b_in.reshape(1, N_IN)
    wm = w_in[:, OFF_MQ:OFF_MO].astype(BF16)
    bm = b2[:, OFF_MQ:OFF_MO]
    wq = w_in[:, OFF_AQ:OFF_AKV].astype(BF16)
    bq = b2[:, OFF_AQ:OFF_AKV]
    wkv = w_in[:, OFF_AKV:OFF_AG].astype(BF16)
    bkv = b2[:, OFF_AKV:OFF_AG]
    n_small = 2 * M_HEADS + 3 * A_HEADS
    ws = jnp.concatenate([w_in[:, OFF_MI:OFF_AQ], w_in[:, OFF_AG:OFF_GA],
                          jnp.zeros((D_MODEL, LANES - n_small), F32)], axis=1)
    bs = jnp.concatenate([b2[:, OFF_MI:OFF_AQ], b2[:, OFF_AG:OFF_GA], jnp.zeros((1, LANES - n_small), F32)], axis=1)
    qg = jnp.tile(q_norm_g, A_HEADS).reshape(1, A_WIDTH)
    kg = jnp.stack([jnp.tile(k_norm_g[1], A_KV), jnp.tile(k_norm_g[2], A_KV)], axis=0)
    kg0 = jnp.tile(k_norm_g[0], A_KV).reshape(1, LANES)
    hid = jnp.arange(A_WIDTH) // A_DH
    bd = jnp.where(hid[:, None] == hid[None, :], 1.0 / A_DH, 0.0).astype(BF16)
    inproj_w = (wm, bm, wq, bq, wkv, bkv, ws, bs, qg, kg, bd)

    z = jnp.zeros((CMP_LEN, A_DH, A_DH), F32)
    r0 = jnp.concatenate([cmp_w_k, z, z, z], axis=2)
    r1 = jnp.concatenate([z, cmp_w_k, z, z], axis=2)
    r2 = jnp.concatenate([z, z, cmp_w_v, z], axis=2)
    r3 = jnp.concatenate([z, z, z, cmp_w_v], axis=2)
    wbd = jnp.concatenate([r0, r1, r2, r3], axis=1).astype(BF16)
    pe = jnp.concatenate([cmp_pe_k, cmp_pe_k, cmp_pe_v, cmp_pe_v], axis=1).reshape(CMP_LEN, 1, 2 * LANES)

    wog = jnp.concatenate([w_in[:, OFF_MO:OFF_MI], w_in[:, OFF_GA:N_IN]], axis=1).astype(BF16)
    bog = jnp.concatenate([b2[:, OFF_MO:OFF_MI], b2[:, OFF_GA:N_IN]], axis=1)
    mixout_w = (wog, bog, w_up_m.astype(BF16), w_up_a.astype(BF16), w_out.astype(BF16),
                w_router.T, b_router.reshape(N_EXPERTS, 1))
    return inproj_w, (wbd, pe, kg0), mixout_w


def _pick_tile(m, pref):
    t = pref
    while m % t:
        t //= 2
    return t


def kernel(x_prompt, x_sample, cache_nsa_kv, state_win_kv, state_mlstm_C, state_mlstm_n, state_mlstm_m, page_table, c_prompt, c_sample, w_ada, b_ada, g_mix, g_ffn, w_in, b_in, q_norm_g, k_norm_g, cmp_pe_k, cmp_pe_v, cmp_w_k, cmp_w_v, w_up_m, w_up_a, w_out, w_router, b_router, w_gu, b_gu, w_dn, b_dn):
    depth = w_in.shape[0]
    assert depth == 1
    B, T, D = x_prompt.shape
    DB, TS, _ = x_sample.shape
    n_pages = page_table.shape[1]
    past_len = n_pages * PAGE_SIZE
    wbuf = state_win_kv.shape[2]
    tp = SAMPLE_PAD_T
    assert TS <= tp and wbuf % tp == 0 and T % 128 == 0

    l = 0
    inproj_w, cmp_w, mixout_w = _prep_weights(
        w_in[l], b_in[l], q_norm_g[l], k_norm_g[l], cmp_pe_k[l], cmp_pe_v[l], cmp_w_k[l], cmp_w_v[l],
        w_up_m[l], w_up_a[l], w_out[l], w_router[l], b_router[l])
    wbd, pe, kg0 = cmp_w
    gmix = g_mix[l].reshape(1, D)
    gffn = g_ffn[l].reshape(1, D)

    nc = B + DB
    nc_pad = -(-nc // SUBLANES) * SUBLANES
    c_all = jnp.concatenate([c_prompt, c_sample, jnp.zeros((nc_pad - nc, D), F32)], axis=0)
    mod = _adaln(c_all, w_ada[l], b_ada[l])
    mod_p = mod[:B].reshape(B, 1, 6 * D)
    mod_s = jnp.repeat(mod[B:B + DB], tp, axis=0).reshape(1, DB * tp, 6 * D)

    mp = B * T
    tm = _pick_tile(T, 256)
    xp = x_prompt.reshape(mp, D)
    tabs_p = _rope_tables(jnp.arange(T, dtype=jnp.int32))
    mq, mk, mv, q, qr, rows, win, small, rows_t = _inproj(xp, mod_p, gmix, tabs_p, inproj_w, tm, T // tm, T // tm,
                                                          rows_t_batches=B)
    Lp = _pick_tile(T, 128)
    hm, C_p, n_p, m_p = _mlstm(mq, mk, mv, small, B, T, T, Lp)
    o_nsa = _nsa_prompt(q, qr, small, rows, win, wbd, pe, kg0, B, T)
    assert T % MOE_TM == 0
    ms_pad = -(-(DB * tp) // MOE_TM) * MOE_TM
    nt_p = mp // MOE_TM
    nt_all = nt_p + ms_pad // MOE_TM + 1
    x1_p, xsl, info, cnt = _mixout(xp, hm, o_nsa, mod_p, gmix, gffn, mixout_w, T // MOE_TM, nt_all)

    ms = DB * tp
    xs_pad = jnp.concatenate([x_sample, jnp.zeros((DB, tp - TS, D), F32)], axis=1).reshape(ms, D)
    pos_s = past_len + jnp.tile(jnp.arange(tp, dtype=jnp.int32), DB)
    tabs_s = _rope_tables(pos_s)
    mq_s, mk_s, mv_s, q_s, qr_s, rows_s, win_s, small_s = _inproj(xs_pad, mod_s, gmix, tabs_s, inproj_w, ms, 1, 1)
    hm_s, C_s, n_s, m_s = _mlstm(mq_s, mk_s, mv_s, small_s, DB, tp, TS, tp,
                                 state=(state_mlstm_C[l], state_mlstm_n[l], state_mlstm_m[l]))
    cache2 = jnp.transpose(cache_nsa_kv[l], (0, 2, 3, 4, 1)).reshape(cache_nsa_kv.shape[1], 4 * LANES, PAGE_SIZE)
    winbuf = state_win_kv[l].reshape(DB, wbuf, 2 * LANES)
    o_nsa_s, win_out_s = _nsa_sample(page_table, cache2, q_s, qr_s, small_s, rows_s, win_s, winbuf,
                                     wbd, pe, kg0, TS)
    assert ms_pad == MOE_TM
    rpad = lambda a: jnp.concatenate([a, jnp.zeros((ms_pad - ms, a.shape[1]), a.dtype)], axis=0) if ms_pad > ms else a
    mod_sp = rpad(mod_s[0])[None]
    x1_s, xsl, info, cnt = _mixout(rpad(xs_pad), rpad(hm_s), rpad(o_nsa_s), mod_sp, gmix, gffn, mixout_w,
                                   1, nt_all, tile0=nt_p, shared=(xsl, info, cnt),
                                   t_mod=tp, t_valid=TS, m_valid=ms)

    ysl = _moe_experts(_moe_plan(cnt[:, :, 0]), xsl, w_gu[l], b_gu[l], w_dn[l], b_dn[l])
    y_p = _combine(ysl, info, x1_p, mod_p, T // MOE_TM).reshape(B, T, D)
    y_s_all = _combine(ysl, info, x1_s, mod_sp, 1, tile0=nt_p)
    valid = lambda a: a.reshape(DB, tp, -1)[:, :TS].reshape(DB * TS, -1)
    y_s = valid(y_s_all[:ms]).reshape(DB, TS, D)

    kv_p = jnp.transpose(rows_t.reshape(B, 4, A_KV, A_DH, T), (0, 4, 1, 2, 3))[None]
    kv_s = valid(rows_s).reshape(1, DB, TS, 4, A_KV, A_DH)
    wp = min(WINDOW, T)
    win_p = win.reshape(B, T, 2, A_KV, A_DH)[:, T - wp:][None]
    win_s_out = win_out_s.reshape(1, DB, wbuf, 2, A_KV, A_DH)
    return (y_p, y_s, kv_p, kv_s, win_p, win_s_out,
            C_p[None], n_p[None], m_p[None], C_s[None], n_s[None], m_s[None])
```

```python
import functools
import math

import jax
import jax.numpy as jnp
from jax import lax
from jax.experimental import pallas as pl
from jax.experimental.pallas import tpu as pltpu

F32 = jnp.float32
BF16 = jnp.bfloat16

D_MODEL = 1024
M_HEADS = 4
M_DH = 128
M_WIDTH = M_HEADS * M_DH
A_HEADS = 8
A_KV = 2
A_HPG = A_HEADS // A_KV
A_DH = 64
A_WIDTH = A_HEADS * A_DH
CMP_STRIDE = 16
CMP_LEN = 32
SEL_LEN = 64
N_SEL = 16
WINDOW = 512
PAGE_SIZE = 128
ROPE_THETA = 500000.0
ROT_DIM = A_DH // 4
ATT_SCALE = A_DH ** -0.5
N_EXPERTS = 32
TOP_K = 4
D_EXPERT = D_MODEL
SWIGLU_LIMIT = 7.0
SWIGLU_ALPHA = 1.702
EPS = 1e-6

OFF_MQ, OFF_MK, OFF_MV, OFF_MO = 0, M_WIDTH, 2 * M_WIDTH, 3 * M_WIDTH
OFF_MI = 4 * M_WIDTH
OFF_MF = OFF_MI + M_HEADS
OFF_AQ = OFF_MF + M_HEADS
OFF_AKV = OFF_AQ + A_WIDTH
OFF_AG = OFF_AKV + 6 * A_KV * A_DH
OFF_GA = OFF_AG + 3 * A_HEADS
OFF_GB = OFF_GA + D_MODEL
N_IN = OFF_GB + D_MODEL

LANES = 128
SUBLANES = 8
VMEM_LIMIT = 56 * 1024 * 1024

NEG_BIG = -1e30
M_INIT = -1e29
LOG2E = 1.4426950408889634
SAMPLE_PAD_T = 8


def _cparams(sem):
    return pltpu.CompilerParams(dimension_semantics=sem, vmem_limit_bytes=VMEM_LIMIT)


def _bdot(a, b):
    return jnp.dot(a.astype(BF16), b.astype(BF16), preferred_element_type=F32)


def _bdot_t(a, b):
    return lax.dot_general(a.astype(BF16), b.astype(BF16), (((1,), (1,)), ((), ())),
                           preferred_element_type=F32)


def _split(a):
    hi = a.astype(BF16)
    lo = (a - hi.astype(F32)).astype(BF16)
    return hi, lo


def _dot3(a, b):
    ah, al = _split(a)
    bh, bl = _split(b)
    return (jnp.dot(ah, bh, preferred_element_type=F32) + jnp.dot(al, bh, preferred_element_type=F32)
            + jnp.dot(ah, bl, preferred_element_type=F32))


def _dot2_exact_rhs(a, b_bf16):
    ah, al = _split(a)
    return jnp.dot(ah, b_bf16, preferred_element_type=F32) + jnp.dot(al, b_bf16, preferred_element_type=F32)


def _sigmoid(x):
    return 0.5 * jnp.tanh(0.5 * x) + 0.5


def _rmsnorm_rows(x, g):
    return x * lax.rsqrt(jnp.mean(x * x, axis=-1, keepdims=True) + EPS) * g


def _adaln_kernel(c_ref, w_ref, b_ref, o_ref):
    c = c_ref[...]
    s = c * _sigmoid(c)
    o_ref[...] = _dot3(s, w_ref[...]) + b_ref[...]


def _adaln(c, w, b):
    mc, d = c.shape
    n = w.shape[1]
    tn = 1024
    return pl.pallas_call(
        _adaln_kernel,
        grid=(n // tn,),
        in_specs=[pl.BlockSpec((mc, d), lambda j: (0, 0)),
                  pl.BlockSpec((d, tn), lambda j: (0, j)),
                  pl.BlockSpec((1, tn), lambda j: (0, j))],
        out_specs=pl.BlockSpec((mc, tn), lambda j: (0, j)),
        out_shape=jax.ShapeDtypeStruct((mc, n), F32),
        compiler_params=_cparams(("parallel",)),
        name="adaln",
    )(c, w, b.reshape(1, n))


def _head_norm(z, bd, gain):
    ms = _dot2_exact_rhs(z * z, bd)
    return z * lax.rsqrt(ms + EPS) * gain


def _rope(z, cos, s_prev, s_next):
    w = z.shape[1]
    rep = w // LANES
    if rep > 1:
        cos = jnp.concatenate([cos] * rep, axis=1)
        s_prev = jnp.concatenate([s_prev] * rep, axis=1)
        s_next = jnp.concatenate([s_next] * rep, axis=1)
    z_prev = pltpu.roll(z, ROT_DIM // 2, 1)
    z_next = pltpu.roll(z, w - ROT_DIM // 2, 1)
    return z * cos + z_prev * s_prev + z_next * s_next


def _inproj_kernel(x_ref, mod_ref, gmix_ref, cos_ref, sp_ref, sn_ref,
                   wm_ref, bm_ref, wq_ref, bq_ref, wkv_ref, bkv_ref, ws_ref, bs_ref,
                   qg_ref, kg_ref, bd_ref,
                   mq_ref, mk_ref, mv_ref, q_ref, qr_ref, rows_ref, win_ref, small_ref, rows_t_ref=None):
    x = x_ref[...]
    sh1 = mod_ref[:, 0:D_MODEL]
    sc1 = mod_ref[:, D_MODEL:2 * D_MODEL]
    h = _rmsnorm_rows(x, gmix_ref[...]) * (1.0 + sc1) + sh1
    hb = h.astype(BF16)

    mq_ref[...] = jnp.dot(hb, wm_ref[:, 0:M_WIDTH], preferred_element_type=F32) + bm_ref[:, 0:M_WIDTH]
    mk = jnp.dot(hb, wm_ref[:, M_WIDTH:2 * M_WIDTH], preferred_element_type=F32) + bm_ref[:, M_WIDTH:2 * M_WIDTH]
    mk_ref[...] = mk * (M_DH ** -0.5)
    mv_ref[...] = (jnp.dot(hb, wm_ref[:, 2 * M_WIDTH:3 * M_WIDTH], preferred_element_type=F32)
                   + bm_ref[:, 2 * M_WIDTH:3 * M_WIDTH])

    cos, sp, sn = cos_ref[...], sp_ref[...], sn_ref[...]
    zq = jnp.dot(hb, wq_ref[...], preferred_element_type=F32) + bq_ref[...]
    qn = _head_norm(zq, bd_ref[...], qg_ref[...])
    q_ref[...] = qn
    qr_ref[...] = _rope(qn, cos, sp, sn)

    zkv = jnp.dot(hb, wkv_ref[...], preferred_element_type=F32) + bkv_ref[...]
    bd2 = bd_ref[0:LANES, 0:LANES]
    ksel = _head_norm(zkv[:, 2 * LANES:3 * LANES], bd2, kg_ref[0:1, :])
    rows = jnp.concatenate([zkv[:, 0:2 * LANES], _rope(ksel, cos, sp, sn), zkv[:, 3 * LANES:4 * LANES]], axis=1)
    rows_ref[...] = rows
    if rows_t_ref is not None:
        rows_t_ref[...] = jnp.transpose(rows)
    kwin = _head_norm(zkv[:, 4 * LANES:5 * LANES], bd2, kg_ref[1:2, :])
    win_ref[:, 0:LANES] = _rope(kwin, cos, sp, sn)
    win_ref[:, LANES:2 * LANES] = zkv[:, 5 * LANES:6 * LANES]

    small_ref[...] = _dot3(h, ws_ref[...]) + bs_ref[...]


def _inproj(x2, mod3, gmix, tabs, wts, tm, tiles_per_mod, pos_tiles, rows_t_batches=None):
    m = x2.shape[0]
    cos_t, sp_t, sn_t = tabs
    (wm, bm, wq, bq, wkv, bkv, ws, bs, qg, kg, bd) = wts
    r = mod3.shape[1]
    row = lambda i: (i, 0)
    const = lambda i: (0, 0)
    tab = lambda i: (i % pos_tiles, 0)
    in_specs = [
        pl.BlockSpec((tm, D_MODEL), row),
        pl.BlockSpec((None, r, 6 * D_MODEL), lambda i: (i // tiles_per_mod, 0, 0)),
        pl.BlockSpec((1, D_MODEL), const),
        pl.BlockSpec((tm, LANES), tab), pl.BlockSpec((tm, LANES), tab), pl.BlockSpec((tm, LANES), tab),
        pl.BlockSpec(wm.shape, const), pl.BlockSpec(bm.shape, const),
        pl.BlockSpec(wq.shape, const), pl.BlockSpec(bq.shape, const),
        pl.BlockSpec(wkv.shape, const), pl.BlockSpec(bkv.shape, const),
        pl.BlockSpec(ws.shape, const), pl.BlockSpec(bs.shape, const),
        pl.BlockSpec(qg.shape, const), pl.BlockSpec(kg.shape, const), pl.BlockSpec(bd.shape, const),
    ]
    widths = (M_WIDTH, M_WIDTH, M_WIDTH, A_WIDTH, A_WIDTH, 4 * LANES, 2 * LANES, LANES)
    out_specs = [pl.BlockSpec((tm, w), row) for w in widths]
    out_shape = [jax.ShapeDtypeStruct((m, w), F32) for w in widths]
    if rows_t_batches is not None:
        out_specs.append(pl.BlockSpec((None, 4 * LANES, tm), lambda i: (i // tiles_per_mod, 0, i % tiles_per_mod)))
        out_shape.append(jax.ShapeDtypeStruct((rows_t_batches, 4 * LANES, m // rows_t_batches), F32))
    return pl.pallas_call(
        _inproj_kernel,
        grid=(m // tm,),
        in_specs=in_specs,
        out_specs=out_specs,
        out_shape=out_shape,
        compiler_params=_cparams(("parallel",)),
        name="inproj",
    )(x2, mod3, gmix, cos_t, sp_t, sn_t, wm, bm, wq, bq, wkv, bkv, ws, bs, qg, kg, bd)


def _log_sigmoid(x):
    return jnp.minimum(x, 0.0) - jnp.log(1.0 + jnp.exp(-jnp.abs(x)))


def _mlstm_kernel(*refs, L, t_valid, has_state):
    if has_state:
        q_ref, k_ref, v_ref, s_ref, c0_ref, n0_ref, m0_ref, h_ref, c_ref, n_ref, m_ref = refs
    else:
        q_ref, k_ref, v_ref, s_ref, h_ref, c_ref, n_ref, m_ref = refs
    c = pl.program_id(1)

    @pl.when(c == 0)
    def _():
        if has_state:
            c_ref[...] = c0_ref[...]
            n_ref[...] = n0_ref[...]
            m_ref[...] = m0_ref[...]
        else:
            c_ref[...] = jnp.zeros(c_ref.shape, F32)
            n_ref[...] = jnp.zeros(n_ref.shape, F32)
            m_ref[...] = jnp.zeros(m_ref.shape, F32)

    row = lax.broadcasted_iota(jnp.int32, (L, L), 0)
    col = lax.broadcasted_iota(jnp.int32, (L, L), 1)
    causal = col <= row
    eye = col == row
    tok_col = c * L + lax.broadcasted_iota(jnp.int32, (L, 1), 0)
    valid_col = tok_col < t_valid
    for hd in range(M_HEADS):
        lo, hi = hd * M_DH, (hd + 1) * M_DH
        q = q_ref[:, lo:hi]
        k = k_ref[:, lo:hi]
        v = v_ref[:, lo:hi]
        i_col = s_ref[:, hd:hd + 1]
        lf_col = _log_sigmoid(s_ref[:, M_HEADS + hd:M_HEADS + hd + 1])
        lf_col = jnp.where(valid_col, lf_col, 0.0)
        i_col = jnp.where(valid_col, i_col, -jnp.inf)
        if L == LANES:
            i_col = jnp.broadcast_to(i_col, (L, L))
            lf_c = jnp.broadcast_to(lf_col, (L, L))
            p0 = lf_c.astype(BF16)
            r1 = lf_c - p0.astype(F32)
            p1 = r1.astype(BF16)
            p2 = (r1 - p1.astype(F32)).astype(BF16)
            tril = jnp.where(causal, 1.0, 0.0).astype(BF16)
            b_col = (jnp.dot(tril, p0, preferred_element_type=F32) + jnp.dot(tril, p1, preferred_element_type=F32)
                     + jnp.dot(tril, p2, preferred_element_type=F32))
            i_row = jnp.transpose(i_col)[0:1, :]
            b_row = jnp.transpose(b_col)[0:1, :]
        else:
            i_row = jnp.sum(jnp.where(eye, i_col, 0.0), axis=0, keepdims=True)
            lf_row = jnp.sum(jnp.where(eye, lf_col, 0.0), axis=0, keepdims=True)
            b_col = jnp.sum(jnp.where(causal, lf_row, 0.0), axis=1, keepdims=True)
            b_row = jnp.sum(jnp.where(row <= col, lf_col, 0.0), axis=0, keepdims=True)
        m_prev = m_ref[:, hd:hd + 1]
        dmat = jnp.where(causal, b_col - b_row + i_row, -jnp.inf)
        inter = b_col + m_prev
        m_row = jnp.maximum(jnp.max(dmat, axis=1, keepdims=True), inter)
        w = jnp.exp(dmat - m_row)
        w_inter = jnp.exp(inter - m_row)
        s = _bdot_t(q, k) * w
        cm = c_ref[hd]
        nv = n_ref[hd]
        num = _bdot(s, v) + w_inter * _bdot_t(q, cm)
        den = jnp.sum(s, axis=1, keepdims=True) + w_inter * jnp.sum(q * nv, axis=1, keepdims=True)
        h_ref[:, lo:hi] = num / jnp.maximum(jnp.abs(den), jnp.exp(-m_row))
        b_last = b_col[L - 1:L, 0:1]
        dec_col = b_last - b_col + i_col
        dec_row = b_last - b_row + i_row
        m_new = jnp.maximum(b_last + m_prev, jnp.max(dec_row, axis=1, keepdims=True))
        ws_col = jnp.exp(dec_col - m_new)
        wc = jnp.exp(b_last + m_prev - m_new)
        vw = (v * ws_col).astype(BF16)
        upd = lax.dot_general(vw, k.astype(BF16), (((0,), (0,)), ((), ())), preferred_element_type=F32)
        c_ref[hd] = wc * cm + upd
        n_ref[hd] = wc * nv + jnp.sum(k * ws_col, axis=0, keepdims=True)
        m_ref[:, hd:hd + 1] = m_new


def _mlstm(mq, mk, mv, small, nb, t_pad, t_valid, L, state=None):
    nc = t_pad // L
    has_state = state is not None
    blk = lambda b, c: (b * nc + c, 0)
    st4 = lambda b, c: (b, 0, 0, 0)
    st3 = lambda b, c: (b, 0, 0)
    in_specs = [pl.BlockSpec((L, M_WIDTH), blk)] * 3 + [pl.BlockSpec((L, LANES), blk)]
    args = [mq, mk, mv, small]
    if has_state:
        c0, n0, m0 = state
        in_specs += [pl.BlockSpec((None, M_HEADS, M_DH, M_DH), st4),
                     pl.BlockSpec((None, M_HEADS, 1, M_DH), st4),
                     pl.BlockSpec((None, 1, M_HEADS), st3)]
        args += [c0, n0.reshape(nb, M_HEADS, 1, M_DH), m0.reshape(nb, 1, M_HEADS)]
    out_specs = [pl.BlockSpec((L, M_WIDTH), blk),
                 pl.BlockSpec((None, M_HEADS, M_DH, M_DH), st4),
                 pl.BlockSpec((None, M_HEADS, 1, M_DH), st4),
                 pl.BlockSpec((None, 1, M_HEADS), st3)]
    out_shape = [jax.ShapeDtypeStruct((nb * t_pad, M_WIDTH), F32),
                 jax.ShapeDtypeStruct((nb, M_HEADS, M_DH, M_DH), F32),
                 jax.ShapeDtypeStruct((nb, M_HEADS, 1, M_DH), F32),
                 jax.ShapeDtypeStruct((nb, 1, M_HEADS), F32)]
    h, cs, ns, ms = pl.pallas_call(
        functools.partial(_mlstm_kernel, L=L, t_valid=t_valid, has_state=has_state),
        grid=(nb, nc),
        in_specs=in_specs,
        out_specs=out_specs,
        out_shape=out_shape,
        compiler_params=_cparams(("parallel", "arbitrary")),
        name="mlstm",
    )(*args)
    return h, cs, ns.reshape(nb, M_HEADS, M_DH), ms.reshape(nb, M_HEADS)


def _stack_heads(qt, g):
    t = qt.shape[0]
    z = jnp.zeros((t, A_DH), F32)
    parts = []
    for hh in range(A_HPG):
        hd = g * A_HPG + hh
        qh = qt[:, hd * A_DH:(hd + 1) * A_DH] * (ATT_SCALE * LOG2E)
        parts.append(jnp.concatenate([qh, z], axis=1) if g == 0 else jnp.concatenate([z, qh], axis=1))
    return jnp.concatenate(parts, axis=0).astype(BF16)


def _gate_cols(small, g, br):
    cols = []
    for hh in range(A_HPG):
        c0 = 2 * M_HEADS + (g * A_HPG + hh) * 3 + br
        cols.append(_sigmoid(small[:, c0:c0 + 1]))
    return jnp.concatenate(cols, axis=0)


def _compress(k_ref, v_ref, nseg, wbd_ref, pe_ref, kg0):
    acc_lo = jnp.zeros((nseg, 2 * LANES), F32)
    acc_hi = jnp.zeros((nseg, 2 * LANES), F32)
    for l in range(CMP_STRIDE):
        xl = jnp.concatenate([k_ref[pl.ds(l, nseg, stride=CMP_STRIDE), :],
                              v_ref[pl.ds(l, nseg, stride=CMP_STRIDE), :]], axis=1)
        acc_lo = acc_lo + _bdot(xl + pe_ref[l], wbd_ref[l])
        acc_hi = acc_hi + _bdot(xl + pe_ref[CMP_STRIDE + l], wbd_ref[CMP_STRIDE + l])
    return _compress_finish(acc_lo, acc_hi, nseg, kg0)


def _compress_grouped(x_ref, nseg, wbd_ref, pe_ref, kg0):
    acc_lo = jnp.zeros((nseg, 2 * LANES), F32)
    acc_hi = jnp.zeros((nseg, 2 * LANES), F32)
    pe_lo = jnp.zeros((SUBLANES, 2 * LANES), F32)
    pe_hi = jnp.zeros((SUBLANES, 2 * LANES), F32)
    for l in range(CMP_STRIDE):
        xl = x_ref[l].astype(BF16)
        acc_lo = acc_lo + jnp.dot(xl, wbd_ref[l], preferred_element_type=F32)
        acc_hi = acc_hi + jnp.dot(xl, wbd_ref[CMP_STRIDE + l], preferred_element_type=F32)
        pe_lo = pe_lo + _bdot(jnp.broadcast_to(pe_ref[l], (SUBLANES, 2 * LANES)), wbd_ref[l])
        pe_hi = pe_hi + _bdot(jnp.broadcast_to(pe_ref[CMP_STRIDE + l], (SUBLANES, 2 * LANES)),
                              wbd_ref[CMP_STRIDE + l])
    return _compress_finish(acc_lo + pe_lo[0:1, :], acc_hi + pe_hi[0:1, :], nseg, kg0)


def _compress_finish(acc_lo, acc_hi, nseg, kg0):
    kv = acc_lo + pltpu.roll(acc_hi, nseg - 1, 0)
    kc = kv[:, 0:LANES]
    vc = kv[:, LANES:2 * LANES]
    lane = lax.broadcasted_iota(jnp.int32, (nseg, LANES), 1)
    sq = kc * kc
    ms0 = jnp.sum(jnp.where(lane < A_DH, sq, 0.0), axis=1, keepdims=True) * (1.0 / A_DH)
    ms1 = jnp.sum(jnp.where(lane >= A_DH, sq, 0.0), axis=1, keepdims=True) * (1.0 / A_DH)
    ms = jnp.where(lane < A_DH, ms0, ms1)
    kc = kc * lax.rsqrt(ms + EPS) * kg0
    return kc, vc


def _cmp_branch(qn_g, kc_b, vc_b, tpos_rows, nseg, n_tok):
    s = _bdot_t(qn_g, kc_b)
    nidx = lax.broadcasted_iota(jnp.int32, (1, nseg), 1)
    vis = (nidx * CMP_STRIDE + (CMP_LEN - 1)) <= tpos_rows
    sm = jnp.where(vis, s, NEG_BIG)
    mx = jnp.max(sm, axis=1, keepdims=True)
    e = jnp.where(vis, jnp.exp2(sm - mx), 0.0)
    d = jnp.sum(e, axis=1, keepdims=True)
    p = e / jnp.where(d > 0, d, 1.0)
    o = _bdot(p, vc_b)
    imp = p[0:n_tok]
    for hh in range(1, A_HPG):
        imp = imp + p[hh * n_tok:(hh + 1) * n_tok]
    return o, imp


def _masked_attn_direct(q_g, k_parts, v_parts, allowed_parts, feature_major):
    ss = [jnp.where(al, _bdot(q_g, kk) if fm else _bdot_t(q_g, kk), NEG_BIG)
          for kk, al, fm in zip(k_parts, allowed_parts, feature_major)]
    mx = ss[0].max(axis=1, keepdims=True)
    for s in ss[1:]:
        mx = jnp.maximum(mx, s.max(axis=1, keepdims=True))
    num = None
    den = None
    for s, al, vv, fm in zip(ss, allowed_parts, v_parts, feature_major):
        e = jnp.where(al, jnp.exp2(s - mx), 0.0)
        dd = jnp.sum(e, axis=1, keepdims=True)
        oo = _bdot_t(e, vv) if fm else _bdot(e, vv)
        num = oo if num is None else num + oo
        den = dd if den is None else den + dd
    return num / jnp.where(den > 0, den, 1.0)


def _assemble_heads(o_groups, n_tok):
    pieces = []
    for g in range(A_KV):
        for hh in range(A_HPG):
            pieces.append(o_groups[g][hh * n_tok:(hh + 1) * n_tok, g * A_DH:(g + 1) * A_DH])
    return jnp.concatenate(pieces, axis=1)


def _lane_rep(a, rep):
    return a if rep == 1 else jnp.concatenate([a] * rep, axis=1)


def _nsa_prompt_kernel(q_ref, qr_ref, small_ref, rows_ref, win_ref, wbd_ref, pe_ref, kg0_ref,
                       pool_ref, o_ref,
                       kraw_sc, vraw_sc, kc_sc, vct_sc, sel_sc, m_sc, acc_sc, s_sc, *, T, tq, kc_len):
    qi = pl.program_id(1)
    nseg = T // CMP_STRIDE
    nsb = T // SEL_LEN
    bpc = kc_len // SEL_LEN

    @pl.when(qi == 0)
    def _():
        kraw_sc[...] = rows_ref[:, 0:LANES]
        vraw_sc[...] = rows_ref[:, LANES:2 * LANES]
        kc, vc = _compress(kraw_sc, vraw_sc, nseg, wbd_ref, pe_ref, kg0_ref[...])
        kc_sc[...] = kc
        vct_sc[...] = jnp.transpose(vc)

    t0 = qi * tq
    tpos = t0 + lax.broadcasted_iota(jnp.int32, (1, tq), 1)
    tpos4 = _lane_rep(tpos, A_HPG)
    q = q_ref[...]
    qr = qr_ref[...]
    small_t = jnp.transpose(small_ref[...])
    kc_b = kc_sc[...].astype(BF16)
    vct_b = vct_sc[...].astype(BF16)
    bidx = lax.broadcasted_iota(jnp.int32, (nsb, tq), 0)
    cur = tpos // SEL_LEN
    vis = (lax.broadcasted_iota(jnp.int32, (nseg, 1), 0) * CMP_STRIDE + (CMP_LEN - 1)) <= tpos4
    qr_gs = [_stack_heads(qr, g) for g in range(A_KV)]
    o_cmps = []
    for g in range(A_KV):
        sm = jnp.where(vis, _bdot_t(kc_b, _stack_heads(q, g)), NEG_BIG)
        mx = jnp.max(sm, axis=0, keepdims=True)
        e = jnp.where(vis, jnp.exp2(sm - mx), 0.0)
        d = jnp.sum(e, axis=0, keepdims=True)
        p = e / jnp.where(d > 0, d, 1.0)
        o_cmps.append(jnp.dot(vct_b, p.astype(BF16), preferred_element_type=F32))
        imp = p[:, 0:tq]
        for hh in range(1, A_HPG):
            imp = imp + p[:, hh * tq:(hh + 1) * tq]
        ih, il = _split(imp)
        imp_t = (jnp.dot(pool_ref[...], ih, preferred_element_type=F32)
                 + jnp.dot(pool_ref[...], il, preferred_element_type=F32))[0:nsb]
        val = jnp.where(bidx < cur, imp_t, -1.0)
        rank = jnp.zeros((nsb, tq), F32)
        for bp in range(nsb):
            vb = val[bp:bp + 1, :]
            rank = rank + jnp.where(vb > val, 1.0, jnp.where((vb == val) & (bidx > bp), 1.0, 0.0))
        sel_sc[g] = jnp.where(((rank < (N_SEL - 1)) & (bidx < cur)) | (bidx == cur), 1.0, 0.0)

    m_sc[...] = jnp.full(m_sc.shape, M_INIT, F32)
    acc_sc[...] = jnp.zeros(acc_sc.shape, F32)

    def with_ones_row(vt_, g):
        vb = vt_.astype(BF16)
        r0, pad = (1 - g) * A_DH, 2 * SUBLANES
        ones = jnp.ones((pad, vb.shape[1]), BF16)
        return jnp.concatenate(([vb[0:r0]] if r0 else []) + [ones, vb[r0 + pad:]], axis=0)

    def sel_body(c, carry):
        k0 = pl.multiple_of(c * kc_len, kc_len)
        kb = rows_ref[pl.ds(k0, kc_len), 2 * LANES:3 * LANES].astype(BF16)
        vt = jnp.transpose(rows_ref[pl.ds(k0, kc_len), 3 * LANES:4 * LANES])
        causal = (k0 + lax.broadcasted_iota(jnp.int32, (kc_len, 1), 0)) <= tpos
        for g in range(A_KV):
            s_sc[g, 0:kc_len, :] = _bdot_t(kb, qr_gs[g])
        for g in range(A_KV):
            selc = sel_sc[g, pl.ds(pl.multiple_of(c * bpc, bpc), bpc), :]
            selx = jnp.concatenate([jnp.broadcast_to(selc[j:j + 1, :], (SEL_LEN, tq)) for j in range(bpc)], axis=0)
            bias = jnp.where(causal & (selx > 0.5), 0.0, NEG_BIG)
            sm = s_sc[g, 0:kc_len, :] + _lane_rep(bias, A_HPG)
            m_prev = m_sc[g]
            m_new = jnp.maximum(m_prev, jnp.max(sm, axis=0, keepdims=True))
            alpha = jnp.exp2(m_prev - m_new)
            p = jnp.exp2(sm - m_new)
            acc_sc[g] = alpha * acc_sc[g] + jnp.dot(with_ones_row(vt, g), p.astype(BF16),
                                                    preferred_element_type=F32)
            m_sc[g] = m_new
        return carry

    lax.fori_loop(0, (t0 + tq + kc_len - 1) // kc_len, sel_body, 0)

    wk = min(WINDOW + tq, T)
    w0 = pl.multiple_of(jnp.clip(t0 + tq - wk, 0, T - wk), tq)
    kw = win_ref[pl.ds(w0, wk), 0:LANES].astype(BF16)
    vwt = jnp.transpose(win_ref[pl.ds(w0, wk), LANES:2 * LANES])
    wdiff = tpos - (w0 + lax.broadcasted_iota(jnp.int32, (wk, 1), 0))
    wbias = _lane_rep(jnp.where((wdiff >= 0) & (wdiff < WINDOW), 0.0, NEG_BIG), A_HPG)

    def gate_row(g, br):
        cols = [2 * M_HEADS + (g * A_HPG + hh) * 3 + br for hh in range(A_HPG)]
        return jnp.concatenate([_sigmoid(small_t[c0:c0 + 1, :]) for c0 in cols], axis=1)

    for g in range(A_KV):
        s_sc[g, 0:wk, :] = _bdot_t(kw, qr_gs[g])
    o_ts = []
    for g in range(A_KV):
        den = (1 - g) * A_DH
        acc = acc_sc[g]
        l = acc[den:den + 1, :]
        o_sel = acc / jnp.where(l > 0, l, 1.0)
        sw = s_sc[g, 0:wk, :] + wbias
        pw = jnp.exp2(sw - jnp.max(sw, axis=0, keepdims=True))
        ow = jnp.dot(with_ones_row(vwt, g), pw.astype(BF16), preferred_element_type=F32)
        o_win = ow / ow[den:den + 1, :]
        o_ts.append(gate_row(g, 0) * o_cmps[g] + gate_row(g, 1) * o_sel + gate_row(g, 2) * o_win)
    for j in range(A_HEADS // 2):
        g, h0 = j // (A_HPG // 2), 2 * (j % (A_HPG // 2))
        og = o_ts[g][g * A_DH:(g + 1) * A_DH, :]
        pair = jnp.concatenate([og[:, h0 * tq:(h0 + 1) * tq], og[:, (h0 + 1) * tq:(h0 + 2) * tq]], axis=0)
        o_ref[:, j * LANES:(j + 1) * LANES] = jnp.transpose(pair)


def _nsa_prompt(q, qr, small, rows, win, wbd, pe, kg0, nb, T):
    tq = 128
    kc_len = _pick_tile(T, 512)
    nq = T // tq
    nseg = T // CMP_STRIDE
    nsb = T // SEL_LEN
    nsb_p = -(-nsb // SUBLANES) * SUBLANES
    pool = (jnp.arange(nsb_p)[:, None] == jnp.arange(nseg)[None, :] // (SEL_LEN // CMP_STRIDE)).astype(BF16)
    tile = lambda b, i: (b * nq + i, 0)
    per_b = lambda b, i: (b, 0)
    c2 = lambda b, i: (0, 0)
    c3 = lambda b, i: (0, 0, 0)
    c4 = A_HPG * tq
    return pl.pallas_call(
        functools.partial(_nsa_prompt_kernel, T=T, tq=tq, kc_len=kc_len),
        grid=(nb, nq),
        in_specs=[pl.BlockSpec((tq, A_WIDTH), tile), pl.BlockSpec((tq, A_WIDTH), tile),
                  pl.BlockSpec((tq, LANES), tile),
                  pl.BlockSpec((T, 4 * LANES), per_b), pl.BlockSpec((T, 2 * LANES), per_b),
                  pl.BlockSpec(wbd.shape, c3), pl.BlockSpec(pe.shape, c3), pl.BlockSpec(kg0.shape, c2),
                  pl.BlockSpec(pool.shape, c2)],
        out_specs=pl.BlockSpec((tq, A_WIDTH), tile),
        out_shape=jax.ShapeDtypeStruct((nb * T, A_WIDTH), F32),
        scratch_shapes=[pltpu.VMEM((T, LANES), F32), pltpu.VMEM((T, LANES), F32),
                        pltpu.VMEM((nseg, LANES), F32), pltpu.VMEM((LANES, nseg), F32),
                        pltpu.VMEM((A_KV, nsb, tq), F32),
                        pltpu.VMEM((A_KV, 1, c4), F32),
                        pltpu.VMEM((A_KV, LANES, c4), F32),
                        pltpu.VMEM((A_KV, max(kc_len, min(WINDOW + tq, T)), c4), F32)],
        compiler_params=_cparams(("parallel", "arbitrary")),
        name="nsa_prompt",
    )(q, qr, small, rows, win, wbd, pe, kg0, pool)


def _nsa_sample_kernel(pt_ref, cache_ref, q_ref, qr_ref, small_ref, rows_ref, winnew_ref, winbuf_ref,
                       wbd_ref, pe_ref, kg0_ref, pool_ref, expand_ref,
                       o_ref, winout_ref,
                       cmp_buf, sel_buf, xperm_sc, sems, *, n_pages, past_len, t_valid):
    b = pl.program_id(0)
    nb = pl.num_programs(0)
    tp = SAMPLE_PAD_T
    nseg = past_len // CMP_STRIDE
    nsb = past_len // SEL_LEN
    wbuf = winbuf_ref.shape[1]

    def page_copies(bb, p, phase):
        page = pt_ref[bb * n_pages + p]
        dst_lanes = pl.ds(pl.multiple_of(p * PAGE_SIZE, PAGE_SIZE), PAGE_SIZE)
        if phase == 0:
            return [pltpu.make_async_copy(cache_ref.at[page, pl.ds(0, 2 * LANES), :],
                                          cmp_buf.at[:, dst_lanes], sems.at[0])]
        return [pltpu.make_async_copy(cache_ref.at[page, pl.ds(2 * LANES, 2 * LANES), :],
                                      sel_buf.at[:, dst_lanes], sems.at[1])]

    def start_all(bb, phase):
        def body(p, c):
            for cp in page_copies(bb, p, phase):
                cp.start()
            return c
        lax.fori_loop(0, n_pages, body, 0)

    def wait_all(bb, phase):
        def body(p, c):
            for cp in page_copies(bb, p, phase):
                cp.wait()
            return c
        lax.fori_loop(0, n_pages, body, 0)

    @pl.when(b == 0)
    def _():
        start_all(b, 0)

    start_all(b, 1)
    wait_all(b, 0)

    seg_pp = PAGE_SIZE // CMP_STRIDE
    pr = lax.broadcasted_iota(jnp.int32, (PAGE_SIZE, PAGE_SIZE), 0)
    pc = lax.broadcasted_iota(jnp.int32, (PAGE_SIZE, PAGE_SIZE), 1)
    perm = jnp.where(pc == CMP_STRIDE * (pr % seg_pp) + pr // seg_pp, 1.0, 0.0).astype(BF16)
    for p in range(n_pages):
        xp = _bdot_t(perm, cmp_buf[:, p * PAGE_SIZE:(p + 1) * PAGE_SIZE])
        for l in range(CMP_STRIDE):
            xperm_sc[l, p * seg_pp:(p + 1) * seg_pp, :] = xp[l * seg_pp:(l + 1) * seg_pp, :]
    kc, vc = _compress_grouped(xperm_sc, nseg, wbd_ref, pe_ref, kg0_ref[...])
    kc_b = kc.astype(BF16)
    vc_b = vc.astype(BF16)
    q = q_ref[...]
    qr = qr_ref[...]
    small = small_ref[...]
    tpos_col = past_len + lax.broadcasted_iota(jnp.int32, (tp, 1), 0)
    tpos_rows = jnp.concatenate([tpos_col] * A_HPG, axis=0)
    bp_idx = lax.broadcasted_iota(jnp.int32, (nsb, nsb), 0)
    b_idx = lax.broadcasted_iota(jnp.int32, (nsb, nsb), 1)
    o_cmps = []
    sels = []
    for g in range(A_KV):
        qn_g = _stack_heads(q, g)
        o_cmp, imp = _cmp_branch(qn_g, kc_b, vc_b, tpos_rows, nseg, tp)
        o_cmps.append(o_cmp)
        imp_sel = _dot2_exact_rhs(imp, pool_ref[...])
        imp_pad = jnp.concatenate([imp_sel, jnp.zeros((nsb - tp, nsb), F32)], axis=0)
        imp_t = jnp.transpose(imp_pad)
        rows_sel = []
        for t in range(tp):
            if t < t_valid:
                row_t = imp_sel[t:t + 1, :]
                col_t = imp_t[:, t:t + 1]
                ahead = jnp.where(col_t > row_t, 1.0, jnp.where((col_t == row_t) & (bp_idx < b_idx), 1.0, 0.0))
                rank = jnp.sum(ahead, axis=0, keepdims=True)
                rows_sel.append(jnp.where(rank < (N_SEL - 1), 1.0, 0.0))
            else:
                rows_sel.append(jnp.zeros((1, nsb), F32))
        sels.append(jnp.concatenate(rows_sel, axis=0).astype(BF16))

    @pl.when(b + 1 < nb)
    def _():
        start_all(b + 1, 0)

    wait_all(b, 1)

    new_idx = lax.broadcasted_iota(jnp.int32, (tp, tp), 1)
    tok_idx = lax.broadcasted_iota(jnp.int32, (tp, tp), 0)
    new_ok = jnp.concatenate([jnp.where(new_idx <= tok_idx, 1.0, 0.0)] * A_HPG, axis=0) > 0.5
    wpos = past_len - wbuf + lax.broadcasted_iota(jnp.int32, (1, wbuf), 1)
    wdiff = tpos_col - wpos
    win_ok = jnp.concatenate([jnp.where((wdiff >= 0) & (wdiff < WINDOW), 1.0, 0.0)] * A_HPG, axis=0) > 0.5
    k_past = sel_buf[0:LANES, :].astype(BF16)
    v_past = sel_buf[LANES:2 * LANES, :].astype(BF16)
    k_new = rows_ref[:, 2 * LANES:3 * LANES]
    v_new = rows_ref[:, 3 * LANES:4 * LANES]
    kw_past = winbuf_ref[0:LANES, :]
    vw_past = winbuf_ref[LANES:2 * LANES, :]
    kw_new = winnew_ref[:, 0:LANES]
    vw_new = winnew_ref[:, LANES:2 * LANES]
    o_groups = []
    for g in range(A_KV):
        qr_g = _stack_heads(qr, g)
        mk = jnp.dot(sels[g], expand_ref[...], preferred_element_type=F32)
        past_ok = jnp.concatenate([mk] * A_HPG, axis=0) > 0.5
        o_sel = _masked_attn_direct(qr_g, [k_past, k_new], [v_past, v_new], [past_ok, new_ok], [True, False])
        o_win = _masked_attn_direct(qr_g, [kw_past, kw_new], [vw_past, vw_new], [win_ok, new_ok], [True, False])
        o_groups.append(_gate_cols(small, g, 0) * o_cmps[g] + _gate_cols(small, g, 1) * o_sel
                        + _gate_cols(small, g, 2) * o_win)
    o_ref[...] = _assemble_heads(o_groups, tp)

    rolled = pltpu.roll(winbuf_ref[...], wbuf - t_valid, 1)
    new_t = jnp.transpose(jnp.concatenate([winnew_ref[...], jnp.zeros((LANES - tp, 2 * LANES), F32)], axis=0))
    new_t = pltpu.roll(new_t, LANES - t_valid, 1)
    lane = lax.broadcasted_iota(jnp.int32, (2 * LANES, LANES), 1)
    winout_ref[:, 0:wbuf - LANES] = rolled[:, 0:wbuf - LANES]
    winout_ref[:, wbuf - LANES:wbuf] = jnp.where(lane < LANES - t_valid, rolled[:, wbuf - LANES:wbuf], new_t)


def _nsa_sample(page_table, cache, q, qr, small, rows, winnew, winbuf, wbd, pe, kg0, t_valid):
    nb, n_pages = page_table.shape
    past_len = n_pages * PAGE_SIZE
    nseg = past_len // CMP_STRIDE
    nsb = past_len // SEL_LEN
    tp = SAMPLE_PAD_T
    wbuf = winbuf.shape[2]
    pool = (jnp.arange(nseg)[:, None] // (SEL_LEN // CMP_STRIDE) == jnp.arange(nsb)[None, :]).astype(BF16)
    expand = (jnp.arange(nsb)[:, None] == jnp.arange(past_len)[None, :] // SEL_LEN).astype(BF16)
    tile = lambda b, pt: (b, 0)
    c2 = lambda b, pt: (0, 0)
    c3 = lambda b, pt: (0, 0, 0)
    gs = pltpu.PrefetchScalarGridSpec(
        num_scalar_prefetch=1,
        grid=(nb,),
        in_specs=[pl.BlockSpec(memory_space=pl.ANY),
                  pl.BlockSpec((tp, A_WIDTH), tile), pl.BlockSpec((tp, A_WIDTH), tile),
                  pl.BlockSpec((tp, LANES), tile), pl.BlockSpec((tp, 4 * LANES), tile),
                  pl.BlockSpec((tp, 2 * LANES), tile),
                  pl.BlockSpec((None, 2 * LANES, wbuf), lambda b, pt: (b, 0, 0)),
                  pl.BlockSpec(wbd.shape, c3), pl.BlockSpec(pe.shape, c3), pl.BlockSpec(kg0.shape, c2),
                  pl.BlockSpec(pool.shape, c2), pl.BlockSpec(expand.shape, c2)],
        out_specs=[pl.BlockSpec((tp, A_WIDTH), tile),
                   pl.BlockSpec((None, 2 * LANES, wbuf), lambda b, pt: (b, 0, 0))],
        scratch_shapes=[pltpu.VMEM((2 * LANES, past_len), F32), pltpu.VMEM((2 * LANES, past_len), F32),
                        pltpu.VMEM((CMP_STRIDE, past_len // CMP_STRIDE, 2 * LANES), F32),
                        pltpu.SemaphoreType.DMA((2,))],
    )
    return pl.pallas_call(
        functools.partial(_nsa_sample_kernel, n_pages=n_pages, past_len=past_len, t_valid=t_valid),
        grid_spec=gs,
        out_shape=[jax.ShapeDtypeStruct((nb * tp, A_WIDTH), F32),
                   jax.ShapeDtypeStruct((nb, 2 * LANES, wbuf), F32)],
        compiler_params=_cparams(("arbitrary",)),
        name="nsa_sample",
    )(page_table.reshape(-1), cache, q, qr, small, rows, winnew, winbuf, wbd, pe, kg0, pool, expand)


MOE_TM = 256
SEG_ALIGN = 8
SEG_BITS = (256, 128, 64, 32, 16, 8)
MOE_RL = -(-(MOE_TM * TOP_K + N_EXPERTS * (SEG_ALIGN - 1)) // LANES) * LANES


def _pack_halves(x, bf16_exact=False):
    w = x.shape[1] // 2
    bits = lax.bitcast_convert_type(x if bf16_exact else x.astype(BF16).astype(F32), jnp.uint32)
    return (bits[:, :w] & jnp.uint32(0xFFFF0000)) | (bits[:, w:] >> 16)


def _unpack_halves(u):
    hi = lax.bitcast_convert_type(u & jnp.uint32(0xFFFF0000), F32).astype(BF16)
    lo = lax.bitcast_convert_type(u << 16, F32).astype(BF16)
    return hi, lo


def _route_and_sort(h2, wrt_ref, brt_ref, xsl_ref, info_ref, cnt_ref, tm, t_mod, t_valid, m_valid):
    ne = N_EXPERTS
    h2b = h2.astype(BF16)
    h2l = (h2 - h2b.astype(F32)).astype(BF16)
    wh, wl = _split(wrt_ref[...])
    lt = _bdot_t(wh, h2b) + _bdot_t(wl, h2b) + _bdot_t(wh, h2l) + brt_ref[...]
    eidx = lax.broadcasted_iota(jnp.int32, (ne, tm), 0)
    rank = jnp.zeros((ne, tm), F32)
    for ep in range(ne):
        v = lt[ep:ep + 1, :]
        rank = rank + jnp.where(v > lt, 1.0, jnp.where((v == lt) & (eidx > ep), 1.0, 0.0))
    sel = rank < TOP_K
    if t_mod is not None:
        tok = pl.program_id(0) * tm + lax.broadcasted_iota(jnp.int32, (1, tm), 1)
        sel = sel & ((tok % t_mod) < t_valid) & (tok < m_valid)
    mx = jnp.max(jnp.where(sel, lt, NEG_BIG), axis=0, keepdims=True)
    ex = jnp.where(sel, jnp.exp(lt - mx), 0.0)
    den = jnp.sum(ex, axis=0, keepdims=True)
    gate = ex / jnp.where(den > 0, den, 1.0)
    self_ = jnp.where(sel, 1.0, 0.0)
    selb = self_.astype(BF16)
    er = lax.broadcasted_iota(jnp.int32, (ne, ne), 0)
    ec = lax.broadcasted_iota(jnp.int32, (ne, ne), 1)
    c = jnp.dot(jnp.where(ec <= er, 1.0, 0.0).astype(BF16), selb, preferred_element_type=F32)
    tr = lax.broadcasted_iota(jnp.int32, (tm, tm), 0)
    tc = lax.broadcasted_iota(jnp.int32, (tm, tm), 1)
    rk = jnp.dot(selb, jnp.where(tr < tc, 1.0, 0.0).astype(BF16), preferred_element_type=F32)
    cnt = jnp.sum(self_, axis=1, keepdims=True)
    cnt_al = jnp.floor((cnt + (SEG_ALIGN - 1)) * (1.0 / SEG_ALIGN)) * SEG_ALIGN
    cnt_b = jnp.broadcast_to(cnt_al, (ne, LANES))
    cnt_ref[...] = cnt_b
    off = jnp.dot(jnp.where(ec < er, 1.0, 0.0).astype(BF16), cnt_b.astype(BF16), preferred_element_type=F32)
    rowidx = off[:, 0:1] + rk
    rows_k, gates_k, exps_k = [], [], []
    for k in range(1, TOP_K + 1):
        mk = sel & (c == k)
        has = jnp.sum(jnp.where(mk, 1.0, 0.0), axis=0, keepdims=True)
        rows_k.append(jnp.sum(jnp.where(mk, rowidx, 0.0), axis=0, keepdims=True) + has - 1.0)
        gates_k.append(jnp.sum(jnp.where(mk, gate, 0.0), axis=0, keepdims=True))
        exps_k.append(jnp.sum(jnp.where(mk, eidx.astype(F32), 0.0), axis=0, keepdims=True))
    info_ref[...] = jnp.concatenate(rows_k + gates_k + exps_k + [jnp.zeros((4, tm), F32)], axis=0)
    ridx = lax.broadcasted_iota(jnp.int32, (MOE_RL, tm), 0).astype(F32)
    perm = jnp.zeros((MOE_RL, tm), F32)
    for k in range(TOP_K):
        perm = perm + jnp.where(ridx == rows_k[k], 1.0, 0.0)
    xs = jnp.dot(perm.astype(BF16), h2b, preferred_element_type=F32)
    xsl_ref[...] = _pack_halves(xs, bf16_exact=True)


def _mixout_kernel(x_ref, hm_ref, on_ref, mod_ref, gmix_ref, gffn_ref,
                   wog_ref, bog_ref, wum_ref, wua_ref, wout_ref, wrt_ref, brt_ref,
                   x1_ref, xsl_ref, info_ref, cnt_ref, *, tm, t_mod, t_valid, m_valid, n_real):
    if n_real is not None:
        @pl.when(pl.program_id(0) >= n_real)
        def _():
            xsl_ref[...] = jnp.zeros(xsl_ref.shape, jnp.uint32)
            info_ref[...] = jnp.zeros(info_ref.shape, F32)
            cnt_ref[...] = jnp.zeros(cnt_ref.shape, F32)

        @pl.when(pl.program_id(0) < n_real)
        def _():
            _mixout_body(x_ref, hm_ref, on_ref, mod_ref, gmix_ref, gffn_ref, wog_ref, bog_ref, wum_ref,
                         wua_ref, wout_ref, wrt_ref, brt_ref, x1_ref, xsl_ref, info_ref, cnt_ref,
                         tm, t_mod, t_valid, m_valid)
    else:
        _mixout_body(x_ref, hm_ref, on_ref, mod_ref, gmix_ref, gffn_ref, wog_ref, bog_ref, wum_ref,
                     wua_ref, wout_ref, wrt_ref, brt_ref, x1_ref, xsl_ref, info_ref, cnt_ref,
                     tm, t_mod, t_valid, m_valid)


def _mixout_body(x_ref, hm_ref, on_ref, mod_ref, gmix_ref, gffn_ref,
                 wog_ref, bog_ref, wum_ref, wua_ref, wout_ref, wrt_ref, brt_ref,
                 x1_ref, xsl_ref, info_ref, cnt_ref, tm, t_mod, t_valid, m_valid):
    d = D_MODEL
    x = x_ref[...]
    sh1, sc1, gt1 = mod_ref[:, 0:d], mod_ref[:, d:2 * d], mod_ref[:, 2 * d:3 * d]
    sh2, sc2 = mod_ref[:, 3 * d:4 * d], mod_ref[:, 4 * d:5 * d]
    h = _rmsnorm_rows(x, gmix_ref[...]) * (1.0 + sc1) + sh1
    hb = h.astype(BF16)
    mo = jnp.dot(hb, wog_ref[:, 0:M_WIDTH], preferred_element_type=F32) + bog_ref[:, 0:M_WIDTH]
    ym = _bdot(_sigmoid(mo) * hm_ref[...], wum_ref[...])
    ya = _bdot(on_ref[...], wua_ref[...])
    ga = jnp.dot(hb, wog_ref[:, M_WIDTH:M_WIDTH + d], preferred_element_type=F32) + bog_ref[:, M_WIDTH:M_WIDTH + d]
    u = _sigmoid(ga) * ym
    gb = (jnp.dot(hb, wog_ref[:, M_WIDTH + d:M_WIDTH + 2 * d], preferred_element_type=F32)
          + bog_ref[:, M_WIDTH + d:M_WIDTH + 2 * d])
    u = u + _sigmoid(gb) * ya
    x1 = x + gt1 * _bdot(u, wout_ref[...])
    x1_ref[...] = x1
    h2 = _rmsnorm_rows(x1, gffn_ref[...]) * (1.0 + sc2) + sh2
    _route_and_sort(h2, wrt_ref, brt_ref, xsl_ref, info_ref, cnt_ref, tm, t_mod, t_valid, m_valid)


def _mixout_with_shared(*refs, n_shared, **kw):
    n_in = 13
    _mixout_kernel(*refs[:n_in], *refs[n_in + n_shared:], **kw)


def _mixout(x2, hm, on, mod3, gmix, gffn, wts, tiles_per_mod, nt_total, tile0=0, shared=None,
            t_mod=None, t_valid=None, m_valid=None):
    m = x2.shape[0]
    tm = MOE_TM
    nt = m // tm
    (wog, bog, wum, wua, wout, wr, br) = wts
    r = mod3.shape[1]
    n_extra = nt_total - tile0 - nt if shared is None else 0
    row = lambda i: (jnp.minimum(i, nt - 1), 0)
    const = lambda i: (0, 0)
    in_specs = [pl.BlockSpec((tm, D_MODEL), row), pl.BlockSpec((tm, M_WIDTH), row),
                pl.BlockSpec((tm, A_WIDTH), row),
                pl.BlockSpec((None, r, 6 * D_MODEL), lambda i: (jnp.minimum(i, nt - 1) // tiles_per_mod, 0, 0)),
                pl.BlockSpec((1, D_MODEL), const), pl.BlockSpec((1, D_MODEL), const),
                pl.BlockSpec(wog.shape, const), pl.BlockSpec(bog.shape, const),
                pl.BlockSpec(wum.shape, const), pl.BlockSpec(wua.shape, const),
                pl.BlockSpec(wout.shape, const), pl.BlockSpec(wr.shape, const),
                pl.BlockSpec(br.shape, const)]
    args = [x2, hm, on, mod3, gmix, gffn, wog, bog, wum, wua, wout, wr, br]
    kw = dict(tm=tm, t_mod=t_mod, t_valid=t_valid, m_valid=m_valid, n_real=nt if n_extra else None)
    body = functools.partial(_mixout_kernel, **kw)
    aliases = {}
    if shared is not None:
        in_specs += [pl.BlockSpec(memory_space=pl.ANY)] * len(shared)
        aliases = {len(args) + j: 1 + j for j in range(len(shared))}
        args += list(shared)
        body = functools.partial(_mixout_with_shared, n_shared=len(shared), **kw)
    return pl.pallas_call(
        body,
        grid=(nt + n_extra,),
        in_specs=in_specs,
        out_specs=[pl.BlockSpec((tm, D_MODEL), row),
                   pl.BlockSpec((MOE_RL, D_MODEL // 2), lambda i: (tile0 + i, 0)),
                   pl.BlockSpec((16, tm), lambda i: (0, tile0 + i)),
                   pl.BlockSpec((None, N_EXPERTS, LANES), lambda i: (tile0 + i, 0, 0))],
        out_shape=[jax.ShapeDtypeStruct((m, D_MODEL), F32),
                   jax.ShapeDtypeStruct((nt_total * MOE_RL, D_MODEL // 2), jnp.uint32),
                   jax.ShapeDtypeStruct((16, nt_total * tm), F32),
                   jax.ShapeDtypeStruct((nt_total, N_EXPERTS, LANES), F32)],
        input_output_aliases=aliases,
        compiler_params=_cparams(("arbitrary" if n_extra else "parallel",)),
        name="mixout",
    )(*args)


MOE_BM = 256
MOE_CH = 512


def _moe_kernel(be_ref, na_ref, grp_ref,
                xsl_ref, wgu_ref, bgu_ref, wdn_ref, bdn_ref, ysl_ref,
                wgu_bf, wdn_bf, xbuf, ybuf, sem_in, sem_out, *, trash_row0):
    i = pl.program_id(0)
    na = na_ref[0]
    e = be_ref[i]
    prev = be_ref[jnp.maximum(i - 1, 0)]
    n_grp = MOE_BM // SEG_ALIGN

    def group_copies(blk, inbound, slot=None):
        slot = blk % 2 if slot is None else slot
        cps = []
        for r in range(n_grp):
            v = grp_ref[blk * n_grp + r]
            vm_rows = pl.ds(r * SEG_ALIGN, SEG_ALIGN)
            if inbound:
                row = pl.multiple_of(jnp.where(v >= 0, v, trash_row0 + 2 * MOE_BM), SEG_ALIGN)
                cps.append(pltpu.make_async_copy(xsl_ref.at[pl.ds(row, SEG_ALIGN), :],
                                                 xbuf.at[slot, vm_rows, :], sem_in.at[slot]))
            else:
                spare = trash_row0 + slot * MOE_BM + r * SEG_ALIGN
                row = pl.multiple_of(jnp.where(v >= 0, v, spare), SEG_ALIGN)
                cps.append(pltpu.make_async_copy(ybuf.at[slot, vm_rows, :],
                                                 ysl_ref.at[pl.ds(row, SEG_ALIGN), :], sem_out.at[slot]))
        return cps

    def start_gather(blk):
        for cp in group_copies(blk, True):
            cp.start()

    def start_scatter(blk):
        for cp in group_copies(blk, False):
            cp.start()

    def wait_rows(blk, sem, inbound):
        slot = blk % 2
        if inbound:
            pltpu.make_async_copy(xsl_ref.at[pl.ds(0, MOE_BM), :], xbuf.at[slot], sem.at[slot]).wait()
        else:
            pltpu.make_async_copy(ybuf.at[slot], ysl_ref.at[pl.ds(0, MOE_BM), :], sem.at[slot]).wait()

    @pl.when(i == 0)
    def _():
        start_gather(i)

    @pl.when(i + 1 < na)
    def _():
        start_gather(i + 1)

    @pl.when((i < na) & ((i == 0) | (e != prev)))
    def _():
        for j in range(2 * D_EXPERT // MOE_CH):
            wgu_bf[:, j * MOE_CH:(j + 1) * MOE_CH] = wgu_ref[:, j * MOE_CH:(j + 1) * MOE_CH].astype(BF16)
        for j in range(D_EXPERT // MOE_CH):
            wdn_bf[j * MOE_CH:(j + 1) * MOE_CH, :] = wdn_ref[j * MOE_CH:(j + 1) * MOE_CH, :].astype(BF16)

    @pl.when(i < na)
    def _():
        slot = i % 2
        wait_rows(i, sem_in, True)

        @pl.when(i >= 2)
        def _():
            wait_rows(i - 2, sem_out, False)

        half = D_MODEL // 2
        xh, xl = _unpack_halves(xbuf[slot])

        def xdot(c0, c1):
            return (jnp.dot(xh, wgu_bf[0:half, c0:c1], preferred_element_type=F32)
                    + jnp.dot(xl, wgu_bf[half:D_MODEL, c0:c1], preferred_element_type=F32))

        acc = jnp.zeros((MOE_BM, D_MODEL), F32) + bdn_ref[...]
        for j in range(D_EXPERT // MOE_CH):
            lo, hi = j * MOE_CH, (j + 1) * MOE_CH
            gj = xdot(lo, hi) + bgu_ref[:, lo:hi]
            uj = xdot(D_EXPERT + lo, D_EXPERT + hi) + bgu_ref[:, D_EXPERT + lo:D_EXPERT + hi]
            gj = jnp.minimum(gj, SWIGLU_LIMIT)
            uj = jnp.clip(uj, -SWIGLU_LIMIT, SWIGLU_LIMIT)
            act = gj * _sigmoid(SWIGLU_ALPHA * gj) * (uj + 1.0)
            acc = acc + jnp.dot(act.astype(BF16), wdn_bf[lo:hi, :], preferred_element_type=F32)
        ybuf[slot] = _pack_halves(acc)
        start_scatter(i)

        @pl.when(i == na - 1)
        def _():
            @pl.when(i >= 1)
            def _():
                wait_rows(i - 1, sem_out, False)
            wait_rows(i, sem_out, False)


def _moe_experts(plan, xsl, w_gu, b_gu, w_dn, b_dn):
    block_e, n_active, grp_rows = plan
    nblk = block_e.shape[0]
    spare_row0 = xsl.shape[0] - MOE_RL
    assert MOE_RL >= 2 * MOE_BM
    wmap = lambda i, be, *_: (be[i], 0, 0)
    anyspec = pl.BlockSpec(memory_space=pl.ANY)
    gs = pltpu.PrefetchScalarGridSpec(
        num_scalar_prefetch=3,
        grid=(nblk,),
        in_specs=[anyspec,
                  pl.BlockSpec((None, D_MODEL, 2 * D_EXPERT), wmap),
                  pl.BlockSpec((None, 1, 2 * D_EXPERT), wmap),
                  pl.BlockSpec((None, D_EXPERT, D_MODEL), wmap),
                  pl.BlockSpec((None, 1, D_MODEL), wmap)],
        out_specs=anyspec,
        scratch_shapes=[pltpu.VMEM((D_MODEL, 2 * D_EXPERT), BF16), pltpu.VMEM((D_EXPERT, D_MODEL), BF16),
                        pltpu.VMEM((2, MOE_BM, D_MODEL // 2), jnp.uint32),
                        pltpu.VMEM((2, MOE_BM, D_MODEL // 2), jnp.uint32),
                        pltpu.SemaphoreType.DMA((2,)), pltpu.SemaphoreType.DMA((2,))],
    )
    return pl.pallas_call(
        functools.partial(_moe_kernel, trash_row0=spare_row0),
        grid_spec=gs,
        out_shape=jax.ShapeDtypeStruct(xsl.shape, jnp.uint32),
        input_output_aliases={3: 0},
        compiler_params=_cparams(("arbitrary",)),
        name="moe_experts",
    )(*plan, xsl, w_gu, b_gu.reshape(N_EXPERTS, 1, -1), w_dn, b_dn.reshape(N_EXPERTS, 1, -1))


def _combine_kernel(ysl_ref, info_ref, x1_ref, mod_ref, y_ref, *, tm):
    info = info_ref[...]
    info_t = jnp.transpose(jnp.concatenate([info, jnp.zeros((LANES - info.shape[0], tm), F32)], axis=0))
    ridx = lax.broadcasted_iota(jnp.int32, (tm, MOE_RL), 1).astype(F32)
    pg = jnp.zeros((tm, MOE_RL), F32)
    for k in range(TOP_K):
        pg = pg + jnp.where(ridx == info_t[:, k:k + 1], info_t[:, TOP_K + k:TOP_K + k + 1], 0.0)
    pgb = pg.astype(BF16)
    yh, yl = _unpack_halves(ysl_ref[...])
    half = D_MODEL // 2
    gt2 = mod_ref[:, 5 * D_MODEL:6 * D_MODEL]
    for c, yy in ((0, yh), (1, yl)):
        moe = jnp.dot(pgb, yy, preferred_element_type=F32)
        y_ref[:, c * half:(c + 1) * half] = (x1_ref[:, c * half:(c + 1) * half]
                                             + gt2[:, c * half:(c + 1) * half] * moe)


def _combine(ysl, info, x1, mod3, tiles_per_mod, tile0=0):
    m = x1.shape[0]
    tm = MOE_TM
    r = mod3.shape[1]
    return pl.pallas_call(
        functools.partial(_combine_kernel, tm=tm),
        grid=(m // tm,),
        in_specs=[pl.BlockSpec((MOE_RL, D_MODEL // 2), lambda i: (tile0 + i, 0)),
                  pl.BlockSpec((16, tm), lambda i: (0, tile0 + i)),
                  pl.BlockSpec((tm, D_MODEL), lambda i: (i, 0)),
                  pl.BlockSpec((None, r, 6 * D_MODEL), lambda i: (i // tiles_per_mod, 0, 0))],
        out_specs=pl.BlockSpec((tm, D_MODEL), lambda i: (i, 0)),
        out_shape=jax.ShapeDtypeStruct((m, D_MODEL), F32),
        compiler_params=_cparams(("parallel",)),
        name="moe_combine",
    )(ysl, info, x1, mod3)


def _moe_plan(cnt):
    cnt = cnt.astype(jnp.int32)
    nt = cnt.shape[0]
    so = jnp.cumsum(cnt, axis=1) - cnt + (jnp.arange(nt) * MOE_RL)[:, None]
    ce = jnp.cumsum(cnt, axis=0)
    cs = ce - cnt
    tot = ce[-1]
    nblk_e = (tot + MOE_BM - 1) // MOE_BM
    blk_end = jnp.cumsum(nblk_e)
    max_rows = nt * MOE_TM * TOP_K + nt * N_EXPERTS * (SEG_ALIGN - 1)
    n_blocks = -(-max_rows // MOE_BM) + N_EXPERTS
    bidx = jnp.arange(n_blocks)
    block_e = jnp.minimum(jnp.sum(blk_end[None, :] <= bidx[:, None], axis=1), N_EXPERTS - 1).astype(jnp.int32)
    is_e = (jnp.arange(N_EXPERTS)[:, None] == block_e[None, :]).astype(jnp.int32)
    per_block = lambda a: jnp.sum(a[..., :, None] * is_e, axis=-2)
    block_r0 = (bidx - per_block(blk_end - nblk_e)) * MOE_BM
    x = block_r0[:, None] + jnp.arange(MOE_BM // SEG_ALIGN)[None, :] * SEG_ALIGN
    ce_b = per_block(ce)[:, :, None]
    cs_b = per_block(cs)[:, :, None]
    inside = (cs_b <= x[None]) & (x[None] < ce_b)
    grp = x + jnp.sum(jnp.where(inside, per_block(so - cs)[:, :, None], 0), axis=0)
    grp = jnp.where(x < per_block(tot)[:, None], grp, -1)
    n_active = blk_end[-1].reshape(1)
    i32 = lambda a: a.reshape(-1).astype(jnp.int32)
    return block_e, i32(n_active), i32(grp)


def _rope_tables(pos):
    half = ROT_DIM // 2
    inv = ROPE_THETA ** (-jnp.arange(half, dtype=F32) * (2.0 / ROT_DIM))
    ang = pos.astype(F32)[:, None] * inv[None, :]
    cos, sin = jnp.cos(ang), jnp.sin(ang)
    n = pos.shape[0]
    ones = jnp.ones((n, A_DH - ROT_DIM), F32)
    zeros_h = jnp.zeros((n, half), F32)
    zeros_r = jnp.zeros((n, A_DH - ROT_DIM), F32)
    cos64 = jnp.concatenate([cos, cos, ones], axis=1)
    sprev64 = jnp.concatenate([zeros_h, sin, zeros_r], axis=1)
    snext64 = jnp.concatenate([-sin, zeros_h, zeros_r], axis=1)
    two = lambda a: jnp.concatenate([a, a], axis=1)
    return two(cos64), two(sprev64), two(snext64)


def _prep_weights(w_in, b_in, q_norm_g, k_norm_g, cmp_pe_k, cmp_pe_v, cmp_w_k, cmp_w_v,
                  w_up_m, w_up_a, w_out, w_router, b_router):
    b2 = b_in.reshape(1, N_IN)
    wm = w_in[:, OFF_MQ:OFF_MO].astype(BF16)
    bm = b2[:, OFF_MQ:OFF_MO]
    wq = w_in[:, OFF_AQ:OFF_AKV].astype(BF16)
    bq = b2[:, OFF_AQ:OFF_AKV]
    wkv = w_in[:, OFF_AKV:OFF_AG].astype(BF16)
    bkv = b2[:, OFF_AKV:OFF_AG]
    n_small = 2 * M_HEADS + 3 * A_HEADS
    ws = jnp.concatenate([w_in[:, OFF_MI:OFF_AQ], w_in[:, OFF_AG:OFF_GA],
                          jnp.zeros((D_MODEL, LANES - n_small), F32)], axis=1)
    bs = jnp.concatenate([b2[:, OFF_MI:OFF_AQ], b2[:, OFF_AG:OFF_GA], jnp.zeros((1, LANES - n_small), F32)], axis=1)
    qg = jnp.tile(q_norm_g, A_HEADS).reshape(1, A_WIDTH)
    kg = jnp.stack([jnp.tile(k_norm_g[1], A_KV), jnp.tile(k_norm_g[2], A_KV)], axis=0)
    kg0 = jnp.tile(k_norm_g[0], A_KV).reshape(1, LANES)
    hid = jnp.arange(A_WIDTH) // A_DH
    bd = jnp.where(hid[:, None] == hid[None, :], 1.0 / A_DH, 0.0).astype(BF16)
    inproj_w = (wm, bm, wq, bq, wkv, bkv, ws, bs, qg, kg, bd)

    z = jnp.zeros((CMP_LEN, A_DH, A_DH), F32)
    r0 = jnp.concatenate([cmp_w_k, z, z, z], axis=2)
    r1 = jnp.concatenate([z, cmp_w_k, z, z], axis=2)
    r2 = jnp.concatenate([z, z, cmp_w_v, z], axis=2)
    r3 = jnp.concatenate([z, z, z, cmp_w_v], axis=2)
    wbd = jnp.concatenate([r0, r1, r2, r3], axis=1).astype(BF16)
    pe = jnp.concatenate([cmp_pe_k, cmp_pe_k, cmp_pe_v, cmp_pe_v], axis=1).reshape(CMP_LEN, 1, 2 * LANES)

    wog = jnp.concatenate([w_in[:, OFF_MO:OFF_MI], w_in[:, OFF_GA:N_IN]], axis=1).astype(BF16)
    bog = jnp.concatenate([b2[:, OFF_MO:OFF_MI], b2[:, OFF_GA:N_IN]], axis=1)
    mixout_w = (wog, bog, w_up_m.astype(BF16), w_up_a.astype(BF16), w_out.astype(BF16),
                w_router.T, b_router.reshape(N_EXPERTS, 1))
    return inproj_w, (wbd, pe, kg0), mixout_w


def _pick_tile(m, pref):
    t = pref
    while m % t:
        t //= 2
    return t


def kernel(x_prompt, x_sample, cache_nsa_kv, state_win_kv, state_mlstm_C, state_mlstm_n, state_mlstm_m, page_table, c_prompt, c_sample, w_ada, b_ada, g_mix, g_ffn, w_in, b_in, q_norm_g, k_norm_g, cmp_pe_k, cmp_pe_v, cmp_w_k, cmp_w_v, w_up_m, w_up_a, w_out, w_router, b_router, w_gu, b_gu, w_dn, b_dn):
    depth = w_in.shape[0]
    assert depth == 1
    B, T, D = x_prompt.shape
    DB, TS, _ = x_sample.shape
    n_pages = page_table.shape[1]
    past_len = n_pages * PAGE_SIZE
    wbuf = state_win_kv.shape[2]
    tp = SAMPLE_PAD_T
    assert TS <= tp and wbuf % tp == 0 and T % 128 == 0

    l = 0
    inproj_w, cmp_w, mixout_w = _prep_weights(
        w_in[l], b_in[l], q_norm_g[l], k_norm_g[l], cmp_pe_k[l], cmp_pe_v[l], cmp_w_k[l], cmp_w_v[l],
        w_up_m[l], w_up_a[l], w_out[l], w_router[l], b_router[l])
    wbd, pe, kg0 = cmp_w
    gmix = g_mix[l].reshape(1, D)
    gffn = g_ffn[l].reshape(1, D)

    nc = B + DB
    nc_pad = -(-nc // SUBLANES) * SUBLANES
    c_all = jnp.concatenate([c_prompt, c_sample, jnp.zeros((nc_pad - nc, D), F32)], axis=0)
    mod = _adaln(c_all, w_ada[l], b_ada[l])
    mod_p = mod[:B].reshape(B, 1, 6 * D)
    mod_s = jnp.repeat(mod[B:B + DB], tp, axis=0).reshape(1, DB * tp, 6 * D)

    mp = B * T
    tm = _pick_tile(T, 256)
    xp = x_prompt.reshape(mp, D)
    tabs_p = _rope_tables(jnp.arange(T, dtype=jnp.int32))
    mq, mk, mv, q, qr, rows, win, small, rows_t = _inproj(xp, mod_p, gmix, tabs_p, inproj_w, tm, T // tm, T // tm,
                                                          rows_t_batches=B)
    Lp = _pick_tile(T, 128)
    hm, C_p, n_p, m_p = _mlstm(mq, mk, mv, small, B, T, T, Lp)
    o_nsa = _nsa_prompt(q, qr, small, rows, win, wbd, pe, kg0, B, T)
    assert T % MOE_TM == 0
    ms_pad = -(-(DB * tp) // MOE_TM) * MOE_TM
    nt_p = mp // MOE_TM
    nt_all = nt_p + ms_pad // MOE_TM + 1
    x1_p, xsl, info, cnt = _mixout(xp, hm, o_nsa, mod_p, gmix, gffn, mixout_w, T // MOE_TM, nt_all)

    ms = DB * tp
    xs_pad = jnp.concatenate([x_sample, jnp.zeros((DB, tp - TS, D), F32)], axis=1).reshape(ms, D)
    pos_s = past_len + jnp.tile(jnp.arange(tp, dtype=jnp.int32), DB)
    tabs_s = _rope_tables(pos_s)
    mq_s, mk_s, mv_s, q_s, qr_s, rows_s, win_s, small_s = _inproj(xs_pad, mod_s, gmix, tabs_s, inproj_w, ms, 1, 1)
    hm_s, C_s, n_s, m_s = _mlstm(mq_s, mk_s, mv_s, small_s, DB, tp, TS, tp,
                                 state=(state_mlstm_C[l], state_mlstm_n[l], state_mlstm_m[l]))
    cache2 = jnp.transpose(cache_nsa_kv[l], (0, 2, 3, 4, 1)).reshape(cache_nsa_kv.shape[1], 4 * LANES, PAGE_SIZE)
    winbuf = jnp.transpose(state_win_kv[l], (0, 2, 3, 4, 1)).reshape(DB, 2 * LANES, wbuf)
    o_nsa_s, win_out_s = _nsa_sample(page_table, cache2, q_s, qr_s, small_s, rows_s, win_s, winbuf,
                                     wbd, pe, kg0, TS)
    assert ms_pad == MOE_TM
    rpad = lambda a: jnp.concatenate([a, jnp.zeros((ms_pad - ms, a.shape[1]), a.dtype)], axis=0) if ms_pad > ms else a
    mod_sp = rpad(mod_s[0])[None]
    x1_s, xsl, info, cnt = _mixout(rpad(xs_pad), rpad(hm_s), rpad(o_nsa_s), mod_sp, gmix, gffn, mixout_w,
                                   1, nt_all, tile0=nt_p, shared=(xsl, info, cnt),
                                   t_mod=tp, t_valid=TS, m_valid=ms)

    ysl = _moe_experts(_moe_plan(cnt[:, :, 0]), xsl, w_gu[l], b_gu[l], w_dn[l], b_dn[l])
    y_p = _combine(ysl, info, x1_p, mod_p, T // MOE_TM).reshape(B, T, D)
    y_s_all = _combine(ysl, info, x1_s, mod_sp, 1, tile0=nt_p)
    valid = lambda a: a.reshape(DB, tp, -1)[:, :TS].reshape(DB * TS, -1)
    y_s = valid(y_s_all[:ms]).reshape(DB, TS, D)

    kv_p = jnp.transpose(rows_t.reshape(B, 4, A_KV, A_DH, T), (0, 4, 1, 2, 3))[None]
    kv_s = valid(rows_s).reshape(1, DB, TS, 4, A_KV, A_DH)
    wp = min(WINDOW, T)
    win_p = win.reshape(B, T, 2, A_KV, A_DH)[:, T - wp:][None]
    win_s_out = jnp.transpose(win_out_s.reshape(DB, 2, A_KV, A_DH, wbuf), (0, 4, 1, 2, 3))[None]
    return (y_p, y_s, kv_p, kv_s, win_p, win_s_out,
            C_p[None], n_p[None], m_p[None], C_s[None], n_s[None], m_s[None])
```

```python
import functools
import math

import jax
import jax.numpy as jnp
from jax import lax
from jax.experimental import pallas as pl
from jax.experimental.pallas import tpu as pltpu

F32 = jnp.float32
BF16 = jnp.bfloat16

D_MODEL = 1024
M_HEADS = 4
M_DH = 128
M_WIDTH = M_HEADS * M_DH
A_HEADS = 8
A_KV = 2
A_HPG = A_HEADS // A_KV
A_DH = 64
A_WIDTH = A_HEADS * A_DH
CMP_STRIDE = 16
CMP_LEN = 32
SEL_LEN = 64
N_SEL = 16
WINDOW = 512
PAGE_SIZE = 128
ROPE_THETA = 500000.0
ROT_DIM = A_DH // 4
ATT_SCALE = A_DH ** -0.5
N_EXPERTS = 32
TOP_K = 4
D_EXPERT = D_MODEL
SWIGLU_LIMIT = 7.0
SWIGLU_ALPHA = 1.702
EPS = 1e-6

OFF_MQ, OFF_MK, OFF_MV, OFF_MO = 0, M_WIDTH, 2 * M_WIDTH, 3 * M_WIDTH
OFF_MI = 4 * M_WIDTH
OFF_MF = OFF_MI + M_HEADS
OFF_AQ = OFF_MF + M_HEADS
OFF_AKV = OFF_AQ + A_WIDTH
OFF_AG = OFF_AKV + 6 * A_KV * A_DH
OFF_GA = OFF_AG + 3 * A_HEADS
OFF_GB = OFF_GA + D_MODEL
N_IN = OFF_GB + D_MODEL

LANES = 128
SUBLANES = 8
VMEM_LIMIT = 56 * 1024 * 1024

NEG_BIG = -1e30
M_INIT = -1e29
LOG2E = 1.4426950408889634
SAMPLE_PAD_T = 8


def _cparams(sem):
    return pltpu.CompilerParams(dimension_semantics=sem, vmem_limit_bytes=VMEM_LIMIT)


def _bdot(a, b):
    return jnp.dot(a.astype(BF16), b.astype(BF16), preferred_element_type=F32)


def _bdot_t(a, b):
    return lax.dot_general(a.astype(BF16), b.astype(BF16), (((1,), (1,)), ((), ())),
                           preferred_element_type=F32)


def _split(a):
    hi = a.astype(BF16)
    lo = (a - hi.astype(F32)).astype(BF16)
    return hi, lo


def _dot3(a, b):
    ah, al = _split(a)
    bh, bl = _split(b)
    return (jnp.dot(ah, bh, preferred_element_type=F32) + jnp.dot(al, bh, preferred_element_type=F32)
            + jnp.dot(ah, bl, preferred_element_type=F32))


def _dot2_exact_rhs(a, b_bf16):
    ah, al = _split(a)
    return jnp.dot(ah, b_bf16, preferred_element_type=F32) + jnp.dot(al, b_bf16, preferred_element_type=F32)


def _sigmoid(x):
    return 0.5 * jnp.tanh(0.5 * x) + 0.5


def _rmsnorm_rows(x, g):
    return x * lax.rsqrt(jnp.mean(x * x, axis=-1, keepdims=True) + EPS) * g


def _adaln_kernel(c_ref, w_ref, b_ref, o_ref):
    c = c_ref[...]
    s = c * _sigmoid(c)
    o_ref[...] = _dot3(s, w_ref[...]) + b_ref[...]


def _adaln(c, w, b):
    mc, d = c.shape
    n = w.shape[1]
    tn = 1024
    return pl.pallas_call(
        _adaln_kernel,
        grid=(n // tn,),
        in_specs=[pl.BlockSpec((mc, d), lambda j: (0, 0)),
                  pl.BlockSpec((d, tn), lambda j: (0, j)),
                  pl.BlockSpec((1, tn), lambda j: (0, j))],
        out_specs=pl.BlockSpec((mc, tn), lambda j: (0, j)),
        out_shape=jax.ShapeDtypeStruct((mc, n), F32),
        compiler_params=_cparams(("parallel",)),
        name="adaln",
    )(c, w, b.reshape(1, n))


def _head_norm(z, bd, gain):
    ms = _dot2_exact_rhs(z * z, bd)
    return z * lax.rsqrt(ms + EPS) * gain


def _rope(z, cos, s_prev, s_next):
    w = z.shape[1]
    rep = w // LANES
    if rep > 1:
        cos = jnp.concatenate([cos] * rep, axis=1)
        s_prev = jnp.concatenate([s_prev] * rep, axis=1)
        s_next = jnp.concatenate([s_next] * rep, axis=1)
    z_prev = pltpu.roll(z, ROT_DIM // 2, 1)
    z_next = pltpu.roll(z, w - ROT_DIM // 2, 1)
    return z * cos + z_prev * s_prev + z_next * s_next


def _inproj_kernel(x_ref, mod_ref, gmix_ref, cos_ref, sp_ref, sn_ref,
                   wm_ref, bm_ref, wq_ref, bq_ref, wkv_ref, bkv_ref, ws_ref, bs_ref,
                   qg_ref, kg_ref, bd_ref,
                   mq_ref, mk_ref, mv_ref, q_ref, qr_ref, rows_ref, win_ref, small_ref, rows_t_ref=None):
    x = x_ref[...]
    sh1 = mod_ref[:, 0:D_MODEL]
    sc1 = mod_ref[:, D_MODEL:2 * D_MODEL]
    h = _rmsnorm_rows(x, gmix_ref[...]) * (1.0 + sc1) + sh1
    hb = h.astype(BF16)

    mq_ref[...] = jnp.dot(hb, wm_ref[:, 0:M_WIDTH], preferred_element_type=F32) + bm_ref[:, 0:M_WIDTH]
    mk = jnp.dot(hb, wm_ref[:, M_WIDTH:2 * M_WIDTH], preferred_element_type=F32) + bm_ref[:, M_WIDTH:2 * M_WIDTH]
    mk_ref[...] = mk * (M_DH ** -0.5)
    mv_ref[...] = (jnp.dot(hb, wm_ref[:, 2 * M_WIDTH:3 * M_WIDTH], preferred_element_type=F32)
                   + bm_ref[:, 2 * M_WIDTH:3 * M_WIDTH])

    cos, sp, sn = cos_ref[...], sp_ref[...], sn_ref[...]
    zq = jnp.dot(hb, wq_ref[...], preferred_element_type=F32) + bq_ref[...]
    qn = _head_norm(zq, bd_ref[...], qg_ref[...])
    q_ref[...] = qn
    qr_ref[...] = _rope(qn, cos, sp, sn)

    zkv = jnp.dot(hb, wkv_ref[...], preferred_element_type=F32) + bkv_ref[...]
    bd2 = bd_ref[0:LANES, 0:LANES]
    ksel = _head_norm(zkv[:, 2 * LANES:3 * LANES], bd2, kg_ref[0:1, :])
    rows = jnp.concatenate([zkv[:, 0:2 * LANES], _rope(ksel, cos, sp, sn), zkv[:, 3 * LANES:4 * LANES]], axis=1)
    rows_ref[...] = rows
    if rows_t_ref is not None:
        rows_t_ref[...] = jnp.transpose(rows)
    kwin = _head_norm(zkv[:, 4 * LANES:5 * LANES], bd2, kg_ref[1:2, :])
    win_ref[:, 0:LANES] = _rope(kwin, cos, sp, sn)
    win_ref[:, LANES:2 * LANES] = zkv[:, 5 * LANES:6 * LANES]

    small_ref[...] = _dot3(h, ws_ref[...]) + bs_ref[...]


def _inproj(x2, mod3, gmix, tabs, wts, tm, tiles_per_mod, pos_tiles, rows_t_batches=None):
    m = x2.shape[0]
    cos_t, sp_t, sn_t = tabs
    (wm, bm, wq, bq, wkv, bkv, ws, bs, qg, kg, bd) = wts
    r = mod3.shape[1]
    row = lambda i: (i, 0)
    const = lambda i: (0, 0)
    tab = lambda i: (i % pos_tiles, 0)
    in_specs = [
        pl.BlockSpec((tm, D_MODEL), row),
        pl.BlockSpec((None, r, 6 * D_MODEL), lambda i: (i // tiles_per_mod, 0, 0)),
        pl.BlockSpec((1, D_MODEL), const),
        pl.BlockSpec((tm, LANES), tab), pl.BlockSpec((tm, LANES), tab), pl.BlockSpec((tm, LANES), tab),
        pl.BlockSpec(wm.shape, const), pl.BlockSpec(bm.shape, const),
        pl.BlockSpec(wq.shape, const), pl.BlockSpec(bq.shape, const),
        pl.BlockSpec(wkv.shape, const), pl.BlockSpec(bkv.shape, const),
        pl.BlockSpec(ws.shape, const), pl.BlockSpec(bs.shape, const),
        pl.BlockSpec(qg.shape, const), pl.BlockSpec(kg.shape, const), pl.BlockSpec(bd.shape, const),
    ]
    widths = (M_WIDTH, M_WIDTH, M_WIDTH, A_WIDTH, A_WIDTH, 4 * LANES, 2 * LANES, LANES)
    out_specs = [pl.BlockSpec((tm, w), row) for w in widths]
    out_shape = [jax.ShapeDtypeStruct((m, w), F32) for w in widths]
    if rows_t_batches is not None:
        out_specs.append(pl.BlockSpec((None, 4 * LANES, tm), lambda i: (i // tiles_per_mod, 0, i % tiles_per_mod)))
        out_shape.append(jax.ShapeDtypeStruct((rows_t_batches, 4 * LANES, m // rows_t_batches), F32))
    return pl.pallas_call(
        _inproj_kernel,
        grid=(m // tm,),
        in_specs=in_specs,
        out_specs=out_specs,
        out_shape=out_shape,
        compiler_params=_cparams(("parallel",)),
        name="inproj",
    )(x2, mod3, gmix, cos_t, sp_t, sn_t, wm, bm, wq, bq, wkv, bkv, ws, bs, qg, kg, bd)


def _log_sigmoid(x):
    return jnp.minimum(x, 0.0) - jnp.log(1.0 + jnp.exp(-jnp.abs(x)))


def _mlstm_kernel(*refs, L, t_valid, has_state):
    if has_state:
        q_ref, k_ref, v_ref, s_ref, c0_ref, n0_ref, m0_ref, h_ref, c_ref, n_ref, m_ref = refs
    else:
        q_ref, k_ref, v_ref, s_ref, h_ref, c_ref, n_ref, m_ref = refs
    c = pl.program_id(1)

    @pl.when(c == 0)
    def _():
        if has_state:
            c_ref[...] = c0_ref[...]
            n_ref[...] = n0_ref[...]
            m_ref[...] = m0_ref[...]
        else:
            c_ref[...] = jnp.zeros(c_ref.shape, F32)
            n_ref[...] = jnp.zeros(n_ref.shape, F32)
            m_ref[...] = jnp.zeros(m_ref.shape, F32)

    row = lax.broadcasted_iota(jnp.int32, (L, L), 0)
    col = lax.broadcasted_iota(jnp.int32, (L, L), 1)
    causal = col <= row
    eye = col == row
    tok_col = c * L + lax.broadcasted_iota(jnp.int32, (L, 1), 0)
    valid_col = tok_col < t_valid
    for hd in range(M_HEADS):
        lo, hi = hd * M_DH, (hd + 1) * M_DH
        q = q_ref[:, lo:hi]
        k = k_ref[:, lo:hi]
        v = v_ref[:, lo:hi]
        i_col = s_ref[:, hd:hd + 1]
        lf_col = _log_sigmoid(s_ref[:, M_HEADS + hd:M_HEADS + hd + 1])
        lf_col = jnp.where(valid_col, lf_col, 0.0)
        i_col = jnp.where(valid_col, i_col, -jnp.inf)
        if L == LANES:
            i_col = jnp.broadcast_to(i_col, (L, L))
            lf_c = jnp.broadcast_to(lf_col, (L, L))
            p0 = lf_c.astype(BF16)
            r1 = lf_c - p0.astype(F32)
            p1 = r1.astype(BF16)
            p2 = (r1 - p1.astype(F32)).astype(BF16)
            tril = jnp.where(causal, 1.0, 0.0).astype(BF16)
            b_col = (jnp.dot(tril, p0, preferred_element_type=F32) + jnp.dot(tril, p1, preferred_element_type=F32)
                     + jnp.dot(tril, p2, preferred_element_type=F32))
            i_row = jnp.transpose(i_col)[0:1, :]
            b_row = jnp.transpose(b_col)[0:1, :]
        else:
            i_row = jnp.sum(jnp.where(eye, i_col, 0.0), axis=0, keepdims=True)
            lf_row = jnp.sum(jnp.where(eye, lf_col, 0.0), axis=0, keepdims=True)
            b_col = jnp.sum(jnp.where(causal, lf_row, 0.0), axis=1, keepdims=True)
            b_row = jnp.sum(jnp.where(row <= col, lf_col, 0.0), axis=0, keepdims=True)
        m_prev = m_ref[:, hd:hd + 1]
        dmat = jnp.where(causal, b_col - b_row + i_row, -jnp.inf)
        inter = b_col + m_prev
        m_row = jnp.maximum(jnp.max(dmat, axis=1, keepdims=True), inter)
        w = jnp.exp(dmat - m_row)
        w_inter = jnp.exp(inter - m_row)
        s = _bdot_t(q, k) * w
        cm = c_ref[hd]
        nv = n_ref[hd]
        num = _bdot(s, v) + w_inter * _bdot_t(q, cm)
        den = jnp.sum(s, axis=1, keepdims=True) + w_inter * jnp.sum(q * nv, axis=1, keepdims=True)
        h_ref[:, lo:hi] = num / jnp.maximum(jnp.abs(den), jnp.exp(-m_row))
        b_last = b_col[L - 1:L, 0:1]
        dec_col = b_last - b_col + i_col
        dec_row = b_last - b_row + i_row
        m_new = jnp.maximum(b_last + m_prev, jnp.max(dec_row, axis=1, keepdims=True))
        ws_col = jnp.exp(dec_col - m_new)
        wc = jnp.exp(b_last + m_prev - m_new)
        vw = (v * ws_col).astype(BF16)
        upd = lax.dot_general(vw, k.astype(BF16), (((0,), (0,)), ((), ())), preferred_element_type=F32)
        c_ref[hd] = wc * cm + upd
        n_ref[hd] = wc * nv + jnp.sum(k * ws_col, axis=0, keepdims=True)
        m_ref[:, hd:hd + 1] = m_new


def _mlstm(mq, mk, mv, small, nb, t_pad, t_valid, L, state=None):
    nc = t_pad // L
    has_state = state is not None
    blk = lambda b, c: (b * nc + c, 0)
    st4 = lambda b, c: (b, 0, 0, 0)
    st3 = lambda b, c: (b, 0, 0)
    in_specs = [pl.BlockSpec((L, M_WIDTH), blk)] * 3 + [pl.BlockSpec((L, LANES), blk)]
    args = [mq, mk, mv, small]
    if has_state:
        c0, n0, m0 = state
        in_specs += [pl.BlockSpec((None, M_HEADS, M_DH, M_DH), st4),
                     pl.BlockSpec((None, M_HEADS, 1, M_DH), st4),
                     pl.BlockSpec((None, 1, M_HEADS), st3)]
        args += [c0, n0.reshape(nb, M_HEADS, 1, M_DH), m0.reshape(nb, 1, M_HEADS)]
    out_specs = [pl.BlockSpec((L, M_WIDTH), blk),
                 pl.BlockSpec((None, M_HEADS, M_DH, M_DH), st4),
                 pl.BlockSpec((None, M_HEADS, 1, M_DH), st4),
                 pl.BlockSpec((None, 1, M_HEADS), st3)]
    out_shape = [jax.ShapeDtypeStruct((nb * t_pad, M_WIDTH), F32),
                 jax.ShapeDtypeStruct((nb, M_HEADS, M_DH, M_DH), F32),
                 jax.ShapeDtypeStruct((nb, M_HEADS, 1, M_DH), F32),
                 jax.ShapeDtypeStruct((nb, 1, M_HEADS), F32)]
    h, cs, ns, ms = pl.pallas_call(
        functools.partial(_mlstm_kernel, L=L, t_valid=t_valid, has_state=has_state),
        grid=(nb, nc),
        in_specs=in_specs,
        out_specs=out_specs,
        out_shape=out_shape,
        compiler_params=_cparams(("parallel", "arbitrary")),
        name="mlstm",
    )(*args)
    return h, cs, ns.reshape(nb, M_HEADS, M_DH), ms.reshape(nb, M_HEADS)


def _stack_heads(qt, g):
    t = qt.shape[0]
    z = jnp.zeros((t, A_DH), F32)
    parts = []
    for hh in range(A_HPG):
        hd = g * A_HPG + hh
        qh = qt[:, hd * A_DH:(hd + 1) * A_DH] * (ATT_SCALE * LOG2E)
        parts.append(jnp.concatenate([qh, z], axis=1) if g == 0 else jnp.concatenate([z, qh], axis=1))
    return jnp.concatenate(parts, axis=0).astype(BF16)


def _gate_cols(small, g, br):
    cols = []
    for hh in range(A_HPG):
        c0 = 2 * M_HEADS + (g * A_HPG + hh) * 3 + br
        cols.append(_sigmoid(small[:, c0:c0 + 1]))
    return jnp.concatenate(cols, axis=0)


def _compress(k_ref, v_ref, nseg, wbd_ref, pe_ref, kg0):
    acc_lo = jnp.zeros((nseg, 2 * LANES), F32)
    acc_hi = jnp.zeros((nseg, 2 * LANES), F32)
    for l in range(CMP_STRIDE):
        xl = jnp.concatenate([k_ref[pl.ds(l, nseg, stride=CMP_STRIDE), :],
                              v_ref[pl.ds(l, nseg, stride=CMP_STRIDE), :]], axis=1)
        acc_lo = acc_lo + _bdot(xl + pe_ref[l], wbd_ref[l])
        acc_hi = acc_hi + _bdot(xl + pe_ref[CMP_STRIDE + l], wbd_ref[CMP_STRIDE + l])
    return _compress_finish(acc_lo, acc_hi, nseg, kg0)


def _compress_grouped(x_ref, nseg, wbd_ref, pe_ref, kg0):
    acc_lo = jnp.zeros((nseg, 2 * LANES), F32)
    acc_hi = jnp.zeros((nseg, 2 * LANES), F32)
    pe_lo = jnp.zeros((SUBLANES, 2 * LANES), F32)
    pe_hi = jnp.zeros((SUBLANES, 2 * LANES), F32)
    for l in range(CMP_STRIDE):
        xl = x_ref[l].astype(BF16)
        acc_lo = acc_lo + jnp.dot(xl, wbd_ref[l], preferred_element_type=F32)
        acc_hi = acc_hi + jnp.dot(xl, wbd_ref[CMP_STRIDE + l], preferred_element_type=F32)
        pe_lo = pe_lo + _bdot(jnp.broadcast_to(pe_ref[l], (SUBLANES, 2 * LANES)), wbd_ref[l])
        pe_hi = pe_hi + _bdot(jnp.broadcast_to(pe_ref[CMP_STRIDE + l], (SUBLANES, 2 * LANES)),
                              wbd_ref[CMP_STRIDE + l])
    return _compress_finish(acc_lo + pe_lo[0:1, :], acc_hi + pe_hi[0:1, :], nseg, kg0)


def _compress_finish(acc_lo, acc_hi, nseg, kg0):
    kv = acc_lo + pltpu.roll(acc_hi, nseg - 1, 0)
    kc = kv[:, 0:LANES]
    vc = kv[:, LANES:2 * LANES]
    lane = lax.broadcasted_iota(jnp.int32, (nseg, LANES), 1)
    sq = kc * kc
    ms0 = jnp.sum(jnp.where(lane < A_DH, sq, 0.0), axis=1, keepdims=True) * (1.0 / A_DH)
    ms1 = jnp.sum(jnp.where(lane >= A_DH, sq, 0.0), axis=1, keepdims=True) * (1.0 / A_DH)
    ms = jnp.where(lane < A_DH, ms0, ms1)
    kc = kc * lax.rsqrt(ms + EPS) * kg0
    return kc, vc


def _cmp_branch(qn_g, kc_b, vc_b, tpos_rows, nseg, n_tok):
    s = _bdot_t(qn_g, kc_b)
    nidx = lax.broadcasted_iota(jnp.int32, (1, nseg), 1)
    vis = (nidx * CMP_STRIDE + (CMP_LEN - 1)) <= tpos_rows
    sm = jnp.where(vis, s, NEG_BIG)
    mx = jnp.max(sm, axis=1, keepdims=True)
    e = jnp.where(vis, jnp.exp2(sm - mx), 0.0)
    d = jnp.sum(e, axis=1, keepdims=True)
    p = e / jnp.where(d > 0, d, 1.0)
    o = _bdot(p, vc_b)
    imp = p[0:n_tok]
    for hh in range(1, A_HPG):
        imp = imp + p[hh * n_tok:(hh + 1) * n_tok]
    return o, imp


def _masked_attn_direct(q_g, k_parts, v_parts, allowed_parts, feature_major):
    ss = [jnp.where(al, _bdot(q_g, kk) if fm else _bdot_t(q_g, kk), NEG_BIG)
          for kk, al, fm in zip(k_parts, allowed_parts, feature_major)]
    mx = ss[0].max(axis=1, keepdims=True)
    for s in ss[1:]:
        mx = jnp.maximum(mx, s.max(axis=1, keepdims=True))
    num = None
    den = None
    for s, al, vv, fm in zip(ss, allowed_parts, v_parts, feature_major):
        e = jnp.where(al, jnp.exp2(s - mx), 0.0)
        dd = jnp.sum(e, axis=1, keepdims=True)
        oo = _bdot_t(e, vv) if fm else _bdot(e, vv)
        num = oo if num is None else num + oo
        den = dd if den is None else den + dd
    return num / jnp.where(den > 0, den, 1.0)


def _assemble_heads(o_groups, n_tok):
    pieces = []
    for g in range(A_KV):
        for hh in range(A_HPG):
            pieces.append(o_groups[g][hh * n_tok:(hh + 1) * n_tok, g * A_DH:(g + 1) * A_DH])
    return jnp.concatenate(pieces, axis=1)


def _lane_rep(a, rep):
    return a if rep == 1 else jnp.concatenate([a] * rep, axis=1)


def _nsa_prompt_kernel(q_ref, qr_ref, small_ref, rows_ref, win_ref, wbd_ref, pe_ref, kg0_ref,
                       pool_ref, o_ref,
                       kraw_sc, vraw_sc, kc_sc, vct_sc, sel_sc, m_sc, acc_sc, s_sc, *, T, tq, kc_len):
    qi = pl.program_id(1)
    nseg = T // CMP_STRIDE
    nsb = T // SEL_LEN
    bpc = kc_len // SEL_LEN

    @pl.when(qi == 0)
    def _():
        kraw_sc[...] = rows_ref[:, 0:LANES]
        vraw_sc[...] = rows_ref[:, LANES:2 * LANES]
        kc, vc = _compress(kraw_sc, vraw_sc, nseg, wbd_ref, pe_ref, kg0_ref[...])
        kc_sc[...] = kc
        vct_sc[...] = jnp.transpose(vc)

    t0 = qi * tq
    tpos = t0 + lax.broadcasted_iota(jnp.int32, (1, tq), 1)
    tpos4 = _lane_rep(tpos, A_HPG)
    q = q_ref[...]
    qr = qr_ref[...]
    small_t = jnp.transpose(small_ref[...])
    kc_b = kc_sc[...].astype(BF16)
    vct_b = vct_sc[...].astype(BF16)
    bidx = lax.broadcasted_iota(jnp.int32, (nsb, tq), 0)
    cur = tpos // SEL_LEN
    vis = (lax.broadcasted_iota(jnp.int32, (nseg, 1), 0) * CMP_STRIDE + (CMP_LEN - 1)) <= tpos4
    qr_gs = [_stack_heads(qr, g) for g in range(A_KV)]
    o_cmps = []
    for g in range(A_KV):
        sm = jnp.where(vis, _bdot_t(kc_b, _stack_heads(q, g)), NEG_BIG)
        mx = jnp.max(sm, axis=0, keepdims=True)
        e = jnp.where(vis, jnp.exp2(sm - mx), 0.0)
        d = jnp.sum(e, axis=0, keepdims=True)
        p = e / jnp.where(d > 0, d, 1.0)
        o_cmps.append(jnp.dot(vct_b, p.astype(BF16), preferred_element_type=F32))
        imp = p[:, 0:tq]
        for hh in range(1, A_HPG):
            imp = imp + p[:, hh * tq:(hh + 1) * tq]
        ih, il = _split(imp)
        imp_t = (jnp.dot(pool_ref[...], ih, preferred_element_type=F32)
                 + jnp.dot(pool_ref[...], il, preferred_element_type=F32))[0:nsb]
        val = jnp.where(bidx < cur, imp_t, -1.0)
        rank = jnp.zeros((nsb, tq), F32)
        for bp in range(nsb):
            vb = val[bp:bp + 1, :]
            rank = rank + jnp.where(vb > val, 1.0, jnp.where((vb == val) & (bidx > bp), 1.0, 0.0))
        sel_sc[g] = jnp.where(((rank < (N_SEL - 1)) & (bidx < cur)) | (bidx == cur), 1.0, 0.0)

    m_sc[...] = jnp.full(m_sc.shape, M_INIT, F32)
    acc_sc[...] = jnp.zeros(acc_sc.shape, F32)

    def with_ones_row(vt_, g):
        vb = vt_.astype(BF16)
        r0, pad = (1 - g) * A_DH, 2 * SUBLANES
        ones = jnp.ones((pad, vb.shape[1]), BF16)
        return jnp.concatenate(([vb[0:r0]] if r0 else []) + [ones, vb[r0 + pad:]], axis=0)

    def sel_body(c, carry):
        k0 = pl.multiple_of(c * kc_len, kc_len)
        kb = rows_ref[pl.ds(k0, kc_len), 2 * LANES:3 * LANES].astype(BF16)
        vt = jnp.transpose(rows_ref[pl.ds(k0, kc_len), 3 * LANES:4 * LANES])
        causal = (k0 + lax.broadcasted_iota(jnp.int32, (kc_len, 1), 0)) <= tpos
        for g in range(A_KV):
            s_sc[g, 0:kc_len, :] = _bdot_t(kb, qr_gs[g])
        for g in range(A_KV):
            selc = sel_sc[g, pl.ds(pl.multiple_of(c * bpc, bpc), bpc), :]
            selx = jnp.concatenate([jnp.broadcast_to(selc[j:j + 1, :], (SEL_LEN, tq)) for j in range(bpc)], axis=0)
            bias = jnp.where(causal & (selx > 0.5), 0.0, NEG_BIG)
            sm = s_sc[g, 0:kc_len, :] + _lane_rep(bias, A_HPG)
            m_prev = m_sc[g]
            m_new = jnp.maximum(m_prev, jnp.max(sm, axis=0, keepdims=True))
            alpha = jnp.exp2(m_prev - m_new)
            p = jnp.exp2(sm - m_new)
            acc_sc[g] = alpha * acc_sc[g] + jnp.dot(with_ones_row(vt, g), p.astype(BF16),
                                                    preferred_element_type=F32)
            m_sc[g] = m_new
        return carry

    lax.fori_loop(0, (t0 + tq + kc_len - 1) // kc_len, sel_body, 0)

    wk = min(WINDOW + tq, T)
    w0 = pl.multiple_of(jnp.clip(t0 + tq - wk, 0, T - wk), tq)
    kw = win_ref[pl.ds(w0, wk), 0:LANES].astype(BF16)
    vwt = jnp.transpose(win_ref[pl.ds(w0, wk), LANES:2 * LANES])
    wdiff = tpos - (w0 + lax.broadcasted_iota(jnp.int32, (wk, 1), 0))
    wbias = _lane_rep(jnp.where((wdiff >= 0) & (wdiff < WINDOW), 0.0, NEG_BIG), A_HPG)

    def gate_row(g, br):
        cols = [2 * M_HEADS + (g * A_HPG + hh) * 3 + br for hh in range(A_HPG)]
        return jnp.concatenate([_sigmoid(small_t[c0:c0 + 1, :]) for c0 in cols], axis=1)

    for g in range(A_KV):
        s_sc[g, 0:wk, :] = _bdot_t(kw, qr_gs[g])
    o_ts = []
    for g in range(A_KV):
        den = (1 - g) * A_DH
        acc = acc_sc[g]
        l = acc[den:den + 1, :]
        o_sel = acc / jnp.where(l > 0, l, 1.0)
        sw = s_sc[g, 0:wk, :] + wbias
        pw = jnp.exp2(sw - jnp.max(sw, axis=0, keepdims=True))
        ow = jnp.dot(with_ones_row(vwt, g), pw.astype(BF16), preferred_element_type=F32)
        o_win = ow / ow[den:den + 1, :]
        o_ts.append(gate_row(g, 0) * o_cmps[g] + gate_row(g, 1) * o_sel + gate_row(g, 2) * o_win)
    for j in range(A_HEADS // 2):
        g, h0 = j // (A_HPG // 2), 2 * (j % (A_HPG // 2))
        og = o_ts[g][g * A_DH:(g + 1) * A_DH, :]
        pair = jnp.concatenate([og[:, h0 * tq:(h0 + 1) * tq], og[:, (h0 + 1) * tq:(h0 + 2) * tq]], axis=0)
        o_ref[:, j * LANES:(j + 1) * LANES] = jnp.transpose(pair)


def _nsa_prompt(q, qr, small, rows, win, wbd, pe, kg0, nb, T):
    tq = 128
    kc_len = _pick_tile(T, 512)
    nq = T // tq
    nseg = T // CMP_STRIDE
    nsb = T // SEL_LEN
    nsb_p = -(-nsb // SUBLANES) * SUBLANES
    pool = (jnp.arange(nsb_p)[:, None] == jnp.arange(nseg)[None, :] // (SEL_LEN // CMP_STRIDE)).astype(BF16)
    tile = lambda b, i: (b * nq + i, 0)
    per_b = lambda b, i: (b, 0)
    c2 = lambda b, i: (0, 0)
    c3 = lambda b, i: (0, 0, 0)
    c4 = A_HPG * tq
    return pl.pallas_call(
        functools.partial(_nsa_prompt_kernel, T=T, tq=tq, kc_len=kc_len),
        grid=(nb, nq),
        in_specs=[pl.BlockSpec((tq, A_WIDTH), tile), pl.BlockSpec((tq, A_WIDTH), tile),
                  pl.BlockSpec((tq, LANES), tile),
                  pl.BlockSpec((T, 4 * LANES), per_b), pl.BlockSpec((T, 2 * LANES), per_b),
                  pl.BlockSpec(wbd.shape, c3), pl.BlockSpec(pe.shape, c3), pl.BlockSpec(kg0.shape, c2),
                  pl.BlockSpec(pool.shape, c2)],
        out_specs=pl.BlockSpec((tq, A_WIDTH), tile),
        out_shape=jax.ShapeDtypeStruct((nb * T, A_WIDTH), F32),
        scratch_shapes=[pltpu.VMEM((T, LANES), F32), pltpu.VMEM((T, LANES), F32),
                        pltpu.VMEM((nseg, LANES), F32), pltpu.VMEM((LANES, nseg), F32),
                        pltpu.VMEM((A_KV, nsb, tq), F32),
                        pltpu.VMEM((A_KV, 1, c4), F32),
                        pltpu.VMEM((A_KV, LANES, c4), F32),
                        pltpu.VMEM((A_KV, max(kc_len, min(WINDOW + tq, T)), c4), F32)],
        compiler_params=_cparams(("parallel", "arbitrary")),
        name="nsa_prompt",
    )(q, qr, small, rows, win, wbd, pe, kg0, pool)


def _nsa_sample_kernel(pt_ref, cache_ref, q_ref, qr_ref, small_ref, rows_ref, winnew_ref, winbuf_ref,
                       wbd_ref, pe_ref, kg0_ref, pool_ref, expand_ref,
                       o_ref, winout_ref,
                       cmp_buf, sel_buf, xperm_sc, sems, *, n_pages, past_len, t_valid):
    b = pl.program_id(0)
    nb = pl.num_programs(0)
    tp = SAMPLE_PAD_T
    nseg = past_len // CMP_STRIDE
    nsb = past_len // SEL_LEN
    wbuf = winbuf_ref.shape[1]

    def page_copies(bb, p, phase):
        page = pt_ref[bb * n_pages + p]
        dst_lanes = pl.ds(pl.multiple_of(p * PAGE_SIZE, PAGE_SIZE), PAGE_SIZE)
        if phase == 0:
            return [pltpu.make_async_copy(cache_ref.at[page, pl.ds(0, 2 * LANES), :],
                                          cmp_buf.at[:, dst_lanes], sems.at[0])]
        return [pltpu.make_async_copy(cache_ref.at[page, pl.ds(2 * LANES, 2 * LANES), :],
                                      sel_buf.at[:, dst_lanes], sems.at[1])]

    def start_all(bb, phase):
        def body(p, c):
            for cp in page_copies(bb, p, phase):
                cp.start()
            return c
        lax.fori_loop(0, n_pages, body, 0)

    def wait_all(bb, phase):
        def body(p, c):
            for cp in page_copies(bb, p, phase):
                cp.wait()
            return c
        lax.fori_loop(0, n_pages, body, 0)

    @pl.when(b == 0)
    def _():
        start_all(b, 0)

    start_all(b, 1)
    wait_all(b, 0)

    seg_pp = PAGE_SIZE // CMP_STRIDE
    pr = lax.broadcasted_iota(jnp.int32, (PAGE_SIZE, PAGE_SIZE), 0)
    pc = lax.broadcasted_iota(jnp.int32, (PAGE_SIZE, PAGE_SIZE), 1)
    perm = jnp.where(pc == CMP_STRIDE * (pr % seg_pp) + pr // seg_pp, 1.0, 0.0).astype(BF16)
    for p in range(n_pages):
        xp = _bdot_t(perm, cmp_buf[:, p * PAGE_SIZE:(p + 1) * PAGE_SIZE])
        for l in range(CMP_STRIDE):
            xperm_sc[l, p * seg_pp:(p + 1) * seg_pp, :] = xp[l * seg_pp:(l + 1) * seg_pp, :]
    kc, vc = _compress_grouped(xperm_sc, nseg, wbd_ref, pe_ref, kg0_ref[...])
    kc_b = kc.astype(BF16)
    vc_b = vc.astype(BF16)
    q = q_ref[...]
    qr = qr_ref[...]
    small = small_ref[...]
    tpos_col = past_len + lax.broadcasted_iota(jnp.int32, (tp, 1), 0)
    tpos_rows = jnp.concatenate([tpos_col] * A_HPG, axis=0)
    bp_idx = lax.broadcasted_iota(jnp.int32, (nsb, nsb), 0)
    b_idx = lax.broadcasted_iota(jnp.int32, (nsb, nsb), 1)
    o_cmps = []
    sels = []
    for g in range(A_KV):
        qn_g = _stack_heads(q, g)
        o_cmp, imp = _cmp_branch(qn_g, kc_b, vc_b, tpos_rows, nseg, tp)
        o_cmps.append(o_cmp)
        imp_sel = _dot2_exact_rhs(imp, pool_ref[...])
        imp_pad = jnp.concatenate([imp_sel, jnp.zeros((nsb - tp, nsb), F32)], axis=0)
        imp_t = jnp.transpose(imp_pad)
        rows_sel = []
        for t in range(tp):
            if t < t_valid:
                row_t = imp_sel[t:t + 1, :]
                col_t = imp_t[:, t:t + 1]
                ahead = jnp.where(col_t > row_t, 1.0, jnp.where((col_t == row_t) & (bp_idx < b_idx), 1.0, 0.0))
                rank = jnp.sum(ahead, axis=0, keepdims=True)
                rows_sel.append(jnp.where(rank < (N_SEL - 1), 1.0, 0.0))
            else:
                rows_sel.append(jnp.zeros((1, nsb), F32))
        sels.append(jnp.concatenate(rows_sel, axis=0).astype(BF16))

    @pl.when(b + 1 < nb)
    def _():
        start_all(b + 1, 0)

    wait_all(b, 1)

    new_idx = lax.broadcasted_iota(jnp.int32, (tp, tp), 1)
    tok_idx = lax.broadcasted_iota(jnp.int32, (tp, tp), 0)
    new_ok = jnp.concatenate([jnp.where(new_idx <= tok_idx, 1.0, 0.0)] * A_HPG, axis=0) > 0.5
    wpos = past_len - wbuf + lax.broadcasted_iota(jnp.int32, (1, wbuf), 1)
    wdiff = tpos_col - wpos
    win_ok = jnp.concatenate([jnp.where((wdiff >= 0) & (wdiff < WINDOW), 1.0, 0.0)] * A_HPG, axis=0) > 0.5
    k_past = sel_buf[0:LANES, :].astype(BF16)
    v_past = sel_buf[LANES:2 * LANES, :].astype(BF16)
    k_new = rows_ref[:, 2 * LANES:3 * LANES]
    v_new = rows_ref[:, 3 * LANES:4 * LANES]
    kw_past = winbuf_ref[0:LANES, :]
    vw_past = winbuf_ref[LANES:2 * LANES, :]
    kw_new = winnew_ref[:, 0:LANES]
    vw_new = winnew_ref[:, LANES:2 * LANES]
    o_groups = []
    for g in range(A_KV):
        qr_g = _stack_heads(qr, g)
        mk = jnp.dot(sels[g], expand_ref[...], preferred_element_type=F32)
        past_ok = jnp.concatenate([mk] * A_HPG, axis=0) > 0.5
        o_sel = _masked_attn_direct(qr_g, [k_past, k_new], [v_past, v_new], [past_ok, new_ok], [True, False])
        o_win = _masked_attn_direct(qr_g, [kw_past, kw_new], [vw_past, vw_new], [win_ok, new_ok], [True, False])
        o_groups.append(_gate_cols(small, g, 0) * o_cmps[g] + _gate_cols(small, g, 1) * o_sel
                        + _gate_cols(small, g, 2) * o_win)
    o_ref[...] = _assemble_heads(o_groups, tp)

    rolled = pltpu.roll(winbuf_ref[...], wbuf - t_valid, 1)
    new_t = jnp.transpose(jnp.concatenate([winnew_ref[...], jnp.zeros((LANES - tp, 2 * LANES), F32)], axis=0))
    new_t = pltpu.roll(new_t, LANES - t_valid, 1)
    lane = lax.broadcasted_iota(jnp.int32, (2 * LANES, LANES), 1)
    winout_ref[:, 0:wbuf - LANES] = rolled[:, 0:wbuf - LANES]
    winout_ref[:, wbuf - LANES:wbuf] = jnp.where(lane < LANES - t_valid, rolled[:, wbuf - LANES:wbuf], new_t)


def _nsa_sample(page_table, cache, q, qr, small, rows, winnew, winbuf, wbd, pe, kg0, t_valid):
    nb, n_pages = page_table.shape
    past_len = n_pages * PAGE_SIZE
    nseg = past_len // CMP_STRIDE
    nsb = past_len // SEL_LEN
    tp = SAMPLE_PAD_T
    wbuf = winbuf.shape[2]
    pool = (jnp.arange(nseg)[:, None] // (SEL_LEN // CMP_STRIDE) == jnp.arange(nsb)[None, :]).astype(BF16)
    expand = (jnp.arange(nsb)[:, None] == jnp.arange(past_len)[None, :] // SEL_LEN).astype(BF16)
    tile = lambda b, pt: (b, 0)
    c2 = lambda b, pt: (0, 0)
    c3 = lambda b, pt: (0, 0, 0)
    gs = pltpu.PrefetchScalarGridSpec(
        num_scalar_prefetch=1,
        grid=(nb,),
        in_specs=[pl.BlockSpec(memory_space=pl.ANY),
                  pl.BlockSpec((tp, A_WIDTH), tile), pl.BlockSpec((tp, A_WIDTH), tile),
                  pl.BlockSpec((tp, LANES), tile), pl.BlockSpec((tp, 4 * LANES), tile),
                  pl.BlockSpec((tp, 2 * LANES), tile),
                  pl.BlockSpec((None, 2 * LANES, wbuf), lambda b, pt: (b, 0, 0)),
                  pl.BlockSpec(wbd.shape, c3), pl.BlockSpec(pe.shape, c3), pl.BlockSpec(kg0.shape, c2),
                  pl.BlockSpec(pool.shape, c2), pl.BlockSpec(expand.shape, c2)],
        out_specs=[pl.BlockSpec((tp, A_WIDTH), tile),
                   pl.BlockSpec((None, 2 * LANES, wbuf), lambda b, pt: (b, 0, 0))],
        scratch_shapes=[pltpu.VMEM((2 * LANES, past_len), F32), pltpu.VMEM((2 * LANES, past_len), F32),
                        pltpu.VMEM((CMP_STRIDE, past_len // CMP_STRIDE, 2 * LANES), F32),
                        pltpu.SemaphoreType.DMA((2,))],
    )
    return pl.pallas_call(
        functools.partial(_nsa_sample_kernel, n_pages=n_pages, past_len=past_len, t_valid=t_valid),
        grid_spec=gs,
        out_shape=[jax.ShapeDtypeStruct((nb * tp, A_WIDTH), F32),
                   jax.ShapeDtypeStruct((nb, 2 * LANES, wbuf), F32)],
        compiler_params=_cparams(("arbitrary",)),
        name="nsa_sample",
    )(page_table.reshape(-1), cache, q, qr, small, rows, winnew, winbuf, wbd, pe, kg0, pool, expand)


MOE_TM = 256
SEG_ALIGN = 8
SEG_BITS = (256, 128, 64, 32, 16, 8)
MOE_RL = -(-(MOE_TM * TOP_K + N_EXPERTS * (SEG_ALIGN - 1)) // LANES) * LANES


def _pack_halves(x, bf16_exact=False):
    w = x.shape[1] // 2
    bits = lax.bitcast_convert_type(x if bf16_exact else x.astype(BF16).astype(F32), jnp.uint32)
    return bits[:, :w] | (bits[:, w:] >> 16)


def _unpack_halves(u):
    hi = lax.bitcast_convert_type(u & jnp.uint32(0xFFFF0000), F32).astype(BF16)
    lo = lax.bitcast_convert_type(u << 16, F32).astype(BF16)
    return hi, lo


def _route_and_sort(h2, wrt_ref, brt_ref, xsl_ref, info_ref, cnt_ref, tm, t_mod, t_valid, m_valid):
    ne = N_EXPERTS
    h2b = h2.astype(BF16)
    h2l = (h2 - h2b.astype(F32)).astype(BF16)
    wh, wl = _split(wrt_ref[...])
    lt = _bdot_t(wh, h2b) + _bdot_t(wl, h2b) + _bdot_t(wh, h2l) + brt_ref[...]
    eidx = lax.broadcasted_iota(jnp.int32, (ne, tm), 0)
    rank = jnp.zeros((ne, tm), F32)
    for ep in range(ne):
        v = lt[ep:ep + 1, :]
        rank = rank + jnp.where(v > lt, 1.0, jnp.where((v == lt) & (eidx > ep), 1.0, 0.0))
    sel = rank < TOP_K
    if t_mod is not None:
        tok = pl.program_id(0) * tm + lax.broadcasted_iota(jnp.int32, (1, tm), 1)
        sel = sel & ((tok % t_mod) < t_valid) & (tok < m_valid)
    mx = jnp.max(jnp.where(sel, lt, NEG_BIG), axis=0, keepdims=True)
    ex = jnp.where(sel, jnp.exp(lt - mx), 0.0)
    den = jnp.sum(ex, axis=0, keepdims=True)
    gate = ex / jnp.where(den > 0, den, 1.0)
    self_ = jnp.where(sel, 1.0, 0.0)
    selb = self_.astype(BF16)
    er = lax.broadcasted_iota(jnp.int32, (ne, ne), 0)
    ec = lax.broadcasted_iota(jnp.int32, (ne, ne), 1)
    c = jnp.dot(jnp.where(ec <= er, 1.0, 0.0).astype(BF16), selb, preferred_element_type=F32)
    tr = lax.broadcasted_iota(jnp.int32, (tm, tm), 0)
    tc = lax.broadcasted_iota(jnp.int32, (tm, tm), 1)
    rk = jnp.dot(selb, jnp.where(tr < tc, 1.0, 0.0).astype(BF16), preferred_element_type=F32)
    cnt = jnp.sum(self_, axis=1, keepdims=True)
    cnt_al = jnp.floor((cnt + (SEG_ALIGN - 1)) * (1.0 / SEG_ALIGN)) * SEG_ALIGN
    cnt_b = jnp.broadcast_to(cnt_al, (ne, LANES))
    cnt_ref[...] = cnt_b
    off = jnp.dot(jnp.where(ec < er, 1.0, 0.0).astype(BF16), cnt_b.astype(BF16), preferred_element_type=F32)
    rowidx = off[:, 0:1] + rk
    rows_k, gates_k, exps_k = [], [], []
    for k in range(1, TOP_K + 1):
        mk = sel & (c == k)
        has = jnp.sum(jnp.where(mk, 1.0, 0.0), axis=0, keepdims=True)
        rows_k.append(jnp.sum(jnp.where(mk, rowidx, 0.0), axis=0, keepdims=True) + has - 1.0)
        gates_k.append(jnp.sum(jnp.where(mk, gate, 0.0), axis=0, keepdims=True))
        exps_k.append(jnp.sum(jnp.where(mk, eidx.astype(F32), 0.0), axis=0, keepdims=True))
    info_ref[...] = jnp.concatenate(rows_k + gates_k + exps_k + [jnp.zeros((4, tm), F32)], axis=0)
    ridx = lax.broadcasted_iota(jnp.int32, (MOE_RL, tm), 0).astype(F32)
    perm = jnp.zeros((MOE_RL, tm), F32)
    for k in range(TOP_K):
        perm = jnp.where(ridx == rows_k[k], 1.0, perm)
    xs = jnp.dot(perm.astype(BF16), h2b, preferred_element_type=F32)
    xsl_ref[...] = _pack_halves(xs, bf16_exact=True)


def _mixout_kernel(x_ref, hm_ref, on_ref, mod_ref, gmix_ref, gffn_ref,
                   wog_ref, bog_ref, wum_ref, wua_ref, wout_ref, wrt_ref, brt_ref,
                   x1_ref, xsl_ref, info_ref, cnt_ref, *, tm, t_mod, t_valid, m_valid, n_real):
    if n_real is not None:
        @pl.when(pl.program_id(0) >= n_real)
        def _():
            xsl_ref[...] = jnp.zeros(xsl_ref.shape, jnp.uint32)
            info_ref[...] = jnp.zeros(info_ref.shape, F32)
            cnt_ref[...] = jnp.zeros(cnt_ref.shape, F32)

        @pl.when(pl.program_id(0) < n_real)
        def _():
            _mixout_body(x_ref, hm_ref, on_ref, mod_ref, gmix_ref, gffn_ref, wog_ref, bog_ref, wum_ref,
                         wua_ref, wout_ref, wrt_ref, brt_ref, x1_ref, xsl_ref, info_ref, cnt_ref,
                         tm, t_mod, t_valid, m_valid)
    else:
        _mixout_body(x_ref, hm_ref, on_ref, mod_ref, gmix_ref, gffn_ref, wog_ref, bog_ref, wum_ref,
                     wua_ref, wout_ref, wrt_ref, brt_ref, x1_ref, xsl_ref, info_ref, cnt_ref,
                     tm, t_mod, t_valid, m_valid)


def _mixout_body(x_ref, hm_ref, on_ref, mod_ref, gmix_ref, gffn_ref,
                 wog_ref, bog_ref, wum_ref, wua_ref, wout_ref, wrt_ref, brt_ref,
                 x1_ref, xsl_ref, info_ref, cnt_ref, tm, t_mod, t_valid, m_valid):
    d = D_MODEL
    x = x_ref[...]
    sh1, sc1, gt1 = mod_ref[:, 0:d], mod_ref[:, d:2 * d], mod_ref[:, 2 * d:3 * d]
    sh2, sc2 = mod_ref[:, 3 * d:4 * d], mod_ref[:, 4 * d:5 * d]
    h = _rmsnorm_rows(x, gmix_ref[...]) * (1.0 + sc1) + sh1
    hb = h.astype(BF16)
    mo = jnp.dot(hb, wog_ref[:, 0:M_WIDTH], preferred_element_type=F32) + bog_ref[:, 0:M_WIDTH]
    ym = _bdot(_sigmoid(mo) * hm_ref[...], wum_ref[...])
    ya = _bdot(on_ref[...], wua_ref[...])
    ga = jnp.dot(hb, wog_ref[:, M_WIDTH:M_WIDTH + d], preferred_element_type=F32) + bog_ref[:, M_WIDTH:M_WIDTH + d]
    u = _sigmoid(ga) * ym
    gb = (jnp.dot(hb, wog_ref[:, M_WIDTH + d:M_WIDTH + 2 * d], preferred_element_type=F32)
          + bog_ref[:, M_WIDTH + d:M_WIDTH + 2 * d])
    u = u + _sigmoid(gb) * ya
    x1 = x + gt1 * _bdot(u, wout_ref[...])
    x1_ref[...] = x1
    h2 = _rmsnorm_rows(x1, gffn_ref[...]) * (1.0 + sc2) + sh2
    _route_and_sort(h2, wrt_ref, brt_ref, xsl_ref, info_ref, cnt_ref, tm, t_mod, t_valid, m_valid)


def _mixout_with_shared(*refs, n_shared, **kw):
    n_in = 13
    _mixout_kernel(*refs[:n_in], *refs[n_in + n_shared:], **kw)


def _mixout(x2, hm, on, mod3, gmix, gffn, wts, tiles_per_mod, nt_total, tile0=0, shared=None,
            t_mod=None, t_valid=None, m_valid=None):
    m = x2.shape[0]
    tm = MOE_TM
    nt = m // tm
    (wog, bog, wum, wua, wout, wr, br) = wts
    r = mod3.shape[1]
    n_extra = nt_total - tile0 - nt if shared is None else 0
    row = lambda i: (jnp.minimum(i, nt - 1), 0)
    const = lambda i: (0, 0)
    in_specs = [pl.BlockSpec((tm, D_MODEL), row), pl.BlockSpec((tm, M_WIDTH), row),
                pl.BlockSpec((tm, A_WIDTH), row),
                pl.BlockSpec((None, r, 6 * D_MODEL), lambda i: (jnp.minimum(i, nt - 1) // tiles_per_mod, 0, 0)),
                pl.BlockSpec((1, D_MODEL), const), pl.BlockSpec((1, D_MODEL), const),
                pl.BlockSpec(wog.shape, const), pl.BlockSpec(bog.shape, const),
                pl.BlockSpec(wum.shape, const), pl.BlockSpec(wua.shape, const),
                pl.BlockSpec(wout.shape, const), pl.BlockSpec(wr.shape, const),
                pl.BlockSpec(br.shape, const)]
    args = [x2, hm, on, mod3, gmix, gffn, wog, bog, wum, wua, wout, wr, br]
    kw = dict(tm=tm, t_mod=t_mod, t_valid=t_valid, m_valid=m_valid, n_real=nt if n_extra else None)
    body = functools.partial(_mixout_kernel, **kw)
    aliases = {}
    if shared is not None:
        in_specs += [pl.BlockSpec(memory_space=pl.ANY)] * len(shared)
        aliases = {len(args) + j: 1 + j for j in range(len(shared))}
        args += list(shared)
        body = functools.partial(_mixout_with_shared, n_shared=len(shared), **kw)
    return pl.pallas_call(
        body,
        grid=(nt + n_extra,),
        in_specs=in_specs,
        out_specs=[pl.BlockSpec((tm, D_MODEL), row),
                   pl.BlockSpec((MOE_RL, D_MODEL // 2), lambda i: (tile0 + i, 0)),
                   pl.BlockSpec((16, tm), lambda i: (0, tile0 + i)),
                   pl.BlockSpec((None, N_EXPERTS, LANES), lambda i: (tile0 + i, 0, 0))],
        out_shape=[jax.ShapeDtypeStruct((m, D_MODEL), F32),
                   jax.ShapeDtypeStruct((nt_total * MOE_RL, D_MODEL // 2), jnp.uint32),
                   jax.ShapeDtypeStruct((16, nt_total * tm), F32),
                   jax.ShapeDtypeStruct((nt_total, N_EXPERTS, LANES), F32)],
        input_output_aliases=aliases,
        compiler_params=_cparams(("arbitrary" if n_extra else "parallel",)),
        name="mixout",
    )(*args)


MOE_BM = 256
MOE_CH = 512


def _moe_kernel(be_ref, na_ref, grp_ref,
                xsl_ref, wgu_ref, bgu_ref, wdn_ref, bdn_ref, ysl_ref,
                wgu_bf, wdn_bf, xbuf, ybuf, sem_in, sem_out, *, trash_row0):
    i = pl.program_id(0)
    na = na_ref[0]
    e = be_ref[i]
    prev = be_ref[jnp.maximum(i - 1, 0)]
    n_grp = MOE_BM // SEG_ALIGN

    def group_copies(blk, inbound, slot=None):
        slot = blk % 2 if slot is None else slot
        cps = []
        for r in range(n_grp):
            v = grp_ref[blk * n_grp + r]
            vm_rows = pl.ds(r * SEG_ALIGN, SEG_ALIGN)
            if inbound:
                row = pl.multiple_of(jnp.where(v >= 0, v, trash_row0 + 2 * MOE_BM), SEG_ALIGN)
                cps.append(pltpu.make_async_copy(xsl_ref.at[pl.ds(row, SEG_ALIGN), :],
                                                 xbuf.at[slot, vm_rows, :], sem_in.at[slot]))
            else:
                spare = trash_row0 + slot * MOE_BM + r * SEG_ALIGN
                row = pl.multiple_of(jnp.where(v >= 0, v, spare), SEG_ALIGN)
                cps.append(pltpu.make_async_copy(ybuf.at[slot, vm_rows, :],
                                                 ysl_ref.at[pl.ds(row, SEG_ALIGN), :], sem_out.at[slot]))
        return cps

    def start_gather(blk):
        for cp in group_copies(blk, True):
            cp.start()

    def start_scatter(blk):
        for cp in group_copies(blk, False):
            cp.start()

    def wait_rows(blk, sem, inbound):
        slot = blk % 2
        if inbound:
            pltpu.make_async_copy(xsl_ref.at[pl.ds(0, MOE_BM), :], xbuf.at[slot], sem.at[slot]).wait()
        else:
            pltpu.make_async_copy(ybuf.at[slot], ysl_ref.at[pl.ds(0, MOE_BM), :], sem.at[slot]).wait()

    @pl.when(i == 0)
    def _():
        start_gather(i)

    @pl.when(i + 1 < na)
    def _():
        start_gather(i + 1)

    @pl.when((i < na) & ((i == 0) | (e != prev)))
    def _():
        for j in range(2 * D_EXPERT // MOE_CH):
            wgu_bf[:, j * MOE_CH:(j + 1) * MOE_CH] = wgu_ref[:, j * MOE_CH:(j + 1) * MOE_CH].astype(BF16)
        for j in range(D_EXPERT // MOE_CH):
            wdn_bf[j * MOE_CH:(j + 1) * MOE_CH, :] = wdn_ref[j * MOE_CH:(j + 1) * MOE_CH, :].astype(BF16)

    @pl.when(i < na)
    def _():
        slot = i % 2
        wait_rows(i, sem_in, True)

        @pl.when(i >= 2)
        def _():
            wait_rows(i - 2, sem_out, False)

        half = D_MODEL // 2
        xh, xl = _unpack_halves(xbuf[slot])

        def xdot(c0, c1):
            return (jnp.dot(xh, wgu_bf[0:half, c0:c1], preferred_element_type=F32)
                    + jnp.dot(xl, wgu_bf[half:D_MODEL, c0:c1], preferred_element_type=F32))

        acc = jnp.zeros((MOE_BM, D_MODEL), F32) + bdn_ref[...]
        for j in range(D_EXPERT // MOE_CH):
            lo, hi = j * MOE_CH, (j + 1) * MOE_CH
            gj = xdot(lo, hi) + bgu_ref[:, lo:hi]
            uj = xdot(D_EXPERT + lo, D_EXPERT + hi) + bgu_ref[:, D_EXPERT + lo:D_EXPERT + hi]
            gj = jnp.minimum(gj, SWIGLU_LIMIT)
            uj = jnp.clip(uj, -SWIGLU_LIMIT, SWIGLU_LIMIT)
            act = gj * _sigmoid(SWIGLU_ALPHA * gj) * (uj + 1.0)
            acc = acc + jnp.dot(act.astype(BF16), wdn_bf[lo:hi, :], preferred_element_type=F32)
        ybuf[slot] = _pack_halves(acc)
        start_scatter(i)

        @pl.when(i == na - 1)
        def _():
            @pl.when(i >= 1)
            def _():
                wait_rows(i - 1, sem_out, False)
            wait_rows(i, sem_out, False)


def _moe_experts(plan, xsl, w_gu, b_gu, w_dn, b_dn):
    block_e, n_active, grp_rows = plan
    nblk = block_e.shape[0]
    spare_row0 = xsl.shape[0] - MOE_RL
    assert MOE_RL >= 2 * MOE_BM
    wmap = lambda i, be, *_: (be[i], 0, 0)
    anyspec = pl.BlockSpec(memory_space=pl.ANY)
    gs = pltpu.PrefetchScalarGridSpec(
        num_scalar_prefetch=3,
        grid=(nblk,),
        in_specs=[anyspec,
                  pl.BlockSpec((None, D_MODEL, 2 * D_EXPERT), wmap),
                  pl.BlockSpec((None, 1, 2 * D_EXPERT), wmap),
                  pl.BlockSpec((None, D_EXPERT, D_MODEL), wmap),
                  pl.BlockSpec((None, 1, D_MODEL), wmap)],
        out_specs=anyspec,
        scratch_shapes=[pltpu.VMEM((D_MODEL, 2 * D_EXPERT), BF16), pltpu.VMEM((D_EXPERT, D_MODEL), BF16),
                        pltpu.VMEM((2, MOE_BM, D_MODEL // 2), jnp.uint32),
                        pltpu.VMEM((2, MOE_BM, D_MODEL // 2), jnp.uint32),
                        pltpu.SemaphoreType.DMA((2,)), pltpu.SemaphoreType.DMA((2,))],
    )
    return pl.pallas_call(
        functools.partial(_moe_kernel, trash_row0=spare_row0),
        grid_spec=gs,
        out_shape=jax.ShapeDtypeStruct(xsl.shape, jnp.uint32),
        input_output_aliases={3: 0},
        compiler_params=_cparams(("arbitrary",)),
        name="moe_experts",
    )(*plan, xsl, w_gu, b_gu.reshape(N_EXPERTS, 1, -1), w_dn, b_dn.reshape(N_EXPERTS, 1, -1))


def _combine_kernel(ysl_ref, info_ref, x1_ref, mod_ref, y_ref, *, tm):
    info = info_ref[...]
    info_t = jnp.transpose(jnp.concatenate([info, jnp.zeros((LANES - info.shape[0], tm), F32)], axis=0))
    ridx = lax.broadcasted_iota(jnp.int32, (tm, MOE_RL), 1).astype(F32)
    pg = jnp.zeros((tm, MOE_RL), F32)
    for k in range(TOP_K):
        pg = jnp.where(ridx == info_t[:, k:k + 1], info_t[:, TOP_K + k:TOP_K + k + 1], pg)
    pgb = pg.astype(BF16)
    yh, yl = _unpack_halves(ysl_ref[...])
    half = D_MODEL // 2
    gt2 = mod_ref[:, 5 * D_MODEL:6 * D_MODEL]
    for c, yy in ((0, yh), (1, yl)):
        moe = jnp.dot(pgb, yy, preferred_element_type=F32)
        y_ref[:, c * half:(c + 1) * half] = (x1_ref[:, c * half:(c + 1) * half]
                                             + gt2[:, c * half:(c + 1) * half] * moe)


def _combine(ysl, info, x1, mod3, tiles_per_mod, tile0=0):
    m = x1.shape[0]
    tm = MOE_TM
    r = mod3.shape[1]
    return pl.pallas_call(
        functools.partial(_combine_kernel, tm=tm),
        grid=(m // tm,),
        in_specs=[pl.BlockSpec((MOE_RL, D_MODEL // 2), lambda i: (tile0 + i, 0)),
                  pl.BlockSpec((16, tm), lambda i: (0, tile0 + i)),
                  pl.BlockSpec((tm, D_MODEL), lambda i: (i, 0)),
                  pl.BlockSpec((None, r, 6 * D_MODEL), lambda i: (i // tiles_per_mod, 0, 0))],
        out_specs=pl.BlockSpec((tm, D_MODEL), lambda i: (i, 0)),
        out_shape=jax.ShapeDtypeStruct((m, D_MODEL), F32),
        compiler_params=_cparams(("parallel",)),
        name="moe_combine",
    )(ysl, info, x1, mod3)


def _moe_plan(cnt):
    cnt = cnt.astype(jnp.int32)
    nt = cnt.shape[0]
    so = jnp.cumsum(cnt, axis=1) - cnt + (jnp.arange(nt) * MOE_RL)[:, None]
    ce = jnp.cumsum(cnt, axis=0)
    cs = ce - cnt
    tot = ce[-1]
    nblk_e = (tot + MOE_BM - 1) // MOE_BM
    blk_end = jnp.cumsum(nblk_e)
    max_rows = nt * MOE_TM * TOP_K + nt * N_EXPERTS * (SEG_ALIGN - 1)
    n_blocks = -(-max_rows // MOE_BM) + N_EXPERTS
    bidx = jnp.arange(n_blocks)
    block_e = jnp.minimum(jnp.sum(blk_end[None, :] <= bidx[:, None], axis=1), N_EXPERTS - 1).astype(jnp.int32)
    is_e = (jnp.arange(N_EXPERTS)[:, None] == block_e[None, :]).astype(jnp.int32)
    per_block = lambda a: jnp.sum(a[..., :, None] * is_e, axis=-2)
    block_r0 = (bidx - per_block(blk_end - nblk_e)) * MOE_BM
    x = block_r0[:, None] + jnp.arange(MOE_BM // SEG_ALIGN)[None, :] * SEG_ALIGN
    ce_b = per_block(ce)[:, :, None]
    cs_b = per_block(cs)[:, :, None]
    inside = (cs_b <= x[None]) & (x[None] < ce_b)
    grp = x + jnp.sum(jnp.where(inside, per_block(so - cs)[:, :, None], 0), axis=0)
    grp = jnp.where(x < per_block(tot)[:, None], grp, -1)
    n_active = blk_end[-1].reshape(1)
    i32 = lambda a: a.reshape(-1).astype(jnp.int32)
    return block_e, i32(n_active), i32(grp)


def _rope_tables(pos):
    half = ROT_DIM // 2
    inv = ROPE_THETA ** (-jnp.arange(half, dtype=F32) * (2.0 / ROT_DIM))
    ang = pos.astype(F32)[:, None] * inv[None, :]
    cos, sin = jnp.cos(ang), jnp.sin(ang)
    n = pos.shape[0]
    ones = jnp.ones((n, A_DH - ROT_DIM), F32)
    zeros_h = jnp.zeros((n, half), F32)
    zeros_r = jnp.zeros((n, A_DH - ROT_DIM), F32)
    cos64 = jnp.concatenate([cos, cos, ones], axis=1)
    sprev64 = jnp.concatenate([zeros_h, sin, zeros_r], axis=1)
    snext64 = jnp.concatenate([-sin, zeros_h, zeros_r], axis=1)
    two = lambda a: jnp.concatenate([a, a], axis=1)
    return two(cos64), two(sprev64), two(snext64)


def _prep_weights(w_in, b_in, q_norm_g, k_norm_g, cmp_pe_k, cmp_pe_v, cmp_w_k, cmp_w_v,
                  w_up_m, w_up_a, w_out, w_router, b_router):
    b2 = b_in.reshape(1, N_IN)
    wm = w_in[:, OFF_MQ:OFF_MO].astype(BF16)
    bm = b2[:, OFF_MQ:OFF_MO]
    wq = w_in[:, OFF_AQ:OFF_AKV].astype(BF16)
    bq = b2[:, OFF_AQ:OFF_AKV]
    wkv = w_in[:, OFF_AKV:OFF_AG].astype(BF16)
    bkv = b2[:, OFF_AKV:OFF_AG]
    n_small = 2 * M_HEADS + 3 * A_HEADS
    ws = jnp.concatenate([w_in[:, OFF_MI:OFF_AQ], w_in[:, OFF_AG:OFF_GA],
                          jnp.zeros((D_MODEL, LANES - n_small), F32)], axis=1)
    bs = jnp.concatenate([b2[:, OFF_MI:OFF_AQ], b2[:, OFF_AG:OFF_GA], jnp.zeros((1, LANES - n_small), F32)], axis=1)
    qg = jnp.tile(q_norm_g, A_HEADS).reshape(1, A_WIDTH)
    kg = jnp.stack([jnp.tile(k_norm_g[1], A_KV), jnp.tile(k_norm_g[2], A_KV)], axis=0)
    kg0 = jnp.tile(k_norm_g[0], A_KV).reshape(1, LANES)
    hid = jnp.arange(A_WIDTH) // A_DH
    bd = jnp.where(hid[:, None] == hid[None, :], 1.0 / A_DH, 0.0).astype(BF16)
    inproj_w = (wm, bm, wq, bq, wkv, bkv, ws, bs, qg, kg, bd)

    z = jnp.zeros((CMP_LEN, A_DH, A_DH), F32)
    r0 = jnp.concatenate([cmp_w_k, z, z, z], axis=2)
    r1 = jnp.concatenate([z, cmp_w_k, z, z], axis=2)
    r2 = jnp.concatenate([z, z, cmp_w_v, z], axis=2)
    r3 = jnp.concatenate([z, z, z, cmp_w_v], axis=2)
    wbd = jnp.concatenate([r0, r1, r2, r3], axis=1).astype(BF16)
    pe = jnp.concatenate([cmp_pe_k, cmp_pe_k, cmp_pe_v, cmp_pe_v], axis=1).reshape(CMP_LEN, 1, 2 * LANES)

    wog = jnp.concatenate([w_in[:, OFF_MO:OFF_MI], w_in[:, OFF_GA:N_IN]], axis=1).astype(BF16)
    bog = jnp.concatenate([b2[:, OFF_MO:OFF_MI], b2[:, OFF_GA:N_IN]], axis=1)
    mixout_w = (wog, bog, w_up_m.astype(BF16), w_up_a.astype(BF16), w_out.astype(BF16),
                w_router.T, b_router.reshape(N_EXPERTS, 1))
    return inproj_w, (wbd, pe, kg0), mixout_w


def _pick_tile(m, pref):
    t = pref
    while m % t:
        t //= 2
    return t


def kernel(x_prompt, x_sample, cache_nsa_kv, state_win_kv, state_mlstm_C, state_mlstm_n, state_mlstm_m, page_table, c_prompt, c_sample, w_ada, b_ada, g_mix, g_ffn, w_in, b_in, q_norm_g, k_norm_g, cmp_pe_k, cmp_pe_v, cmp_w_k, cmp_w_v, w_up_m, w_up_a, w_out, w_router, b_router, w_gu, b_gu, w_dn, b_dn):
    depth = w_in.shape[0]
    assert depth == 1
    B, T, D = x_prompt.shape
    DB, TS, _ = x_sample.shape
    n_pages = page_table.shape[1]
    past_len = n_pages * PAGE_SIZE
    wbuf = state_win_kv.shape[2]
    tp = SAMPLE_PAD_T
    assert TS <= tp and wbuf % tp == 0 and T % 128 == 0

    l = 0
    inproj_w, cmp_w, mixout_w = _prep_weights(
        w_in[l], b_in[l], q_norm_g[l], k_norm_g[l], cmp_pe_k[l], cmp_pe_v[l], cmp_w_k[l], cmp_w_v[l],
        w_up_m[l], w_up_a[l], w_out[l], w_router[l], b_router[l])
    wbd, pe, kg0 = cmp_w
    gmix = g_mix[l].reshape(1, D)
    gffn = g_ffn[l].reshape(1, D)

    nc = B + DB
    nc_pad = -(-nc // SUBLANES) * SUBLANES
    c_all = jnp.concatenate([c_prompt, c_sample, jnp.zeros((nc_pad - nc, D), F32)], axis=0)
    mod = _adaln(c_all, w_ada[l], b_ada[l])
    mod_p = mod[:B].reshape(B, 1, 6 * D)
    mod_s = jnp.repeat(mod[B:B + DB], tp, axis=0).reshape(1, DB * tp, 6 * D)

    mp = B * T
    tm = _pick_tile(T, 256)
    xp = x_prompt.reshape(mp, D)
    tabs_p = _rope_tables(jnp.arange(T, dtype=jnp.int32))
    mq, mk, mv, q, qr, rows, win, small, rows_t = _inproj(xp, mod_p, gmix, tabs_p, inproj_w, tm, T // tm, T // tm,
                                                          rows_t_batches=B)
    Lp = _pick_tile(T, 128)
    hm, C_p, n_p, m_p = _mlstm(mq, mk, mv, small, B, T, T, Lp)
    o_nsa = _nsa_prompt(q, qr, small, rows, win, wbd, pe, kg0, B, T)
    assert T % MOE_TM == 0
    ms_pad = -(-(DB * tp) // MOE_TM) * MOE_TM
    nt_p = mp // MOE_TM
    nt_all = nt_p + ms_pad // MOE_TM + 1
    x1_p, xsl, info, cnt = _mixout(xp, hm, o_nsa, mod_p, gmix, gffn, mixout_w, T // MOE_TM, nt_all)

    ms = DB * tp
    xs_pad = jnp.concatenate([x_sample, jnp.zeros((DB, tp - TS, D), F32)], axis=1).reshape(ms, D)
    pos_s = past_len + jnp.tile(jnp.arange(tp, dtype=jnp.int32), DB)
    tabs_s = _rope_tables(pos_s)
    mq_s, mk_s, mv_s, q_s, qr_s, rows_s, win_s, small_s = _inproj(xs_pad, mod_s, gmix, tabs_s, inproj_w, ms, 1, 1)
    hm_s, C_s, n_s, m_s = _mlstm(mq_s, mk_s, mv_s, small_s, DB, tp, TS, tp,
                                 state=(state_mlstm_C[l], state_mlstm_n[l], state_mlstm_m[l]))
    cache2 = jnp.transpose(cache_nsa_kv[l], (0, 2, 3, 4, 1)).reshape(cache_nsa_kv.shape[1], 4 * LANES, PAGE_SIZE)
    winbuf = jnp.transpose(state_win_kv[l], (0, 2, 3, 4, 1)).reshape(DB, 2 * LANES, wbuf)
    o_nsa_s, win_out_s = _nsa_sample(page_table, cache2, q_s, qr_s, small_s, rows_s, win_s, winbuf,
                                     wbd, pe, kg0, TS)
    assert ms_pad == MOE_TM
    rpad = lambda a: jnp.concatenate([a, jnp.zeros((ms_pad - ms, a.shape[1]), a.dtype)], axis=0) if ms_pad > ms else a
    mod_sp = rpad(mod_s[0])[None]
    x1_s, xsl, info, cnt = _mixout(rpad(xs_pad), rpad(hm_s), rpad(o_nsa_s), mod_sp, gmix, gffn, mixout_w,
                                   1, nt_all, tile0=nt_p, shared=(xsl, info, cnt),
                                   t_mod=tp, t_valid=TS, m_valid=ms)

    ysl = _moe_experts(_moe_plan(cnt[:, :, 0]), xsl, w_gu[l], b_gu[l], w_dn[l], b_dn[l])
    y_p = _combine(ysl, info, x1_p, mod_p, T // MOE_TM).reshape(B, T, D)
    y_s_all = _combine(ysl, info, x1_s, mod_sp, 1, tile0=nt_p)
    valid = lambda a: a.reshape(DB, tp, -1)[:, :TS].reshape(DB * TS, -1)
    y_s = valid(y_s_all[:ms]).reshape(DB, TS, D)

    kv_p = jnp.transpose(rows_t.reshape(B, 4, A_KV, A_DH, T), (0, 4, 1, 2, 3))[None]
    kv_s = valid(rows_s).reshape(1, DB, TS, 4, A_KV, A_DH)
    wp = min(WINDOW, T)
    win_p = win.reshape(B, T, 2, A_KV, A_DH)[:, T - wp:][None]
    win_s_out = jnp.transpose(win_out_s.reshape(DB, 2, A_KV, A_DH, wbuf), (0, 4, 1, 2, 3))[None]
    return (y_p, y_s, kv_p, kv_s, win_p, win_s_out,
            C_p[None], n_p[None], m_p[None], C_s[None], n_s[None], m_s[None])
```

```python
import functools

import jax
import jax.numpy as jnp
from jax import lax
from jax.experimental import pallas as pl
from jax.experimental.pallas import tpu as pltpu

F32 = jnp.float32
BF16 = jnp.bfloat16

D_MODEL = 1024
M_HEADS = 4
M_DH = 128
M_WIDTH = M_HEADS * M_DH
A_HEADS = 8
A_KV = 2
A_HPG = A_HEADS // A_KV
A_DH = 64
A_WIDTH = A_HEADS * A_DH
CMP_STRIDE = 16
CMP_LEN = 32
SEL_LEN = 64
N_SEL = 16
WINDOW = 512
PAGE_SIZE = 128
ROPE_THETA = 500000.0
ROT_DIM = A_DH // 4
ATT_SCALE = A_DH ** -0.5
N_EXPERTS = 32
TOP_K = 4
D_EXPERT = D_MODEL
SWIGLU_LIMIT = 7.0
SWIGLU_ALPHA = 1.702
EPS = 1e-6

OFF_MQ, OFF_MK, OFF_MV, OFF_MO = 0, M_WIDTH, 2 * M_WIDTH, 3 * M_WIDTH
OFF_MI = 4 * M_WIDTH
OFF_MF = OFF_MI + M_HEADS
OFF_AQ = OFF_MF + M_HEADS
OFF_AKV = OFF_AQ + A_WIDTH
OFF_AG = OFF_AKV + 6 * A_KV * A_DH
OFF_GA = OFF_AG + 3 * A_HEADS
OFF_GB = OFF_GA + D_MODEL
N_IN = OFF_GB + D_MODEL

LANES = 128
SUBLANES = 8
VMEM_LIMIT = 56 * 1024 * 1024

NEG_BIG = -1e30
M_INIT = -1e29
LOG2E = 1.4426950408889634
SAMPLE_PAD_T = 8


def _cparams(sem):
    return pltpu.CompilerParams(dimension_semantics=sem, vmem_limit_bytes=VMEM_LIMIT)


def _bdot(a, b):
    return jnp.dot(a.astype(BF16), b.astype(BF16), preferred_element_type=F32)


def _bdot_t(a, b):
    return lax.dot_general(a.astype(BF16), b.astype(BF16), (((1,), (1,)), ((), ())),
                           preferred_element_type=F32)


def _split(a):
    hi = a.astype(BF16)
    lo = (a - hi.astype(F32)).astype(BF16)
    return hi, lo


def _dot3(a, b):
    ah, al = _split(a)
    bh, bl = _split(b)
    return (jnp.dot(ah, bh, preferred_element_type=F32) + jnp.dot(al, bh, preferred_element_type=F32)
            + jnp.dot(ah, bl, preferred_element_type=F32))


def _dot2_exact_rhs(a, b_bf16):
    ah, al = _split(a)
    return jnp.dot(ah, b_bf16, preferred_element_type=F32) + jnp.dot(al, b_bf16, preferred_element_type=F32)


def _sigmoid(x):
    return 0.5 * jnp.tanh(0.5 * x) + 0.5


def _rmsnorm_rows(x, g):
    return x * lax.rsqrt(jnp.mean(x * x, axis=-1, keepdims=True) + EPS) * g


def _adaln_kernel(c_ref, w_ref, b_ref, o_ref):
    c = c_ref[...]
    s = c * _sigmoid(c)
    o_ref[...] = _dot3(s, w_ref[...]) + b_ref[...]


def _adaln(c, w, b):
    mc, d = c.shape
    n = w.shape[1]
    tn = 1024
    return pl.pallas_call(
        _adaln_kernel,
        grid=(n // tn,),
        in_specs=[pl.BlockSpec((mc, d), lambda j: (0, 0)),
                  pl.BlockSpec((d, tn), lambda j: (0, j)),
                  pl.BlockSpec((1, tn), lambda j: (0, j))],
        out_specs=pl.BlockSpec((mc, tn), lambda j: (0, j)),
        out_shape=jax.ShapeDtypeStruct((mc, n), F32),
        compiler_params=_cparams(("parallel",)),
        name="adaln",
    )(c, w, b.reshape(1, n))


def _head_norm(z, bd, gain):
    ms = _dot2_exact_rhs(z * z, bd)
    return z * lax.rsqrt(ms + EPS) * gain


def _rope(z, cos, s_prev, s_next):
    w = z.shape[1]
    rep = w // LANES
    if rep > 1:
        cos = jnp.concatenate([cos] * rep, axis=1)
        s_prev = jnp.concatenate([s_prev] * rep, axis=1)
        s_next = jnp.concatenate([s_next] * rep, axis=1)
    z_prev = pltpu.roll(z, ROT_DIM // 2, 1)
    z_next = pltpu.roll(z, w - ROT_DIM // 2, 1)
    return z * cos + z_prev * s_prev + z_next * s_next


def _inproj_kernel(x_ref, mod_ref, gmix_ref, cos_ref, sp_ref, sn_ref,
                   wm_ref, bm_ref, wq_ref, bq_ref, wkv_ref, bkv_ref, ws_ref, bs_ref,
                   qg_ref, kg_ref, bd_ref,
                   mq_ref, mk_ref, mv_ref, q_ref, qr_ref, rows_ref, win_ref, small_ref,
                   rows_t_ref=None, win_t_ref=None):
    x = x_ref[...]
    sh1 = mod_ref[:, 0:D_MODEL]
    sc1 = mod_ref[:, D_MODEL:2 * D_MODEL]
    h = _rmsnorm_rows(x, gmix_ref[...]) * (1.0 + sc1) + sh1
    hb = h.astype(BF16)

    mq_ref[...] = jnp.dot(hb, wm_ref[:, 0:M_WIDTH], preferred_element_type=F32) + bm_ref[:, 0:M_WIDTH]
    mk = jnp.dot(hb, wm_ref[:, M_WIDTH:2 * M_WIDTH], preferred_element_type=F32) + bm_ref[:, M_WIDTH:2 * M_WIDTH]
    mk_ref[...] = mk * (M_DH ** -0.5)
    mv_ref[...] = (jnp.dot(hb, wm_ref[:, 2 * M_WIDTH:3 * M_WIDTH], preferred_element_type=F32)
                   + bm_ref[:, 2 * M_WIDTH:3 * M_WIDTH])

    cos, sp, sn = cos_ref[...], sp_ref[...], sn_ref[...]
    zq = jnp.dot(hb, wq_ref[...], preferred_element_type=F32) + bq_ref[...]
    qn = _head_norm(zq, bd_ref[...], qg_ref[...])
    q_ref[...] = qn
    qr_ref[...] = _rope(qn, cos, sp, sn)

    zkv = jnp.dot(hb, wkv_ref[...], preferred_element_type=F32) + bkv_ref[...]
    bd2 = bd_ref[0:LANES, 0:LANES]
    ksel = _head_norm(zkv[:, 2 * LANES:3 * LANES], bd2, kg_ref[0:1, :])
    rows = jnp.concatenate([zkv[:, 0:2 * LANES], _rope(ksel, cos, sp, sn), zkv[:, 3 * LANES:4 * LANES]], axis=1)
    rows_ref[...] = rows
    if rows_t_ref is not None:
        rows_t_ref[...] = jnp.transpose(rows)
    kwin = _head_norm(zkv[:, 4 * LANES:5 * LANES], bd2, kg_ref[1:2, :])
    win = jnp.concatenate([_rope(kwin, cos, sp, sn), zkv[:, 5 * LANES:6 * LANES]], axis=1)
    win_ref[...] = win
    if win_t_ref is not None:
        win_t_ref[...] = jnp.transpose(win)

    small_ref[...] = _dot3(h, ws_ref[...]) + bs_ref[...]


def _inproj(x2, mod3, gmix, tabs, wts, tm, tiles_per_mod, pos_tiles, rows_t_batches=None):
    m = x2.shape[0]
    cos_t, sp_t, sn_t = tabs
    (wm, bm, wq, bq, wkv, bkv, ws, bs, qg, kg, bd) = wts
    r = mod3.shape[1]
    row = lambda i: (i, 0)
    const = lambda i: (0, 0)
    tab = lambda i: (i % pos_tiles, 0)
    in_specs = [
        pl.BlockSpec((tm, D_MODEL), row),
        pl.BlockSpec((None, r, 6 * D_MODEL), lambda i: (i // tiles_per_mod, 0, 0)),
        pl.BlockSpec((1, D_MODEL), const),
        pl.BlockSpec((tm, LANES), tab), pl.BlockSpec((tm, LANES), tab), pl.BlockSpec((tm, LANES), tab),
        pl.BlockSpec(wm.shape, const), pl.BlockSpec(bm.shape, const),
        pl.BlockSpec(wq.shape, const), pl.BlockSpec(bq.shape, const),
        pl.BlockSpec(wkv.shape, const), pl.BlockSpec(bkv.shape, const),
        pl.BlockSpec(ws.shape, const), pl.BlockSpec(bs.shape, const),
        pl.BlockSpec(qg.shape, const), pl.BlockSpec(kg.shape, const), pl.BlockSpec(bd.shape, const),
    ]
    widths = (M_WIDTH, M_WIDTH, M_WIDTH, A_WIDTH, A_WIDTH, 4 * LANES, 2 * LANES, LANES)
    out_specs = [pl.BlockSpec((tm, w), row) for w in widths]
    out_shape = [jax.ShapeDtypeStruct((m, w), F32) for w in widths]
    if rows_t_batches is not None:
        for w in (4 * LANES, 2 * LANES):
            out_specs.append(pl.BlockSpec((None, w, tm), lambda i: (i // tiles_per_mod, 0, i % tiles_per_mod)))
            out_shape.append(jax.ShapeDtypeStruct((rows_t_batches, w, m // rows_t_batches), F32))
    return pl.pallas_call(
        _inproj_kernel,
        grid=(m // tm,),
        in_specs=in_specs,
        out_specs=out_specs,
        out_shape=out_shape,
        compiler_params=_cparams(("parallel",)),
        name="inproj",
    )(x2, mod3, gmix, cos_t, sp_t, sn_t, wm, bm, wq, bq, wkv, bkv, ws, bs, qg, kg, bd)


def _log_sigmoid(x):
    return jnp.minimum(x, 0.0) - jnp.log(1.0 + jnp.exp(-jnp.abs(x)))


def _mlstm_kernel(*refs, L, t_valid, has_state):
    if has_state:
        q_ref, k_ref, v_ref, s_ref, c0_ref, n0_ref, m0_ref, h_ref, c_ref, n_ref, m_ref = refs
    else:
        q_ref, k_ref, v_ref, s_ref, h_ref, c_ref, n_ref, m_ref = refs
    c = pl.program_id(1)

    @pl.when(c == 0)
    def _():
        if has_state:
            c_ref[...] = c0_ref[...]
            n_ref[...] = n0_ref[...]
            m_ref[...] = m0_ref[...]
        else:
            c_ref[...] = jnp.zeros(c_ref.shape, F32)
            n_ref[...] = jnp.zeros(n_ref.shape, F32)
            m_ref[...] = jnp.zeros(m_ref.shape, F32)

    row = lax.broadcasted_iota(jnp.int32, (L, L), 0)
    col = lax.broadcasted_iota(jnp.int32, (L, L), 1)
    causal = col <= row
    eye = col == row
    tok_col = c * L + lax.broadcasted_iota(jnp.int32, (L, 1), 0)
    valid_col = tok_col < t_valid
    for hd in range(M_HEADS):
        lo, hi = hd * M_DH, (hd + 1) * M_DH
        q = q_ref[:, lo:hi]
        k = k_ref[:, lo:hi]
        v = v_ref[:, lo:hi]
        i_col = s_ref[:, hd:hd + 1]
        lf_col = _log_sigmoid(s_ref[:, M_HEADS + hd:M_HEADS + hd + 1])
        lf_col = jnp.where(valid_col, lf_col, 0.0)
        i_col = jnp.where(valid_col, i_col, -jnp.inf)
        if L == LANES:
            i_col = jnp.broadcast_to(i_col, (L, L))
            lf_c = jnp.broadcast_to(lf_col, (L, L))
            p0 = lf_c.astype(BF16)
            r1 = lf_c - p0.astype(F32)
            p1 = r1.astype(BF16)
            p2 = (r1 - p1.astype(F32)).astype(BF16)
            tril = jnp.where(causal, 1.0, 0.0).astype(BF16)
            b_col = (jnp.dot(tril, p0, preferred_element_type=F32) + jnp.dot(tril, p1, preferred_element_type=F32)
                     + jnp.dot(tril, p2, preferred_element_type=F32))
            i_row = jnp.transpose(i_col)[0:1, :]
            b_row = jnp.transpose(b_col)[0:1, :]
        else:
            i_row = jnp.sum(jnp.where(eye, i_col, 0.0), axis=0, keepdims=True)
            lf_row = jnp.sum(jnp.where(eye, lf_col, 0.0), axis=0, keepdims=True)
            b_col = jnp.sum(jnp.where(causal, lf_row, 0.0), axis=1, keepdims=True)
            b_row = jnp.sum(jnp.where(row <= col, lf_col, 0.0), axis=0, keepdims=True)
        m_prev = m_ref[:, hd:hd + 1]
        dmat = jnp.where(causal, b_col - b_row + i_row, -jnp.inf)
        inter = b_col + m_prev
        m_row = jnp.maximum(jnp.max(dmat, axis=1, keepdims=True), inter)
        w = jnp.exp(dmat - m_row)
        w_inter = jnp.exp(inter - m_row)
        s = _bdot_t(q, k) * w
        cm = c_ref[hd]
        nv = n_ref[hd]
        num = _bdot(s, v) + w_inter * _bdot_t(q, cm)
        den = jnp.sum(s, axis=1, keepdims=True) + w_inter * jnp.sum(q * nv, axis=1, keepdims=True)
        h_ref[:, lo:hi] = num / jnp.maximum(jnp.abs(den), jnp.exp(-m_row))
        b_last = b_col[L - 1:L, 0:1]
        dec_col = b_last - b_col + i_col
        dec_row = b_last - b_row + i_row
        m_new = jnp.maximum(b_last + m_prev, jnp.max(dec_row, axis=1, keepdims=True))
        ws_col = jnp.exp(dec_col - m_new)
        wc = jnp.exp(b_last + m_prev - m_new)
        vw = (v * ws_col).astype(BF16)
        upd = lax.dot_general(vw, k.astype(BF16), (((0,), (0,)), ((), ())), preferred_element_type=F32)
        c_ref[hd] = wc * cm + upd
        n_ref[hd] = wc * nv + jnp.sum(k * ws_col, axis=0, keepdims=True)
        m_ref[:, hd:hd + 1] = m_new


def _mlstm(mq, mk, mv, small, nb, t_pad, t_valid, L, state=None):
    nc = t_pad // L
    has_state = state is not None
    blk = lambda b, c: (b * nc + c, 0)
    st4 = lambda b, c: (b, 0, 0, 0)
    st3 = lambda b, c: (b, 0, 0)
    in_specs = [pl.BlockSpec((L, M_WIDTH), blk)] * 3 + [pl.BlockSpec((L, LANES), blk)]
    args = [mq, mk, mv, small]
    if has_state:
        c0, n0, m0 = state
        in_specs += [pl.BlockSpec((None, M_HEADS, M_DH, M_DH), st4),
                     pl.BlockSpec((None, M_HEADS, 1, M_DH), st4),
                     pl.BlockSpec((None, 1, M_HEADS), st3)]
        args += [c0, n0.reshape(nb, M_HEADS, 1, M_DH), m0.reshape(nb, 1, M_HEADS)]
    out_specs = [pl.BlockSpec((L, M_WIDTH), blk),
                 pl.BlockSpec((None, M_HEADS, M_DH, M_DH), st4),
                 pl.BlockSpec((None, M_HEADS, 1, M_DH), st4),
                 pl.BlockSpec((None, 1, M_HEADS), st3)]
    out_shape = [jax.ShapeDtypeStruct((nb * t_pad, M_WIDTH), F32),
                 jax.ShapeDtypeStruct((nb, M_HEADS, M_DH, M_DH), F32),
                 jax.ShapeDtypeStruct((nb, M_HEADS, 1, M_DH), F32),
                 jax.ShapeDtypeStruct((nb, 1, M_HEADS), F32)]
    h, cs, ns, ms = pl.pallas_call(
        functools.partial(_mlstm_kernel, L=L, t_valid=t_valid, has_state=has_state),
        grid=(nb, nc),
        in_specs=in_specs,
        out_specs=out_specs,
        out_shape=out_shape,
        compiler_params=_cparams(("parallel", "arbitrary")),
        name="mlstm",
    )(*args)
    return h, cs, ns.reshape(nb, M_HEADS, M_DH), ms.reshape(nb, M_HEADS)


def _stack_heads(qt, g):
    t = qt.shape[0]
    z = jnp.zeros((t, A_DH), F32)
    parts = []
    for hh in range(A_HPG):
        hd = g * A_HPG + hh
        qh = qt[:, hd * A_DH:(hd + 1) * A_DH] * (ATT_SCALE * LOG2E)
        parts.append(jnp.concatenate([qh, z], axis=1) if g == 0 else jnp.concatenate([z, qh], axis=1))
    return jnp.concatenate(parts, axis=0).astype(BF16)


def _gate_cols(small, g, br):
    cols = []
    for hh in range(A_HPG):
        c0 = 2 * M_HEADS + (g * A_HPG + hh) * 3 + br
        cols.append(_sigmoid(small[:, c0:c0 + 1]))
    return jnp.concatenate(cols, axis=0)


def _compress(k_ref, v_ref, nseg, wbd_ref, pe_ref, kg0):
    acc_lo = jnp.zeros((nseg, 2 * LANES), F32)
    acc_hi = jnp.zeros((nseg, 2 * LANES), F32)
    for l in range(CMP_STRIDE):
        xl = jnp.concatenate([k_ref[pl.ds(l, nseg, stride=CMP_STRIDE), :],
                              v_ref[pl.ds(l, nseg, stride=CMP_STRIDE), :]], axis=1)
        acc_lo = acc_lo + _bdot(xl + pe_ref[l], wbd_ref[l])
        acc_hi = acc_hi + _bdot(xl + pe_ref[CMP_STRIDE + l], wbd_ref[CMP_STRIDE + l])
    return _compress_finish(acc_lo, acc_hi, nseg, kg0)


def _compress_grouped(x_ref, nseg, wbd_ref, pe_ref, kg0):
    acc_lo = jnp.zeros((nseg, 2 * LANES), F32)
    acc_hi = jnp.zeros((nseg, 2 * LANES), F32)
    pe_lo = jnp.zeros((SUBLANES, 2 * LANES), F32)
    pe_hi = jnp.zeros((SUBLANES, 2 * LANES), F32)
    for l in range(CMP_STRIDE):
        xl = x_ref[l].astype(BF16)
        acc_lo = acc_lo + jnp.dot(xl, wbd_ref[l], preferred_element_type=F32)
        acc_hi = acc_hi + jnp.dot(xl, wbd_ref[CMP_STRIDE + l], preferred_element_type=F32)
        pe_lo = pe_lo + _bdot(jnp.broadcast_to(pe_ref[l], (SUBLANES, 2 * LANES)), wbd_ref[l])
        pe_hi = pe_hi + _bdot(jnp.broadcast_to(pe_ref[CMP_STRIDE + l], (SUBLANES, 2 * LANES)),
                              wbd_ref[CMP_STRIDE + l])
    return _compress_finish(acc_lo + pe_lo[0:1, :], acc_hi + pe_hi[0:1, :], nseg, kg0)


def _compress_finish(acc_lo, acc_hi, nseg, kg0):
    kv = acc_lo + pltpu.roll(acc_hi, nseg - 1, 0)
    kc = kv[:, 0:LANES]
    vc = kv[:, LANES:2 * LANES]
    lane = lax.broadcasted_iota(jnp.int32, (nseg, LANES), 1)
    sq = kc * kc
    ms0 = jnp.sum(jnp.where(lane < A_DH, sq, 0.0), axis=1, keepdims=True) * (1.0 / A_DH)
    ms1 = jnp.sum(jnp.where(lane >= A_DH, sq, 0.0), axis=1, keepdims=True) * (1.0 / A_DH)
    ms = jnp.where(lane < A_DH, ms0, ms1)
    kc = kc * lax.rsqrt(ms + EPS) * kg0
    return kc, vc


def _cmp_branch(qn_g, kc_b, vc_b, tpos_rows, nseg, n_tok):
    s = _bdot_t(qn_g, kc_b)
    nidx = lax.broadcasted_iota(jnp.int32, (1, nseg), 1)
    vis = (nidx * CMP_STRIDE + (CMP_LEN - 1)) <= tpos_rows
    sm = jnp.where(vis, s, NEG_BIG)
    mx = jnp.max(sm, axis=1, keepdims=True)
    e = jnp.where(vis, jnp.exp2(sm - mx), 0.0)
    d = jnp.sum(e, axis=1, keepdims=True)
    p = e / jnp.where(d > 0, d, 1.0)
    o = _bdot(p, vc_b)
    imp = p[0:n_tok]
    for hh in range(1, A_HPG):
        imp = imp + p[hh * n_tok:(hh + 1) * n_tok]
    return o, imp


def _masked_attn_direct(q_g, k_parts, v_parts, allowed_parts, feature_major):
    ss = [jnp.where(al, _bdot(q_g, kk) if fm else _bdot_t(q_g, kk), NEG_BIG)
          for kk, al, fm in zip(k_parts, allowed_parts, feature_major)]
    mx = ss[0].max(axis=1, keepdims=True)
    for s in ss[1:]:
        mx = jnp.maximum(mx, s.max(axis=1, keepdims=True))
    num = None
    den = None
    for s, al, vv, fm in zip(ss, allowed_parts, v_parts, feature_major):
        e = jnp.where(al, jnp.exp2(s - mx), 0.0)
        dd = jnp.sum(e, axis=1, keepdims=True)
        oo = _bdot_t(e, vv) if fm else _bdot(e, vv)
        num = oo if num is None else num + oo
        den = dd if den is None else den + dd
    return num / jnp.where(den > 0, den, 1.0)


def _assemble_heads(o_groups, n_tok):
    pieces = []
    for g in range(A_KV):
        for hh in range(A_HPG):
            pieces.append(o_groups[g][hh * n_tok:(hh + 1) * n_tok, g * A_DH:(g + 1) * A_DH])
    return jnp.concatenate(pieces, axis=1)


def _lane_rep(a, rep):
    return a if rep == 1 else jnp.concatenate([a] * rep, axis=1)


def _nsa_prompt_kernel(q_ref, qr_ref, small_ref, rows_ref, win_ref, wbd_ref, pe_ref, kg0_ref,
                       pool_ref, o_ref,
                       kraw_sc, vraw_sc, kc_sc, vct_sc, sel_sc, m_sc, acc_sc, s_sc, *, T, tq, kc_len):
    qi = pl.program_id(1)
    nseg = T // CMP_STRIDE
    nsb = T // SEL_LEN
    bpc = kc_len // SEL_LEN

    @pl.when(qi == 0)
    def _():
        kraw_sc[...] = rows_ref[:, 0:LANES]
        vraw_sc[...] = rows_ref[:, LANES:2 * LANES]
        kc, vc = _compress(kraw_sc, vraw_sc, nseg, wbd_ref, pe_ref, kg0_ref[...])
        kc_sc[...] = kc
        vct_sc[...] = jnp.transpose(vc)

    t0 = qi * tq
    tpos = t0 + lax.broadcasted_iota(jnp.int32, (1, tq), 1)
    tpos4 = _lane_rep(tpos, A_HPG)
    q = q_ref[...]
    qr = qr_ref[...]
    small_t = jnp.transpose(small_ref[...])
    kc_b = kc_sc[...].astype(BF16)
    vct_b = vct_sc[...].astype(BF16)
    bidx = lax.broadcasted_iota(jnp.int32, (nsb, tq), 0)
    cur = tpos // SEL_LEN
    vis = (lax.broadcasted_iota(jnp.int32, (nseg, 1), 0) * CMP_STRIDE + (CMP_LEN - 1)) <= tpos4
    qr_gs = [_stack_heads(qr, g) for g in range(A_KV)]
    o_cmps = []
    for g in range(A_KV):
        sm = jnp.where(vis, _bdot_t(kc_b, _stack_heads(q, g)), NEG_BIG)
        mx = jnp.max(sm, axis=0, keepdims=True)
        e = jnp.where(vis, jnp.exp2(sm - mx), 0.0)
        d = jnp.sum(e, axis=0, keepdims=True)
        p = e / jnp.where(d > 0, d, 1.0)
        o_cmps.append(jnp.dot(vct_b, p.astype(BF16), preferred_element_type=F32))
        imp = p[:, 0:tq]
        for hh in range(1, A_HPG):
            imp = imp + p[:, hh * tq:(hh + 1) * tq]
        ih, il = _split(imp)
        imp_t = (jnp.dot(pool_ref[...], ih, preferred_element_type=F32)
                 + jnp.dot(pool_ref[...], il, preferred_element_type=F32))[0:nsb]
        val = jnp.where(bidx < cur, imp_t, -1.0)
        rank = jnp.zeros((nsb, tq), F32)
        for bp in range(nsb):
            vb = val[bp:bp + 1, :]
            rank = rank + jnp.where(vb > val, 1.0, jnp.where((vb == val) & (bidx > bp), 1.0, 0.0))
        sel_sc[g] = jnp.where(((rank < (N_SEL - 1)) & (bidx < cur)) | (bidx == cur), 1.0, 0.0)

    m_sc[...] = jnp.full(m_sc.shape, M_INIT, F32)
    acc_sc[...] = jnp.zeros(acc_sc.shape, F32)

    def with_ones_row(vt_, g):
        vb = vt_.astype(BF16)
        r0, pad = (1 - g) * A_DH, 2 * SUBLANES
        ones = jnp.ones((pad, vb.shape[1]), BF16)
        return jnp.concatenate(([vb[0:r0]] if r0 else []) + [ones, vb[r0 + pad:]], axis=0)

    def sel_body(c, carry):
        k0 = pl.multiple_of(c * kc_len, kc_len)
        kb = rows_ref[pl.ds(k0, kc_len), 2 * LANES:3 * LANES].astype(BF16)
        vt = jnp.transpose(rows_ref[pl.ds(k0, kc_len), 3 * LANES:4 * LANES])
        causal = (k0 + lax.broadcasted_iota(jnp.int32, (kc_len, 1), 0)) <= tpos
        for g in range(A_KV):
            s_sc[g, 0:kc_len, :] = _bdot_t(kb, qr_gs[g])
        for g in range(A_KV):
            selc = sel_sc[g, pl.ds(pl.multiple_of(c * bpc, bpc), bpc), :]
            selx = jnp.concatenate([jnp.broadcast_to(selc[j:j + 1, :], (SEL_LEN, tq)) for j in range(bpc)], axis=0)
            bias = jnp.where(causal & (selx > 0.5), 0.0, NEG_BIG)
            sm = s_sc[g, 0:kc_len, :] + _lane_rep(bias, A_HPG)
            m_prev = m_sc[g]
            m_new = jnp.maximum(m_prev, jnp.max(sm, axis=0, keepdims=True))
            alpha = jnp.exp2(m_prev - m_new)
            p = jnp.exp2(sm - m_new)
            acc_sc[g] = alpha * acc_sc[g] + jnp.dot(with_ones_row(vt, g), p.astype(BF16),
                                                    preferred_element_type=F32)
            m_sc[g] = m_new
        return carry

    lax.fori_loop(0, (t0 + tq + kc_len - 1) // kc_len, sel_body, 0)

    wk = min(WINDOW + tq, T)
    w0 = pl.multiple_of(jnp.clip(t0 + tq - wk, 0, T - wk), tq)
    kw = win_ref[pl.ds(w0, wk), 0:LANES].astype(BF16)
    vwt = jnp.transpose(win_ref[pl.ds(w0, wk), LANES:2 * LANES])
    wdiff = tpos - (w0 + lax.broadcasted_iota(jnp.int32, (wk, 1), 0))
    wbias = _lane_rep(jnp.where((wdiff >= 0) & (wdiff < WINDOW), 0.0, NEG_BIG), A_HPG)

    def gate_row(g, br):
        cols = [2 * M_HEADS + (g * A_HPG + hh) * 3 + br for hh in range(A_HPG)]
        return jnp.concatenate([_sigmoid(small_t[c0:c0 + 1, :]) for c0 in cols], axis=1)

    for g in range(A_KV):
        s_sc[g, 0:wk, :] = _bdot_t(kw, qr_gs[g])
    o_ts = []
    for g in range(A_KV):
        den = (1 - g) * A_DH
        acc = acc_sc[g]
        l = acc[den:den + 1, :]
        o_sel = acc / jnp.where(l > 0, l, 1.0)
        sw = s_sc[g, 0:wk, :] + wbias
        pw = jnp.exp2(sw - jnp.max(sw, axis=0, keepdims=True))
        ow = jnp.dot(with_ones_row(vwt, g), pw.astype(BF16), preferred_element_type=F32)
        o_win = ow / ow[den:den + 1, :]
        o_ts.append(gate_row(g, 0) * o_cmps[g] + gate_row(g, 1) * o_sel + gate_row(g, 2) * o_win)
    for j in range(A_HEADS // 2):
        g, h0 = j // (A_HPG // 2), 2 * (j % (A_HPG // 2))
        og = o_ts[g][g * A_DH:(g + 1) * A_DH, :]
        pair = jnp.concatenate([og[:, h0 * tq:(h0 + 1) * tq], og[:, (h0 + 1) * tq:(h0 + 2) * tq]], axis=0)
        o_ref[:, j * LANES:(j + 1) * LANES] = jnp.transpose(pair)


def _nsa_prompt(q, qr, small, rows, win, wbd, pe, kg0, nb, T):
    tq = 128
    kc_len = _pick_tile(T, 512)
    nq = T // tq
    nseg = T // CMP_STRIDE
    nsb = T // SEL_LEN
    nsb_p = -(-nsb // SUBLANES) * SUBLANES
    pool = (jnp.arange(nsb_p)[:, None] == jnp.arange(nseg)[None, :] // (SEL_LEN // CMP_STRIDE)).astype(BF16)
    tile = lambda b, i: (b * nq + i, 0)
    per_b = lambda b, i: (b, 0)
    c2 = lambda b, i: (0, 0)
    c3 = lambda b, i: (0, 0, 0)
    c4 = A_HPG * tq
    return pl.pallas_call(
        functools.partial(_nsa_prompt_kernel, T=T, tq=tq, kc_len=kc_len),
        grid=(nb, nq),
        in_specs=[pl.BlockSpec((tq, A_WIDTH), tile), pl.BlockSpec((tq, A_WIDTH), tile),
                  pl.BlockSpec((tq, LANES), tile),
                  pl.BlockSpec((T, 4 * LANES), per_b), pl.BlockSpec((T, 2 * LANES), per_b),
                  pl.BlockSpec(wbd.shape, c3), pl.BlockSpec(pe.shape, c3), pl.BlockSpec(kg0.shape, c2),
                  pl.BlockSpec(pool.shape, c2)],
        out_specs=pl.BlockSpec((tq, A_WIDTH), tile),
        out_shape=jax.ShapeDtypeStruct((nb * T, A_WIDTH), F32),
        scratch_shapes=[pltpu.VMEM((T, LANES), F32), pltpu.VMEM((T, LANES), F32),
                        pltpu.VMEM((nseg, LANES), F32), pltpu.VMEM((LANES, nseg), F32),
                        pltpu.VMEM((A_KV, nsb, tq), F32),
                        pltpu.VMEM((A_KV, 1, c4), F32),
                        pltpu.VMEM((A_KV, LANES, c4), F32),
                        pltpu.VMEM((A_KV, max(kc_len, min(WINDOW + tq, T)), c4), F32)],
        compiler_params=_cparams(("parallel", "arbitrary")),
        name="nsa_prompt",
    )(q, qr, small, rows, win, wbd, pe, kg0, pool)


def _nsa_sample_kernel(pt_ref, cache_ref, q_ref, qr_ref, small_ref, rows_ref, winnew_ref, winbuf_ref,
                       wbd_ref, pe_ref, kg0_ref, pool_ref, expand_ref,
                       o_ref, winout_ref,
                       cmp_buf, sel_buf, xperm_sc, sems, *, n_pages, past_len, t_valid):
    b = pl.program_id(0)
    nb = pl.num_programs(0)
    tp = SAMPLE_PAD_T
    nseg = past_len // CMP_STRIDE
    nsb = past_len // SEL_LEN
    wbuf = winbuf_ref.shape[1]

    def page_copies(bb, p, phase):
        page = pt_ref[bb * n_pages + p]
        dst_lanes = pl.ds(pl.multiple_of(p * PAGE_SIZE, PAGE_SIZE), PAGE_SIZE)
        if phase == 0:
            return [pltpu.make_async_copy(cache_ref.at[page, pl.ds(0, 2 * LANES), :],
                                          cmp_buf.at[:, dst_lanes], sems.at[0])]
        return [pltpu.make_async_copy(cache_ref.at[page, pl.ds(2 * LANES, 2 * LANES), :],
                                      sel_buf.at[:, dst_lanes], sems.at[1])]

    def start_all(bb, phase):
        def body(p, c):
            for cp in page_copies(bb, p, phase):
                cp.start()
            return c
        lax.fori_loop(0, n_pages, body, 0)

    def wait_all(bb, phase):
        def body(p, c):
            for cp in page_copies(bb, p, phase):
                cp.wait()
            return c
        lax.fori_loop(0, n_pages, body, 0)

    @pl.when(b == 0)
    def _():
        start_all(b, 0)

    start_all(b, 1)
    wait_all(b, 0)

    seg_pp = PAGE_SIZE // CMP_STRIDE
    pr = lax.broadcasted_iota(jnp.int32, (PAGE_SIZE, PAGE_SIZE), 0)
    pc = lax.broadcasted_iota(jnp.int32, (PAGE_SIZE, PAGE_SIZE), 1)
    perm = jnp.where(pc == CMP_STRIDE * (pr % seg_pp) + pr // seg_pp, 1.0, 0.0).astype(BF16)
    for p in range(n_pages):
        xp = _bdot_t(perm, cmp_buf[:, p * PAGE_SIZE:(p + 1) * PAGE_SIZE])
        for l in range(CMP_STRIDE):
            xperm_sc[l, p * seg_pp:(p + 1) * seg_pp, :] = xp[l * seg_pp:(l + 1) * seg_pp, :]
    kc, vc = _compress_grouped(xperm_sc, nseg, wbd_ref, pe_ref, kg0_ref[...])
    kc_b = kc.astype(BF16)
    vc_b = vc.astype(BF16)
    q = q_ref[...]
    qr = qr_ref[...]
    small = small_ref[...]
    tpos_col = past_len + lax.broadcasted_iota(jnp.int32, (tp, 1), 0)
    tpos_rows = jnp.concatenate([tpos_col] * A_HPG, axis=0)
    bp_idx = lax.broadcasted_iota(jnp.int32, (nsb, nsb), 0)
    b_idx = lax.broadcasted_iota(jnp.int32, (nsb, nsb), 1)
    o_cmps = []
    sels = []
    for g in range(A_KV):
        qn_g = _stack_heads(q, g)
        o_cmp, imp = _cmp_branch(qn_g, kc_b, vc_b, tpos_rows, nseg, tp)
        o_cmps.append(o_cmp)
        imp_sel = _dot2_exact_rhs(imp, pool_ref[...])
        imp_pad = jnp.concatenate([imp_sel, jnp.zeros((nsb - tp, nsb), F32)], axis=0)
        imp_t = jnp.transpose(imp_pad)
        rows_sel = []
        for t in range(tp):
            if t < t_valid:
                row_t = imp_sel[t:t + 1, :]
                col_t = imp_t[:, t:t + 1]
                ahead = jnp.where(col_t > row_t, 1.0, jnp.where((col_t == row_t) & (bp_idx < b_idx), 1.0, 0.0))
                rank = jnp.sum(ahead, axis=0, keepdims=True)
                rows_sel.append(jnp.where(rank < (N_SEL - 1), 1.0, 0.0))
            else:
                rows_sel.append(jnp.zeros((1, nsb), F32))
        sels.append(jnp.concatenate(rows_sel, axis=0).astype(BF16))

    @pl.when(b + 1 < nb)
    def _():
        start_all(b + 1, 0)

    wait_all(b, 1)

    new_idx = lax.broadcasted_iota(jnp.int32, (tp, tp), 1)
    tok_idx = lax.broadcasted_iota(jnp.int32, (tp, tp), 0)
    new_ok = jnp.concatenate([jnp.where(new_idx <= tok_idx, 1.0, 0.0)] * A_HPG, axis=0) > 0.5
    wpos = past_len - wbuf + lax.broadcasted_iota(jnp.int32, (1, wbuf), 1)
    wdiff = tpos_col - wpos
    win_ok = jnp.concatenate([jnp.where((wdiff >= 0) & (wdiff < WINDOW), 1.0, 0.0)] * A_HPG, axis=0) > 0.5
    k_past = sel_buf[0:LANES, :].astype(BF16)
    v_past = sel_buf[LANES:2 * LANES, :].astype(BF16)
    k_new = rows_ref[:, 2 * LANES:3 * LANES]
    v_new = rows_ref[:, 3 * LANES:4 * LANES]
    kw_past = winbuf_ref[0:LANES, :]
    vw_past = winbuf_ref[LANES:2 * LANES, :]
    kw_new = winnew_ref[:, 0:LANES]
    vw_new = winnew_ref[:, LANES:2 * LANES]
    o_groups = []
    for g in range(A_KV):
        qr_g = _stack_heads(qr, g)
        mk = jnp.dot(sels[g], expand_ref[...], preferred_element_type=F32)
        past_ok = jnp.concatenate([mk] * A_HPG, axis=0) > 0.5
        o_sel = _masked_attn_direct(qr_g, [k_past, k_new], [v_past, v_new], [past_ok, new_ok], [True, False])
        o_win = _masked_attn_direct(qr_g, [kw_past, kw_new], [vw_past, vw_new], [win_ok, new_ok], [True, False])
        o_groups.append(_gate_cols(small, g, 0) * o_cmps[g] + _gate_cols(small, g, 1) * o_sel
                        + _gate_cols(small, g, 2) * o_win)
    o_ref[...] = _assemble_heads(o_groups, tp)

    rolled = pltpu.roll(winbuf_ref[...], wbuf - t_valid, 1)
    new_t = jnp.transpose(jnp.concatenate([winnew_ref[...], jnp.zeros((LANES - tp, 2 * LANES), F32)], axis=0))
    new_t = pltpu.roll(new_t, LANES - t_valid, 1)
    lane = lax.broadcasted_iota(jnp.int32, (2 * LANES, LANES), 1)
    winout_ref[:, 0:wbuf - LANES] = rolled[:, 0:wbuf - LANES]
    winout_ref[:, wbuf - LANES:wbuf] = jnp.where(lane < LANES - t_valid, rolled[:, wbuf - LANES:wbuf], new_t)


def _nsa_sample(page_table, cache, q, qr, small, rows, winnew, winbuf, wbd, pe, kg0, t_valid):
    nb, n_pages = page_table.shape
    past_len = n_pages * PAGE_SIZE
    nseg = past_len // CMP_STRIDE
    nsb = past_len // SEL_LEN
    tp = SAMPLE_PAD_T
    wbuf = winbuf.shape[2]
    pool = (jnp.arange(nseg)[:, None] // (SEL_LEN // CMP_STRIDE) == jnp.arange(nsb)[None, :]).astype(BF16)
    expand = (jnp.arange(nsb)[:, None] == jnp.arange(past_len)[None, :] // SEL_LEN).astype(BF16)
    tile = lambda b, pt: (b, 0)
    c2 = lambda b, pt: (0, 0)
    c3 = lambda b, pt: (0, 0, 0)
    gs = pltpu.PrefetchScalarGridSpec(
        num_scalar_prefetch=1,
        grid=(nb,),
        in_specs=[pl.BlockSpec(memory_space=pl.ANY),
                  pl.BlockSpec((tp, A_WIDTH), tile), pl.BlockSpec((tp, A_WIDTH), tile),
                  pl.BlockSpec((tp, LANES), tile), pl.BlockSpec((tp, 4 * LANES), tile),
                  pl.BlockSpec((tp, 2 * LANES), tile),
                  pl.BlockSpec((None, 2 * LANES, wbuf), lambda b, pt: (b, 0, 0)),
                  pl.BlockSpec(wbd.shape, c3), pl.BlockSpec(pe.shape, c3), pl.BlockSpec(kg0.shape, c2),
                  pl.BlockSpec(pool.shape, c2), pl.BlockSpec(expand.shape, c2)],
        out_specs=[pl.BlockSpec((tp, A_WIDTH), tile),
                   pl.BlockSpec((None, 2 * LANES, wbuf), lambda b, pt: (b, 0, 0))],
        scratch_shapes=[pltpu.VMEM((2 * LANES, past_len), F32), pltpu.VMEM((2 * LANES, past_len), F32),
                        pltpu.VMEM((CMP_STRIDE, past_len // CMP_STRIDE, 2 * LANES), F32),
                        pltpu.SemaphoreType.DMA((2,))],
    )
    return pl.pallas_call(
        functools.partial(_nsa_sample_kernel, n_pages=n_pages, past_len=past_len, t_valid=t_valid),
        grid_spec=gs,
        out_shape=[jax.ShapeDtypeStruct((nb * tp, A_WIDTH), F32),
                   jax.ShapeDtypeStruct((nb, 2 * LANES, wbuf), F32)],
        compiler_params=_cparams(("arbitrary",)),
        name="nsa_sample",
    )(page_table.reshape(-1), cache, q, qr, small, rows, winnew, winbuf, wbd, pe, kg0, pool, expand)


MOE_TM = 256
SEG_ALIGN = 8
MOE_RL = -(-(MOE_TM * TOP_K + N_EXPERTS * (SEG_ALIGN - 1)) // LANES) * LANES


def _pack_halves(x, bf16_exact=False):
    w = x.shape[1] // 2
    bits = lax.bitcast_convert_type(x if bf16_exact else x.astype(BF16).astype(F32), jnp.uint32)
    return bits[:, :w] | (bits[:, w:] >> 16)


def _unpack_halves(u):
    hi = lax.bitcast_convert_type(u & jnp.uint32(0xFFFF0000), F32).astype(BF16)
    lo = lax.bitcast_convert_type(u << 16, F32).astype(BF16)
    return hi, lo


def _route_and_sort(h2, wrt_ref, brt_ref, xsl_ref, info_ref, cnt_ref, tm, t_mod, t_valid, m_valid):
    ne = N_EXPERTS
    h2b = h2.astype(BF16)
    h2l = (h2 - h2b.astype(F32)).astype(BF16)
    wh, wl = _split(wrt_ref[...])
    lt = _bdot_t(wh, h2b) + _bdot_t(wl, h2b) + _bdot_t(wh, h2l) + brt_ref[...]
    eidx = lax.broadcasted_iota(jnp.int32, (ne, tm), 0)
    rank = jnp.zeros((ne, tm), F32)
    for ep in range(ne):
        v = lt[ep:ep + 1, :]
        rank = rank + jnp.where(v > lt, 1.0, jnp.where((v == lt) & (eidx > ep), 1.0, 0.0))
    sel = rank < TOP_K
    if t_mod is not None:
        tok = pl.program_id(0) * tm + lax.broadcasted_iota(jnp.int32, (1, tm), 1)
        sel = sel & ((tok % t_mod) < t_valid) & (tok < m_valid)
    mx = jnp.max(jnp.where(sel, lt, NEG_BIG), axis=0, keepdims=True)
    ex = jnp.where(sel, jnp.exp(lt - mx), 0.0)
    den = jnp.sum(ex, axis=0, keepdims=True)
    gate = ex / jnp.where(den > 0, den, 1.0)
    self_ = jnp.where(sel, 1.0, 0.0)
    selb = self_.astype(BF16)
    er = lax.broadcasted_iota(jnp.int32, (ne, ne), 0)
    ec = lax.broadcasted_iota(jnp.int32, (ne, ne), 1)
    c = jnp.dot(jnp.where(ec <= er, 1.0, 0.0).astype(BF16), selb, preferred_element_type=F32)
    tr = lax.broadcasted_iota(jnp.int32, (tm, tm), 0)
    tc = lax.broadcasted_iota(jnp.int32, (tm, tm), 1)
    rk = jnp.dot(selb, jnp.where(tr < tc, 1.0, 0.0).astype(BF16), preferred_element_type=F32)
    cnt = jnp.sum(self_, axis=1, keepdims=True)
    cnt_al = jnp.floor((cnt + (SEG_ALIGN - 1)) * (1.0 / SEG_ALIGN)) * SEG_ALIGN
    cnt_b = jnp.broadcast_to(cnt_al, (ne, LANES))
    cnt_ref[...] = cnt_b
    off = jnp.dot(jnp.where(ec < er, 1.0, 0.0).astype(BF16), cnt_b.astype(BF16), preferred_element_type=F32)
    rowidx = off[:, 0:1] + rk
    rows_k, gates_k, exps_k = [], [], []
    for k in range(1, TOP_K + 1):
        mk = sel & (c == k)
        has = jnp.sum(jnp.where(mk, 1.0, 0.0), axis=0, keepdims=True)
        rows_k.append(jnp.sum(jnp.where(mk, rowidx, 0.0), axis=0, keepdims=True) + has - 1.0)
        gates_k.append(jnp.sum(jnp.where(mk, gate, 0.0), axis=0, keepdims=True))
        exps_k.append(jnp.sum(jnp.where(mk, eidx.astype(F32), 0.0), axis=0, keepdims=True))
    info_ref[...] = jnp.concatenate(rows_k + gates_k + exps_k + [jnp.zeros((4, tm), F32)], axis=0)
    ridx = lax.broadcasted_iota(jnp.int32, (MOE_RL, tm), 0).astype(F32)
    perm = jnp.zeros((MOE_RL, tm), F32)
    for k in range(TOP_K):
        perm = jnp.where(ridx == rows_k[k], 1.0, perm)
    xs = jnp.dot(perm.astype(BF16), h2b, preferred_element_type=F32)
    xsl_ref[...] = _pack_halves(xs, bf16_exact=True)


def _mixout_kernel(x_ref, hm_ref, on_ref, mod_ref, gmix_ref, gffn_ref,
                   wog_ref, bog_ref, wum_ref, wua_ref, wout_ref, wrt_ref, brt_ref,
                   x1_ref, xsl_ref, info_ref, cnt_ref, *, tm, t_mod, t_valid, m_valid, n_real):
    if n_real is not None:
        @pl.when(pl.program_id(0) >= n_real)
        def _():
            xsl_ref[...] = jnp.zeros(xsl_ref.shape, jnp.uint32)
            info_ref[...] = jnp.zeros(info_ref.shape, F32)
            cnt_ref[...] = jnp.zeros(cnt_ref.shape, F32)

        @pl.when(pl.program_id(0) < n_real)
        def _():
            _mixout_body(x_ref, hm_ref, on_ref, mod_ref, gmix_ref, gffn_ref, wog_ref, bog_ref, wum_ref,
                         wua_ref, wout_ref, wrt_ref, brt_ref, x1_ref, xsl_ref, info_ref, cnt_ref,
                         tm, t_mod, t_valid, m_valid)
    else:
        _mixout_body(x_ref, hm_ref, on_ref, mod_ref, gmix_ref, gffn_ref, wog_ref, bog_ref, wum_ref,
                     wua_ref, wout_ref, wrt_ref, brt_ref, x1_ref, xsl_ref, info_ref, cnt_ref,
                     tm, t_mod, t_valid, m_valid)


def _mixout_body(x_ref, hm_ref, on_ref, mod_ref, gmix_ref, gffn_ref,
                 wog_ref, bog_ref, wum_ref, wua_ref, wout_ref, wrt_ref, brt_ref,
                 x1_ref, xsl_ref, info_ref, cnt_ref, tm, t_mod, t_valid, m_valid):
    d = D_MODEL
    x = x_ref[...]
    sh1, sc1, gt1 = mod_ref[:, 0:d], mod_ref[:, d:2 * d], mod_ref[:, 2 * d:3 * d]
    sh2, sc2 = mod_ref[:, 3 * d:4 * d], mod_ref[:, 4 * d:5 * d]
    h = _rmsnorm_rows(x, gmix_ref[...]) * (1.0 + sc1) + sh1
    hb = h.astype(BF16)
    mo = jnp.dot(hb, wog_ref[:, 0:M_WIDTH], preferred_element_type=F32) + bog_ref[:, 0:M_WIDTH]
    ym = _bdot(_sigmoid(mo) * hm_ref[...], wum_ref[...])
    ya = _bdot(on_ref[...], wua_ref[...])
    ga = jnp.dot(hb, wog_ref[:, M_WIDTH:M_WIDTH + d], preferred_element_type=F32) + bog_ref[:, M_WIDTH:M_WIDTH + d]
    u = _sigmoid(ga) * ym
    gb = (jnp.dot(hb, wog_ref[:, M_WIDTH + d:M_WIDTH + 2 * d], preferred_element_type=F32)
          + bog_ref[:, M_WIDTH + d:M_WIDTH + 2 * d])
    u = u + _sigmoid(gb) * ya
    x1 = x + gt1 * _bdot(u, wout_ref[...])
    x1_ref[...] = x1
    h2 = _rmsnorm_rows(x1, gffn_ref[...]) * (1.0 + sc2) + sh2
    _route_and_sort(h2, wrt_ref, brt_ref, xsl_ref, info_ref, cnt_ref, tm, t_mod, t_valid, m_valid)


def _mixout_with_shared(*refs, n_shared, **kw):
    n_in = 13
    _mixout_kernel(*refs[:n_in], *refs[n_in + n_shared:], **kw)


def _mixout(x2, hm, on, mod3, gmix, gffn, wts, tiles_per_mod, nt_total, tile0=0, shared=None,
            t_mod=None, t_valid=None, m_valid=None):
    m = x2.shape[0]
    tm = MOE_TM
    nt = m // tm
    (wog, bog, wum, wua, wout, wr, br) = wts
    r = mod3.shape[1]
    n_extra = nt_total - tile0 - nt if shared is None else 0
    row = lambda i: (jnp.minimum(i, nt - 1), 0)
    const = lambda i: (0, 0)
    in_specs = [pl.BlockSpec((tm, D_MODEL), row), pl.BlockSpec((tm, M_WIDTH), row),
                pl.BlockSpec((tm, A_WIDTH), row),
                pl.BlockSpec((None, r, 6 * D_MODEL), lambda i: (jnp.minimum(i, nt - 1) // tiles_per_mod, 0, 0)),
                pl.BlockSpec((1, D_MODEL), const), pl.BlockSpec((1, D_MODEL), const),
                pl.BlockSpec(wog.shape, const), pl.BlockSpec(bog.shape, const),
                pl.BlockSpec(wum.shape, const), pl.BlockSpec(wua.shape, const),
                pl.BlockSpec(wout.shape, const), pl.BlockSpec(wr.shape, const),
                pl.BlockSpec(br.shape, const)]
    args = [x2, hm, on, mod3, gmix, gffn, wog, bog, wum, wua, wout, wr, br]
    kw = dict(tm=tm, t_mod=t_mod, t_valid=t_valid, m_valid=m_valid, n_real=nt if n_extra else None)
    body = functools.partial(_mixout_kernel, **kw)
    aliases = {}
    if shared is not None:
        in_specs += [pl.BlockSpec(memory_space=pl.ANY)] * len(shared)
        aliases = {len(args) + j: 1 + j for j in range(len(shared))}
        args += list(shared)
        body = functools.partial(_mixout_with_shared, n_shared=len(shared), **kw)
    return pl.pallas_call(
        body,
        grid=(nt + n_extra,),
        in_specs=in_specs,
        out_specs=[pl.BlockSpec((tm, D_MODEL), row),
                   pl.BlockSpec((MOE_RL, D_MODEL // 2), lambda i: (tile0 + i, 0)),
                   pl.BlockSpec((16, tm), lambda i: (0, tile0 + i)),
                   pl.BlockSpec((None, N_EXPERTS, LANES), lambda i: (tile0 + i, 0, 0))],
        out_shape=[jax.ShapeDtypeStruct((m, D_MODEL), F32),
                   jax.ShapeDtypeStruct((nt_total * MOE_RL, D_MODEL // 2), jnp.uint32),
                   jax.ShapeDtypeStruct((16, nt_total * tm), F32),
                   jax.ShapeDtypeStruct((nt_total, N_EXPERTS, LANES), F32)],
        input_output_aliases=aliases,
        compiler_params=_cparams(("arbitrary" if n_extra else "parallel",)),
        name="mixout",
    )(*args)


MOE_BM = 256
MOE_CH = 512


def _moe_kernel(be_ref, na_ref, grp_ref,
                xsl_ref, wgu_ref, bgu_ref, wdn_ref, bdn_ref, ysl_ref,
                wgu_bf, wdn_bf, xbuf, ybuf, sem_in, sem_out, *, trash_row0):
    i = pl.program_id(0)
    na = na_ref[0]
    e = be_ref[i]
    prev = be_ref[jnp.maximum(i - 1, 0)]
    n_grp = MOE_BM // SEG_ALIGN

    def group_copies(blk, inbound, slot=None):
        slot = blk % 2 if slot is None else slot
        cps = []
        for r in range(n_grp):
            v = grp_ref[blk * n_grp + r]
            vm_rows = pl.ds(r * SEG_ALIGN, SEG_ALIGN)
            if inbound:
                row = pl.multiple_of(jnp.where(v >= 0, v, trash_row0 + 2 * MOE_BM), SEG_ALIGN)
                cps.append(pltpu.make_async_copy(xsl_ref.at[pl.ds(row, SEG_ALIGN), :],
                                                 xbuf.at[slot, vm_rows, :], sem_in.at[slot]))
            else:
                spare = trash_row0 + slot * MOE_BM + r * SEG_ALIGN
                row = pl.multiple_of(jnp.where(v >= 0, v, spare), SEG_ALIGN)
                cps.append(pltpu.make_async_copy(ybuf.at[slot, vm_rows, :],
                                                 ysl_ref.at[pl.ds(row, SEG_ALIGN), :], sem_out.at[slot]))
        return cps

    def start_gather(blk):
        for cp in group_copies(blk, True):
            cp.start()

    def start_scatter(blk):
        for cp in group_copies(blk, False):
            cp.start()

    def wait_rows(blk, sem, inbound):
        slot = blk % 2
        if inbound:
            pltpu.make_async_copy(xsl_ref.at[pl.ds(0, MOE_BM), :], xbuf.at[slot], sem.at[slot]).wait()
        else:
            pltpu.make_async_copy(ybuf.at[slot], ysl_ref.at[pl.ds(0, MOE_BM), :], sem.at[slot]).wait()

    @pl.when(i == 0)
    def _():
        start_gather(i)

    @pl.when(i + 1 < na)
    def _():
        start_gather(i + 1)

    @pl.when((i < na) & ((i == 0) | (e != prev)))
    def _():
        for j in range(2 * D_EXPERT // MOE_CH):
            wgu_bf[:, j * MOE_CH:(j + 1) * MOE_CH] = wgu_ref[:, j * MOE_CH:(j + 1) * MOE_CH].astype(BF16)
        for j in range(D_EXPERT // MOE_CH):
            wdn_bf[j * MOE_CH:(j + 1) * MOE_CH, :] = wdn_ref[j * MOE_CH:(j + 1) * MOE_CH, :].astype(BF16)

    @pl.when(i < na)
    def _():
        slot = i % 2
        wait_rows(i, sem_in, True)

        @pl.when(i >= 2)
        def _():
            wait_rows(i - 2, sem_out, False)

        half = D_MODEL // 2
        xh, xl = _unpack_halves(xbuf[slot])

        def xdot(c0, c1):
            return (jnp.dot(xh, wgu_bf[0:half, c0:c1], preferred_element_type=F32)
                    + jnp.dot(xl, wgu_bf[half:D_MODEL, c0:c1], preferred_element_type=F32))

        acc = jnp.zeros((MOE_BM, D_MODEL), F32) + bdn_ref[...]
        for j in range(D_EXPERT // MOE_CH):
            lo, hi = j * MOE_CH, (j + 1) * MOE_CH
            gj = xdot(lo, hi) + bgu_ref[:, lo:hi]
            uj = xdot(D_EXPERT + lo, D_EXPERT + hi) + bgu_ref[:, D_EXPERT + lo:D_EXPERT + hi]
            gj = jnp.minimum(gj, SWIGLU_LIMIT)
            uj = jnp.clip(uj, -SWIGLU_LIMIT, SWIGLU_LIMIT)
            act = gj * _sigmoid(SWIGLU_ALPHA * gj) * (uj + 1.0)
            acc = acc + jnp.dot(act.astype(BF16), wdn_bf[lo:hi, :], preferred_element_type=F32)
        ybuf[slot] = _pack_halves(acc)
        start_scatter(i)

        @pl.when(i == na - 1)
        def _():
            @pl.when(i >= 1)
            def _():
                wait_rows(i - 1, sem_out, False)
            wait_rows(i, sem_out, False)


def _moe_experts(plan, xsl, w_gu, b_gu, w_dn, b_dn):
    block_e, n_active, grp_rows = plan
    nblk = block_e.shape[0]
    spare_row0 = xsl.shape[0] - MOE_RL
    assert MOE_RL >= 2 * MOE_BM
    wmap = lambda i, be, *_: (be[i], 0, 0)
    anyspec = pl.BlockSpec(memory_space=pl.ANY)
    gs = pltpu.PrefetchScalarGridSpec(
        num_scalar_prefetch=3,
        grid=(nblk,),
        in_specs=[anyspec,
                  pl.BlockSpec((None, D_MODEL, 2 * D_EXPERT), wmap),
                  pl.BlockSpec((None, 1, 2 * D_EXPERT), wmap),
                  pl.BlockSpec((None, D_EXPERT, D_MODEL), wmap),
                  pl.BlockSpec((None, 1, D_MODEL), wmap)],
        out_specs=anyspec,
        scratch_shapes=[pltpu.VMEM((D_MODEL, 2 * D_EXPERT), BF16), pltpu.VMEM((D_EXPERT, D_MODEL), BF16),
                        pltpu.VMEM((2, MOE_BM, D_MODEL // 2), jnp.uint32),
                        pltpu.VMEM((2, MOE_BM, D_MODEL // 2), jnp.uint32),
                        pltpu.SemaphoreType.DMA((2,)), pltpu.SemaphoreType.DMA((2,))],
    )
    return pl.pallas_call(
        functools.partial(_moe_kernel, trash_row0=spare_row0),
        grid_spec=gs,
        out_shape=jax.ShapeDtypeStruct(xsl.shape, jnp.uint32),
        input_output_aliases={3: 0},
        compiler_params=_cparams(("arbitrary",)),
        name="moe_experts",
    )(*plan, xsl, w_gu, b_gu.reshape(N_EXPERTS, 1, -1), w_dn, b_dn.reshape(N_EXPERTS, 1, -1))


def _combine_kernel(ysl_ref, info_ref, x1_ref, mod_ref, y_ref, *, tm):
    info = info_ref[...]
    info_t = jnp.transpose(jnp.concatenate([info, jnp.zeros((LANES - info.shape[0], tm), F32)], axis=0))
    ridx = lax.broadcasted_iota(jnp.int32, (tm, MOE_RL), 1).astype(F32)
    pg = jnp.zeros((tm, MOE_RL), F32)
    for k in range(TOP_K):
        pg = jnp.where(ridx == info_t[:, k:k + 1], info_t[:, TOP_K + k:TOP_K + k + 1], pg)
    pgb = pg.astype(BF16)
    yh, yl = _unpack_halves(ysl_ref[...])
    half = D_MODEL // 2
    gt2 = mod_ref[:, 5 * D_MODEL:6 * D_MODEL]
    for c, yy in ((0, yh), (1, yl)):
        moe = jnp.dot(pgb, yy, preferred_element_type=F32)
        y_ref[:, c * half:(c + 1) * half] = (x1_ref[:, c * half:(c + 1) * half]
                                             + gt2[:, c * half:(c + 1) * half] * moe)


def _combine(ysl, info, x1, mod3, tiles_per_mod, tile0=0):
    m = x1.shape[0]
    tm = MOE_TM
    r = mod3.shape[1]
    return pl.pallas_call(
        functools.partial(_combine_kernel, tm=tm),
        grid=(m // tm,),
        in_specs=[pl.BlockSpec((MOE_RL, D_MODEL // 2), lambda i: (tile0 + i, 0)),
                  pl.BlockSpec((16, tm), lambda i: (0, tile0 + i)),
                  pl.BlockSpec((tm, D_MODEL), lambda i: (i, 0)),
                  pl.BlockSpec((None, r, 6 * D_MODEL), lambda i: (i // tiles_per_mod, 0, 0))],
        out_specs=pl.BlockSpec((tm, D_MODEL), lambda i: (i, 0)),
        out_shape=jax.ShapeDtypeStruct((m, D_MODEL), F32),
        compiler_params=_cparams(("parallel",)),
        name="moe_combine",
    )(ysl, info, x1, mod3)


def _moe_plan(cnt):
    cnt = cnt.astype(jnp.int32)
    nt = cnt.shape[0]
    so = jnp.cumsum(cnt, axis=1) - cnt + (jnp.arange(nt) * MOE_RL)[:, None]
    ce = jnp.cumsum(cnt, axis=0)
    cs = ce - cnt
    tot = ce[-1]
    nblk_e = (tot + MOE_BM - 1) // MOE_BM
    blk_end = jnp.cumsum(nblk_e)
    max_rows = nt * MOE_TM * TOP_K + nt * N_EXPERTS * (SEG_ALIGN - 1)
    n_blocks = -(-max_rows // MOE_BM) + N_EXPERTS
    bidx = jnp.arange(n_blocks)
    block_e = jnp.minimum(jnp.sum(blk_end[None, :] <= bidx[:, None], axis=1), N_EXPERTS - 1).astype(jnp.int32)
    is_e = (jnp.arange(N_EXPERTS)[:, None] == block_e[None, :]).astype(jnp.int32)
    per_block = lambda a: jnp.sum(a[..., :, None] * is_e, axis=-2)
    block_r0 = (bidx - per_block(blk_end - nblk_e)) * MOE_BM
    x = block_r0[:, None] + jnp.arange(MOE_BM // SEG_ALIGN)[None, :] * SEG_ALIGN
    ce_b = per_block(ce)[:, :, None]
    cs_b = per_block(cs)[:, :, None]
    inside = (cs_b <= x[None]) & (x[None] < ce_b)
    grp = x + jnp.sum(jnp.where(inside, per_block(so - cs)[:, :, None], 0), axis=0)
    grp = jnp.where(x < per_block(tot)[:, None], grp, -1)
    n_active = blk_end[-1].reshape(1)
    i32 = lambda a: a.reshape(-1).astype(jnp.int32)
    return block_e, i32(n_active), i32(grp)


def _rope_tables(pos):
    half = ROT_DIM // 2
    inv = ROPE_THETA ** (-jnp.arange(half, dtype=F32) * (2.0 / ROT_DIM))
    ang = pos.astype(F32)[:, None] * inv[None, :]
    cos, sin = jnp.cos(ang), jnp.sin(ang)
    n = pos.shape[0]
    ones = jnp.ones((n, A_DH - ROT_DIM), F32)
    zeros_h = jnp.zeros((n, half), F32)
    zeros_r = jnp.zeros((n, A_DH - ROT_DIM), F32)
    cos64 = jnp.concatenate([cos, cos, ones], axis=1)
    sprev64 = jnp.concatenate([zeros_h, sin, zeros_r], axis=1)
    snext64 = jnp.concatenate([-sin, zeros_h, zeros_r], axis=1)
    two = lambda a: jnp.concatenate([a, a], axis=1)
    return two(cos64), two(sprev64), two(snext64)


def _prep_weights(w_in, b_in, q_norm_g, k_norm_g, cmp_pe_k, cmp_pe_v, cmp_w_k, cmp_w_v,
                  w_up_m, w_up_a, w_out, w_router, b_router):
    b2 = b_in.reshape(1, N_IN)
    wm = w_in[:, OFF_MQ:OFF_MO].astype(BF16)
    bm = b2[:, OFF_MQ:OFF_MO]
    wq = w_in[:, OFF_AQ:OFF_AKV].astype(BF16)
    bq = b2[:, OFF_AQ:OFF_AKV]
    wkv = w_in[:, OFF_AKV:OFF_AG].astype(BF16)
    bkv = b2[:, OFF_AKV:OFF_AG]
    n_small = 2 * M_HEADS + 3 * A_HEADS
    ws = jnp.concatenate([w_in[:, OFF_MI:OFF_AQ], w_in[:, OFF_AG:OFF_GA],
                          jnp.zeros((D_MODEL, LANES - n_small), F32)], axis=1)
    bs = jnp.concatenate([b2[:, OFF_MI:OFF_AQ], b2[:, OFF_AG:OFF_GA], jnp.zeros((1, LANES - n_small), F32)], axis=1)
    qg = jnp.tile(q_norm_g, A_HEADS).reshape(1, A_WIDTH)
    kg = jnp.stack([jnp.tile(k_norm_g[1], A_KV), jnp.tile(k_norm_g[2], A_KV)], axis=0)
    kg0 = jnp.tile(k_norm_g[0], A_KV).reshape(1, LANES)
    hid = jnp.arange(A_WIDTH) // A_DH
    bd = jnp.where(hid[:, None] == hid[None, :], 1.0 / A_DH, 0.0).astype(BF16)
    inproj_w = (wm, bm, wq, bq, wkv, bkv, ws, bs, qg, kg, bd)

    z = jnp.zeros((CMP_LEN, A_DH, A_DH), F32)
    r0 = jnp.concatenate([cmp_w_k, z, z, z], axis=2)
    r1 = jnp.concatenate([z, cmp_w_k, z, z], axis=2)
    r2 = jnp.concatenate([z, z, cmp_w_v, z], axis=2)
    r3 = jnp.concatenate([z, z, z, cmp_w_v], axis=2)
    wbd = jnp.concatenate([r0, r1, r2, r3], axis=1).astype(BF16)
    pe = jnp.concatenate([cmp_pe_k, cmp_pe_k, cmp_pe_v, cmp_pe_v], axis=1).reshape(CMP_LEN, 1, 2 * LANES)

    wog = jnp.concatenate([w_in[:, OFF_MO:OFF_MI], w_in[:, OFF_GA:N_IN]], axis=1).astype(BF16)
    bog = jnp.concatenate([b2[:, OFF_MO:OFF_MI], b2[:, OFF_GA:N_IN]], axis=1)
    mixout_w = (wog, bog, w_up_m.astype(BF16), w_up_a.astype(BF16), w_out.astype(BF16),
                w_router.T, b_router.reshape(N_EXPERTS, 1))
    return inproj_w, (wbd, pe, kg0), mixout_w


def _pick_tile(m, pref):
    t = pref
    while m % t:
        t //= 2
    return t


def kernel(x_prompt, x_sample, cache_nsa_kv, state_win_kv, state_mlstm_C, state_mlstm_n, state_mlstm_m, page_table, c_prompt, c_sample, w_ada, b_ada, g_mix, g_ffn, w_in, b_in, q_norm_g, k_norm_g, cmp_pe_k, cmp_pe_v, cmp_w_k, cmp_w_v, w_up_m, w_up_a, w_out, w_router, b_router, w_gu, b_gu, w_dn, b_dn):
    depth = w_in.shape[0]
    assert depth == 1
    B, T, D = x_prompt.shape
    DB, TS, _ = x_sample.shape
    n_pages = page_table.shape[1]
    past_len = n_pages * PAGE_SIZE
    wbuf = state_win_kv.shape[2]
    tp = SAMPLE_PAD_T
    assert TS <= tp and wbuf % tp == 0 and T % 128 == 0

    l = 0
    inproj_w, cmp_w, mixout_w = _prep_weights(
        w_in[l], b_in[l], q_norm_g[l], k_norm_g[l], cmp_pe_k[l], cmp_pe_v[l], cmp_w_k[l], cmp_w_v[l],
        w_up_m[l], w_up_a[l], w_out[l], w_router[l], b_router[l])
    wbd, pe, kg0 = cmp_w
    gmix = g_mix[l].reshape(1, D)
    gffn = g_ffn[l].reshape(1, D)

    nc = B + DB
    nc_pad = -(-nc // SUBLANES) * SUBLANES
    c_all = jnp.concatenate([c_prompt, c_sample, jnp.zeros((nc_pad - nc, D), F32)], axis=0)
    mod = _adaln(c_all, w_ada[l], b_ada[l])
    mod_p = mod[:B].reshape(B, 1, 6 * D)
    mod_s = jnp.repeat(mod[B:B + DB], tp, axis=0).reshape(1, DB * tp, 6 * D)

    mp = B * T
    tm = _pick_tile(T, 256)
    xp = x_prompt.reshape(mp, D)
    tabs_p = _rope_tables(jnp.arange(T, dtype=jnp.int32))
    mq, mk, mv, q, qr, rows, win, small, rows_t, win_t = _inproj(xp, mod_p, gmix, tabs_p, inproj_w, tm, T // tm,
                                                                 T // tm, rows_t_batches=B)
    Lp = _pick_tile(T, 128)
    hm, C_p, n_p, m_p = _mlstm(mq, mk, mv, small, B, T, T, Lp)
    o_nsa = _nsa_prompt(q, qr, small, rows, win, wbd, pe, kg0, B, T)
    assert T % MOE_TM == 0
    ms_pad = -(-(DB * tp) // MOE_TM) * MOE_TM
    nt_p = mp // MOE_TM
    nt_all = nt_p + ms_pad // MOE_TM + 1
    x1_p, xsl, info, cnt = _mixout(xp, hm, o_nsa, mod_p, gmix, gffn, mixout_w, T // MOE_TM, nt_all)

    ms = DB * tp
    xs_pad = jnp.concatenate([x_sample, jnp.zeros((DB, tp - TS, D), F32)], axis=1).reshape(ms, D)
    pos_s = past_len + jnp.tile(jnp.arange(tp, dtype=jnp.int32), DB)
    tabs_s = _rope_tables(pos_s)
    mq_s, mk_s, mv_s, q_s, qr_s, rows_s, win_s, small_s = _inproj(xs_pad, mod_s, gmix, tabs_s, inproj_w, ms, 1, 1)
    hm_s, C_s, n_s, m_s = _mlstm(mq_s, mk_s, mv_s, small_s, DB, tp, TS, tp,
                                 state=(state_mlstm_C[l], state_mlstm_n[l], state_mlstm_m[l]))
    cache2 = jnp.transpose(cache_nsa_kv[l], (0, 2, 3, 4, 1)).reshape(cache_nsa_kv.shape[1], 4 * LANES, PAGE_SIZE)
    winbuf = jnp.transpose(state_win_kv[l], (0, 2, 3, 4, 1)).reshape(DB, 2 * LANES, wbuf)
    o_nsa_s, win_out_s = _nsa_sample(page_table, cache2, q_s, qr_s, small_s, rows_s, win_s, winbuf,
                                     wbd, pe, kg0, TS)
    assert ms_pad == MOE_TM
    rpad = lambda a: jnp.concatenate([a, jnp.zeros((ms_pad - ms, a.shape[1]), a.dtype)], axis=0) if ms_pad > ms else a
    mod_sp = rpad(mod_s[0])[None]
    x1_s, xsl, info, cnt = _mixout(rpad(xs_pad), rpad(hm_s), rpad(o_nsa_s), mod_sp, gmix, gffn, mixout_w,
                                   1, nt_all, tile0=nt_p, shared=(xsl, info, cnt),
                                   t_mod=tp, t_valid=TS, m_valid=ms)

    ysl = _moe_experts(_moe_plan(cnt[:, :, 0]), xsl, w_gu[l], b_gu[l], w_dn[l], b_dn[l])
    y_p = _combine(ysl, info, x1_p, mod_p, T // MOE_TM).reshape(B, T, D)
    y_s_all = _combine(ysl, info, x1_s, mod_sp, 1, tile0=nt_p)
    valid = lambda a: a.reshape(DB, tp, -1)[:, :TS].reshape(DB * TS, -1)
    y_s = valid(y_s_all[:ms]).reshape(DB, TS, D)

    kv_p = jnp.transpose(rows_t.reshape(B, 4, A_KV, A_DH, T), (0, 4, 1, 2, 3))[None]
    kv_s = valid(rows_s).reshape(1, DB, TS, 4, A_KV, A_DH)
    wp = min(WINDOW, T)
    win_p = jnp.transpose(win_t[:, :, T - wp:].reshape(B, 2, A_KV, A_DH, wp), (0, 4, 1, 2, 3))[None]
    win_s_out = jnp.transpose(win_out_s.reshape(DB, 2, A_KV, A_DH, wbuf), (0, 4, 1, 2, 3))[None]
    return (y_p, y_s, kv_p, kv_s, win_p, win_s_out,
            C_p[None], n_p[None], m_p[None], C_s[None], n_s[None], m_s[None])
```

```python
import functools

import jax
import jax.numpy as jnp
from jax import lax
from jax.experimental import pallas as pl
from jax.experimental.pallas import tpu as pltpu

F32 = jnp.float32
BF16 = jnp.bfloat16

D_MODEL = 1024
M_HEADS = 4
M_DH = 128
M_WIDTH = M_HEADS * M_DH
A_HEADS = 8
A_KV = 2
A_HPG = A_HEADS // A_KV
A_DH = 64
A_WIDTH = A_HEADS * A_DH
CMP_STRIDE = 16
CMP_LEN = 32
SEL_LEN = 64
N_SEL = 16
WINDOW = 512
PAGE_SIZE = 128
ROPE_THETA = 500000.0
ROT_DIM = A_DH // 4
ATT_SCALE = A_DH ** -0.5
N_EXPERTS = 32
TOP_K = 4
D_EXPERT = D_MODEL
SWIGLU_LIMIT = 7.0
SWIGLU_ALPHA = 1.702
EPS = 1e-6

OFF_MQ, OFF_MK, OFF_MV, OFF_MO = 0, M_WIDTH, 2 * M_WIDTH, 3 * M_WIDTH
OFF_MI = 4 * M_WIDTH
OFF_MF = OFF_MI + M_HEADS
OFF_AQ = OFF_MF + M_HEADS
OFF_AKV = OFF_AQ + A_WIDTH
OFF_AG = OFF_AKV + 6 * A_KV * A_DH
OFF_GA = OFF_AG + 3 * A_HEADS
OFF_GB = OFF_GA + D_MODEL
N_IN = OFF_GB + D_MODEL

LANES = 128
SUBLANES = 8
VMEM_LIMIT = 56 * 1024 * 1024

NEG_BIG = -1e30
M_INIT = -1e29
LOG2E = 1.4426950408889634
SAMPLE_PAD_T = 8


def _cparams(sem):
    return pltpu.CompilerParams(dimension_semantics=sem, vmem_limit_bytes=VMEM_LIMIT)


def _bdot(a, b):
    return jnp.dot(a.astype(BF16), b.astype(BF16), preferred_element_type=F32)


def _bdot_t(a, b):
    return lax.dot_general(a.astype(BF16), b.astype(BF16), (((1,), (1,)), ((), ())),
                           preferred_element_type=F32)


def _split(a):
    hi = a.astype(BF16)
    lo = (a - hi.astype(F32)).astype(BF16)
    return hi, lo


def _dot3(a, b):
    ah, al = _split(a)
    bh, bl = _split(b)
    return (jnp.dot(ah, bh, preferred_element_type=F32) + jnp.dot(al, bh, preferred_element_type=F32)
            + jnp.dot(ah, bl, preferred_element_type=F32))


def _dot2_exact_rhs(a, b_bf16):
    ah, al = _split(a)
    return jnp.dot(ah, b_bf16, preferred_element_type=F32) + jnp.dot(al, b_bf16, preferred_element_type=F32)


def _sigmoid(x):
    return 0.5 * jnp.tanh(0.5 * x) + 0.5


def _rmsnorm_rows(x, g):
    return x * lax.rsqrt(jnp.mean(x * x, axis=-1, keepdims=True) + EPS) * g


def _adaln_kernel(c_ref, w_ref, b_ref, o_ref):
    c = c_ref[...]
    s = c * _sigmoid(c)
    o_ref[...] = _dot3(s, w_ref[...]) + b_ref[...]


def _adaln(c, w, b):
    mc, d = c.shape
    n = w.shape[1]
    tn = 1024
    return pl.pallas_call(
        _adaln_kernel,
        grid=(n // tn,),
        in_specs=[pl.BlockSpec((mc, d), lambda j: (0, 0)),
                  pl.BlockSpec((d, tn), lambda j: (0, j)),
                  pl.BlockSpec((1, tn), lambda j: (0, j))],
        out_specs=pl.BlockSpec((mc, tn), lambda j: (0, j)),
        out_shape=jax.ShapeDtypeStruct((mc, n), F32),
        compiler_params=_cparams(("parallel",)),
        name="adaln",
    )(c, w, b.reshape(1, n))


def _head_norm(z, bd, gain):
    ms = _dot2_exact_rhs(z * z, bd)
    return z * lax.rsqrt(ms + EPS) * gain


def _rope(z, cos, s_prev, s_next):
    w = z.shape[1]
    rep = w // LANES
    if rep > 1:
        cos = jnp.concatenate([cos] * rep, axis=1)
        s_prev = jnp.concatenate([s_prev] * rep, axis=1)
        s_next = jnp.concatenate([s_next] * rep, axis=1)
    z_prev = pltpu.roll(z, ROT_DIM // 2, 1)
    z_next = pltpu.roll(z, w - ROT_DIM // 2, 1)
    return z * cos + z_prev * s_prev + z_next * s_next


def _inproj_kernel(x_ref, mod_ref, gmix_ref, cos_ref, sp_ref, sn_ref,
                   wm_ref, bm_ref, wq_ref, bq_ref, wkv_ref, bkv_ref, ws_ref, bs_ref,
                   qg_ref, kg_ref, bd_ref,
                   mq_ref, mk_ref, mv_ref, q_ref, qr_ref, rows_ref, win_ref, small_ref,
                   rows_t_ref=None, win_t_ref=None):
    x = x_ref[...]
    sh1 = mod_ref[:, 0:D_MODEL]
    sc1 = mod_ref[:, D_MODEL:2 * D_MODEL]
    h = _rmsnorm_rows(x, gmix_ref[...]) * (1.0 + sc1) + sh1
    hb = h.astype(BF16)

    mq_ref[...] = jnp.dot(hb, wm_ref[:, 0:M_WIDTH], preferred_element_type=F32) + bm_ref[:, 0:M_WIDTH]
    mk = jnp.dot(hb, wm_ref[:, M_WIDTH:2 * M_WIDTH], preferred_element_type=F32) + bm_ref[:, M_WIDTH:2 * M_WIDTH]
    mk_ref[...] = mk * (M_DH ** -0.5)
    mv_ref[...] = (jnp.dot(hb, wm_ref[:, 2 * M_WIDTH:3 * M_WIDTH], preferred_element_type=F32)
                   + bm_ref[:, 2 * M_WIDTH:3 * M_WIDTH])

    cos, sp, sn = cos_ref[...], sp_ref[...], sn_ref[...]
    zq = jnp.dot(hb, wq_ref[...], preferred_element_type=F32) + bq_ref[...]
    qn = _head_norm(zq, bd_ref[...], qg_ref[...])
    q_ref[...] = qn
    qr_ref[...] = _rope(qn, cos, sp, sn)

    zkv = jnp.dot(hb, wkv_ref[...], preferred_element_type=F32) + bkv_ref[...]
    bd2 = bd_ref[0:LANES, 0:LANES]
    ksel = _head_norm(zkv[:, 2 * LANES:3 * LANES], bd2, kg_ref[0:1, :])
    rows = jnp.concatenate([zkv[:, 0:2 * LANES], _rope(ksel, cos, sp, sn), zkv[:, 3 * LANES:4 * LANES]], axis=1)
    rows_ref[...] = rows
    if rows_t_ref is not None:
        rows_t_ref[...] = jnp.transpose(rows)
    kwin = _head_norm(zkv[:, 4 * LANES:5 * LANES], bd2, kg_ref[1:2, :])
    win = jnp.concatenate([_rope(kwin, cos, sp, sn), zkv[:, 5 * LANES:6 * LANES]], axis=1)
    win_ref[...] = win
    if win_t_ref is not None:
        win_t_ref[...] = jnp.transpose(win)

    small_ref[...] = _dot3(h, ws_ref[...]) + bs_ref[...]


def _inproj(x2, mod3, gmix, tabs, wts, tm, tiles_per_mod, pos_tiles, rows_t_batches=None):
    m = x2.shape[0]
    cos_t, sp_t, sn_t = tabs
    (wm, bm, wq, bq, wkv, bkv, ws, bs, qg, kg, bd) = wts
    r = mod3.shape[1]
    row = lambda i: (i, 0)
    const = lambda i: (0, 0)
    tab = lambda i: (i % pos_tiles, 0)
    in_specs = [
        pl.BlockSpec((tm, D_MODEL), row),
        pl.BlockSpec((None, r, 6 * D_MODEL), lambda i: (i // tiles_per_mod, 0, 0)),
        pl.BlockSpec((1, D_MODEL), const),
        pl.BlockSpec((tm, LANES), tab), pl.BlockSpec((tm, LANES), tab), pl.BlockSpec((tm, LANES), tab),
        pl.BlockSpec(wm.shape, const), pl.BlockSpec(bm.shape, const),
        pl.BlockSpec(wq.shape, const), pl.BlockSpec(bq.shape, const),
        pl.BlockSpec(wkv.shape, const), pl.BlockSpec(bkv.shape, const),
        pl.BlockSpec(ws.shape, const), pl.BlockSpec(bs.shape, const),
        pl.BlockSpec(qg.shape, const), pl.BlockSpec(kg.shape, const), pl.BlockSpec(bd.shape, const),
    ]
    widths = (M_WIDTH, M_WIDTH, M_WIDTH, A_WIDTH, A_WIDTH, 4 * LANES, 2 * LANES, LANES)
    out_specs = [pl.BlockSpec((tm, w), row) for w in widths]
    out_shape = [jax.ShapeDtypeStruct((m, w), F32) for w in widths]
    if rows_t_batches is not None:
        for w in (4 * LANES, 2 * LANES):
            out_specs.append(pl.BlockSpec((None, w, tm), lambda i: (i // tiles_per_mod, 0, i % tiles_per_mod)))
            out_shape.append(jax.ShapeDtypeStruct((rows_t_batches, w, m // rows_t_batches), F32))
    return pl.pallas_call(
        _inproj_kernel,
        grid=(m // tm,),
        in_specs=in_specs,
        out_specs=out_specs,
        out_shape=out_shape,
        compiler_params=_cparams(("parallel",)),
        name="inproj",
    )(x2, mod3, gmix, cos_t, sp_t, sn_t, wm, bm, wq, bq, wkv, bkv, ws, bs, qg, kg, bd)


def _log_sigmoid(x):
    return jnp.minimum(x, 0.0) - jnp.log(1.0 + jnp.exp(-jnp.abs(x)))


def _mlstm_kernel(*refs, L, t_valid, has_state):
    if has_state:
        q_ref, k_ref, v_ref, s_ref, c0_ref, n0_ref, m0_ref, h_ref, c_ref, n_ref, m_ref = refs
    else:
        q_ref, k_ref, v_ref, s_ref, h_ref, c_ref, n_ref, m_ref = refs
    c = pl.program_id(1)

    @pl.when(c == 0)
    def _():
        if has_state:
            c_ref[...] = c0_ref[...]
            n_ref[...] = n0_ref[...]
            m_ref[...] = m0_ref[...]
        else:
            c_ref[...] = jnp.zeros(c_ref.shape, F32)
            n_ref[...] = jnp.zeros(n_ref.shape, F32)
            m_ref[...] = jnp.zeros(m_ref.shape, F32)

    row = lax.broadcasted_iota(jnp.int32, (L, L), 0)
    col = lax.broadcasted_iota(jnp.int32, (L, L), 1)
    causal = col <= row
    eye = col == row
    tok_col = c * L + lax.broadcasted_iota(jnp.int32, (L, 1), 0)
    valid_col = tok_col < t_valid
    for hd in range(M_HEADS):
        lo, hi = hd * M_DH, (hd + 1) * M_DH
        q = q_ref[:, lo:hi]
        k = k_ref[:, lo:hi]
        v = v_ref[:, lo:hi]
        i_col = s_ref[:, hd:hd + 1]
        lf_col = _log_sigmoid(s_ref[:, M_HEADS + hd:M_HEADS + hd + 1])
        lf_col = jnp.where(valid_col, lf_col, 0.0)
        i_col = jnp.where(valid_col, i_col, -jnp.inf)
        if L == LANES:
            i_col = jnp.broadcast_to(i_col, (L, L))
            lf_c = jnp.broadcast_to(lf_col, (L, L))
            p0 = lf_c.astype(BF16)
            r1 = lf_c - p0.astype(F32)
            p1 = r1.astype(BF16)
            p2 = (r1 - p1.astype(F32)).astype(BF16)
            tril = jnp.where(causal, 1.0, 0.0).astype(BF16)
            b_col = (jnp.dot(tril, p0, preferred_element_type=F32) + jnp.dot(tril, p1, preferred_element_type=F32)
                     + jnp.dot(tril, p2, preferred_element_type=F32))
            i_row = jnp.transpose(i_col)[0:1, :]
            b_row = jnp.transpose(b_col)[0:1, :]
        else:
            i_row = jnp.sum(jnp.where(eye, i_col, 0.0), axis=0, keepdims=True)
            lf_row = jnp.sum(jnp.where(eye, lf_col, 0.0), axis=0, keepdims=True)
            b_col = jnp.sum(jnp.where(causal, lf_row, 0.0), axis=1, keepdims=True)
            b_row = jnp.sum(jnp.where(row <= col, lf_col, 0.0), axis=0, keepdims=True)
        m_prev = m_ref[:, hd:hd + 1]
        dmat = jnp.where(causal, b_col - b_row + i_row, -jnp.inf)
        inter = b_col + m_prev
        m_row = jnp.maximum(jnp.max(dmat, axis=1, keepdims=True), inter)
        w = jnp.exp(dmat - m_row)
        w_inter = jnp.exp(inter - m_row)
        s = _bdot_t(q, k) * w
        cm = c_ref[hd]
        nv = n_ref[hd]
        num = _bdot(s, v) + w_inter * _bdot_t(q, cm)
        den = jnp.sum(s, axis=1, keepdims=True) + w_inter * jnp.sum(q * nv, axis=1, keepdims=True)
        h_ref[:, lo:hi] = num / jnp.maximum(jnp.abs(den), jnp.exp(-m_row))
        b_last = b_col[L - 1:L, 0:1]
        dec_col = b_last - b_col + i_col
        dec_row = b_last - b_row + i_row
        m_new = jnp.maximum(b_last + m_prev, jnp.max(dec_row, axis=1, keepdims=True))
        ws_col = jnp.exp(dec_col - m_new)
        wc = jnp.exp(b_last + m_prev - m_new)
        vw = (v * ws_col).astype(BF16)
        upd = lax.dot_general(vw, k.astype(BF16), (((0,), (0,)), ((), ())), preferred_element_type=F32)
        c_ref[hd] = wc * cm + upd
        n_ref[hd] = wc * nv + jnp.sum(k * ws_col, axis=0, keepdims=True)
        m_ref[:, hd:hd + 1] = m_new


def _mlstm(mq, mk, mv, small, nb, t_pad, t_valid, L, state=None):
    nc = t_pad // L
    has_state = state is not None
    blk = lambda b, c: (b * nc + c, 0)
    st4 = lambda b, c: (b, 0, 0, 0)
    st3 = lambda b, c: (b, 0, 0)
    in_specs = [pl.BlockSpec((L, M_WIDTH), blk)] * 3 + [pl.BlockSpec((L, LANES), blk)]
    args = [mq, mk, mv, small]
    if has_state:
        c0, n0, m0 = state
        in_specs += [pl.BlockSpec((None, M_HEADS, M_DH, M_DH), st4),
                     pl.BlockSpec((None, M_HEADS, 1, M_DH), st4),
                     pl.BlockSpec((None, 1, M_HEADS), st3)]
        args += [c0, n0.reshape(nb, M_HEADS, 1, M_DH), m0.reshape(nb, 1, M_HEADS)]
    out_specs = [pl.BlockSpec((L, M_WIDTH), blk),
                 pl.BlockSpec((None, M_HEADS, M_DH, M_DH), st4),
                 pl.BlockSpec((None, M_HEADS, 1, M_DH), st4),
                 pl.BlockSpec((None, 1, M_HEADS), st3)]
    out_shape = [jax.ShapeDtypeStruct((nb * t_pad, M_WIDTH), F32),
                 jax.ShapeDtypeStruct((nb, M_HEADS, M_DH, M_DH), F32),
                 jax.ShapeDtypeStruct((nb, M_HEADS, 1, M_DH), F32),
                 jax.ShapeDtypeStruct((nb, 1, M_HEADS), F32)]
    h, cs, ns, ms = pl.pallas_call(
        functools.partial(_mlstm_kernel, L=L, t_valid=t_valid, has_state=has_state),
        grid=(nb, nc),
        in_specs=in_specs,
        out_specs=out_specs,
        out_shape=out_shape,
        compiler_params=_cparams(("parallel", "arbitrary")),
        name="mlstm",
    )(*args)
    return h, cs, ns.reshape(nb, M_HEADS, M_DH), ms.reshape(nb, M_HEADS)


def _stack_heads(qt, g):
    t = qt.shape[0]
    z = jnp.zeros((t, A_DH), F32)
    parts = []
    for hh in range(A_HPG):
        hd = g * A_HPG + hh
        qh = qt[:, hd * A_DH:(hd + 1) * A_DH] * (ATT_SCALE * LOG2E)
        parts.append(jnp.concatenate([qh, z], axis=1) if g == 0 else jnp.concatenate([z, qh], axis=1))
    return jnp.concatenate(parts, axis=0).astype(BF16)


def _gate_cols(small, g, br):
    cols = []
    for hh in range(A_HPG):
        c0 = 2 * M_HEADS + (g * A_HPG + hh) * 3 + br
        cols.append(_sigmoid(small[:, c0:c0 + 1]))
    return jnp.concatenate(cols, axis=0)


def _compress(k_ref, v_ref, nseg, wbd_ref, pe_ref, kg0):
    acc_lo = jnp.zeros((nseg, 2 * LANES), F32)
    acc_hi = jnp.zeros((nseg, 2 * LANES), F32)
    for l in range(CMP_STRIDE):
        xl = jnp.concatenate([k_ref[pl.ds(l, nseg, stride=CMP_STRIDE), :],
                              v_ref[pl.ds(l, nseg, stride=CMP_STRIDE), :]], axis=1)
        acc_lo = acc_lo + _bdot(xl + pe_ref[l], wbd_ref[l])
        acc_hi = acc_hi + _bdot(xl + pe_ref[CMP_STRIDE + l], wbd_ref[CMP_STRIDE + l])
    return _compress_finish(acc_lo, acc_hi, nseg, kg0)


def _compress_grouped(x_ref, nseg, wbd_ref, pe_ref, kg0):
    acc_lo = jnp.zeros((nseg, 2 * LANES), F32)
    acc_hi = jnp.zeros((nseg, 2 * LANES), F32)
    pe_lo = jnp.zeros((SUBLANES, 2 * LANES), F32)
    pe_hi = jnp.zeros((SUBLANES, 2 * LANES), F32)
    for l in range(CMP_STRIDE):
        xl = x_ref[l].astype(BF16)
        acc_lo = acc_lo + jnp.dot(xl, wbd_ref[l], preferred_element_type=F32)
        acc_hi = acc_hi + jnp.dot(xl, wbd_ref[CMP_STRIDE + l], preferred_element_type=F32)
        pe_lo = pe_lo + _bdot(jnp.broadcast_to(pe_ref[l], (SUBLANES, 2 * LANES)), wbd_ref[l])
        pe_hi = pe_hi + _bdot(jnp.broadcast_to(pe_ref[CMP_STRIDE + l], (SUBLANES, 2 * LANES)),
                              wbd_ref[CMP_STRIDE + l])
    return _compress_finish(acc_lo + pe_lo[0:1, :], acc_hi + pe_hi[0:1, :], nseg, kg0)


def _compress_finish(acc_lo, acc_hi, nseg, kg0):
    kv = acc_lo + pltpu.roll(acc_hi, nseg - 1, 0)
    kc = kv[:, 0:LANES]
    vc = kv[:, LANES:2 * LANES]
    lane = lax.broadcasted_iota(jnp.int32, (nseg, LANES), 1)
    sq = kc * kc
    ms0 = jnp.sum(jnp.where(lane < A_DH, sq, 0.0), axis=1, keepdims=True) * (1.0 / A_DH)
    ms1 = jnp.sum(jnp.where(lane >= A_DH, sq, 0.0), axis=1, keepdims=True) * (1.0 / A_DH)
    ms = jnp.where(lane < A_DH, ms0, ms1)
    kc = kc * lax.rsqrt(ms + EPS) * kg0
    return kc, vc


def _cmp_branch(qn_g, kc_b, vc_b, tpos_rows, nseg, n_tok):
    s = _bdot_t(qn_g, kc_b)
    nidx = lax.broadcasted_iota(jnp.int32, (1, nseg), 1)
    vis = (nidx * CMP_STRIDE + (CMP_LEN - 1)) <= tpos_rows
    sm = jnp.where(vis, s, NEG_BIG)
    mx = jnp.max(sm, axis=1, keepdims=True)
    e = jnp.where(vis, jnp.exp2(sm - mx), 0.0)
    d = jnp.sum(e, axis=1, keepdims=True)
    p = e / jnp.where(d > 0, d, 1.0)
    o = _bdot(p, vc_b)
    imp = p[0:n_tok]
    for hh in range(1, A_HPG):
        imp = imp + p[hh * n_tok:(hh + 1) * n_tok]
    return o, imp


def _masked_attn_direct(q_g, k_parts, v_parts, allowed_parts, feature_major):
    ss = [jnp.where(al, _bdot(q_g, kk) if fm else _bdot_t(q_g, kk), NEG_BIG)
          for kk, al, fm in zip(k_parts, allowed_parts, feature_major)]
    mx = ss[0].max(axis=1, keepdims=True)
    for s in ss[1:]:
        mx = jnp.maximum(mx, s.max(axis=1, keepdims=True))
    num = None
    den = None
    for s, al, vv, fm in zip(ss, allowed_parts, v_parts, feature_major):
        e = jnp.where(al, jnp.exp2(s - mx), 0.0)
        dd = jnp.sum(e, axis=1, keepdims=True)
        oo = _bdot_t(e, vv) if fm else _bdot(e, vv)
        num = oo if num is None else num + oo
        den = dd if den is None else den + dd
    return num / jnp.where(den > 0, den, 1.0)


def _assemble_heads(o_groups, n_tok):
    pieces = []
    for g in range(A_KV):
        for hh in range(A_HPG):
            pieces.append(o_groups[g][hh * n_tok:(hh + 1) * n_tok, g * A_DH:(g + 1) * A_DH])
    return jnp.concatenate(pieces, axis=1)


def _lane_rep(a, rep):
    return a if rep == 1 else jnp.concatenate([a] * rep, axis=1)


def _nsa_prompt_kernel(q_ref, qr_ref, small_ref, rows_ref, win_ref, wbd_ref, pe_ref, kg0_ref,
                       pool_ref, o_ref,
                       kraw_sc, vraw_sc, kc_sc, vct_sc, sel_sc, m_sc, acc_sc, s_sc, *, T, tq, kc_len):
    qi = pl.program_id(1)
    nseg = T // CMP_STRIDE
    nsb = T // SEL_LEN
    bpc = kc_len // SEL_LEN

    @pl.when(qi == 0)
    def _():
        kraw_sc[...] = rows_ref[:, 0:LANES]
        vraw_sc[...] = rows_ref[:, LANES:2 * LANES]
        kc, vc = _compress(kraw_sc, vraw_sc, nseg, wbd_ref, pe_ref, kg0_ref[...])
        kc_sc[...] = kc
        vct_sc[...] = jnp.transpose(vc)

    t0 = qi * tq
    tpos = t0 + lax.broadcasted_iota(jnp.int32, (1, tq), 1)
    tpos4 = _lane_rep(tpos, A_HPG)
    q = q_ref[...]
    qr = qr_ref[...]
    small_t = jnp.transpose(small_ref[...])
    kc_b = kc_sc[...].astype(BF16)
    vct_b = vct_sc[...].astype(BF16)
    bidx = lax.broadcasted_iota(jnp.int32, (nsb, tq), 0)
    cur = tpos // SEL_LEN
    vis = (lax.broadcasted_iota(jnp.int32, (nseg, 1), 0) * CMP_STRIDE + (CMP_LEN - 1)) <= tpos4
    qr_gs = [_stack_heads(qr, g) for g in range(A_KV)]
    o_cmps = []
    for g in range(A_KV):
        sm = jnp.where(vis, _bdot_t(kc_b, _stack_heads(q, g)), NEG_BIG)
        mx = jnp.max(sm, axis=0, keepdims=True)
        e = jnp.where(vis, jnp.exp2(sm - mx), 0.0)
        d = jnp.sum(e, axis=0, keepdims=True)
        p = e / jnp.where(d > 0, d, 1.0)
        o_cmps.append(jnp.dot(vct_b, p.astype(BF16), preferred_element_type=F32))
        imp = p[:, 0:tq]
        for hh in range(1, A_HPG):
            imp = imp + p[:, hh * tq:(hh + 1) * tq]
        ih, il = _split(imp)
        imp_t = (jnp.dot(pool_ref[...], ih, preferred_element_type=F32)
                 + jnp.dot(pool_ref[...], il, preferred_element_type=F32))[0:nsb]
        val = jnp.where(bidx < cur, imp_t, -1.0)
        rank = jnp.zeros((nsb, tq), F32)
        for bp in range(nsb):
            vb = val[bp:bp + 1, :]
            rank = rank + jnp.where(vb > val, 1.0, jnp.where((vb == val) & (bidx > bp), 1.0, 0.0))
        sel_sc[g] = jnp.where(((rank < (N_SEL - 1)) & (bidx < cur)) | (bidx == cur), 1.0, 0.0)

    m_sc[...] = jnp.full(m_sc.shape, M_INIT, F32)
    acc_sc[...] = jnp.zeros(acc_sc.shape, F32)

    def with_ones_row(vt_, g):
        vb = vt_.astype(BF16)
        r0, pad = (1 - g) * A_DH, 2 * SUBLANES
        ones = jnp.ones((pad, vb.shape[1]), BF16)
        return jnp.concatenate(([vb[0:r0]] if r0 else []) + [ones, vb[r0 + pad:]], axis=0)

    def sel_body(c, carry):
        k0 = pl.multiple_of(c * kc_len, kc_len)
        kb = rows_ref[pl.ds(k0, kc_len), 2 * LANES:3 * LANES].astype(BF16)
        vt = jnp.transpose(rows_ref[pl.ds(k0, kc_len), 3 * LANES:4 * LANES])
        causal = (k0 + lax.broadcasted_iota(jnp.int32, (kc_len, 1), 0)) <= tpos
        for g in range(A_KV):
            s_sc[g, 0:kc_len, :] = _bdot_t(kb, qr_gs[g])
        for g in range(A_KV):
            selc = sel_sc[g, pl.ds(pl.multiple_of(c * bpc, bpc), bpc), :]
            selx = jnp.concatenate([jnp.broadcast_to(selc[j:j + 1, :], (SEL_LEN, tq)) for j in range(bpc)], axis=0)
            bias = jnp.where(causal & (selx > 0.5), 0.0, NEG_BIG)
            sm = s_sc[g, 0:kc_len, :] + _lane_rep(bias, A_HPG)
            m_prev = m_sc[g]
            m_new = jnp.maximum(m_prev, jnp.max(sm, axis=0, keepdims=True))
            alpha = jnp.exp2(m_prev - m_new)
            p = jnp.exp2(sm - m_new)
            acc_sc[g] = alpha * acc_sc[g] + jnp.dot(with_ones_row(vt, g), p.astype(BF16),
                                                    preferred_element_type=F32)
            m_sc[g] = m_new
        return carry

    lax.fori_loop(0, (t0 + tq + kc_len - 1) // kc_len, sel_body, 0)

    wk = min(WINDOW + tq, T)
    w0 = pl.multiple_of(jnp.clip(t0 + tq - wk, 0, T - wk), tq)
    kw = win_ref[pl.ds(w0, wk), 0:LANES].astype(BF16)
    vwt = jnp.transpose(win_ref[pl.ds(w0, wk), LANES:2 * LANES])
    wdiff = tpos - (w0 + lax.broadcasted_iota(jnp.int32, (wk, 1), 0))
    wbias = _lane_rep(jnp.where((wdiff >= 0) & (wdiff < WINDOW), 0.0, NEG_BIG), A_HPG)

    def gate_row(g, br):
        cols = [2 * M_HEADS + (g * A_HPG + hh) * 3 + br for hh in range(A_HPG)]
        return jnp.concatenate([_sigmoid(small_t[c0:c0 + 1, :]) for c0 in cols], axis=1)

    for g in range(A_KV):
        s_sc[g, 0:wk, :] = _bdot_t(kw, qr_gs[g])
    o_ts = []
    for g in range(A_KV):
        den = (1 - g) * A_DH
        acc = acc_sc[g]
        l = acc[den:den + 1, :]
        o_sel = acc / jnp.where(l > 0, l, 1.0)
        sw = s_sc[g, 0:wk, :] + wbias
        pw = jnp.exp2(sw - jnp.max(sw, axis=0, keepdims=True))
        ow = jnp.dot(with_ones_row(vwt, g), pw.astype(BF16), preferred_element_type=F32)
        o_win = ow / ow[den:den + 1, :]
        o_ts.append(gate_row(g, 0) * o_cmps[g] + gate_row(g, 1) * o_sel + gate_row(g, 2) * o_win)
    for j in range(A_HEADS // 2):
        g, h0 = j // (A_HPG // 2), 2 * (j % (A_HPG // 2))
        og = o_ts[g][g * A_DH:(g + 1) * A_DH, :]
        pair = jnp.concatenate([og[:, h0 * tq:(h0 + 1) * tq], og[:, (h0 + 1) * tq:(h0 + 2) * tq]], axis=0)
        o_ref[:, j * LANES:(j + 1) * LANES] = jnp.transpose(pair)


def _nsa_prompt(q, qr, small, rows, win, wbd, pe, kg0, nb, T):
    tq = 128
    kc_len = _pick_tile(T, 512)
    nq = T // tq
    nseg = T // CMP_STRIDE
    nsb = T // SEL_LEN
    nsb_p = -(-nsb // SUBLANES) * SUBLANES
    pool = (jnp.arange(nsb_p)[:, None] == jnp.arange(nseg)[None, :] // (SEL_LEN // CMP_STRIDE)).astype(BF16)
    tile = lambda b, i: (b * nq + i, 0)
    per_b = lambda b, i: (b, 0)
    c2 = lambda b, i: (0, 0)
    c3 = lambda b, i: (0, 0, 0)
    c4 = A_HPG * tq
    return pl.pallas_call(
        functools.partial(_nsa_prompt_kernel, T=T, tq=tq, kc_len=kc_len),
        grid=(nb, nq),
        in_specs=[pl.BlockSpec((tq, A_WIDTH), tile), pl.BlockSpec((tq, A_WIDTH), tile),
                  pl.BlockSpec((tq, LANES), tile),
                  pl.BlockSpec((T, 4 * LANES), per_b), pl.BlockSpec((T, 2 * LANES), per_b),
                  pl.BlockSpec(wbd.shape, c3), pl.BlockSpec(pe.shape, c3), pl.BlockSpec(kg0.shape, c2),
                  pl.BlockSpec(pool.shape, c2)],
        out_specs=pl.BlockSpec((tq, A_WIDTH), tile),
        out_shape=jax.ShapeDtypeStruct((nb * T, A_WIDTH), F32),
        scratch_shapes=[pltpu.VMEM((T, LANES), F32), pltpu.VMEM((T, LANES), F32),
                        pltpu.VMEM((nseg, LANES), F32), pltpu.VMEM((LANES, nseg), F32),
                        pltpu.VMEM((A_KV, nsb, tq), F32),
                        pltpu.VMEM((A_KV, 1, c4), F32),
                        pltpu.VMEM((A_KV, LANES, c4), F32),
                        pltpu.VMEM((A_KV, max(kc_len, min(WINDOW + tq, T)), c4), F32)],
        compiler_params=_cparams(("parallel", "arbitrary")),
        name="nsa_prompt",
    )(q, qr, small, rows, win, wbd, pe, kg0, pool)


def _nsa_sample_kernel(pt_ref, cache_ref, q_ref, qr_ref, small_ref, rows_ref, winnew_ref, winbuf_ref,
                       wbd_ref, pe_ref, kg0_ref, pool_ref, expand_ref,
                       o_ref, winout_ref,
                       cmp_buf, sel_buf, xperm_sc, sems, *, n_pages, past_len, t_valid):
    b = pl.program_id(0)
    nb = pl.num_programs(0)
    tp = SAMPLE_PAD_T
    nseg = past_len // CMP_STRIDE
    nsb = past_len // SEL_LEN
    wbuf = winbuf_ref.shape[1]

    def page_copies(bb, p, phase):
        page = pt_ref[bb * n_pages + p]
        dst_lanes = pl.ds(pl.multiple_of(p * PAGE_SIZE, PAGE_SIZE), PAGE_SIZE)
        if phase == 0:
            return [pltpu.make_async_copy(cache_ref.at[page, pl.ds(0, 2 * LANES), :],
                                          cmp_buf.at[:, dst_lanes], sems.at[0])]
        return [pltpu.make_async_copy(cache_ref.at[page, pl.ds(2 * LANES, 2 * LANES), :],
                                      sel_buf.at[:, dst_lanes], sems.at[1])]

    def start_all(bb, phase):
        def body(p, c):
            for cp in page_copies(bb, p, phase):
                cp.start()
            return c
        lax.fori_loop(0, n_pages, body, 0)

    def wait_all(bb, phase):
        def body(p, c):
            for cp in page_copies(bb, p, phase):
                cp.wait()
            return c
        lax.fori_loop(0, n_pages, body, 0)

    @pl.when(b == 0)
    def _():
        start_all(b, 0)

    start_all(b, 1)
    wait_all(b, 0)

    seg_pp = PAGE_SIZE // CMP_STRIDE
    pr = lax.broadcasted_iota(jnp.int32, (PAGE_SIZE, PAGE_SIZE), 0)
    pc = lax.broadcasted_iota(jnp.int32, (PAGE_SIZE, PAGE_SIZE), 1)
    perm = jnp.where(pc == CMP_STRIDE * (pr % seg_pp) + pr // seg_pp, 1.0, 0.0).astype(BF16)
    for p in range(n_pages):
        xp = _bdot_t(perm, cmp_buf[:, p * PAGE_SIZE:(p + 1) * PAGE_SIZE])
        for l in range(CMP_STRIDE):
            xperm_sc[l, p * seg_pp:(p + 1) * seg_pp, :] = xp[l * seg_pp:(l + 1) * seg_pp, :]
    kc, vc = _compress_grouped(xperm_sc, nseg, wbd_ref, pe_ref, kg0_ref[...])
    kc_b = kc.astype(BF16)
    vc_b = vc.astype(BF16)
    q = q_ref[...]
    qr = qr_ref[...]
    small = small_ref[...]
    tpos_col = past_len + lax.broadcasted_iota(jnp.int32, (tp, 1), 0)
    tpos_rows = jnp.concatenate([tpos_col] * A_HPG, axis=0)
    bp_idx = lax.broadcasted_iota(jnp.int32, (nsb, nsb), 0)
    b_idx = lax.broadcasted_iota(jnp.int32, (nsb, nsb), 1)
    o_cmps = []
    sels = []
    for g in range(A_KV):
        qn_g = _stack_heads(q, g)
        o_cmp, imp = _cmp_branch(qn_g, kc_b, vc_b, tpos_rows, nseg, tp)
        o_cmps.append(o_cmp)
        imp_sel = _dot2_exact_rhs(imp, pool_ref[...])
        imp_pad = jnp.concatenate([imp_sel, jnp.zeros((nsb - tp, nsb), F32)], axis=0)
        imp_t = jnp.transpose(imp_pad)
        rows_sel = []
        for t in range(tp):
            if t < t_valid:
                row_t = imp_sel[t:t + 1, :]
                col_t = imp_t[:, t:t + 1]
                ahead = jnp.where(col_t > row_t, 1.0, jnp.where((col_t == row_t) & (bp_idx < b_idx), 1.0, 0.0))
                rank = jnp.sum(ahead, axis=0, keepdims=True)
                rows_sel.append(jnp.where(rank < (N_SEL - 1), 1.0, 0.0))
            else:
                rows_sel.append(jnp.zeros((1, nsb), F32))
        sels.append(jnp.concatenate(rows_sel, axis=0))

    @pl.when(b + 1 < nb)
    def _():
        start_all(b + 1, 0)

    wait_all(b, 1)

    new_idx = lax.broadcasted_iota(jnp.int32, (tp, tp), 1)
    tok_idx = lax.broadcasted_iota(jnp.int32, (tp, tp), 0)
    new_ok = jnp.concatenate([jnp.where(new_idx <= tok_idx, 1.0, 0.0)] * A_HEADS, axis=0) > 0.5
    wpos = past_len - wbuf + lax.broadcasted_iota(jnp.int32, (1, wbuf), 1)
    wdiff = tpos_col - wpos
    win_ok = jnp.concatenate([jnp.where((wdiff >= 0) & (wdiff < WINDOW), 1.0, 0.0)] * A_HEADS, axis=0) > 0.5
    k_past = sel_buf[0:LANES, :].astype(BF16)
    v_past = sel_buf[LANES:2 * LANES, :].astype(BF16)
    k_new = rows_ref[:, 2 * LANES:3 * LANES]
    v_new = rows_ref[:, 3 * LANES:4 * LANES]
    kw_past = winbuf_ref[0:LANES, :]
    vw_past = winbuf_ref[LANES:2 * LANES, :]
    kw_new = winnew_ref[:, 0:LANES]
    vw_new = winnew_ref[:, LANES:2 * LANES]
    r4 = A_HPG * tp
    qr_all = jnp.concatenate([_stack_heads(qr, g) for g in range(A_KV)], axis=0)
    mk = jnp.dot(jnp.concatenate(sels, axis=0).astype(BF16), expand_ref[...],
                 preferred_element_type=F32)
    past_ok = jnp.concatenate([mk[g * tp:(g + 1) * tp] for g in range(A_KV) for _ in range(A_HPG)], axis=0) > 0.5
    o_sel = _masked_attn_direct(qr_all, [k_past, k_new], [v_past, v_new], [past_ok, new_ok], [True, False])
    o_win = _masked_attn_direct(qr_all, [kw_past, kw_new], [vw_past, vw_new], [win_ok, new_ok], [True, False])
    o_groups = []
    for g in range(A_KV):
        rs = slice(g * r4, (g + 1) * r4)
        o_groups.append(_gate_cols(small, g, 0) * o_cmps[g] + _gate_cols(small, g, 1) * o_sel[rs]
                        + _gate_cols(small, g, 2) * o_win[rs])
    o_ref[...] = _assemble_heads(o_groups, tp)

    rolled = pltpu.roll(winbuf_ref[...], wbuf - t_valid, 1)
    new_t = jnp.transpose(jnp.concatenate([winnew_ref[...], jnp.zeros((LANES - tp, 2 * LANES), F32)], axis=0))
    new_t = pltpu.roll(new_t, LANES - t_valid, 1)
    lane = lax.broadcasted_iota(jnp.int32, (2 * LANES, LANES), 1)
    winout_ref[:, 0:wbuf - LANES] = rolled[:, 0:wbuf - LANES]
    winout_ref[:, wbuf - LANES:wbuf] = jnp.where(lane < LANES - t_valid, rolled[:, wbuf - LANES:wbuf], new_t)


def _nsa_sample(page_table, cache, q, qr, small, rows, winnew, winbuf, wbd, pe, kg0, t_valid):
    nb, n_pages = page_table.shape
    past_len = n_pages * PAGE_SIZE
    nseg = past_len // CMP_STRIDE
    nsb = past_len // SEL_LEN
    tp = SAMPLE_PAD_T
    wbuf = winbuf.shape[2]
    pool = (jnp.arange(nseg)[:, None] // (SEL_LEN // CMP_STRIDE) == jnp.arange(nsb)[None, :]).astype(BF16)
    expand = (jnp.arange(nsb)[:, None] == jnp.arange(past_len)[None, :] // SEL_LEN).astype(BF16)
    tile = lambda b, pt: (b, 0)
    c2 = lambda b, pt: (0, 0)
    c3 = lambda b, pt: (0, 0, 0)
    gs = pltpu.PrefetchScalarGridSpec(
        num_scalar_prefetch=1,
        grid=(nb,),
        in_specs=[pl.BlockSpec(memory_space=pl.ANY),
                  pl.BlockSpec((tp, A_WIDTH), tile), pl.BlockSpec((tp, A_WIDTH), tile),
                  pl.BlockSpec((tp, LANES), tile), pl.BlockSpec((tp, 4 * LANES), tile),
                  pl.BlockSpec((tp, 2 * LANES), tile),
                  pl.BlockSpec((None, 2 * LANES, wbuf), lambda b, pt: (b, 0, 0)),
                  pl.BlockSpec(wbd.shape, c3), pl.BlockSpec(pe.shape, c3), pl.BlockSpec(kg0.shape, c2),
                  pl.BlockSpec(pool.shape, c2), pl.BlockSpec(expand.shape, c2)],
        out_specs=[pl.BlockSpec((tp, A_WIDTH), tile),
                   pl.BlockSpec((None, 2 * LANES, wbuf), lambda b, pt: (b, 0, 0))],
        scratch_shapes=[pltpu.VMEM((2 * LANES, past_len), F32), pltpu.VMEM((2 * LANES, past_len), F32),
                        pltpu.VMEM((CMP_STRIDE, past_len // CMP_STRIDE, 2 * LANES), F32),
                        pltpu.SemaphoreType.DMA((2,))],
    )
    return pl.pallas_call(
        functools.partial(_nsa_sample_kernel, n_pages=n_pages, past_len=past_len, t_valid=t_valid),
        grid_spec=gs,
        out_shape=[jax.ShapeDtypeStruct((nb * tp, A_WIDTH), F32),
                   jax.ShapeDtypeStruct((nb, 2 * LANES, wbuf), F32)],
        compiler_params=_cparams(("arbitrary",)),
        name="nsa_sample",
    )(page_table.reshape(-1), cache, q, qr, small, rows, winnew, winbuf, wbd, pe, kg0, pool, expand)


MOE_TM = 256
SEG_ALIGN = 8
MOE_RL = -(-(MOE_TM * TOP_K + N_EXPERTS * (SEG_ALIGN - 1)) // LANES) * LANES


def _pack_halves(x, bf16_exact=False):
    w = x.shape[1] // 2
    bits = lax.bitcast_convert_type(x if bf16_exact else x.astype(BF16).astype(F32), jnp.uint32)
    return bits[:, :w] | (bits[:, w:] >> 16)


def _unpack_halves(u):
    hi = lax.bitcast_convert_type(u & jnp.uint32(0xFFFF0000), F32).astype(BF16)
    lo = lax.bitcast_convert_type(u << 16, F32).astype(BF16)
    return hi, lo


def _route_and_sort(h2, wrt_ref, brt_ref, xsl_ref, info_ref, cnt_ref, tm, t_mod, t_valid, m_valid):
    ne = N_EXPERTS
    h2b = h2.astype(BF16)
    h2l = (h2 - h2b.astype(F32)).astype(BF16)
    wh, wl = _split(wrt_ref[...])
    lt = _bdot_t(wh, h2b) + _bdot_t(wl, h2b) + _bdot_t(wh, h2l) + brt_ref[...]
    eidx = lax.broadcasted_iota(jnp.int32, (ne, tm), 0)
    rank = jnp.zeros((ne, tm), F32)
    for ep in range(ne):
        v = lt[ep:ep + 1, :]
        rank = rank + jnp.where(v > lt, 1.0, jnp.where((v == lt) & (eidx > ep), 1.0, 0.0))
    sel = rank < TOP_K
    if t_mod is not None:
        tok = pl.program_id(0) * tm + lax.broadcasted_iota(jnp.int32, (1, tm), 1)
        sel = sel & ((tok % t_mod) < t_valid) & (tok < m_valid)
    mx = jnp.max(jnp.where(sel, lt, NEG_BIG), axis=0, keepdims=True)
    ex = jnp.where(sel, jnp.exp(lt - mx), 0.0)
    den = jnp.sum(ex, axis=0, keepdims=True)
    gate = ex / jnp.where(den > 0, den, 1.0)
    self_ = jnp.where(sel, 1.0, 0.0)
    selb = self_.astype(BF16)
    er = lax.broadcasted_iota(jnp.int32, (ne, ne), 0)
    ec = lax.broadcasted_iota(jnp.int32, (ne, ne), 1)
    c = jnp.dot(jnp.where(ec <= er, 1.0, 0.0).astype(BF16), selb, preferred_element_type=F32)
    tr = lax.broadcasted_iota(jnp.int32, (tm, tm), 0)
    tc = lax.broadcasted_iota(jnp.int32, (tm, tm), 1)
    rk = jnp.dot(selb, jnp.where(tr < tc, 1.0, 0.0).astype(BF16), preferred_element_type=F32)
    cnt = jnp.sum(self_, axis=1, keepdims=True)
    cnt_al = jnp.floor((cnt + (SEG_ALIGN - 1)) * (1.0 / SEG_ALIGN)) * SEG_ALIGN
    cnt_b = jnp.broadcast_to(cnt_al, (ne, LANES))
    cnt_ref[...] = cnt_b
    off = jnp.dot(jnp.where(ec < er, 1.0, 0.0).astype(BF16), cnt_b.astype(BF16), preferred_element_type=F32)
    rowidx = off[:, 0:1] + rk
    rows_k, gates_k, exps_k = [], [], []
    for k in range(1, TOP_K + 1):
        mk = sel & (c == k)
        has = jnp.sum(jnp.where(mk, 1.0, 0.0), axis=0, keepdims=True)
        rows_k.append(jnp.sum(jnp.where(mk, rowidx, 0.0), axis=0, keepdims=True) + has - 1.0)
        gates_k.append(jnp.sum(jnp.where(mk, gate, 0.0), axis=0, keepdims=True))
        exps_k.append(jnp.sum(jnp.where(mk, eidx.astype(F32), 0.0), axis=0, keepdims=True))
    info_ref[...] = jnp.concatenate(rows_k + gates_k + exps_k + [jnp.zeros((4, tm), F32)], axis=0)
    ridx = lax.broadcasted_iota(jnp.int32, (MOE_RL, tm), 0).astype(F32)
    perm = jnp.zeros((MOE_RL, tm), F32)
    for k in range(TOP_K):
        perm = jnp.where(ridx == rows_k[k], 1.0, perm)
    xs = jnp.dot(perm.astype(BF16), h2b, preferred_element_type=F32)
    xsl_ref[...] = _pack_halves(xs, bf16_exact=True)


def _mixout_kernel(x_ref, hm_ref, on_ref, mod_ref, gmix_ref, gffn_ref,
                   wog_ref, bog_ref, wum_ref, wua_ref, wout_ref, wrt_ref, brt_ref,
                   x1_ref, xsl_ref, info_ref, cnt_ref, *, tm, t_mod, t_valid, m_valid, n_real):
    if n_real is not None:
        @pl.when(pl.program_id(0) >= n_real)
        def _():
            xsl_ref[...] = jnp.zeros(xsl_ref.shape, jnp.uint32)
            info_ref[...] = jnp.zeros(info_ref.shape, F32)
            cnt_ref[...] = jnp.zeros(cnt_ref.shape, F32)

        @pl.when(pl.program_id(0) < n_real)
        def _():
            _mixout_body(x_ref, hm_ref, on_ref, mod_ref, gmix_ref, gffn_ref, wog_ref, bog_ref, wum_ref,
                         wua_ref, wout_ref, wrt_ref, brt_ref, x1_ref, xsl_ref, info_ref, cnt_ref,
                         tm, t_mod, t_valid, m_valid)
    else:
        _mixout_body(x_ref, hm_ref, on_ref, mod_ref, gmix_ref, gffn_ref, wog_ref, bog_ref, wum_ref,
                     wua_ref, wout_ref, wrt_ref, brt_ref, x1_ref, xsl_ref, info_ref, cnt_ref,
                     tm, t_mod, t_valid, m_valid)


def _mixout_body(x_ref, hm_ref, on_ref, mod_ref, gmix_ref, gffn_ref,
                 wog_ref, bog_ref, wum_ref, wua_ref, wout_ref, wrt_ref, brt_ref,
                 x1_ref, xsl_ref, info_ref, cnt_ref, tm, t_mod, t_valid, m_valid):
    d = D_MODEL
    x = x_ref[...]
    sh1, sc1, gt1 = mod_ref[:, 0:d], mod_ref[:, d:2 * d], mod_ref[:, 2 * d:3 * d]
    sh2, sc2 = mod_ref[:, 3 * d:4 * d], mod_ref[:, 4 * d:5 * d]
    h = _rmsnorm_rows(x, gmix_ref[...]) * (1.0 + sc1) + sh1
    hb = h.astype(BF16)
    mo = jnp.dot(hb, wog_ref[:, 0:M_WIDTH], preferred_element_type=F32) + bog_ref[:, 0:M_WIDTH]
    ym = _bdot(_sigmoid(mo) * hm_ref[...], wum_ref[...])
    ya = _bdot(on_ref[...], wua_ref[...])
    ga = jnp.dot(hb, wog_ref[:, M_WIDTH:M_WIDTH + d], preferred_element_type=F32) + bog_ref[:, M_WIDTH:M_WIDTH + d]
    u = _sigmoid(ga) * ym
    gb = (jnp.dot(hb, wog_ref[:, M_WIDTH + d:M_WIDTH + 2 * d], preferred_element_type=F32)
          + bog_ref[:, M_WIDTH + d:M_WIDTH + 2 * d])
    u = u + _sigmoid(gb) * ya
    x1 = x + gt1 * _bdot(u, wout_ref[...])
    x1_ref[...] = x1
    h2 = _rmsnorm_rows(x1, gffn_ref[...]) * (1.0 + sc2) + sh2
    _route_and_sort(h2, wrt_ref, brt_ref, xsl_ref, info_ref, cnt_ref, tm, t_mod, t_valid, m_valid)


def _mixout_with_shared(*refs, n_shared, **kw):
    n_in = 13
    _mixout_kernel(*refs[:n_in], *refs[n_in + n_shared:], **kw)


def _mixout(x2, hm, on, mod3, gmix, gffn, wts, tiles_per_mod, nt_total, tile0=0, shared=None,
            t_mod=None, t_valid=None, m_valid=None):
    m = x2.shape[0]
    tm = MOE_TM
    nt = m // tm
    (wog, bog, wum, wua, wout, wr, br) = wts
    r = mod3.shape[1]
    n_extra = nt_total - tile0 - nt if shared is None else 0
    row = lambda i: (jnp.minimum(i, nt - 1), 0)
    const = lambda i: (0, 0)
    in_specs = [pl.BlockSpec((tm, D_MODEL), row), pl.BlockSpec((tm, M_WIDTH), row),
                pl.BlockSpec((tm, A_WIDTH), row),
                pl.BlockSpec((None, r, 6 * D_MODEL), lambda i: (jnp.minimum(i, nt - 1) // tiles_per_mod, 0, 0)),
                pl.BlockSpec((1, D_MODEL), const), pl.BlockSpec((1, D_MODEL), const),
                pl.BlockSpec(wog.shape, const), pl.BlockSpec(bog.shape, const),
                pl.BlockSpec(wum.shape, const), pl.BlockSpec(wua.shape, const),
                pl.BlockSpec(wout.shape, const), pl.BlockSpec(wr.shape, const),
                pl.BlockSpec(br.shape, const)]
    args = [x2, hm, on, mod3, gmix, gffn, wog, bog, wum, wua, wout, wr, br]
    kw = dict(tm=tm, t_mod=t_mod, t_valid=t_valid, m_valid=m_valid, n_real=nt if n_extra else None)
    body = functools.partial(_mixout_kernel, **kw)
    aliases = {}
    if shared is not None:
        in_specs += [pl.BlockSpec(memory_space=pl.ANY)] * len(shared)
        aliases = {len(args) + j: 1 + j for j in range(len(shared))}
        args += list(shared)
        body = functools.partial(_mixout_with_shared, n_shared=len(shared), **kw)
    return pl.pallas_call(
        body,
        grid=(nt + n_extra,),
        in_specs=in_specs,
        out_specs=[pl.BlockSpec((tm, D_MODEL), row),
                   pl.BlockSpec((MOE_RL, D_MODEL // 2), lambda i: (tile0 + i, 0)),
                   pl.BlockSpec((16, tm), lambda i: (0, tile0 + i)),
                   pl.BlockSpec((None, N_EXPERTS, LANES), lambda i: (tile0 + i, 0, 0))],
        out_shape=[jax.ShapeDtypeStruct((m, D_MODEL), F32),
                   jax.ShapeDtypeStruct((nt_total * MOE_RL, D_MODEL // 2), jnp.uint32),
                   jax.ShapeDtypeStruct((16, nt_total * tm), F32),
                   jax.ShapeDtypeStruct((nt_total, N_EXPERTS, LANES), F32)],
        input_output_aliases=aliases,
        compiler_params=_cparams(("arbitrary" if n_extra else "parallel",)),
        name="mixout",
    )(*args)


MOE_BM = 256
MOE_CH = 512


def _moe_kernel(be_ref, na_ref, grp_ref,
                xsl_ref, wgu_ref, bgu_ref, wdn_ref, bdn_ref, ysl_ref,
                wgu_bf, wdn_bf, xbuf, ybuf, sem_in, sem_out, *, trash_row0):
    i = pl.program_id(0)
    na = na_ref[0]
    e = be_ref[i]
    prev = be_ref[jnp.maximum(i - 1, 0)]
    n_grp = MOE_BM // SEG_ALIGN

    def group_copies(blk, inbound, slot=None):
        slot = blk % 2 if slot is None else slot
        cps = []
        for r in range(n_grp):
            v = grp_ref[blk * n_grp + r]
            vm_rows = pl.ds(r * SEG_ALIGN, SEG_ALIGN)
            if inbound:
                row = pl.multiple_of(jnp.where(v >= 0, v, trash_row0 + 2 * MOE_BM), SEG_ALIGN)
                cps.append(pltpu.make_async_copy(xsl_ref.at[pl.ds(row, SEG_ALIGN), :],
                                                 xbuf.at[slot, vm_rows, :], sem_in.at[slot]))
            else:
                spare = trash_row0 + slot * MOE_BM + r * SEG_ALIGN
                row = pl.multiple_of(jnp.where(v >= 0, v, spare), SEG_ALIGN)
                cps.append(pltpu.make_async_copy(ybuf.at[slot, vm_rows, :],
                                                 ysl_ref.at[pl.ds(row, SEG_ALIGN), :], sem_out.at[slot]))
        return cps

    def start_gather(blk):
        for cp in group_copies(blk, True):
            cp.start()

    def start_scatter(blk):
        for cp in group_copies(blk, False):
            cp.start()

    def wait_rows(blk, sem, inbound):
        slot = blk % 2
        if inbound:
            pltpu.make_async_copy(xsl_ref.at[pl.ds(0, MOE_BM), :], xbuf.at[slot], sem.at[slot]).wait()
        else:
            pltpu.make_async_copy(ybuf.at[slot], ysl_ref.at[pl.ds(0, MOE_BM), :], sem.at[slot]).wait()

    @pl.when(i == 0)
    def _():
        start_gather(i)

    @pl.when(i + 1 < na)
    def _():
        start_gather(i + 1)

    @pl.when((i < na) & ((i == 0) | (e != prev)))
    def _():
        for j in range(2 * D_EXPERT // MOE_CH):
            wgu_bf[:, j * MOE_CH:(j + 1) * MOE_CH] = wgu_ref[:, j * MOE_CH:(j + 1) * MOE_CH].astype(BF16)
        for j in range(D_EXPERT // MOE_CH):
            wdn_bf[j * MOE_CH:(j + 1) * MOE_CH, :] = wdn_ref[j * MOE_CH:(j + 1) * MOE_CH, :].astype(BF16)

    @pl.when(i < na)
    def _():
        slot = i % 2
        wait_rows(i, sem_in, True)

        @pl.when(i >= 2)
        def _():
            wait_rows(i - 2, sem_out, False)

        half = D_MODEL // 2
        xh, xl = _unpack_halves(xbuf[slot])

        def xdot(c0, c1):
            return (jnp.dot(xh, wgu_bf[0:half, c0:c1], preferred_element_type=F32)
                    + jnp.dot(xl, wgu_bf[half:D_MODEL, c0:c1], preferred_element_type=F32))

        acc = jnp.zeros((MOE_BM, D_MODEL), F32) + bdn_ref[...]
        for j in range(D_EXPERT // MOE_CH):
            lo, hi = j * MOE_CH, (j + 1) * MOE_CH
            gj = xdot(lo, hi) + bgu_ref[:, lo:hi]
            uj = xdot(D_EXPERT + lo, D_EXPERT + hi) + bgu_ref[:, D_EXPERT + lo:D_EXPERT + hi]
            gj = jnp.minimum(gj, SWIGLU_LIMIT)
            uj = jnp.clip(uj, -SWIGLU_LIMIT, SWIGLU_LIMIT)
            act = gj * _sigmoid(SWIGLU_ALPHA * gj) * (uj + 1.0)
            acc = acc + jnp.dot(act.astype(BF16), wdn_bf[lo:hi, :], preferred_element_type=F32)
        ybuf[slot] = _pack_halves(acc)
        start_scatter(i)

        @pl.when(i == na - 1)
        def _():
            @pl.when(i >= 1)
            def _():
                wait_rows(i - 1, sem_out, False)
            wait_rows(i, sem_out, False)


def _moe_experts(plan, xsl, w_gu, b_gu, w_dn, b_dn):
    block_e, n_active, grp_rows = plan
    nblk = block_e.shape[0]
    spare_row0 = xsl.shape[0] - MOE_RL
    assert MOE_RL >= 2 * MOE_BM
    wmap = lambda i, be, *_: (be[i], 0, 0)
    anyspec = pl.BlockSpec(memory_space=pl.ANY)
    gs = pltpu.PrefetchScalarGridSpec(
        num_scalar_prefetch=3,
        grid=(nblk,),
        in_specs=[anyspec,
                  pl.BlockSpec((None, D_MODEL, 2 * D_EXPERT), wmap),
                  pl.BlockSpec((None, 1, 2 * D_EXPERT), wmap),
                  pl.BlockSpec((None, D_EXPERT, D_MODEL), wmap),
                  pl.BlockSpec((None, 1, D_MODEL), wmap)],
        out_specs=anyspec,
        scratch_shapes=[pltpu.VMEM((D_MODEL, 2 * D_EXPERT), BF16), pltpu.VMEM((D_EXPERT, D_MODEL), BF16),
                        pltpu.VMEM((2, MOE_BM, D_MODEL // 2), jnp.uint32),
                        pltpu.VMEM((2, MOE_BM, D_MODEL // 2), jnp.uint32),
                        pltpu.SemaphoreType.DMA((2,)), pltpu.SemaphoreType.DMA((2,))],
    )
    return pl.pallas_call(
        functools.partial(_moe_kernel, trash_row0=spare_row0),
        grid_spec=gs,
        out_shape=jax.ShapeDtypeStruct(xsl.shape, jnp.uint32),
        input_output_aliases={3: 0},
        compiler_params=_cparams(("arbitrary",)),
        name="moe_experts",
    )(*plan, xsl, w_gu, b_gu.reshape(N_EXPERTS, 1, -1), w_dn, b_dn.reshape(N_EXPERTS, 1, -1))


def _combine_kernel(ysl_ref, info_ref, x1_ref, mod_ref, y_ref, *, tm):
    info = info_ref[...]
    info_t = jnp.transpose(jnp.concatenate([info, jnp.zeros((LANES - info.shape[0], tm), F32)], axis=0))
    ridx = lax.broadcasted_iota(jnp.int32, (tm, MOE_RL), 1).astype(F32)
    pg = jnp.zeros((tm, MOE_RL), F32)
    for k in range(TOP_K):
        pg = jnp.where(ridx == info_t[:, k:k + 1], info_t[:, TOP_K + k:TOP_K + k + 1], pg)
    pgb = pg.astype(BF16)
    yh, yl = _unpack_halves(ysl_ref[...])
    half = D_MODEL // 2
    gt2 = mod_ref[:, 5 * D_MODEL:6 * D_MODEL]
    for c, yy in ((0, yh), (1, yl)):
        moe = jnp.dot(pgb, yy, preferred_element_type=F32)
        y_ref[:, c * half:(c + 1) * half] = (x1_ref[:, c * half:(c + 1) * half]
                                             + gt2[:, c * half:(c + 1) * half] * moe)


def _combine(ysl, info, x1, mod3, tiles_per_mod, tile0=0):
    m = x1.shape[0]
    tm = MOE_TM
    r = mod3.shape[1]
    return pl.pallas_call(
        functools.partial(_combine_kernel, tm=tm),
        grid=(m // tm,),
        in_specs=[pl.BlockSpec((MOE_RL, D_MODEL // 2), lambda i: (tile0 + i, 0)),
                  pl.BlockSpec((16, tm), lambda i: (0, tile0 + i)),
                  pl.BlockSpec((tm, D_MODEL), lambda i: (i, 0)),
                  pl.BlockSpec((None, r, 6 * D_MODEL), lambda i: (i // tiles_per_mod, 0, 0))],
        out_specs=pl.BlockSpec((tm, D_MODEL), lambda i: (i, 0)),
        out_shape=jax.ShapeDtypeStruct((m, D_MODEL), F32),
        compiler_params=_cparams(("parallel",)),
        name="moe_combine",
    )(ysl, info, x1, mod3)


def _moe_plan(cnt):
    cnt = cnt.astype(jnp.int32)
    nt = cnt.shape[0]
    so = jnp.cumsum(cnt, axis=1) - cnt + (jnp.arange(nt) * MOE_RL)[:, None]
    ce = jnp.cumsum(cnt, axis=0)
    cs = ce - cnt
    tot = ce[-1]
    nblk_e = (tot + MOE_BM - 1) // MOE_BM
    blk_end = jnp.cumsum(nblk_e)
    max_rows = nt * MOE_TM * TOP_K + nt * N_EXPERTS * (SEG_ALIGN - 1)
    n_blocks = -(-max_rows // MOE_BM) + N_EXPERTS
    bidx = jnp.arange(n_blocks)
    block_e = jnp.minimum(jnp.sum(blk_end[None, :] <= bidx[:, None], axis=1), N_EXPERTS - 1).astype(jnp.int32)
    is_e = (jnp.arange(N_EXPERTS)[:, None] == block_e[None, :]).astype(jnp.int32)
    per_block = lambda a: jnp.sum(a[..., :, None] * is_e, axis=-2)
    block_r0 = (bidx - per_block(blk_end - nblk_e)) * MOE_BM
    x = block_r0[:, None] + jnp.arange(MOE_BM // SEG_ALIGN)[None, :] * SEG_ALIGN
    ce_b = per_block(ce)[:, :, None]
    cs_b = per_block(cs)[:, :, None]
    inside = (cs_b <= x[None]) & (x[None] < ce_b)
    grp = x + jnp.sum(jnp.where(inside, per_block(so - cs)[:, :, None], 0), axis=0)
    grp = jnp.where(x < per_block(tot)[:, None], grp, -1)
    n_active = blk_end[-1].reshape(1)
    i32 = lambda a: a.reshape(-1).astype(jnp.int32)
    return block_e, i32(n_active), i32(grp)


def _rope_tables(pos):
    half = ROT_DIM // 2
    inv = ROPE_THETA ** (-jnp.arange(half, dtype=F32) * (2.0 / ROT_DIM))
    ang = pos.astype(F32)[:, None] * inv[None, :]
    cos, sin = jnp.cos(ang), jnp.sin(ang)
    n = pos.shape[0]
    ones = jnp.ones((n, A_DH - ROT_DIM), F32)
    zeros_h = jnp.zeros((n, half), F32)
    zeros_r = jnp.zeros((n, A_DH - ROT_DIM), F32)
    cos64 = jnp.concatenate([cos, cos, ones], axis=1)
    sprev64 = jnp.concatenate([zeros_h, sin, zeros_r], axis=1)
    snext64 = jnp.concatenate([-sin, zeros_h, zeros_r], axis=1)
    two = lambda a: jnp.concatenate([a, a], axis=1)
    return two(cos64), two(sprev64), two(snext64)


def _prep_weights(w_in, b_in, q_norm_g, k_norm_g, cmp_pe_k, cmp_pe_v, cmp_w_k, cmp_w_v,
                  w_up_m, w_up_a, w_out, w_router, b_router):
    b2 = b_in.reshape(1, N_IN)
    wm = w_in[:, OFF_MQ:OFF_MO].astype(BF16)
    bm = b2[:, OFF_MQ:OFF_MO]
    wq = w_in[:, OFF_AQ:OFF_AKV].astype(BF16)
    bq = b2[:, OFF_AQ:OFF_AKV]
    wkv = w_in[:, OFF_AKV:OFF_AG].astype(BF16)
    bkv = b2[:, OFF_AKV:OFF_AG]
    n_small = 2 * M_HEADS + 3 * A_HEADS
    ws = jnp.concatenate([w_in[:, OFF_MI:OFF_AQ], w_in[:, OFF_AG:OFF_GA],
                          jnp.zeros((D_MODEL, LANES - n_small), F32)], axis=1)
    bs = jnp.concatenate([b2[:, OFF_MI:OFF_AQ], b2[:, OFF_AG:OFF_GA], jnp.zeros((1, LANES - n_small), F32)], axis=1)
    qg = jnp.tile(q_norm_g, A_HEADS).reshape(1, A_WIDTH)
    kg = jnp.stack([jnp.tile(k_norm_g[1], A_KV), jnp.tile(k_norm_g[2], A_KV)], axis=0)
    kg0 = jnp.tile(k_norm_g[0], A_KV).reshape(1, LANES)
    hid = jnp.arange(A_WIDTH) // A_DH
    bd = jnp.where(hid[:, None] == hid[None, :], 1.0 / A_DH, 0.0).astype(BF16)
    inproj_w = (wm, bm, wq, bq, wkv, bkv, ws, bs, qg, kg, bd)

    z = jnp.zeros((CMP_LEN, A_DH, A_DH), F32)
    r0 = jnp.concatenate([cmp_w_k, z, z, z], axis=2)
    r1 = jnp.concatenate([z, cmp_w_k, z, z], axis=2)
    r2 = jnp.concatenate([z, z, cmp_w_v, z], axis=2)
    r3 = jnp.concatenate([z, z, z, cmp_w_v], axis=2)
    wbd = jnp.concatenate([r0, r1, r2, r3], axis=1).astype(BF16)
    pe = jnp.concatenate([cmp_pe_k, cmp_pe_k, cmp_pe_v, cmp_pe_v], axis=1).reshape(CMP_LEN, 1, 2 * LANES)

    wog = jnp.concatenate([w_in[:, OFF_MO:OFF_MI], w_in[:, OFF_GA:N_IN]], axis=1).astype(BF16)
    bog = jnp.concatenate([b2[:, OFF_MO:OFF_MI], b2[:, OFF_GA:N_IN]], axis=1)
    mixout_w = (wog, bog, w_up_m.astype(BF16), w_up_a.astype(BF16), w_out.astype(BF16),
                w_router.T, b_router.reshape(N_EXPERTS, 1))
    return inproj_w, (wbd, pe, kg0), mixout_w


def _pick_tile(m, pref):
    t = pref
    while m % t:
        t //= 2
    return t


def kernel(x_prompt, x_sample, cache_nsa_kv, state_win_kv, state_mlstm_C, state_mlstm_n, state_mlstm_m, page_table, c_prompt, c_sample, w_ada, b_ada, g_mix, g_ffn, w_in, b_in, q_norm_g, k_norm_g, cmp_pe_k, cmp_pe_v, cmp_w_k, cmp_w_v, w_up_m, w_up_a, w_out, w_router, b_router, w_gu, b_gu, w_dn, b_dn):
    depth = w_in.shape[0]
    assert depth == 1
    B, T, D = x_prompt.shape
    DB, TS, _ = x_sample.shape
    n_pages = page_table.shape[1]
    past_len = n_pages * PAGE_SIZE
    wbuf = state_win_kv.shape[2]
    tp = SAMPLE_PAD_T
    assert TS <= tp and wbuf % tp == 0 and T % 128 == 0

    l = 0
    inproj_w, cmp_w, mixout_w = _prep_weights(
        w_in[l], b_in[l], q_norm_g[l], k_norm_g[l], cmp_pe_k[l], cmp_pe_v[l], cmp_w_k[l], cmp_w_v[l],
        w_up_m[l], w_up_a[l], w_out[l], w_router[l], b_router[l])
    wbd, pe, kg0 = cmp_w
    gmix = g_mix[l].reshape(1, D)
    gffn = g_ffn[l].reshape(1, D)

    nc = B + DB
    nc_pad = -(-nc // SUBLANES) * SUBLANES
    c_all = jnp.concatenate([c_prompt, c_sample, jnp.zeros((nc_pad - nc, D), F32)], axis=0)
    mod = _adaln(c_all, w_ada[l], b_ada[l])
    mod_p = mod[:B].reshape(B, 1, 6 * D)
    mod_s = jnp.repeat(mod[B:B + DB], tp, axis=0).reshape(1, DB * tp, 6 * D)

    mp = B * T
    tm = _pick_tile(T, 256)
    xp = x_prompt.reshape(mp, D)
    tabs_p = _rope_tables(jnp.arange(T, dtype=jnp.int32))
    mq, mk, mv, q, qr, rows, win, small, rows_t, win_t = _inproj(xp, mod_p, gmix, tabs_p, inproj_w, tm, T // tm,
                                                                 T // tm, rows_t_batches=B)
    Lp = _pick_tile(T, 128)
    hm, C_p, n_p, m_p = _mlstm(mq, mk, mv, small, B, T, T, Lp)
    o_nsa = _nsa_prompt(q, qr, small, rows, win, wbd, pe, kg0, B, T)
    assert T % MOE_TM == 0
    ms_pad = -(-(DB * tp) // MOE_TM) * MOE_TM
    nt_p = mp // MOE_TM
    nt_all = nt_p + ms_pad // MOE_TM + 1
    x1_p, xsl, info, cnt = _mixout(xp, hm, o_nsa, mod_p, gmix, gffn, mixout_w, T // MOE_TM, nt_all)

    ms = DB * tp
    xs_pad = jnp.concatenate([x_sample, jnp.zeros((DB, tp - TS, D), F32)], axis=1).reshape(ms, D)
    pos_s = past_len + jnp.tile(jnp.arange(tp, dtype=jnp.int32), DB)
    tabs_s = _rope_tables(pos_s)
    mq_s, mk_s, mv_s, q_s, qr_s, rows_s, win_s, small_s = _inproj(xs_pad, mod_s, gmix, tabs_s, inproj_w, ms, 1, 1)
    hm_s, C_s, n_s, m_s = _mlstm(mq_s, mk_s, mv_s, small_s, DB, tp, TS, tp,
                                 state=(state_mlstm_C[l], state_mlstm_n[l], state_mlstm_m[l]))
    cache2 = jnp.transpose(cache_nsa_kv[l], (0, 2, 3, 4, 1)).reshape(cache_nsa_kv.shape[1], 4 * LANES, PAGE_SIZE)
    winbuf = jnp.transpose(state_win_kv[l], (0, 2, 3, 4, 1)).reshape(DB, 2 * LANES, wbuf)
    o_nsa_s, win_out_s = _nsa_sample(page_table, cache2, q_s, qr_s, small_s, rows_s, win_s, winbuf,
                                     wbd, pe, kg0, TS)
    assert ms_pad == MOE_TM
    rpad = lambda a: jnp.concatenate([a, jnp.zeros((ms_pad - ms, a.shape[1]), a.dtype)], axis=0) if ms_pad > ms else a
    mod_sp = rpad(mod_s[0])[None]
    x1_s, xsl, info, cnt = _mixout(rpad(xs_pad), rpad(hm_s), rpad(o_nsa_s), mod_sp, gmix, gffn, mixout_w,
                                   1, nt_all, tile0=nt_p, shared=(xsl, info, cnt),
                                   t_mod=tp, t_valid=TS, m_valid=ms)

    ysl = _moe_experts(_moe_plan(cnt[:, :, 0]), xsl, w_gu[l], b_gu[l], w_dn[l], b_dn[l])
    y_p = _combine(ysl, info, x1_p, mod_p, T // MOE_TM).reshape(B, T, D)
    y_s_all = _combine(ysl, info, x1_s, mod_sp, 1, tile0=nt_p)
    valid = lambda a: a.reshape(DB, tp, -1)[:, :TS].reshape(DB * TS, -1)
    y_s = valid(y_s_all[:ms]).reshape(DB, TS, D)

    kv_p = jnp.transpose(rows_t.reshape(B, 4, A_KV, A_DH, T), (0, 4, 1, 2, 3))[None]
    kv_s = valid(rows_s).reshape(1, DB, TS, 4, A_KV, A_DH)
    wp = min(WINDOW, T)
    win_p = jnp.transpose(win_t[:, :, T - wp:].reshape(B, 2, A_KV, A_DH, wp), (0, 4, 1, 2, 3))[None]
    win_s_out = jnp.transpose(win_out_s.reshape(DB, 2, A_KV, A_DH, wbuf), (0, 4, 1, 2, 3))[None]
    return (y_p, y_s, kv_p, kv_s, win_p, win_s_out,
            C_p[None], n_p[None], m_p[None], C_s[None], n_s[None], m_s[None])
```

```python
import functools

import jax
import jax.numpy as jnp
from jax import lax
from jax.experimental import pallas as pl
from jax.experimental.pallas import tpu as pltpu

F32 = jnp.float32
BF16 = jnp.bfloat16

D_MODEL = 1024
M_HEADS = 4
M_DH = 128
M_WIDTH = M_HEADS * M_DH
A_HEADS = 8
A_KV = 2
A_HPG = A_HEADS // A_KV
A_DH = 64
A_WIDTH = A_HEADS * A_DH
CMP_STRIDE = 16
CMP_LEN = 32
SEL_LEN = 64
N_SEL = 16
WINDOW = 512
PAGE_SIZE = 128
ROPE_THETA = 500000.0
ROT_DIM = A_DH // 4
ATT_SCALE = A_DH ** -0.5
N_EXPERTS = 32
TOP_K = 4
D_EXPERT = D_MODEL
SWIGLU_LIMIT = 7.0
SWIGLU_ALPHA = 1.702
EPS = 1e-6

OFF_MQ, OFF_MK, OFF_MV, OFF_MO = 0, M_WIDTH, 2 * M_WIDTH, 3 * M_WIDTH
OFF_MI = 4 * M_WIDTH
OFF_MF = OFF_MI + M_HEADS
OFF_AQ = OFF_MF + M_HEADS
OFF_AKV = OFF_AQ + A_WIDTH
OFF_AG = OFF_AKV + 6 * A_KV * A_DH
OFF_GA = OFF_AG + 3 * A_HEADS
OFF_GB = OFF_GA + D_MODEL
N_IN = OFF_GB + D_MODEL

LANES = 128
SUBLANES = 8
VMEM_LIMIT = 56 * 1024 * 1024

NEG_BIG = -1e30
M_INIT = -1e29
LOG2E = 1.4426950408889634
SAMPLE_PAD_T = 8


def _cparams(sem):
    return pltpu.CompilerParams(dimension_semantics=sem, vmem_limit_bytes=VMEM_LIMIT)


def _bdot(a, b):
    return jnp.dot(a.astype(BF16), b.astype(BF16), preferred_element_type=F32)


def _bdot_t(a, b):
    return lax.dot_general(a.astype(BF16), b.astype(BF16), (((1,), (1,)), ((), ())),
                           preferred_element_type=F32)


def _split(a):
    hi = a.astype(BF16)
    lo = (a - hi.astype(F32)).astype(BF16)
    return hi, lo


def _dot3(a, b):
    ah, al = _split(a)
    bh, bl = _split(b)
    return (jnp.dot(ah, bh, preferred_element_type=F32) + jnp.dot(al, bh, preferred_element_type=F32)
            + jnp.dot(ah, bl, preferred_element_type=F32))


def _dot2_exact_rhs(a, b_bf16):
    ah, al = _split(a)
    return jnp.dot(ah, b_bf16, preferred_element_type=F32) + jnp.dot(al, b_bf16, preferred_element_type=F32)


def _sigmoid(x):
    return 0.5 * jnp.tanh(0.5 * x) + 0.5


def _rmsnorm_rows(x, g):
    return x * lax.rsqrt(jnp.mean(x * x, axis=-1, keepdims=True) + EPS) * g


def _adaln_kernel(c_ref, w_ref, b_ref, o_ref):
    c = c_ref[...]
    s = c * _sigmoid(c)
    o_ref[...] = _dot3(s, w_ref[...]) + b_ref[...]


def _adaln(c, w, b):
    mc, d = c.shape
    n = w.shape[1]
    tn = 1024
    return pl.pallas_call(
        _adaln_kernel,
        grid=(n // tn,),
        in_specs=[pl.BlockSpec((mc, d), lambda j: (0, 0)),
                  pl.BlockSpec((d, tn), lambda j: (0, j)),
                  pl.BlockSpec((1, tn), lambda j: (0, j))],
        out_specs=pl.BlockSpec((mc, tn), lambda j: (0, j)),
        out_shape=jax.ShapeDtypeStruct((mc, n), F32),
        compiler_params=_cparams(("parallel",)),
        name="adaln",
    )(c, w, b.reshape(1, n))


def _head_norm(z, bd, gain):
    ms = _dot2_exact_rhs(z * z, bd)
    return z * lax.rsqrt(ms + EPS) * gain


def _rope(z, cos, s_prev, s_next):
    w = z.shape[1]
    rep = w // LANES
    if rep > 1:
        cos = jnp.concatenate([cos] * rep, axis=1)
        s_prev = jnp.concatenate([s_prev] * rep, axis=1)
        s_next = jnp.concatenate([s_next] * rep, axis=1)
    z_prev = pltpu.roll(z, ROT_DIM // 2, 1)
    z_next = pltpu.roll(z, w - ROT_DIM // 2, 1)
    return z * cos + z_prev * s_prev + z_next * s_next


def _inproj_kernel(x_ref, mod_ref, gmix_ref, cos_ref, sp_ref, sn_ref,
                   wm_ref, bm_ref, wq_ref, bq_ref, wkv_ref, bkv_ref, ws_ref, bs_ref,
                   qg_ref, kg_ref, bd_ref,
                   mq_ref, mk_ref, mv_ref, q_ref, qr_ref, rows_ref, win_ref, small_ref,
                   rows_t_ref=None, win_t_ref=None):
    x = x_ref[...]
    sh1 = mod_ref[:, 0:D_MODEL]
    sc1 = mod_ref[:, D_MODEL:2 * D_MODEL]
    h = _rmsnorm_rows(x, gmix_ref[...]) * (1.0 + sc1) + sh1
    hb = h.astype(BF16)

    mq_ref[...] = jnp.dot(hb, wm_ref[:, 0:M_WIDTH], preferred_element_type=F32) + bm_ref[:, 0:M_WIDTH]
    mk = jnp.dot(hb, wm_ref[:, M_WIDTH:2 * M_WIDTH], preferred_element_type=F32) + bm_ref[:, M_WIDTH:2 * M_WIDTH]
    mk_ref[...] = mk * (M_DH ** -0.5)
    mv_ref[...] = (jnp.dot(hb, wm_ref[:, 2 * M_WIDTH:3 * M_WIDTH], preferred_element_type=F32)
                   + bm_ref[:, 2 * M_WIDTH:3 * M_WIDTH])

    cos, sp, sn = cos_ref[...], sp_ref[...], sn_ref[...]
    zq = jnp.dot(hb, wq_ref[...], preferred_element_type=F32) + bq_ref[...]
    qn = _head_norm(zq, bd_ref[...], qg_ref[...])
    q_ref[...] = qn
    qr_ref[...] = _rope(qn, cos, sp, sn)

    zkv = jnp.dot(hb, wkv_ref[...], preferred_element_type=F32) + bkv_ref[...]
    bd2 = bd_ref[0:LANES, 0:LANES]
    ksel = _head_norm(zkv[:, 2 * LANES:3 * LANES], bd2, kg_ref[0:1, :])
    rows = jnp.concatenate([zkv[:, 0:2 * LANES], _rope(ksel, cos, sp, sn), zkv[:, 3 * LANES:4 * LANES]], axis=1)
    rows_ref[...] = rows
    if rows_t_ref is not None:
        rows_t_ref[...] = jnp.transpose(rows)
    kwin = _head_norm(zkv[:, 4 * LANES:5 * LANES], bd2, kg_ref[1:2, :])
    win = jnp.concatenate([_rope(kwin, cos, sp, sn), zkv[:, 5 * LANES:6 * LANES]], axis=1)
    win_ref[...] = win
    if win_t_ref is not None:
        win_t_ref[...] = jnp.transpose(win)

    small_ref[...] = _dot3(h, ws_ref[...]) + bs_ref[...]


def _inproj(x2, mod3, gmix, tabs, wts, tm, tiles_per_mod, pos_tiles, rows_t_batches=None):
    m = x2.shape[0]
    cos_t, sp_t, sn_t = tabs
    (wm, bm, wq, bq, wkv, bkv, ws, bs, qg, kg, bd) = wts
    r = mod3.shape[1]
    row = lambda i: (i, 0)
    const = lambda i: (0, 0)
    tab = lambda i: (i % pos_tiles, 0)
    in_specs = [
        pl.BlockSpec((tm, D_MODEL), row),
        pl.BlockSpec((None, r, 6 * D_MODEL), lambda i: (i // tiles_per_mod, 0, 0)),
        pl.BlockSpec((1, D_MODEL), const),
        pl.BlockSpec((tm, LANES), tab), pl.BlockSpec((tm, LANES), tab), pl.BlockSpec((tm, LANES), tab),
        pl.BlockSpec(wm.shape, const), pl.BlockSpec(bm.shape, const),
        pl.BlockSpec(wq.shape, const), pl.BlockSpec(bq.shape, const),
        pl.BlockSpec(wkv.shape, const), pl.BlockSpec(bkv.shape, const),
        pl.BlockSpec(ws.shape, const), pl.BlockSpec(bs.shape, const),
        pl.BlockSpec(qg.shape, const), pl.BlockSpec(kg.shape, const), pl.BlockSpec(bd.shape, const),
    ]
    widths = (M_WIDTH, M_WIDTH, M_WIDTH, A_WIDTH, A_WIDTH, 4 * LANES, 2 * LANES, LANES)
    out_specs = [pl.BlockSpec((tm, w), row) for w in widths]
    out_shape = [jax.ShapeDtypeStruct((m, w), F32) for w in widths]
    if rows_t_batches is not None:
        for w in (4 * LANES, 2 * LANES):
            out_specs.append(pl.BlockSpec((None, w, tm), lambda i: (i // tiles_per_mod, 0, i % tiles_per_mod)))
            out_shape.append(jax.ShapeDtypeStruct((rows_t_batches, w, m // rows_t_batches), F32))
    return pl.pallas_call(
        _inproj_kernel,
        grid=(m // tm,),
        in_specs=in_specs,
        out_specs=out_specs,
        out_shape=out_shape,
        compiler_params=_cparams(("parallel",)),
        name="inproj",
    )(x2, mod3, gmix, cos_t, sp_t, sn_t, wm, bm, wq, bq, wkv, bkv, ws, bs, qg, kg, bd)


def _log_sigmoid(x):
    return jnp.minimum(x, 0.0) - jnp.log(1.0 + jnp.exp(-jnp.abs(x)))


def _mlstm_kernel(*refs, L, t_valid, has_state):
    if has_state:
        q_ref, k_ref, v_ref, s_ref, c0_ref, n0_ref, m0_ref, h_ref, c_ref, n_ref, m_ref = refs
    else:
        q_ref, k_ref, v_ref, s_ref, h_ref, c_ref, n_ref, m_ref = refs
    c = pl.program_id(1)

    @pl.when(c == 0)
    def _():
        if has_state:
            c_ref[...] = c0_ref[...]
            n_ref[...] = n0_ref[...]
            m_ref[...] = m0_ref[...]
        else:
            c_ref[...] = jnp.zeros(c_ref.shape, F32)
            n_ref[...] = jnp.zeros(n_ref.shape, F32)
            m_ref[...] = jnp.zeros(m_ref.shape, F32)

    row = lax.broadcasted_iota(jnp.int32, (L, L), 0)
    col = lax.broadcasted_iota(jnp.int32, (L, L), 1)
    causal = col <= row
    eye = col == row
    tok_col = c * L + lax.broadcasted_iota(jnp.int32, (L, 1), 0)
    valid_col = tok_col < t_valid
    for hd in range(M_HEADS):
        lo, hi = hd * M_DH, (hd + 1) * M_DH
        q = q_ref[:, lo:hi]
        k = k_ref[:, lo:hi]
        v = v_ref[:, lo:hi]
        i_col = s_ref[:, hd:hd + 1]
        lf_col = _log_sigmoid(s_ref[:, M_HEADS + hd:M_HEADS + hd + 1])
        lf_col = jnp.where(valid_col, lf_col, 0.0)
        i_col = jnp.where(valid_col, i_col, -jnp.inf)
        if L == LANES:
            i_col = jnp.broadcast_to(i_col, (L, L))
            lf_c = jnp.broadcast_to(lf_col, (L, L))
            p0 = lf_c.astype(BF16)
            r1 = lf_c - p0.astype(F32)
            p1 = r1.astype(BF16)
            p2 = (r1 - p1.astype(F32)).astype(BF16)
            tril = jnp.where(causal, 1.0, 0.0).astype(BF16)
            b_col = (jnp.dot(tril, p0, preferred_element_type=F32) + jnp.dot(tril, p1, preferred_element_type=F32)
                     + jnp.dot(tril, p2, preferred_element_type=F32))
            i_row = jnp.transpose(i_col)[0:1, :]
            b_row = jnp.transpose(b_col)[0:1, :]
        else:
            i_row = jnp.sum(jnp.where(eye, i_col, 0.0), axis=0, keepdims=True)
            lf_row = jnp.sum(jnp.where(eye, lf_col, 0.0), axis=0, keepdims=True)
            b_col = jnp.sum(jnp.where(causal, lf_row, 0.0), axis=1, keepdims=True)
            b_row = jnp.sum(jnp.where(row <= col, lf_col, 0.0), axis=0, keepdims=True)
        m_prev = m_ref[:, hd:hd + 1]
        dmat = jnp.where(causal, b_col - b_row + i_row, -jnp.inf)
        inter = b_col + m_prev
        m_row = jnp.maximum(jnp.max(dmat, axis=1, keepdims=True), inter)
        w = jnp.exp(dmat - m_row)
        w_inter = jnp.exp(inter - m_row)
        s = _bdot_t(q, k) * w
        cm = c_ref[hd]
        nv = n_ref[hd]
        num = _bdot(s, v) + w_inter * _bdot_t(q, cm)
        den = jnp.sum(s, axis=1, keepdims=True) + w_inter * jnp.sum(q * nv, axis=1, keepdims=True)
        h_ref[:, lo:hi] = num / jnp.maximum(jnp.abs(den), jnp.exp(-m_row))
        b_last = b_col[L - 1:L, 0:1]
        dec_col = b_last - b_col + i_col
        dec_row = b_last - b_row + i_row
        m_new = jnp.maximum(b_last + m_prev, jnp.max(dec_row, axis=1, keepdims=True))
        ws_col = jnp.exp(dec_col - m_new)
        wc = jnp.exp(b_last + m_prev - m_new)
        vw = (v * ws_col).astype(BF16)
        upd = lax.dot_general(vw, k.astype(BF16), (((0,), (0,)), ((), ())), preferred_element_type=F32)
        c_ref[hd] = wc * cm + upd
        n_ref[hd] = wc * nv + jnp.sum(k * ws_col, axis=0, keepdims=True)
        m_ref[:, hd:hd + 1] = m_new


def _mlstm(mq, mk, mv, small, nb, t_pad, t_valid, L, state=None):
    nc = t_pad // L
    has_state = state is not None
    blk = lambda b, c: (b * nc + c, 0)
    st4 = lambda b, c: (b, 0, 0, 0)
    st3 = lambda b, c: (b, 0, 0)
    in_specs = [pl.BlockSpec((L, M_WIDTH), blk)] * 3 + [pl.BlockSpec((L, LANES), blk)]
    args = [mq, mk, mv, small]
    if has_state:
        c0, n0, m0 = state
        in_specs += [pl.BlockSpec((None, M_HEADS, M_DH, M_DH), st4),
                     pl.BlockSpec((None, M_HEADS, 1, M_DH), st4),
                     pl.BlockSpec((None, 1, M_HEADS), st3)]
        args += [c0, n0.reshape(nb, M_HEADS, 1, M_DH), m0.reshape(nb, 1, M_HEADS)]
    out_specs = [pl.BlockSpec((L, M_WIDTH), blk),
                 pl.BlockSpec((None, M_HEADS, M_DH, M_DH), st4),
                 pl.BlockSpec((None, M_HEADS, 1, M_DH), st4),
                 pl.BlockSpec((None, 1, M_HEADS), st3)]
    out_shape = [jax.ShapeDtypeStruct((nb * t_pad, M_WIDTH), F32),
                 jax.ShapeDtypeStruct((nb, M_HEADS, M_DH, M_DH), F32),
                 jax.ShapeDtypeStruct((nb, M_HEADS, 1, M_DH), F32),
                 jax.ShapeDtypeStruct((nb, 1, M_HEADS), F32)]
    h, cs, ns, ms = pl.pallas_call(
        functools.partial(_mlstm_kernel, L=L, t_valid=t_valid, has_state=has_state),
        grid=(nb, nc),
        in_specs=in_specs,
        out_specs=out_specs,
        out_shape=out_shape,
        compiler_params=_cparams(("parallel", "arbitrary")),
        name="mlstm",
    )(*args)
    return h, cs, ns.reshape(nb, M_HEADS, M_DH), ms.reshape(nb, M_HEADS)


def _stack_heads(qt, g):
    t = qt.shape[0]
    z = jnp.zeros((t, A_DH), F32)
    parts = []
    for hh in range(A_HPG):
        hd = g * A_HPG + hh
        qh = qt[:, hd * A_DH:(hd + 1) * A_DH] * (ATT_SCALE * LOG2E)
        parts.append(jnp.concatenate([qh, z], axis=1) if g == 0 else jnp.concatenate([z, qh], axis=1))
    return jnp.concatenate(parts, axis=0).astype(BF16)


def _gate_cols(small, g, br):
    cols = []
    for hh in range(A_HPG):
        c0 = 2 * M_HEADS + (g * A_HPG + hh) * 3 + br
        cols.append(_sigmoid(small[:, c0:c0 + 1]))
    return jnp.concatenate(cols, axis=0)


def _compress(k_ref, v_ref, nseg, wbd_ref, pe_ref, kg0):
    acc_lo = jnp.zeros((nseg, 2 * LANES), F32)
    acc_hi = jnp.zeros((nseg, 2 * LANES), F32)
    for l in range(CMP_STRIDE):
        xl = jnp.concatenate([k_ref[pl.ds(l, nseg, stride=CMP_STRIDE), :],
                              v_ref[pl.ds(l, nseg, stride=CMP_STRIDE), :]], axis=1)
        acc_lo = acc_lo + _bdot(xl + pe_ref[l], wbd_ref[l])
        acc_hi = acc_hi + _bdot(xl + pe_ref[CMP_STRIDE + l], wbd_ref[CMP_STRIDE + l])
    return _compress_finish(acc_lo, acc_hi, nseg, kg0)


def _compress_grouped(x_ref, nseg, wbd_ref, pe_ref, kg0):
    acc_lo = jnp.zeros((nseg, 2 * LANES), F32)
    acc_hi = jnp.zeros((nseg, 2 * LANES), F32)
    pe_lo = jnp.zeros((SUBLANES, 2 * LANES), F32)
    pe_hi = jnp.zeros((SUBLANES, 2 * LANES), F32)
    for l in range(CMP_STRIDE):
        xl = x_ref[l].astype(BF16)
        acc_lo = acc_lo + jnp.dot(xl, wbd_ref[l], preferred_element_type=F32)
        acc_hi = acc_hi + jnp.dot(xl, wbd_ref[CMP_STRIDE + l], preferred_element_type=F32)
        pe_lo = pe_lo + _bdot(jnp.broadcast_to(pe_ref[l], (SUBLANES, 2 * LANES)), wbd_ref[l])
        pe_hi = pe_hi + _bdot(jnp.broadcast_to(pe_ref[CMP_STRIDE + l], (SUBLANES, 2 * LANES)),
                              wbd_ref[CMP_STRIDE + l])
    return _compress_finish(acc_lo + pe_lo[0:1, :], acc_hi + pe_hi[0:1, :], nseg, kg0)


def _compress_finish(acc_lo, acc_hi, nseg, kg0):
    kv = acc_lo + pltpu.roll(acc_hi, nseg - 1, 0)
    kc = kv[:, 0:LANES]
    vc = kv[:, LANES:2 * LANES]
    lane = lax.broadcasted_iota(jnp.int32, (nseg, LANES), 1)
    sq = kc * kc
    ms0 = jnp.sum(jnp.where(lane < A_DH, sq, 0.0), axis=1, keepdims=True) * (1.0 / A_DH)
    ms1 = jnp.sum(jnp.where(lane >= A_DH, sq, 0.0), axis=1, keepdims=True) * (1.0 / A_DH)
    ms = jnp.where(lane < A_DH, ms0, ms1)
    kc = kc * lax.rsqrt(ms + EPS) * kg0
    return kc, vc


def _cmp_branch(qn_g, kc_b, vc_b, tpos_rows, nseg, n_tok):
    s = _bdot_t(qn_g, kc_b)
    nidx = lax.broadcasted_iota(jnp.int32, (1, nseg), 1)
    vis = (nidx * CMP_STRIDE + (CMP_LEN - 1)) <= tpos_rows
    sm = jnp.where(vis, s, NEG_BIG)
    mx = jnp.max(sm, axis=1, keepdims=True)
    e = jnp.where(vis, jnp.exp2(sm - mx), 0.0)
    d = jnp.sum(e, axis=1, keepdims=True)
    p = e / jnp.where(d > 0, d, 1.0)
    o = _bdot(p, vc_b)
    imp = p[0:n_tok]
    for hh in range(1, A_HPG):
        imp = imp + p[hh * n_tok:(hh + 1) * n_tok]
    return o, imp


def _masked_attn_direct(q_g, k_parts, v_parts, allowed_parts, feature_major):
    ss = [jnp.where(al, _bdot(q_g, kk) if fm else _bdot_t(q_g, kk), NEG_BIG)
          for kk, al, fm in zip(k_parts, allowed_parts, feature_major)]
    mx = ss[0].max(axis=1, keepdims=True)
    for s in ss[1:]:
        mx = jnp.maximum(mx, s.max(axis=1, keepdims=True))
    num = None
    den = None
    for s, al, vv, fm in zip(ss, allowed_parts, v_parts, feature_major):
        e = jnp.where(al, jnp.exp2(s - mx), 0.0)
        dd = jnp.sum(e, axis=1, keepdims=True)
        oo = _bdot_t(e, vv) if fm else _bdot(e, vv)
        num = oo if num is None else num + oo
        den = dd if den is None else den + dd
    return num / jnp.where(den > 0, den, 1.0)


def _assemble_heads(o_groups, n_tok):
    pieces = []
    for g in range(A_KV):
        for hh in range(A_HPG):
            pieces.append(o_groups[g][hh * n_tok:(hh + 1) * n_tok, g * A_DH:(g + 1) * A_DH])
    return jnp.concatenate(pieces, axis=1)


def _lane_rep(a, rep):
    return a if rep == 1 else jnp.concatenate([a] * rep, axis=1)


def _nsa_prompt_kernel(q_ref, qr_ref, small_ref, rows_ref, win_ref, wbd_ref, pe_ref, kg0_ref,
                       pool_ref, o_ref,
                       kraw_sc, vraw_sc, kc_sc, vct_sc, sel_sc, m_sc, acc_sc, s_sc, *, T, tq, kc_len):
    qi = pl.program_id(1)
    nseg = T // CMP_STRIDE
    nsb = T // SEL_LEN
    bpc = kc_len // SEL_LEN

    @pl.when(qi == 0)
    def _():
        kraw_sc[...] = rows_ref[:, 0:LANES]
        vraw_sc[...] = rows_ref[:, LANES:2 * LANES]
        kc, vc = _compress(kraw_sc, vraw_sc, nseg, wbd_ref, pe_ref, kg0_ref[...])
        kc_sc[...] = kc
        vct_sc[...] = jnp.transpose(vc)

    t0 = qi * tq
    tpos = t0 + lax.broadcasted_iota(jnp.int32, (1, tq), 1)
    tpos4 = _lane_rep(tpos, A_HPG)
    q = q_ref[...]
    qr = qr_ref[...]
    small_t = jnp.transpose(small_ref[...])
    kc_b = kc_sc[...].astype(BF16)
    vct_b = vct_sc[...].astype(BF16)
    bidx = lax.broadcasted_iota(jnp.int32, (nsb, tq), 0)
    cur = tpos // SEL_LEN
    vis = (lax.broadcasted_iota(jnp.int32, (nseg, 1), 0) * CMP_STRIDE + (CMP_LEN - 1)) <= tpos4
    qr_gs = [_stack_heads(qr, g) for g in range(A_KV)]
    o_cmps = []
    for g in range(A_KV):
        sm = jnp.where(vis, _bdot_t(kc_b, _stack_heads(q, g)), NEG_BIG)
        mx = jnp.max(sm, axis=0, keepdims=True)
        e = jnp.where(vis, jnp.exp2(sm - mx), 0.0)
        d = jnp.sum(e, axis=0, keepdims=True)
        p = e / jnp.where(d > 0, d, 1.0)
        o_cmps.append(jnp.dot(vct_b, p.astype(BF16), preferred_element_type=F32))
        imp = p[:, 0:tq]
        for hh in range(1, A_HPG):
            imp = imp + p[:, hh * tq:(hh + 1) * tq]
        ih, il = _split(imp)
        imp_t = (jnp.dot(pool_ref[...], ih, preferred_element_type=F32)
                 + jnp.dot(pool_ref[...], il, preferred_element_type=F32))[0:nsb]
        val = jnp.where(bidx < cur, imp_t, -1.0)
        rank = jnp.zeros((nsb, tq), F32)
        for bp in range(nsb):
            vb = val[bp:bp + 1, :]
            rank = rank + jnp.where(vb > val, 1.0, jnp.where((vb == val) & (bidx > bp), 1.0, 0.0))
        sel_sc[g] = jnp.where(((rank < (N_SEL - 1)) & (bidx < cur)) | (bidx == cur), 1.0, 0.0)

    m_sc[...] = jnp.full(m_sc.shape, M_INIT, F32)
    acc_sc[...] = jnp.zeros(acc_sc.shape, F32)

    def with_ones_row(vt_, g):
        vb = vt_.astype(BF16)
        r0, pad = (1 - g) * A_DH, 2 * SUBLANES
        ones = jnp.ones((pad, vb.shape[1]), BF16)
        return jnp.concatenate(([vb[0:r0]] if r0 else []) + [ones, vb[r0 + pad:]], axis=0)

    def sel_body(c, carry):
        k0 = pl.multiple_of(c * kc_len, kc_len)
        kb = rows_ref[pl.ds(k0, kc_len), 2 * LANES:3 * LANES].astype(BF16)
        vt = jnp.transpose(rows_ref[pl.ds(k0, kc_len), 3 * LANES:4 * LANES])
        causal = (k0 + lax.broadcasted_iota(jnp.int32, (kc_len, 1), 0)) <= tpos
        for g in range(A_KV):
            s_sc[g, 0:kc_len, :] = _bdot_t(kb, qr_gs[g])
        for g in range(A_KV):
            selc = sel_sc[g, pl.ds(pl.multiple_of(c * bpc, bpc), bpc), :]
            selx = jnp.concatenate([jnp.broadcast_to(selc[j:j + 1, :], (SEL_LEN, tq)) for j in range(bpc)], axis=0)
            bias = jnp.where(causal & (selx > 0.5), 0.0, NEG_BIG)
            sm = s_sc[g, 0:kc_len, :] + _lane_rep(bias, A_HPG)
            m_prev = m_sc[g]
            m_new = jnp.maximum(m_prev, jnp.max(sm, axis=0, keepdims=True))
            alpha = jnp.exp2(m_prev - m_new)
            p = jnp.exp2(sm - m_new)
            acc_sc[g] = alpha * acc_sc[g] + jnp.dot(with_ones_row(vt, g), p.astype(BF16),
                                                    preferred_element_type=F32)
            m_sc[g] = m_new
        return carry

    lax.fori_loop(0, (t0 + tq + kc_len - 1) // kc_len, sel_body, 0)

    wk = min(WINDOW + tq, T)
    w0 = pl.multiple_of(jnp.clip(t0 + tq - wk, 0, T - wk), tq)
    kw = win_ref[pl.ds(w0, wk), 0:LANES].astype(BF16)
    vwt = jnp.transpose(win_ref[pl.ds(w0, wk), LANES:2 * LANES])
    wdiff = tpos - (w0 + lax.broadcasted_iota(jnp.int32, (wk, 1), 0))
    wbias = _lane_rep(jnp.where((wdiff >= 0) & (wdiff < WINDOW), 0.0, NEG_BIG), A_HPG)

    def gate_row(g, br):
        cols = [2 * M_HEADS + (g * A_HPG + hh) * 3 + br for hh in range(A_HPG)]
        return jnp.concatenate([_sigmoid(small_t[c0:c0 + 1, :]) for c0 in cols], axis=1)

    for g in range(A_KV):
        s_sc[g, 0:wk, :] = _bdot_t(kw, qr_gs[g])
    o_ts = []
    for g in range(A_KV):
        den = (1 - g) * A_DH
        acc = acc_sc[g]
        l = acc[den:den + 1, :]
        o_sel = acc / jnp.where(l > 0, l, 1.0)
        sw = s_sc[g, 0:wk, :] + wbias
        pw = jnp.exp2(sw - jnp.max(sw, axis=0, keepdims=True))
        ow = jnp.dot(with_ones_row(vwt, g), pw.astype(BF16), preferred_element_type=F32)
        o_win = ow / ow[den:den + 1, :]
        o_ts.append(gate_row(g, 0) * o_cmps[g] + gate_row(g, 1) * o_sel + gate_row(g, 2) * o_win)
    for j in range(A_HEADS // 2):
        g, h0 = j // (A_HPG // 2), 2 * (j % (A_HPG // 2))
        og = o_ts[g][g * A_DH:(g + 1) * A_DH, :]
        pair = jnp.concatenate([og[:, h0 * tq:(h0 + 1) * tq], og[:, (h0 + 1) * tq:(h0 + 2) * tq]], axis=0)
        o_ref[:, j * LANES:(j + 1) * LANES] = jnp.transpose(pair)


def _nsa_prompt(q, qr, small, rows, win, wbd, pe, kg0, nb, T):
    tq = 128
    kc_len = _pick_tile(T, 512)
    nq = T // tq
    nseg = T // CMP_STRIDE
    nsb = T // SEL_LEN
    nsb_p = -(-nsb // SUBLANES) * SUBLANES
    pool = (jnp.arange(nsb_p)[:, None] == jnp.arange(nseg)[None, :] // (SEL_LEN // CMP_STRIDE)).astype(BF16)
    tile = lambda b, i: (b * nq + i, 0)
    per_b = lambda b, i: (b, 0)
    c2 = lambda b, i: (0, 0)
    c3 = lambda b, i: (0, 0, 0)
    c4 = A_HPG * tq
    return pl.pallas_call(
        functools.partial(_nsa_prompt_kernel, T=T, tq=tq, kc_len=kc_len),
        grid=(nb, nq),
        in_specs=[pl.BlockSpec((tq, A_WIDTH), tile), pl.BlockSpec((tq, A_WIDTH), tile),
                  pl.BlockSpec((tq, LANES), tile),
                  pl.BlockSpec((T, 4 * LANES), per_b), pl.BlockSpec((T, 2 * LANES), per_b),
                  pl.BlockSpec(wbd.shape, c3), pl.BlockSpec(pe.shape, c3), pl.BlockSpec(kg0.shape, c2),
                  pl.BlockSpec(pool.shape, c2)],
        out_specs=pl.BlockSpec((tq, A_WIDTH), tile),
        out_shape=jax.ShapeDtypeStruct((nb * T, A_WIDTH), F32),
        scratch_shapes=[pltpu.VMEM((T, LANES), F32), pltpu.VMEM((T, LANES), F32),
                        pltpu.VMEM((nseg, LANES), F32), pltpu.VMEM((LANES, nseg), F32),
                        pltpu.VMEM((A_KV, nsb, tq), F32),
                        pltpu.VMEM((A_KV, 1, c4), F32),
                        pltpu.VMEM((A_KV, LANES, c4), F32),
                        pltpu.VMEM((A_KV, max(kc_len, min(WINDOW + tq, T)), c4), F32)],
        compiler_params=_cparams(("parallel", "arbitrary")),
        name="nsa_prompt",
    )(q, qr, small, rows, win, wbd, pe, kg0, pool)


def _nsa_sample_kernel(pt_ref, cache_ref, q_ref, qr_ref, small_ref, rows_ref, winnew_ref, winbuf_ref,
                       wbd_ref, pe_ref, kg0_ref, pool_ref, expand_ref,
                       o_ref, winout_ref,
                       cmp_buf, sel_buf, xperm_sc, sems, *, n_pages, past_len, t_valid):
    b = pl.program_id(0)
    nb = pl.num_programs(0)
    tp = SAMPLE_PAD_T
    nseg = past_len // CMP_STRIDE
    nsb = past_len // SEL_LEN
    wbuf = winbuf_ref.shape[1]

    def page_copies(bb, p, phase):
        page = pt_ref[bb * n_pages + p]
        dst_lanes = pl.ds(pl.multiple_of(p * PAGE_SIZE, PAGE_SIZE), PAGE_SIZE)
        if phase == 0:
            return [pltpu.make_async_copy(cache_ref.at[page, pl.ds(0, 2 * LANES), :],
                                          cmp_buf.at[:, dst_lanes], sems.at[0])]
        return [pltpu.make_async_copy(cache_ref.at[page, pl.ds(2 * LANES, 2 * LANES), :],
                                      sel_buf.at[:, dst_lanes], sems.at[1])]

    def start_all(bb, phase):
        def body(p, c):
            for cp in page_copies(bb, p, phase):
                cp.start()
            return c
        lax.fori_loop(0, n_pages, body, 0)

    def wait_all(bb, phase):
        def body(p, c):
            for cp in page_copies(bb, p, phase):
                cp.wait()
            return c
        lax.fori_loop(0, n_pages, body, 0)

    @pl.when(b == 0)
    def _():
        start_all(b, 0)

    start_all(b, 1)
    wait_all(b, 0)

    seg_pp = PAGE_SIZE // CMP_STRIDE
    pr = lax.broadcasted_iota(jnp.int32, (PAGE_SIZE, PAGE_SIZE), 0)
    pc = lax.broadcasted_iota(jnp.int32, (PAGE_SIZE, PAGE_SIZE), 1)
    perm = jnp.where(pc == CMP_STRIDE * (pr % seg_pp) + pr // seg_pp, 1.0, 0.0).astype(BF16)
    for p in range(n_pages):
        xp = _bdot_t(perm, cmp_buf[:, p * PAGE_SIZE:(p + 1) * PAGE_SIZE])
        for l in range(CMP_STRIDE):
            xperm_sc[l, p * seg_pp:(p + 1) * seg_pp, :] = xp[l * seg_pp:(l + 1) * seg_pp, :]
    kc, vc = _compress_grouped(xperm_sc, nseg, wbd_ref, pe_ref, kg0_ref[...])
    kc_b = kc.astype(BF16)
    vc_b = vc.astype(BF16)
    q = q_ref[...]
    qr = qr_ref[...]
    small = small_ref[...]
    tpos_col = past_len + lax.broadcasted_iota(jnp.int32, (tp, 1), 0)
    tpos_rows = jnp.concatenate([tpos_col] * A_HPG, axis=0)
    bp_idx = lax.broadcasted_iota(jnp.int32, (nsb, nsb), 0)
    b_idx = lax.broadcasted_iota(jnp.int32, (nsb, nsb), 1)
    o_cmps = []
    sels = []
    for g in range(A_KV):
        qn_g = _stack_heads(q, g)
        o_cmp, imp = _cmp_branch(qn_g, kc_b, vc_b, tpos_rows, nseg, tp)
        o_cmps.append(o_cmp)
        imp_sel = _dot2_exact_rhs(imp, pool_ref[...])
        imp_pad = jnp.concatenate([imp_sel, jnp.zeros((nsb - tp, nsb), F32)], axis=0)
        imp_t = jnp.transpose(imp_pad)
        rows_sel = []
        for t in range(tp):
            if t < t_valid:
                row_t = imp_sel[t:t + 1, :]
                col_t = imp_t[:, t:t + 1]
                ahead = jnp.where(col_t > row_t, 1.0, jnp.where((col_t == row_t) & (bp_idx < b_idx), 1.0, 0.0))
                rank = jnp.sum(ahead, axis=0, keepdims=True)
                rows_sel.append(jnp.where(rank < (N_SEL - 1), 1.0, 0.0))
            else:
                rows_sel.append(jnp.zeros((1, nsb), F32))
        sels.append(jnp.concatenate(rows_sel, axis=0))

    @pl.when(b + 1 < nb)
    def _():
        start_all(b + 1, 0)

    wait_all(b, 1)

    new_idx = lax.broadcasted_iota(jnp.int32, (tp, tp), 1)
    tok_idx = lax.broadcasted_iota(jnp.int32, (tp, tp), 0)
    new_ok = jnp.concatenate([jnp.where(new_idx <= tok_idx, 1.0, 0.0)] * A_HEADS, axis=0) > 0.5
    wpos = past_len - wbuf + lax.broadcasted_iota(jnp.int32, (1, wbuf), 1)
    wdiff = tpos_col - wpos
    win_ok = jnp.concatenate([jnp.where((wdiff >= 0) & (wdiff < WINDOW), 1.0, 0.0)] * A_HEADS, axis=0) > 0.5
    k_past = sel_buf[0:LANES, :].astype(BF16)
    v_past = sel_buf[LANES:2 * LANES, :].astype(BF16)
    k_new = rows_ref[:, 2 * LANES:3 * LANES]
    v_new = rows_ref[:, 3 * LANES:4 * LANES]
    kw_past = winbuf_ref[0:LANES, :]
    vw_past = winbuf_ref[LANES:2 * LANES, :]
    kw_new = winnew_ref[:, 0:LANES]
    vw_new = winnew_ref[:, LANES:2 * LANES]
    r4 = A_HPG * tp
    qr_all = jnp.concatenate([_stack_heads(qr, g) for g in range(A_KV)], axis=0)
    mk = jnp.dot(jnp.concatenate(sels, axis=0).astype(BF16), expand_ref[...],
                 preferred_element_type=F32)
    past_ok = jnp.concatenate([mk[g * tp:(g + 1) * tp] for g in range(A_KV) for _ in range(A_HPG)], axis=0) > 0.5
    o_sel = _masked_attn_direct(qr_all, [k_past, k_new], [v_past, v_new], [past_ok, new_ok], [True, False])
    o_win = _masked_attn_direct(qr_all, [kw_past, kw_new], [vw_past, vw_new], [win_ok, new_ok], [True, False])
    o_groups = []
    for g in range(A_KV):
        rs = slice(g * r4, (g + 1) * r4)
        o_groups.append(_gate_cols(small, g, 0) * o_cmps[g] + _gate_cols(small, g, 1) * o_sel[rs]
                        + _gate_cols(small, g, 2) * o_win[rs])
    o_ref[...] = _assemble_heads(o_groups, tp)

    rolled = pltpu.roll(winbuf_ref[...], wbuf - t_valid, 1)
    new_t = jnp.transpose(jnp.concatenate([winnew_ref[...], jnp.zeros((LANES - tp, 2 * LANES), F32)], axis=0))
    new_t = pltpu.roll(new_t, LANES - t_valid, 1)
    lane = lax.broadcasted_iota(jnp.int32, (2 * LANES, LANES), 1)
    winout_ref[:, 0:wbuf - LANES] = rolled[:, 0:wbuf - LANES]
    winout_ref[:, wbuf - LANES:wbuf] = jnp.where(lane < LANES - t_valid, rolled[:, wbuf - LANES:wbuf], new_t)


def _nsa_sample(page_table, cache, q, qr, small, rows, winnew, winbuf, wbd, pe, kg0, t_valid):
    nb, n_pages = page_table.shape
    past_len = n_pages * PAGE_SIZE
    nseg = past_len // CMP_STRIDE
    nsb = past_len // SEL_LEN
    tp = SAMPLE_PAD_T
    wbuf = winbuf.shape[2]
    pool = (jnp.arange(nseg)[:, None] // (SEL_LEN // CMP_STRIDE) == jnp.arange(nsb)[None, :]).astype(BF16)
    expand = (jnp.arange(nsb)[:, None] == jnp.arange(past_len)[None, :] // SEL_LEN).astype(BF16)
    tile = lambda b, pt: (b, 0)
    c2 = lambda b, pt: (0, 0)
    c3 = lambda b, pt: (0, 0, 0)
    gs = pltpu.PrefetchScalarGridSpec(
        num_scalar_prefetch=1,
        grid=(nb,),
        in_specs=[pl.BlockSpec(memory_space=pl.ANY),
                  pl.BlockSpec((tp, A_WIDTH), tile), pl.BlockSpec((tp, A_WIDTH), tile),
                  pl.BlockSpec((tp, LANES), tile), pl.BlockSpec((tp, 4 * LANES), tile),
                  pl.BlockSpec((tp, 2 * LANES), tile),
                  pl.BlockSpec((None, 2 * LANES, wbuf), lambda b, pt: (b, 0, 0)),
                  pl.BlockSpec(wbd.shape, c3), pl.BlockSpec(pe.shape, c3), pl.BlockSpec(kg0.shape, c2),
                  pl.BlockSpec(pool.shape, c2), pl.BlockSpec(expand.shape, c2)],
        out_specs=[pl.BlockSpec((tp, A_WIDTH), tile),
                   pl.BlockSpec((None, 2 * LANES, wbuf), lambda b, pt: (b, 0, 0))],
        scratch_shapes=[pltpu.VMEM((2 * LANES, past_len), F32), pltpu.VMEM((2 * LANES, past_len), F32),
                        pltpu.VMEM((CMP_STRIDE, past_len // CMP_STRIDE, 2 * LANES), F32),
                        pltpu.SemaphoreType.DMA((2,))],
    )
    return pl.pallas_call(
        functools.partial(_nsa_sample_kernel, n_pages=n_pages, past_len=past_len, t_valid=t_valid),
        grid_spec=gs,
        out_shape=[jax.ShapeDtypeStruct((nb * tp, A_WIDTH), F32),
                   jax.ShapeDtypeStruct((nb, 2 * LANES, wbuf), F32)],
        compiler_params=_cparams(("arbitrary",)),
        name="nsa_sample",
    )(page_table.reshape(-1), cache, q, qr, small, rows, winnew, winbuf, wbd, pe, kg0, pool, expand)


MOE_TM = 256
SEG_ALIGN = 8
MOE_RL = -(-(MOE_TM * TOP_K + N_EXPERTS * (SEG_ALIGN - 1)) // LANES) * LANES


def _pack_halves(x, bf16_exact=False):
    w = x.shape[1] // 2
    bits = lax.bitcast_convert_type(x if bf16_exact else x.astype(BF16).astype(F32), jnp.uint32)
    return bits[:, :w] | (bits[:, w:] >> 16)


def _unpack_halves(u):
    hi = lax.bitcast_convert_type(u & jnp.uint32(0xFFFF0000), F32).astype(BF16)
    lo = lax.bitcast_convert_type(u << 16, F32).astype(BF16)
    return hi, lo


def _route_and_sort(h2, wrt_ref, brt_ref, xsl_ref, info_ref, cnt_ref, tm, t_mod, t_valid, m_valid):
    ne = N_EXPERTS
    h2b = h2.astype(BF16)
    h2l = (h2 - h2b.astype(F32)).astype(BF16)
    wh, wl = _split(wrt_ref[...])
    lt = _bdot_t(wh, h2b) + _bdot_t(wl, h2b) + _bdot_t(wh, h2l) + brt_ref[...]
    eidx = lax.broadcasted_iota(jnp.int32, (ne, tm), 0)
    rank = jnp.zeros((ne, tm), F32)
    for ep in range(ne):
        v = lt[ep:ep + 1, :]
        rank = rank + jnp.where(v > lt, 1.0, jnp.where((v == lt) & (eidx > ep), 1.0, 0.0))
    sel = rank < TOP_K
    if t_mod is not None:
        tok = pl.program_id(0) * tm + lax.broadcasted_iota(jnp.int32, (1, tm), 1)
        sel = sel & ((tok % t_mod) < t_valid) & (tok < m_valid)
    mx = jnp.max(jnp.where(sel, lt, NEG_BIG), axis=0, keepdims=True)
    ex = jnp.where(sel, jnp.exp(lt - mx), 0.0)
    den = jnp.sum(ex, axis=0, keepdims=True)
    gate = ex / jnp.where(den > 0, den, 1.0)
    self_ = jnp.where(sel, 1.0, 0.0)
    selb = self_.astype(BF16)
    er = lax.broadcasted_iota(jnp.int32, (ne, ne), 0)
    ec = lax.broadcasted_iota(jnp.int32, (ne, ne), 1)
    c = jnp.dot(jnp.where(ec <= er, 1.0, 0.0).astype(BF16), selb, preferred_element_type=F32)
    tr = lax.broadcasted_iota(jnp.int32, (tm, tm), 0)
    tc = lax.broadcasted_iota(jnp.int32, (tm, tm), 1)
    rk = jnp.dot(selb, jnp.where(tr < tc, 1.0, 0.0).astype(BF16), preferred_element_type=F32)
    cnt = jnp.sum(self_, axis=1, keepdims=True)
    cnt_al = jnp.floor((cnt + (SEG_ALIGN - 1)) * (1.0 / SEG_ALIGN)) * SEG_ALIGN
    cnt_b = jnp.broadcast_to(cnt_al, (ne, LANES))
    cnt_ref[...] = cnt_b
    off = jnp.dot(jnp.where(ec < er, 1.0, 0.0).astype(BF16), cnt_b.astype(BF16), preferred_element_type=F32)
    rowidx = off[:, 0:1] + rk
    rows_k, gates_k, exps_k = [], [], []
    for k in range(1, TOP_K + 1):
        mk = sel & (c == k)
        has = jnp.sum(jnp.where(mk, 1.0, 0.0), axis=0, keepdims=True)
        rows_k.append(jnp.sum(jnp.where(mk, rowidx, 0.0), axis=0, keepdims=True) + has - 1.0)
        gates_k.append(jnp.sum(jnp.where(mk, gate, 0.0), axis=0, keepdims=True))
        exps_k.append(jnp.sum(jnp.where(mk, eidx.astype(F32), 0.0), axis=0, keepdims=True))
    info_ref[...] = jnp.concatenate(rows_k + gates_k + exps_k + [jnp.zeros((4, tm), F32)], axis=0)
    ridx = lax.broadcasted_iota(jnp.int32, (MOE_RL, tm), 0).astype(F32)
    perm = jnp.zeros((MOE_RL, tm), F32)
    for k in range(TOP_K):
        perm = jnp.where(ridx == rows_k[k], 1.0, perm)
    xs = jnp.dot(perm.astype(BF16), h2b, preferred_element_type=F32)
    xsl_ref[...] = _pack_halves(xs, bf16_exact=True)


def _mixout_kernel(x_ref, hm_ref, on_ref, mod_ref, gmix_ref, gffn_ref,
                   wog_ref, bog_ref, wum_ref, wua_ref, wout_ref, wrt_ref, brt_ref,
                   x1_ref, xsl_ref, info_ref, cnt_ref, *, tm, t_mod, t_valid, m_valid, n_real):
    if n_real is not None:
        @pl.when(pl.program_id(0) >= n_real)
        def _():
            xsl_ref[...] = jnp.zeros(xsl_ref.shape, jnp.uint32)
            info_ref[...] = jnp.zeros(info_ref.shape, F32)
            cnt_ref[...] = jnp.zeros(cnt_ref.shape, F32)

        @pl.when(pl.program_id(0) < n_real)
        def _():
            _mixout_body(x_ref, hm_ref, on_ref, mod_ref, gmix_ref, gffn_ref, wog_ref, bog_ref, wum_ref,
                         wua_ref, wout_ref, wrt_ref, brt_ref, x1_ref, xsl_ref, info_ref, cnt_ref,
                         tm, t_mod, t_valid, m_valid)
    else:
        _mixout_body(x_ref, hm_ref, on_ref, mod_ref, gmix_ref, gffn_ref, wog_ref, bog_ref, wum_ref,
                     wua_ref, wout_ref, wrt_ref, brt_ref, x1_ref, xsl_ref, info_ref, cnt_ref,
                     tm, t_mod, t_valid, m_valid)


def _mixout_body(x_ref, hm_ref, on_ref, mod_ref, gmix_ref, gffn_ref,
                 wog_ref, bog_ref, wum_ref, wua_ref, wout_ref, wrt_ref, brt_ref,
                 x1_ref, xsl_ref, info_ref, cnt_ref, tm, t_mod, t_valid, m_valid):
    d = D_MODEL
    x = x_ref[...]
    sh1, sc1, gt1 = mod_ref[:, 0:d], mod_ref[:, d:2 * d], mod_ref[:, 2 * d:3 * d]
    sh2, sc2 = mod_ref[:, 3 * d:4 * d], mod_ref[:, 4 * d:5 * d]
    h = _rmsnorm_rows(x, gmix_ref[...]) * (1.0 + sc1) + sh1
    hb = h.astype(BF16)
    mo = jnp.dot(hb, wog_ref[:, 0:M_WIDTH], preferred_element_type=F32) + bog_ref[:, 0:M_WIDTH]
    ym = _bdot(_sigmoid(mo) * hm_ref[...], wum_ref[...])
    ya = _bdot(on_ref[...], wua_ref[...])
    ga = jnp.dot(hb, wog_ref[:, M_WIDTH:M_WIDTH + d], preferred_element_type=F32) + bog_ref[:, M_WIDTH:M_WIDTH + d]
    u = _sigmoid(ga) * ym
    gb = (jnp.dot(hb, wog_ref[:, M_WIDTH + d:M_WIDTH + 2 * d], preferred_element_type=F32)
          + bog_ref[:, M_WIDTH + d:M_WIDTH + 2 * d])
    u = u + _sigmoid(gb) * ya
    x1 = x + gt1 * _bdot(u, wout_ref[...])
    x1_ref[...] = x1
    h2 = _rmsnorm_rows(x1, gffn_ref[...]) * (1.0 + sc2) + sh2
    _route_and_sort(h2, wrt_ref, brt_ref, xsl_ref, info_ref, cnt_ref, tm, t_mod, t_valid, m_valid)


def _mixout_with_shared(*refs, n_shared, **kw):
    n_in = 13
    _mixout_kernel(*refs[:n_in], *refs[n_in + n_shared:], **kw)


def _mixout(x2, hm, on, mod3, gmix, gffn, wts, tiles_per_mod, nt_total, tile0=0, shared=None,
            t_mod=None, t_valid=None, m_valid=None):
    m = x2.shape[0]
    tm = MOE_TM
    nt = m // tm
    (wog, bog, wum, wua, wout, wr, br) = wts
    r = mod3.shape[1]
    n_extra = nt_total - tile0 - nt if shared is None else 0
    row = lambda i: (jnp.minimum(i, nt - 1), 0)
    const = lambda i: (0, 0)
    in_specs = [pl.BlockSpec((tm, D_MODEL), row), pl.BlockSpec((tm, M_WIDTH), row),
                pl.BlockSpec((tm, A_WIDTH), row),
                pl.BlockSpec((None, r, 6 * D_MODEL), lambda i: (jnp.minimum(i, nt - 1) // tiles_per_mod, 0, 0)),
                pl.BlockSpec((1, D_MODEL), const), pl.BlockSpec((1, D_MODEL), const),
                pl.BlockSpec(wog.shape, const), pl.BlockSpec(bog.shape, const),
                pl.BlockSpec(wum.shape, const), pl.BlockSpec(wua.shape, const),
                pl.BlockSpec(wout.shape, const), pl.BlockSpec(wr.shape, const),
                pl.BlockSpec(br.shape, const)]
    args = [x2, hm, on, mod3, gmix, gffn, wog, bog, wum, wua, wout, wr, br]
    kw = dict(tm=tm, t_mod=t_mod, t_valid=t_valid, m_valid=m_valid, n_real=nt if n_extra else None)
    body = functools.partial(_mixout_kernel, **kw)
    aliases = {}
    if shared is not None:
        in_specs += [pl.BlockSpec(memory_space=pl.ANY)] * len(shared)
        aliases = {len(args) + j: 1 + j for j in range(len(shared))}
        args += list(shared)
        body = functools.partial(_mixout_with_shared, n_shared=len(shared), **kw)
    return pl.pallas_call(
        body,
        grid=(nt + n_extra,),
        in_specs=in_specs,
        out_specs=[pl.BlockSpec((tm, D_MODEL), row),
                   pl.BlockSpec((MOE_RL, D_MODEL // 2), lambda i: (tile0 + i, 0)),
                   pl.BlockSpec((16, tm), lambda i: (0, tile0 + i)),
                   pl.BlockSpec((None, N_EXPERTS, LANES), lambda i: (tile0 + i, 0, 0))],
        out_shape=[jax.ShapeDtypeStruct((m, D_MODEL), F32),
                   jax.ShapeDtypeStruct((nt_total * MOE_RL, D_MODEL // 2), jnp.uint32),
                   jax.ShapeDtypeStruct((16, nt_total * tm), F32),
                   jax.ShapeDtypeStruct((nt_total, N_EXPERTS, LANES), F32)],
        input_output_aliases=aliases,
        compiler_params=_cparams(("arbitrary" if n_extra else "parallel",)),
        name="mixout",
    )(*args)


MOE_BM = 256
MOE_CH = 512


def _moe_kernel(be_ref, na_ref, grp_ref, first_ref, wslot_ref, nxt_ref,
                xsl_ref, wgu_ref, bgu_ref, wdn_ref, bdn_ref, ysl_ref,
                wgu_bf, wdn_bf, xbuf, ybuf, wgu_f, wdn_f, sem_in, sem_out, sem_w, *, trash_row0):
    i = pl.program_id(0)
    na = na_ref[0]
    e = be_ref[i]
    n_grp = MOE_BM // SEG_ALIGN

    def weight_copies(ex, s):
        return [pltpu.make_async_copy(wgu_ref.at[ex], wgu_f.at[s], sem_w.at[s]),
                pltpu.make_async_copy(wdn_ref.at[ex], wdn_f.at[s], sem_w.at[s])]

    def group_copies(blk, inbound, slot=None):
        slot = blk % 2 if slot is None else slot
        cps = []
        for r in range(n_grp):
            v = grp_ref[blk * n_grp + r]
            vm_rows = pl.ds(r * SEG_ALIGN, SEG_ALIGN)
            if inbound:
                row = pl.multiple_of(jnp.where(v >= 0, v, trash_row0 + 2 * MOE_BM), SEG_ALIGN)
                cps.append(pltpu.make_async_copy(xsl_ref.at[pl.ds(row, SEG_ALIGN), :],
                                                 xbuf.at[slot, vm_rows, :], sem_in.at[slot]))
            else:
                spare = trash_row0 + slot * MOE_BM + r * SEG_ALIGN
                row = pl.multiple_of(jnp.where(v >= 0, v, spare), SEG_ALIGN)
                cps.append(pltpu.make_async_copy(ybuf.at[slot, vm_rows, :],
                                                 ysl_ref.at[pl.ds(row, SEG_ALIGN), :], sem_out.at[slot]))
        return cps

    def start_gather(blk):
        for cp in group_copies(blk, True):
            cp.start()

    def start_scatter(blk):
        for cp in group_copies(blk, False):
            cp.start()

    def wait_rows(blk, sem, inbound):
        slot = blk % 2
        if inbound:
            pltpu.make_async_copy(xsl_ref.at[pl.ds(0, MOE_BM), :], xbuf.at[slot], sem.at[slot]).wait()
        else:
            pltpu.make_async_copy(ybuf.at[slot], ysl_ref.at[pl.ds(0, MOE_BM), :], sem.at[slot]).wait()

    @pl.when(i == 0)
    def _():
        start_gather(i)
        for cp in weight_copies(e, 0):
            cp.start()

    @pl.when(i + 1 < na)
    def _():
        start_gather(i + 1)

    @pl.when((i < na) & (first_ref[i] != 0))
    def _():
        s = wslot_ref[i]
        for cp in weight_copies(e, s):
            cp.wait()
        nx = nxt_ref[i]

        @pl.when(nx >= 0)
        def _():
            for cp in weight_copies(nx, 1 - s):
                cp.start()

        for j in range(2 * D_EXPERT // MOE_CH):
            wgu_bf[:, j * MOE_CH:(j + 1) * MOE_CH] = wgu_f[s, :, j * MOE_CH:(j + 1) * MOE_CH].astype(BF16)
        for j in range(D_EXPERT // MOE_CH):
            wdn_bf[j * MOE_CH:(j + 1) * MOE_CH, :] = wdn_f[s, j * MOE_CH:(j + 1) * MOE_CH, :].astype(BF16)

    @pl.when(i < na)
    def _():
        slot = i % 2
        wait_rows(i, sem_in, True)

        @pl.when(i >= 2)
        def _():
            wait_rows(i - 2, sem_out, False)

        half = D_MODEL // 2
        xh, xl = _unpack_halves(xbuf[slot])

        def xdot(c0, c1):
            return (jnp.dot(xh, wgu_bf[0:half, c0:c1], preferred_element_type=F32)
                    + jnp.dot(xl, wgu_bf[half:D_MODEL, c0:c1], preferred_element_type=F32))

        acc = jnp.zeros((MOE_BM, D_MODEL), F32) + bdn_ref[...]
        for j in range(D_EXPERT // MOE_CH):
            lo, hi = j * MOE_CH, (j + 1) * MOE_CH
            gj = xdot(lo, hi) + bgu_ref[:, lo:hi]
            uj = xdot(D_EXPERT + lo, D_EXPERT + hi) + bgu_ref[:, D_EXPERT + lo:D_EXPERT + hi]
            gj = jnp.minimum(gj, SWIGLU_LIMIT)
            uj = jnp.clip(uj, -SWIGLU_LIMIT, SWIGLU_LIMIT)
            act = gj * _sigmoid(SWIGLU_ALPHA * gj) * (uj + 1.0)
            acc = acc + jnp.dot(act.astype(BF16), wdn_bf[lo:hi, :], preferred_element_type=F32)
        ybuf[slot] = _pack_halves(acc)
        start_scatter(i)

        @pl.when(i == na - 1)
        def _():
            @pl.when(i >= 1)
            def _():
                wait_rows(i - 1, sem_out, False)
            wait_rows(i, sem_out, False)


def _moe_experts(plan, xsl, w_gu, b_gu, w_dn, b_dn):
    block_e = plan[0]
    nblk = block_e.shape[0]
    spare_row0 = xsl.shape[0] - MOE_RL
    assert MOE_RL >= 2 * MOE_BM
    wmap = lambda i, be, *_: (be[i], 0, 0)
    anyspec = pl.BlockSpec(memory_space=pl.ANY)
    gs = pltpu.PrefetchScalarGridSpec(
        num_scalar_prefetch=len(plan),
        grid=(nblk,),
        in_specs=[anyspec,
                  anyspec,
                  pl.BlockSpec((None, 1, 2 * D_EXPERT), wmap),
                  anyspec,
                  pl.BlockSpec((None, 1, D_MODEL), wmap)],
        out_specs=anyspec,
        scratch_shapes=[pltpu.VMEM((D_MODEL, 2 * D_EXPERT), BF16), pltpu.VMEM((D_EXPERT, D_MODEL), BF16),
                        pltpu.VMEM((2, MOE_BM, D_MODEL // 2), jnp.uint32),
                        pltpu.VMEM((2, MOE_BM, D_MODEL // 2), jnp.uint32),
                        pltpu.VMEM((2, D_MODEL, 2 * D_EXPERT), F32), pltpu.VMEM((2, D_EXPERT, D_MODEL), F32),
                        pltpu.SemaphoreType.DMA((2,)), pltpu.SemaphoreType.DMA((2,)),
                        pltpu.SemaphoreType.DMA((2,))],
    )
    return pl.pallas_call(
        functools.partial(_moe_kernel, trash_row0=spare_row0),
        grid_spec=gs,
        out_shape=jax.ShapeDtypeStruct(xsl.shape, jnp.uint32),
        input_output_aliases={len(plan): 0},
        compiler_params=_cparams(("arbitrary",)),
        name="moe_experts",
    )(*plan, xsl, w_gu, b_gu.reshape(N_EXPERTS, 1, -1), w_dn, b_dn.reshape(N_EXPERTS, 1, -1))


def _combine_kernel(ysl_ref, info_ref, x1_ref, mod_ref, y_ref, *, tm):
    info = info_ref[...]
    info_t = jnp.transpose(jnp.concatenate([info, jnp.zeros((LANES - info.shape[0], tm), F32)], axis=0))
    ridx = lax.broadcasted_iota(jnp.int32, (tm, MOE_RL), 1).astype(F32)
    pg = jnp.zeros((tm, MOE_RL), F32)
    for k in range(TOP_K):
        pg = jnp.where(ridx == info_t[:, k:k + 1], info_t[:, TOP_K + k:TOP_K + k + 1], pg)
    pgb = pg.astype(BF16)
    yh, yl = _unpack_halves(ysl_ref[...])
    half = D_MODEL // 2
    gt2 = mod_ref[:, 5 * D_MODEL:6 * D_MODEL]
    for c, yy in ((0, yh), (1, yl)):
        moe = jnp.dot(pgb, yy, preferred_element_type=F32)
        y_ref[:, c * half:(c + 1) * half] = (x1_ref[:, c * half:(c + 1) * half]
                                             + gt2[:, c * half:(c + 1) * half] * moe)


def _combine(ysl, info, x1, mod3, tiles_per_mod, tile0=0):
    m = x1.shape[0]
    tm = MOE_TM
    r = mod3.shape[1]
    return pl.pallas_call(
        functools.partial(_combine_kernel, tm=tm),
        grid=(m // tm,),
        in_specs=[pl.BlockSpec((MOE_RL, D_MODEL // 2), lambda i: (tile0 + i, 0)),
                  pl.BlockSpec((16, tm), lambda i: (0, tile0 + i)),
                  pl.BlockSpec((tm, D_MODEL), lambda i: (i, 0)),
                  pl.BlockSpec((None, r, 6 * D_MODEL), lambda i: (i // tiles_per_mod, 0, 0))],
        out_specs=pl.BlockSpec((tm, D_MODEL), lambda i: (i, 0)),
        out_shape=jax.ShapeDtypeStruct((m, D_MODEL), F32),
        compiler_params=_cparams(("parallel",)),
        name="moe_combine",
    )(ysl, info, x1, mod3)


def _moe_plan(cnt):
    cnt = cnt.astype(jnp.int32)
    nt = cnt.shape[0]
    so = jnp.cumsum(cnt, axis=1) - cnt + (jnp.arange(nt) * MOE_RL)[:, None]
    ce = jnp.cumsum(cnt, axis=0)
    cs = ce - cnt
    tot = ce[-1]
    nblk_e = (tot + MOE_BM - 1) // MOE_BM
    blk_end = jnp.cumsum(nblk_e)
    max_rows = nt * MOE_TM * TOP_K + nt * N_EXPERTS * (SEG_ALIGN - 1)
    n_blocks = -(-max_rows // MOE_BM) + N_EXPERTS
    bidx = jnp.arange(n_blocks)
    block_e = jnp.minimum(jnp.sum(blk_end[None, :] <= bidx[:, None], axis=1), N_EXPERTS - 1).astype(jnp.int32)
    is_e = (jnp.arange(N_EXPERTS)[:, None] == block_e[None, :]).astype(jnp.int32)
    per_block = lambda a: jnp.sum(a[..., :, None] * is_e, axis=-2)
    block_r0 = (bidx - per_block(blk_end - nblk_e)) * MOE_BM
    x = block_r0[:, None] + jnp.arange(MOE_BM // SEG_ALIGN)[None, :] * SEG_ALIGN
    ce_b = per_block(ce)[:, :, None]
    cs_b = per_block(cs)[:, :, None]
    inside = (cs_b <= x[None]) & (x[None] < ce_b)
    grp = x + jnp.sum(jnp.where(inside, per_block(so - cs)[:, :, None], 0), axis=0)
    grp = jnp.where(x < per_block(tot)[:, None], grp, -1)
    n_active = blk_end[-1].reshape(1)
    used = nblk_e > 0
    first = (bidx == per_block(blk_end - nblk_e)) & (bidx < n_active[0])
    wslot = per_block(jnp.cumsum(used) - 1) % 2
    eidx = jnp.arange(N_EXPERTS)
    later_used = used[None, :] & (eidx[None, :] > eidx[:, None])
    nxt_e = jnp.min(jnp.where(later_used, eidx[None, :], N_EXPERTS), axis=1)
    nxt = per_block(jnp.where(nxt_e < N_EXPERTS, nxt_e, -1))
    i32 = lambda a: a.reshape(-1).astype(jnp.int32)
    return block_e, i32(n_active), i32(grp), i32(first), i32(wslot), i32(nxt)


def _rope_tables(pos):
    half = ROT_DIM // 2
    inv = ROPE_THETA ** (-jnp.arange(half, dtype=F32) * (2.0 / ROT_DIM))
    ang = pos.astype(F32)[:, None] * inv[None, :]
    cos, sin = jnp.cos(ang), jnp.sin(ang)
    n = pos.shape[0]
    ones = jnp.ones((n, A_DH - ROT_DIM), F32)
    zeros_h = jnp.zeros((n, half), F32)
    zeros_r = jnp.zeros((n, A_DH - ROT_DIM), F32)
    cos64 = jnp.concatenate([cos, cos, ones], axis=1)
    sprev64 = jnp.concatenate([zeros_h, sin, zeros_r], axis=1)
    snext64 = jnp.concatenate([-sin, zeros_h, zeros_r], axis=1)
    two = lambda a: jnp.concatenate([a, a], axis=1)
    return two(cos64), two(sprev64), two(snext64)


def _prep_weights(w_in, b_in, q_norm_g, k_norm_g, cmp_pe_k, cmp_pe_v, cmp_w_k, cmp_w_v,
                  w_up_m, w_up_a, w_out, w_router, b_router):
    b2 = b_in.reshape(1, N_IN)
    wm = w_in[:, OFF_MQ:OFF_MO].astype(BF16)
    bm = b2[:, OFF_MQ:OFF_MO]
    wq = w_in[:, OFF_AQ:OFF_AKV].astype(BF16)
    bq = b2[:, OFF_AQ:OFF_AKV]
    wkv = w_in[:, OFF_AKV:OFF_AG].astype(BF16)
    bkv = b2[:, OFF_AKV:OFF_AG]
    n_small = 2 * M_HEADS + 3 * A_HEADS
    ws = jnp.concatenate([w_in[:, OFF_MI:OFF_AQ], w_in[:, OFF_AG:OFF_GA],
                          jnp.zeros((D_MODEL, LANES - n_small), F32)], axis=1)
    bs = jnp.concatenate([b2[:, OFF_MI:OFF_AQ], b2[:, OFF_AG:OFF_GA], jnp.zeros((1, LANES - n_small), F32)], axis=1)
    qg = jnp.tile(q_norm_g, A_HEADS).reshape(1, A_WIDTH)
    kg = jnp.stack([jnp.tile(k_norm_g[1], A_KV), jnp.tile(k_norm_g[2], A_KV)], axis=0)
    kg0 = jnp.tile(k_norm_g[0], A_KV).reshape(1, LANES)
    hid = jnp.arange(A_WIDTH) // A_DH
    bd = jnp.where(hid[:, None] == hid[None, :], 1.0 / A_DH, 0.0).astype(BF16)
    inproj_w = (wm, bm, wq, bq, wkv, bkv, ws, bs, qg, kg, bd)

    z = jnp.zeros((CMP_LEN, A_DH, A_DH), F32)
    r0 = jnp.concatenate([cmp_w_k, z, z, z], axis=2)
    r1 = jnp.concatenate([z, cmp_w_k, z, z], axis=2)
    r2 = jnp.concatenate([z, z, cmp_w_v, z], axis=2)
    r3 = jnp.concatenate([z, z, z, cmp_w_v], axis=2)
    wbd = jnp.concatenate([r0, r1, r2, r3], axis=1).astype(BF16)
    pe = jnp.concatenate([cmp_pe_k, cmp_pe_k, cmp_pe_v, cmp_pe_v], axis=1).reshape(CMP_LEN, 1, 2 * LANES)

    wog = jnp.concatenate([w_in[:, OFF_MO:OFF_MI], w_in[:, OFF_GA:N_IN]], axis=1).astype(BF16)
    bog = jnp.concatenate([b2[:, OFF_MO:OFF_MI], b2[:, OFF_GA:N_IN]], axis=1)
    mixout_w = (wog, bog, w_up_m.astype(BF16), w_up_a.astype(BF16), w_out.astype(BF16),
                w_router.T, b_router.reshape(N_EXPERTS, 1))
    return inproj_w, (wbd, pe, kg0), mixout_w


def _pick_tile(m, pref):
    t = pref
    while m % t:
        t //= 2
    return t


def kernel(x_prompt, x_sample, cache_nsa_kv, state_win_kv, state_mlstm_C, state_mlstm_n, state_mlstm_m, page_table, c_prompt, c_sample, w_ada, b_ada, g_mix, g_ffn, w_in, b_in, q_norm_g, k_norm_g, cmp_pe_k, cmp_pe_v, cmp_w_k, cmp_w_v, w_up_m, w_up_a, w_out, w_router, b_router, w_gu, b_gu, w_dn, b_dn):
    depth = w_in.shape[0]
    assert depth == 1
    B, T, D = x_prompt.shape
    DB, TS, _ = x_sample.shape
    n_pages = page_table.shape[1]
    past_len = n_pages * PAGE_SIZE
    wbuf = state_win_kv.shape[2]
    tp = SAMPLE_PAD_T
    assert TS <= tp and wbuf % tp == 0 and T % 128 == 0

    l = 0
    inproj_w, cmp_w, mixout_w = _prep_weights(
        w_in[l], b_in[l], q_norm_g[l], k_norm_g[l], cmp_pe_k[l], cmp_pe_v[l], cmp_w_k[l], cmp_w_v[l],
        w_up_m[l], w_up_a[l], w_out[l], w_router[l], b_router[l])
    wbd, pe, kg0 = cmp_w
    gmix = g_mix[l].reshape(1, D)
    gffn = g_ffn[l].reshape(1, D)

    nc = B + DB
    nc_pad = -(-nc // SUBLANES) * SUBLANES
    c_all = jnp.concatenate([c_prompt, c_sample, jnp.zeros((nc_pad - nc, D), F32)], axis=0)
    mod = _adaln(c_all, w_ada[l], b_ada[l])
    mod_p = mod[:B].reshape(B, 1, 6 * D)
    mod_s = jnp.repeat(mod[B:B + DB], tp, axis=0).reshape(1, DB * tp, 6 * D)

    mp = B * T
    tm = _pick_tile(T, 256)
    xp = x_prompt.reshape(mp, D)
    tabs_p = _rope_tables(jnp.arange(T, dtype=jnp.int32))
    mq, mk, mv, q, qr, rows, win, small, rows_t, win_t = _inproj(xp, mod_p, gmix, tabs_p, inproj_w, tm, T // tm,
                                                                 T // tm, rows_t_batches=B)
    Lp = _pick_tile(T, 128)
    hm, C_p, n_p, m_p = _mlstm(mq, mk, mv, small, B, T, T, Lp)
    o_nsa = _nsa_prompt(q, qr, small, rows, win, wbd, pe, kg0, B, T)
    assert T % MOE_TM == 0
    ms_pad = -(-(DB * tp) // MOE_TM) * MOE_TM
    nt_p = mp // MOE_TM
    nt_all = nt_p + ms_pad // MOE_TM + 1
    x1_p, xsl, info, cnt = _mixout(xp, hm, o_nsa, mod_p, gmix, gffn, mixout_w, T // MOE_TM, nt_all)

    ms = DB * tp
    xs_pad = jnp.concatenate([x_sample, jnp.zeros((DB, tp - TS, D), F32)], axis=1).reshape(ms, D)
    pos_s = past_len + jnp.tile(jnp.arange(tp, dtype=jnp.int32), DB)
    tabs_s = _rope_tables(pos_s)
    mq_s, mk_s, mv_s, q_s, qr_s, rows_s, win_s, small_s = _inproj(xs_pad, mod_s, gmix, tabs_s, inproj_w, ms, 1, 1)
    hm_s, C_s, n_s, m_s = _mlstm(mq_s, mk_s, mv_s, small_s, DB, tp, TS, tp,
                                 state=(state_mlstm_C[l], state_mlstm_n[l], state_mlstm_m[l]))
    cache2 = jnp.transpose(cache_nsa_kv[l], (0, 2, 3, 4, 1)).reshape(cache_nsa_kv.shape[1], 4 * LANES, PAGE_SIZE)
    winbuf = jnp.transpose(state_win_kv[l], (0, 2, 3, 4, 1)).reshape(DB, 2 * LANES, wbuf)
    o_nsa_s, win_out_s = _nsa_sample(page_table, cache2, q_s, qr_s, small_s, rows_s, win_s, winbuf,
                                     wbd, pe, kg0, TS)
    assert ms_pad == MOE_TM
    rpad = lambda a: jnp.concatenate([a, jnp.zeros((ms_pad - ms, a.shape[1]), a.dtype)], axis=0) if ms_pad > ms else a
    mod_sp = rpad(mod_s[0])[None]
    x1_s, xsl, info, cnt = _mixout(rpad(xs_pad), rpad(hm_s), rpad(o_nsa_s), mod_sp, gmix, gffn, mixout_w,
                                   1, nt_all, tile0=nt_p, shared=(xsl, info, cnt),
                                   t_mod=tp, t_valid=TS, m_valid=ms)

    ysl = _moe_experts(_moe_plan(cnt[:, :, 0]), xsl, w_gu[l], b_gu[l], w_dn[l], b_dn[l])
    y_p = _combine(ysl, info, x1_p, mod_p, T // MOE_TM).reshape(B, T, D)
    y_s_all = _combine(ysl, info, x1_s, mod_sp, 1, tile0=nt_p)
    valid = lambda a: a.reshape(DB, tp, -1)[:, :TS].reshape(DB * TS, -1)
    y_s = valid(y_s_all[:ms]).reshape(DB, TS, D)

    kv_p = jnp.transpose(rows_t.reshape(B, 4, A_KV, A_DH, T), (0, 4, 1, 2, 3))[None]
    kv_s = valid(rows_s).reshape(1, DB, TS, 4, A_KV, A_DH)
    wp = min(WINDOW, T)
    win_p = jnp.transpose(win_t[:, :, T - wp:].reshape(B, 2, A_KV, A_DH, wp), (0, 4, 1, 2, 3))[None]
    win_s_out = jnp.transpose(win_out_s.reshape(DB, 2, A_KV, A_DH, wbuf), (0, 4, 1, 2, 3))[None]
    return (y_p, y_s, kv_p, kv_s, win_p, win_s_out,
            C_p[None], n_p[None], m_p[None], C_s[None], n_s[None], m_s[None])
```

```python
import functools

import jax
import jax.numpy as jnp
from jax import lax
from jax.experimental import pallas as pl
from jax.experimental.pallas import tpu as pltpu

F32 = jnp.float32
BF16 = jnp.bfloat16

D_MODEL = 1024
M_HEADS = 4
M_DH = 128
M_WIDTH = M_HEADS * M_DH
A_HEADS = 8
A_KV = 2
A_HPG = A_HEADS // A_KV
A_DH = 64
A_WIDTH = A_HEADS * A_DH
CMP_STRIDE = 16
CMP_LEN = 32
SEL_LEN = 64
N_SEL = 16
WINDOW = 512
PAGE_SIZE = 128
ROPE_THETA = 500000.0
ROT_DIM = A_DH // 4
ATT_SCALE = A_DH ** -0.5
N_EXPERTS = 32
TOP_K = 4
D_EXPERT = D_MODEL
SWIGLU_LIMIT = 7.0
SWIGLU_ALPHA = 1.702
EPS = 1e-6

OFF_MQ, OFF_MK, OFF_MV, OFF_MO = 0, M_WIDTH, 2 * M_WIDTH, 3 * M_WIDTH
OFF_MI = 4 * M_WIDTH
OFF_MF = OFF_MI + M_HEADS
OFF_AQ = OFF_MF + M_HEADS
OFF_AKV = OFF_AQ + A_WIDTH
OFF_AG = OFF_AKV + 6 * A_KV * A_DH
OFF_GA = OFF_AG + 3 * A_HEADS
OFF_GB = OFF_GA + D_MODEL
N_IN = OFF_GB + D_MODEL

LANES = 128
SUBLANES = 8
VMEM_LIMIT = 56 * 1024 * 1024

NEG_BIG = -1e30
M_INIT = -1e29
LOG2E = 1.4426950408889634
SAMPLE_PAD_T = 8


def _cparams(sem):
    return pltpu.CompilerParams(dimension_semantics=sem, vmem_limit_bytes=VMEM_LIMIT)


def _bdot(a, b):
    return jnp.dot(a.astype(BF16), b.astype(BF16), preferred_element_type=F32)


def _bdot_t(a, b):
    return lax.dot_general(a.astype(BF16), b.astype(BF16), (((1,), (1,)), ((), ())),
                           preferred_element_type=F32)


def _split(a):
    hi = a.astype(BF16)
    lo = (a - hi.astype(F32)).astype(BF16)
    return hi, lo


def _dot3(a, b):
    ah, al = _split(a)
    bh, bl = _split(b)
    return (jnp.dot(ah, bh, preferred_element_type=F32) + jnp.dot(al, bh, preferred_element_type=F32)
            + jnp.dot(ah, bl, preferred_element_type=F32))


def _dot2_exact_rhs(a, b_bf16):
    ah, al = _split(a)
    return jnp.dot(ah, b_bf16, preferred_element_type=F32) + jnp.dot(al, b_bf16, preferred_element_type=F32)


def _sigmoid(x):
    return 0.5 * jnp.tanh(0.5 * x) + 0.5


def _rmsnorm_rows(x, g):
    return x * lax.rsqrt(jnp.mean(x * x, axis=-1, keepdims=True) + EPS) * g


def _adaln_kernel(c_ref, w_ref, b_ref, o_ref):
    c = c_ref[...]
    s = c * _sigmoid(c)
    o_ref[...] = _dot3(s, w_ref[...]) + b_ref[...]


def _adaln(c, w, b):
    mc, d = c.shape
    n = w.shape[1]
    tn = 1024
    return pl.pallas_call(
        _adaln_kernel,
        grid=(n // tn,),
        in_specs=[pl.BlockSpec((mc, d), lambda j: (0, 0)),
                  pl.BlockSpec((d, tn), lambda j: (0, j)),
                  pl.BlockSpec((1, tn), lambda j: (0, j))],
        out_specs=pl.BlockSpec((mc, tn), lambda j: (0, j)),
        out_shape=jax.ShapeDtypeStruct((mc, n), F32),
        compiler_params=_cparams(("parallel",)),
        name="adaln",
    )(c, w, b.reshape(1, n))


def _head_norm(z, bd, gain):
    ms = _dot2_exact_rhs(z * z, bd)
    return z * lax.rsqrt(ms + EPS) * gain


def _rope(z, cos, s_prev, s_next):
    w = z.shape[1]
    rep = w // LANES
    if rep > 1:
        cos = jnp.concatenate([cos] * rep, axis=1)
        s_prev = jnp.concatenate([s_prev] * rep, axis=1)
        s_next = jnp.concatenate([s_next] * rep, axis=1)
    z_prev = pltpu.roll(z, ROT_DIM // 2, 1)
    z_next = pltpu.roll(z, w - ROT_DIM // 2, 1)
    return z * cos + z_prev * s_prev + z_next * s_next


def _inproj_kernel(x_ref, mod_ref, gmix_ref, cos_ref, sp_ref, sn_ref,
                   wm_ref, bm_ref, wq_ref, bq_ref, wkv_ref, bkv_ref, ws_ref, bs_ref,
                   qg_ref, kg_ref, bd_ref,
                   mq_ref, mk_ref, mv_ref, q_ref, qr_ref, rows_ref, win_ref, small_ref,
                   rows_t_ref=None, win_t_ref=None):
    x = x_ref[...]
    sh1 = mod_ref[:, 0:D_MODEL]
    sc1 = mod_ref[:, D_MODEL:2 * D_MODEL]
    h = _rmsnorm_rows(x, gmix_ref[...]) * (1.0 + sc1) + sh1
    hb = h.astype(BF16)

    mq_ref[...] = jnp.dot(hb, wm_ref[:, 0:M_WIDTH], preferred_element_type=F32) + bm_ref[:, 0:M_WIDTH]
    mk = jnp.dot(hb, wm_ref[:, M_WIDTH:2 * M_WIDTH], preferred_element_type=F32) + bm_ref[:, M_WIDTH:2 * M_WIDTH]
    mk_ref[...] = mk * (M_DH ** -0.5)
    mv_ref[...] = (jnp.dot(hb, wm_ref[:, 2 * M_WIDTH:3 * M_WIDTH], preferred_element_type=F32)
                   + bm_ref[:, 2 * M_WIDTH:3 * M_WIDTH])

    cos, sp, sn = cos_ref[...], sp_ref[...], sn_ref[...]
    zq = jnp.dot(hb, wq_ref[...], preferred_element_type=F32) + bq_ref[...]
    qn = _head_norm(zq, bd_ref[...], qg_ref[...])
    q_ref[...] = qn
    qr_ref[...] = _rope(qn, cos, sp, sn)

    zkv = jnp.dot(hb, wkv_ref[...], preferred_element_type=F32) + bkv_ref[...]
    bd2 = bd_ref[0:LANES, 0:LANES]
    ksel = _head_norm(zkv[:, 2 * LANES:3 * LANES], bd2, kg_ref[0:1, :])
    rows = jnp.concatenate([zkv[:, 0:2 * LANES], _rope(ksel, cos, sp, sn), zkv[:, 3 * LANES:4 * LANES]], axis=1)
    rows_ref[...] = rows
    if rows_t_ref is not None:
        rows_t_ref[...] = jnp.transpose(rows)
    kwin = _head_norm(zkv[:, 4 * LANES:5 * LANES], bd2, kg_ref[1:2, :])
    win = jnp.concatenate([_rope(kwin, cos, sp, sn), zkv[:, 5 * LANES:6 * LANES]], axis=1)
    win_ref[...] = win
    if win_t_ref is not None:
        win_t_ref[...] = jnp.transpose(win)

    small_ref[...] = _dot3(h, ws_ref[...]) + bs_ref[...]


def _inproj(x2, mod3, gmix, tabs, wts, tm, tiles_per_mod, pos_tiles, rows_t_batches=None):
    m = x2.shape[0]
    cos_t, sp_t, sn_t = tabs
    (wm, bm, wq, bq, wkv, bkv, ws, bs, qg, kg, bd) = wts
    r = mod3.shape[1]
    row = lambda i: (i, 0)
    const = lambda i: (0, 0)
    tab = lambda i: (i % pos_tiles, 0)
    in_specs = [
        pl.BlockSpec((tm, D_MODEL), row),
        pl.BlockSpec((None, r, 6 * D_MODEL), lambda i: (i // tiles_per_mod, 0, 0)),
        pl.BlockSpec((1, D_MODEL), const),
        pl.BlockSpec((tm, LANES), tab), pl.BlockSpec((tm, LANES), tab), pl.BlockSpec((tm, LANES), tab),
        pl.BlockSpec(wm.shape, const), pl.BlockSpec(bm.shape, const),
        pl.BlockSpec(wq.shape, const), pl.BlockSpec(bq.shape, const),
        pl.BlockSpec(wkv.shape, const), pl.BlockSpec(bkv.shape, const),
        pl.BlockSpec(ws.shape, const), pl.BlockSpec(bs.shape, const),
        pl.BlockSpec(qg.shape, const), pl.BlockSpec(kg.shape, const), pl.BlockSpec(bd.shape, const),
    ]
    widths = (M_WIDTH, M_WIDTH, M_WIDTH, A_WIDTH, A_WIDTH, 4 * LANES, 2 * LANES, LANES)
    out_specs = [pl.BlockSpec((tm, w), row) for w in widths]
    out_shape = [jax.ShapeDtypeStruct((m, w), F32) for w in widths]
    if rows_t_batches is not None:
        for w in (4 * LANES, 2 * LANES):
            out_specs.append(pl.BlockSpec((None, w, tm), lambda i: (i // tiles_per_mod, 0, i % tiles_per_mod)))
            out_shape.append(jax.ShapeDtypeStruct((rows_t_batches, w, m // rows_t_batches), F32))
    return pl.pallas_call(
        _inproj_kernel,
        grid=(m // tm,),
        in_specs=in_specs,
        out_specs=out_specs,
        out_shape=out_shape,
        compiler_params=_cparams(("parallel",)),
        name="inproj",
    )(x2, mod3, gmix, cos_t, sp_t, sn_t, wm, bm, wq, bq, wkv, bkv, ws, bs, qg, kg, bd)


def _log_sigmoid(x):
    return jnp.minimum(x, 0.0) - jnp.log(1.0 + jnp.exp(-jnp.abs(x)))


def _mlstm_kernel(*refs, L, t_valid, has_state):
    if has_state:
        q_ref, k_ref, v_ref, s_ref, c0_ref, n0_ref, m0_ref, h_ref, c_ref, n_ref, m_ref = refs
    else:
        q_ref, k_ref, v_ref, s_ref, h_ref, c_ref, n_ref, m_ref = refs
    c = pl.program_id(1)

    @pl.when(c == 0)
    def _():
        if has_state:
            c_ref[...] = c0_ref[...]
            n_ref[...] = n0_ref[...]
            m_ref[...] = m0_ref[...]
        else:
            c_ref[...] = jnp.zeros(c_ref.shape, F32)
            n_ref[...] = jnp.zeros(n_ref.shape, F32)
            m_ref[...] = jnp.zeros(m_ref.shape, F32)

    row = lax.broadcasted_iota(jnp.int32, (L, L), 0)
    col = lax.broadcasted_iota(jnp.int32, (L, L), 1)
    causal = col <= row
    eye = col == row
    tok_col = c * L + lax.broadcasted_iota(jnp.int32, (L, 1), 0)
    valid_col = tok_col < t_valid
    for hd in range(M_HEADS):
        lo, hi = hd * M_DH, (hd + 1) * M_DH
        q = q_ref[:, lo:hi]
        k = k_ref[:, lo:hi]
        v = v_ref[:, lo:hi]
        i_col = s_ref[:, hd:hd + 1]
        lf_col = _log_sigmoid(s_ref[:, M_HEADS + hd:M_HEADS + hd + 1])
        lf_col = jnp.where(valid_col, lf_col, 0.0)
        i_col = jnp.where(valid_col, i_col, -jnp.inf)
        if L == LANES:
            i_col = jnp.broadcast_to(i_col, (L, L))
            lf_c = jnp.broadcast_to(lf_col, (L, L))
            p0 = lf_c.astype(BF16)
            r1 = lf_c - p0.astype(F32)
            p1 = r1.astype(BF16)
            p2 = (r1 - p1.astype(F32)).astype(BF16)
            tril = jnp.where(causal, 1.0, 0.0).astype(BF16)
            b_col = (jnp.dot(tril, p0, preferred_element_type=F32) + jnp.dot(tril, p1, preferred_element_type=F32)
                     + jnp.dot(tril, p2, preferred_element_type=F32))
            i_row = jnp.transpose(i_col)[0:1, :]
            b_row = jnp.transpose(b_col)[0:1, :]
        else:
            i_row = jnp.sum(jnp.where(eye, i_col, 0.0), axis=0, keepdims=True)
            lf_row = jnp.sum(jnp.where(eye, lf_col, 0.0), axis=0, keepdims=True)
            b_col = jnp.sum(jnp.where(causal, lf_row, 0.0), axis=1, keepdims=True)
            b_row = jnp.sum(jnp.where(row <= col, lf_col, 0.0), axis=0, keepdims=True)
        m_prev = m_ref[:, hd:hd + 1]
        dmat = jnp.where(causal, b_col - b_row + i_row, -jnp.inf)
        inter = b_col + m_prev
        m_row = jnp.maximum(jnp.max(dmat, axis=1, keepdims=True), inter)
        w = jnp.exp(dmat - m_row)
        w_inter = jnp.exp(inter - m_row)
        s = _bdot_t(q, k) * w
        cm = c_ref[hd]
        nv = n_ref[hd]
        num = _bdot(s, v) + w_inter * _bdot_t(q, cm)
        den = jnp.sum(s, axis=1, keepdims=True) + w_inter * jnp.sum(q * nv, axis=1, keepdims=True)
        h_ref[:, lo:hi] = num / jnp.maximum(jnp.abs(den), jnp.exp(-m_row))
        b_last = b_col[L - 1:L, 0:1]
        dec_col = b_last - b_col + i_col
        dec_row = b_last - b_row + i_row
        m_new = jnp.maximum(b_last + m_prev, jnp.max(dec_row, axis=1, keepdims=True))
        ws_col = jnp.exp(dec_col - m_new)
        wc = jnp.exp(b_last + m_prev - m_new)
        vw = (v * ws_col).astype(BF16)
        upd = lax.dot_general(vw, k.astype(BF16), (((0,), (0,)), ((), ())), preferred_element_type=F32)
        c_ref[hd] = wc * cm + upd
        n_ref[hd] = wc * nv + jnp.sum(k * ws_col, axis=0, keepdims=True)
        m_ref[:, hd:hd + 1] = m_new


def _mlstm(mq, mk, mv, small, nb, t_pad, t_valid, L, state=None):
    nc = t_pad // L
    has_state = state is not None
    blk = lambda b, c: (b * nc + c, 0)
    st4 = lambda b, c: (b, 0, 0, 0)
    st3 = lambda b, c: (b, 0, 0)
    in_specs = [pl.BlockSpec((L, M_WIDTH), blk)] * 3 + [pl.BlockSpec((L, LANES), blk)]
    args = [mq, mk, mv, small]
    if has_state:
        c0, n0, m0 = state
        in_specs += [pl.BlockSpec((None, M_HEADS, M_DH, M_DH), st4),
                     pl.BlockSpec((None, M_HEADS, 1, M_DH), st4),
                     pl.BlockSpec((None, 1, M_HEADS), st3)]
        args += [c0, n0.reshape(nb, M_HEADS, 1, M_DH), m0.reshape(nb, 1, M_HEADS)]
    out_specs = [pl.BlockSpec((L, M_WIDTH), blk),
                 pl.BlockSpec((None, M_HEADS, M_DH, M_DH), st4),
                 pl.BlockSpec((None, M_HEADS, 1, M_DH), st4),
                 pl.BlockSpec((None, 1, M_HEADS), st3)]
    out_shape = [jax.ShapeDtypeStruct((nb * t_pad, M_WIDTH), F32),
                 jax.ShapeDtypeStruct((nb, M_HEADS, M_DH, M_DH), F32),
                 jax.ShapeDtypeStruct((nb, M_HEADS, 1, M_DH), F32),
                 jax.ShapeDtypeStruct((nb, 1, M_HEADS), F32)]
    h, cs, ns, ms = pl.pallas_call(
        functools.partial(_mlstm_kernel, L=L, t_valid=t_valid, has_state=has_state),
        grid=(nb, nc),
        in_specs=in_specs,
        out_specs=out_specs,
        out_shape=out_shape,
        compiler_params=_cparams(("parallel", "arbitrary")),
        name="mlstm",
    )(*args)
    return h, cs, ns.reshape(nb, M_HEADS, M_DH), ms.reshape(nb, M_HEADS)


def _stack_heads(qt, g):
    t = qt.shape[0]
    z = jnp.zeros((t, A_DH), F32)
    parts = []
    for hh in range(A_HPG):
        hd = g * A_HPG + hh
        qh = qt[:, hd * A_DH:(hd + 1) * A_DH] * (ATT_SCALE * LOG2E)
        parts.append(jnp.concatenate([qh, z], axis=1) if g == 0 else jnp.concatenate([z, qh], axis=1))
    return jnp.concatenate(parts, axis=0).astype(BF16)


def _gate_cols(small, g, br):
    cols = []
    for hh in range(A_HPG):
        c0 = 2 * M_HEADS + (g * A_HPG + hh) * 3 + br
        cols.append(_sigmoid(small[:, c0:c0 + 1]))
    return jnp.concatenate(cols, axis=0)


def _compress(k_ref, v_ref, nseg, wbd_ref, pe_ref, kg0):
    acc_lo = jnp.zeros((nseg, 2 * LANES), F32)
    acc_hi = jnp.zeros((nseg, 2 * LANES), F32)
    for l in range(CMP_STRIDE):
        xl = jnp.concatenate([k_ref[pl.ds(l, nseg, stride=CMP_STRIDE), :],
                              v_ref[pl.ds(l, nseg, stride=CMP_STRIDE), :]], axis=1)
        acc_lo = acc_lo + _bdot(xl + pe_ref[l], wbd_ref[l])
        acc_hi = acc_hi + _bdot(xl + pe_ref[CMP_STRIDE + l], wbd_ref[CMP_STRIDE + l])
    return _compress_finish(acc_lo, acc_hi, nseg, kg0)


def _compress_grouped(x_ref, nseg, wbd_ref, pe_ref, kg0):
    acc_lo = jnp.zeros((nseg, 2 * LANES), F32)
    acc_hi = jnp.zeros((nseg, 2 * LANES), F32)
    pe_lo = jnp.zeros((SUBLANES, 2 * LANES), F32)
    pe_hi = jnp.zeros((SUBLANES, 2 * LANES), F32)
    for l in range(CMP_STRIDE):
        xl = x_ref[l].astype(BF16)
        acc_lo = acc_lo + jnp.dot(xl, wbd_ref[l], preferred_element_type=F32)
        acc_hi = acc_hi + jnp.dot(xl, wbd_ref[CMP_STRIDE + l], preferred_element_type=F32)
        pe_lo = pe_lo + _bdot(jnp.broadcast_to(pe_ref[l], (SUBLANES, 2 * LANES)), wbd_ref[l])
        pe_hi = pe_hi + _bdot(jnp.broadcast_to(pe_ref[CMP_STRIDE + l], (SUBLANES, 2 * LANES)),
                              wbd_ref[CMP_STRIDE + l])
    return _compress_finish(acc_lo + pe_lo[0:1, :], acc_hi + pe_hi[0:1, :], nseg, kg0)


def _compress_finish(acc_lo, acc_hi, nseg, kg0):
    kv = acc_lo + pltpu.roll(acc_hi, nseg - 1, 0)
    kc = kv[:, 0:LANES]
    vc = kv[:, LANES:2 * LANES]
    lane = lax.broadcasted_iota(jnp.int32, (nseg, LANES), 1)
    sq = kc * kc
    ms0 = jnp.sum(jnp.where(lane < A_DH, sq, 0.0), axis=1, keepdims=True) * (1.0 / A_DH)
    ms1 = jnp.sum(jnp.where(lane >= A_DH, sq, 0.0), axis=1, keepdims=True) * (1.0 / A_DH)
    ms = jnp.where(lane < A_DH, ms0, ms1)
    kc = kc * lax.rsqrt(ms + EPS) * kg0
    return kc, vc


def _cmp_branch(qn_g, kc_b, vc_b, tpos_rows, nseg, n_tok):
    s = _bdot_t(qn_g, kc_b)
    nidx = lax.broadcasted_iota(jnp.int32, (1, nseg), 1)
    vis = (nidx * CMP_STRIDE + (CMP_LEN - 1)) <= tpos_rows
    sm = jnp.where(vis, s, NEG_BIG)
    mx = jnp.max(sm, axis=1, keepdims=True)
    e = jnp.where(vis, jnp.exp2(sm - mx), 0.0)
    d = jnp.sum(e, axis=1, keepdims=True)
    p = e / jnp.where(d > 0, d, 1.0)
    o = _bdot(p, vc_b)
    imp = p[0:n_tok]
    for hh in range(1, A_HPG):
        imp = imp + p[hh * n_tok:(hh + 1) * n_tok]
    return o, imp


def _masked_attn_direct(q_g, k_parts, v_parts, allowed_parts, feature_major):
    ss = [jnp.where(al, _bdot(q_g, kk) if fm else _bdot_t(q_g, kk), NEG_BIG)
          for kk, al, fm in zip(k_parts, allowed_parts, feature_major)]
    mx = ss[0].max(axis=1, keepdims=True)
    for s in ss[1:]:
        mx = jnp.maximum(mx, s.max(axis=1, keepdims=True))
    num = None
    den = None
    for s, al, vv, fm in zip(ss, allowed_parts, v_parts, feature_major):
        e = jnp.where(al, jnp.exp2(s - mx), 0.0)
        dd = jnp.sum(e, axis=1, keepdims=True)
        oo = _bdot_t(e, vv) if fm else _bdot(e, vv)
        num = oo if num is None else num + oo
        den = dd if den is None else den + dd
    return num / jnp.where(den > 0, den, 1.0)


def _assemble_heads(o_groups, n_tok):
    pieces = []
    for g in range(A_KV):
        for hh in range(A_HPG):
            pieces.append(o_groups[g][hh * n_tok:(hh + 1) * n_tok, g * A_DH:(g + 1) * A_DH])
    return jnp.concatenate(pieces, axis=1)


def _lane_rep(a, rep):
    return a if rep == 1 else jnp.concatenate([a] * rep, axis=1)


def _nsa_prompt_kernel(q_ref, qr_ref, small_ref, rows_ref, win_ref, wbd_ref, pe_ref, kg0_ref,
                       pool_ref, o_ref,
                       kraw_sc, vraw_sc, kc_sc, vct_sc, sel_sc, m_sc, acc_sc, s_sc, *, T, tq, kc_len):
    qi = pl.program_id(1)
    nseg = T // CMP_STRIDE
    nsb = T // SEL_LEN
    bpc = kc_len // SEL_LEN

    @pl.when(qi == 0)
    def _():
        kraw_sc[...] = rows_ref[:, 0:LANES]
        vraw_sc[...] = rows_ref[:, LANES:2 * LANES]
        kc, vc = _compress(kraw_sc, vraw_sc, nseg, wbd_ref, pe_ref, kg0_ref[...])
        kc_sc[...] = kc
        vct_sc[...] = jnp.transpose(vc)

    t0 = qi * tq
    tpos = t0 + lax.broadcasted_iota(jnp.int32, (1, tq), 1)
    tpos4 = _lane_rep(tpos, A_HPG)
    q = q_ref[...]
    qr = qr_ref[...]
    small_t = jnp.transpose(small_ref[...])
    kc_b = kc_sc[...].astype(BF16)
    vct_b = vct_sc[...].astype(BF16)
    bidx = lax.broadcasted_iota(jnp.int32, (nsb, tq), 0)
    cur = tpos // SEL_LEN
    vis = (lax.broadcasted_iota(jnp.int32, (nseg, 1), 0) * CMP_STRIDE + (CMP_LEN - 1)) <= tpos4
    qr_gs = [_stack_heads(qr, g) for g in range(A_KV)]
    o_cmps = []
    for g in range(A_KV):
        sm = jnp.where(vis, _bdot_t(kc_b, _stack_heads(q, g)), NEG_BIG)
        mx = jnp.max(sm, axis=0, keepdims=True)
        e = jnp.where(vis, jnp.exp2(sm - mx), 0.0)
        d = jnp.sum(e, axis=0, keepdims=True)
        p = e / jnp.where(d > 0, d, 1.0)
        o_cmps.append(jnp.dot(vct_b, p.astype(BF16), preferred_element_type=F32))
        imp = p[:, 0:tq]
        for hh in range(1, A_HPG):
            imp = imp + p[:, hh * tq:(hh + 1) * tq]
        ih, il = _split(imp)
        imp_t = (jnp.dot(pool_ref[...], ih, preferred_element_type=F32)
                 + jnp.dot(pool_ref[...], il, preferred_element_type=F32))[0:nsb]
        val = jnp.where(bidx < cur, imp_t, -1.0)
        rank = jnp.zeros((nsb, tq), F32)
        for bp in range(nsb):
            vb = val[bp:bp + 1, :]
            rank = rank + jnp.where(vb > val, 1.0, jnp.where((vb == val) & (bidx > bp), 1.0, 0.0))
        sel_sc[g] = jnp.where(((rank < (N_SEL - 1)) & (bidx < cur)) | (bidx == cur), 1.0, 0.0)

    m_sc[...] = jnp.full(m_sc.shape, M_INIT, F32)
    acc_sc[...] = jnp.zeros(acc_sc.shape, F32)

    def with_ones_row(vt_, g):
        vb = vt_.astype(BF16)
        r0, pad = (1 - g) * A_DH, 2 * SUBLANES
        ones = jnp.ones((pad, vb.shape[1]), BF16)
        return jnp.concatenate(([vb[0:r0]] if r0 else []) + [ones, vb[r0 + pad:]], axis=0)

    def sel_body(c, carry):
        k0 = pl.multiple_of(c * kc_len, kc_len)
        kb = rows_ref[pl.ds(k0, kc_len), 2 * LANES:3 * LANES].astype(BF16)
        vt = jnp.transpose(rows_ref[pl.ds(k0, kc_len), 3 * LANES:4 * LANES])
        causal = (k0 + lax.broadcasted_iota(jnp.int32, (kc_len, 1), 0)) <= tpos
        for g in range(A_KV):
            s_sc[g, 0:kc_len, :] = _bdot_t(kb, qr_gs[g])
        for g in range(A_KV):
            selc = sel_sc[g, pl.ds(pl.multiple_of(c * bpc, bpc), bpc), :]
            selx = jnp.concatenate([jnp.broadcast_to(selc[j:j + 1, :], (SEL_LEN, tq)) for j in range(bpc)], axis=0)
            bias = jnp.where(causal & (selx > 0.5), 0.0, NEG_BIG)
            sm = s_sc[g, 0:kc_len, :] + _lane_rep(bias, A_HPG)
            m_prev = m_sc[g]
            m_new = jnp.maximum(m_prev, jnp.max(sm, axis=0, keepdims=True))
            alpha = jnp.exp2(m_prev - m_new)
            p = jnp.exp2(sm - m_new)
            acc_sc[g] = alpha * acc_sc[g] + jnp.dot(with_ones_row(vt, g), p.astype(BF16),
                                                    preferred_element_type=F32)
            m_sc[g] = m_new
        return carry

    lax.fori_loop(0, (t0 + tq + kc_len - 1) // kc_len, sel_body, 0)

    wk = min(WINDOW + tq, T)
    w0 = pl.multiple_of(jnp.clip(t0 + tq - wk, 0, T - wk), tq)
    kw = win_ref[pl.ds(w0, wk), 0:LANES].astype(BF16)
    vwt = jnp.transpose(win_ref[pl.ds(w0, wk), LANES:2 * LANES])
    wdiff = tpos - (w0 + lax.broadcasted_iota(jnp.int32, (wk, 1), 0))
    wbias = _lane_rep(jnp.where((wdiff >= 0) & (wdiff < WINDOW), 0.0, NEG_BIG), A_HPG)

    def gate_row(g, br):
        cols = [2 * M_HEADS + (g * A_HPG + hh) * 3 + br for hh in range(A_HPG)]
        return jnp.concatenate([_sigmoid(small_t[c0:c0 + 1, :]) for c0 in cols], axis=1)

    for g in range(A_KV):
        s_sc[g, 0:wk, :] = _bdot_t(kw, qr_gs[g])
    o_ts = []
    for g in range(A_KV):
        den = (1 - g) * A_DH
        acc = acc_sc[g]
        l = acc[den:den + 1, :]
        o_sel = acc / jnp.where(l > 0, l, 1.0)
        sw = s_sc[g, 0:wk, :] + wbias
        pw = jnp.exp2(sw - jnp.max(sw, axis=0, keepdims=True))
        ow = jnp.dot(with_ones_row(vwt, g), pw.astype(BF16), preferred_element_type=F32)
        o_win = ow / ow[den:den + 1, :]
        o_ts.append(gate_row(g, 0) * o_cmps[g] + gate_row(g, 1) * o_sel + gate_row(g, 2) * o_win)
    for j in range(A_HEADS // 2):
        g, h0 = j // (A_HPG // 2), 2 * (j % (A_HPG // 2))
        og = o_ts[g][g * A_DH:(g + 1) * A_DH, :]
        pair = jnp.concatenate([og[:, h0 * tq:(h0 + 1) * tq], og[:, (h0 + 1) * tq:(h0 + 2) * tq]], axis=0)
        o_ref[:, j * LANES:(j + 1) * LANES] = jnp.transpose(pair)


def _nsa_prompt(q, qr, small, rows, win, wbd, pe, kg0, nb, T):
    tq = 128
    kc_len = _pick_tile(T, 512)
    nq = T // tq
    nseg = T // CMP_STRIDE
    nsb = T // SEL_LEN
    nsb_p = -(-nsb // SUBLANES) * SUBLANES
    pool = (jnp.arange(nsb_p)[:, None] == jnp.arange(nseg)[None, :] // (SEL_LEN // CMP_STRIDE)).astype(BF16)
    tile = lambda b, i: (b * nq + i, 0)
    per_b = lambda b, i: (b, 0)
    c2 = lambda b, i: (0, 0)
    c3 = lambda b, i: (0, 0, 0)
    c4 = A_HPG * tq
    return pl.pallas_call(
        functools.partial(_nsa_prompt_kernel, T=T, tq=tq, kc_len=kc_len),
        grid=(nb, nq),
        in_specs=[pl.BlockSpec((tq, A_WIDTH), tile), pl.BlockSpec((tq, A_WIDTH), tile),
                  pl.BlockSpec((tq, LANES), tile),
                  pl.BlockSpec((T, 4 * LANES), per_b), pl.BlockSpec((T, 2 * LANES), per_b),
                  pl.BlockSpec(wbd.shape, c3), pl.BlockSpec(pe.shape, c3), pl.BlockSpec(kg0.shape, c2),
                  pl.BlockSpec(pool.shape, c2)],
        out_specs=pl.BlockSpec((tq, A_WIDTH), tile),
        out_shape=jax.ShapeDtypeStruct((nb * T, A_WIDTH), F32),
        scratch_shapes=[pltpu.VMEM((T, LANES), F32), pltpu.VMEM((T, LANES), F32),
                        pltpu.VMEM((nseg, LANES), F32), pltpu.VMEM((LANES, nseg), F32),
                        pltpu.VMEM((A_KV, nsb, tq), F32),
                        pltpu.VMEM((A_KV, 1, c4), F32),
                        pltpu.VMEM((A_KV, LANES, c4), F32),
                        pltpu.VMEM((A_KV, max(kc_len, min(WINDOW + tq, T)), c4), F32)],
        compiler_params=_cparams(("parallel", "arbitrary")),
        name="nsa_prompt",
    )(q, qr, small, rows, win, wbd, pe, kg0, pool)


def _nsa_sample_kernel(pt_ref, cache_ref, q_ref, qr_ref, small_ref, rows_ref, winnew_ref, winbuf_ref,
                       wbd_ref, pe_ref, kg0_ref, pool_ref, expand_ref,
                       o_ref, winout_ref,
                       page_buf, xperm_sc, sems, *, n_pages, past_len, t_valid):
    b = pl.program_id(0)
    nb = pl.num_programs(0)
    tp = SAMPLE_PAD_T
    nseg = past_len // CMP_STRIDE
    nsb = past_len // SEL_LEN
    wbuf = winbuf_ref.shape[1]

    def page_copy(bb, p):
        page = pt_ref[bb * n_pages + p]
        dst_lanes = pl.ds(pl.multiple_of(p * PAGE_SIZE, PAGE_SIZE), PAGE_SIZE)
        return pltpu.make_async_copy(cache_ref.at[page], page_buf.at[bb % 2, :, dst_lanes], sems.at[bb % 2])

    def start_all(bb):
        def body(p, c):
            page_copy(bb, p).start()
            return c
        lax.fori_loop(0, n_pages, body, 0)

    def wait_all(bb):
        def body(p, c):
            page_copy(bb, p).wait()
            return c
        lax.fori_loop(0, n_pages, body, 0)

    @pl.when(b == 0)
    def _():
        start_all(b)

    @pl.when(b + 1 < nb)
    def _():
        start_all(b + 1)

    wait_all(b)
    cmp_buf = page_buf.at[b % 2, pl.ds(0, 2 * LANES), :]
    sel_buf = page_buf.at[b % 2, pl.ds(2 * LANES, 2 * LANES), :]

    seg_pp = PAGE_SIZE // CMP_STRIDE
    pr = lax.broadcasted_iota(jnp.int32, (PAGE_SIZE, PAGE_SIZE), 0)
    pc = lax.broadcasted_iota(jnp.int32, (PAGE_SIZE, PAGE_SIZE), 1)
    perm = jnp.where(pc == CMP_STRIDE * (pr % seg_pp) + pr // seg_pp, 1.0, 0.0).astype(BF16)
    for p in range(n_pages):
        xp = _bdot_t(perm, cmp_buf[:, p * PAGE_SIZE:(p + 1) * PAGE_SIZE])
        for l in range(CMP_STRIDE):
            xperm_sc[l, p * seg_pp:(p + 1) * seg_pp, :] = xp[l * seg_pp:(l + 1) * seg_pp, :]
    kc, vc = _compress_grouped(xperm_sc, nseg, wbd_ref, pe_ref, kg0_ref[...])
    kc_b = kc.astype(BF16)
    vc_b = vc.astype(BF16)
    q = q_ref[...]
    qr = qr_ref[...]
    small = small_ref[...]
    tpos_col = past_len + lax.broadcasted_iota(jnp.int32, (tp, 1), 0)
    tpos_rows = jnp.concatenate([tpos_col] * A_HPG, axis=0)
    bp_idx = lax.broadcasted_iota(jnp.int32, (nsb, nsb), 0)
    b_idx = lax.broadcasted_iota(jnp.int32, (nsb, nsb), 1)
    o_cmps = []
    sels = []
    for g in range(A_KV):
        qn_g = _stack_heads(q, g)
        o_cmp, imp = _cmp_branch(qn_g, kc_b, vc_b, tpos_rows, nseg, tp)
        o_cmps.append(o_cmp)
        imp_sel = _dot2_exact_rhs(imp, pool_ref[...])
        imp_pad = jnp.concatenate([imp_sel, jnp.zeros((nsb - tp, nsb), F32)], axis=0)
        imp_t = jnp.transpose(imp_pad)
        rows_sel = []
        for t in range(tp):
            if t < t_valid:
                row_t = imp_sel[t:t + 1, :]
                col_t = imp_t[:, t:t + 1]
                ahead = jnp.where(col_t > row_t, 1.0, jnp.where((col_t == row_t) & (bp_idx < b_idx), 1.0, 0.0))
                rank = jnp.sum(ahead, axis=0, keepdims=True)
                rows_sel.append(jnp.where(rank < (N_SEL - 1), 1.0, 0.0))
            else:
                rows_sel.append(jnp.zeros((1, nsb), F32))
        sels.append(jnp.concatenate(rows_sel, axis=0))

    new_idx = lax.broadcasted_iota(jnp.int32, (tp, tp), 1)
    tok_idx = lax.broadcasted_iota(jnp.int32, (tp, tp), 0)
    new_ok = jnp.concatenate([jnp.where(new_idx <= tok_idx, 1.0, 0.0)] * A_HEADS, axis=0) > 0.5
    wpos = past_len - wbuf + lax.broadcasted_iota(jnp.int32, (1, wbuf), 1)
    wdiff = tpos_col - wpos
    win_ok = jnp.concatenate([jnp.where((wdiff >= 0) & (wdiff < WINDOW), 1.0, 0.0)] * A_HEADS, axis=0) > 0.5
    k_past = sel_buf[0:LANES, :].astype(BF16)
    v_past = sel_buf[LANES:2 * LANES, :].astype(BF16)
    k_new = rows_ref[:, 2 * LANES:3 * LANES]
    v_new = rows_ref[:, 3 * LANES:4 * LANES]
    kw_past = winbuf_ref[0:LANES, :]
    vw_past = winbuf_ref[LANES:2 * LANES, :]
    kw_new = winnew_ref[:, 0:LANES]
    vw_new = winnew_ref[:, LANES:2 * LANES]
    r4 = A_HPG * tp
    qr_all = jnp.concatenate([_stack_heads(qr, g) for g in range(A_KV)], axis=0)
    mk = jnp.dot(jnp.concatenate(sels, axis=0).astype(BF16), expand_ref[...],
                 preferred_element_type=F32)
    past_ok = jnp.concatenate([mk[g * tp:(g + 1) * tp] for g in range(A_KV) for _ in range(A_HPG)], axis=0) > 0.5
    o_sel = _masked_attn_direct(qr_all, [k_past, k_new], [v_past, v_new], [past_ok, new_ok], [True, False])
    o_win = _masked_attn_direct(qr_all, [kw_past, kw_new], [vw_past, vw_new], [win_ok, new_ok], [True, False])
    o_groups = []
    for g in range(A_KV):
        rs = slice(g * r4, (g + 1) * r4)
        o_groups.append(_gate_cols(small, g, 0) * o_cmps[g] + _gate_cols(small, g, 1) * o_sel[rs]
                        + _gate_cols(small, g, 2) * o_win[rs])
    o_ref[...] = _assemble_heads(o_groups, tp)

    rolled = pltpu.roll(winbuf_ref[...], wbuf - t_valid, 1)
    new_t = jnp.transpose(jnp.concatenate([winnew_ref[...], jnp.zeros((LANES - tp, 2 * LANES), F32)], axis=0))
    new_t = pltpu.roll(new_t, LANES - t_valid, 1)
    lane = lax.broadcasted_iota(jnp.int32, (2 * LANES, LANES), 1)
    winout_ref[:, 0:wbuf - LANES] = rolled[:, 0:wbuf - LANES]
    winout_ref[:, wbuf - LANES:wbuf] = jnp.where(lane < LANES - t_valid, rolled[:, wbuf - LANES:wbuf], new_t)


def _nsa_sample(page_table, cache, q, qr, small, rows, winnew, winbuf, wbd, pe, kg0, t_valid):
    nb, n_pages = page_table.shape
    past_len = n_pages * PAGE_SIZE
    nseg = past_len // CMP_STRIDE
    nsb = past_len // SEL_LEN
    tp = SAMPLE_PAD_T
    wbuf = winbuf.shape[2]
    pool = (jnp.arange(nseg)[:, None] // (SEL_LEN // CMP_STRIDE) == jnp.arange(nsb)[None, :]).astype(BF16)
    expand = (jnp.arange(nsb)[:, None] == jnp.arange(past_len)[None, :] // SEL_LEN).astype(BF16)
    tile = lambda b, pt: (b, 0)
    c2 = lambda b, pt: (0, 0)
    c3 = lambda b, pt: (0, 0, 0)
    gs = pltpu.PrefetchScalarGridSpec(
        num_scalar_prefetch=1,
        grid=(nb,),
        in_specs=[pl.BlockSpec(memory_space=pl.ANY),
                  pl.BlockSpec((tp, A_WIDTH), tile), pl.BlockSpec((tp, A_WIDTH), tile),
                  pl.BlockSpec((tp, LANES), tile), pl.BlockSpec((tp, 4 * LANES), tile),
                  pl.BlockSpec((tp, 2 * LANES), tile),
                  pl.BlockSpec((None, 2 * LANES, wbuf), lambda b, pt: (b, 0, 0)),
                  pl.BlockSpec(wbd.shape, c3), pl.BlockSpec(pe.shape, c3), pl.BlockSpec(kg0.shape, c2),
                  pl.BlockSpec(pool.shape, c2), pl.BlockSpec(expand.shape, c2)],
        out_specs=[pl.BlockSpec((tp, A_WIDTH), tile),
                   pl.BlockSpec((None, 2 * LANES, wbuf), lambda b, pt: (b, 0, 0))],
        scratch_shapes=[pltpu.VMEM((2, 4 * LANES, past_len), F32),
                        pltpu.VMEM((CMP_STRIDE, past_len // CMP_STRIDE, 2 * LANES), F32),
                        pltpu.SemaphoreType.DMA((2,))],
    )
    return pl.pallas_call(
        functools.partial(_nsa_sample_kernel, n_pages=n_pages, past_len=past_len, t_valid=t_valid),
        grid_spec=gs,
        out_shape=[jax.ShapeDtypeStruct((nb * tp, A_WIDTH), F32),
                   jax.ShapeDtypeStruct((nb, 2 * LANES, wbuf), F32)],
        compiler_params=_cparams(("arbitrary",)),
        name="nsa_sample",
    )(page_table.reshape(-1), cache, q, qr, small, rows, winnew, winbuf, wbd, pe, kg0, pool, expand)


MOE_TM = 256
SEG_ALIGN = 8
MOE_RL = -(-(MOE_TM * TOP_K + N_EXPERTS * (SEG_ALIGN - 1)) // LANES) * LANES


def _pack_halves(x, bf16_exact=False):
    w = x.shape[1] // 2
    bits = lax.bitcast_convert_type(x if bf16_exact else x.astype(BF16).astype(F32), jnp.uint32)
    return bits[:, :w] | (bits[:, w:] >> 16)


def _unpack_halves(u):
    hi = lax.bitcast_convert_type(u & jnp.uint32(0xFFFF0000), F32).astype(BF16)
    lo = lax.bitcast_convert_type(u << 16, F32).astype(BF16)
    return hi, lo


def _route_and_sort(h2, wrt_ref, brt_ref, xsl_ref, info_ref, cnt_ref, tm, t_mod, t_valid, m_valid):
    ne = N_EXPERTS
    h2b = h2.astype(BF16)
    h2l = (h2 - h2b.astype(F32)).astype(BF16)
    wh, wl = _split(wrt_ref[...])
    lt = _bdot_t(wh, h2b) + _bdot_t(wl, h2b) + _bdot_t(wh, h2l) + brt_ref[...]
    eidx = lax.broadcasted_iota(jnp.int32, (ne, tm), 0)
    rank = jnp.zeros((ne, tm), F32)
    for ep in range(ne):
        v = lt[ep:ep + 1, :]
        rank = rank + jnp.where(v > lt, 1.0, jnp.where((v == lt) & (eidx > ep), 1.0, 0.0))
    sel = rank < TOP_K
    if t_mod is not None:
        tok = pl.program_id(0) * tm + lax.broadcasted_iota(jnp.int32, (1, tm), 1)
        sel = sel & ((tok % t_mod) < t_valid) & (tok < m_valid)
    mx = jnp.max(jnp.where(sel, lt, NEG_BIG), axis=0, keepdims=True)
    ex = jnp.where(sel, jnp.exp(lt - mx), 0.0)
    den = jnp.sum(ex, axis=0, keepdims=True)
    gate = ex / jnp.where(den > 0, den, 1.0)
    self_ = jnp.where(sel, 1.0, 0.0)
    selb = self_.astype(BF16)
    er = lax.broadcasted_iota(jnp.int32, (ne, ne), 0)
    ec = lax.broadcasted_iota(jnp.int32, (ne, ne), 1)
    c = jnp.dot(jnp.where(ec <= er, 1.0, 0.0).astype(BF16), selb, preferred_element_type=F32)
    tr = lax.broadcasted_iota(jnp.int32, (tm, tm), 0)
    tc = lax.broadcasted_iota(jnp.int32, (tm, tm), 1)
    rk = jnp.dot(selb, jnp.where(tr < tc, 1.0, 0.0).astype(BF16), preferred_element_type=F32)
    cnt = jnp.sum(self_, axis=1, keepdims=True)
    cnt_al = jnp.floor((cnt + (SEG_ALIGN - 1)) * (1.0 / SEG_ALIGN)) * SEG_ALIGN
    cnt_b = jnp.broadcast_to(cnt_al, (ne, LANES))
    cnt_ref[...] = cnt_b
    off = jnp.dot(jnp.where(ec < er, 1.0, 0.0).astype(BF16), cnt_b.astype(BF16), preferred_element_type=F32)
    rowidx = off[:, 0:1] + rk
    rows_k, gates_k, exps_k = [], [], []
    for k in range(1, TOP_K + 1):
        mk = sel & (c == k)
        has = jnp.sum(jnp.where(mk, 1.0, 0.0), axis=0, keepdims=True)
        rows_k.append(jnp.sum(jnp.where(mk, rowidx, 0.0), axis=0, keepdims=True) + has - 1.0)
        gates_k.append(jnp.sum(jnp.where(mk, gate, 0.0), axis=0, keepdims=True))
        exps_k.append(jnp.sum(jnp.where(mk, eidx.astype(F32), 0.0), axis=0, keepdims=True))
    info_ref[...] = jnp.concatenate(rows_k + gates_k + exps_k + [jnp.zeros((4, tm), F32)], axis=0)
    ridx = lax.broadcasted_iota(jnp.int32, (MOE_RL, tm), 0).astype(F32)
    perm = jnp.zeros((MOE_RL, tm), F32)
    for k in range(TOP_K):
        perm = jnp.where(ridx == rows_k[k], 1.0, perm)
    xs = jnp.dot(perm.astype(BF16), h2b, preferred_element_type=F32)
    xsl_ref[...] = _pack_halves(xs, bf16_exact=True)


def _mixout_kernel(x_ref, hm_ref, on_ref, mod_ref, gmix_ref, gffn_ref,
                   wog_ref, bog_ref, wum_ref, wua_ref, wout_ref, wrt_ref, brt_ref,
                   x1_ref, xsl_ref, info_ref, cnt_ref, *, tm, t_mod, t_valid, m_valid, n_real):
    if n_real is not None:
        @pl.when(pl.program_id(0) >= n_real)
        def _():
            xsl_ref[...] = jnp.zeros(xsl_ref.shape, jnp.uint32)
            info_ref[...] = jnp.zeros(info_ref.shape, F32)
            cnt_ref[...] = jnp.zeros(cnt_ref.shape, F32)

        @pl.when(pl.program_id(0) < n_real)
        def _():
            _mixout_body(x_ref, hm_ref, on_ref, mod_ref, gmix_ref, gffn_ref, wog_ref, bog_ref, wum_ref,
                         wua_ref, wout_ref, wrt_ref, brt_ref, x1_ref, xsl_ref, info_ref, cnt_ref,
                         tm, t_mod, t_valid, m_valid)
    else:
        _mixout_body(x_ref, hm_ref, on_ref, mod_ref, gmix_ref, gffn_ref, wog_ref, bog_ref, wum_ref,
                     wua_ref, wout_ref, wrt_ref, brt_ref, x1_ref, xsl_ref, info_ref, cnt_ref,
                     tm, t_mod, t_valid, m_valid)


def _mixout_body(x_ref, hm_ref, on_ref, mod_ref, gmix_ref, gffn_ref,
                 wog_ref, bog_ref, wum_ref, wua_ref, wout_ref, wrt_ref, brt_ref,
                 x1_ref, xsl_ref, info_ref, cnt_ref, tm, t_mod, t_valid, m_valid):
    d = D_MODEL
    x = x_ref[...]
    sh1, sc1, gt1 = mod_ref[:, 0:d], mod_ref[:, d:2 * d], mod_ref[:, 2 * d:3 * d]
    sh2, sc2 = mod_ref[:, 3 * d:4 * d], mod_ref[:, 4 * d:5 * d]
    h = _rmsnorm_rows(x, gmix_ref[...]) * (1.0 + sc1) + sh1
    hb = h.astype(BF16)
    mo = jnp.dot(hb, wog_ref[:, 0:M_WIDTH], preferred_element_type=F32) + bog_ref[:, 0:M_WIDTH]
    ym = _bdot(_sigmoid(mo) * hm_ref[...], wum_ref[...])
    ya = _bdot(on_ref[...], wua_ref[...])
    ga = jnp.dot(hb, wog_ref[:, M_WIDTH:M_WIDTH + d], preferred_element_type=F32) + bog_ref[:, M_WIDTH:M_WIDTH + d]
    u = _sigmoid(ga) * ym
    gb = (jnp.dot(hb, wog_ref[:, M_WIDTH + d:M_WIDTH + 2 * d], preferred_element_type=F32)
          + bog_ref[:, M_WIDTH + d:M_WIDTH + 2 * d])
    u = u + _sigmoid(gb) * ya
    x1 = x + gt1 * _bdot(u, wout_ref[...])
    x1_ref[...] = x1
    h2 = _rmsnorm_rows(x1, gffn_ref[...]) * (1.0 + sc2) + sh2
    _route_and_sort(h2, wrt_ref, brt_ref, xsl_ref, info_ref, cnt_ref, tm, t_mod, t_valid, m_valid)


def _mixout_with_shared(*refs, n_shared, **kw):
    n_in = 13
    _mixout_kernel(*refs[:n_in], *refs[n_in + n_shared:], **kw)


def _mixout(x2, hm, on, mod3, gmix, gffn, wts, tiles_per_mod, nt_total, tile0=0, shared=None,
            t_mod=None, t_valid=None, m_valid=None):
    m = x2.shape[0]
    tm = MOE_TM
    nt = m // tm
    (wog, bog, wum, wua, wout, wr, br) = wts
    r = mod3.shape[1]
    n_extra = nt_total - tile0 - nt if shared is None else 0
    row = lambda i: (jnp.minimum(i, nt - 1), 0)
    const = lambda i: (0, 0)
    in_specs = [pl.BlockSpec((tm, D_MODEL), row), pl.BlockSpec((tm, M_WIDTH), row),
                pl.BlockSpec((tm, A_WIDTH), row),
                pl.BlockSpec((None, r, 6 * D_MODEL), lambda i: (jnp.minimum(i, nt - 1) // tiles_per_mod, 0, 0)),
                pl.BlockSpec((1, D_MODEL), const), pl.BlockSpec((1, D_MODEL), const),
                pl.BlockSpec(wog.shape, const), pl.BlockSpec(bog.shape, const),
                pl.BlockSpec(wum.shape, const), pl.BlockSpec(wua.shape, const),
                pl.BlockSpec(wout.shape, const), pl.BlockSpec(wr.shape, const),
                pl.BlockSpec(br.shape, const)]
    args = [x2, hm, on, mod3, gmix, gffn, wog, bog, wum, wua, wout, wr, br]
    kw = dict(tm=tm, t_mod=t_mod, t_valid=t_valid, m_valid=m_valid, n_real=nt if n_extra else None)
    body = functools.partial(_mixout_kernel, **kw)
    aliases = {}
    if shared is not None:
        in_specs += [pl.BlockSpec(memory_space=pl.ANY)] * len(shared)
        aliases = {len(args) + j: 1 + j for j in range(len(shared))}
        args += list(shared)
        body = functools.partial(_mixout_with_shared, n_shared=len(shared), **kw)
    return pl.pallas_call(
        body,
        grid=(nt + n_extra,),
        in_specs=in_specs,
        out_specs=[pl.BlockSpec((tm, D_MODEL), row),
                   pl.BlockSpec((MOE_RL, D_MODEL // 2), lambda i: (tile0 + i, 0)),
                   pl.BlockSpec((16, tm), lambda i: (0, tile0 + i)),
                   pl.BlockSpec((None, N_EXPERTS, LANES), lambda i: (tile0 + i, 0, 0))],
        out_shape=[jax.ShapeDtypeStruct((m, D_MODEL), F32),
                   jax.ShapeDtypeStruct((nt_total * MOE_RL, D_MODEL // 2), jnp.uint32),
                   jax.ShapeDtypeStruct((16, nt_total * tm), F32),
                   jax.ShapeDtypeStruct((nt_total, N_EXPERTS, LANES), F32)],
        input_output_aliases=aliases,
        compiler_params=_cparams(("arbitrary" if n_extra else "parallel",)),
        name="mixout",
    )(*args)


MOE_BM = 256
MOE_CH = 512


def _moe_kernel(be_ref, na_ref, grp_ref, first_ref, wslot_ref, nxt_ref,
                xsl_ref, wgu_ref, bgu_ref, wdn_ref, bdn_ref, ysl_ref,
                wgu_bf, wdn_bf, xbuf, ybuf, wgu_f, wdn_f, sem_in, sem_out, sem_w, *, trash_row0):
    i = pl.program_id(0)
    na = na_ref[0]
    e = be_ref[i]
    n_grp = MOE_BM // SEG_ALIGN

    def weight_copies(ex, s):
        return [pltpu.make_async_copy(wgu_ref.at[ex], wgu_f.at[s], sem_w.at[s]),
                pltpu.make_async_copy(wdn_ref.at[ex], wdn_f.at[s], sem_w.at[s])]

    def group_copies(blk, inbound, slot=None):
        slot = blk % 2 if slot is None else slot
        cps = []
        for r in range(n_grp):
            v = grp_ref[blk * n_grp + r]
            vm_rows = pl.ds(r * SEG_ALIGN, SEG_ALIGN)
            if inbound:
                row = pl.multiple_of(jnp.where(v >= 0, v, trash_row0 + 2 * MOE_BM), SEG_ALIGN)
                cps.append(pltpu.make_async_copy(xsl_ref.at[pl.ds(row, SEG_ALIGN), :],
                                                 xbuf.at[slot, vm_rows, :], sem_in.at[slot]))
            else:
                spare = trash_row0 + slot * MOE_BM + r * SEG_ALIGN
                row = pl.multiple_of(jnp.where(v >= 0, v, spare), SEG_ALIGN)
                cps.append(pltpu.make_async_copy(ybuf.at[slot, vm_rows, :],
                                                 ysl_ref.at[pl.ds(row, SEG_ALIGN), :], sem_out.at[slot]))
        return cps

    def start_gather(blk):
        for cp in group_copies(blk, True):
            cp.start()

    def start_scatter(blk):
        for cp in group_copies(blk, False):
            cp.start()

    def wait_rows(blk, sem, inbound):
        slot = blk % 2
        if inbound:
            pltpu.make_async_copy(xsl_ref.at[pl.ds(0, MOE_BM), :], xbuf.at[slot], sem.at[slot]).wait()
        else:
            pltpu.make_async_copy(ybuf.at[slot], ysl_ref.at[pl.ds(0, MOE_BM), :], sem.at[slot]).wait()

    @pl.when(i == 0)
    def _():
        start_gather(i)
        for cp in weight_copies(e, 0):
            cp.start()

    @pl.when(i + 1 < na)
    def _():
        start_gather(i + 1)

    @pl.when((i < na) & (first_ref[i] != 0))
    def _():
        s = wslot_ref[i]
        for cp in weight_copies(e, s):
            cp.wait()
        nx = nxt_ref[i]

        @pl.when(nx >= 0)
        def _():
            for cp in weight_copies(nx, 1 - s):
                cp.start()

        for j in range(2 * D_EXPERT // MOE_CH):
            wgu_bf[:, j * MOE_CH:(j + 1) * MOE_CH] = wgu_f[s, :, j * MOE_CH:(j + 1) * MOE_CH].astype(BF16)
        for j in range(D_EXPERT // MOE_CH):
            wdn_bf[j * MOE_CH:(j + 1) * MOE_CH, :] = wdn_f[s, j * MOE_CH:(j + 1) * MOE_CH, :].astype(BF16)

    @pl.when(i < na)
    def _():
        slot = i % 2
        wait_rows(i, sem_in, True)

        @pl.when(i >= 2)
        def _():
            wait_rows(i - 2, sem_out, False)

        half = D_MODEL // 2
        xh, xl = _unpack_halves(xbuf[slot])

        def xdot(c0, c1):
            return (jnp.dot(xh, wgu_bf[0:half, c0:c1], preferred_element_type=F32)
                    + jnp.dot(xl, wgu_bf[half:D_MODEL, c0:c1], preferred_element_type=F32))

        acc = jnp.zeros((MOE_BM, D_MODEL), F32) + bdn_ref[...]
        for j in range(D_EXPERT // MOE_CH):
            lo, hi = j * MOE_CH, (j + 1) * MOE_CH
            gj = xdot(lo, hi) + bgu_ref[:, lo:hi]
            uj = xdot(D_EXPERT + lo, D_EXPERT + hi) + bgu_ref[:, D_EXPERT + lo:D_EXPERT + hi]
            gj = jnp.minimum(gj, SWIGLU_LIMIT)
            uj = jnp.clip(uj, -SWIGLU_LIMIT, SWIGLU_LIMIT)
            act = gj * _sigmoid(SWIGLU_ALPHA * gj) * (uj + 1.0)
            acc = acc + jnp.dot(act.astype(BF16), wdn_bf[lo:hi, :], preferred_element_type=F32)
        ybuf[slot] = _pack_halves(acc)
        start_scatter(i)

        @pl.when(i == na - 1)
        def _():
            @pl.when(i >= 1)
            def _():
                wait_rows(i - 1, sem_out, False)
            wait_rows(i, sem_out, False)


def _moe_experts(plan, xsl, w_gu, b_gu, w_dn, b_dn):
    block_e = plan[0]
    nblk = block_e.shape[0]
    spare_row0 = xsl.shape[0] - MOE_RL
    assert MOE_RL >= 2 * MOE_BM
    wmap = lambda i, be, *_: (be[i], 0, 0)
    anyspec = pl.BlockSpec(memory_space=pl.ANY)
    gs = pltpu.PrefetchScalarGridSpec(
        num_scalar_prefetch=len(plan),
        grid=(nblk,),
        in_specs=[anyspec,
                  anyspec,
                  pl.BlockSpec((None, 1, 2 * D_EXPERT), wmap),
                  anyspec,
                  pl.BlockSpec((None, 1, D_MODEL), wmap)],
        out_specs=anyspec,
        scratch_shapes=[pltpu.VMEM((D_MODEL, 2 * D_EXPERT), BF16), pltpu.VMEM((D_EXPERT, D_MODEL), BF16),
                        pltpu.VMEM((2, MOE_BM, D_MODEL // 2), jnp.uint32),
                        pltpu.VMEM((2, MOE_BM, D_MODEL // 2), jnp.uint32),
                        pltpu.VMEM((2, D_MODEL, 2 * D_EXPERT), F32), pltpu.VMEM((2, D_EXPERT, D_MODEL), F32),
                        pltpu.SemaphoreType.DMA((2,)), pltpu.SemaphoreType.DMA((2,)),
                        pltpu.SemaphoreType.DMA((2,))],
    )
    return pl.pallas_call(
        functools.partial(_moe_kernel, trash_row0=spare_row0),
        grid_spec=gs,
        out_shape=jax.ShapeDtypeStruct(xsl.shape, jnp.uint32),
        input_output_aliases={len(plan): 0},
        compiler_params=_cparams(("arbitrary",)),
        name="moe_experts",
    )(*plan, xsl, w_gu, b_gu.reshape(N_EXPERTS, 1, -1), w_dn, b_dn.reshape(N_EXPERTS, 1, -1))


def _combine_kernel(ysl_ref, info_ref, x1_ref, mod_ref, y_ref, *, tm):
    info = info_ref[...]
    info_t = jnp.transpose(jnp.concatenate([info, jnp.zeros((LANES - info.shape[0], tm), F32)], axis=0))
    ridx = lax.broadcasted_iota(jnp.int32, (tm, MOE_RL), 1).astype(F32)
    pg = jnp.zeros((tm, MOE_RL), F32)
    for k in range(TOP_K):
        pg = jnp.where(ridx == info_t[:, k:k + 1], info_t[:, TOP_K + k:TOP_K + k + 1], pg)
    pgb = pg.astype(BF16)
    yh, yl = _unpack_halves(ysl_ref[...])
    half = D_MODEL // 2
    gt2 = mod_ref[:, 5 * D_MODEL:6 * D_MODEL]
    for c, yy in ((0, yh), (1, yl)):
        moe = jnp.dot(pgb, yy, preferred_element_type=F32)
        y_ref[:, c * half:(c + 1) * half] = (x1_ref[:, c * half:(c + 1) * half]
                                             + gt2[:, c * half:(c + 1) * half] * moe)


def _combine(ysl, info, x1, mod3, tiles_per_mod, tile0=0):
    m = x1.shape[0]
    tm = MOE_TM
    r = mod3.shape[1]
    return pl.pallas_call(
        functools.partial(_combine_kernel, tm=tm),
        grid=(m // tm,),
        in_specs=[pl.BlockSpec((MOE_RL, D_MODEL // 2), lambda i: (tile0 + i, 0)),
                  pl.BlockSpec((16, tm), lambda i: (0, tile0 + i)),
                  pl.BlockSpec((tm, D_MODEL), lambda i: (i, 0)),
                  pl.BlockSpec((None, r, 6 * D_MODEL), lambda i: (i // tiles_per_mod, 0, 0))],
        out_specs=pl.BlockSpec((tm, D_MODEL), lambda i: (i, 0)),
        out_shape=jax.ShapeDtypeStruct((m, D_MODEL), F32),
        compiler_params=_cparams(("parallel",)),
        name="moe_combine",
    )(ysl, info, x1, mod3)


def _moe_plan(cnt):
    cnt = cnt.astype(jnp.int32)
    nt = cnt.shape[0]
    so = jnp.cumsum(cnt, axis=1) - cnt + (jnp.arange(nt) * MOE_RL)[:, None]
    ce = jnp.cumsum(cnt, axis=0)
    cs = ce - cnt
    tot = ce[-1]
    nblk_e = (tot + MOE_BM - 1) // MOE_BM
    blk_end = jnp.cumsum(nblk_e)
    max_rows = nt * MOE_TM * TOP_K + nt * N_EXPERTS * (SEG_ALIGN - 1)
    n_blocks = -(-max_rows // MOE_BM) + N_EXPERTS
    bidx = jnp.arange(n_blocks)
    block_e = jnp.minimum(jnp.sum(blk_end[None, :] <= bidx[:, None], axis=1), N_EXPERTS - 1).astype(jnp.int32)
    is_e = (jnp.arange(N_EXPERTS)[:, None] == block_e[None, :]).astype(jnp.int32)
    per_block = lambda a: jnp.sum(a[..., :, None] * is_e, axis=-2)
    block_r0 = (bidx - per_block(blk_end - nblk_e)) * MOE_BM
    x = block_r0[:, None] + jnp.arange(MOE_BM // SEG_ALIGN)[None, :] * SEG_ALIGN
    ce_b = per_block(ce)[:, :, None]
    cs_b = per_block(cs)[:, :, None]
    inside = (cs_b <= x[None]) & (x[None] < ce_b)
    grp = x + jnp.sum(jnp.where(inside, per_block(so - cs)[:, :, None], 0), axis=0)
    grp = jnp.where(x < per_block(tot)[:, None], grp, -1)
    n_active = blk_end[-1].reshape(1)
    used = nblk_e > 0
    first = (bidx == per_block(blk_end - nblk_e)) & (bidx < n_active[0])
    wslot = per_block(jnp.cumsum(used) - 1) % 2
    eidx = jnp.arange(N_EXPERTS)
    later_used = used[None, :] & (eidx[None, :] > eidx[:, None])
    nxt_e = jnp.min(jnp.where(later_used, eidx[None, :], N_EXPERTS), axis=1)
    nxt = per_block(jnp.where(nxt_e < N_EXPERTS, nxt_e, -1))
    i32 = lambda a: a.reshape(-1).astype(jnp.int32)
    return block_e, i32(n_active), i32(grp), i32(first), i32(wslot), i32(nxt)


def _rope_tables(pos):
    half = ROT_DIM // 2
    inv = ROPE_THETA ** (-jnp.arange(half, dtype=F32) * (2.0 / ROT_DIM))
    ang = pos.astype(F32)[:, None] * inv[None, :]
    cos, sin = jnp.cos(ang), jnp.sin(ang)
    n = pos.shape[0]
    ones = jnp.ones((n, A_DH - ROT_DIM), F32)
    zeros_h = jnp.zeros((n, half), F32)
    zeros_r = jnp.zeros((n, A_DH - ROT_DIM), F32)
    cos64 = jnp.concatenate([cos, cos, ones], axis=1)
    sprev64 = jnp.concatenate([zeros_h, sin, zeros_r], axis=1)
    snext64 = jnp.concatenate([-sin, zeros_h, zeros_r], axis=1)
    two = lambda a: jnp.concatenate([a, a], axis=1)
    return two(cos64), two(sprev64), two(snext64)


def _prep_weights(w_in, b_in, q_norm_g, k_norm_g, cmp_pe_k, cmp_pe_v, cmp_w_k, cmp_w_v,
                  w_up_m, w_up_a, w_out, w_router, b_router):
    b2 = b_in.reshape(1, N_IN)
    wm = w_in[:, OFF_MQ:OFF_MO].astype(BF16)
    bm = b2[:, OFF_MQ:OFF_MO]
    wq = w_in[:, OFF_AQ:OFF_AKV].astype(BF16)
    bq = b2[:, OFF_AQ:OFF_AKV]
    wkv = w_in[:, OFF_AKV:OFF_AG].astype(BF16)
    bkv = b2[:, OFF_AKV:OFF_AG]
    n_small = 2 * M_HEADS + 3 * A_HEADS
    ws = jnp.concatenate([w_in[:, OFF_MI:OFF_AQ], w_in[:, OFF_AG:OFF_GA],
                          jnp.zeros((D_MODEL, LANES - n_small), F32)], axis=1)
    bs = jnp.concatenate([b2[:, OFF_MI:OFF_AQ], b2[:, OFF_AG:OFF_GA], jnp.zeros((1, LANES - n_small), F32)], axis=1)
    qg = jnp.tile(q_norm_g, A_HEADS).reshape(1, A_WIDTH)
    kg = jnp.stack([jnp.tile(k_norm_g[1], A_KV), jnp.tile(k_norm_g[2], A_KV)], axis=0)
    kg0 = jnp.tile(k_norm_g[0], A_KV).reshape(1, LANES)
    hid = jnp.arange(A_WIDTH) // A_DH
    bd = jnp.where(hid[:, None] == hid[None, :], 1.0 / A_DH, 0.0).astype(BF16)
    inproj_w = (wm, bm, wq, bq, wkv, bkv, ws, bs, qg, kg, bd)

    z = jnp.zeros((CMP_LEN, A_DH, A_DH), F32)
    r0 = jnp.concatenate([cmp_w_k, z, z, z], axis=2)
    r1 = jnp.concatenate([z, cmp_w_k, z, z], axis=2)
    r2 = jnp.concatenate([z, z, cmp_w_v, z], axis=2)
    r3 = jnp.concatenate([z, z, z, cmp_w_v], axis=2)
    wbd = jnp.concatenate([r0, r1, r2, r3], axis=1).astype(BF16)
    pe = jnp.concatenate([cmp_pe_k, cmp_pe_k, cmp_pe_v, cmp_pe_v], axis=1).reshape(CMP_LEN, 1, 2 * LANES)

    wog = jnp.concatenate([w_in[:, OFF_MO:OFF_MI], w_in[:, OFF_GA:N_IN]], axis=1).astype(BF16)
    bog = jnp.concatenate([b2[:, OFF_MO:OFF_MI], b2[:, OFF_GA:N_IN]], axis=1)
    mixout_w = (wog, bog, w_up_m.astype(BF16), w_up_a.astype(BF16), w_out.astype(BF16),
                w_router.T, b_router.reshape(N_EXPERTS, 1))
    return inproj_w, (wbd, pe, kg0), mixout_w


def _pick_tile(m, pref):
    t = pref
    while m % t:
        t //= 2
    return t


def kernel(x_prompt, x_sample, cache_nsa_kv, state_win_kv, state_mlstm_C, state_mlstm_n, state_mlstm_m, page_table, c_prompt, c_sample, w_ada, b_ada, g_mix, g_ffn, w_in, b_in, q_norm_g, k_norm_g, cmp_pe_k, cmp_pe_v, cmp_w_k, cmp_w_v, w_up_m, w_up_a, w_out, w_router, b_router, w_gu, b_gu, w_dn, b_dn):
    depth = w_in.shape[0]
    assert depth == 1
    B, T, D = x_prompt.shape
    DB, TS, _ = x_sample.shape
    n_pages = page_table.shape[1]
    past_len = n_pages * PAGE_SIZE
    wbuf = state_win_kv.shape[2]
    tp = SAMPLE_PAD_T
    assert TS <= tp and wbuf % tp == 0 and T % 128 == 0

    l = 0
    inproj_w, cmp_w, mixout_w = _prep_weights(
        w_in[l], b_in[l], q_norm_g[l], k_norm_g[l], cmp_pe_k[l], cmp_pe_v[l], cmp_w_k[l], cmp_w_v[l],
        w_up_m[l], w_up_a[l], w_out[l], w_router[l], b_router[l])
    wbd, pe, kg0 = cmp_w
    gmix = g_mix[l].reshape(1, D)
    gffn = g_ffn[l].reshape(1, D)

    nc = B + DB
    nc_pad = -(-nc // SUBLANES) * SUBLANES
    c_all = jnp.concatenate([c_prompt, c_sample, jnp.zeros((nc_pad - nc, D), F32)], axis=0)
    mod = _adaln(c_all, w_ada[l], b_ada[l])
    mod_p = mod[:B].reshape(B, 1, 6 * D)
    mod_s = jnp.repeat(mod[B:B + DB], tp, axis=0).reshape(1, DB * tp, 6 * D)

    mp = B * T
    tm = _pick_tile(T, 256)
    xp = x_prompt.reshape(mp, D)
    tabs_p = _rope_tables(jnp.arange(T, dtype=jnp.int32))
    mq, mk, mv, q, qr, rows, win, small, rows_t, win_t = _inproj(xp, mod_p, gmix, tabs_p, inproj_w, tm, T // tm,
                                                                 T // tm, rows_t_batches=B)
    Lp = _pick_tile(T, 128)
    hm, C_p, n_p, m_p = _mlstm(mq, mk, mv, small, B, T, T, Lp)
    o_nsa = _nsa_prompt(q, qr, small, rows, win, wbd, pe, kg0, B, T)
    assert T % MOE_TM == 0
    ms_pad = -(-(DB * tp) // MOE_TM) * MOE_TM
    nt_p = mp // MOE_TM
    nt_all = nt_p + ms_pad // MOE_TM + 1
    x1_p, xsl, info, cnt = _mixout(xp, hm, o_nsa, mod_p, gmix, gffn, mixout_w, T // MOE_TM, nt_all)

    ms = DB * tp
    xs_pad = jnp.concatenate([x_sample, jnp.zeros((DB, tp - TS, D), F32)], axis=1).reshape(ms, D)
    pos_s = past_len + jnp.tile(jnp.arange(tp, dtype=jnp.int32), DB)
    tabs_s = _rope_tables(pos_s)
    mq_s, mk_s, mv_s, q_s, qr_s, rows_s, win_s, small_s = _inproj(xs_pad, mod_s, gmix, tabs_s, inproj_w, ms, 1, 1)
    hm_s, C_s, n_s, m_s = _mlstm(mq_s, mk_s, mv_s, small_s, DB, tp, TS, tp,
                                 state=(state_mlstm_C[l], state_mlstm_n[l], state_mlstm_m[l]))
    cache2 = jnp.transpose(cache_nsa_kv[l], (0, 2, 3, 4, 1)).reshape(cache_nsa_kv.shape[1], 4 * LANES, PAGE_SIZE)
    winbuf = jnp.transpose(state_win_kv[l], (0, 2, 3, 4, 1)).reshape(DB, 2 * LANES, wbuf)
    o_nsa_s, win_out_s = _nsa_sample(page_table, cache2, q_s, qr_s, small_s, rows_s, win_s, winbuf,
                                     wbd, pe, kg0, TS)
    assert ms_pad == MOE_TM
    rpad = lambda a: jnp.concatenate([a, jnp.zeros((ms_pad - ms, a.shape[1]), a.dtype)], axis=0) if ms_pad > ms else a
    mod_sp = rpad(mod_s[0])[None]
    x1_s, xsl, info, cnt = _mixout(rpad(xs_pad), rpad(hm_s), rpad(o_nsa_s), mod_sp, gmix, gffn, mixout_w,
                                   1, nt_all, tile0=nt_p, shared=(xsl, info, cnt),
                                   t_mod=tp, t_valid=TS, m_valid=ms)

    ysl = _moe_experts(_moe_plan(cnt[:, :, 0]), xsl, w_gu[l], b_gu[l], w_dn[l], b_dn[l])
    y_p = _combine(ysl, info, x1_p, mod_p, T // MOE_TM).reshape(B, T, D)
    y_s_all = _combine(ysl, info, x1_s, mod_sp, 1, tile0=nt_p)
    valid = lambda a: a.reshape(DB, tp, -1)[:, :TS].reshape(DB * TS, -1)
    y_s = valid(y_s_all[:ms]).reshape(DB, TS, D)

    kv_p = jnp.transpose(rows_t.reshape(B, 4, A_KV, A_DH, T), (0, 4, 1, 2, 3))[None]
    kv_s = valid(rows_s).reshape(1, DB, TS, 4, A_KV, A_DH)
    wp = min(WINDOW, T)
    win_p = jnp.transpose(win_t[:, :, T - wp:].reshape(B, 2, A_KV, A_DH, wp), (0, 4, 1, 2, 3))[None]
    win_s_out = jnp.transpose(win_out_s.reshape(DB, 2, A_KV, A_DH, wbuf), (0, 4, 1, 2, 3))[None]
    return (y_p, y_s, kv_p, kv_s, win_p, win_s_out,
            C_p[None], n_p[None], m_p[None], C_s[None], n_s[None], m_s[None])
```

```python
import functools

import jax
import jax.numpy as jnp
from jax import lax
from jax.experimental import pallas as pl
from jax.experimental.pallas import tpu as pltpu

F32 = jnp.float32
BF16 = jnp.bfloat16

D_MODEL = 1024
M_HEADS = 4
M_DH = 128
M_WIDTH = M_HEADS * M_DH
A_HEADS = 8
A_KV = 2
A_HPG = A_HEADS // A_KV
A_DH = 64
A_WIDTH = A_HEADS * A_DH
CMP_STRIDE = 16
CMP_LEN = 32
SEL_LEN = 64
N_SEL = 16
WINDOW = 512
PAGE_SIZE = 128
ROPE_THETA = 500000.0
ROT_DIM = A_DH // 4
ATT_SCALE = A_DH ** -0.5
N_EXPERTS = 32
TOP_K = 4
D_EXPERT = D_MODEL
SWIGLU_LIMIT = 7.0
SWIGLU_ALPHA = 1.702
EPS = 1e-6

OFF_MQ, OFF_MK, OFF_MV, OFF_MO = 0, M_WIDTH, 2 * M_WIDTH, 3 * M_WIDTH
OFF_MI = 4 * M_WIDTH
OFF_MF = OFF_MI + M_HEADS
OFF_AQ = OFF_MF + M_HEADS
OFF_AKV = OFF_AQ + A_WIDTH
OFF_AG = OFF_AKV + 6 * A_KV * A_DH
OFF_GA = OFF_AG + 3 * A_HEADS
OFF_GB = OFF_GA + D_MODEL
N_IN = OFF_GB + D_MODEL

LANES = 128
SUBLANES = 8
VMEM_LIMIT = 56 * 1024 * 1024

NEG_BIG = -1e30
M_INIT = -1e29
LOG2E = 1.4426950408889634
SAMPLE_PAD_T = 8


def _cparams(sem):
    return pltpu.CompilerParams(dimension_semantics=sem, vmem_limit_bytes=VMEM_LIMIT)


def _bdot(a, b):
    return jnp.dot(a.astype(BF16), b.astype(BF16), preferred_element_type=F32)


def _bdot_t(a, b):
    return lax.dot_general(a.astype(BF16), b.astype(BF16), (((1,), (1,)), ((), ())),
                           preferred_element_type=F32)


def _split(a):
    hi = a.astype(BF16)
    lo = (a - hi.astype(F32)).astype(BF16)
    return hi, lo


def _dot3(a, b):
    ah, al = _split(a)
    bh, bl = _split(b)
    return (jnp.dot(ah, bh, preferred_element_type=F32) + jnp.dot(al, bh, preferred_element_type=F32)
            + jnp.dot(ah, bl, preferred_element_type=F32))


def _dot2_exact_rhs(a, b_bf16):
    ah, al = _split(a)
    return jnp.dot(ah, b_bf16, preferred_element_type=F32) + jnp.dot(al, b_bf16, preferred_element_type=F32)


def _sigmoid(x):
    return 0.5 * jnp.tanh(0.5 * x) + 0.5


def _rmsnorm_rows(x, g):
    return x * lax.rsqrt(jnp.mean(x * x, axis=-1, keepdims=True) + EPS) * g


def _adaln_kernel(c_ref, w_ref, b_ref, o_ref):
    c = c_ref[...]
    s = c * _sigmoid(c)
    o_ref[...] = _dot3(s, w_ref[...]) + b_ref[...]


def _adaln(c, w, b):
    mc, d = c.shape
    n = w.shape[1]
    tn = 1024
    return pl.pallas_call(
        _adaln_kernel,
        grid=(n // tn,),
        in_specs=[pl.BlockSpec((mc, d), lambda j: (0, 0)),
                  pl.BlockSpec((d, tn), lambda j: (0, j)),
                  pl.BlockSpec((1, tn), lambda j: (0, j))],
        out_specs=pl.BlockSpec((mc, tn), lambda j: (0, j)),
        out_shape=jax.ShapeDtypeStruct((mc, n), F32),
        compiler_params=_cparams(("parallel",)),
        name="adaln",
    )(c, w, b.reshape(1, n))


def _head_norm(z, bd, gain):
    ms = _dot2_exact_rhs(z * z, bd)
    return z * lax.rsqrt(ms + EPS) * gain


def _rope(z, cos, s_prev, s_next):
    w = z.shape[1]
    rep = w // LANES
    if rep > 1:
        cos = jnp.concatenate([cos] * rep, axis=1)
        s_prev = jnp.concatenate([s_prev] * rep, axis=1)
        s_next = jnp.concatenate([s_next] * rep, axis=1)
    z_prev = pltpu.roll(z, ROT_DIM // 2, 1)
    z_next = pltpu.roll(z, w - ROT_DIM // 2, 1)
    return z * cos + z_prev * s_prev + z_next * s_next


def _inproj_kernel(x_ref, mod_ref, gmix_ref, cos_ref, sp_ref, sn_ref,
                   wm_ref, bm_ref, wq_ref, bq_ref, wkv_ref, bkv_ref, ws_ref, bs_ref,
                   qg_ref, kg_ref, bd_ref,
                   mq_ref, mk_ref, mv_ref, q_ref, qr_ref, rows_ref, win_ref, small_ref,
                   rows_t_ref=None, win_t_ref=None):
    x = x_ref[...]
    sh1 = mod_ref[:, 0:D_MODEL]
    sc1 = mod_ref[:, D_MODEL:2 * D_MODEL]
    h = _rmsnorm_rows(x, gmix_ref[...]) * (1.0 + sc1) + sh1
    hb = h.astype(BF16)

    mq_ref[...] = jnp.dot(hb, wm_ref[:, 0:M_WIDTH], preferred_element_type=F32) + bm_ref[:, 0:M_WIDTH]
    mk = jnp.dot(hb, wm_ref[:, M_WIDTH:2 * M_WIDTH], preferred_element_type=F32) + bm_ref[:, M_WIDTH:2 * M_WIDTH]
    mk_ref[...] = mk * (M_DH ** -0.5)
    mv_ref[...] = (jnp.dot(hb, wm_ref[:, 2 * M_WIDTH:3 * M_WIDTH], preferred_element_type=F32)
                   + bm_ref[:, 2 * M_WIDTH:3 * M_WIDTH])

    cos, sp, sn = cos_ref[...], sp_ref[...], sn_ref[...]
    zq = jnp.dot(hb, wq_ref[...], preferred_element_type=F32) + bq_ref[...]
    qn = _head_norm(zq, bd_ref[...], qg_ref[...])
    q_ref[...] = qn
    qr_ref[...] = _rope(qn, cos, sp, sn)

    zkv = jnp.dot(hb, wkv_ref[...], preferred_element_type=F32) + bkv_ref[...]
    bd2 = bd_ref[0:LANES, 0:LANES]
    ksel = _head_norm(zkv[:, 2 * LANES:3 * LANES], bd2, kg_ref[0:1, :])
    rows = jnp.concatenate([zkv[:, 0:2 * LANES], _rope(ksel, cos, sp, sn), zkv[:, 3 * LANES:4 * LANES]], axis=1)
    rows_ref[...] = rows
    if rows_t_ref is not None:
        rows_t_ref[...] = jnp.transpose(rows)
    kwin = _head_norm(zkv[:, 4 * LANES:5 * LANES], bd2, kg_ref[1:2, :])
    win = jnp.concatenate([_rope(kwin, cos, sp, sn), zkv[:, 5 * LANES:6 * LANES]], axis=1)
    win_ref[...] = win
    if win_t_ref is not None:
        win_t_ref[...] = jnp.transpose(win)

    small_ref[...] = _dot3(h, ws_ref[...]) + bs_ref[...]


def _inproj(x2, mod3, gmix, tabs, wts, tm, tiles_per_mod, pos_tiles, rows_t_batches=None):
    m = x2.shape[0]
    cos_t, sp_t, sn_t = tabs
    (wm, bm, wq, bq, wkv, bkv, ws, bs, qg, kg, bd) = wts
    r = mod3.shape[1]
    row = lambda i: (i, 0)
    const = lambda i: (0, 0)
    tab = lambda i: (i % pos_tiles, 0)
    in_specs = [
        pl.BlockSpec((tm, D_MODEL), row),
        pl.BlockSpec((None, r, 6 * D_MODEL), lambda i: (i // tiles_per_mod, 0, 0)),
        pl.BlockSpec((1, D_MODEL), const),
        pl.BlockSpec((tm, LANES), tab), pl.BlockSpec((tm, LANES), tab), pl.BlockSpec((tm, LANES), tab),
        pl.BlockSpec(wm.shape, const), pl.BlockSpec(bm.shape, const),
        pl.BlockSpec(wq.shape, const), pl.BlockSpec(bq.shape, const),
        pl.BlockSpec(wkv.shape, const), pl.BlockSpec(bkv.shape, const),
        pl.BlockSpec(ws.shape, const), pl.BlockSpec(bs.shape, const),
        pl.BlockSpec(qg.shape, const), pl.BlockSpec(kg.shape, const), pl.BlockSpec(bd.shape, const),
    ]
    widths = (M_WIDTH, M_WIDTH, M_WIDTH, A_WIDTH, A_WIDTH, 4 * LANES, 2 * LANES, LANES)
    out_specs = [pl.BlockSpec((tm, w), row) for w in widths]
    out_shape = [jax.ShapeDtypeStruct((m, w), F32) for w in widths]
    if rows_t_batches is not None:
        for w in (4 * LANES, 2 * LANES):
            out_specs.append(pl.BlockSpec((None, w, tm), lambda i: (i // tiles_per_mod, 0, i % tiles_per_mod)))
            out_shape.append(jax.ShapeDtypeStruct((rows_t_batches, w, m // rows_t_batches), F32))
    return pl.pallas_call(
        _inproj_kernel,
        grid=(m // tm,),
        in_specs=in_specs,
        out_specs=out_specs,
        out_shape=out_shape,
        compiler_params=_cparams(("parallel",)),
        name="inproj",
    )(x2, mod3, gmix, cos_t, sp_t, sn_t, wm, bm, wq, bq, wkv, bkv, ws, bs, qg, kg, bd)


def _log_sigmoid(x):
    return jnp.minimum(x, 0.0) - jnp.log(1.0 + jnp.exp(-jnp.abs(x)))


def _mlstm_kernel(*refs, L, t_valid, has_state, nseq):
    if has_state:
        q_ref, k_ref, v_ref, s_ref, c0_ref, n0_ref, m0_ref, h_ref, c_ref, n_ref, m_ref = refs
    else:
        q_ref, k_ref, v_ref, s_ref, h_ref, c_ref, n_ref, m_ref = refs
    c = pl.program_id(1)

    @pl.when(c == 0)
    def _():
        if has_state:
            c_ref[...] = c0_ref[...]
            n_ref[...] = n0_ref[...]
            m_ref[...] = m0_ref[...]
        else:
            c_ref[...] = jnp.zeros(c_ref.shape, F32)
            n_ref[...] = jnp.zeros(n_ref.shape, F32)
            m_ref[...] = jnp.zeros(m_ref.shape, F32)

    row = lax.broadcasted_iota(jnp.int32, (L, L), 0)
    col = lax.broadcasted_iota(jnp.int32, (L, L), 1)
    causal = col <= row
    eye = col == row
    tok_col = c * L + lax.broadcasted_iota(jnp.int32, (L, 1), 0)
    valid_col = tok_col < t_valid
    for sq, hd in [(a, b) for a in range(nseq) for b in range(M_HEADS)]:
        lo, hi = hd * M_DH, (hd + 1) * M_DH
        rs = slice(sq * L, (sq + 1) * L)
        q = q_ref[rs, lo:hi]
        k = k_ref[rs, lo:hi]
        v = v_ref[rs, lo:hi]
        i_col = s_ref[rs, hd:hd + 1]
        lf_col = _log_sigmoid(s_ref[rs, M_HEADS + hd:M_HEADS + hd + 1])
        lf_col = jnp.where(valid_col, lf_col, 0.0)
        i_col = jnp.where(valid_col, i_col, -jnp.inf)
        if L == LANES:
            i_col = jnp.broadcast_to(i_col, (L, L))
            lf_c = jnp.broadcast_to(lf_col, (L, L))
            p0 = lf_c.astype(BF16)
            r1 = lf_c - p0.astype(F32)
            p1 = r1.astype(BF16)
            p2 = (r1 - p1.astype(F32)).astype(BF16)
            tril = jnp.where(causal, 1.0, 0.0).astype(BF16)
            b_col = (jnp.dot(tril, p0, preferred_element_type=F32) + jnp.dot(tril, p1, preferred_element_type=F32)
                     + jnp.dot(tril, p2, preferred_element_type=F32))
            i_row = jnp.transpose(i_col)[0:1, :]
            b_row = jnp.transpose(b_col)[0:1, :]
        else:
            i_row = jnp.sum(jnp.where(eye, i_col, 0.0), axis=0, keepdims=True)
            lf_row = jnp.sum(jnp.where(eye, lf_col, 0.0), axis=0, keepdims=True)
            b_col = jnp.sum(jnp.where(causal, lf_row, 0.0), axis=1, keepdims=True)
            b_row = jnp.sum(jnp.where(row <= col, lf_col, 0.0), axis=0, keepdims=True)
        m_prev = m_ref[sq, :, hd:hd + 1]
        dmat = jnp.where(causal, b_col - b_row + i_row, -jnp.inf)
        inter = b_col + m_prev
        m_row = jnp.maximum(jnp.max(dmat, axis=1, keepdims=True), inter)
        w = jnp.exp(dmat - m_row)
        w_inter = jnp.exp(inter - m_row)
        s = _bdot_t(q, k) * w
        cm = c_ref[sq, hd]
        nv = n_ref[sq, hd]
        num = _bdot(s, v) + w_inter * _bdot_t(q, cm)
        den = jnp.sum(s, axis=1, keepdims=True) + w_inter * jnp.sum(q * nv, axis=1, keepdims=True)
        h_ref[rs, lo:hi] = num / jnp.maximum(jnp.abs(den), jnp.exp(-m_row))
        b_last = b_col[L - 1:L, 0:1]
        dec_col = b_last - b_col + i_col
        dec_row = b_last - b_row + i_row
        m_new = jnp.maximum(b_last + m_prev, jnp.max(dec_row, axis=1, keepdims=True))
        ws_col = jnp.exp(dec_col - m_new)
        wc = jnp.exp(b_last + m_prev - m_new)
        vw = (v * ws_col).astype(BF16)
        upd = lax.dot_general(vw, k.astype(BF16), (((0,), (0,)), ((), ())), preferred_element_type=F32)
        c_ref[sq, hd] = wc * cm + upd
        n_ref[sq, hd] = wc * nv + jnp.sum(k * ws_col, axis=0, keepdims=True)
        m_ref[sq, :, hd:hd + 1] = m_new


def _mlstm(mq, mk, mv, small, nb, t_pad, t_valid, L, state=None, nseq=1):
    nc = t_pad // L
    assert nseq == 1 or (nc == 1 and nb % nseq == 0)
    has_state = state is not None
    rows = nseq * L
    blk = lambda b, c: (b * nc + c, 0)
    st4 = lambda b, c: (b, 0, 0, 0)
    st3 = lambda b, c: (b, 0, 0)
    in_specs = [pl.BlockSpec((rows, M_WIDTH), blk)] * 3 + [pl.BlockSpec((rows, LANES), blk)]
    args = [mq, mk, mv, small]
    if has_state:
        c0, n0, m0 = state
        in_specs += [pl.BlockSpec((nseq, M_HEADS, M_DH, M_DH), st4),
                     pl.BlockSpec((nseq, M_HEADS, 1, M_DH), st4),
                     pl.BlockSpec((nseq, 1, M_HEADS), st3)]
        args += [c0, n0.reshape(nb, M_HEADS, 1, M_DH), m0.reshape(nb, 1, M_HEADS)]
    out_specs = [pl.BlockSpec((rows, M_WIDTH), blk),
                 pl.BlockSpec((nseq, M_HEADS, M_DH, M_DH), st4),
                 pl.BlockSpec((nseq, M_HEADS, 1, M_DH), st4),
                 pl.BlockSpec((nseq, 1, M_HEADS), st3)]
    out_shape = [jax.ShapeDtypeStruct((nb * t_pad, M_WIDTH), F32),
                 jax.ShapeDtypeStruct((nb, M_HEADS, M_DH, M_DH), F32),
                 jax.ShapeDtypeStruct((nb, M_HEADS, 1, M_DH), F32),
                 jax.ShapeDtypeStruct((nb, 1, M_HEADS), F32)]
    h, cs, ns, ms = pl.pallas_call(
        functools.partial(_mlstm_kernel, L=L, t_valid=t_valid, has_state=has_state, nseq=nseq),
        grid=(nb // nseq, nc),
        in_specs=in_specs,
        out_specs=out_specs,
        out_shape=out_shape,
        compiler_params=_cparams(("parallel", "arbitrary")),
        name="mlstm",
    )(*args)
    return h, cs, ns.reshape(nb, M_HEADS, M_DH), ms.reshape(nb, M_HEADS)


def _stack_heads(qt, g):
    t = qt.shape[0]
    z = jnp.zeros((t, A_DH), F32)
    parts = []
    for hh in range(A_HPG):
        hd = g * A_HPG + hh
        qh = qt[:, hd * A_DH:(hd + 1) * A_DH] * (ATT_SCALE * LOG2E)
        parts.append(jnp.concatenate([qh, z], axis=1) if g == 0 else jnp.concatenate([z, qh], axis=1))
    return jnp.concatenate(parts, axis=0).astype(BF16)


def _gate_cols(small, g, br):
    cols = []
    for hh in range(A_HPG):
        c0 = 2 * M_HEADS + (g * A_HPG + hh) * 3 + br
        cols.append(_sigmoid(small[:, c0:c0 + 1]))
    return jnp.concatenate(cols, axis=0)


def _compress(k_ref, v_ref, nseg, wbd_ref, pe_ref, kg0):
    acc_lo = jnp.zeros((nseg, 2 * LANES), F32)
    acc_hi = jnp.zeros((nseg, 2 * LANES), F32)
    for l in range(CMP_STRIDE):
        xl = jnp.concatenate([k_ref[pl.ds(l, nseg, stride=CMP_STRIDE), :],
                              v_ref[pl.ds(l, nseg, stride=CMP_STRIDE), :]], axis=1)
        acc_lo = acc_lo + _bdot(xl + pe_ref[l], wbd_ref[l])
        acc_hi = acc_hi + _bdot(xl + pe_ref[CMP_STRIDE + l], wbd_ref[CMP_STRIDE + l])
    return _compress_finish(acc_lo, acc_hi, nseg, kg0)


def _compress_grouped(x_ref, nseg, wbd_ref, pe_ref, kg0):
    acc_lo = jnp.zeros((nseg, 2 * LANES), F32)
    acc_hi = jnp.zeros((nseg, 2 * LANES), F32)
    pe_lo = jnp.zeros((SUBLANES, 2 * LANES), F32)
    pe_hi = jnp.zeros((SUBLANES, 2 * LANES), F32)
    for l in range(CMP_STRIDE):
        xl = x_ref[l].astype(BF16)
        acc_lo = acc_lo + jnp.dot(xl, wbd_ref[l], preferred_element_type=F32)
        acc_hi = acc_hi + jnp.dot(xl, wbd_ref[CMP_STRIDE + l], preferred_element_type=F32)
        pe_lo = pe_lo + _bdot(jnp.broadcast_to(pe_ref[l], (SUBLANES, 2 * LANES)), wbd_ref[l])
        pe_hi = pe_hi + _bdot(jnp.broadcast_to(pe_ref[CMP_STRIDE + l], (SUBLANES, 2 * LANES)),
                              wbd_ref[CMP_STRIDE + l])
    return _compress_finish(acc_lo + pe_lo[0:1, :], acc_hi + pe_hi[0:1, :], nseg, kg0)


def _compress_finish(acc_lo, acc_hi, nseg, kg0):
    kv = acc_lo + pltpu.roll(acc_hi, nseg - 1, 0)
    kc = kv[:, 0:LANES]
    vc = kv[:, LANES:2 * LANES]
    lane = lax.broadcasted_iota(jnp.int32, (nseg, LANES), 1)
    sq = kc * kc
    ms0 = jnp.sum(jnp.where(lane < A_DH, sq, 0.0), axis=1, keepdims=True) * (1.0 / A_DH)
    ms1 = jnp.sum(jnp.where(lane >= A_DH, sq, 0.0), axis=1, keepdims=True) * (1.0 / A_DH)
    ms = jnp.where(lane < A_DH, ms0, ms1)
    kc = kc * lax.rsqrt(ms + EPS) * kg0
    return kc, vc


def _cmp_branch(qn_g, kc_b, vc_b, tpos_rows, nseg, n_tok):
    s = _bdot_t(qn_g, kc_b)
    nidx = lax.broadcasted_iota(jnp.int32, (1, nseg), 1)
    vis = (nidx * CMP_STRIDE + (CMP_LEN - 1)) <= tpos_rows
    sm = jnp.where(vis, s, NEG_BIG)
    mx = jnp.max(sm, axis=1, keepdims=True)
    e = jnp.where(vis, jnp.exp2(sm - mx), 0.0)
    d = jnp.sum(e, axis=1, keepdims=True)
    p = e / jnp.where(d > 0, d, 1.0)
    o = _bdot(p, vc_b)
    imp = p[0:n_tok]
    for hh in range(1, A_HPG):
        imp = imp + p[hh * n_tok:(hh + 1) * n_tok]
    return o, imp


def _masked_attn_direct(q_g, k_parts, v_parts, allowed_parts, feature_major):
    ss = [jnp.where(al, _bdot(q_g, kk) if fm else _bdot_t(q_g, kk), NEG_BIG)
          for kk, al, fm in zip(k_parts, allowed_parts, feature_major)]
    mx = ss[0].max(axis=1, keepdims=True)
    for s in ss[1:]:
        mx = jnp.maximum(mx, s.max(axis=1, keepdims=True))
    num = None
    den = None
    for s, al, vv, fm in zip(ss, allowed_parts, v_parts, feature_major):
        e = jnp.where(al, jnp.exp2(s - mx), 0.0)
        dd = jnp.sum(e, axis=1, keepdims=True)
        oo = _bdot_t(e, vv) if fm else _bdot(e, vv)
        num = oo if num is None else num + oo
        den = dd if den is None else den + dd
    return num / jnp.where(den > 0, den, 1.0)


def _assemble_heads(o_groups, n_tok):
    pieces = []
    for g in range(A_KV):
        for hh in range(A_HPG):
            pieces.append(o_groups[g][hh * n_tok:(hh + 1) * n_tok, g * A_DH:(g + 1) * A_DH])
    return jnp.concatenate(pieces, axis=1)


def _lane_rep(a, rep):
    return a if rep == 1 else jnp.concatenate([a] * rep, axis=1)


def _nsa_prompt_kernel(q_ref, qr_ref, small_ref, rows_ref, win_ref, wbd_ref, pe_ref, kg0_ref,
                       pool_ref, o_ref,
                       kraw_sc, vraw_sc, kc_sc, vct_sc, sel_sc, m_sc, acc_sc, s_sc, *, T, tq, kc_len):
    qi = pl.program_id(1)
    nseg = T // CMP_STRIDE
    nsb = T // SEL_LEN
    bpc = kc_len // SEL_LEN

    @pl.when(qi == 0)
    def _():
        kraw_sc[...] = rows_ref[:, 0:LANES]
        vraw_sc[...] = rows_ref[:, LANES:2 * LANES]
        kc, vc = _compress(kraw_sc, vraw_sc, nseg, wbd_ref, pe_ref, kg0_ref[...])
        kc_sc[...] = kc
        vct_sc[...] = jnp.transpose(vc)

    t0 = qi * tq
    tpos = t0 + lax.broadcasted_iota(jnp.int32, (1, tq), 1)
    tpos4 = _lane_rep(tpos, A_HPG)
    q = q_ref[...]
    qr = qr_ref[...]
    small_t = jnp.transpose(small_ref[...])
    kc_b = kc_sc[...].astype(BF16)
    vct_b = vct_sc[...].astype(BF16)
    bidx = lax.broadcasted_iota(jnp.int32, (nsb, tq), 0)
    cur = tpos // SEL_LEN
    vis = (lax.broadcasted_iota(jnp.int32, (nseg, 1), 0) * CMP_STRIDE + (CMP_LEN - 1)) <= tpos4
    qr_gs = [_stack_heads(qr, g) for g in range(A_KV)]
    o_cmps = []
    for g in range(A_KV):
        sm = jnp.where(vis, _bdot_t(kc_b, _stack_heads(q, g)), NEG_BIG)
        mx = jnp.max(sm, axis=0, keepdims=True)
        e = jnp.where(vis, jnp.exp2(sm - mx), 0.0)
        d = jnp.sum(e, axis=0, keepdims=True)
        p = e / jnp.where(d > 0, d, 1.0)
        o_cmps.append(jnp.dot(vct_b, p.astype(BF16), preferred_element_type=F32))
        imp = p[:, 0:tq]
        for hh in range(1, A_HPG):
            imp = imp + p[:, hh * tq:(hh + 1) * tq]
        ih, il = _split(imp)
        imp_t = (jnp.dot(pool_ref[...], ih, preferred_element_type=F32)
                 + jnp.dot(pool_ref[...], il, preferred_element_type=F32))[0:nsb]
        val = jnp.where(bidx < cur, imp_t, -1.0)
        rank = jnp.zeros((nsb, tq), F32)
        for bp in range(nsb):
            vb = val[bp:bp + 1, :]
            rank = rank + jnp.where(vb > val, 1.0, jnp.where((vb == val) & (bidx > bp), 1.0, 0.0))
        sel_sc[g] = jnp.where(((rank < (N_SEL - 1)) & (bidx < cur)) | (bidx == cur), 1.0, 0.0)

    m_sc[...] = jnp.full(m_sc.shape, M_INIT, F32)
    acc_sc[...] = jnp.zeros(acc_sc.shape, F32)

    def with_ones_row(vt_, g):
        vb = vt_.astype(BF16)
        r0, pad = (1 - g) * A_DH, 2 * SUBLANES
        ones = jnp.ones((pad, vb.shape[1]), BF16)
        return jnp.concatenate(([vb[0:r0]] if r0 else []) + [ones, vb[r0 + pad:]], axis=0)

    def sel_body(c, carry):
        k0 = pl.multiple_of(c * kc_len, kc_len)
        kb = rows_ref[pl.ds(k0, kc_len), 2 * LANES:3 * LANES].astype(BF16)
        vt = jnp.transpose(rows_ref[pl.ds(k0, kc_len), 3 * LANES:4 * LANES])
        causal = (k0 + lax.broadcasted_iota(jnp.int32, (kc_len, 1), 0)) <= tpos
        for g in range(A_KV):
            s_sc[g, 0:kc_len, :] = _bdot_t(kb, qr_gs[g])
        for g in range(A_KV):
            selc = sel_sc[g, pl.ds(pl.multiple_of(c * bpc, bpc), bpc), :]
            selx = jnp.concatenate([jnp.broadcast_to(selc[j:j + 1, :], (SEL_LEN, tq)) for j in range(bpc)], axis=0)
            bias = jnp.where(causal & (selx > 0.5), 0.0, NEG_BIG)
            sm = s_sc[g, 0:kc_len, :] + _lane_rep(bias, A_HPG)
            m_prev = m_sc[g]
            m_new = jnp.maximum(m_prev, jnp.max(sm, axis=0, keepdims=True))
            alpha = jnp.exp2(m_prev - m_new)
            p = jnp.exp2(sm - m_new)
            acc_sc[g] = alpha * acc_sc[g] + jnp.dot(with_ones_row(vt, g), p.astype(BF16),
                                                    preferred_element_type=F32)
            m_sc[g] = m_new
        return carry

    lax.fori_loop(0, (t0 + tq + kc_len - 1) // kc_len, sel_body, 0)

    wk = min(WINDOW + tq, T)
    w0 = pl.multiple_of(jnp.clip(t0 + tq - wk, 0, T - wk), tq)
    kw = win_ref[pl.ds(w0, wk), 0:LANES].astype(BF16)
    vwt = jnp.transpose(win_ref[pl.ds(w0, wk), LANES:2 * LANES])
    wdiff = tpos - (w0 + lax.broadcasted_iota(jnp.int32, (wk, 1), 0))
    wbias = _lane_rep(jnp.where((wdiff >= 0) & (wdiff < WINDOW), 0.0, NEG_BIG), A_HPG)

    def gate_row(g, br):
        cols = [2 * M_HEADS + (g * A_HPG + hh) * 3 + br for hh in range(A_HPG)]
        return jnp.concatenate([_sigmoid(small_t[c0:c0 + 1, :]) for c0 in cols], axis=1)

    for g in range(A_KV):
        s_sc[g, 0:wk, :] = _bdot_t(kw, qr_gs[g])
    o_ts = []
    for g in range(A_KV):
        den = (1 - g) * A_DH
        acc = acc_sc[g]
        l = acc[den:den + 1, :]
        o_sel = acc / jnp.where(l > 0, l, 1.0)
        sw = s_sc[g, 0:wk, :] + wbias
        pw = jnp.exp2(sw - jnp.max(sw, axis=0, keepdims=True))
        ow = jnp.dot(with_ones_row(vwt, g), pw.astype(BF16), preferred_element_type=F32)
        o_win = ow / ow[den:den + 1, :]
        o_ts.append(gate_row(g, 0) * o_cmps[g] + gate_row(g, 1) * o_sel + gate_row(g, 2) * o_win)
    for j in range(A_HEADS // 2):
        g, h0 = j // (A_HPG // 2), 2 * (j % (A_HPG // 2))
        og = o_ts[g][g * A_DH:(g + 1) * A_DH, :]
        pair = jnp.concatenate([og[:, h0 * tq:(h0 + 1) * tq], og[:, (h0 + 1) * tq:(h0 + 2) * tq]], axis=0)
        o_ref[:, j * LANES:(j + 1) * LANES] = jnp.transpose(pair)


def _nsa_prompt(q, qr, small, rows, win, wbd, pe, kg0, nb, T):
    tq = 128
    kc_len = _pick_tile(T, 512)
    nq = T // tq
    nseg = T // CMP_STRIDE
    nsb = T // SEL_LEN
    nsb_p = -(-nsb // SUBLANES) * SUBLANES
    pool = (jnp.arange(nsb_p)[:, None] == jnp.arange(nseg)[None, :] // (SEL_LEN // CMP_STRIDE)).astype(BF16)
    tile = lambda b, i: (b * nq + i, 0)
    per_b = lambda b, i: (b, 0)
    c2 = lambda b, i: (0, 0)
    c3 = lambda b, i: (0, 0, 0)
    c4 = A_HPG * tq
    return pl.pallas_call(
        functools.partial(_nsa_prompt_kernel, T=T, tq=tq, kc_len=kc_len),
        grid=(nb, nq),
        in_specs=[pl.BlockSpec((tq, A_WIDTH), tile), pl.BlockSpec((tq, A_WIDTH), tile),
                  pl.BlockSpec((tq, LANES), tile),
                  pl.BlockSpec((T, 4 * LANES), per_b), pl.BlockSpec((T, 2 * LANES), per_b),
                  pl.BlockSpec(wbd.shape, c3), pl.BlockSpec(pe.shape, c3), pl.BlockSpec(kg0.shape, c2),
                  pl.BlockSpec(pool.shape, c2)],
        out_specs=pl.BlockSpec((tq, A_WIDTH), tile),
        out_shape=jax.ShapeDtypeStruct((nb * T, A_WIDTH), F32),
        scratch_shapes=[pltpu.VMEM((T, LANES), F32), pltpu.VMEM((T, LANES), F32),
                        pltpu.VMEM((nseg, LANES), F32), pltpu.VMEM((LANES, nseg), F32),
                        pltpu.VMEM((A_KV, nsb, tq), F32),
                        pltpu.VMEM((A_KV, 1, c4), F32),
                        pltpu.VMEM((A_KV, LANES, c4), F32),
                        pltpu.VMEM((A_KV, max(kc_len, min(WINDOW + tq, T)), c4), F32)],
        compiler_params=_cparams(("parallel", "arbitrary")),
        name="nsa_prompt",
    )(q, qr, small, rows, win, wbd, pe, kg0, pool)


def _nsa_sample_kernel(pt_ref, cache_ref, q_ref, qr_ref, small_ref, rows_ref, winnew_ref, winbuf_ref,
                       wbd_ref, pe_ref, kg0_ref, pool_ref, expand_ref,
                       o_ref, winout_ref,
                       page_buf, xperm_sc, sems, *, n_pages, past_len, t_valid):
    b = pl.program_id(0)
    nb = pl.num_programs(0)
    tp = SAMPLE_PAD_T
    nseg = past_len // CMP_STRIDE
    nsb = past_len // SEL_LEN
    wbuf = winbuf_ref.shape[1]

    def page_copy(bb, p):
        page = pt_ref[bb * n_pages + p]
        dst_lanes = pl.ds(pl.multiple_of(p * PAGE_SIZE, PAGE_SIZE), PAGE_SIZE)
        return pltpu.make_async_copy(cache_ref.at[page], page_buf.at[bb % 2, :, dst_lanes], sems.at[bb % 2])

    def start_all(bb):
        def body(p, c):
            page_copy(bb, p).start()
            return c
        lax.fori_loop(0, n_pages, body, 0)

    def wait_all(bb):
        def body(p, c):
            page_copy(bb, p).wait()
            return c
        lax.fori_loop(0, n_pages, body, 0)

    @pl.when(b == 0)
    def _():
        start_all(b)

    @pl.when(b + 1 < nb)
    def _():
        start_all(b + 1)

    wait_all(b)
    cmp_buf = page_buf.at[b % 2, pl.ds(0, 2 * LANES), :]
    sel_buf = page_buf.at[b % 2, pl.ds(2 * LANES, 2 * LANES), :]

    seg_pp = PAGE_SIZE // CMP_STRIDE
    pr = lax.broadcasted_iota(jnp.int32, (PAGE_SIZE, PAGE_SIZE), 0)
    pc = lax.broadcasted_iota(jnp.int32, (PAGE_SIZE, PAGE_SIZE), 1)
    perm = jnp.where(pc == CMP_STRIDE * (pr % seg_pp) + pr // seg_pp, 1.0, 0.0).astype(BF16)
    for p in range(n_pages):
        xp = _bdot_t(perm, cmp_buf[:, p * PAGE_SIZE:(p + 1) * PAGE_SIZE])
        for l in range(CMP_STRIDE):
            xperm_sc[l, p * seg_pp:(p + 1) * seg_pp, :] = xp[l * seg_pp:(l + 1) * seg_pp, :]
    kc, vc = _compress_grouped(xperm_sc, nseg, wbd_ref, pe_ref, kg0_ref[...])
    kc_b = kc.astype(BF16)
    vc_b = vc.astype(BF16)
    q = q_ref[...]
    qr = qr_ref[...]
    small = small_ref[...]
    tpos_col = past_len + lax.broadcasted_iota(jnp.int32, (tp, 1), 0)
    tpos_rows = jnp.concatenate([tpos_col] * A_HPG, axis=0)
    bp_idx = lax.broadcasted_iota(jnp.int32, (nsb, nsb), 0)
    b_idx = lax.broadcasted_iota(jnp.int32, (nsb, nsb), 1)
    o_cmps = []
    sels = []
    for g in range(A_KV):
        qn_g = _stack_heads(q, g)
        o_cmp, imp = _cmp_branch(qn_g, kc_b, vc_b, tpos_rows, nseg, tp)
        o_cmps.append(o_cmp)
        imp_sel = _dot2_exact_rhs(imp, pool_ref[...])
        imp_pad = jnp.concatenate([imp_sel, jnp.zeros((nsb - tp, nsb), F32)], axis=0)
        imp_t = jnp.transpose(imp_pad)
        rows_sel = []
        for t in range(tp):
            if t < t_valid:
                row_t = imp_sel[t:t + 1, :]
                col_t = imp_t[:, t:t + 1]
                ahead = jnp.where(col_t > row_t, 1.0, jnp.where((col_t == row_t) & (bp_idx < b_idx), 1.0, 0.0))
                rank = jnp.sum(ahead, axis=0, keepdims=True)
                rows_sel.append(jnp.where(rank < (N_SEL - 1), 1.0, 0.0))
            else:
                rows_sel.append(jnp.zeros((1, nsb), F32))
        sels.append(jnp.concatenate(rows_sel, axis=0))

    new_idx = lax.broadcasted_iota(jnp.int32, (tp, tp), 1)
    tok_idx = lax.broadcasted_iota(jnp.int32, (tp, tp), 0)
    new_ok = jnp.concatenate([jnp.where(new_idx <= tok_idx, 1.0, 0.0)] * A_HEADS, axis=0) > 0.5
    wpos = past_len - wbuf + lax.broadcasted_iota(jnp.int32, (1, wbuf), 1)
    wdiff = tpos_col - wpos
    win_ok = jnp.concatenate([jnp.where((wdiff >= 0) & (wdiff < WINDOW), 1.0, 0.0)] * A_HEADS, axis=0) > 0.5
    k_past = sel_buf[0:LANES, :].astype(BF16)
    v_past = sel_buf[LANES:2 * LANES, :].astype(BF16)
    k_new = rows_ref[:, 2 * LANES:3 * LANES]
    v_new = rows_ref[:, 3 * LANES:4 * LANES]
    kw_past = winbuf_ref[0:LANES, :]
    vw_past = winbuf_ref[LANES:2 * LANES, :]
    kw_new = winnew_ref[:, 0:LANES]
    vw_new = winnew_ref[:, LANES:2 * LANES]
    r4 = A_HPG * tp
    qr_all = jnp.concatenate([_stack_heads(qr, g) for g in range(A_KV)], axis=0)
    mk = jnp.dot(jnp.concatenate(sels, axis=0).astype(BF16), expand_ref[...],
                 preferred_element_type=F32)
    past_ok = jnp.concatenate([mk[g * tp:(g + 1) * tp] for g in range(A_KV) for _ in range(A_HPG)], axis=0) > 0.5
    o_sel = _masked_attn_direct(qr_all, [k_past, k_new], [v_past, v_new], [past_ok, new_ok], [True, False])
    o_win = _masked_attn_direct(qr_all, [kw_past, kw_new], [vw_past, vw_new], [win_ok, new_ok], [True, False])
    o_groups = []
    for g in range(A_KV):
        rs = slice(g * r4, (g + 1) * r4)
        o_groups.append(_gate_cols(small, g, 0) * o_cmps[g] + _gate_cols(small, g, 1) * o_sel[rs]
                        + _gate_cols(small, g, 2) * o_win[rs])
    o_ref[...] = _assemble_heads(o_groups, tp)

    rolled = pltpu.roll(winbuf_ref[...], wbuf - t_valid, 1)
    new_t = jnp.transpose(jnp.concatenate([winnew_ref[...], jnp.zeros((LANES - tp, 2 * LANES), F32)], axis=0))
    new_t = pltpu.roll(new_t, LANES - t_valid, 1)
    lane = lax.broadcasted_iota(jnp.int32, (2 * LANES, LANES), 1)
    winout_ref[:, 0:wbuf - LANES] = rolled[:, 0:wbuf - LANES]
    winout_ref[:, wbuf - LANES:wbuf] = jnp.where(lane < LANES - t_valid, rolled[:, wbuf - LANES:wbuf], new_t)


def _nsa_sample(page_table, cache, q, qr, small, rows, winnew, winbuf, wbd, pe, kg0, t_valid):
    nb, n_pages = page_table.shape
    past_len = n_pages * PAGE_SIZE
    nseg = past_len // CMP_STRIDE
    nsb = past_len // SEL_LEN
    tp = SAMPLE_PAD_T
    wbuf = winbuf.shape[2]
    pool = (jnp.arange(nseg)[:, None] // (SEL_LEN // CMP_STRIDE) == jnp.arange(nsb)[None, :]).astype(BF16)
    expand = (jnp.arange(nsb)[:, None] == jnp.arange(past_len)[None, :] // SEL_LEN).astype(BF16)
    tile = lambda b, pt: (b, 0)
    c2 = lambda b, pt: (0, 0)
    c3 = lambda b, pt: (0, 0, 0)
    gs = pltpu.PrefetchScalarGridSpec(
        num_scalar_prefetch=1,
        grid=(nb,),
        in_specs=[pl.BlockSpec(memory_space=pl.ANY),
                  pl.BlockSpec((tp, A_WIDTH), tile), pl.BlockSpec((tp, A_WIDTH), tile),
                  pl.BlockSpec((tp, LANES), tile), pl.BlockSpec((tp, 4 * LANES), tile),
                  pl.BlockSpec((tp, 2 * LANES), tile),
                  pl.BlockSpec((None, 2 * LANES, wbuf), lambda b, pt: (b, 0, 0)),
                  pl.BlockSpec(wbd.shape, c3), pl.BlockSpec(pe.shape, c3), pl.BlockSpec(kg0.shape, c2),
                  pl.BlockSpec(pool.shape, c2), pl.BlockSpec(expand.shape, c2)],
        out_specs=[pl.BlockSpec((tp, A_WIDTH), tile),
                   pl.BlockSpec((None, 2 * LANES, wbuf), lambda b, pt: (b, 0, 0))],
        scratch_shapes=[pltpu.VMEM((2, 4 * LANES, past_len), F32),
                        pltpu.VMEM((CMP_STRIDE, past_len // CMP_STRIDE, 2 * LANES), F32),
                        pltpu.SemaphoreType.DMA((2,))],
    )
    return pl.pallas_call(
        functools.partial(_nsa_sample_kernel, n_pages=n_pages, past_len=past_len, t_valid=t_valid),
        grid_spec=gs,
        out_shape=[jax.ShapeDtypeStruct((nb * tp, A_WIDTH), F32),
                   jax.ShapeDtypeStruct((nb, 2 * LANES, wbuf), F32)],
        compiler_params=_cparams(("arbitrary",)),
        name="nsa_sample",
    )(page_table.reshape(-1), cache, q, qr, small, rows, winnew, winbuf, wbd, pe, kg0, pool, expand)


MOE_TM = 256
SEG_ALIGN = 8
MOE_RL = -(-(MOE_TM * TOP_K + N_EXPERTS * (SEG_ALIGN - 1)) // LANES) * LANES


def _pack_halves(x, bf16_exact=False):
    w = x.shape[1] // 2
    bits = lax.bitcast_convert_type(x if bf16_exact else x.astype(BF16).astype(F32), jnp.uint32)
    return bits[:, :w] | (bits[:, w:] >> 16)


def _unpack_halves(u):
    hi = lax.bitcast_convert_type(u & jnp.uint32(0xFFFF0000), F32).astype(BF16)
    lo = lax.bitcast_convert_type(u << 16, F32).astype(BF16)
    return hi, lo


def _route_and_sort(h2, wrt_ref, brt_ref, xsl_ref, info_ref, cnt_ref, tm, t_mod, t_valid, m_valid):
    ne = N_EXPERTS
    h2b = h2.astype(BF16)
    h2l = (h2 - h2b.astype(F32)).astype(BF16)
    wh, wl = _split(wrt_ref[...])
    lt = _bdot_t(wh, h2b) + _bdot_t(wl, h2b) + _bdot_t(wh, h2l) + brt_ref[...]
    eidx = lax.broadcasted_iota(jnp.int32, (ne, tm), 0)
    rank = jnp.zeros((ne, tm), F32)
    for ep in range(ne):
        v = lt[ep:ep + 1, :]
        rank = rank + jnp.where(v > lt, 1.0, jnp.where((v == lt) & (eidx > ep), 1.0, 0.0))
    sel = rank < TOP_K
    if t_mod is not None:
        tok = pl.program_id(0) * tm + lax.broadcasted_iota(jnp.int32, (1, tm), 1)
        sel = sel & ((tok % t_mod) < t_valid) & (tok < m_valid)
    mx = jnp.max(jnp.where(sel, lt, NEG_BIG), axis=0, keepdims=True)
    ex = jnp.where(sel, jnp.exp(lt - mx), 0.0)
    den = jnp.sum(ex, axis=0, keepdims=True)
    gate = ex / jnp.where(den > 0, den, 1.0)
    self_ = jnp.where(sel, 1.0, 0.0)
    selb = self_.astype(BF16)
    er = lax.broadcasted_iota(jnp.int32, (ne, ne), 0)
    ec = lax.broadcasted_iota(jnp.int32, (ne, ne), 1)
    c = jnp.dot(jnp.where(ec <= er, 1.0, 0.0).astype(BF16), selb, preferred_element_type=F32)
    tr = lax.broadcasted_iota(jnp.int32, (tm, tm), 0)
    tc = lax.broadcasted_iota(jnp.int32, (tm, tm), 1)
    rk = jnp.dot(selb, jnp.where(tr < tc, 1.0, 0.0).astype(BF16), preferred_element_type=F32)
    cnt = jnp.sum(self_, axis=1, keepdims=True)
    cnt_al = jnp.floor((cnt + (SEG_ALIGN - 1)) * (1.0 / SEG_ALIGN)) * SEG_ALIGN
    cnt_b = jnp.broadcast_to(cnt_al, (ne, LANES))
    cnt_ref[...] = cnt_b
    off = jnp.dot(jnp.where(ec < er, 1.0, 0.0).astype(BF16), cnt_b.astype(BF16), preferred_element_type=F32)
    rowidx = off[:, 0:1] + rk
    rows_k, gates_k, exps_k = [], [], []
    for k in range(1, TOP_K + 1):
        mk = sel & (c == k)
        has = jnp.sum(jnp.where(mk, 1.0, 0.0), axis=0, keepdims=True)
        rows_k.append(jnp.sum(jnp.where(mk, rowidx, 0.0), axis=0, keepdims=True) + has - 1.0)
        gates_k.append(jnp.sum(jnp.where(mk, gate, 0.0), axis=0, keepdims=True))
        exps_k.append(jnp.sum(jnp.where(mk, eidx.astype(F32), 0.0), axis=0, keepdims=True))
    info_ref[...] = jnp.concatenate(rows_k + gates_k + exps_k + [jnp.zeros((4, tm), F32)], axis=0)
    ridx = lax.broadcasted_iota(jnp.int32, (MOE_RL, tm), 0).astype(F32)
    perm = jnp.zeros((MOE_RL, tm), F32)
    for k in range(TOP_K):
        perm = jnp.where(ridx == rows_k[k], 1.0, perm)
    xs = jnp.dot(perm.astype(BF16), h2b, preferred_element_type=F32)
    xsl_ref[...] = _pack_halves(xs, bf16_exact=True)


def _mixout_kernel(x_ref, hm_ref, on_ref, mod_ref, gmix_ref, gffn_ref,
                   wog_ref, bog_ref, wum_ref, wua_ref, wout_ref, wrt_ref, brt_ref,
                   x1_ref, xsl_ref, info_ref, cnt_ref, *, tm, t_mod, t_valid, m_valid, n_real):
    if n_real is not None:
        @pl.when(pl.program_id(0) >= n_real)
        def _():
            xsl_ref[...] = jnp.zeros(xsl_ref.shape, jnp.uint32)
            info_ref[...] = jnp.zeros(info_ref.shape, F32)
            cnt_ref[...] = jnp.zeros(cnt_ref.shape, F32)

        @pl.when(pl.program_id(0) < n_real)
        def _():
            _mixout_body(x_ref, hm_ref, on_ref, mod_ref, gmix_ref, gffn_ref, wog_ref, bog_ref, wum_ref,
                         wua_ref, wout_ref, wrt_ref, brt_ref, x1_ref, xsl_ref, info_ref, cnt_ref,
                         tm, t_mod, t_valid, m_valid)
    else:
        _mixout_body(x_ref, hm_ref, on_ref, mod_ref, gmix_ref, gffn_ref, wog_ref, bog_ref, wum_ref,
                     wua_ref, wout_ref, wrt_ref, brt_ref, x1_ref, xsl_ref, info_ref, cnt_ref,
                     tm, t_mod, t_valid, m_valid)


def _mixout_body(x_ref, hm_ref, on_ref, mod_ref, gmix_ref, gffn_ref,
                 wog_ref, bog_ref, wum_ref, wua_ref, wout_ref, wrt_ref, brt_ref,
                 x1_ref, xsl_ref, info_ref, cnt_ref, tm, t_mod, t_valid, m_valid):
    d = D_MODEL
    x = x_ref[...]
    sh1, sc1, gt1 = mod_ref[:, 0:d], mod_ref[:, d:2 * d], mod_ref[:, 2 * d:3 * d]
    sh2, sc2 = mod_ref[:, 3 * d:4 * d], mod_ref[:, 4 * d:5 * d]
    h = _rmsnorm_rows(x, gmix_ref[...]) * (1.0 + sc1) + sh1
    hb = h.astype(BF16)
    mo = jnp.dot(hb, wog_ref[:, 0:M_WIDTH], preferred_element_type=F32) + bog_ref[:, 0:M_WIDTH]
    ym = _bdot(_sigmoid(mo) * hm_ref[...], wum_ref[...])
    ya = _bdot(on_ref[...], wua_ref[...])
    ga = jnp.dot(hb, wog_ref[:, M_WIDTH:M_WIDTH + d], preferred_element_type=F32) + bog_ref[:, M_WIDTH:M_WIDTH + d]
    u = _sigmoid(ga) * ym
    gb = (jnp.dot(hb, wog_ref[:, M_WIDTH + d:M_WIDTH + 2 * d], preferred_element_type=F32)
          + bog_ref[:, M_WIDTH + d:M_WIDTH + 2 * d])
    u = u + _sigmoid(gb) * ya
    x1 = x + gt1 * _bdot(u, wout_ref[...])
    x1_ref[...] = x1
    h2 = _rmsnorm_rows(x1, gffn_ref[...]) * (1.0 + sc2) + sh2
    _route_and_sort(h2, wrt_ref, brt_ref, xsl_ref, info_ref, cnt_ref, tm, t_mod, t_valid, m_valid)


def _mixout_with_shared(*refs, n_shared, **kw):
    n_in = 13
    _mixout_kernel(*refs[:n_in], *refs[n_in + n_shared:], **kw)


def _mixout(x2, hm, on, mod3, gmix, gffn, wts, tiles_per_mod, nt_total, tile0=0, shared=None,
            t_mod=None, t_valid=None, m_valid=None):
    m = x2.shape[0]
    tm = MOE_TM
    nt = m // tm
    (wog, bog, wum, wua, wout, wr, br) = wts
    r = mod3.shape[1]
    n_extra = nt_total - tile0 - nt if shared is None else 0
    row = lambda i: (jnp.minimum(i, nt - 1), 0)
    const = lambda i: (0, 0)
    in_specs = [pl.BlockSpec((tm, D_MODEL), row), pl.BlockSpec((tm, M_WIDTH), row),
                pl.BlockSpec((tm, A_WIDTH), row),
                pl.BlockSpec((None, r, 6 * D_MODEL), lambda i: (jnp.minimum(i, nt - 1) // tiles_per_mod, 0, 0)),
                pl.BlockSpec((1, D_MODEL), const), pl.BlockSpec((1, D_MODEL), const),
                pl.BlockSpec(wog.shape, const), pl.BlockSpec(bog.shape, const),
                pl.BlockSpec(wum.shape, const), pl.BlockSpec(wua.shape, const),
                pl.BlockSpec(wout.shape, const), pl.BlockSpec(wr.shape, const),
                pl.BlockSpec(br.shape, const)]
    args = [x2, hm, on, mod3, gmix, gffn, wog, bog, wum, wua, wout, wr, br]
    kw = dict(tm=tm, t_mod=t_mod, t_valid=t_valid, m_valid=m_valid, n_real=nt if n_extra else None)
    body = functools.partial(_mixout_kernel, **kw)
    aliases = {}
    if shared is not None:
        in_specs += [pl.BlockSpec(memory_space=pl.ANY)] * len(shared)
        aliases = {len(args) + j: 1 + j for j in range(len(shared))}
        args += list(shared)
        body = functools.partial(_mixout_with_shared, n_shared=len(shared), **kw)
    return pl.pallas_call(
        body,
        grid=(nt + n_extra,),
        in_specs=in_specs,
        out_specs=[pl.BlockSpec((tm, D_MODEL), row),
                   pl.BlockSpec((MOE_RL, D_MODEL // 2), lambda i: (tile0 + i, 0)),
                   pl.BlockSpec((16, tm), lambda i: (0, tile0 + i)),
                   pl.BlockSpec((None, N_EXPERTS, LANES), lambda i: (tile0 + i, 0, 0))],
        out_shape=[jax.ShapeDtypeStruct((m, D_MODEL), F32),
                   jax.ShapeDtypeStruct((nt_total * MOE_RL, D_MODEL // 2), jnp.uint32),
                   jax.ShapeDtypeStruct((16, nt_total * tm), F32),
                   jax.ShapeDtypeStruct((nt_total, N_EXPERTS, LANES), F32)],
        input_output_aliases=aliases,
        compiler_params=_cparams(("arbitrary" if n_extra else "parallel",)),
        name="mixout",
    )(*args)


MOE_BM = 256
MOE_CH = 512


def _moe_kernel(be_ref, na_ref, grp_ref, first_ref, wslot_ref, nxt_ref,
                xsl_ref, wgu_ref, bgu_ref, wdn_ref, bdn_ref, ysl_ref,
                wgu_bf, wdn_bf, xbuf, ybuf, wgu_f, wdn_f, sem_in, sem_out, sem_w, *, trash_row0):
    i = pl.program_id(0)
    na = na_ref[0]
    e = be_ref[i]
    n_grp = MOE_BM // SEG_ALIGN

    def weight_copies(ex, s):
        return [pltpu.make_async_copy(wgu_ref.at[ex], wgu_f.at[s], sem_w.at[s]),
                pltpu.make_async_copy(wdn_ref.at[ex], wdn_f.at[s], sem_w.at[s])]

    def group_copies(blk, inbound, slot=None):
        slot = blk % 2 if slot is None else slot
        cps = []
        for r in range(n_grp):
            v = grp_ref[blk * n_grp + r]
            vm_rows = pl.ds(r * SEG_ALIGN, SEG_ALIGN)
            if inbound:
                row = pl.multiple_of(jnp.where(v >= 0, v, trash_row0 + 2 * MOE_BM), SEG_ALIGN)
                cps.append(pltpu.make_async_copy(xsl_ref.at[pl.ds(row, SEG_ALIGN), :],
                                                 xbuf.at[slot, vm_rows, :], sem_in.at[slot]))
            else:
                spare = trash_row0 + slot * MOE_BM + r * SEG_ALIGN
                row = pl.multiple_of(jnp.where(v >= 0, v, spare), SEG_ALIGN)
                cps.append(pltpu.make_async_copy(ybuf.at[slot, vm_rows, :],
                                                 ysl_ref.at[pl.ds(row, SEG_ALIGN), :], sem_out.at[slot]))
        return cps

    def start_gather(blk):
        for cp in group_copies(blk, True):
            cp.start()

    def start_scatter(blk):
        for cp in group_copies(blk, False):
            cp.start()

    def wait_rows(blk, sem, inbound):
        slot = blk % 2
        if inbound:
            pltpu.make_async_copy(xsl_ref.at[pl.ds(0, MOE_BM), :], xbuf.at[slot], sem.at[slot]).wait()
        else:
            pltpu.make_async_copy(ybuf.at[slot], ysl_ref.at[pl.ds(0, MOE_BM), :], sem.at[slot]).wait()

    @pl.when(i == 0)
    def _():
        start_gather(i)
        for cp in weight_copies(e, 0):
            cp.start()

    @pl.when(i + 1 < na)
    def _():
        start_gather(i + 1)

    @pl.when((i < na) & (first_ref[i] != 0))
    def _():
        s = wslot_ref[i]
        for cp in weight_copies(e, s):
            cp.wait()
        nx = nxt_ref[i]

        @pl.when(nx >= 0)
        def _():
            for cp in weight_copies(nx, 1 - s):
                cp.start()

        for j in range(2 * D_EXPERT // MOE_CH):
            wgu_bf[:, j * MOE_CH:(j + 1) * MOE_CH] = wgu_f[s, :, j * MOE_CH:(j + 1) * MOE_CH].astype(BF16)
        for j in range(D_EXPERT // MOE_CH):
            wdn_bf[j * MOE_CH:(j + 1) * MOE_CH, :] = wdn_f[s, j * MOE_CH:(j + 1) * MOE_CH, :].astype(BF16)

    @pl.when(i < na)
    def _():
        slot = i % 2
        wait_rows(i, sem_in, True)

        @pl.when(i >= 2)
        def _():
            wait_rows(i - 2, sem_out, False)

        half = D_MODEL // 2
        xh, xl = _unpack_halves(xbuf[slot])

        def xdot(c0, c1):
            return (jnp.dot(xh, wgu_bf[0:half, c0:c1], preferred_element_type=F32)
                    + jnp.dot(xl, wgu_bf[half:D_MODEL, c0:c1], preferred_element_type=F32))

        acc = jnp.zeros((MOE_BM, D_MODEL), F32) + bdn_ref[...]
        for j in range(D_EXPERT // MOE_CH):
            lo, hi = j * MOE_CH, (j + 1) * MOE_CH
            gj = xdot(lo, hi) + bgu_ref[:, lo:hi]
            uj = xdot(D_EXPERT + lo, D_EXPERT + hi) + bgu_ref[:, D_EXPERT + lo:D_EXPERT + hi]
            gj = jnp.minimum(gj, SWIGLU_LIMIT)
            uj = jnp.clip(uj, -SWIGLU_LIMIT, SWIGLU_LIMIT)
            act = gj * _sigmoid(SWIGLU_ALPHA * gj) * (uj + 1.0)
            acc = acc + jnp.dot(act.astype(BF16), wdn_bf[lo:hi, :], preferred_element_type=F32)
        ybuf[slot] = _pack_halves(acc)
        start_scatter(i)

        @pl.when(i == na - 1)
        def _():
            @pl.when(i >= 1)
            def _():
                wait_rows(i - 1, sem_out, False)
            wait_rows(i, sem_out, False)


def _moe_experts(plan, xsl, w_gu, b_gu, w_dn, b_dn):
    block_e = plan[0]
    nblk = block_e.shape[0]
    spare_row0 = xsl.shape[0] - MOE_RL
    assert MOE_RL >= 2 * MOE_BM
    wmap = lambda i, be, *_: (be[i], 0, 0)
    anyspec = pl.BlockSpec(memory_space=pl.ANY)
    gs = pltpu.PrefetchScalarGridSpec(
        num_scalar_prefetch=len(plan),
        grid=(nblk,),
        in_specs=[anyspec,
                  anyspec,
                  pl.BlockSpec((None, 1, 2 * D_EXPERT), wmap),
                  anyspec,
                  pl.BlockSpec((None, 1, D_MODEL), wmap)],
        out_specs=anyspec,
        scratch_shapes=[pltpu.VMEM((D_MODEL, 2 * D_EXPERT), BF16), pltpu.VMEM((D_EXPERT, D_MODEL), BF16),
                        pltpu.VMEM((2, MOE_BM, D_MODEL // 2), jnp.uint32),
                        pltpu.VMEM((2, MOE_BM, D_MODEL // 2), jnp.uint32),
                        pltpu.VMEM((2, D_MODEL, 2 * D_EXPERT), F32), pltpu.VMEM((2, D_EXPERT, D_MODEL), F32),
                        pltpu.SemaphoreType.DMA((2,)), pltpu.SemaphoreType.DMA((2,)),
                        pltpu.SemaphoreType.DMA((2,))],
    )
    return pl.pallas_call(
        functools.partial(_moe_kernel, trash_row0=spare_row0),
        grid_spec=gs,
        out_shape=jax.ShapeDtypeStruct(xsl.shape, jnp.uint32),
        input_output_aliases={len(plan): 0},
        compiler_params=_cparams(("arbitrary",)),
        name="moe_experts",
    )(*plan, xsl, w_gu, b_gu.reshape(N_EXPERTS, 1, -1), w_dn, b_dn.reshape(N_EXPERTS, 1, -1))


def _combine_kernel(ysl_ref, info_ref, x1_ref, mod_ref, y_ref, *, tm):
    info = info_ref[...]
    info_t = jnp.transpose(jnp.concatenate([info, jnp.zeros((LANES - info.shape[0], tm), F32)], axis=0))
    ridx = lax.broadcasted_iota(jnp.int32, (tm, MOE_RL), 1).astype(F32)
    pg = jnp.zeros((tm, MOE_RL), F32)
    for k in range(TOP_K):
        pg = jnp.where(ridx == info_t[:, k:k + 1], info_t[:, TOP_K + k:TOP_K + k + 1], pg)
    pgb = pg.astype(BF16)
    yh, yl = _unpack_halves(ysl_ref[...])
    half = D_MODEL // 2
    gt2 = mod_ref[:, 5 * D_MODEL:6 * D_MODEL]
    for c, yy in ((0, yh), (1, yl)):
        moe = jnp.dot(pgb, yy, preferred_element_type=F32)
        y_ref[:, c * half:(c + 1) * half] = (x1_ref[:, c * half:(c + 1) * half]
                                             + gt2[:, c * half:(c + 1) * half] * moe)


def _combine(ysl, info, x1, mod3, tiles_per_mod, tile0=0):
    m = x1.shape[0]
    tm = MOE_TM
    r = mod3.shape[1]
    return pl.pallas_call(
        functools.partial(_combine_kernel, tm=tm),
        grid=(m // tm,),
        in_specs=[pl.BlockSpec((MOE_RL, D_MODEL // 2), lambda i: (tile0 + i, 0)),
                  pl.BlockSpec((16, tm), lambda i: (0, tile0 + i)),
                  pl.BlockSpec((tm, D_MODEL), lambda i: (i, 0)),
                  pl.BlockSpec((None, r, 6 * D_MODEL), lambda i: (i // tiles_per_mod, 0, 0))],
        out_specs=pl.BlockSpec((tm, D_MODEL), lambda i: (i, 0)),
        out_shape=jax.ShapeDtypeStruct((m, D_MODEL), F32),
        compiler_params=_cparams(("parallel",)),
        name="moe_combine",
    )(ysl, info, x1, mod3)


def _moe_plan(cnt):
    cnt = cnt.astype(jnp.int32)
    nt = cnt.shape[0]
    so = jnp.cumsum(cnt, axis=1) - cnt + (jnp.arange(nt) * MOE_RL)[:, None]
    ce = jnp.cumsum(cnt, axis=0)
    cs = ce - cnt
    tot = ce[-1]
    nblk_e = (tot + MOE_BM - 1) // MOE_BM
    blk_end = jnp.cumsum(nblk_e)
    max_rows = nt * MOE_TM * TOP_K + nt * N_EXPERTS * (SEG_ALIGN - 1)
    n_blocks = -(-max_rows // MOE_BM) + N_EXPERTS
    bidx = jnp.arange(n_blocks)
    block_e = jnp.minimum(jnp.sum(blk_end[None, :] <= bidx[:, None], axis=1), N_EXPERTS - 1).astype(jnp.int32)
    is_e = (jnp.arange(N_EXPERTS)[:, None] == block_e[None, :]).astype(jnp.int32)
    per_block = lambda a: jnp.sum(a[..., :, None] * is_e, axis=-2)
    block_r0 = (bidx - per_block(blk_end - nblk_e)) * MOE_BM
    x = block_r0[:, None] + jnp.arange(MOE_BM // SEG_ALIGN)[None, :] * SEG_ALIGN
    ce_b = per_block(ce)[:, :, None]
    cs_b = per_block(cs)[:, :, None]
    inside = (cs_b <= x[None]) & (x[None] < ce_b)
    grp = x + jnp.sum(jnp.where(inside, per_block(so - cs)[:, :, None], 0), axis=0)
    grp = jnp.where(x < per_block(tot)[:, None], grp, -1)
    n_active = blk_end[-1].reshape(1)
    used = nblk_e > 0
    first = (bidx == per_block(blk_end - nblk_e)) & (bidx < n_active[0])
    wslot = per_block(jnp.cumsum(used) - 1) % 2
    eidx = jnp.arange(N_EXPERTS)
    later_used = used[None, :] & (eidx[None, :] > eidx[:, None])
    nxt_e = jnp.min(jnp.where(later_used, eidx[None, :], N_EXPERTS), axis=1)
    nxt = per_block(jnp.where(nxt_e < N_EXPERTS, nxt_e, -1))
    i32 = lambda a: a.reshape(-1).astype(jnp.int32)
    return block_e, i32(n_active), i32(grp), i32(first), i32(wslot), i32(nxt)


def _rope_tables(pos):
    half = ROT_DIM // 2
    inv = ROPE_THETA ** (-jnp.arange(half, dtype=F32) * (2.0 / ROT_DIM))
    ang = pos.astype(F32)[:, None] * inv[None, :]
    cos, sin = jnp.cos(ang), jnp.sin(ang)
    n = pos.shape[0]
    ones = jnp.ones((n, A_DH - ROT_DIM), F32)
    zeros_h = jnp.zeros((n, half), F32)
    zeros_r = jnp.zeros((n, A_DH - ROT_DIM), F32)
    cos64 = jnp.concatenate([cos, cos, ones], axis=1)
    sprev64 = jnp.concatenate([zeros_h, sin, zeros_r], axis=1)
    snext64 = jnp.concatenate([-sin, zeros_h, zeros_r], axis=1)
    two = lambda a: jnp.concatenate([a, a], axis=1)
    return two(cos64), two(sprev64), two(snext64)


def _prep_weights(w_in, b_in, q_norm_g, k_norm_g, cmp_pe_k, cmp_pe_v, cmp_w_k, cmp_w_v,
                  w_up_m, w_up_a, w_out, w_router, b_router):
    b2 = b_in.reshape(1, N_IN)
    wm = w_in[:, OFF_MQ:OFF_MO].astype(BF16)
    bm = b2[:, OFF_MQ:OFF_MO]
    wq = w_in[:, OFF_AQ:OFF_AKV].astype(BF16)
    bq = b2[:, OFF_AQ:OFF_AKV]
    wkv = w_in[:, OFF_AKV:OFF_AG].astype(BF16)
    bkv = b2[:, OFF_AKV:OFF_AG]
    n_small = 2 * M_HEADS + 3 * A_HEADS
    ws = jnp.concatenate([w_in[:, OFF_MI:OFF_AQ], w_in[:, OFF_AG:OFF_GA],
                          jnp.zeros((D_MODEL, LANES - n_small), F32)], axis=1)
    bs = jnp.concatenate([b2[:, OFF_MI:OFF_AQ], b2[:, OFF_AG:OFF_GA], jnp.zeros((1, LANES - n_small), F32)], axis=1)
    qg = jnp.tile(q_norm_g, A_HEADS).reshape(1, A_WIDTH)
    kg = jnp.stack([jnp.tile(k_norm_g[1], A_KV), jnp.tile(k_norm_g[2], A_KV)], axis=0)
    kg0 = jnp.tile(k_norm_g[0], A_KV).reshape(1, LANES)
    hid = jnp.arange(A_WIDTH) // A_DH
    bd = jnp.where(hid[:, None] == hid[None, :], 1.0 / A_DH, 0.0).astype(BF16)
    inproj_w = (wm, bm, wq, bq, wkv, bkv, ws, bs, qg, kg, bd)

    z = jnp.zeros((CMP_LEN, A_DH, A_DH), F32)
    r0 = jnp.concatenate([cmp_w_k, z, z, z], axis=2)
    r1 = jnp.concatenate([z, cmp_w_k, z, z], axis=2)
    r2 = jnp.concatenate([z, z, cmp_w_v, z], axis=2)
    r3 = jnp.concatenate([z, z, z, cmp_w_v], axis=2)
    wbd = jnp.concatenate([r0, r1, r2, r3], axis=1).astype(BF16)
    pe = jnp.concatenate([cmp_pe_k, cmp_pe_k, cmp_pe_v, cmp_pe_v], axis=1).reshape(CMP_LEN, 1, 2 * LANES)

    wog = jnp.concatenate([w_in[:, OFF_MO:OFF_MI], w_in[:, OFF_GA:N_IN]], axis=1).astype(BF16)
    bog = jnp.concatenate([b2[:, OFF_MO:OFF_MI], b2[:, OFF_GA:N_IN]], axis=1)
    mixout_w = (wog, bog, w_up_m.astype(BF16), w_up_a.astype(BF16), w_out.astype(BF16),
                w_router.T, b_router.reshape(N_EXPERTS, 1))
    return inproj_w, (wbd, pe, kg0), mixout_w


def _pick_tile(m, pref):
    t = pref
    while m % t:
        t //= 2
    return t


def kernel(x_prompt, x_sample, cache_nsa_kv, state_win_kv, state_mlstm_C, state_mlstm_n, state_mlstm_m, page_table, c_prompt, c_sample, w_ada, b_ada, g_mix, g_ffn, w_in, b_in, q_norm_g, k_norm_g, cmp_pe_k, cmp_pe_v, cmp_w_k, cmp_w_v, w_up_m, w_up_a, w_out, w_router, b_router, w_gu, b_gu, w_dn, b_dn):
    depth = w_in.shape[0]
    assert depth == 1
    B, T, D = x_prompt.shape
    DB, TS, _ = x_sample.shape
    n_pages = page_table.shape[1]
    past_len = n_pages * PAGE_SIZE
    wbuf = state_win_kv.shape[2]
    tp = SAMPLE_PAD_T
    assert TS <= tp and wbuf % tp == 0 and T % 128 == 0

    l = 0
    inproj_w, cmp_w, mixout_w = _prep_weights(
        w_in[l], b_in[l], q_norm_g[l], k_norm_g[l], cmp_pe_k[l], cmp_pe_v[l], cmp_w_k[l], cmp_w_v[l],
        w_up_m[l], w_up_a[l], w_out[l], w_router[l], b_router[l])
    wbd, pe, kg0 = cmp_w
    gmix = g_mix[l].reshape(1, D)
    gffn = g_ffn[l].reshape(1, D)

    nc = B + DB
    nc_pad = -(-nc // SUBLANES) * SUBLANES
    c_all = jnp.concatenate([c_prompt, c_sample, jnp.zeros((nc_pad - nc, D), F32)], axis=0)
    mod = _adaln(c_all, w_ada[l], b_ada[l])
    mod_p = mod[:B].reshape(B, 1, 6 * D)
    mod_s = jnp.repeat(mod[B:B + DB], tp, axis=0).reshape(1, DB * tp, 6 * D)

    mp = B * T
    tm = _pick_tile(T, 256)
    xp = x_prompt.reshape(mp, D)
    tabs_p = _rope_tables(jnp.arange(T, dtype=jnp.int32))
    mq, mk, mv, q, qr, rows, win, small, rows_t, win_t = _inproj(xp, mod_p, gmix, tabs_p, inproj_w, tm, T // tm,
                                                                 T // tm, rows_t_batches=B)
    Lp = _pick_tile(T, 128)
    hm, C_p, n_p, m_p = _mlstm(mq, mk, mv, small, B, T, T, Lp)
    o_nsa = _nsa_prompt(q, qr, small, rows, win, wbd, pe, kg0, B, T)
    assert T % MOE_TM == 0
    ms_pad = -(-(DB * tp) // MOE_TM) * MOE_TM
    nt_p = mp // MOE_TM
    nt_all = nt_p + ms_pad // MOE_TM + 1
    x1_p, xsl, info, cnt = _mixout(xp, hm, o_nsa, mod_p, gmix, gffn, mixout_w, T // MOE_TM, nt_all)

    ms = DB * tp
    xs_pad = jnp.concatenate([x_sample, jnp.zeros((DB, tp - TS, D), F32)], axis=1).reshape(ms, D)
    pos_s = past_len + jnp.tile(jnp.arange(tp, dtype=jnp.int32), DB)
    tabs_s = _rope_tables(pos_s)
    mq_s, mk_s, mv_s, q_s, qr_s, rows_s, win_s, small_s = _inproj(xs_pad, mod_s, gmix, tabs_s, inproj_w, ms, 1, 1)
    hm_s, C_s, n_s, m_s = _mlstm(mq_s, mk_s, mv_s, small_s, DB, tp, TS, tp,
                                 state=(state_mlstm_C[l], state_mlstm_n[l], state_mlstm_m[l]),
                                 nseq=_pick_tile(DB, 4))
    cache2 = jnp.transpose(cache_nsa_kv[l], (0, 2, 3, 4, 1)).reshape(cache_nsa_kv.shape[1], 4 * LANES, PAGE_SIZE)
    winbuf = jnp.transpose(state_win_kv[l], (0, 2, 3, 4, 1)).reshape(DB, 2 * LANES, wbuf)
    o_nsa_s, win_out_s = _nsa_sample(page_table, cache2, q_s, qr_s, small_s, rows_s, win_s, winbuf,
                                     wbd, pe, kg0, TS)
    assert ms_pad == MOE_TM
    rpad = lambda a: jnp.concatenate([a, jnp.zeros((ms_pad - ms, a.shape[1]), a.dtype)], axis=0) if ms_pad > ms else a
    mod_sp = rpad(mod_s[0])[None]
    x1_s, xsl, info, cnt = _mixout(rpad(xs_pad), rpad(hm_s), rpad(o_nsa_s), mod_sp, gmix, gffn, mixout_w,
                                   1, nt_all, tile0=nt_p, shared=(xsl, info, cnt),
                                   t_mod=tp, t_valid=TS, m_valid=ms)

    ysl = _moe_experts(_moe_plan(cnt[:, :, 0]), xsl, w_gu[l], b_gu[l], w_dn[l], b_dn[l])
    y_p = _combine(ysl, info, x1_p, mod_p, T // MOE_TM).reshape(B, T, D)
    y_s_all = _combine(ysl, info, x1_s, mod_sp, 1, tile0=nt_p)
    valid = lambda a: a.reshape(DB, tp, -1)[:, :TS].reshape(DB * TS, -1)
    y_s = valid(y_s_all[:ms]).reshape(DB, TS, D)

    kv_p = jnp.transpose(rows_t.reshape(B, 4, A_KV, A_DH, T), (0, 4, 1, 2, 3))[None]
    kv_s = valid(rows_s).reshape(1, DB, TS, 4, A_KV, A_DH)
    wp = min(WINDOW, T)
    win_p = jnp.transpose(win_t[:, :, T - wp:].reshape(B, 2, A_KV, A_DH, wp), (0, 4, 1, 2, 3))[None]
    win_s_out = jnp.transpose(win_out_s.reshape(DB, 2, A_KV, A_DH, wbuf), (0, 4, 1, 2, 3))[None]
    return (y_p, y_s, kv_p, kv_s, win_p, win_s_out,
            C_p[None], n_p[None], m_p[None], C_s[None], n_s[None], m_s[None])
```

```python
import functools

import jax
import jax.numpy as jnp
from jax import lax
from jax.experimental import pallas as pl
from jax.experimental.pallas import tpu as pltpu

F32 = jnp.float32
BF16 = jnp.bfloat16

D_MODEL = 1024
M_HEADS = 4
M_DH = 128
M_WIDTH = M_HEADS * M_DH
A_HEADS = 8
A_KV = 2
A_HPG = A_HEADS // A_KV
A_DH = 64
A_WIDTH = A_HEADS * A_DH
CMP_STRIDE = 16
CMP_LEN = 32
SEL_LEN = 64
N_SEL = 16
WINDOW = 512
PAGE_SIZE = 128
ROPE_THETA = 500000.0
ROT_DIM = A_DH // 4
ATT_SCALE = A_DH ** -0.5
N_EXPERTS = 32
TOP_K = 4
D_EXPERT = D_MODEL
SWIGLU_LIMIT = 7.0
SWIGLU_ALPHA = 1.702
EPS = 1e-6

OFF_MQ, OFF_MK, OFF_MV, OFF_MO = 0, M_WIDTH, 2 * M_WIDTH, 3 * M_WIDTH
OFF_MI = 4 * M_WIDTH
OFF_MF = OFF_MI + M_HEADS
OFF_AQ = OFF_MF + M_HEADS
OFF_AKV = OFF_AQ + A_WIDTH
OFF_AG = OFF_AKV + 6 * A_KV * A_DH
OFF_GA = OFF_AG + 3 * A_HEADS
OFF_GB = OFF_GA + D_MODEL
N_IN = OFF_GB + D_MODEL

LANES = 128
SUBLANES = 8
VMEM_LIMIT = 56 * 1024 * 1024

NEG_BIG = -1e30
M_INIT = -1e29
LOG2E = 1.4426950408889634
SAMPLE_PAD_T = 8


def _cparams(sem):
    return pltpu.CompilerParams(dimension_semantics=sem, vmem_limit_bytes=VMEM_LIMIT)


def _bdot(a, b):
    return jnp.dot(a.astype(BF16), b.astype(BF16), preferred_element_type=F32)


def _bdot_t(a, b):
    return lax.dot_general(a.astype(BF16), b.astype(BF16), (((1,), (1,)), ((), ())),
                           preferred_element_type=F32)


def _split(a):
    hi = a.astype(BF16)
    lo = (a - hi.astype(F32)).astype(BF16)
    return hi, lo


def _dot3(a, b):
    ah, al = _split(a)
    bh, bl = _split(b)
    return (jnp.dot(ah, bh, preferred_element_type=F32) + jnp.dot(al, bh, preferred_element_type=F32)
            + jnp.dot(ah, bl, preferred_element_type=F32))


def _dot2_exact_rhs(a, b_bf16):
    ah, al = _split(a)
    return jnp.dot(ah, b_bf16, preferred_element_type=F32) + jnp.dot(al, b_bf16, preferred_element_type=F32)


def _sigmoid(x):
    return 0.5 * jnp.tanh(0.5 * x) + 0.5


def _rmsnorm_rows(x, g):
    return x * lax.rsqrt(jnp.mean(x * x, axis=-1, keepdims=True) + EPS) * g


def _adaln_kernel(c_ref, w_ref, b_ref, o_ref):
    c = c_ref[...]
    s = c * _sigmoid(c)
    o_ref[...] = _dot3(s, w_ref[...]) + b_ref[...]


def _adaln(c, w, b):
    mc, d = c.shape
    n = w.shape[1]
    tn = 1024
    return pl.pallas_call(
        _adaln_kernel,
        grid=(n // tn,),
        in_specs=[pl.BlockSpec((mc, d), lambda j: (0, 0)),
                  pl.BlockSpec((d, tn), lambda j: (0, j)),
                  pl.BlockSpec((1, tn), lambda j: (0, j))],
        out_specs=pl.BlockSpec((mc, tn), lambda j: (0, j)),
        out_shape=jax.ShapeDtypeStruct((mc, n), F32),
        compiler_params=_cparams(("parallel",)),
        name="adaln",
    )(c, w, b.reshape(1, n))


def _head_norm(z, bd, gain):
    ms = _dot2_exact_rhs(z * z, bd)
    return z * lax.rsqrt(ms + EPS) * gain


def _rope(z, cos, s_prev, s_next):
    w = z.shape[1]
    rep = w // LANES
    if rep > 1:
        cos = jnp.concatenate([cos] * rep, axis=1)
        s_prev = jnp.concatenate([s_prev] * rep, axis=1)
        s_next = jnp.concatenate([s_next] * rep, axis=1)
    z_prev = pltpu.roll(z, ROT_DIM // 2, 1)
    z_next = pltpu.roll(z, w - ROT_DIM // 2, 1)
    return z * cos + z_prev * s_prev + z_next * s_next


def _inproj_kernel(x_ref, mod_ref, gmix_ref, cos_ref, sp_ref, sn_ref,
                   wm_ref, bm_ref, wq_ref, bq_ref, wkv_ref, bkv_ref, ws_ref, bs_ref,
                   qg_ref, kg_ref, bd_ref,
                   mq_ref, mk_ref, mv_ref, q_ref, qr_ref, rows_ref, win_ref, small_ref,
                   rows_t_ref=None, win_t_ref=None):
    x = x_ref[...]
    sh1 = mod_ref[:, 0:D_MODEL]
    sc1 = mod_ref[:, D_MODEL:2 * D_MODEL]
    h = _rmsnorm_rows(x, gmix_ref[...]) * (1.0 + sc1) + sh1
    hb = h.astype(BF16)

    mq_ref[...] = jnp.dot(hb, wm_ref[:, 0:M_WIDTH], preferred_element_type=F32) + bm_ref[:, 0:M_WIDTH]
    mk = jnp.dot(hb, wm_ref[:, M_WIDTH:2 * M_WIDTH], preferred_element_type=F32) + bm_ref[:, M_WIDTH:2 * M_WIDTH]
    mk_ref[...] = mk * (M_DH ** -0.5)
    mv_ref[...] = (jnp.dot(hb, wm_ref[:, 2 * M_WIDTH:3 * M_WIDTH], preferred_element_type=F32)
                   + bm_ref[:, 2 * M_WIDTH:3 * M_WIDTH])

    cos, sp, sn = cos_ref[...], sp_ref[...], sn_ref[...]
    zq = jnp.dot(hb, wq_ref[...], preferred_element_type=F32) + bq_ref[...]
    qn = _head_norm(zq, bd_ref[...], qg_ref[...])
    q_ref[...] = qn
    qr_ref[...] = _rope(qn, cos, sp, sn)

    zkv = jnp.dot(hb, wkv_ref[...], preferred_element_type=F32) + bkv_ref[...]
    bd2 = bd_ref[0:LANES, 0:LANES]
    ksel = _head_norm(zkv[:, 2 * LANES:3 * LANES], bd2, kg_ref[0:1, :])
    rows = jnp.concatenate([zkv[:, 0:2 * LANES], _rope(ksel, cos, sp, sn), zkv[:, 3 * LANES:4 * LANES]], axis=1)
    rows_ref[...] = rows
    if rows_t_ref is not None:
        rows_t_ref[...] = jnp.transpose(rows)
    kwin = _head_norm(zkv[:, 4 * LANES:5 * LANES], bd2, kg_ref[1:2, :])
    win = jnp.concatenate([_rope(kwin, cos, sp, sn), zkv[:, 5 * LANES:6 * LANES]], axis=1)
    win_ref[...] = win
    if win_t_ref is not None:
        win_t_ref[...] = jnp.transpose(win)

    small_ref[...] = _dot3(h, ws_ref[...]) + bs_ref[...]


def _inproj(x2, mod3, gmix, tabs, wts, tm, tiles_per_mod, pos_tiles, rows_t_batches=None):
    m = x2.shape[0]
    cos_t, sp_t, sn_t = tabs
    (wm, bm, wq, bq, wkv, bkv, ws, bs, qg, kg, bd) = wts
    r = mod3.shape[1]
    row = lambda i: (i, 0)
    const = lambda i: (0, 0)
    tab = lambda i: (i % pos_tiles, 0)
    in_specs = [
        pl.BlockSpec((tm, D_MODEL), row),
        pl.BlockSpec((None, r, 6 * D_MODEL), lambda i: (i // tiles_per_mod, 0, 0)),
        pl.BlockSpec((1, D_MODEL), const),
        pl.BlockSpec((tm, LANES), tab), pl.BlockSpec((tm, LANES), tab), pl.BlockSpec((tm, LANES), tab),
        pl.BlockSpec(wm.shape, const), pl.BlockSpec(bm.shape, const),
        pl.BlockSpec(wq.shape, const), pl.BlockSpec(bq.shape, const),
        pl.BlockSpec(wkv.shape, const), pl.BlockSpec(bkv.shape, const),
        pl.BlockSpec(ws.shape, const), pl.BlockSpec(bs.shape, const),
        pl.BlockSpec(qg.shape, const), pl.BlockSpec(kg.shape, const), pl.BlockSpec(bd.shape, const),
    ]
    widths = (M_WIDTH, M_WIDTH, M_WIDTH, A_WIDTH, A_WIDTH, 4 * LANES, 2 * LANES, LANES)
    out_specs = [pl.BlockSpec((tm, w), row) for w in widths]
    out_shape = [jax.ShapeDtypeStruct((m, w), F32) for w in widths]
    if rows_t_batches is not None:
        for w in (4 * LANES, 2 * LANES):
            out_specs.append(pl.BlockSpec((None, w, tm), lambda i: (i // tiles_per_mod, 0, i % tiles_per_mod)))
            out_shape.append(jax.ShapeDtypeStruct((rows_t_batches, w, m // rows_t_batches), F32))
    return pl.pallas_call(
        _inproj_kernel,
        grid=(m // tm,),
        in_specs=in_specs,
        out_specs=out_specs,
        out_shape=out_shape,
        compiler_params=_cparams(("parallel",)),
        name="inproj",
    )(x2, mod3, gmix, cos_t, sp_t, sn_t, wm, bm, wq, bq, wkv, bkv, ws, bs, qg, kg, bd)


def _log_sigmoid(x):
    return jnp.minimum(x, 0.0) - jnp.log(1.0 + jnp.exp(-jnp.abs(x)))


def _mlstm_kernel(*refs, L, t_valid, has_state, nseq):
    if has_state:
        q_ref, k_ref, v_ref, s_ref, c0_ref, n0_ref, m0_ref, h_ref, c_ref, n_ref, m_ref = refs
    else:
        q_ref, k_ref, v_ref, s_ref, h_ref, c_ref, n_ref, m_ref = refs
    c = pl.program_id(1)

    @pl.when(c == 0)
    def _():
        if has_state:
            c_ref[...] = c0_ref[...]
            n_ref[...] = n0_ref[...]
            m_ref[...] = m0_ref[...]
        else:
            c_ref[...] = jnp.zeros(c_ref.shape, F32)
            n_ref[...] = jnp.zeros(n_ref.shape, F32)
            m_ref[...] = jnp.zeros(m_ref.shape, F32)

    row = lax.broadcasted_iota(jnp.int32, (L, L), 0)
    col = lax.broadcasted_iota(jnp.int32, (L, L), 1)
    causal = col <= row
    eye = col == row
    tok_col = c * L + lax.broadcasted_iota(jnp.int32, (L, 1), 0)
    valid_col = tok_col < t_valid
    for sq, hd in [(a, b) for a in range(nseq) for b in range(M_HEADS)]:
        lo, hi = hd * M_DH, (hd + 1) * M_DH
        rs = slice(sq * L, (sq + 1) * L)
        q = q_ref[rs, lo:hi]
        k = k_ref[rs, lo:hi]
        v = v_ref[rs, lo:hi]
        i_col = s_ref[rs, hd:hd + 1]
        lf_col = _log_sigmoid(s_ref[rs, M_HEADS + hd:M_HEADS + hd + 1])
        lf_col = jnp.where(valid_col, lf_col, 0.0)
        i_col = jnp.where(valid_col, i_col, -jnp.inf)
        if L == LANES:
            i_col = jnp.broadcast_to(i_col, (L, L))
            lf_c = jnp.broadcast_to(lf_col, (L, L))
            p0 = lf_c.astype(BF16)
            r1 = lf_c - p0.astype(F32)
            p1 = r1.astype(BF16)
            p2 = (r1 - p1.astype(F32)).astype(BF16)
            tril = jnp.where(causal, 1.0, 0.0).astype(BF16)
            b_col = (jnp.dot(tril, p0, preferred_element_type=F32) + jnp.dot(tril, p1, preferred_element_type=F32)
                     + jnp.dot(tril, p2, preferred_element_type=F32))
            i_row = jnp.transpose(i_col)[0:1, :]
            b_row = jnp.transpose(b_col)[0:1, :]
        else:
            i_row = jnp.sum(jnp.where(eye, i_col, 0.0), axis=0, keepdims=True)
            lf_row = jnp.sum(jnp.where(eye, lf_col, 0.0), axis=0, keepdims=True)
            b_col = jnp.sum(jnp.where(causal, lf_row, 0.0), axis=1, keepdims=True)
            b_row = jnp.sum(jnp.where(row <= col, lf_col, 0.0), axis=0, keepdims=True)
        m_prev = m_ref[sq, :, hd:hd + 1]
        dmat = jnp.where(causal, b_col - b_row + i_row, -jnp.inf)
        inter = b_col + m_prev
        m_row = jnp.maximum(jnp.max(dmat, axis=1, keepdims=True), inter)
        w = jnp.exp(dmat - m_row)
        w_inter = jnp.exp(inter - m_row)
        s = _bdot_t(q, k) * w
        cm = c_ref[sq, hd]
        nv = n_ref[sq, hd]
        num = _bdot(s, v) + w_inter * _bdot_t(q, cm)
        den = jnp.sum(s, axis=1, keepdims=True) + w_inter * jnp.sum(q * nv, axis=1, keepdims=True)
        h_ref[rs, lo:hi] = num / jnp.maximum(jnp.abs(den), jnp.exp(-m_row))
        b_last = b_col[L - 1:L, 0:1]
        dec_col = b_last - b_col + i_col
        dec_row = b_last - b_row + i_row
        m_new = jnp.maximum(b_last + m_prev, jnp.max(dec_row, axis=1, keepdims=True))
        ws_col = jnp.exp(dec_col - m_new)
        wc = jnp.exp(b_last + m_prev - m_new)
        vw = (v * ws_col).astype(BF16)
        upd = lax.dot_general(vw, k.astype(BF16), (((0,), (0,)), ((), ())), preferred_element_type=F32)
        c_ref[sq, hd] = wc * cm + upd
        n_ref[sq, hd] = wc * nv + jnp.sum(k * ws_col, axis=0, keepdims=True)
        m_ref[sq, :, hd:hd + 1] = m_new


def _mlstm(mq, mk, mv, small, nb, t_pad, t_valid, L, state=None, nseq=1):
    nc = t_pad // L
    assert nseq == 1 or (nc == 1 and nb % nseq == 0)
    has_state = state is not None
    rows = nseq * L
    blk = lambda b, c: (b * nc + c, 0)
    st4 = lambda b, c: (b, 0, 0, 0)
    st3 = lambda b, c: (b, 0, 0)
    in_specs = [pl.BlockSpec((rows, M_WIDTH), blk)] * 3 + [pl.BlockSpec((rows, LANES), blk)]
    args = [mq, mk, mv, small]
    if has_state:
        c0, n0, m0 = state
        in_specs += [pl.BlockSpec((nseq, M_HEADS, M_DH, M_DH), st4),
                     pl.BlockSpec((nseq, M_HEADS, 1, M_DH), st4),
                     pl.BlockSpec((nseq, 1, M_HEADS), st3)]
        args += [c0, n0.reshape(nb, M_HEADS, 1, M_DH), m0.reshape(nb, 1, M_HEADS)]
    out_specs = [pl.BlockSpec((rows, M_WIDTH), blk),
                 pl.BlockSpec((nseq, M_HEADS, M_DH, M_DH), st4),
                 pl.BlockSpec((nseq, M_HEADS, 1, M_DH), st4),
                 pl.BlockSpec((nseq, 1, M_HEADS), st3)]
    out_shape = [jax.ShapeDtypeStruct((nb * t_pad, M_WIDTH), F32),
                 jax.ShapeDtypeStruct((nb, M_HEADS, M_DH, M_DH), F32),
                 jax.ShapeDtypeStruct((nb, M_HEADS, 1, M_DH), F32),
                 jax.ShapeDtypeStruct((nb, 1, M_HEADS), F32)]
    h, cs, ns, ms = pl.pallas_call(
        functools.partial(_mlstm_kernel, L=L, t_valid=t_valid, has_state=has_state, nseq=nseq),
        grid=(nb // nseq, nc),
        in_specs=in_specs,
        out_specs=out_specs,
        out_shape=out_shape,
        compiler_params=_cparams(("parallel", "arbitrary")),
        name="mlstm",
    )(*args)
    return h, cs, ns.reshape(nb, M_HEADS, M_DH), ms.reshape(nb, M_HEADS)


def _stack_heads(qt, g):
    t = qt.shape[0]
    z = jnp.zeros((t, A_DH), F32)
    parts = []
    for hh in range(A_HPG):
        hd = g * A_HPG + hh
        qh = qt[:, hd * A_DH:(hd + 1) * A_DH] * (ATT_SCALE * LOG2E)
        parts.append(jnp.concatenate([qh, z], axis=1) if g == 0 else jnp.concatenate([z, qh], axis=1))
    return jnp.concatenate(parts, axis=0).astype(BF16)


def _gate_cols(small, g, br):
    cols = []
    for hh in range(A_HPG):
        c0 = 2 * M_HEADS + (g * A_HPG + hh) * 3 + br
        cols.append(_sigmoid(small[:, c0:c0 + 1]))
    return jnp.concatenate(cols, axis=0)


def _compress(k_ref, v_ref, nseg, wbd_ref, pe_ref, kg0):
    acc_lo = jnp.zeros((nseg, 2 * LANES), F32)
    acc_hi = jnp.zeros((nseg, 2 * LANES), F32)
    for l in range(CMP_STRIDE):
        xl = jnp.concatenate([k_ref[pl.ds(l, nseg, stride=CMP_STRIDE), :],
                              v_ref[pl.ds(l, nseg, stride=CMP_STRIDE), :]], axis=1)
        acc_lo = acc_lo + _bdot(xl + pe_ref[l], wbd_ref[l])
        acc_hi = acc_hi + _bdot(xl + pe_ref[CMP_STRIDE + l], wbd_ref[CMP_STRIDE + l])
    return _compress_finish(acc_lo, acc_hi, nseg, kg0)


def _compress_grouped(x_ref, nseg, wbd_ref, pe_ref, kg0):
    acc_lo = jnp.zeros((nseg, 2 * LANES), F32)
    acc_hi = jnp.zeros((nseg, 2 * LANES), F32)
    pe_lo = jnp.zeros((SUBLANES, 2 * LANES), F32)
    pe_hi = jnp.zeros((SUBLANES, 2 * LANES), F32)
    for l in range(CMP_STRIDE):
        xl = x_ref[l].astype(BF16)
        acc_lo = acc_lo + jnp.dot(xl, wbd_ref[l], preferred_element_type=F32)
        acc_hi = acc_hi + jnp.dot(xl, wbd_ref[CMP_STRIDE + l], preferred_element_type=F32)
        pe_lo = pe_lo + _bdot(jnp.broadcast_to(pe_ref[l], (SUBLANES, 2 * LANES)), wbd_ref[l])
        pe_hi = pe_hi + _bdot(jnp.broadcast_to(pe_ref[CMP_STRIDE + l], (SUBLANES, 2 * LANES)),
                              wbd_ref[CMP_STRIDE + l])
    return _compress_finish(acc_lo + pe_lo[0:1, :], acc_hi + pe_hi[0:1, :], nseg, kg0)


def _compress_finish(acc_lo, acc_hi, nseg, kg0):
    kv = acc_lo + pltpu.roll(acc_hi, nseg - 1, 0)
    kc = kv[:, 0:LANES]
    vc = kv[:, LANES:2 * LANES]
    lane = lax.broadcasted_iota(jnp.int32, (nseg, LANES), 1)
    sq = kc * kc
    ms0 = jnp.sum(jnp.where(lane < A_DH, sq, 0.0), axis=1, keepdims=True) * (1.0 / A_DH)
    ms1 = jnp.sum(jnp.where(lane >= A_DH, sq, 0.0), axis=1, keepdims=True) * (1.0 / A_DH)
    ms = jnp.where(lane < A_DH, ms0, ms1)
    kc = kc * lax.rsqrt(ms + EPS) * kg0
    return kc, vc


def _cmp_branch(qn_g, kc_b, vc_b, tpos_rows, nseg, n_tok):
    s = _bdot_t(qn_g, kc_b)
    nidx = lax.broadcasted_iota(jnp.int32, (1, nseg), 1)
    vis = (nidx * CMP_STRIDE + (CMP_LEN - 1)) <= tpos_rows
    sm = jnp.where(vis, s, NEG_BIG)
    mx = jnp.max(sm, axis=1, keepdims=True)
    e = jnp.where(vis, jnp.exp2(sm - mx), 0.0)
    d = jnp.sum(e, axis=1, keepdims=True)
    p = e / jnp.where(d > 0, d, 1.0)
    o = _bdot(p, vc_b)
    imp = p[0:n_tok]
    for hh in range(1, A_HPG):
        imp = imp + p[hh * n_tok:(hh + 1) * n_tok]
    return o, imp


def _masked_attn_direct(q_g, k_parts, v_parts, allowed_parts, feature_major):
    ss = [jnp.where(al, _bdot(q_g, kk) if fm else _bdot_t(q_g, kk), NEG_BIG)
          for kk, al, fm in zip(k_parts, allowed_parts, feature_major)]
    mx = ss[0].max(axis=1, keepdims=True)
    for s in ss[1:]:
        mx = jnp.maximum(mx, s.max(axis=1, keepdims=True))
    num = None
    den = None
    for s, al, vv, fm in zip(ss, allowed_parts, v_parts, feature_major):
        e = jnp.where(al, jnp.exp2(s - mx), 0.0)
        dd = jnp.sum(e, axis=1, keepdims=True)
        oo = _bdot_t(e, vv) if fm else _bdot(e, vv)
        num = oo if num is None else num + oo
        den = dd if den is None else den + dd
    return num / jnp.where(den > 0, den, 1.0)


def _assemble_heads(o_groups, n_tok):
    pieces = []
    for g in range(A_KV):
        for hh in range(A_HPG):
            pieces.append(o_groups[g][hh * n_tok:(hh + 1) * n_tok, g * A_DH:(g + 1) * A_DH])
    return jnp.concatenate(pieces, axis=1)


def _lane_rep(a, rep):
    return a if rep == 1 else jnp.concatenate([a] * rep, axis=1)


def _nsa_prompt_kernel(q_ref, qr_ref, small_ref, rows_ref, win_ref, wbd_ref, pe_ref, kg0_ref,
                       pool_ref, o_ref,
                       kraw_sc, vraw_sc, kc_sc, vct_sc, sel_sc, m_sc, acc_sc, s_sc, *, T, tq, kc_len):
    qi = pl.program_id(1)
    nseg = T // CMP_STRIDE
    nsb = T // SEL_LEN
    bpc = kc_len // SEL_LEN

    @pl.when(qi == 0)
    def _():
        kraw_sc[...] = rows_ref[:, 0:LANES]
        vraw_sc[...] = rows_ref[:, LANES:2 * LANES]
        kc, vc = _compress(kraw_sc, vraw_sc, nseg, wbd_ref, pe_ref, kg0_ref[...])
        kc_sc[...] = kc
        vct_sc[...] = jnp.transpose(vc)

    t0 = qi * tq
    tpos = t0 + lax.broadcasted_iota(jnp.int32, (1, tq), 1)
    tpos4 = _lane_rep(tpos, A_HPG)
    q = q_ref[...]
    qr = qr_ref[...]
    small_t = jnp.transpose(small_ref[...])
    kc_b = kc_sc[...].astype(BF16)
    vct_b = vct_sc[...].astype(BF16)
    bidx = lax.broadcasted_iota(jnp.int32, (nsb, tq), 0)
    cur = tpos // SEL_LEN
    vis = (lax.broadcasted_iota(jnp.int32, (nseg, 1), 0) * CMP_STRIDE + (CMP_LEN - 1)) <= tpos4
    qr_gs = [_stack_heads(qr, g) for g in range(A_KV)]
    o_cmps = []
    for g in range(A_KV):
        sm = jnp.where(vis, _bdot_t(kc_b, _stack_heads(q, g)), NEG_BIG)
        mx = jnp.max(sm, axis=0, keepdims=True)
        e = jnp.where(vis, jnp.exp2(sm - mx), 0.0)
        d = jnp.sum(e, axis=0, keepdims=True)
        p = e / jnp.where(d > 0, d, 1.0)
        o_cmps.append(jnp.dot(vct_b, p.astype(BF16), preferred_element_type=F32))
        imp = p[:, 0:tq]
        for hh in range(1, A_HPG):
            imp = imp + p[:, hh * tq:(hh + 1) * tq]
        ih, il = _split(imp)
        imp_t = (jnp.dot(pool_ref[...], ih, preferred_element_type=F32)
                 + jnp.dot(pool_ref[...], il, preferred_element_type=F32))[0:nsb]
        val = jnp.where(bidx < cur, imp_t, -1.0)
        rank = jnp.zeros((nsb, tq), F32)
        for bp in range(nsb):
            vb = val[bp:bp + 1, :]
            rank = rank + jnp.where(vb > val, 1.0, jnp.where((vb == val) & (bidx > bp), 1.0, 0.0))
        sel_sc[g] = jnp.where(((rank < (N_SEL - 1)) & (bidx < cur)) | (bidx == cur), 1.0, 0.0)

    m_sc[...] = jnp.full(m_sc.shape, M_INIT, F32)
    acc_sc[...] = jnp.zeros(acc_sc.shape, F32)

    def with_ones_row(vt_, g):
        vb = vt_.astype(BF16)
        r0, pad = (1 - g) * A_DH, 2 * SUBLANES
        ones = jnp.ones((pad, vb.shape[1]), BF16)
        return jnp.concatenate(([vb[0:r0]] if r0 else []) + [ones, vb[r0 + pad:]], axis=0)

    def sel_body(c, carry):
        k0 = pl.multiple_of(c * kc_len, kc_len)
        kb = rows_ref[pl.ds(k0, kc_len), 2 * LANES:3 * LANES].astype(BF16)
        vt = jnp.transpose(rows_ref[pl.ds(k0, kc_len), 3 * LANES:4 * LANES])
        causal = (k0 + lax.broadcasted_iota(jnp.int32, (kc_len, 1), 0)) <= tpos
        for g in range(A_KV):
            s_sc[g, 0:kc_len, :] = _bdot_t(kb, qr_gs[g])
        for g in range(A_KV):
            selc = sel_sc[g, pl.ds(pl.multiple_of(c * bpc, bpc), bpc), :]
            selx = jnp.concatenate([jnp.broadcast_to(selc[j:j + 1, :], (SEL_LEN, tq)) for j in range(bpc)], axis=0)
            bias = jnp.where(causal & (selx > 0.5), 0.0, NEG_BIG)
            sm = s_sc[g, 0:kc_len, :] + _lane_rep(bias, A_HPG)
            m_prev = m_sc[g]
            m_new = jnp.maximum(m_prev, jnp.max(sm, axis=0, keepdims=True))
            alpha = jnp.exp2(m_prev - m_new)
            p = jnp.exp2(sm - m_new)
            acc_sc[g] = alpha * acc_sc[g] + jnp.dot(with_ones_row(vt, g), p.astype(BF16),
                                                    preferred_element_type=F32)
            m_sc[g] = m_new
        return carry

    lax.fori_loop(0, (t0 + tq + kc_len - 1) // kc_len, sel_body, 0)

    wk = min(WINDOW + tq, T)
    w0 = pl.multiple_of(jnp.clip(t0 + tq - wk, 0, T - wk), tq)
    kw = win_ref[pl.ds(w0, wk), 0:LANES].astype(BF16)
    vwt = jnp.transpose(win_ref[pl.ds(w0, wk), LANES:2 * LANES])
    wdiff = tpos - (w0 + lax.broadcasted_iota(jnp.int32, (wk, 1), 0))
    wbias = _lane_rep(jnp.where((wdiff >= 0) & (wdiff < WINDOW), 0.0, NEG_BIG), A_HPG)

    def gate_row(g, br):
        cols = [2 * M_HEADS + (g * A_HPG + hh) * 3 + br for hh in range(A_HPG)]
        return jnp.concatenate([_sigmoid(small_t[c0:c0 + 1, :]) for c0 in cols], axis=1)

    for g in range(A_KV):
        s_sc[g, 0:wk, :] = _bdot_t(kw, qr_gs[g])
    o_ts = []
    for g in range(A_KV):
        den = (1 - g) * A_DH
        acc = acc_sc[g]
        l = acc[den:den + 1, :]
        o_sel = acc / jnp.where(l > 0, l, 1.0)
        sw = s_sc[g, 0:wk, :] + wbias
        pw = jnp.exp2(sw - jnp.max(sw, axis=0, keepdims=True))
        ow = jnp.dot(with_ones_row(vwt, g), pw.astype(BF16), preferred_element_type=F32)
        o_win = ow / ow[den:den + 1, :]
        o_ts.append(gate_row(g, 0) * o_cmps[g] + gate_row(g, 1) * o_sel + gate_row(g, 2) * o_win)
    for j in range(A_HEADS // 2):
        g, h0 = j // (A_HPG // 2), 2 * (j % (A_HPG // 2))
        og = o_ts[g][g * A_DH:(g + 1) * A_DH, :]
        pair = jnp.concatenate([og[:, h0 * tq:(h0 + 1) * tq], og[:, (h0 + 1) * tq:(h0 + 2) * tq]], axis=0)
        o_ref[:, j * LANES:(j + 1) * LANES] = jnp.transpose(pair)


def _nsa_prompt(q, qr, small, rows, win, wbd, pe, kg0, nb, T):
    tq = 128
    kc_len = _pick_tile(T, 512)
    nq = T // tq
    nseg = T // CMP_STRIDE
    nsb = T // SEL_LEN
    nsb_p = -(-nsb // SUBLANES) * SUBLANES
    pool = (jnp.arange(nsb_p)[:, None] == jnp.arange(nseg)[None, :] // (SEL_LEN // CMP_STRIDE)).astype(BF16)
    tile = lambda b, i: (b * nq + i, 0)
    per_b = lambda b, i: (b, 0)
    c2 = lambda b, i: (0, 0)
    c3 = lambda b, i: (0, 0, 0)
    c4 = A_HPG * tq
    return pl.pallas_call(
        functools.partial(_nsa_prompt_kernel, T=T, tq=tq, kc_len=kc_len),
        grid=(nb, nq),
        in_specs=[pl.BlockSpec((tq, A_WIDTH), tile), pl.BlockSpec((tq, A_WIDTH), tile),
                  pl.BlockSpec((tq, LANES), tile),
                  pl.BlockSpec((T, 4 * LANES), per_b), pl.BlockSpec((T, 2 * LANES), per_b),
                  pl.BlockSpec(wbd.shape, c3), pl.BlockSpec(pe.shape, c3), pl.BlockSpec(kg0.shape, c2),
                  pl.BlockSpec(pool.shape, c2)],
        out_specs=pl.BlockSpec((tq, A_WIDTH), tile),
        out_shape=jax.ShapeDtypeStruct((nb * T, A_WIDTH), F32),
        scratch_shapes=[pltpu.VMEM((T, LANES), F32), pltpu.VMEM((T, LANES), F32),
                        pltpu.VMEM((nseg, LANES), F32), pltpu.VMEM((LANES, nseg), F32),
                        pltpu.VMEM((A_KV, nsb, tq), F32),
                        pltpu.VMEM((A_KV, 1, c4), F32),
                        pltpu.VMEM((A_KV, LANES, c4), F32),
                        pltpu.VMEM((A_KV, max(kc_len, min(WINDOW + tq, T)), c4), F32)],
        compiler_params=_cparams(("parallel", "arbitrary")),
        name="nsa_prompt",
    )(q, qr, small, rows, win, wbd, pe, kg0, pool)


def _nsa_sample_kernel(pt_ref, cache_ref, q_ref, qr_ref, small_ref, rows_ref, winnew_ref, winbuf_ref,
                       wbd_ref, pe_ref, kg0_ref, pool_ref, expand_ref,
                       o_ref, winout_ref,
                       page_buf, xperm_sc, sems, *, n_pages, past_len, t_valid):
    b = pl.program_id(0)
    nb = pl.num_programs(0)
    tp = SAMPLE_PAD_T
    nseg = past_len // CMP_STRIDE
    nsb = past_len // SEL_LEN
    wbuf = winbuf_ref.shape[1]

    def page_copy(bb, p):
        page = pt_ref[bb * n_pages + p]
        dst_lanes = pl.ds(pl.multiple_of(p * PAGE_SIZE, PAGE_SIZE), PAGE_SIZE)
        return pltpu.make_async_copy(cache_ref.at[page], page_buf.at[bb % 2, :, dst_lanes], sems.at[bb % 2])

    def start_all(bb):
        def body(p, c):
            page_copy(bb, p).start()
            return c
        lax.fori_loop(0, n_pages, body, 0)

    def wait_all(bb):
        def body(p, c):
            page_copy(bb, p).wait()
            return c
        lax.fori_loop(0, n_pages, body, 0)

    @pl.when(b == 0)
    def _():
        start_all(b)

    @pl.when(b + 1 < nb)
    def _():
        start_all(b + 1)

    wait_all(b)
    cmp_buf = page_buf.at[b % 2, pl.ds(0, 2 * LANES), :]
    sel_buf = page_buf.at[b % 2, pl.ds(2 * LANES, 2 * LANES), :]

    seg_pp = PAGE_SIZE // CMP_STRIDE
    pr = lax.broadcasted_iota(jnp.int32, (PAGE_SIZE, PAGE_SIZE), 0)
    pc = lax.broadcasted_iota(jnp.int32, (PAGE_SIZE, PAGE_SIZE), 1)
    perm = jnp.where(pc == CMP_STRIDE * (pr % seg_pp) + pr // seg_pp, 1.0, 0.0).astype(BF16)
    for p in range(n_pages):
        xp = _bdot_t(perm, cmp_buf[:, p * PAGE_SIZE:(p + 1) * PAGE_SIZE])
        for l in range(CMP_STRIDE):
            xperm_sc[l, p * seg_pp:(p + 1) * seg_pp, :] = xp[l * seg_pp:(l + 1) * seg_pp, :]
    kc, vc = _compress_grouped(xperm_sc, nseg, wbd_ref, pe_ref, kg0_ref[...])
    kc_b = kc.astype(BF16)
    vc_b = vc.astype(BF16)
    q = q_ref[...]
    qr = qr_ref[...]
    small = small_ref[...]
    tpos_col = past_len + lax.broadcasted_iota(jnp.int32, (tp, 1), 0)
    tpos_rows = jnp.concatenate([tpos_col] * A_HPG, axis=0)
    bp_idx = lax.broadcasted_iota(jnp.int32, (nsb, nsb), 0)
    b_idx = lax.broadcasted_iota(jnp.int32, (nsb, nsb), 1)
    o_cmps = []
    sels = []
    for g in range(A_KV):
        qn_g = _stack_heads(q, g)
        o_cmp, imp = _cmp_branch(qn_g, kc_b, vc_b, tpos_rows, nseg, tp)
        o_cmps.append(o_cmp)
        imp_sel = _dot2_exact_rhs(imp, pool_ref[...])
        imp_pad = jnp.concatenate([imp_sel, jnp.zeros((nsb - tp, nsb), F32)], axis=0)
        imp_t = jnp.transpose(imp_pad)
        rows_sel = []
        for t in range(tp):
            if t < t_valid:
                row_t = imp_sel[t:t + 1, :]
                col_t = imp_t[:, t:t + 1]
                ahead = jnp.where(col_t > row_t, 1.0, jnp.where((col_t == row_t) & (bp_idx < b_idx), 1.0, 0.0))
                rank = jnp.sum(ahead, axis=0, keepdims=True)
                rows_sel.append(jnp.where(rank < (N_SEL - 1), 1.0, 0.0))
            else:
                rows_sel.append(jnp.zeros((1, nsb), F32))
        sels.append(jnp.concatenate(rows_sel, axis=0))

    new_idx = lax.broadcasted_iota(jnp.int32, (tp, tp), 1)
    tok_idx = lax.broadcasted_iota(jnp.int32, (tp, tp), 0)
    new_ok = jnp.concatenate([jnp.where(new_idx <= tok_idx, 1.0, 0.0)] * A_HEADS, axis=0) > 0.5
    wpos = past_len - wbuf + lax.broadcasted_iota(jnp.int32, (1, wbuf), 1)
    wdiff = tpos_col - wpos
    win_ok = jnp.concatenate([jnp.where((wdiff >= 0) & (wdiff < WINDOW), 1.0, 0.0)] * A_HEADS, axis=0) > 0.5
    k_past = sel_buf[0:LANES, :].astype(BF16)
    v_past = sel_buf[LANES:2 * LANES, :].astype(BF16)
    k_new = rows_ref[:, 2 * LANES:3 * LANES]
    v_new = rows_ref[:, 3 * LANES:4 * LANES]
    kw_past = winbuf_ref[0:LANES, :]
    vw_past = winbuf_ref[LANES:2 * LANES, :]
    kw_new = winnew_ref[:, 0:LANES]
    vw_new = winnew_ref[:, LANES:2 * LANES]
    r4 = A_HPG * tp
    qr_all = jnp.concatenate([_stack_heads(qr, g) for g in range(A_KV)], axis=0)
    mk = jnp.dot(jnp.concatenate(sels, axis=0).astype(BF16), expand_ref[...],
                 preferred_element_type=F32)
    past_ok = jnp.concatenate([mk[g * tp:(g + 1) * tp] for g in range(A_KV) for _ in range(A_HPG)], axis=0) > 0.5
    o_sel = _masked_attn_direct(qr_all, [k_past, k_new], [v_past, v_new], [past_ok, new_ok], [True, False])
    o_win = _masked_attn_direct(qr_all, [kw_past, kw_new], [vw_past, vw_new], [win_ok, new_ok], [True, False])
    o_groups = []
    for g in range(A_KV):
        rs = slice(g * r4, (g + 1) * r4)
        o_groups.append(_gate_cols(small, g, 0) * o_cmps[g] + _gate_cols(small, g, 1) * o_sel[rs]
                        + _gate_cols(small, g, 2) * o_win[rs])
    o_ref[...] = _assemble_heads(o_groups, tp)

    rolled = pltpu.roll(winbuf_ref[...], wbuf - t_valid, 1)
    new_t = jnp.transpose(jnp.concatenate([winnew_ref[...], jnp.zeros((LANES - tp, 2 * LANES), F32)], axis=0))
    new_t = pltpu.roll(new_t, LANES - t_valid, 1)
    lane = lax.broadcasted_iota(jnp.int32, (2 * LANES, LANES), 1)
    winout_ref[:, 0:wbuf - LANES] = rolled[:, 0:wbuf - LANES]
    winout_ref[:, wbuf - LANES:wbuf] = jnp.where(lane < LANES - t_valid, rolled[:, wbuf - LANES:wbuf], new_t)


def _nsa_sample(page_table, cache, q, qr, small, rows, winnew, winbuf, wbd, pe, kg0, t_valid):
    nb, n_pages = page_table.shape
    past_len = n_pages * PAGE_SIZE
    nseg = past_len // CMP_STRIDE
    nsb = past_len // SEL_LEN
    tp = SAMPLE_PAD_T
    wbuf = winbuf.shape[2]
    pool = (jnp.arange(nseg)[:, None] // (SEL_LEN // CMP_STRIDE) == jnp.arange(nsb)[None, :]).astype(BF16)
    expand = (jnp.arange(nsb)[:, None] == jnp.arange(past_len)[None, :] // SEL_LEN).astype(BF16)
    tile = lambda b, pt: (b, 0)
    c2 = lambda b, pt: (0, 0)
    c3 = lambda b, pt: (0, 0, 0)
    gs = pltpu.PrefetchScalarGridSpec(
        num_scalar_prefetch=1,
        grid=(nb,),
        in_specs=[pl.BlockSpec(memory_space=pl.ANY),
                  pl.BlockSpec((tp, A_WIDTH), tile), pl.BlockSpec((tp, A_WIDTH), tile),
                  pl.BlockSpec((tp, LANES), tile), pl.BlockSpec((tp, 4 * LANES), tile),
                  pl.BlockSpec((tp, 2 * LANES), tile),
                  pl.BlockSpec((None, 2 * LANES, wbuf), lambda b, pt: (b, 0, 0)),
                  pl.BlockSpec(wbd.shape, c3), pl.BlockSpec(pe.shape, c3), pl.BlockSpec(kg0.shape, c2),
                  pl.BlockSpec(pool.shape, c2), pl.BlockSpec(expand.shape, c2)],
        out_specs=[pl.BlockSpec((tp, A_WIDTH), tile),
                   pl.BlockSpec((None, 2 * LANES, wbuf), lambda b, pt: (b, 0, 0))],
        scratch_shapes=[pltpu.VMEM((2, 4 * LANES, past_len), F32),
                        pltpu.VMEM((CMP_STRIDE, past_len // CMP_STRIDE, 2 * LANES), F32),
                        pltpu.SemaphoreType.DMA((2,))],
    )
    return pl.pallas_call(
        functools.partial(_nsa_sample_kernel, n_pages=n_pages, past_len=past_len, t_valid=t_valid),
        grid_spec=gs,
        out_shape=[jax.ShapeDtypeStruct((nb * tp, A_WIDTH), F32),
                   jax.ShapeDtypeStruct((nb, 2 * LANES, wbuf), F32)],
        compiler_params=_cparams(("arbitrary",)),
        name="nsa_sample",
    )(page_table.reshape(-1), cache, q, qr, small, rows, winnew, winbuf, wbd, pe, kg0, pool, expand)


MOE_TM = 256
SEG_ALIGN = 8
MOE_RL = -(-(MOE_TM * TOP_K + N_EXPERTS * (SEG_ALIGN - 1)) // LANES) * LANES


def _pack_halves(x, bf16_exact=False):
    w = x.shape[1] // 2
    bits = lax.bitcast_convert_type(x if bf16_exact else x.astype(BF16).astype(F32), jnp.uint32)
    return bits[:, :w] | (bits[:, w:] >> 16)


def _unpack_halves(u):
    hi = lax.bitcast_convert_type(u & jnp.uint32(0xFFFF0000), F32).astype(BF16)
    lo = lax.bitcast_convert_type(u << 16, F32).astype(BF16)
    return hi, lo


def _route_and_sort(h2, wrt_ref, brt_ref, xsl_ref, info_ref, cnt_ref, tm, t_mod, t_valid, m_valid):
    ne = N_EXPERTS
    h2b = h2.astype(BF16)
    h2l = (h2 - h2b.astype(F32)).astype(BF16)
    wh, wl = _split(wrt_ref[...])
    lt = _bdot_t(wh, h2b) + _bdot_t(wl, h2b) + _bdot_t(wh, h2l) + brt_ref[...]
    eidx = lax.broadcasted_iota(jnp.int32, (ne, tm), 0)
    rank = jnp.zeros((ne, tm), F32)
    for ep in range(ne):
        v = lt[ep:ep + 1, :]
        rank = rank + jnp.where(v > lt, 1.0, jnp.where((v == lt) & (eidx > ep), 1.0, 0.0))
    sel = rank < TOP_K
    if t_mod is not None:
        tok = pl.program_id(0) * tm + lax.broadcasted_iota(jnp.int32, (1, tm), 1)
        sel = sel & ((tok % t_mod) < t_valid) & (tok < m_valid)
    mx = jnp.max(jnp.where(sel, lt, NEG_BIG), axis=0, keepdims=True)
    ex = jnp.where(sel, jnp.exp(lt - mx), 0.0)
    den = jnp.sum(ex, axis=0, keepdims=True)
    gate = ex / jnp.where(den > 0, den, 1.0)
    self_ = jnp.where(sel, 1.0, 0.0)
    selb = self_.astype(BF16)
    er = lax.broadcasted_iota(jnp.int32, (ne, ne), 0)
    ec = lax.broadcasted_iota(jnp.int32, (ne, ne), 1)
    c = jnp.dot(jnp.where(ec <= er, 1.0, 0.0).astype(BF16), selb, preferred_element_type=F32)
    tr = lax.broadcasted_iota(jnp.int32, (tm, tm), 0)
    tc = lax.broadcasted_iota(jnp.int32, (tm, tm), 1)
    rk = jnp.dot(selb, jnp.where(tr < tc, 1.0, 0.0).astype(BF16), preferred_element_type=F32)
    cnt = jnp.sum(self_, axis=1, keepdims=True)
    cnt_al = jnp.floor((cnt + (SEG_ALIGN - 1)) * (1.0 / SEG_ALIGN)) * SEG_ALIGN
    cnt_b = jnp.broadcast_to(cnt_al, (ne, LANES))
    cnt_ref[...] = cnt_b
    off = jnp.dot(jnp.where(ec < er, 1.0, 0.0).astype(BF16), cnt_b.astype(BF16), preferred_element_type=F32)
    rowidx = off[:, 0:1] + rk
    rows_k, gates_k, exps_k = [], [], []
    for k in range(1, TOP_K + 1):
        mk = sel & (c == k)
        has = jnp.sum(jnp.where(mk, 1.0, 0.0), axis=0, keepdims=True)
        rows_k.append(jnp.sum(jnp.where(mk, rowidx, 0.0), axis=0, keepdims=True) + has - 1.0)
        gates_k.append(jnp.sum(jnp.where(mk, gate, 0.0), axis=0, keepdims=True))
        exps_k.append(jnp.sum(jnp.where(mk, eidx.astype(F32), 0.0), axis=0, keepdims=True))
    info_ref[...] = jnp.concatenate(rows_k + gates_k + exps_k + [jnp.zeros((4, tm), F32)], axis=0)
    ridx = lax.broadcasted_iota(jnp.int32, (MOE_RL, tm), 0).astype(F32)
    perm = jnp.zeros((MOE_RL, tm), F32)
    for k in range(TOP_K):
        perm = jnp.where(ridx == rows_k[k], 1.0, perm)
    xs = jnp.dot(perm.astype(BF16), h2b, preferred_element_type=F32)
    xsl_ref[...] = _pack_halves(xs, bf16_exact=True)


def _mixout_kernel(x_ref, hm_ref, on_ref, mod_ref, gmix_ref, gffn_ref,
                   wog_ref, bog_ref, wum_ref, wua_ref, wout_ref, wrt_ref, brt_ref,
                   x1_ref, xsl_ref, info_ref, cnt_ref, *, tm, t_mod, t_valid, m_valid, n_real):
    if n_real is not None:
        @pl.when(pl.program_id(0) >= n_real)
        def _():
            xsl_ref[...] = jnp.zeros(xsl_ref.shape, jnp.uint32)
            info_ref[...] = jnp.zeros(info_ref.shape, F32)
            cnt_ref[...] = jnp.zeros(cnt_ref.shape, F32)

        @pl.when(pl.program_id(0) < n_real)
        def _():
            _mixout_body(x_ref, hm_ref, on_ref, mod_ref, gmix_ref, gffn_ref, wog_ref, bog_ref, wum_ref,
                         wua_ref, wout_ref, wrt_ref, brt_ref, x1_ref, xsl_ref, info_ref, cnt_ref,
                         tm, t_mod, t_valid, m_valid)
    else:
        _mixout_body(x_ref, hm_ref, on_ref, mod_ref, gmix_ref, gffn_ref, wog_ref, bog_ref, wum_ref,
                     wua_ref, wout_ref, wrt_ref, brt_ref, x1_ref, xsl_ref, info_ref, cnt_ref,
                     tm, t_mod, t_valid, m_valid)


def _mixout_body(x_ref, hm_ref, on_ref, mod_ref, gmix_ref, gffn_ref,
                 wog_ref, bog_ref, wum_ref, wua_ref, wout_ref, wrt_ref, brt_ref,
                 x1_ref, xsl_ref, info_ref, cnt_ref, tm, t_mod, t_valid, m_valid):
    d = D_MODEL
    x = x_ref[...]
    sh1, sc1, gt1 = mod_ref[:, 0:d], mod_ref[:, d:2 * d], mod_ref[:, 2 * d:3 * d]
    sh2, sc2 = mod_ref[:, 3 * d:4 * d], mod_ref[:, 4 * d:5 * d]
    h = _rmsnorm_rows(x, gmix_ref[...]) * (1.0 + sc1) + sh1
    hb = h.astype(BF16)
    mo = jnp.dot(hb, wog_ref[:, 0:M_WIDTH], preferred_element_type=F32) + bog_ref[:, 0:M_WIDTH]
    ym = _bdot(_sigmoid(mo) * hm_ref[...], wum_ref[...])
    ya = _bdot(on_ref[...], wua_ref[...])
    ga = jnp.dot(hb, wog_ref[:, M_WIDTH:M_WIDTH + d], preferred_element_type=F32) + bog_ref[:, M_WIDTH:M_WIDTH + d]
    u = _sigmoid(ga) * ym
    gb = (jnp.dot(hb, wog_ref[:, M_WIDTH + d:M_WIDTH + 2 * d], preferred_element_type=F32)
          + bog_ref[:, M_WIDTH + d:M_WIDTH + 2 * d])
    u = u + _sigmoid(gb) * ya
    x1 = x + gt1 * _bdot(u, wout_ref[...])
    x1_ref[...] = x1
    h2 = _rmsnorm_rows(x1, gffn_ref[...]) * (1.0 + sc2) + sh2
    _route_and_sort(h2, wrt_ref, brt_ref, xsl_ref, info_ref, cnt_ref, tm, t_mod, t_valid, m_valid)


def _mixout_with_shared(*refs, n_shared, **kw):
    n_in = 13
    _mixout_kernel(*refs[:n_in], *refs[n_in + n_shared:], **kw)


def _mixout(x2, hm, on, mod3, gmix, gffn, wts, tiles_per_mod, nt_total, tile0=0, shared=None,
            t_mod=None, t_valid=None, m_valid=None):
    m = x2.shape[0]
    tm = MOE_TM
    nt = m // tm
    (wog, bog, wum, wua, wout, wr, br) = wts
    r = mod3.shape[1]
    n_extra = nt_total - tile0 - nt if shared is None else 0
    row = lambda i: (jnp.minimum(i, nt - 1), 0)
    const = lambda i: (0, 0)
    in_specs = [pl.BlockSpec((tm, D_MODEL), row), pl.BlockSpec((tm, M_WIDTH), row),
                pl.BlockSpec((tm, A_WIDTH), row),
                pl.BlockSpec((None, r, 6 * D_MODEL), lambda i: (jnp.minimum(i, nt - 1) // tiles_per_mod, 0, 0)),
                pl.BlockSpec((1, D_MODEL), const), pl.BlockSpec((1, D_MODEL), const),
                pl.BlockSpec(wog.shape, const), pl.BlockSpec(bog.shape, const),
                pl.BlockSpec(wum.shape, const), pl.BlockSpec(wua.shape, const),
                pl.BlockSpec(wout.shape, const), pl.BlockSpec(wr.shape, const),
                pl.BlockSpec(br.shape, const)]
    args = [x2, hm, on, mod3, gmix, gffn, wog, bog, wum, wua, wout, wr, br]
    kw = dict(tm=tm, t_mod=t_mod, t_valid=t_valid, m_valid=m_valid, n_real=nt if n_extra else None)
    body = functools.partial(_mixout_kernel, **kw)
    aliases = {}
    if shared is not None:
        in_specs += [pl.BlockSpec(memory_space=pl.ANY)] * len(shared)
        aliases = {len(args) + j: 1 + j for j in range(len(shared))}
        args += list(shared)
        body = functools.partial(_mixout_with_shared, n_shared=len(shared), **kw)
    return pl.pallas_call(
        body,
        grid=(nt + n_extra,),
        in_specs=in_specs,
        out_specs=[pl.BlockSpec((tm, D_MODEL), row),
                   pl.BlockSpec((MOE_RL, D_MODEL // 2), lambda i: (tile0 + i, 0)),
                   pl.BlockSpec((16, tm), lambda i: (0, tile0 + i)),
                   pl.BlockSpec((None, N_EXPERTS, LANES), lambda i: (tile0 + i, 0, 0))],
        out_shape=[jax.ShapeDtypeStruct((m, D_MODEL), F32),
                   jax.ShapeDtypeStruct((nt_total * MOE_RL, D_MODEL // 2), jnp.uint32),
                   jax.ShapeDtypeStruct((16, nt_total * tm), F32),
                   jax.ShapeDtypeStruct((nt_total, N_EXPERTS, LANES), F32)],
        input_output_aliases=aliases,
        compiler_params=_cparams(("arbitrary" if n_extra else "parallel",)),
        name="mixout",
    )(*args)


MOE_BM = 256
MOE_CH = 1024


def _moe_kernel(be_ref, na_ref, grp_ref, first_ref, wslot_ref, nxt_ref,
                xsl_ref, wgu_ref, bgu_ref, wdn_ref, bdn_ref, ysl_ref,
                wgu_bf, wdn_bf, xbuf, ybuf, wgu_f, wdn_f, sem_in, sem_out, sem_w, *, trash_row0):
    i = pl.program_id(0)
    na = na_ref[0]
    e = be_ref[i]
    n_grp = MOE_BM // SEG_ALIGN

    def weight_copies(ex, s):
        return [pltpu.make_async_copy(wgu_ref.at[ex], wgu_f.at[s], sem_w.at[s]),
                pltpu.make_async_copy(wdn_ref.at[ex], wdn_f.at[s], sem_w.at[s])]

    def group_copies(blk, inbound, slot=None):
        slot = blk % 2 if slot is None else slot
        cps = []
        for r in range(n_grp):
            v = grp_ref[blk * n_grp + r]
            vm_rows = pl.ds(r * SEG_ALIGN, SEG_ALIGN)
            if inbound:
                row = pl.multiple_of(jnp.where(v >= 0, v, trash_row0 + 2 * MOE_BM), SEG_ALIGN)
                cps.append(pltpu.make_async_copy(xsl_ref.at[pl.ds(row, SEG_ALIGN), :],
                                                 xbuf.at[slot, vm_rows, :], sem_in.at[slot]))
            else:
                spare = trash_row0 + slot * MOE_BM + r * SEG_ALIGN
                row = pl.multiple_of(jnp.where(v >= 0, v, spare), SEG_ALIGN)
                cps.append(pltpu.make_async_copy(ybuf.at[slot, vm_rows, :],
                                                 ysl_ref.at[pl.ds(row, SEG_ALIGN), :], sem_out.at[slot]))
        return cps

    def start_gather(blk):
        for cp in group_copies(blk, True):
            cp.start()

    def start_scatter(blk):
        for cp in group_copies(blk, False):
            cp.start()

    def wait_rows(blk, sem, inbound):
        slot = blk % 2
        if inbound:
            pltpu.make_async_copy(xsl_ref.at[pl.ds(0, MOE_BM), :], xbuf.at[slot], sem.at[slot]).wait()
        else:
            pltpu.make_async_copy(ybuf.at[slot], ysl_ref.at[pl.ds(0, MOE_BM), :], sem.at[slot]).wait()

    @pl.when(i == 0)
    def _():
        start_gather(i)
        for cp in weight_copies(e, 0):
            cp.start()

    @pl.when(i + 1 < na)
    def _():
        start_gather(i + 1)

    @pl.when((i < na) & (first_ref[i] != 0))
    def _():
        s = wslot_ref[i]
        for cp in weight_copies(e, s):
            cp.wait()
        nx = nxt_ref[i]

        @pl.when(nx >= 0)
        def _():
            for cp in weight_copies(nx, 1 - s):
                cp.start()

        for j in range(2 * D_EXPERT // MOE_CH):
            wgu_bf[:, j * MOE_CH:(j + 1) * MOE_CH] = wgu_f[s, :, j * MOE_CH:(j + 1) * MOE_CH].astype(BF16)
        for j in range(D_EXPERT // MOE_CH):
            wdn_bf[j * MOE_CH:(j + 1) * MOE_CH, :] = wdn_f[s, j * MOE_CH:(j + 1) * MOE_CH, :].astype(BF16)

    @pl.when(i < na)
    def _():
        slot = i % 2
        wait_rows(i, sem_in, True)

        @pl.when(i >= 2)
        def _():
            wait_rows(i - 2, sem_out, False)

        half = D_MODEL // 2
        xh, xl = _unpack_halves(xbuf[slot])

        def xdot(c0, c1):
            return (jnp.dot(xh, wgu_bf[0:half, c0:c1], preferred_element_type=F32)
                    + jnp.dot(xl, wgu_bf[half:D_MODEL, c0:c1], preferred_element_type=F32))

        acc = jnp.zeros((MOE_BM, D_MODEL), F32) + bdn_ref[...]
        for j in range(D_EXPERT // MOE_CH):
            lo, hi = j * MOE_CH, (j + 1) * MOE_CH
            gj = xdot(lo, hi) + bgu_ref[:, lo:hi]
            uj = xdot(D_EXPERT + lo, D_EXPERT + hi) + bgu_ref[:, D_EXPERT + lo:D_EXPERT + hi]
            gj = jnp.minimum(gj, SWIGLU_LIMIT)
            uj = jnp.clip(uj, -SWIGLU_LIMIT, SWIGLU_LIMIT)
            act = gj * _sigmoid(SWIGLU_ALPHA * gj) * (uj + 1.0)
            acc = acc + jnp.dot(act.astype(BF16), wdn_bf[lo:hi, :], preferred_element_type=F32)
        ybuf[slot] = _pack_halves(acc)
        start_scatter(i)

        @pl.when(i == na - 1)
        def _():
            @pl.when(i >= 1)
            def _():
                wait_rows(i - 1, sem_out, False)
            wait_rows(i, sem_out, False)


def _moe_experts(plan, xsl, w_gu, b_gu, w_dn, b_dn):
    block_e = plan[0]
    nblk = block_e.shape[0]
    spare_row0 = xsl.shape[0] - MOE_RL
    assert MOE_RL >= 2 * MOE_BM
    wmap = lambda i, be, *_: (be[i], 0, 0)
    anyspec = pl.BlockSpec(memory_space=pl.ANY)
    gs = pltpu.PrefetchScalarGridSpec(
        num_scalar_prefetch=len(plan),
        grid=(nblk,),
        in_specs=[anyspec,
                  anyspec,
                  pl.BlockSpec((None, 1, 2 * D_EXPERT), wmap),
                  anyspec,
                  pl.BlockSpec((None, 1, D_MODEL), wmap)],
        out_specs=anyspec,
        scratch_shapes=[pltpu.VMEM((D_MODEL, 2 * D_EXPERT), BF16), pltpu.VMEM((D_EXPERT, D_MODEL), BF16),
                        pltpu.VMEM((2, MOE_BM, D_MODEL // 2), jnp.uint32),
                        pltpu.VMEM((2, MOE_BM, D_MODEL // 2), jnp.uint32),
                        pltpu.VMEM((2, D_MODEL, 2 * D_EXPERT), F32), pltpu.VMEM((2, D_EXPERT, D_MODEL), F32),
                        pltpu.SemaphoreType.DMA((2,)), pltpu.SemaphoreType.DMA((2,)),
                        pltpu.SemaphoreType.DMA((2,))],
    )
    return pl.pallas_call(
        functools.partial(_moe_kernel, trash_row0=spare_row0),
        grid_spec=gs,
        out_shape=jax.ShapeDtypeStruct(xsl.shape, jnp.uint32),
        input_output_aliases={len(plan): 0},
        compiler_params=_cparams(("arbitrary",)),
        name="moe_experts",
    )(*plan, xsl, w_gu, b_gu.reshape(N_EXPERTS, 1, -1), w_dn, b_dn.reshape(N_EXPERTS, 1, -1))


def _combine_kernel(ysl_ref, info_ref, x1_ref, mod_ref, y_ref, *, tm):
    info = info_ref[...]
    info_t = jnp.transpose(jnp.concatenate([info, jnp.zeros((LANES - info.shape[0], tm), F32)], axis=0))
    ridx = lax.broadcasted_iota(jnp.int32, (tm, MOE_RL), 1).astype(F32)
    pg = jnp.zeros((tm, MOE_RL), F32)
    for k in range(TOP_K):
        pg = jnp.where(ridx == info_t[:, k:k + 1], info_t[:, TOP_K + k:TOP_K + k + 1], pg)
    pgb = pg.astype(BF16)
    yh, yl = _unpack_halves(ysl_ref[...])
    half = D_MODEL // 2
    gt2 = mod_ref[:, 5 * D_MODEL:6 * D_MODEL]
    for c, yy in ((0, yh), (1, yl)):
        moe = jnp.dot(pgb, yy, preferred_element_type=F32)
        y_ref[:, c * half:(c + 1) * half] = (x1_ref[:, c * half:(c + 1) * half]
                                             + gt2[:, c * half:(c + 1) * half] * moe)


def _combine(ysl, info, x1, mod3, tiles_per_mod, tile0=0):
    m = x1.shape[0]
    tm = MOE_TM
    r = mod3.shape[1]
    return pl.pallas_call(
        functools.partial(_combine_kernel, tm=tm),
        grid=(m // tm,),
        in_specs=[pl.BlockSpec((MOE_RL, D_MODEL // 2), lambda i: (tile0 + i, 0)),
                  pl.BlockSpec((16, tm), lambda i: (0, tile0 + i)),
                  pl.BlockSpec((tm, D_MODEL), lambda i: (i, 0)),
                  pl.BlockSpec((None, r, 6 * D_MODEL), lambda i: (i // tiles_per_mod, 0, 0))],
        out_specs=pl.BlockSpec((tm, D_MODEL), lambda i: (i, 0)),
        out_shape=jax.ShapeDtypeStruct((m, D_MODEL), F32),
        compiler_params=_cparams(("parallel",)),
        name="moe_combine",
    )(ysl, info, x1, mod3)


def _moe_plan(cnt):
    cnt = cnt.astype(jnp.int32)
    nt = cnt.shape[0]
    so = jnp.cumsum(cnt, axis=1) - cnt + (jnp.arange(nt) * MOE_RL)[:, None]
    ce = jnp.cumsum(cnt, axis=0)
    cs = ce - cnt
    tot = ce[-1]
    nblk_e = (tot + MOE_BM - 1) // MOE_BM
    blk_end = jnp.cumsum(nblk_e)
    max_rows = nt * MOE_TM * TOP_K + nt * N_EXPERTS * (SEG_ALIGN - 1)
    n_blocks = -(-max_rows // MOE_BM) + N_EXPERTS
    bidx = jnp.arange(n_blocks)
    block_e = jnp.minimum(jnp.sum(blk_end[None, :] <= bidx[:, None], axis=1), N_EXPERTS - 1).astype(jnp.int32)
    is_e = (jnp.arange(N_EXPERTS)[:, None] == block_e[None, :]).astype(jnp.int32)
    per_block = lambda a: jnp.sum(a[..., :, None] * is_e, axis=-2)
    block_r0 = (bidx - per_block(blk_end - nblk_e)) * MOE_BM
    x = block_r0[:, None] + jnp.arange(MOE_BM // SEG_ALIGN)[None, :] * SEG_ALIGN
    ce_b = per_block(ce)[:, :, None]
    cs_b = per_block(cs)[:, :, None]
    inside = (cs_b <= x[None]) & (x[None] < ce_b)
    grp = x + jnp.sum(jnp.where(inside, per_block(so - cs)[:, :, None], 0), axis=0)
    grp = jnp.where(x < per_block(tot)[:, None], grp, -1)
    n_active = blk_end[-1].reshape(1)
    used = nblk_e > 0
    first = (bidx == per_block(blk_end - nblk_e)) & (bidx < n_active[0])
    wslot = per_block(jnp.cumsum(used) - 1) % 2
    eidx = jnp.arange(N_EXPERTS)
    later_used = used[None, :] & (eidx[None, :] > eidx[:, None])
    nxt_e = jnp.min(jnp.where(later_used, eidx[None, :], N_EXPERTS), axis=1)
    nxt = per_block(jnp.where(nxt_e < N_EXPERTS, nxt_e, -1))
    i32 = lambda a: a.reshape(-1).astype(jnp.int32)
    return block_e, i32(n_active), i32(grp), i32(first), i32(wslot), i32(nxt)


def _rope_tables(pos):
    half = ROT_DIM // 2
    inv = ROPE_THETA ** (-jnp.arange(half, dtype=F32) * (2.0 / ROT_DIM))
    ang = pos.astype(F32)[:, None] * inv[None, :]
    cos, sin = jnp.cos(ang), jnp.sin(ang)
    n = pos.shape[0]
    ones = jnp.ones((n, A_DH - ROT_DIM), F32)
    zeros_h = jnp.zeros((n, half), F32)
    zeros_r = jnp.zeros((n, A_DH - ROT_DIM), F32)
    cos64 = jnp.concatenate([cos, cos, ones], axis=1)
    sprev64 = jnp.concatenate([zeros_h, sin, zeros_r], axis=1)
    snext64 = jnp.concatenate([-sin, zeros_h, zeros_r], axis=1)
    two = lambda a: jnp.concatenate([a, a], axis=1)
    return two(cos64), two(sprev64), two(snext64)


def _prep_weights(w_in, b_in, q_norm_g, k_norm_g, cmp_pe_k, cmp_pe_v, cmp_w_k, cmp_w_v,
                  w_up_m, w_up_a, w_out, w_router, b_router):
    b2 = b_in.reshape(1, N_IN)
    wm = w_in[:, OFF_MQ:OFF_MO].astype(BF16)
    bm = b2[:, OFF_MQ:OFF_MO]
    wq = w_in[:, OFF_AQ:OFF_AKV].astype(BF16)
    bq = b2[:, OFF_AQ:OFF_AKV]
    wkv = w_in[:, OFF_AKV:OFF_AG].astype(BF16)
    bkv = b2[:, OFF_AKV:OFF_AG]
    n_small = 2 * M_HEADS + 3 * A_HEADS
    ws = jnp.concatenate([w_in[:, OFF_MI:OFF_AQ], w_in[:, OFF_AG:OFF_GA],
                          jnp.zeros((D_MODEL, LANES - n_small), F32)], axis=1)
    bs = jnp.concatenate([b2[:, OFF_MI:OFF_AQ], b2[:, OFF_AG:OFF_GA], jnp.zeros((1, LANES - n_small), F32)], axis=1)
    qg = jnp.tile(q_norm_g, A_HEADS).reshape(1, A_WIDTH)
    kg = jnp.stack([jnp.tile(k_norm_g[1], A_KV), jnp.tile(k_norm_g[2], A_KV)], axis=0)
    kg0 = jnp.tile(k_norm_g[0], A_KV).reshape(1, LANES)
    hid = jnp.arange(A_WIDTH) // A_DH
    bd = jnp.where(hid[:, None] == hid[None, :], 1.0 / A_DH, 0.0).astype(BF16)
    inproj_w = (wm, bm, wq, bq, wkv, bkv, ws, bs, qg, kg, bd)

    z = jnp.zeros((CMP_LEN, A_DH, A_DH), F32)
    r0 = jnp.concatenate([cmp_w_k, z, z, z], axis=2)
    r1 = jnp.concatenate([z, cmp_w_k, z, z], axis=2)
    r2 = jnp.concatenate([z, z, cmp_w_v, z], axis=2)
    r3 = jnp.concatenate([z, z, z, cmp_w_v], axis=2)
    wbd = jnp.concatenate([r0, r1, r2, r3], axis=1).astype(BF16)
    pe = jnp.concatenate([cmp_pe_k, cmp_pe_k, cmp_pe_v, cmp_pe_v], axis=1).reshape(CMP_LEN, 1, 2 * LANES)

    wog = jnp.concatenate([w_in[:, OFF_MO:OFF_MI], w_in[:, OFF_GA:N_IN]], axis=1).astype(BF16)
    bog = jnp.concatenate([b2[:, OFF_MO:OFF_MI], b2[:, OFF_GA:N_IN]], axis=1)
    mixout_w = (wog, bog, w_up_m.astype(BF16), w_up_a.astype(BF16), w_out.astype(BF16),
                w_router.T, b_router.reshape(N_EXPERTS, 1))
    return inproj_w, (wbd, pe, kg0), mixout_w


def _pick_tile(m, pref):
    t = pref
    while m % t:
        t //= 2
    return t


def kernel(x_prompt, x_sample, cache_nsa_kv, state_win_kv, state_mlstm_C, state_mlstm_n, state_mlstm_m, page_table, c_prompt, c_sample, w_ada, b_ada, g_mix, g_ffn, w_in, b_in, q_norm_g, k_norm_g, cmp_pe_k, cmp_pe_v, cmp_w_k, cmp_w_v, w_up_m, w_up_a, w_out, w_router, b_router, w_gu, b_gu, w_dn, b_dn):
    depth = w_in.shape[0]
    assert depth == 1
    B, T, D = x_prompt.shape
    DB, TS, _ = x_sample.shape
    n_pages = page_table.shape[1]
    past_len = n_pages * PAGE_SIZE
    wbuf = state_win_kv.shape[2]
    tp = SAMPLE_PAD_T
    assert TS <= tp and wbuf % tp == 0 and T % 128 == 0

    l = 0
    inproj_w, cmp_w, mixout_w = _prep_weights(
        w_in[l], b_in[l], q_norm_g[l], k_norm_g[l], cmp_pe_k[l], cmp_pe_v[l], cmp_w_k[l], cmp_w_v[l],
        w_up_m[l], w_up_a[l], w_out[l], w_router[l], b_router[l])
    wbd, pe, kg0 = cmp_w
    gmix = g_mix[l].reshape(1, D)
    gffn = g_ffn[l].reshape(1, D)

    nc = B + DB
    nc_pad = -(-nc // SUBLANES) * SUBLANES
    c_all = jnp.concatenate([c_prompt, c_sample, jnp.zeros((nc_pad - nc, D), F32)], axis=0)
    mod = _adaln(c_all, w_ada[l], b_ada[l])
    mod_p = mod[:B].reshape(B, 1, 6 * D)
    mod_s = jnp.repeat(mod[B:B + DB], tp, axis=0).reshape(1, DB * tp, 6 * D)

    mp = B * T
    tm = _pick_tile(T, 256)
    xp = x_prompt.reshape(mp, D)
    tabs_p = _rope_tables(jnp.arange(T, dtype=jnp.int32))
    mq, mk, mv, q, qr, rows, win, small, rows_t, win_t = _inproj(xp, mod_p, gmix, tabs_p, inproj_w, tm, T // tm,
                                                                 T // tm, rows_t_batches=B)
    Lp = _pick_tile(T, 128)
    hm, C_p, n_p, m_p = _mlstm(mq, mk, mv, small, B, T, T, Lp)
    o_nsa = _nsa_prompt(q, qr, small, rows, win, wbd, pe, kg0, B, T)
    assert T % MOE_TM == 0
    ms_pad = -(-(DB * tp) // MOE_TM) * MOE_TM
    nt_p = mp // MOE_TM
    nt_all = nt_p + ms_pad // MOE_TM + 1
    x1_p, xsl, info, cnt = _mixout(xp, hm, o_nsa, mod_p, gmix, gffn, mixout_w, T // MOE_TM, nt_all)

    ms = DB * tp
    xs_pad = jnp.concatenate([x_sample, jnp.zeros((DB, tp - TS, D), F32)], axis=1).reshape(ms, D)
    pos_s = past_len + jnp.tile(jnp.arange(tp, dtype=jnp.int32), DB)
    tabs_s = _rope_tables(pos_s)
    mq_s, mk_s, mv_s, q_s, qr_s, rows_s, win_s, small_s = _inproj(xs_pad, mod_s, gmix, tabs_s, inproj_w, ms, 1, 1)
    hm_s, C_s, n_s, m_s = _mlstm(mq_s, mk_s, mv_s, small_s, DB, tp, TS, tp,
                                 state=(state_mlstm_C[l], state_mlstm_n[l], state_mlstm_m[l]),
                                 nseq=_pick_tile(DB, 4))
    cache2 = jnp.transpose(cache_nsa_kv[l], (0, 2, 3, 4, 1)).reshape(cache_nsa_kv.shape[1], 4 * LANES, PAGE_SIZE)
    winbuf = jnp.transpose(state_win_kv[l], (0, 2, 3, 4, 1)).reshape(DB, 2 * LANES, wbuf)
    o_nsa_s, win_out_s = _nsa_sample(page_table, cache2, q_s, qr_s, small_s, rows_s, win_s, winbuf,
                                     wbd, pe, kg0, TS)
    assert ms_pad == MOE_TM
    rpad = lambda a: jnp.concatenate([a, jnp.zeros((ms_pad - ms, a.shape[1]), a.dtype)], axis=0) if ms_pad > ms else a
    mod_sp = rpad(mod_s[0])[None]
    x1_s, xsl, info, cnt = _mixout(rpad(xs_pad), rpad(hm_s), rpad(o_nsa_s), mod_sp, gmix, gffn, mixout_w,
                                   1, nt_all, tile0=nt_p, shared=(xsl, info, cnt),
                                   t_mod=tp, t_valid=TS, m_valid=ms)

    ysl = _moe_experts(_moe_plan(cnt[:, :, 0]), xsl, w_gu[l], b_gu[l], w_dn[l], b_dn[l])
    y_p = _combine(ysl, info, x1_p, mod_p, T // MOE_TM).reshape(B, T, D)
    y_s_all = _combine(ysl, info, x1_s, mod_sp, 1, tile0=nt_p)
    valid = lambda a: a.reshape(DB, tp, -1)[:, :TS].reshape(DB * TS, -1)
    y_s = valid(y_s_all[:ms]).reshape(DB, TS, D)

    kv_p = jnp.transpose(rows_t.reshape(B, 4, A_KV, A_DH, T), (0, 4, 1, 2, 3))[None]
    kv_s = valid(rows_s).reshape(1, DB, TS, 4, A_KV, A_DH)
    wp = min(WINDOW, T)
    win_p = jnp.transpose(win_t[:, :, T - wp:].reshape(B, 2, A_KV, A_DH, wp), (0, 4, 1, 2, 3))[None]
    win_s_out = jnp.transpose(win_out_s.reshape(DB, 2, A_KV, A_DH, wbuf), (0, 4, 1, 2, 3))[None]
    return (y_p, y_s, kv_p, kv_s, win_p, win_s_out,
            C_p[None], n_p[None], m_p[None], C_s[None], n_s[None], m_s[None])
```

```python
import functools

import jax
import jax.numpy as jnp
from jax import lax
from jax.experimental import pallas as pl
from jax.experimental.pallas import tpu as pltpu

F32 = jnp.float32
BF16 = jnp.bfloat16

D_MODEL = 1024
M_HEADS = 4
M_DH = 128
M_WIDTH = M_HEADS * M_DH
A_HEADS = 8
A_KV = 2
A_HPG = A_HEADS // A_KV
A_DH = 64
A_WIDTH = A_HEADS * A_DH
CMP_STRIDE = 16
CMP_LEN = 32
SEL_LEN = 64
N_SEL = 16
WINDOW = 512
PAGE_SIZE = 128
ROPE_THETA = 500000.0
ROT_DIM = A_DH // 4
ATT_SCALE = A_DH ** -0.5
N_EXPERTS = 32
TOP_K = 4
D_EXPERT = D_MODEL
SWIGLU_LIMIT = 7.0
SWIGLU_ALPHA = 1.702
EPS = 1e-6

OFF_MQ, OFF_MK, OFF_MV, OFF_MO = 0, M_WIDTH, 2 * M_WIDTH, 3 * M_WIDTH
OFF_MI = 4 * M_WIDTH
OFF_MF = OFF_MI + M_HEADS
OFF_AQ = OFF_MF + M_HEADS
OFF_AKV = OFF_AQ + A_WIDTH
OFF_AG = OFF_AKV + 6 * A_KV * A_DH
OFF_GA = OFF_AG + 3 * A_HEADS
OFF_GB = OFF_GA + D_MODEL
N_IN = OFF_GB + D_MODEL

LANES = 128
SUBLANES = 8
VMEM_LIMIT = 56 * 1024 * 1024

NEG_BIG = -1e30
M_INIT = -1e29
LOG2E = 1.4426950408889634
SAMPLE_PAD_T = 8


def _cparams(sem):
    return pltpu.CompilerParams(dimension_semantics=sem, vmem_limit_bytes=VMEM_LIMIT)


def _bdot(a, b):
    return jnp.dot(a.astype(BF16), b.astype(BF16), preferred_element_type=F32)


def _bdot_t(a, b):
    return lax.dot_general(a.astype(BF16), b.astype(BF16), (((1,), (1,)), ((), ())),
                           preferred_element_type=F32)


def _split(a):
    hi = a.astype(BF16)
    lo = (a - hi.astype(F32)).astype(BF16)
    return hi, lo


def _dot3(a, b):
    ah, al = _split(a)
    bh, bl = _split(b)
    return (jnp.dot(ah, bh, preferred_element_type=F32) + jnp.dot(al, bh, preferred_element_type=F32)
            + jnp.dot(ah, bl, preferred_element_type=F32))


def _dot2_exact_rhs(a, b_bf16):
    ah, al = _split(a)
    return jnp.dot(ah, b_bf16, preferred_element_type=F32) + jnp.dot(al, b_bf16, preferred_element_type=F32)


def _sigmoid(x):
    return 0.5 * jnp.tanh(0.5 * x) + 0.5


def _rmsnorm_rows(x, g):
    return x * lax.rsqrt(jnp.mean(x * x, axis=-1, keepdims=True) + EPS) * g


def _adaln_kernel(c_ref, w_ref, b_ref, o_ref):
    c = c_ref[...]
    s = c * _sigmoid(c)
    o_ref[...] = _dot3(s, w_ref[...]) + b_ref[...]


def _adaln(c, w, b):
    mc, d = c.shape
    n = w.shape[1]
    tn = 1024
    return pl.pallas_call(
        _adaln_kernel,
        grid=(n // tn,),
        in_specs=[pl.BlockSpec((mc, d), lambda j: (0, 0)),
                  pl.BlockSpec((d, tn), lambda j: (0, j)),
                  pl.BlockSpec((1, tn), lambda j: (0, j))],
        out_specs=pl.BlockSpec((mc, tn), lambda j: (0, j)),
        out_shape=jax.ShapeDtypeStruct((mc, n), F32),
        compiler_params=_cparams(("parallel",)),
        name="adaln",
    )(c, w, b.reshape(1, n))


def _head_norm(z, bd, gain):
    ms = _dot2_exact_rhs(z * z, bd)
    return z * lax.rsqrt(ms + EPS) * gain


def _rope(z, cos, s_prev, s_next):
    w = z.shape[1]
    rep = w // LANES
    if rep > 1:
        cos = jnp.concatenate([cos] * rep, axis=1)
        s_prev = jnp.concatenate([s_prev] * rep, axis=1)
        s_next = jnp.concatenate([s_next] * rep, axis=1)
    z_prev = pltpu.roll(z, ROT_DIM // 2, 1)
    z_next = pltpu.roll(z, w - ROT_DIM // 2, 1)
    return z * cos + z_prev * s_prev + z_next * s_next


def _inproj_kernel(x_ref, mod_ref, gmix_ref, cos_ref, sp_ref, sn_ref,
                   wm_ref, bm_ref, wq_ref, bq_ref, wkv_ref, bkv_ref, ws_ref, bs_ref,
                   qg_ref, kg_ref, bd_ref,
                   mq_ref, mk_ref, mv_ref, q_ref, qr_ref, rows_ref, win_ref, small_ref,
                   rows_t_ref=None, win_t_ref=None):
    x = x_ref[...]
    sh1 = mod_ref[:, 0:D_MODEL]
    sc1 = mod_ref[:, D_MODEL:2 * D_MODEL]
    h = _rmsnorm_rows(x, gmix_ref[...]) * (1.0 + sc1) + sh1
    hb = h.astype(BF16)

    mq_ref[...] = jnp.dot(hb, wm_ref[:, 0:M_WIDTH], preferred_element_type=F32) + bm_ref[:, 0:M_WIDTH]
    mk = jnp.dot(hb, wm_ref[:, M_WIDTH:2 * M_WIDTH], preferred_element_type=F32) + bm_ref[:, M_WIDTH:2 * M_WIDTH]
    mk_ref[...] = mk * (M_DH ** -0.5)
    mv_ref[...] = (jnp.dot(hb, wm_ref[:, 2 * M_WIDTH:3 * M_WIDTH], preferred_element_type=F32)
                   + bm_ref[:, 2 * M_WIDTH:3 * M_WIDTH])

    cos, sp, sn = cos_ref[...], sp_ref[...], sn_ref[...]
    zq = jnp.dot(hb, wq_ref[...], preferred_element_type=F32) + bq_ref[...]
    qn = _head_norm(zq, bd_ref[...], qg_ref[...])
    q_ref[...] = qn
    qr_ref[...] = _rope(qn, cos, sp, sn)

    zkv = jnp.dot(hb, wkv_ref[...], preferred_element_type=F32) + bkv_ref[...]
    bd2 = bd_ref[0:LANES, 0:LANES]
    ksel = _head_norm(zkv[:, 2 * LANES:3 * LANES], bd2, kg_ref[0:1, :])
    rows = jnp.concatenate([zkv[:, 0:2 * LANES], _rope(ksel, cos, sp, sn), zkv[:, 3 * LANES:4 * LANES]], axis=1)
    rows_ref[...] = rows
    if rows_t_ref is not None:
        rows_t_ref[...] = jnp.transpose(rows)
    kwin = _head_norm(zkv[:, 4 * LANES:5 * LANES], bd2, kg_ref[1:2, :])
    win = jnp.concatenate([_rope(kwin, cos, sp, sn), zkv[:, 5 * LANES:6 * LANES]], axis=1)
    win_ref[...] = win
    if win_t_ref is not None:
        win_t_ref[...] = jnp.transpose(win)

    small_ref[...] = _dot3(h, ws_ref[...]) + bs_ref[...]


def _inproj(x2, mod3, gmix, tabs, wts, tm, tiles_per_mod, pos_tiles, rows_t_batches=None):
    m = x2.shape[0]
    cos_t, sp_t, sn_t = tabs
    (wm, bm, wq, bq, wkv, bkv, ws, bs, qg, kg, bd) = wts
    r = mod3.shape[1]
    row = lambda i: (i, 0)
    const = lambda i: (0, 0)
    tab = lambda i: (i % pos_tiles, 0)
    in_specs = [
        pl.BlockSpec((tm, D_MODEL), row),
        pl.BlockSpec((None, r, 6 * D_MODEL), lambda i: (i // tiles_per_mod, 0, 0)),
        pl.BlockSpec((1, D_MODEL), const),
        pl.BlockSpec((tm, LANES), tab), pl.BlockSpec((tm, LANES), tab), pl.BlockSpec((tm, LANES), tab),
        pl.BlockSpec(wm.shape, const), pl.BlockSpec(bm.shape, const),
        pl.BlockSpec(wq.shape, const), pl.BlockSpec(bq.shape, const),
        pl.BlockSpec(wkv.shape, const), pl.BlockSpec(bkv.shape, const),
        pl.BlockSpec(ws.shape, const), pl.BlockSpec(bs.shape, const),
        pl.BlockSpec(qg.shape, const), pl.BlockSpec(kg.shape, const), pl.BlockSpec(bd.shape, const),
    ]
    widths = (M_WIDTH, M_WIDTH, M_WIDTH, A_WIDTH, A_WIDTH, 4 * LANES, 2 * LANES, LANES)
    out_specs = [pl.BlockSpec((tm, w), row) for w in widths]
    out_shape = [jax.ShapeDtypeStruct((m, w), F32) for w in widths]
    if rows_t_batches is not None:
        for w in (4 * LANES, 2 * LANES):
            out_specs.append(pl.BlockSpec((None, w, tm), lambda i: (i // tiles_per_mod, 0, i % tiles_per_mod)))
            out_shape.append(jax.ShapeDtypeStruct((rows_t_batches, w, m // rows_t_batches), F32))
    return pl.pallas_call(
        _inproj_kernel,
        grid=(m // tm,),
        in_specs=in_specs,
        out_specs=out_specs,
        out_shape=out_shape,
        compiler_params=_cparams(("parallel",)),
        name="inproj",
    )(x2, mod3, gmix, cos_t, sp_t, sn_t, wm, bm, wq, bq, wkv, bkv, ws, bs, qg, kg, bd)


def _log_sigmoid(x):
    return jnp.minimum(x, 0.0) - jnp.log(1.0 + jnp.exp(-jnp.abs(x)))


def _mlstm_kernel(*refs, L, t_valid, has_state, nseq):
    if has_state:
        q_ref, k_ref, v_ref, s_ref, c0_ref, n0_ref, m0_ref, h_ref, c_ref, n_ref, m_ref = refs
    else:
        q_ref, k_ref, v_ref, s_ref, h_ref, c_ref, n_ref, m_ref = refs
    c = pl.program_id(1)

    @pl.when(c == 0)
    def _():
        if has_state:
            c_ref[...] = c0_ref[...]
            n_ref[...] = n0_ref[...]
            m_ref[...] = m0_ref[...]
        else:
            c_ref[...] = jnp.zeros(c_ref.shape, F32)
            n_ref[...] = jnp.zeros(n_ref.shape, F32)
            m_ref[...] = jnp.zeros(m_ref.shape, F32)

    row = lax.broadcasted_iota(jnp.int32, (L, L), 0)
    col = lax.broadcasted_iota(jnp.int32, (L, L), 1)
    causal = col <= row
    eye = col == row
    tok_col = c * L + lax.broadcasted_iota(jnp.int32, (L, 1), 0)
    valid_col = tok_col < t_valid
    for sq, hd in [(a, b) for a in range(nseq) for b in range(M_HEADS)]:
        lo, hi = hd * M_DH, (hd + 1) * M_DH
        rs = slice(sq * L, (sq + 1) * L)
        q = q_ref[rs, lo:hi]
        k = k_ref[rs, lo:hi]
        v = v_ref[rs, lo:hi]
        i_col = s_ref[rs, hd:hd + 1]
        lf_col = _log_sigmoid(s_ref[rs, M_HEADS + hd:M_HEADS + hd + 1])
        lf_col = jnp.where(valid_col, lf_col, 0.0)
        i_col = jnp.where(valid_col, i_col, -jnp.inf)
        if L == LANES:
            i_col = jnp.broadcast_to(i_col, (L, L))
            lf_c = jnp.broadcast_to(lf_col, (L, L))
            p0 = lf_c.astype(BF16)
            r1 = lf_c - p0.astype(F32)
            p1 = r1.astype(BF16)
            p2 = (r1 - p1.astype(F32)).astype(BF16)
            tril = jnp.where(causal, 1.0, 0.0).astype(BF16)
            b_col = (jnp.dot(tril, p0, preferred_element_type=F32) + jnp.dot(tril, p1, preferred_element_type=F32)
                     + jnp.dot(tril, p2, preferred_element_type=F32))
            i_row = jnp.transpose(i_col)[0:1, :]
            b_row = jnp.transpose(b_col)[0:1, :]
        else:
            i_row = jnp.sum(jnp.where(eye, i_col, 0.0), axis=0, keepdims=True)
            lf_row = jnp.sum(jnp.where(eye, lf_col, 0.0), axis=0, keepdims=True)
            b_col = jnp.sum(jnp.where(causal, lf_row, 0.0), axis=1, keepdims=True)
            b_row = jnp.sum(jnp.where(row <= col, lf_col, 0.0), axis=0, keepdims=True)
        m_prev = m_ref[sq, :, hd:hd + 1]
        dmat = jnp.where(causal, b_col - b_row + i_row, -jnp.inf)
        inter = b_col + m_prev
        m_row = jnp.maximum(jnp.max(dmat, axis=1, keepdims=True), inter)
        w = jnp.exp(dmat - m_row)
        w_inter = jnp.exp(inter - m_row)
        s = _bdot_t(q, k) * w
        cm = c_ref[sq, hd]
        nv = n_ref[sq, hd]
        num = _bdot(s, v) + w_inter * _bdot_t(q, cm)
        den = jnp.sum(s, axis=1, keepdims=True) + w_inter * jnp.sum(q * nv, axis=1, keepdims=True)
        h_ref[rs, lo:hi] = num / jnp.maximum(jnp.abs(den), jnp.exp(-m_row))
        b_last = b_col[L - 1:L, 0:1]
        dec_col = b_last - b_col + i_col
        dec_row = b_last - b_row + i_row
        m_new = jnp.maximum(b_last + m_prev, jnp.max(dec_row, axis=1, keepdims=True))
        ws_col = jnp.exp(dec_col - m_new)
        wc = jnp.exp(b_last + m_prev - m_new)
        vw = (v * ws_col).astype(BF16)
        upd = lax.dot_general(vw, k.astype(BF16), (((0,), (0,)), ((), ())), preferred_element_type=F32)
        c_ref[sq, hd] = wc * cm + upd
        n_ref[sq, hd] = wc * nv + jnp.sum(k * ws_col, axis=0, keepdims=True)
        m_ref[sq, :, hd:hd + 1] = m_new


def _mlstm(mq, mk, mv, small, nb, t_pad, t_valid, L, state=None, nseq=1):
    nc = t_pad // L
    assert nseq == 1 or (nc == 1 and nb % nseq == 0)
    has_state = state is not None
    rows = nseq * L
    blk = lambda b, c: (b * nc + c, 0)
    st4 = lambda b, c: (b, 0, 0, 0)
    st3 = lambda b, c: (b, 0, 0)
    in_specs = [pl.BlockSpec((rows, M_WIDTH), blk)] * 3 + [pl.BlockSpec((rows, LANES), blk)]
    args = [mq, mk, mv, small]
    if has_state:
        c0, n0, m0 = state
        in_specs += [pl.BlockSpec((nseq, M_HEADS, M_DH, M_DH), st4),
                     pl.BlockSpec((nseq, M_HEADS, 1, M_DH), st4),
                     pl.BlockSpec((nseq, 1, M_HEADS), st3)]
        args += [c0, n0.reshape(nb, M_HEADS, 1, M_DH), m0.reshape(nb, 1, M_HEADS)]
    out_specs = [pl.BlockSpec((rows, M_WIDTH), blk),
                 pl.BlockSpec((nseq, M_HEADS, M_DH, M_DH), st4),
                 pl.BlockSpec((nseq, M_HEADS, 1, M_DH), st4),
                 pl.BlockSpec((nseq, 1, M_HEADS), st3)]
    out_shape = [jax.ShapeDtypeStruct((nb * t_pad, M_WIDTH), F32),
                 jax.ShapeDtypeStruct((nb, M_HEADS, M_DH, M_DH), F32),
                 jax.ShapeDtypeStruct((nb, M_HEADS, 1, M_DH), F32),
                 jax.ShapeDtypeStruct((nb, 1, M_HEADS), F32)]
    h, cs, ns, ms = pl.pallas_call(
        functools.partial(_mlstm_kernel, L=L, t_valid=t_valid, has_state=has_state, nseq=nseq),
        grid=(nb // nseq, nc),
        in_specs=in_specs,
        out_specs=out_specs,
        out_shape=out_shape,
        compiler_params=_cparams(("parallel", "arbitrary")),
        name="mlstm",
    )(*args)
    return h, cs, ns.reshape(nb, M_HEADS, M_DH), ms.reshape(nb, M_HEADS)


def _stack_heads(qt, g):
    t = qt.shape[0]
    z = jnp.zeros((t, A_DH), F32)
    parts = []
    for hh in range(A_HPG):
        hd = g * A_HPG + hh
        qh = qt[:, hd * A_DH:(hd + 1) * A_DH] * (ATT_SCALE * LOG2E)
        parts.append(jnp.concatenate([qh, z], axis=1) if g == 0 else jnp.concatenate([z, qh], axis=1))
    return jnp.concatenate(parts, axis=0).astype(BF16)


def _gate_cols(small, g, br):
    cols = []
    for hh in range(A_HPG):
        c0 = 2 * M_HEADS + (g * A_HPG + hh) * 3 + br
        cols.append(_sigmoid(small[:, c0:c0 + 1]))
    return jnp.concatenate(cols, axis=0)


def _compress(k_ref, v_ref, nseg, wbd_ref, pe_ref, kg0):
    acc_lo = jnp.zeros((nseg, 2 * LANES), F32)
    acc_hi = jnp.zeros((nseg, 2 * LANES), F32)
    for l in range(CMP_STRIDE):
        xl = jnp.concatenate([k_ref[pl.ds(l, nseg, stride=CMP_STRIDE), :],
                              v_ref[pl.ds(l, nseg, stride=CMP_STRIDE), :]], axis=1)
        acc_lo = acc_lo + _bdot(xl + pe_ref[l], wbd_ref[l])
        acc_hi = acc_hi + _bdot(xl + pe_ref[CMP_STRIDE + l], wbd_ref[CMP_STRIDE + l])
    return _compress_finish(acc_lo, acc_hi, nseg, kg0)


def _compress_grouped(x_ref, nseg, wbd_ref, pe_ref, kg0):
    kdim = CMP_STRIDE * 2 * LANES
    xcat = jnp.concatenate([x_ref[l].astype(BF16) for l in range(CMP_STRIDE)], axis=1)
    pecat = jnp.concatenate([jnp.broadcast_to(pe_ref[l], (SUBLANES, 2 * LANES)) for l in range(CMP_LEN)],
                            axis=1).astype(BF16)
    w_lo = wbd_ref[0:CMP_STRIDE].reshape(kdim, 2 * LANES)
    w_hi = wbd_ref[CMP_STRIDE:CMP_LEN].reshape(kdim, 2 * LANES)
    acc_lo = jnp.dot(xcat, w_lo, preferred_element_type=F32)
    acc_hi = jnp.dot(xcat, w_hi, preferred_element_type=F32)
    pe_lo = jnp.dot(pecat[:, 0:kdim], w_lo, preferred_element_type=F32)
    pe_hi = jnp.dot(pecat[:, kdim:2 * kdim], w_hi, preferred_element_type=F32)
    return _compress_finish(acc_lo + pe_lo[0:1, :], acc_hi + pe_hi[0:1, :], nseg, kg0)


def _compress_finish(acc_lo, acc_hi, nseg, kg0):
    kv = acc_lo + pltpu.roll(acc_hi, nseg - 1, 0)
    kc = kv[:, 0:LANES]
    vc = kv[:, LANES:2 * LANES]
    lane = lax.broadcasted_iota(jnp.int32, (nseg, LANES), 1)
    sq = kc * kc
    ms0 = jnp.sum(jnp.where(lane < A_DH, sq, 0.0), axis=1, keepdims=True) * (1.0 / A_DH)
    ms1 = jnp.sum(jnp.where(lane >= A_DH, sq, 0.0), axis=1, keepdims=True) * (1.0 / A_DH)
    ms = jnp.where(lane < A_DH, ms0, ms1)
    kc = kc * lax.rsqrt(ms + EPS) * kg0
    return kc, vc


def _cmp_branch(qn_g, kc_b, vc_b, tpos_rows, nseg, n_tok):
    s = _bdot_t(qn_g, kc_b)
    nidx = lax.broadcasted_iota(jnp.int32, (1, nseg), 1)
    vis = (nidx * CMP_STRIDE + (CMP_LEN - 1)) <= tpos_rows
    sm = jnp.where(vis, s, NEG_BIG)
    mx = jnp.max(sm, axis=1, keepdims=True)
    e = jnp.where(vis, jnp.exp2(sm - mx), 0.0)
    d = jnp.sum(e, axis=1, keepdims=True)
    p = e / jnp.where(d > 0, d, 1.0)
    o = _bdot(p, vc_b)
    imp = p[0:n_tok]
    for hh in range(1, A_HPG):
        imp = imp + p[hh * n_tok:(hh + 1) * n_tok]
    return o, imp


def _masked_attn_direct(q_g, k_parts, v_parts, allowed_parts, feature_major):
    ss = [jnp.where(al, _bdot(q_g, kk) if fm else _bdot_t(q_g, kk), NEG_BIG)
          for kk, al, fm in zip(k_parts, allowed_parts, feature_major)]
    mx = ss[0].max(axis=1, keepdims=True)
    for s in ss[1:]:
        mx = jnp.maximum(mx, s.max(axis=1, keepdims=True))
    num = None
    den = None
    for s, al, vv, fm in zip(ss, allowed_parts, v_parts, feature_major):
        e = jnp.where(al, jnp.exp2(s - mx), 0.0)
        dd = jnp.sum(e, axis=1, keepdims=True)
        oo = _bdot_t(e, vv) if fm else _bdot(e, vv)
        num = oo if num is None else num + oo
        den = dd if den is None else den + dd
    return num / jnp.where(den > 0, den, 1.0)


def _assemble_heads(o_groups, n_tok):
    pieces = []
    for g in range(A_KV):
        for hh in range(A_HPG):
            pieces.append(o_groups[g][hh * n_tok:(hh + 1) * n_tok, g * A_DH:(g + 1) * A_DH])
    return jnp.concatenate(pieces, axis=1)


def _lane_rep(a, rep):
    return a if rep == 1 else jnp.concatenate([a] * rep, axis=1)


def _nsa_prompt_kernel(q_ref, qr_ref, small_ref, rows_ref, win_ref, wbd_ref, pe_ref, kg0_ref,
                       pool_ref, o_ref,
                       kraw_sc, vraw_sc, kc_sc, vct_sc, sel_sc, m_sc, acc_sc, s_sc, *, T, tq, kc_len):
    qi = pl.program_id(1)
    nseg = T // CMP_STRIDE
    nsb = T // SEL_LEN
    bpc = kc_len // SEL_LEN

    @pl.when(qi == 0)
    def _():
        kraw_sc[...] = rows_ref[:, 0:LANES]
        vraw_sc[...] = rows_ref[:, LANES:2 * LANES]
        kc, vc = _compress(kraw_sc, vraw_sc, nseg, wbd_ref, pe_ref, kg0_ref[...])
        kc_sc[...] = kc
        vct_sc[...] = jnp.transpose(vc)

    t0 = qi * tq
    tpos = t0 + lax.broadcasted_iota(jnp.int32, (1, tq), 1)
    tpos4 = _lane_rep(tpos, A_HPG)
    q = q_ref[...]
    qr = qr_ref[...]
    small_t = jnp.transpose(small_ref[...])
    kc_b = kc_sc[...].astype(BF16)
    vct_b = vct_sc[...].astype(BF16)
    bidx = lax.broadcasted_iota(jnp.int32, (nsb, tq), 0)
    cur = tpos // SEL_LEN
    vis = (lax.broadcasted_iota(jnp.int32, (nseg, 1), 0) * CMP_STRIDE + (CMP_LEN - 1)) <= tpos4
    qr_gs = [_stack_heads(qr, g) for g in range(A_KV)]
    o_cmps = []
    for g in range(A_KV):
        sm = jnp.where(vis, _bdot_t(kc_b, _stack_heads(q, g)), NEG_BIG)
        mx = jnp.max(sm, axis=0, keepdims=True)
        e = jnp.where(vis, jnp.exp2(sm - mx), 0.0)
        d = jnp.sum(e, axis=0, keepdims=True)
        p = e / jnp.where(d > 0, d, 1.0)
        o_cmps.append(jnp.dot(vct_b, p.astype(BF16), preferred_element_type=F32))
        imp = p[:, 0:tq]
        for hh in range(1, A_HPG):
            imp = imp + p[:, hh * tq:(hh + 1) * tq]
        ih, il = _split(imp)
        imp_t = (jnp.dot(pool_ref[...], ih, preferred_element_type=F32)
                 + jnp.dot(pool_ref[...], il, preferred_element_type=F32))[0:nsb]
        val = jnp.where(bidx < cur, imp_t, -1.0)
        rank = jnp.zeros((nsb, tq), F32)
        for bp in range(nsb):
            vb = val[bp:bp + 1, :]
            rank = rank + jnp.where(vb > val, 1.0, jnp.where((vb == val) & (bidx > bp), 1.0, 0.0))
        sel_sc[g] = jnp.where(((rank < (N_SEL - 1)) & (bidx < cur)) | (bidx == cur), 1.0, 0.0)

    m_sc[...] = jnp.full(m_sc.shape, M_INIT, F32)
    acc_sc[...] = jnp.zeros(acc_sc.shape, F32)

    def with_ones_row(vt_, g):
        vb = vt_.astype(BF16)
        r0, pad = (1 - g) * A_DH, 2 * SUBLANES
        ones = jnp.ones((pad, vb.shape[1]), BF16)
        return jnp.concatenate(([vb[0:r0]] if r0 else []) + [ones, vb[r0 + pad:]], axis=0)

    def sel_body(c, carry):
        k0 = pl.multiple_of(c * kc_len, kc_len)
        kb = rows_ref[pl.ds(k0, kc_len), 2 * LANES:3 * LANES].astype(BF16)
        vt = jnp.transpose(rows_ref[pl.ds(k0, kc_len), 3 * LANES:4 * LANES])
        causal = (k0 + lax.broadcasted_iota(jnp.int32, (kc_len, 1), 0)) <= tpos
        for g in range(A_KV):
            s_sc[g, 0:kc_len, :] = _bdot_t(kb, qr_gs[g])
        for g in range(A_KV):
            selc = sel_sc[g, pl.ds(pl.multiple_of(c * bpc, bpc), bpc), :]
            selx = jnp.concatenate([jnp.broadcast_to(selc[j:j + 1, :], (SEL_LEN, tq)) for j in range(bpc)], axis=0)
            bias = jnp.where(causal & (selx > 0.5), 0.0, NEG_BIG)
            sm = s_sc[g, 0:kc_len, :] + _lane_rep(bias, A_HPG)
            m_prev = m_sc[g]
            m_new = jnp.maximum(m_prev, jnp.max(sm, axis=0, keepdims=True))
            alpha = jnp.exp2(m_prev - m_new)
            p = jnp.exp2(sm - m_new)
            acc_sc[g] = alpha * acc_sc[g] + jnp.dot(with_ones_row(vt, g), p.astype(BF16),
                                                    preferred_element_type=F32)
            m_sc[g] = m_new
        return carry

    lax.fori_loop(0, (t0 + tq + kc_len - 1) // kc_len, sel_body, 0)

    wk = min(WINDOW + tq, T)
    w0 = pl.multiple_of(jnp.clip(t0 + tq - wk, 0, T - wk), tq)
    kw = win_ref[pl.ds(w0, wk), 0:LANES].astype(BF16)
    vwt = jnp.transpose(win_ref[pl.ds(w0, wk), LANES:2 * LANES])
    wdiff = tpos - (w0 + lax.broadcasted_iota(jnp.int32, (wk, 1), 0))
    wbias = _lane_rep(jnp.where((wdiff >= 0) & (wdiff < WINDOW), 0.0, NEG_BIG), A_HPG)

    def gate_row(g, br):
        cols = [2 * M_HEADS + (g * A_HPG + hh) * 3 + br for hh in range(A_HPG)]
        return jnp.concatenate([_sigmoid(small_t[c0:c0 + 1, :]) for c0 in cols], axis=1)

    for g in range(A_KV):
        s_sc[g, 0:wk, :] = _bdot_t(kw, qr_gs[g])
    o_ts = []
    for g in range(A_KV):
        den = (1 - g) * A_DH
        acc = acc_sc[g]
        l = acc[den:den + 1, :]
        o_sel = acc / jnp.where(l > 0, l, 1.0)
        sw = s_sc[g, 0:wk, :] + wbias
        pw = jnp.exp2(sw - jnp.max(sw, axis=0, keepdims=True))
        ow = jnp.dot(with_ones_row(vwt, g), pw.astype(BF16), preferred_element_type=F32)
        o_win = ow / ow[den:den + 1, :]
        o_ts.append(gate_row(g, 0) * o_cmps[g] + gate_row(g, 1) * o_sel + gate_row(g, 2) * o_win)
    for j in range(A_HEADS // 2):
        g, h0 = j // (A_HPG // 2), 2 * (j % (A_HPG // 2))
        og = o_ts[g][g * A_DH:(g + 1) * A_DH, :]
        pair = jnp.concatenate([og[:, h0 * tq:(h0 + 1) * tq], og[:, (h0 + 1) * tq:(h0 + 2) * tq]], axis=0)
        o_ref[:, j * LANES:(j + 1) * LANES] = jnp.transpose(pair)


def _nsa_prompt(q, qr, small, rows, win, wbd, pe, kg0, nb, T):
    tq = 128
    kc_len = _pick_tile(T, 512)
    nq = T // tq
    nseg = T // CMP_STRIDE
    nsb = T // SEL_LEN
    nsb_p = -(-nsb // SUBLANES) * SUBLANES
    pool = (jnp.arange(nsb_p)[:, None] == jnp.arange(nseg)[None, :] // (SEL_LEN // CMP_STRIDE)).astype(BF16)
    tile = lambda b, i: (b * nq + i, 0)
    per_b = lambda b, i: (b, 0)
    c2 = lambda b, i: (0, 0)
    c3 = lambda b, i: (0, 0, 0)
    c4 = A_HPG * tq
    return pl.pallas_call(
        functools.partial(_nsa_prompt_kernel, T=T, tq=tq, kc_len=kc_len),
        grid=(nb, nq),
        in_specs=[pl.BlockSpec((tq, A_WIDTH), tile), pl.BlockSpec((tq, A_WIDTH), tile),
                  pl.BlockSpec((tq, LANES), tile),
                  pl.BlockSpec((T, 4 * LANES), per_b), pl.BlockSpec((T, 2 * LANES), per_b),
                  pl.BlockSpec(wbd.shape, c3), pl.BlockSpec(pe.shape, c3), pl.BlockSpec(kg0.shape, c2),
                  pl.BlockSpec(pool.shape, c2)],
        out_specs=pl.BlockSpec((tq, A_WIDTH), tile),
        out_shape=jax.ShapeDtypeStruct((nb * T, A_WIDTH), F32),
        scratch_shapes=[pltpu.VMEM((T, LANES), F32), pltpu.VMEM((T, LANES), F32),
                        pltpu.VMEM((nseg, LANES), F32), pltpu.VMEM((LANES, nseg), F32),
                        pltpu.VMEM((A_KV, nsb, tq), F32),
                        pltpu.VMEM((A_KV, 1, c4), F32),
                        pltpu.VMEM((A_KV, LANES, c4), F32),
                        pltpu.VMEM((A_KV, max(kc_len, min(WINDOW + tq, T)), c4), F32)],
        compiler_params=_cparams(("parallel", "arbitrary")),
        name="nsa_prompt",
    )(q, qr, small, rows, win, wbd, pe, kg0, pool)


def _nsa_sample_kernel(pt_ref, cache_ref, q_ref, qr_ref, small_ref, rows_ref, winnew_ref, winbuf_ref,
                       wbd_ref, pe_ref, kg0_ref, pool_ref, expand_ref,
                       o_ref, winout_ref,
                       page_buf, xperm_sc, sems, *, n_pages, past_len, t_valid):
    b = pl.program_id(0)
    nb = pl.num_programs(0)
    tp = SAMPLE_PAD_T
    nseg = past_len // CMP_STRIDE
    nsb = past_len // SEL_LEN
    wbuf = winbuf_ref.shape[1]

    def page_copy(bb, p):
        page = pt_ref[bb * n_pages + p]
        dst_lanes = pl.ds(pl.multiple_of(p * PAGE_SIZE, PAGE_SIZE), PAGE_SIZE)
        return pltpu.make_async_copy(cache_ref.at[page], page_buf.at[bb % 2, :, dst_lanes], sems.at[bb % 2])

    def start_all(bb):
        def body(p, c):
            page_copy(bb, p).start()
            return c
        lax.fori_loop(0, n_pages, body, 0)

    def wait_all(bb):
        def body(p, c):
            page_copy(bb, p).wait()
            return c
        lax.fori_loop(0, n_pages, body, 0)

    @pl.when(b == 0)
    def _():
        start_all(b)

    @pl.when(b + 1 < nb)
    def _():
        start_all(b + 1)

    wait_all(b)
    cmp_buf = page_buf.at[b % 2, pl.ds(0, 2 * LANES), :]
    sel_buf = page_buf.at[b % 2, pl.ds(2 * LANES, 2 * LANES), :]

    seg_pp = PAGE_SIZE // CMP_STRIDE
    pr = lax.broadcasted_iota(jnp.int32, (PAGE_SIZE, PAGE_SIZE), 0)
    pc = lax.broadcasted_iota(jnp.int32, (PAGE_SIZE, PAGE_SIZE), 1)
    perm = jnp.where(pc == CMP_STRIDE * (pr % seg_pp) + pr // seg_pp, 1.0, 0.0).astype(BF16)
    for p in range(n_pages):
        xp = _bdot_t(perm, cmp_buf[:, p * PAGE_SIZE:(p + 1) * PAGE_SIZE])
        for l in range(CMP_STRIDE):
            xperm_sc[l, p * seg_pp:(p + 1) * seg_pp, :] = xp[l * seg_pp:(l + 1) * seg_pp, :]
    kc, vc = _compress_grouped(xperm_sc, nseg, wbd_ref, pe_ref, kg0_ref[...])
    kc_b = kc.astype(BF16)
    vc_b = vc.astype(BF16)
    q = q_ref[...]
    qr = qr_ref[...]
    small = small_ref[...]
    tpos_col = past_len + lax.broadcasted_iota(jnp.int32, (tp, 1), 0)
    tpos_rows = jnp.concatenate([tpos_col] * A_HPG, axis=0)
    bp_idx = lax.broadcasted_iota(jnp.int32, (nsb, nsb), 0)
    b_idx = lax.broadcasted_iota(jnp.int32, (nsb, nsb), 1)
    o_cmps = []
    sels = []
    for g in range(A_KV):
        qn_g = _stack_heads(q, g)
        o_cmp, imp = _cmp_branch(qn_g, kc_b, vc_b, tpos_rows, nseg, tp)
        o_cmps.append(o_cmp)
        imp_sel = _dot2_exact_rhs(imp, pool_ref[...])
        imp_pad = jnp.concatenate([imp_sel, jnp.zeros((nsb - tp, nsb), F32)], axis=0)
        imp_t = jnp.transpose(imp_pad)
        rows_sel = []
        for t in range(tp):
            if t < t_valid:
                row_t = imp_sel[t:t + 1, :]
                col_t = imp_t[:, t:t + 1]
                ahead = jnp.where(col_t > row_t, 1.0, jnp.where((col_t == row_t) & (bp_idx < b_idx), 1.0, 0.0))
                rank = jnp.sum(ahead, axis=0, keepdims=True)
                rows_sel.append(jnp.where(rank < (N_SEL - 1), 1.0, 0.0))
            else:
                rows_sel.append(jnp.zeros((1, nsb), F32))
        sels.append(jnp.concatenate(rows_sel, axis=0))

    new_idx = lax.broadcasted_iota(jnp.int32, (tp, tp), 1)
    tok_idx = lax.broadcasted_iota(jnp.int32, (tp, tp), 0)
    new_ok = jnp.concatenate([jnp.where(new_idx <= tok_idx, 1.0, 0.0)] * A_HEADS, axis=0) > 0.5
    wpos = past_len - wbuf + lax.broadcasted_iota(jnp.int32, (1, wbuf), 1)
    wdiff = tpos_col - wpos
    win_ok = jnp.concatenate([jnp.where((wdiff >= 0) & (wdiff < WINDOW), 1.0, 0.0)] * A_HEADS, axis=0) > 0.5
    k_past = sel_buf[0:LANES, :].astype(BF16)
    v_past = sel_buf[LANES:2 * LANES, :].astype(BF16)
    k_new = rows_ref[:, 2 * LANES:3 * LANES]
    v_new = rows_ref[:, 3 * LANES:4 * LANES]
    kw_past = winbuf_ref[0:LANES, :]
    vw_past = winbuf_ref[LANES:2 * LANES, :]
    kw_new = winnew_ref[:, 0:LANES]
    vw_new = winnew_ref[:, LANES:2 * LANES]
    r4 = A_HPG * tp
    qr_all = jnp.concatenate([_stack_heads(qr, g) for g in range(A_KV)], axis=0)
    mk = jnp.dot(jnp.concatenate(sels, axis=0).astype(BF16), expand_ref[...],
                 preferred_element_type=F32)
    past_ok = jnp.concatenate([mk[g * tp:(g + 1) * tp] for g in range(A_KV) for _ in range(A_HPG)], axis=0) > 0.5
    o_sel = _masked_attn_direct(qr_all, [k_past, k_new], [v_past, v_new], [past_ok, new_ok], [True, False])
    o_win = _masked_attn_direct(qr_all, [kw_past, kw_new], [vw_past, vw_new], [win_ok, new_ok], [True, False])
    o_groups = []
    for g in range(A_KV):
        rs = slice(g * r4, (g + 1) * r4)
        o_groups.append(_gate_cols(small, g, 0) * o_cmps[g] + _gate_cols(small, g, 1) * o_sel[rs]
                        + _gate_cols(small, g, 2) * o_win[rs])
    o_ref[...] = _assemble_heads(o_groups, tp)

    rolled = pltpu.roll(winbuf_ref[...], wbuf - t_valid, 1)
    new_t = jnp.transpose(jnp.concatenate([winnew_ref[...], jnp.zeros((LANES - tp, 2 * LANES), F32)], axis=0))
    new_t = pltpu.roll(new_t, LANES - t_valid, 1)
    lane = lax.broadcasted_iota(jnp.int32, (2 * LANES, LANES), 1)
    winout_ref[:, 0:wbuf - LANES] = rolled[:, 0:wbuf - LANES]
    winout_ref[:, wbuf - LANES:wbuf] = jnp.where(lane < LANES - t_valid, rolled[:, wbuf - LANES:wbuf], new_t)


def _nsa_sample(page_table, cache, q, qr, small, rows, winnew, winbuf, wbd, pe, kg0, t_valid):
    nb, n_pages = page_table.shape
    past_len = n_pages * PAGE_SIZE
    nseg = past_len // CMP_STRIDE
    nsb = past_len // SEL_LEN
    tp = SAMPLE_PAD_T
    wbuf = winbuf.shape[2]
    pool = (jnp.arange(nseg)[:, None] // (SEL_LEN // CMP_STRIDE) == jnp.arange(nsb)[None, :]).astype(BF16)
    expand = (jnp.arange(nsb)[:, None] == jnp.arange(past_len)[None, :] // SEL_LEN).astype(BF16)
    tile = lambda b, pt: (b, 0)
    c2 = lambda b, pt: (0, 0)
    c3 = lambda b, pt: (0, 0, 0)
    gs = pltpu.PrefetchScalarGridSpec(
        num_scalar_prefetch=1,
        grid=(nb,),
        in_specs=[pl.BlockSpec(memory_space=pl.ANY),
                  pl.BlockSpec((tp, A_WIDTH), tile), pl.BlockSpec((tp, A_WIDTH), tile),
                  pl.BlockSpec((tp, LANES), tile), pl.BlockSpec((tp, 4 * LANES), tile),
                  pl.BlockSpec((tp, 2 * LANES), tile),
                  pl.BlockSpec((None, 2 * LANES, wbuf), lambda b, pt: (b, 0, 0)),
                  pl.BlockSpec(wbd.shape, c3), pl.BlockSpec(pe.shape, c3), pl.BlockSpec(kg0.shape, c2),
                  pl.BlockSpec(pool.shape, c2), pl.BlockSpec(expand.shape, c2)],
        out_specs=[pl.BlockSpec((tp, A_WIDTH), tile),
                   pl.BlockSpec((None, 2 * LANES, wbuf), lambda b, pt: (b, 0, 0))],
        scratch_shapes=[pltpu.VMEM((2, 4 * LANES, past_len), F32),
                        pltpu.VMEM((CMP_STRIDE, past_len // CMP_STRIDE, 2 * LANES), F32),
                        pltpu.SemaphoreType.DMA((2,))],
    )
    return pl.pallas_call(
        functools.partial(_nsa_sample_kernel, n_pages=n_pages, past_len=past_len, t_valid=t_valid),
        grid_spec=gs,
        out_shape=[jax.ShapeDtypeStruct((nb * tp, A_WIDTH), F32),
                   jax.ShapeDtypeStruct((nb, 2 * LANES, wbuf), F32)],
        compiler_params=_cparams(("arbitrary",)),
        name="nsa_sample",
    )(page_table.reshape(-1), cache, q, qr, small, rows, winnew, winbuf, wbd, pe, kg0, pool, expand)


MOE_TM = 256
SEG_ALIGN = 8
MOE_RL = -(-(MOE_TM * TOP_K + N_EXPERTS * (SEG_ALIGN - 1)) // LANES) * LANES


def _pack_halves(x, bf16_exact=False):
    w = x.shape[1] // 2
    bits = lax.bitcast_convert_type(x if bf16_exact else x.astype(BF16).astype(F32), jnp.uint32)
    return bits[:, :w] | (bits[:, w:] >> 16)


def _unpack_halves(u):
    hi = lax.bitcast_convert_type(u & jnp.uint32(0xFFFF0000), F32).astype(BF16)
    lo = lax.bitcast_convert_type(u << 16, F32).astype(BF16)
    return hi, lo


def _route_and_sort(h2, wrt_ref, brt_ref, xsl_ref, info_ref, cnt_ref, tm, t_mod, t_valid, m_valid):
    ne = N_EXPERTS
    h2b = h2.astype(BF16)
    h2l = (h2 - h2b.astype(F32)).astype(BF16)
    wh, wl = _split(wrt_ref[...])
    lt = _bdot_t(wh, h2b) + _bdot_t(wl, h2b) + _bdot_t(wh, h2l) + brt_ref[...]
    eidx = lax.broadcasted_iota(jnp.int32, (ne, tm), 0)
    rank = jnp.zeros((ne, tm), F32)
    for ep in range(ne):
        v = lt[ep:ep + 1, :]
        rank = rank + jnp.where(v > lt, 1.0, jnp.where((v == lt) & (eidx > ep), 1.0, 0.0))
    sel = rank < TOP_K
    if t_mod is not None:
        tok = pl.program_id(0) * tm + lax.broadcasted_iota(jnp.int32, (1, tm), 1)
        sel = sel & ((tok % t_mod) < t_valid) & (tok < m_valid)
    mx = jnp.max(jnp.where(sel, lt, NEG_BIG), axis=0, keepdims=True)
    ex = jnp.where(sel, jnp.exp(lt - mx), 0.0)
    den = jnp.sum(ex, axis=0, keepdims=True)
    gate = ex / jnp.where(den > 0, den, 1.0)
    self_ = jnp.where(sel, 1.0, 0.0)
    selb = self_.astype(BF16)
    er = lax.broadcasted_iota(jnp.int32, (ne, ne), 0)
    ec = lax.broadcasted_iota(jnp.int32, (ne, ne), 1)
    c = jnp.dot(jnp.where(ec <= er, 1.0, 0.0).astype(BF16), selb, preferred_element_type=F32)
    tr = lax.broadcasted_iota(jnp.int32, (tm, tm), 0)
    tc = lax.broadcasted_iota(jnp.int32, (tm, tm), 1)
    rk = jnp.dot(selb, jnp.where(tr < tc, 1.0, 0.0).astype(BF16), preferred_element_type=F32)
    cnt = jnp.sum(self_, axis=1, keepdims=True)
    cnt_al = jnp.floor((cnt + (SEG_ALIGN - 1)) * (1.0 / SEG_ALIGN)) * SEG_ALIGN
    cnt_b = jnp.broadcast_to(cnt_al, (ne, LANES))
    cnt_ref[...] = cnt_b
    off = jnp.dot(jnp.where(ec < er, 1.0, 0.0).astype(BF16), cnt_b.astype(BF16), preferred_element_type=F32)
    rowidx = off[:, 0:1] + rk
    rows_k, gates_k, exps_k = [], [], []
    for k in range(1, TOP_K + 1):
        mk = sel & (c == k)
        has = jnp.sum(jnp.where(mk, 1.0, 0.0), axis=0, keepdims=True)
        rows_k.append(jnp.sum(jnp.where(mk, rowidx, 0.0), axis=0, keepdims=True) + has - 1.0)
        gates_k.append(jnp.sum(jnp.where(mk, gate, 0.0), axis=0, keepdims=True))
        exps_k.append(jnp.sum(jnp.where(mk, eidx.astype(F32), 0.0), axis=0, keepdims=True))
    info_ref[...] = jnp.concatenate(rows_k + gates_k + exps_k + [jnp.zeros((4, tm), F32)], axis=0)
    ridx = lax.broadcasted_iota(jnp.int32, (MOE_RL, tm), 0).astype(F32)
    perm = jnp.zeros((MOE_RL, tm), F32)
    for k in range(TOP_K):
        perm = jnp.where(ridx == rows_k[k], 1.0, perm)
    xs = jnp.dot(perm.astype(BF16), h2b, preferred_element_type=F32)
    xsl_ref[...] = _pack_halves(xs, bf16_exact=True)


def _mixout_kernel(x_ref, hm_ref, on_ref, mod_ref, gmix_ref, gffn_ref,
                   wog_ref, bog_ref, wum_ref, wua_ref, wout_ref, wrt_ref, brt_ref,
                   x1_ref, xsl_ref, info_ref, cnt_ref, *, tm, t_mod, t_valid, m_valid, n_real):
    if n_real is not None:
        @pl.when(pl.program_id(0) >= n_real)
        def _():
            xsl_ref[...] = jnp.zeros(xsl_ref.shape, jnp.uint32)
            info_ref[...] = jnp.zeros(info_ref.shape, F32)
            cnt_ref[...] = jnp.zeros(cnt_ref.shape, F32)

        @pl.when(pl.program_id(0) < n_real)
        def _():
            _mixout_body(x_ref, hm_ref, on_ref, mod_ref, gmix_ref, gffn_ref, wog_ref, bog_ref, wum_ref,
                         wua_ref, wout_ref, wrt_ref, brt_ref, x1_ref, xsl_ref, info_ref, cnt_ref,
                         tm, t_mod, t_valid, m_valid)
    else:
        _mixout_body(x_ref, hm_ref, on_ref, mod_ref, gmix_ref, gffn_ref, wog_ref, bog_ref, wum_ref,
                     wua_ref, wout_ref, wrt_ref, brt_ref, x1_ref, xsl_ref, info_ref, cnt_ref,
                     tm, t_mod, t_valid, m_valid)


def _mixout_body(x_ref, hm_ref, on_ref, mod_ref, gmix_ref, gffn_ref,
                 wog_ref, bog_ref, wum_ref, wua_ref, wout_ref, wrt_ref, brt_ref,
                 x1_ref, xsl_ref, info_ref, cnt_ref, tm, t_mod, t_valid, m_valid):
    d = D_MODEL
    x = x_ref[...]
    sh1, sc1, gt1 = mod_ref[:, 0:d], mod_ref[:, d:2 * d], mod_ref[:, 2 * d:3 * d]
    sh2, sc2 = mod_ref[:, 3 * d:4 * d], mod_ref[:, 4 * d:5 * d]
    h = _rmsnorm_rows(x, gmix_ref[...]) * (1.0 + sc1) + sh1
    hb = h.astype(BF16)
    mo = jnp.dot(hb, wog_ref[:, 0:M_WIDTH], preferred_element_type=F32) + bog_ref[:, 0:M_WIDTH]
    ym = _bdot(_sigmoid(mo) * hm_ref[...], wum_ref[...])
    ya = _bdot(on_ref[...], wua_ref[...])
    ga = jnp.dot(hb, wog_ref[:, M_WIDTH:M_WIDTH + d], preferred_element_type=F32) + bog_ref[:, M_WIDTH:M_WIDTH + d]
    u = _sigmoid(ga) * ym
    gb = (jnp.dot(hb, wog_ref[:, M_WIDTH + d:M_WIDTH + 2 * d], preferred_element_type=F32)
          + bog_ref[:, M_WIDTH + d:M_WIDTH + 2 * d])
    u = u + _sigmoid(gb) * ya
    x1 = x + gt1 * _bdot(u, wout_ref[...])
    x1_ref[...] = x1
    h2 = _rmsnorm_rows(x1, gffn_ref[...]) * (1.0 + sc2) + sh2
    _route_and_sort(h2, wrt_ref, brt_ref, xsl_ref, info_ref, cnt_ref, tm, t_mod, t_valid, m_valid)


def _mixout_with_shared(*refs, n_shared, **kw):
    n_in = 13
    _mixout_kernel(*refs[:n_in], *refs[n_in + n_shared:], **kw)


def _mixout(x2, hm, on, mod3, gmix, gffn, wts, tiles_per_mod, nt_total, tile0=0, shared=None,
            t_mod=None, t_valid=None, m_valid=None):
    m = x2.shape[0]
    tm = MOE_TM
    nt = m // tm
    (wog, bog, wum, wua, wout, wr, br) = wts
    r = mod3.shape[1]
    n_extra = nt_total - tile0 - nt if shared is None else 0
    row = lambda i: (jnp.minimum(i, nt - 1), 0)
    const = lambda i: (0, 0)
    in_specs = [pl.BlockSpec((tm, D_MODEL), row), pl.BlockSpec((tm, M_WIDTH), row),
                pl.BlockSpec((tm, A_WIDTH), row),
                pl.BlockSpec((None, r, 6 * D_MODEL), lambda i: (jnp.minimum(i, nt - 1) // tiles_per_mod, 0, 0)),
                pl.BlockSpec((1, D_MODEL), const), pl.BlockSpec((1, D_MODEL), const),
                pl.BlockSpec(wog.shape, const), pl.BlockSpec(bog.shape, const),
                pl.BlockSpec(wum.shape, const), pl.BlockSpec(wua.shape, const),
                pl.BlockSpec(wout.shape, const), pl.BlockSpec(wr.shape, const),
                pl.BlockSpec(br.shape, const)]
    args = [x2, hm, on, mod3, gmix, gffn, wog, bog, wum, wua, wout, wr, br]
    kw = dict(tm=tm, t_mod=t_mod, t_valid=t_valid, m_valid=m_valid, n_real=nt if n_extra else None)
    body = functools.partial(_mixout_kernel, **kw)
    aliases = {}
    if shared is not None:
        in_specs += [pl.BlockSpec(memory_space=pl.ANY)] * len(shared)
        aliases = {len(args) + j: 1 + j for j in range(len(shared))}
        args += list(shared)
        body = functools.partial(_mixout_with_shared, n_shared=len(shared), **kw)
    return pl.pallas_call(
        body,
        grid=(nt + n_extra,),
        in_specs=in_specs,
        out_specs=[pl.BlockSpec((tm, D_MODEL), row),
                   pl.BlockSpec((MOE_RL, D_MODEL // 2), lambda i: (tile0 + i, 0)),
                   pl.BlockSpec((16, tm), lambda i: (0, tile0 + i)),
                   pl.BlockSpec((None, N_EXPERTS, LANES), lambda i: (tile0 + i, 0, 0))],
        out_shape=[jax.ShapeDtypeStruct((m, D_MODEL), F32),
                   jax.ShapeDtypeStruct((nt_total * MOE_RL, D_MODEL // 2), jnp.uint32),
                   jax.ShapeDtypeStruct((16, nt_total * tm), F32),
                   jax.ShapeDtypeStruct((nt_total, N_EXPERTS, LANES), F32)],
        input_output_aliases=aliases,
        compiler_params=_cparams(("arbitrary" if n_extra else "parallel",)),
        name="mixout",
    )(*args)


MOE_BM = 256
MOE_CH = 1024


def _moe_kernel(be_ref, na_ref, grp_ref, first_ref, wslot_ref, nxt_ref,
                xsl_ref, wgu_ref, bgu_ref, wdn_ref, bdn_ref, ysl_ref,
                wgu_bf, wdn_bf, xbuf, ybuf, wgu_f, wdn_f, sem_in, sem_out, sem_w, *, trash_row0):
    i = pl.program_id(0)
    na = na_ref[0]
    e = be_ref[i]
    n_grp = MOE_BM // SEG_ALIGN

    def weight_copies(ex, s):
        return [pltpu.make_async_copy(wgu_ref.at[ex], wgu_f.at[s], sem_w.at[s]),
                pltpu.make_async_copy(wdn_ref.at[ex], wdn_f.at[s], sem_w.at[s])]

    def group_copies(blk, inbound, slot=None):
        slot = blk % 2 if slot is None else slot
        cps = []
        for r in range(n_grp):
            v = grp_ref[blk * n_grp + r]
            vm_rows = pl.ds(r * SEG_ALIGN, SEG_ALIGN)
            if inbound:
                row = pl.multiple_of(jnp.where(v >= 0, v, trash_row0 + 2 * MOE_BM), SEG_ALIGN)
                cps.append(pltpu.make_async_copy(xsl_ref.at[pl.ds(row, SEG_ALIGN), :],
                                                 xbuf.at[slot, vm_rows, :], sem_in.at[slot]))
            else:
                spare = trash_row0 + slot * MOE_BM + r * SEG_ALIGN
                row = pl.multiple_of(jnp.where(v >= 0, v, spare), SEG_ALIGN)
                cps.append(pltpu.make_async_copy(ybuf.at[slot, vm_rows, :],
                                                 ysl_ref.at[pl.ds(row, SEG_ALIGN), :], sem_out.at[slot]))
        return cps

    def start_gather(blk):
        for cp in group_copies(blk, True):
            cp.start()

    def start_scatter(blk):
        for cp in group_copies(blk, False):
            cp.start()

    def wait_rows(blk, sem, inbound):
        slot = blk % 2
        if inbound:
            pltpu.make_async_copy(xsl_ref.at[pl.ds(0, MOE_BM), :], xbuf.at[slot], sem.at[slot]).wait()
        else:
            pltpu.make_async_copy(ybuf.at[slot], ysl_ref.at[pl.ds(0, MOE_BM), :], sem.at[slot]).wait()

    @pl.when(i == 0)
    def _():
        start_gather(i)
        for cp in weight_copies(e, 0):
            cp.start()

    @pl.when(i + 1 < na)
    def _():
        start_gather(i + 1)

    @pl.when((i < na) & (first_ref[i] != 0))
    def _():
        s = wslot_ref[i]
        for cp in weight_copies(e, s):
            cp.wait()
        nx = nxt_ref[i]

        @pl.when(nx >= 0)
        def _():
            for cp in weight_copies(nx, 1 - s):
                cp.start()

        for j in range(2 * D_EXPERT // MOE_CH):
            wgu_bf[:, j * MOE_CH:(j + 1) * MOE_CH] = wgu_f[s, :, j * MOE_CH:(j + 1) * MOE_CH].astype(BF16)
        for j in range(D_EXPERT // MOE_CH):
            wdn_bf[j * MOE_CH:(j + 1) * MOE_CH, :] = wdn_f[s, j * MOE_CH:(j + 1) * MOE_CH, :].astype(BF16)

    @pl.when(i < na)
    def _():
        slot = i % 2
        wait_rows(i, sem_in, True)

        @pl.when(i >= 2)
        def _():
            wait_rows(i - 2, sem_out, False)

        half = D_MODEL // 2
        xh, xl = _unpack_halves(xbuf[slot])

        def xdot(c0, c1):
            return (jnp.dot(xh, wgu_bf[0:half, c0:c1], preferred_element_type=F32)
                    + jnp.dot(xl, wgu_bf[half:D_MODEL, c0:c1], preferred_element_type=F32))

        acc = jnp.zeros((MOE_BM, D_MODEL), F32) + bdn_ref[...]
        for j in range(D_EXPERT // MOE_CH):
            lo, hi = j * MOE_CH, (j + 1) * MOE_CH
            gj = xdot(lo, hi) + bgu_ref[:, lo:hi]
            uj = xdot(D_EXPERT + lo, D_EXPERT + hi) + bgu_ref[:, D_EXPERT + lo:D_EXPERT + hi]
            gj = jnp.minimum(gj, SWIGLU_LIMIT)
            uj = jnp.clip(uj, -SWIGLU_LIMIT, SWIGLU_LIMIT)
            act = gj * _sigmoid(SWIGLU_ALPHA * gj) * (uj + 1.0)
            acc = acc + jnp.dot(act.astype(BF16), wdn_bf[lo:hi, :], preferred_element_type=F32)
        ybuf[slot] = _pack_halves(acc)
        start_scatter(i)

        @pl.when(i == na - 1)
        def _():
            @pl.when(i >= 1)
            def _():
                wait_rows(i - 1, sem_out, False)
            wait_rows(i, sem_out, False)


def _moe_experts(plan, xsl, w_gu, b_gu, w_dn, b_dn):
    block_e = plan[0]
    nblk = block_e.shape[0]
    spare_row0 = xsl.shape[0] - MOE_RL
    assert MOE_RL >= 2 * MOE_BM
    wmap = lambda i, be, *_: (be[i], 0, 0)
    anyspec = pl.BlockSpec(memory_space=pl.ANY)
    gs = pltpu.PrefetchScalarGridSpec(
        num_scalar_prefetch=len(plan),
        grid=(nblk,),
        in_specs=[anyspec,
                  anyspec,
                  pl.BlockSpec((None, 1, 2 * D_EXPERT), wmap),
                  anyspec,
                  pl.BlockSpec((None, 1, D_MODEL), wmap)],
        out_specs=anyspec,
        scratch_shapes=[pltpu.VMEM((D_MODEL, 2 * D_EXPERT), BF16), pltpu.VMEM((D_EXPERT, D_MODEL), BF16),
                        pltpu.VMEM((2, MOE_BM, D_MODEL // 2), jnp.uint32),
                        pltpu.VMEM((2, MOE_BM, D_MODEL // 2), jnp.uint32),
                        pltpu.VMEM((2, D_MODEL, 2 * D_EXPERT), F32), pltpu.VMEM((2, D_EXPERT, D_MODEL), F32),
                        pltpu.SemaphoreType.DMA((2,)), pltpu.SemaphoreType.DMA((2,)),
                        pltpu.SemaphoreType.DMA((2,))],
    )
    return pl.pallas_call(
        functools.partial(_moe_kernel, trash_row0=spare_row0),
        grid_spec=gs,
        out_shape=jax.ShapeDtypeStruct(xsl.shape, jnp.uint32),
        input_output_aliases={len(plan): 0},
        compiler_params=_cparams(("arbitrary",)),
        name="moe_experts",
    )(*plan, xsl, w_gu, b_gu.reshape(N_EXPERTS, 1, -1), w_dn, b_dn.reshape(N_EXPERTS, 1, -1))


def _combine_kernel(ysl_ref, info_ref, x1_ref, mod_ref, y_ref, *, tm):
    info = info_ref[...]
    info_t = jnp.transpose(jnp.concatenate([info, jnp.zeros((LANES - info.shape[0], tm), F32)], axis=0))
    ridx = lax.broadcasted_iota(jnp.int32, (tm, MOE_RL), 1).astype(F32)
    pg = jnp.zeros((tm, MOE_RL), F32)
    for k in range(TOP_K):
        pg = jnp.where(ridx == info_t[:, k:k + 1], info_t[:, TOP_K + k:TOP_K + k + 1], pg)
    pgb = pg.astype(BF16)
    yh, yl = _unpack_halves(ysl_ref[...])
    half = D_MODEL // 2
    gt2 = mod_ref[:, 5 * D_MODEL:6 * D_MODEL]
    for c, yy in ((0, yh), (1, yl)):
        moe = jnp.dot(pgb, yy, preferred_element_type=F32)
        y_ref[:, c * half:(c + 1) * half] = (x1_ref[:, c * half:(c + 1) * half]
                                             + gt2[:, c * half:(c + 1) * half] * moe)


def _combine(ysl, info, x1, mod3, tiles_per_mod, tile0=0):
    m = x1.shape[0]
    tm = MOE_TM
    r = mod3.shape[1]
    return pl.pallas_call(
        functools.partial(_combine_kernel, tm=tm),
        grid=(m // tm,),
        in_specs=[pl.BlockSpec((MOE_RL, D_MODEL // 2), lambda i: (tile0 + i, 0)),
                  pl.BlockSpec((16, tm), lambda i: (0, tile0 + i)),
                  pl.BlockSpec((tm, D_MODEL), lambda i: (i, 0)),
                  pl.BlockSpec((None, r, 6 * D_MODEL), lambda i: (i // tiles_per_mod, 0, 0))],
        out_specs=pl.BlockSpec((tm, D_MODEL), lambda i: (i, 0)),
        out_shape=jax.ShapeDtypeStruct((m, D_MODEL), F32),
        compiler_params=_cparams(("parallel",)),
        name="moe_combine",
    )(ysl, info, x1, mod3)


def _moe_plan(cnt):
    cnt = cnt.astype(jnp.int32)
    nt = cnt.shape[0]
    so = jnp.cumsum(cnt, axis=1) - cnt + (jnp.arange(nt) * MOE_RL)[:, None]
    ce = jnp.cumsum(cnt, axis=0)
    cs = ce - cnt
    tot = ce[-1]
    nblk_e = (tot + MOE_BM - 1) // MOE_BM
    blk_end = jnp.cumsum(nblk_e)
    max_rows = nt * MOE_TM * TOP_K + nt * N_EXPERTS * (SEG_ALIGN - 1)
    n_blocks = -(-max_rows // MOE_BM) + N_EXPERTS
    bidx = jnp.arange(n_blocks)
    block_e = jnp.minimum(jnp.sum(blk_end[None, :] <= bidx[:, None], axis=1), N_EXPERTS - 1).astype(jnp.int32)
    is_e = (jnp.arange(N_EXPERTS)[:, None] == block_e[None, :]).astype(jnp.int32)
    per_block = lambda a: jnp.sum(a[..., :, None] * is_e, axis=-2)
    block_r0 = (bidx - per_block(blk_end - nblk_e)) * MOE_BM
    x = block_r0[:, None] + jnp.arange(MOE_BM // SEG_ALIGN)[None, :] * SEG_ALIGN
    ce_b = per_block(ce)[:, :, None]
    cs_b = per_block(cs)[:, :, None]
    inside = (cs_b <= x[None]) & (x[None] < ce_b)
    grp = x + jnp.sum(jnp.where(inside, per_block(so - cs)[:, :, None], 0), axis=0)
    grp = jnp.where(x < per_block(tot)[:, None], grp, -1)
    n_active = blk_end[-1].reshape(1)
    used = nblk_e > 0
    first = (bidx == per_block(blk_end - nblk_e)) & (bidx < n_active[0])
    wslot = per_block(jnp.cumsum(used) - 1) % 2
    eidx = jnp.arange(N_EXPERTS)
    later_used = used[None, :] & (eidx[None, :] > eidx[:, None])
    nxt_e = jnp.min(jnp.where(later_used, eidx[None, :], N_EXPERTS), axis=1)
    nxt = per_block(jnp.where(nxt_e < N_EXPERTS, nxt_e, -1))
    i32 = lambda a: a.reshape(-1).astype(jnp.int32)
    return block_e, i32(n_active), i32(grp), i32(first), i32(wslot), i32(nxt)


def _rope_tables(pos):
    half = ROT_DIM // 2
    inv = ROPE_THETA ** (-jnp.arange(half, dtype=F32) * (2.0 / ROT_DIM))
    ang = pos.astype(F32)[:, None] * inv[None, :]
    cos, sin = jnp.cos(ang), jnp.sin(ang)
    n = pos.shape[0]
    ones = jnp.ones((n, A_DH - ROT_DIM), F32)
    zeros_h = jnp.zeros((n, half), F32)
    zeros_r = jnp.zeros((n, A_DH - ROT_DIM), F32)
    cos64 = jnp.concatenate([cos, cos, ones], axis=1)
    sprev64 = jnp.concatenate([zeros_h, sin, zeros_r], axis=1)
    snext64 = jnp.concatenate([-sin, zeros_h, zeros_r], axis=1)
    two = lambda a: jnp.concatenate([a, a], axis=1)
    return two(cos64), two(sprev64), two(snext64)


def _prep_weights(w_in, b_in, q_norm_g, k_norm_g, cmp_pe_k, cmp_pe_v, cmp_w_k, cmp_w_v,
                  w_up_m, w_up_a, w_out, w_router, b_router):
    b2 = b_in.reshape(1, N_IN)
    wm = w_in[:, OFF_MQ:OFF_MO].astype(BF16)
    bm = b2[:, OFF_MQ:OFF_MO]
    wq = w_in[:, OFF_AQ:OFF_AKV].astype(BF16)
    bq = b2[:, OFF_AQ:OFF_AKV]
    wkv = w_in[:, OFF_AKV:OFF_AG].astype(BF16)
    bkv = b2[:, OFF_AKV:OFF_AG]
    n_small = 2 * M_HEADS + 3 * A_HEADS
    ws = jnp.concatenate([w_in[:, OFF_MI:OFF_AQ], w_in[:, OFF_AG:OFF_GA],
                          jnp.zeros((D_MODEL, LANES - n_small), F32)], axis=1)
    bs = jnp.concatenate([b2[:, OFF_MI:OFF_AQ], b2[:, OFF_AG:OFF_GA], jnp.zeros((1, LANES - n_small), F32)], axis=1)
    qg = jnp.tile(q_norm_g, A_HEADS).reshape(1, A_WIDTH)
    kg = jnp.stack([jnp.tile(k_norm_g[1], A_KV), jnp.tile(k_norm_g[2], A_KV)], axis=0)
    kg0 = jnp.tile(k_norm_g[0], A_KV).reshape(1, LANES)
    hid = jnp.arange(A_WIDTH) // A_DH
    bd = jnp.where(hid[:, None] == hid[None, :], 1.0 / A_DH, 0.0).astype(BF16)
    inproj_w = (wm, bm, wq, bq, wkv, bkv, ws, bs, qg, kg, bd)

    z = jnp.zeros((CMP_LEN, A_DH, A_DH), F32)
    r0 = jnp.concatenate([cmp_w_k, z, z, z], axis=2)
    r1 = jnp.concatenate([z, cmp_w_k, z, z], axis=2)
    r2 = jnp.concatenate([z, z, cmp_w_v, z], axis=2)
    r3 = jnp.concatenate([z, z, z, cmp_w_v], axis=2)
    wbd = jnp.concatenate([r0, r1, r2, r3], axis=1).astype(BF16)
    pe = jnp.concatenate([cmp_pe_k, cmp_pe_k, cmp_pe_v, cmp_pe_v], axis=1).reshape(CMP_LEN, 1, 2 * LANES)

    wog = jnp.concatenate([w_in[:, OFF_MO:OFF_MI], w_in[:, OFF_GA:N_IN]], axis=1).astype(BF16)
    bog = jnp.concatenate([b2[:, OFF_MO:OFF_MI], b2[:, OFF_GA:N_IN]], axis=1)
    mixout_w = (wog, bog, w_up_m.astype(BF16), w_up_a.astype(BF16), w_out.astype(BF16),
                w_router.T, b_router.reshape(N_EXPERTS, 1))
    return inproj_w, (wbd, pe, kg0), mixout_w


def _pick_tile(m, pref):
    t = pref
    while m % t:
        t //= 2
    return t


def kernel(x_prompt, x_sample, cache_nsa_kv, state_win_kv, state_mlstm_C, state_mlstm_n, state_mlstm_m, page_table, c_prompt, c_sample, w_ada, b_ada, g_mix, g_ffn, w_in, b_in, q_norm_g, k_norm_g, cmp_pe_k, cmp_pe_v, cmp_w_k, cmp_w_v, w_up_m, w_up_a, w_out, w_router, b_router, w_gu, b_gu, w_dn, b_dn):
    depth = w_in.shape[0]
    assert depth == 1
    B, T, D = x_prompt.shape
    DB, TS, _ = x_sample.shape
    n_pages = page_table.shape[1]
    past_len = n_pages * PAGE_SIZE
    wbuf = state_win_kv.shape[2]
    tp = SAMPLE_PAD_T
    assert TS <= tp and wbuf % tp == 0 and T % 128 == 0

    l = 0
    inproj_w, cmp_w, mixout_w = _prep_weights(
        w_in[l], b_in[l], q_norm_g[l], k_norm_g[l], cmp_pe_k[l], cmp_pe_v[l], cmp_w_k[l], cmp_w_v[l],
        w_up_m[l], w_up_a[l], w_out[l], w_router[l], b_router[l])
    wbd, pe, kg0 = cmp_w
    gmix = g_mix[l].reshape(1, D)
    gffn = g_ffn[l].reshape(1, D)

    nc = B + DB
    nc_pad = -(-nc // SUBLANES) * SUBLANES
    c_all = jnp.concatenate([c_prompt, c_sample, jnp.zeros((nc_pad - nc, D), F32)], axis=0)
    mod = _adaln(c_all, w_ada[l], b_ada[l])
    mod_p = mod[:B].reshape(B, 1, 6 * D)
    mod_s = jnp.repeat(mod[B:B + DB], tp, axis=0).reshape(1, DB * tp, 6 * D)

    mp = B * T
    tm = _pick_tile(T, 256)
    xp = x_prompt.reshape(mp, D)
    tabs_p = _rope_tables(jnp.arange(T, dtype=jnp.int32))
    mq, mk, mv, q, qr, rows, win, small, rows_t, win_t = _inproj(xp, mod_p, gmix, tabs_p, inproj_w, tm, T // tm,
                                                                 T // tm, rows_t_batches=B)
    Lp = _pick_tile(T, 128)
    hm, C_p, n_p, m_p = _mlstm(mq, mk, mv, small, B, T, T, Lp)
    o_nsa = _nsa_prompt(q, qr, small, rows, win, wbd, pe, kg0, B, T)
    assert T % MOE_TM == 0
    ms_pad = -(-(DB * tp) // MOE_TM) * MOE_TM
    nt_p = mp // MOE_TM
    nt_all = nt_p + ms_pad // MOE_TM + 1
    x1_p, xsl, info, cnt = _mixout(xp, hm, o_nsa, mod_p, gmix, gffn, mixout_w, T // MOE_TM, nt_all)

    ms = DB * tp
    xs_pad = jnp.concatenate([x_sample, jnp.zeros((DB, tp - TS, D), F32)], axis=1).reshape(ms, D)
    pos_s = past_len + jnp.tile(jnp.arange(tp, dtype=jnp.int32), DB)
    tabs_s = _rope_tables(pos_s)
    mq_s, mk_s, mv_s, q_s, qr_s, rows_s, win_s, small_s = _inproj(xs_pad, mod_s, gmix, tabs_s, inproj_w, ms, 1, 1)
    hm_s, C_s, n_s, m_s = _mlstm(mq_s, mk_s, mv_s, small_s, DB, tp, TS, tp,
                                 state=(state_mlstm_C[l], state_mlstm_n[l], state_mlstm_m[l]),
                                 nseq=_pick_tile(DB, 4))
    cache2 = jnp.transpose(cache_nsa_kv[l], (0, 2, 3, 4, 1)).reshape(cache_nsa_kv.shape[1], 4 * LANES, PAGE_SIZE)
    winbuf = jnp.transpose(state_win_kv[l], (0, 2, 3, 4, 1)).reshape(DB, 2 * LANES, wbuf)
    o_nsa_s, win_out_s = _nsa_sample(page_table, cache2, q_s, qr_s, small_s, rows_s, win_s, winbuf,
                                     wbd, pe, kg0, TS)
    assert ms_pad == MOE_TM
    rpad = lambda a: jnp.concatenate([a, jnp.zeros((ms_pad - ms, a.shape[1]), a.dtype)], axis=0) if ms_pad > ms else a
    mod_sp = rpad(mod_s[0])[None]
    x1_s, xsl, info, cnt = _mixout(rpad(xs_pad), rpad(hm_s), rpad(o_nsa_s), mod_sp, gmix, gffn, mixout_w,
                                   1, nt_all, tile0=nt_p, shared=(xsl, info, cnt),
                                   t_mod=tp, t_valid=TS, m_valid=ms)

    ysl = _moe_experts(_moe_plan(cnt[:, :, 0]), xsl, w_gu[l], b_gu[l], w_dn[l], b_dn[l])
    y_p = _combine(ysl, info, x1_p, mod_p, T // MOE_TM).reshape(B, T, D)
    y_s_all = _combine(ysl, info, x1_s, mod_sp, 1, tile0=nt_p)
    valid = lambda a: a.reshape(DB, tp, -1)[:, :TS].reshape(DB * TS, -1)
    y_s = valid(y_s_all[:ms]).reshape(DB, TS, D)

    kv_p = jnp.transpose(rows_t.reshape(B, 4, A_KV, A_DH, T), (0, 4, 1, 2, 3))[None]
    kv_s = valid(rows_s).reshape(1, DB, TS, 4, A_KV, A_DH)
    wp = min(WINDOW, T)
    win_p = jnp.transpose(win_t[:, :, T - wp:].reshape(B, 2, A_KV, A_DH, wp), (0, 4, 1, 2, 3))[None]
    win_s_out = jnp.transpose(win_out_s.reshape(DB, 2, A_KV, A_DH, wbuf), (0, 4, 1, 2, 3))[None]
    return (y_p, y_s, kv_p, kv_s, win_p, win_s_out,
            C_p[None], n_p[None], m_p[None], C_s[None], n_s[None], m_s[None])
```
